```python
import jax, jax.numpy as jnp
from jax import lax
import numpy as np

D_MODEL = 1024
BATCH = 8
SEQ = 8192
DEPTH = 4

N_MIXERS = 4
N_MEM = 256
MIX_WIDTH = D_MODEL
XATTN_HEADS = 4
XATTN_HEAD_DIM = 64
XATTN_WIDTH = XATTN_HEADS * XATTN_HEAD_DIM
TOK_WIDTH = MIX_WIDTH - XATTN_WIDTH

GMLP_CHUNK = 128
GMLP_HEAD_DIM = 128
GMLP_HEADS = TOK_WIDTH // GMLP_HEAD_DIM

HGRN_HEAD_DIM = 128
HGRN_HEADS = TOK_WIDTH // HGRN_HEAD_DIM
HGRN_CHUNK = 16

POOL_WINDOWS = (2, 4, 8, 16)
POOL_GROUP = TOK_WIDTH // len(POOL_WINDOWS)

LRU_HEAD_DIM = 128
LRU_HEADS = TOK_WIDTH // LRU_HEAD_DIM
CONV_WIDTH = 4
LRU_C = 8.0

ALPHA = (2 * DEPTH) ** 0.25
BETA = (8 * DEPTH) ** -0.25
LN_EPS = 1e-5
RMS_EPS = 1e-6

IN_WIDTH_A = 2 * TOK_WIDTH + XATTN_WIDTH + MIX_WIDTH
IN_WIDTH_B = 3 * TOK_WIDTH + XATTN_WIDTH + MIX_WIDTH
IN_WIDTH_C = TOK_WIDTH + XATTN_WIDTH + MIX_WIDTH
IN_WIDTH_D = TOK_WIDTH + XATTN_WIDTH + MIX_WIDTH

kernel_name = 'hybrid_gmlp_hgrn2_pool_rglru_trunk'


def n_of_kind(kind):
    return len(range(kind, DEPTH, N_MIXERS))


def split_cols(t, widths):
    idx = [int(v) for v in np.cumsum(widths)[:-1]]
    return jnp.split(t, idx, axis=-1)


def layer_norm(x, g, b):
    xf = x.astype(jnp.float32)
    mu = jnp.mean(xf, -1, keepdims=True)
    var = jnp.mean(jnp.square(xf - mu), -1, keepdims=True)
    y = (xf - mu) * lax.rsqrt(var + LN_EPS) * g.astype(jnp.float32) + b.astype(jnp.float32)
    return y.astype(x.dtype)


def memory_cross_attention(q, mem_k, mem_v):
    B, S, _ = q.shape
    qh = q.reshape(B, S, XATTN_HEADS, XATTN_HEAD_DIM)
    s = jnp.einsum('bshd,bmhd->bhsm', qh, mem_k).astype(jnp.float32) * (XATTN_HEAD_DIM ** -0.5)
    p = jax.nn.softmax(s, axis=-1).astype(mem_v.dtype)
    o = jnp.einsum('bhsm,bmhd->bshd', p, mem_v)
    return o.reshape(B, S, XATTN_WIDTH).astype(q.dtype)


def chunked_spatial_gating(u, v, w_s, b_s):
    dt = u.dtype
    B, S, _ = v.shape
    n = S // GMLP_CHUNK
    u = jax.nn.gelu(u.astype(jnp.float32))
    v = jax.nn.gelu(v.astype(jnp.float32))
    vg = v.reshape(B, n, GMLP_CHUNK, GMLP_HEADS, GMLP_HEAD_DIM)
    mu = jnp.mean(vg, -1, keepdims=True)
    var = jnp.mean(jnp.square(vg - mu), -1, keepdims=True)
    vn = (vg - mu) * lax.rsqrt(var + LN_EPS)
    causal = jnp.tril(jnp.ones((GMLP_CHUNK, GMLP_CHUNK), dtype=bool))
    w = jnp.where(causal[None], w_s.astype(jnp.float32), 0.0)
    mixed = jnp.einsum('gts,bnsgc->bntgc', w, vn) + b_s.astype(jnp.float32).T[:, :, None]
    return (u * mixed.reshape(B, S, TOK_WIDTH)).astype(dt)


def hgrn2(q, f_logit, i, lb, norm_g):
    dt = q.dtype
    B, S, _ = q.shape
    H, K, C = HGRN_HEADS, HGRN_HEAD_DIM, HGRN_CHUNK
    n = S // C
    f = lb + (1.0 - lb) * jax.nn.sigmoid(f_logit.astype(jnp.float32))
    log_f = jnp.log(f)
    k = 1.0 - f
    qf = jax.nn.silu(q.astype(jnp.float32))
    vf = i.astype(jnp.float32)

    def chunks(t):
        return t.reshape(B, n, C, H, K).transpose(1, 0, 2, 3, 4)

    causal = jnp.tril(jnp.ones((C, C), dtype=bool))

    def step(state, xs):
        qc, kc, vc, lfc = xs
        g = jnp.cumsum(lfc, axis=1)
        g_last = g[:, -1]
        q_dec = qc * jnp.exp(g)
        k_inv = kc * jnp.exp(-g)
        scores = jnp.einsum('bthk,bshk->bhts', q_dec, k_inv)
        scores = jnp.where(causal, scores, 0.0)
        o = (jnp.einsum('bhts,bshv->bthv', scores, vc)
             + jnp.einsum('bthk,bhkv->bthv', q_dec, state))
        k_end = kc * jnp.exp(g_last[:, None] - g)
        state = jnp.exp(g_last)[..., None] * state + jnp.einsum('bshk,bshv->bhkv', k_end, vc)
        return state, o

    s0 = jnp.zeros((B, H, K, K), jnp.float32)
    _, o = lax.scan(step, s0, (chunks(qf), chunks(k), chunks(vf), chunks(log_f)))
    o = o.transpose(1, 0, 2, 3, 4).reshape(B, S, H, K)
    o = o * lax.rsqrt(jnp.mean(jnp.square(o), -1, keepdims=True) + RMS_EPS)
    return (o.reshape(B, S, TOK_WIDTH) * norm_g.astype(jnp.float32)).astype(dt)


def multiscale_pool(p, w_pool, scale):
    dt = p.dtype
    B, S, _ = p.shape
    grp = p.astype(jnp.float32).reshape(B, S, len(POOL_WINDOWS), POOL_GROUP)
    cs = jnp.cumsum(grp, axis=1)
    pos = jnp.arange(S)
    pooled = []
    for g, w in enumerate(POOL_WINDOWS):
        csg = cs[:, :, g]
        prev = jnp.pad(csg, ((0, 0), (w, 0), (0, 0)))[:, :S]
        cnt = jnp.minimum(pos + 1, w).astype(jnp.float32)[None, :, None]
        pooled.append((csg - prev) / cnt)
    pooled = jnp.stack(pooled, axis=2)
    y = jnp.einsum('bsgc,gcd->bsgd', pooled - grp, w_pool.astype(jnp.float32))
    return (y.reshape(B, S, TOK_WIDTH) * scale.astype(jnp.float32)).astype(dt)


def rg_lru_branch(xb, conv_w, conv_b, w_gx, b_gx, w_ga, b_ga, a_param):
    dt = xb.dtype
    B, S, _ = xb.shape
    xc = lax.conv_general_dilated(
        xb, conv_w.astype(dt)[:, None, :], window_strides=(1,),
        padding=[(CONV_WIDTH - 1, 0)], dimension_numbers=('NWC', 'WIO', 'NWC'),
        feature_group_count=TOK_WIDTH) + conv_b.astype(dt)
    xh = xc.astype(jnp.float32).reshape(B, S, LRU_HEADS, LRU_HEAD_DIM)
    gate_x = jax.nn.sigmoid(jnp.einsum('bshi,hij->bshj', xh, w_gx.astype(jnp.float32)) + b_gx.astype(jnp.float32))
    gate_a = jax.nn.sigmoid(jnp.einsum('bshi,hij->bshj', xh, w_ga.astype(jnp.float32)) + b_ga.astype(jnp.float32))
    log_a = -LRU_C * gate_a * jax.nn.softplus(-a_param.astype(jnp.float32).reshape(LRU_HEADS, LRU_HEAD_DIM))
    a = jnp.exp(log_a)
    mult = jnp.sqrt(-jnp.expm1(2.0 * log_a))
    first = (jnp.arange(S) == 0)[None, :, None, None]
    mult = jnp.where(first, 1.0, mult)
    b_term = mult * gate_x * xh

    def combine(l, r):
        a1, b1 = l
        a2, b2 = r
        return a1 * a2, a2 * b1 + b2

    _, h = lax.associative_scan(combine, (a, b_term), axis=1)
    return h.reshape(B, S, TOK_WIDTH).astype(dt)


def _fwd_setup_inputs(seed: int = 0) -> dict:
    key = jax.random.key(seed)
    ks = jax.random.split(key, 24)
    f32 = jnp.float32

    def nrm(k, shape, scale):
        return jax.random.normal(k, shape, f32) * scale

    nA, nB, nC, nD = n_of_kind(0), n_of_kind(1), n_of_kind(2), n_of_kind(3)
    u = jax.random.uniform(ks[22], (nD, TOK_WIDTH), f32, minval=0.9, maxval=0.999)
    s = u ** (1.0 / LRU_C)
    return {
        'x': nrm(ks[0], (BATCH, SEQ, D_MODEL), 1.0),
        'mem': nrm(ks[1], (BATCH, N_MEM, D_MODEL), 1.0),
        'mem_kv_w': nrm(ks[2], (D_MODEL, 2 * XATTN_WIDTH), D_MODEL ** -0.5),
        'ln_g': 1.0 + nrm(ks[3], (DEPTH, D_MODEL), 0.02),
        'ln_b': nrm(ks[4], (DEPTH, D_MODEL), 0.02),
        'w_out': nrm(ks[5], (DEPTH, MIX_WIDTH, D_MODEL), BETA * MIX_WIDTH ** -0.5),
        'hgrn_lb_logits': 1.0 + nrm(ks[6], (DEPTH, TOK_WIDTH), 0.1),
        'a_w_in': nrm(ks[7], (nA, D_MODEL, IN_WIDTH_A), D_MODEL ** -0.5),
        'a_w_s': nrm(ks[8], (nA, GMLP_HEADS, GMLP_CHUNK, GMLP_CHUNK), GMLP_CHUNK ** -0.5),
        'a_b_s': 1.0 + nrm(ks[9], (nA, GMLP_HEADS, GMLP_CHUNK), 0.02),
        'b_w_in': nrm(ks[10], (nB, D_MODEL, IN_WIDTH_B), D_MODEL ** -0.5),
        'b_norm_g': 1.0 + nrm(ks[11], (nB, TOK_WIDTH), 0.02),
        'c_w_in': nrm(ks[12], (nC, D_MODEL, IN_WIDTH_C), D_MODEL ** -0.5),
        'c_w_pool': nrm(ks[13], (nC, len(POOL_WINDOWS), POOL_GROUP, POOL_GROUP), POOL_GROUP ** -0.5),
        'c_scale': 1.0 + nrm(ks[14], (nC, TOK_WIDTH), 0.02),
        'd_w_in': nrm(ks[15], (nD, D_MODEL, IN_WIDTH_D), D_MODEL ** -0.5),
        'd_conv_w': nrm(ks[16], (nD, CONV_WIDTH, TOK_WIDTH), CONV_WIDTH ** -0.5),
        'd_conv_b': nrm(ks[17], (nD, TOK_WIDTH), 0.02),
        'd_w_gx': nrm(ks[18], (nD, LRU_HEADS, LRU_HEAD_DIM, LRU_HEAD_DIM), LRU_HEAD_DIM ** -0.5),
        'd_b_gx': nrm(ks[19], (nD, LRU_HEADS, LRU_HEAD_DIM), 0.02),
        'd_w_ga': nrm(ks[20], (nD, LRU_HEADS, LRU_HEAD_DIM, LRU_HEAD_DIM), LRU_HEAD_DIM ** -0.5),
        'd_b_ga': nrm(ks[21], (nD, LRU_HEADS, LRU_HEAD_DIM), 0.02),
        'd_a_param': jnp.log(s) - jnp.log1p(-s),
    }


def _fwd_reference(x, mem, mem_kv_w, ln_g, ln_b, w_out, hgrn_lb_logits,
              a_w_in, a_w_s, a_b_s,
              b_w_in, b_norm_g,
              c_w_in, c_w_pool, c_scale,
              d_w_in, d_conv_w, d_conv_b, d_w_gx, d_b_gx, d_w_ga, d_b_ga, d_a_param):
    B = x.shape[0]
    M = mem.shape[1]
    kv = jnp.einsum('bmd,de->bme', mem, mem_kv_w)
    mem_k = kv[..., :XATTN_WIDTH].reshape(B, M, XATTN_HEADS, XATTN_HEAD_DIM)
    mem_v = kv[..., XATTN_WIDTH:].reshape(B, M, XATTN_HEADS, XATTN_HEAD_DIM)
    lb_p = jax.nn.softmax(hgrn_lb_logits.astype(jnp.float32), axis=0)
    lower_bounds = jnp.cumsum(lb_p, axis=0) - lb_p[0]

    for i in range(DEPTH):
        kind, j = i % N_MIXERS, i // N_MIXERS
        if kind == 0:
            proj = jnp.einsum('bsd,de->bse', x, a_w_in[j])
            u, v, q_x, gate = split_cols(proj, [TOK_WIDTH, TOK_WIDTH, XATTN_WIDTH, MIX_WIDTH])
            tok = chunked_spatial_gating(u, v, a_w_s[j], a_b_s[j])
        elif kind == 1:
            proj = jnp.einsum('bsd,de->bse', x, b_w_in[j])
            q, f_logit, inp, q_x, gate = split_cols(
                proj, [TOK_WIDTH, TOK_WIDTH, TOK_WIDTH, XATTN_WIDTH, MIX_WIDTH])
            tok = hgrn2(q, f_logit, inp, lower_bounds[i], b_norm_g[j])
        elif kind == 2:
            proj = jnp.einsum('bsd,de->bse', x, c_w_in[j])
            p, q_x, gate = split_cols(proj, [TOK_WIDTH, XATTN_WIDTH, MIX_WIDTH])
            tok = multiscale_pool(p, c_w_pool[j], c_scale[j])
        else:
            proj = jnp.einsum('bsd,de->bse', x, d_w_in[j])
            xb, q_x, gate = split_cols(proj, [TOK_WIDTH, XATTN_WIDTH, MIX_WIDTH])
            tok = rg_lru_branch(xb, d_conv_w[j], d_conv_b[j], d_w_gx[j], d_b_gx[j],
                                d_w_ga[j], d_b_ga[j], d_a_param[j])
        xo = memory_cross_attention(q_x, mem_k, mem_v)
        mixed = jnp.concatenate([tok.astype(x.dtype), xo.astype(x.dtype)], axis=-1) * jax.nn.silu(gate)
        y = jnp.einsum('bse,ed->bsd', mixed, w_out[i]).astype(x.dtype)
        x = layer_norm(ALPHA * x + y, ln_g[i], ln_b[i])
    return x


import jax as _jax
import jax.numpy as _jnp

TWIN_FORMAT = 'train_step'
FWD_PARAMS = ['x', 'mem', 'mem_kv_w', 'ln_g', 'ln_b', 'w_out', 'hgrn_lb_logits', 'a_w_in', 'a_w_s', 'a_b_s', 'b_w_in', 'b_norm_g', 'c_w_in', 'c_w_pool', 'c_scale', 'd_w_in', 'd_conv_w', 'd_conv_b', 'd_w_gx', 'd_b_gx', 'd_w_ga', 'd_b_ga', 'd_a_param']
TWIN_WEIGHTS = ['mem_kv_w', 'ln_g', 'ln_b', 'w_out', 'hgrn_lb_logits', 'a_w_in', 'a_w_s', 'a_b_s', 'b_w_in', 'b_norm_g', 'c_w_in', 'c_w_pool', 'c_scale', 'd_w_in', 'd_conv_w', 'd_conv_b', 'd_w_gx', 'd_b_gx', 'd_w_ga', 'd_b_ga', 'd_a_param']
TWIN_DIFF_INPUT = 'x'
TWIN_INPUTS = ['x', 'mem', 'mem_kv_w', 'ln_g', 'ln_b', 'w_out', 'hgrn_lb_logits', 'a_w_in', 'a_w_s', 'a_b_s', 'b_w_in', 'b_norm_g', 'c_w_in', 'c_w_pool', 'c_scale', 'd_w_in', 'd_conv_w', 'd_conv_b', 'd_w_gx', 'd_b_gx', 'd_w_ga', 'd_b_ga', 'd_a_param', 'loss_target', 'm_mem_kv_w', 'm_ln_g', 'm_ln_b', 'm_w_out', 'm_hgrn_lb_logits', 'm_a_w_in', 'm_a_w_s', 'm_a_b_s', 'm_b_w_in', 'm_b_norm_g', 'm_c_w_in', 'm_c_w_pool', 'm_c_scale', 'm_d_w_in', 'm_d_conv_w', 'm_d_conv_b', 'm_d_w_gx', 'm_d_b_gx', 'm_d_w_ga', 'm_d_b_ga', 'm_d_a_param', 'v_mem_kv_w', 'v_ln_g', 'v_ln_b', 'v_w_out', 'v_hgrn_lb_logits', 'v_a_w_in', 'v_a_w_s', 'v_a_b_s', 'v_b_w_in', 'v_b_norm_g', 'v_c_w_in', 'v_c_w_pool', 'v_c_scale', 'v_d_w_in', 'v_d_conv_w', 'v_d_conv_b', 'v_d_w_gx', 'v_d_b_gx', 'v_d_w_ga', 'v_d_b_ga', 'v_d_a_param']
TWIN_OUTPUTS = ['loss', 'grad_x', 'grad_mem_kv_w', 'grad_ln_g', 'grad_ln_b', 'grad_w_out', 'grad_hgrn_lb_logits', 'grad_a_w_in', 'grad_a_w_s', 'grad_a_b_s', 'grad_b_w_in', 'grad_b_norm_g', 'grad_c_w_in', 'grad_c_w_pool', 'grad_c_scale', 'grad_d_w_in', 'grad_d_conv_w', 'grad_d_conv_b', 'grad_d_w_gx', 'grad_d_b_gx', 'grad_d_w_ga', 'grad_d_b_ga', 'grad_d_a_param', 'delta_mem_kv_w', 'delta_ln_g', 'delta_ln_b', 'delta_w_out', 'delta_hgrn_lb_logits', 'delta_a_w_in', 'delta_a_w_s', 'delta_a_b_s', 'delta_b_w_in', 'delta_b_norm_g', 'delta_c_w_in', 'delta_c_w_pool', 'delta_c_scale', 'delta_d_w_in', 'delta_d_conv_w', 'delta_d_conv_b', 'delta_d_w_gx', 'delta_d_b_gx', 'delta_d_w_ga', 'delta_d_b_ga', 'delta_d_a_param', 'new_m_mem_kv_w', 'new_m_ln_g', 'new_m_ln_b', 'new_m_w_out', 'new_m_hgrn_lb_logits', 'new_m_a_w_in', 'new_m_a_w_s', 'new_m_a_b_s', 'new_m_b_w_in', 'new_m_b_norm_g', 'new_m_c_w_in', 'new_m_c_w_pool', 'new_m_c_scale', 'new_m_d_w_in', 'new_m_d_conv_w', 'new_m_d_conv_b', 'new_m_d_w_gx', 'new_m_d_b_gx', 'new_m_d_w_ga', 'new_m_d_b_ga', 'new_m_d_a_param', 'new_v_mem_kv_w', 'new_v_ln_g', 'new_v_ln_b', 'new_v_w_out', 'new_v_hgrn_lb_logits', 'new_v_a_w_in', 'new_v_a_w_s', 'new_v_a_b_s', 'new_v_b_w_in', 'new_v_b_norm_g', 'new_v_c_w_in', 'new_v_c_w_pool', 'new_v_c_scale', 'new_v_d_w_in', 'new_v_d_conv_w', 'new_v_d_conv_b', 'new_v_d_w_gx', 'new_v_d_b_gx', 'new_v_d_w_ga', 'new_v_d_b_ga', 'new_v_d_a_param']
TWIN_LEAF_KINDS = {'loss': 'loss', 'grad_x': 'grad_x', 'grad_mem_kv_w': 'grad_w', 'grad_ln_g': 'grad_w', 'grad_ln_b': 'grad_w', 'grad_w_out': 'grad_w', 'grad_hgrn_lb_logits': 'grad_w', 'grad_a_w_in': 'grad_w', 'grad_a_w_s': 'grad_w', 'grad_a_b_s': 'grad_w', 'grad_b_w_in': 'grad_w', 'grad_b_norm_g': 'grad_w', 'grad_c_w_in': 'grad_w', 'grad_c_w_pool': 'grad_w', 'grad_c_scale': 'grad_w', 'grad_d_w_in': 'grad_w', 'grad_d_conv_w': 'grad_w', 'grad_d_conv_b': 'grad_w', 'grad_d_w_gx': 'grad_w', 'grad_d_b_gx': 'grad_w', 'grad_d_w_ga': 'grad_w', 'grad_d_b_ga': 'grad_w', 'grad_d_a_param': 'grad_w', 'delta_mem_kv_w': 'delta_w', 'delta_ln_g': 'delta_w', 'delta_ln_b': 'delta_w', 'delta_w_out': 'delta_w', 'delta_hgrn_lb_logits': 'delta_w', 'delta_a_w_in': 'delta_w', 'delta_a_w_s': 'delta_w', 'delta_a_b_s': 'delta_w', 'delta_b_w_in': 'delta_w', 'delta_b_norm_g': 'delta_w', 'delta_c_w_in': 'delta_w', 'delta_c_w_pool': 'delta_w', 'delta_c_scale': 'delta_w', 'delta_d_w_in': 'delta_w', 'delta_d_conv_w': 'delta_w', 'delta_d_conv_b': 'delta_w', 'delta_d_w_gx': 'delta_w', 'delta_d_b_gx': 'delta_w', 'delta_d_w_ga': 'delta_w', 'delta_d_b_ga': 'delta_w', 'delta_d_a_param': 'delta_w', 'new_m_mem_kv_w': 'new_m', 'new_m_ln_g': 'new_m', 'new_m_ln_b': 'new_m', 'new_m_w_out': 'new_m', 'new_m_hgrn_lb_logits': 'new_m', 'new_m_a_w_in': 'new_m', 'new_m_a_w_s': 'new_m', 'new_m_a_b_s': 'new_m', 'new_m_b_w_in': 'new_m', 'new_m_b_norm_g': 'new_m', 'new_m_c_w_in': 'new_m', 'new_m_c_w_pool': 'new_m', 'new_m_c_scale': 'new_m', 'new_m_d_w_in': 'new_m', 'new_m_d_conv_w': 'new_m', 'new_m_d_conv_b': 'new_m', 'new_m_d_w_gx': 'new_m', 'new_m_d_b_gx': 'new_m', 'new_m_d_w_ga': 'new_m', 'new_m_d_b_ga': 'new_m', 'new_m_d_a_param': 'new_m', 'new_v_mem_kv_w': 'new_v', 'new_v_ln_g': 'new_v', 'new_v_ln_b': 'new_v', 'new_v_w_out': 'new_v', 'new_v_hgrn_lb_logits': 'new_v', 'new_v_a_w_in': 'new_v', 'new_v_a_w_s': 'new_v', 'new_v_a_b_s': 'new_v', 'new_v_b_w_in': 'new_v', 'new_v_b_norm_g': 'new_v', 'new_v_c_w_in': 'new_v', 'new_v_c_w_pool': 'new_v', 'new_v_c_scale': 'new_v', 'new_v_d_w_in': 'new_v', 'new_v_d_conv_w': 'new_v', 'new_v_d_conv_b': 'new_v', 'new_v_d_w_gx': 'new_v', 'new_v_d_b_gx': 'new_v', 'new_v_d_w_ga': 'new_v', 'new_v_d_b_ga': 'new_v', 'new_v_d_a_param': 'new_v'}


def _forward(args):
    return _fwd_reference(*[args[k] for k in FWD_PARAMS])


def _output_shape():
    def fwd():
        inp = _fwd_setup_inputs(0)
        return _fwd_reference(*[inp[k] for k in FWD_PARAMS])
    out = _jax.eval_shape(fwd)
    return out.shape, out.dtype

N_MICROBATCH = 1
ADAM_LR = 0.001
ADAM_B1 = 0.9
ADAM_B2 = 0.999
ADAM_EPS = 1e-08
ADAM_WD = 0.01
ADAM_STEP = 10
PER_EXAMPLE_BATCH_AXIS = {'x': 0, 'mem': 0, 'loss_target': 0}
SHARED_INPUTS = []
_WEIGHT_DTYPES = {'mem_kv_w': _jnp.float32, 'ln_g': _jnp.float32, 'ln_b': _jnp.float32, 'w_out': _jnp.float32, 'hgrn_lb_logits': _jnp.float32, 'a_w_in': _jnp.float32, 'a_w_s': _jnp.float32, 'a_b_s': _jnp.float32, 'b_w_in': _jnp.float32, 'b_norm_g': _jnp.float32, 'c_w_in': _jnp.float32, 'c_w_pool': _jnp.float32, 'c_scale': _jnp.float32, 'd_w_in': _jnp.float32, 'd_conv_w': _jnp.float32, 'd_conv_b': _jnp.float32, 'd_w_gx': _jnp.float32, 'd_b_gx': _jnp.float32, 'd_w_ga': _jnp.float32, 'd_b_ga': _jnp.float32, 'd_a_param': _jnp.float32}
MOMENT_SCALE = {'mem_kv_w': 8.112955e-03, 'ln_g': 3.211333e+01, 'ln_b': 1.501147e+00, 'w_out': 6.874053e-02, 'hgrn_lb_logits': 1.147915e-03, 'a_w_in': 2.552602e-02, 'a_w_s': 1.774291e-02, 'a_b_s': 2.460304e-02, 'b_w_in': 2.525895e-02, 'b_norm_g': 3.891039e-02, 'c_w_in': 2.881890e-02, 'c_w_pool': 3.270126e-02, 'c_scale': 3.198449e-02, 'd_w_in': 2.436643e-02, 'd_conv_w': 3.182155e-02, 'd_conv_b': 3.759565e-01, 'd_w_gx': 1.679635e-02, 'd_b_gx': 1.129587e-02, 'd_w_ga': 9.191687e-03, 'd_b_ga': 8.388946e-03, 'd_a_param': 1.800789e-02}


def _to_microbatches(a, axis):
    t = _jnp.moveaxis(a, axis, 0)
    t = t.reshape((N_MICROBATCH, t.shape[0] // N_MICROBATCH) + t.shape[1:])
    return _jnp.moveaxis(t, 1, axis + 1)


def setup_inputs(seed: int = 0) -> dict:
    inp = _fwd_setup_inputs(seed)
    key = _jax.random.fold_in(_jax.random.key(seed), 7919)
    shape, _ = _output_shape()
    out = dict(inp)
    out["loss_target"] = _jax.random.normal(_jax.random.fold_in(key, 0), shape, _jnp.float32)
    for i, name in enumerate(TWIN_WEIGHTS):
        w = inp[name].astype(_jnp.float32)
        if MOMENT_SCALE is None:
            s = _jnp.sqrt(_jnp.mean(_jnp.square(w)) + 1e-30)
        else:
            s = MOMENT_SCALE[name]
        km, kv = _jax.random.split(_jax.random.fold_in(key, i + 1))
        out[name] = w
        out["m_" + name] = s * _jax.random.normal(km, w.shape, _jnp.float32)
        out["v_" + name] = (s * s) * _jax.random.uniform(kv, w.shape, _jnp.float32, 0.5, 1.5)
    if N_MICROBATCH > 1:
        for name, axis in PER_EXAMPLE_BATCH_AXIS.items():
            out[name] = _to_microbatches(out[name], axis)
    return {'x': out['x'], 'mem': out['mem'], 'mem_kv_w': out['mem_kv_w'], 'ln_g': out['ln_g'], 'ln_b': out['ln_b'], 'w_out': out['w_out'], 'hgrn_lb_logits': out['hgrn_lb_logits'], 'a_w_in': out['a_w_in'], 'a_w_s': out['a_w_s'], 'a_b_s': out['a_b_s'], 'b_w_in': out['b_w_in'], 'b_norm_g': out['b_norm_g'], 'c_w_in': out['c_w_in'], 'c_w_pool': out['c_w_pool'], 'c_scale': out['c_scale'], 'd_w_in': out['d_w_in'], 'd_conv_w': out['d_conv_w'], 'd_conv_b': out['d_conv_b'], 'd_w_gx': out['d_w_gx'], 'd_b_gx': out['d_b_gx'], 'd_w_ga': out['d_w_ga'], 'd_b_ga': out['d_b_ga'], 'd_a_param': out['d_a_param'], 'loss_target': out['loss_target'], 'm_mem_kv_w': out['m_mem_kv_w'], 'm_ln_g': out['m_ln_g'], 'm_ln_b': out['m_ln_b'], 'm_w_out': out['m_w_out'], 'm_hgrn_lb_logits': out['m_hgrn_lb_logits'], 'm_a_w_in': out['m_a_w_in'], 'm_a_w_s': out['m_a_w_s'], 'm_a_b_s': out['m_a_b_s'], 'm_b_w_in': out['m_b_w_in'], 'm_b_norm_g': out['m_b_norm_g'], 'm_c_w_in': out['m_c_w_in'], 'm_c_w_pool': out['m_c_w_pool'], 'm_c_scale': out['m_c_scale'], 'm_d_w_in': out['m_d_w_in'], 'm_d_conv_w': out['m_d_conv_w'], 'm_d_conv_b': out['m_d_conv_b'], 'm_d_w_gx': out['m_d_w_gx'], 'm_d_b_gx': out['m_d_b_gx'], 'm_d_w_ga': out['m_d_w_ga'], 'm_d_b_ga': out['m_d_b_ga'], 'm_d_a_param': out['m_d_a_param'], 'v_mem_kv_w': out['v_mem_kv_w'], 'v_ln_g': out['v_ln_g'], 'v_ln_b': out['v_ln_b'], 'v_w_out': out['v_w_out'], 'v_hgrn_lb_logits': out['v_hgrn_lb_logits'], 'v_a_w_in': out['v_a_w_in'], 'v_a_w_s': out['v_a_w_s'], 'v_a_b_s': out['v_a_b_s'], 'v_b_w_in': out['v_b_w_in'], 'v_b_norm_g': out['v_b_norm_g'], 'v_c_w_in': out['v_c_w_in'], 'v_c_w_pool': out['v_c_w_pool'], 'v_c_scale': out['v_c_scale'], 'v_d_w_in': out['v_d_w_in'], 'v_d_conv_w': out['v_d_conv_w'], 'v_d_conv_b': out['v_d_conv_b'], 'v_d_w_gx': out['v_d_w_gx'], 'v_d_b_gx': out['v_d_b_gx'], 'v_d_w_ga': out['v_d_w_ga'], 'v_d_b_ga': out['v_d_b_ga'], 'v_d_a_param': out['v_d_a_param']}


def _loss(weights, diff, rest, loss_target):
    with _jax.named_scope("forward"):
        args = {**rest, TWIN_DIFF_INPUT: diff, **{k: w.astype(_WEIGHT_DTYPES[k]) for k, w in weights.items()}}
        y = _forward(args)
    with _jax.named_scope("loss_head"):
        err = _jnp.square(y.astype(_jnp.float32) - loss_target)
        return 0.5 * _jnp.sum(_jnp.mean(err, axis=-1)) if err.ndim else 0.5 * err


def _adamw(w, g, m, v):
    m = ADAM_B1 * m + (1.0 - ADAM_B1) * g
    v = ADAM_B2 * v + (1.0 - ADAM_B2) * _jnp.square(g)
    m_hat = m / (1.0 - ADAM_B1 ** ADAM_STEP)
    v_hat = v / (1.0 - ADAM_B2 ** ADAM_STEP)
    delta = -ADAM_LR * (m_hat / (_jnp.sqrt(v_hat) + ADAM_EPS) + ADAM_WD * w)
    return delta, m, v


def reference(x, mem, mem_kv_w, ln_g, ln_b, w_out, hgrn_lb_logits, a_w_in, a_w_s, a_b_s, b_w_in, b_norm_g, c_w_in, c_w_pool, c_scale, d_w_in, d_conv_w, d_conv_b, d_w_gx, d_b_gx, d_w_ga, d_b_ga, d_a_param, loss_target, m_mem_kv_w, m_ln_g, m_ln_b, m_w_out, m_hgrn_lb_logits, m_a_w_in, m_a_w_s, m_a_b_s, m_b_w_in, m_b_norm_g, m_c_w_in, m_c_w_pool, m_c_scale, m_d_w_in, m_d_conv_w, m_d_conv_b, m_d_w_gx, m_d_b_gx, m_d_w_ga, m_d_b_ga, m_d_a_param, v_mem_kv_w, v_ln_g, v_ln_b, v_w_out, v_hgrn_lb_logits, v_a_w_in, v_a_w_s, v_a_b_s, v_b_w_in, v_b_norm_g, v_c_w_in, v_c_w_pool, v_c_scale, v_d_w_in, v_d_conv_w, v_d_conv_b, v_d_w_gx, v_d_b_gx, v_d_w_ga, v_d_b_ga, v_d_a_param):
    given = dict(x=x, mem=mem, mem_kv_w=mem_kv_w, ln_g=ln_g, ln_b=ln_b, w_out=w_out, hgrn_lb_logits=hgrn_lb_logits, a_w_in=a_w_in, a_w_s=a_w_s, a_b_s=a_b_s, b_w_in=b_w_in, b_norm_g=b_norm_g, c_w_in=c_w_in, c_w_pool=c_w_pool, c_scale=c_scale, d_w_in=d_w_in, d_conv_w=d_conv_w, d_conv_b=d_conv_b, d_w_gx=d_w_gx, d_b_gx=d_b_gx, d_w_ga=d_w_ga, d_b_ga=d_b_ga, d_a_param=d_a_param, loss_target=loss_target, m_mem_kv_w=m_mem_kv_w, m_ln_g=m_ln_g, m_ln_b=m_ln_b, m_w_out=m_w_out, m_hgrn_lb_logits=m_hgrn_lb_logits, m_a_w_in=m_a_w_in, m_a_w_s=m_a_w_s, m_a_b_s=m_a_b_s, m_b_w_in=m_b_w_in, m_b_norm_g=m_b_norm_g, m_c_w_in=m_c_w_in, m_c_w_pool=m_c_w_pool, m_c_scale=m_c_scale, m_d_w_in=m_d_w_in, m_d_conv_w=m_d_conv_w, m_d_conv_b=m_d_conv_b, m_d_w_gx=m_d_w_gx, m_d_b_gx=m_d_b_gx, m_d_w_ga=m_d_w_ga, m_d_b_ga=m_d_b_ga, m_d_a_param=m_d_a_param, v_mem_kv_w=v_mem_kv_w, v_ln_g=v_ln_g, v_ln_b=v_ln_b, v_w_out=v_w_out, v_hgrn_lb_logits=v_hgrn_lb_logits, v_a_w_in=v_a_w_in, v_a_w_s=v_a_w_s, v_a_b_s=v_a_b_s, v_b_w_in=v_b_w_in, v_b_norm_g=v_b_norm_g, v_c_w_in=v_c_w_in, v_c_w_pool=v_c_w_pool, v_c_scale=v_c_scale, v_d_w_in=v_d_w_in, v_d_conv_w=v_d_conv_w, v_d_conv_b=v_d_conv_b, v_d_w_gx=v_d_w_gx, v_d_b_gx=v_d_b_gx, v_d_w_ga=v_d_w_ga, v_d_b_ga=v_d_b_ga, v_d_a_param=v_d_a_param)
    weights = {n: given[n] for n in TWIN_WEIGHTS}
    shared = {n: given[n] for n in SHARED_INPUTS}
    per_example = {n: given[n] for n in ['x', 'mem']}
    grad_fn = _jax.value_and_grad(_loss, argnums=(0, 1))

    def one_microbatch(ex, loss_target):
        ex = dict(ex)
        diff = ex.pop(TWIN_DIFF_INPUT)
        return grad_fn(weights, diff, {**shared, **ex}, loss_target)

    if N_MICROBATCH == 1:
        loss, (grad_w, grad_x) = one_microbatch(per_example, given["loss_target"])
    else:
        def body(carry, xs):
            loss_sum, grad_sum = carry
            l_k, (gw_k, gx_k) = one_microbatch(xs[0], xs[1])
            with _jax.named_scope("update"):
                return (loss_sum + l_k, _jax.tree.map(_jnp.add, grad_sum, gw_k)), gx_k

        init = (_jnp.zeros((), _jnp.float32), _jax.tree.map(_jnp.zeros_like, weights))
        (loss, grad_w), grad_x = _jax.lax.scan(body, init, (per_example, given["loss_target"]))
    with _jax.named_scope("update"):
        delta_w, new_m, new_v = {}, {}, {}
        for n in TWIN_WEIGHTS:
            delta_w[n], new_m[n], new_v[n] = _adamw(weights[n], grad_w[n], given["m_" + n], given["v_" + n])
    return (loss, grad_x, *[grad_w[n] for n in TWIN_WEIGHTS], *[delta_w[n] for n in TWIN_WEIGHTS],
            *[new_m[n] for n in TWIN_WEIGHTS], *[new_v[n] for n in TWIN_WEIGHTS])
```

```python
import functools
import math

import jax
import jax.numpy as jnp
from jax import lax
from jax.experimental import pallas as pl
from jax.experimental.pallas import tpu as pltpu

f32 = jnp.float32
bf16 = jnp.bfloat16
MM = bf16

D_MODEL = 1024
TOK = 768
XW = 256
XHEADS = 4
XDIM = 64
HD = 128
NH = TOK // HD
CHUNK = 16
POOL_GROUP = 192
DEPTH = 4
ALPHA = (2 * DEPTH) ** 0.25
LN_EPS = 1e-5
RMS_EPS = 1e-6
LRU_C = 8.0
ADAM_LR, ADAM_B1, ADAM_B2, ADAM_EPS, ADAM_WD, ADAM_STEP = 0.001, 0.9, 0.999, 1e-08, 0.01, 10

_TS = (256, 128, 256, 256)
TK = 512
SUB = 8
LANE = 128
VMEM_LIMIT = 58 * 1024 * 1024

_OFFS = (
    dict(u=0, v=768, qx=1536, gate=1792, W=2816),
    dict(q=0, f=768, i=1536, qx=2304, gate=2560, W=3584),
    dict(p=0, qx=768, gate=1024, W=2048),
    dict(xb=0, qx=768, gate=1024, W=2048),
)
_PRM = (
    ("wtri", "wtriT", "bcolb"),
    ("lb", "ng"),
    ("wbd", "wbdT", "scale"),
    ("cw", "cb", "wgx", "wgxT", "bgx", "wga", "wgaT", "bga", "ap"),
)
MESH = pl.DeviceIdType.MESH


def _mm(a, b):
    return jnp.dot(a.astype(MM), b.astype(MM), preferred_element_type=f32)


def _mm_nt(a, b):
    return lax.dot_general(a.astype(MM), b.astype(MM), (((1,), (1,)), ((), ())), preferred_element_type=f32)


def _mm_tn(a, b):
    return lax.dot_general(a.astype(MM), b.astype(MM), (((0,), (0,)), ((), ())), preferred_element_type=f32)


def _mm_exact(a, b):
    return jnp.dot(a, b, preferred_element_type=f32, precision=lax.Precision.HIGHEST)


def _sig(x):
    return jax.nn.sigmoid(x)


_GC = math.sqrt(2.0 / math.pi)


def _gelu(x):
    t = jnp.tanh(_GC * (x + 0.044715 * x * x * x))
    return 0.5 * x * (1.0 + t), t


def _gelu_grad(x, t):
    return 0.5 * (1.0 + t) + 0.5 * x * (1.0 - t * t) * _GC * (1.0 + 3.0 * 0.044715 * x * x)


def _rowsum(x):
    return jnp.sum(x, axis=0, keepdims=True)


def _lmean(x):
    return jnp.mean(x, axis=-1, keepdims=True)


def _ln(z):
    mu = _lmean(z)
    zc = z - mu
    rstd = lax.rsqrt(_lmean(zc * zc) + LN_EPS)
    return zc * rstd, rstd


def _ln_bwd(dxh, xhat, rstd):
    return rstd * (dxh - _lmean(dxh) - xhat * _lmean(dxh * xhat))


def _hs(h):
    return slice(h * HD, (h + 1) * HD)


def _expm1(x):
    small = x * (1.0 + x * 0.5 * (1.0 + x * (1.0 / 3.0) * (1.0 + x * 0.25 * (1.0 + x * 0.2 * (1.0 + x * (1.0 / 6.0))))))
    return jnp.where(jnp.abs(x) < 0.25, small, jnp.exp(x) - 1.0)


def _softplus(x):
    e = jnp.exp(-jnp.abs(x))
    l1p = jnp.where(e < 1e-4, e - 0.5 * e * e, jnp.log(1.0 + e))
    return jnp.maximum(x, 0.0) + l1p


def _scan_fwd(a, b):
    n = a.shape[0]
    row = lax.broadcasted_iota(jnp.int32, a.shape, 0)
    d = 1
    while d < n:
        m = row >= d
        b = jnp.where(m, a * pltpu.roll(b, d, 0) + b, b)
        a = jnp.where(m, a * pltpu.roll(a, d, 0), a)
        d *= 2
    return a, b


def _scan_bwd(a, b):
    n = a.shape[0]
    row = lax.broadcasted_iota(jnp.int32, a.shape, 0)
    d = 1
    while d < n:
        m = row < n - d
        b = jnp.where(m, a * pltpu.roll(b, n - d, 0) + b, b)
        a = jnp.where(m, a * pltpu.roll(a, n - d, 0), a)
        d *= 2
    return a, b


def _chunk_mats(n):
    r = lax.broadcasted_iota(jnp.int32, (n, n), 0)
    c = lax.broadcasted_iota(jnp.int32, (n, n), 1)
    same = (r // CHUNK) == (c // CHUNK)
    return same, jnp.logical_and(same, c <= r)


def _pool_w(shape):
    lane = lax.broadcasted_iota(jnp.int32, shape, 1)
    return jnp.where(lane < POOL_GROUP, 2, jnp.where(lane < 2 * POOL_GROUP, 4, jnp.where(lane < 3 * POOL_GROUP, 8, 16)))


def _pool_pick(r1, r2, r3, r4):
    lane = lax.broadcasted_iota(jnp.int32, r1.shape, 1)
    return jnp.where(lane < POOL_GROUP, r1, jnp.where(lane < 2 * POOL_GROUP, r2, jnp.where(lane < 3 * POOL_GROUP, r3, r4)))


def _const_spec(a):
    nd = a.ndim
    return pl.BlockSpec(a.shape, lambda i, _nd=nd: (0,) * _nd, pipeline_mode=pl.Buffered(1))


def _acc_spec(shape):
    nd = len(shape)
    return pl.BlockSpec(shape, lambda i, _nd=nd: (0,) * _nd)


def _params(sem="arbitrary"):
    return pltpu.CompilerParams(dimension_semantics=(sem,), vmem_limit_bytes=VMEM_LIMIT)


def _xattn_fwd(qx, khT_ref, vh_ref):
    xo = jnp.zeros((qx.shape[0], XW), f32)
    ps = []
    for h in range(XHEADS):
        s = _mm(qx, khT_ref[h]) * (XDIM ** -0.5)
        e = jnp.exp(s - jnp.max(s, axis=-1, keepdims=True))
        p = e / jnp.sum(e, axis=-1, keepdims=True)
        xo = xo + _mm(p, vh_ref[h])
        ps.append(p)
    return xo, ps


def _hgrn_parallel(q_raw, fl, lb):
    n = q_raw.shape[0]
    same, tri = _chunk_mats(n)
    sq = _sig(q_raw)
    qf = q_raw * sq
    sgm = _sig(fl)
    f = lb + (1.0 - lb) * sgm
    logf = jnp.log(f)
    k = 1.0 - f
    g = _mm_exact(tri.astype(f32), logf)
    gl = _mm_exact(same.astype(f32), logf)
    eg = jnp.exp(g)
    eng = jnp.exp(-g)
    ee = jnp.exp(gl - g)
    return dict(sq=sq, qf=qf, sgm=sgm, f=f, k=k, eg=eg, eng=eng, ee=ee, q_dec=qf * eg, k_inv=k * eng, k_end=k * ee,
                a=jnp.exp(gl))


def _hgrn_intra(q_dec, k_inv, v):
    n = q_dec.shape[0]
    _, tri = _chunk_mats(HD)
    outs = []
    for h in range(NH):
        blks = []
        for b in range(n // HD):
            rs = slice(b * HD, (b + 1) * HD)
            sc = jnp.where(tri, _mm_nt(q_dec[rs, _hs(h)], k_inv[rs, _hs(h)]), 0.0)
            blks.append(_mm(sc, v[rs, _hs(h)]))
        outs.append(jnp.concatenate(blks, axis=0))
    return jnp.concatenate(outs, axis=-1)


def _hgrn_inter_fwd(qdec_s, kend_s, v_s, a_s, oint_s, st_ref, states_s):
    def step(c, carry):
        r = pl.multiple_of(c * CHUNK, CHUNK)
        for h in range(NH):
            st = st_ref[h]
            if states_s is not None:
                states_s[c, h] = st
            oint_s[pl.ds(r, CHUNK), _hs(h)] = _mm_nt(qdec_s[pl.ds(r, CHUNK), _hs(h)], st)
            arow = a_s[pl.ds(r, 1), _hs(h)]
            st_ref[h] = st * arow + _mm_tn(v_s[pl.ds(r, CHUNK), _hs(h)], kend_s[pl.ds(r, CHUNK), _hs(h)])
        return carry

    lax.fori_loop(0, qdec_s.shape[0] // CHUNK, step, 0)


def _rms(o):
    outs, rs = [], []
    for h in range(NH):
        oh = o[:, _hs(h)]
        r = lax.rsqrt(_lmean(oh * oh) + RMS_EPS)
        outs.append(oh * r)
        rs.append(r)
    return jnp.concatenate(outs, axis=-1), rs


def _gmlp_core(u_raw, v_raw, wtri_ref, bcolb_ref):
    gu, tu = _gelu(u_raw)
    gv, tv = _gelu(v_raw)
    vns, rstds, mixeds = [], [], []
    for h in range(NH):
        vn, rstd = _ln(gv[:, _hs(h)])
        blks = []
        for n in range(u_raw.shape[0] // HD):
            blks.append(_mm(wtri_ref[h], vn[n * HD:(n + 1) * HD]) + bcolb_ref[h])
        vns.append(vn)
        rstds.append(rstd)
        mixeds.append(jnp.concatenate(blks, axis=0))
    mixed = jnp.concatenate(mixeds, axis=-1)
    return gu, tu, tv, vns, rstds, mixed


def _pool_core(p, carry, row0, wbd_ref):
    ext = jnp.concatenate([carry, p], axis=0)
    r1 = ext + pltpu.roll(ext, 1, 0)
    r2 = r1 + pltpu.roll(r1, 2, 0)
    r3 = r2 + pltpu.roll(r2, 4, 0)
    r4 = r3 + pltpu.roll(r3, 8, 0)
    sel = _pool_pick(r1, r2, r3, r4)[2 * SUB:]
    grow = row0 + lax.broadcasted_iota(jnp.int32, p.shape, 0)
    inv_cnt = 1.0 / jnp.minimum(grow + 1, _pool_w(p.shape)).astype(f32)
    diff = sel * inv_cnt - p
    return diff, inv_cnt, _mm(diff, wbd_ref[...])


def _lru_core(xb, ccar, row0, p):
    ext = jnp.concatenate([ccar, xb], axis=0)
    cw = p["cw"]
    x1, x2, x3 = pltpu.roll(ext, 1, 0)[SUB:], pltpu.roll(ext, 2, 0)[SUB:], pltpu.roll(ext, 3, 0)[SUB:]
    xc = cw[3:4, :] * xb + cw[2:3, :] * x1 + cw[1:2, :] * x2 + cw[0:1, :] * x3 + p["cb"][...]
    gxs, gas = [], []
    for h in range(NH):
        gxs.append(_mm(xc[:, _hs(h)], p["wgx"][h]))
        gas.append(_mm(xc[:, _hs(h)], p["wga"][h]))
    gx = _sig(jnp.concatenate(gxs, axis=-1) + p["bgx"][...])
    ga = _sig(jnp.concatenate(gas, axis=-1) + p["bga"][...])
    sp = _softplus(-p["ap"][...])
    la = -LRU_C * ga * sp
    a = jnp.exp(la)
    grow = row0 + lax.broadcasted_iota(jnp.int32, xb.shape, 0)
    first = grow == 0
    mult = jnp.where(first, 1.0, jnp.sqrt(-_expm1(2.0 * la)))
    bt = mult * gx * xc
    return dict(x1=x1, x2=x2, x3=x3, xc=xc, gx=gx, ga=ga, sp=sp, a=a, mult=mult, bt=bt, first=first)


def _fwd_layer(kind, xin, w_in, w_out, lng, lnb, khT, vh, prm, tgt):
    S = xin.shape[0]
    TS = _TS[kind]
    nt = S // TS
    off = _OFFS[kind]
    W = off["W"]
    last = tgt is not None
    pnames = _PRM[kind]
    pvals = [prm[n] for n in pnames]

    def body(*refs):
        it = iter(refs)
        xin_ref, win_ref, wout_ref, lng_ref, lnb_ref, khT_ref, vh_ref = (next(it) for _ in range(7))
        p = {n: next(it) for n in pnames}
        tgt_ref = next(it) if last else None
        xout_ref, proj_ref, z_ref = next(it), next(it), next(it)
        loss_ref = next(it) if last else None
        rest = list(it)
        i = pl.program_id(0)
        x = xin_ref[...]
        proj_ref[...] = _mm(x, win_ref[...])

        if kind == 0:
            gu, _, _, _, _, mixed = _gmlp_core(proj_ref[:, 0:TOK], proj_ref[:, TOK:2 * TOK], p["wtri"], p["bcolb"])
            tok = gu * mixed
        elif kind == 1:
            st_save, st_ref, qdec_s, kend_s, v_s, a_s, oint_s = rest

            @pl.when(i == 0)
            def _():
                st_ref[...] = jnp.zeros_like(st_ref)

            st_save[0] = st_ref[...]
            v = proj_ref[:, 2 * TOK:3 * TOK]
            hp = _hgrn_parallel(proj_ref[:, 0:TOK], proj_ref[:, TOK:2 * TOK], p["lb"][...])
            qdec_s[...] = hp["q_dec"]
            kend_s[...] = hp["k_end"]
            v_s[...] = v
            a_s[...] = hp["a"]
            o_intra = _hgrn_intra(hp["q_dec"], hp["k_inv"], v)
            _hgrn_inter_fwd(qdec_s, kend_s, v_s, a_s, oint_s, st_ref, None)
            on, _ = _rms(o_intra + oint_s[...])
            tok = on * p["ng"][...]
        elif kind == 2:
            pc_save, pcar = rest

            @pl.when(i == 0)
            def _():
                pcar[...] = jnp.zeros_like(pcar)

            pc_save[0] = pcar[...]
            pp = proj_ref[:, 0:TOK]
            _, _, y = _pool_core(pp, pcar[...], i * TS, p["wbd"])
            pcar[...] = pp[TS - 2 * SUB:, :]
            tok = y * p["scale"][...]
        else:
            cc_save, hc_save, ccar, hcar = rest

            @pl.when(i == 0)
            def _():
                ccar[...] = jnp.zeros_like(ccar)
                hcar[...] = jnp.zeros_like(hcar)

            cc_save[0] = ccar[...]
            hc_save[0] = hcar[...]
            xb = proj_ref[:, 0:TOK]
            lc = _lru_core(xb, ccar[...], i * TS, p)
            P, B = _scan_fwd(lc["a"], lc["bt"])
            tok = P * hcar[SUB - 1:SUB, :] + B
            ccar[...] = xb[TS - SUB:, :]
            hcar[...] = tok[TS - SUB:, :]

        xo, _ = _xattn_fwd(proj_ref[:, off["qx"]:off["qx"] + XW], khT_ref, vh_ref)
        gate = proj_ref[:, off["gate"]:off["gate"] + D_MODEL]
        mixed = jnp.concatenate([tok, xo], axis=-1) * (gate * _sig(gate))
        z = ALPHA * x + _mm(mixed, wout_ref[...])
        z_ref[...] = z
        xhat, _ = _ln(z)
        xout = xhat * lng_ref[...] + lnb_ref[...]
        if last:
            e = xout - tgt_ref[...]
            xout_ref[...] = e * (1.0 / D_MODEL)
            es = _rowsum(e * e)
            tot = es[:, 0:LANE]
            for j in range(1, D_MODEL // LANE):
                tot = tot + es[:, j * LANE:(j + 1) * LANE]

            @pl.when(i == 0)
            def _():
                loss_ref[...] = jnp.zeros_like(loss_ref)

            loss_ref[0:1, :] += tot
        else:
            xout_ref[...] = xout

    tile = lambda w: pl.BlockSpec((TS, w), lambda i: (i, 0))
    in_arrays = [xin, w_in, w_out, lng, lnb, khT, vh] + pvals + ([tgt] if last else [])
    in_specs = [tile(D_MODEL)] + [_const_spec(a) for a in in_arrays[1:7 + len(pvals)]] + ([tile(D_MODEL)] if last else [])
    out_shape = [jax.ShapeDtypeStruct((S, D_MODEL), f32), jax.ShapeDtypeStruct((S, W), f32), jax.ShapeDtypeStruct((S, D_MODEL), f32)]
    out_specs = [tile(D_MODEL), tile(W), tile(D_MODEL)]
    if last:
        out_shape.append(jax.ShapeDtypeStruct((SUB, LANE), f32))
        out_specs.append(_acc_spec((SUB, LANE)))
    scratch = []
    save = lambda *s: (jax.ShapeDtypeStruct((nt,) + s, f32), pl.BlockSpec((1,) + s, lambda i, _n=len(s): (i,) + (0,) * _n))
    if kind == 1:
        sh, sp = save(NH, HD, HD)
        out_shape.append(sh)
        out_specs.append(sp)
        scratch = [pltpu.VMEM((NH, HD, HD), f32)] + [pltpu.VMEM((TS, TOK), f32)] * 5
    elif kind == 2:
        sh, sp = save(2 * SUB, TOK)
        out_shape.append(sh)
        out_specs.append(sp)
        scratch = [pltpu.VMEM((2 * SUB, TOK), f32)]
    elif kind == 3:
        for _ in range(2):
            sh, sp = save(SUB, TOK)
            out_shape.append(sh)
            out_specs.append(sp)
        scratch = [pltpu.VMEM((SUB, TOK), f32)] * 2
    return pl.pallas_call(body, name=f"fwd_layer{kind}", grid=(nt,), in_specs=in_specs, out_specs=out_specs,
                          out_shape=out_shape, scratch_shapes=scratch, compiler_params=_params())(*in_arrays)


def _small_grad_shapes(kind):
    if kind == 0:
        return dict(dwtri=(NH, HD, HD), dbacc=(NH, HD, HD))
    if kind == 1:
        return dict(dlb=(SUB, TOK), dng=(SUB, TOK))
    if kind == 2:
        return dict(dwbd=(TOK, TOK), dscale=(SUB, TOK))
    return dict(dcw=(SUB, TOK), dvec=(SUB, TOK), dwgx=(NH, HD, HD), dwga=(NH, HD, HD))


def _bwd_layer(kind, dxout, z, proj, w_inT, w_outT, lng, kh, khT, vh, vhT, prm, saves):
    S = dxout.shape[0]
    TS = _TS[kind]
    nt = S // TS
    off = _OFFS[kind]
    W = off["W"]
    pnames = _PRM[kind]
    pvals = [prm[n] for n in pnames]
    sg_shapes = _small_grad_shapes(kind)
    sg_names = list(sg_shapes)
    n_saves = len(saves)

    def body(*refs):
        it = iter(refs)
        dxo_ref, z_ref, proj_ref, winT_ref, woutT_ref, lng_ref, kh_ref, khT_ref, vh_ref, vhT_ref = (next(it) for _ in range(10))
        p = {n: next(it) for n in pnames}
        sv = [next(it) for _ in range(n_saves)]
        dxin_ref, dproj_ref, mixed_ref, dy_ref, dln_ref, dk_ref, dv_ref = (next(it) for _ in range(7))
        sg = {n: next(it) for n in sg_names}
        rest = list(it)
        step = pl.program_id(0)
        i = nt - 1 - step

        @pl.when(step == 0)
        def _():
            dln_ref[...] = jnp.zeros_like(dln_ref)
            dk_ref[...] = jnp.zeros_like(dk_ref)
            dv_ref[...] = jnp.zeros_like(dv_ref)
            for n in sg_names:
                sg[n][...] = jnp.zeros_like(sg[n])

        dxo = dxo_ref[...]
        xhat, rstd = _ln(z_ref[...])
        dln_ref[0:1, :] += _rowsum(dxo * xhat)
        dln_ref[1:2, :] += _rowsum(dxo)
        dz = _ln_bwd(dxo * lng_ref[...], xhat, rstd)
        dyb = dz.astype(bf16)
        dy_ref[...] = dyb
        dmixed = _mm(dyb, woutT_ref[...])

        aux = {}
        if kind == 0:
            u_raw, v_raw = proj_ref[:, 0:TOK], proj_ref[:, TOK:2 * TOK]
            gu, tu, tv, vns, rstds, mx = _gmlp_core(u_raw, v_raw, p["wtri"], p["bcolb"])
            tok = gu * mx
        elif kind == 1:
            st_save, = sv
            dst_ref, fst_ref, states_s, qdec_s, kend_s, v_s, a_s, oint_s, do_s, dqdec_s, dkend_s, dv_s, dgl_s = rest

            @pl.when(step == 0)
            def _():
                dst_ref[...] = jnp.zeros_like(dst_ref)

            fst_ref[...] = st_save[0]
            v = proj_ref[:, 2 * TOK:3 * TOK]
            hp = _hgrn_parallel(proj_ref[:, 0:TOK], proj_ref[:, TOK:2 * TOK], p["lb"][...])
            qdec_s[...] = hp["q_dec"]
            kend_s[...] = hp["k_end"]
            v_s[...] = v
            a_s[...] = hp["a"]
            o_intra = _hgrn_intra(hp["q_dec"], hp["k_inv"], v)
            _hgrn_inter_fwd(qdec_s, kend_s, v_s, a_s, oint_s, fst_ref, states_s)
            o = o_intra + oint_s[...]
            on, rs = _rms(o)
            tok = on * p["ng"][...]
            aux = dict(hp=hp, v=v, o=o, on=on, rs=rs)
        elif kind == 2:
            pc_save, = sv
            dpcar, = rest
            pp = proj_ref[:, 0:TOK]
            diff, inv_cnt, y = _pool_core(pp, pc_save[0], i * TS, p["wbd"])
            tok = y * p["scale"][...]
        else:
            cc_save, hc_save = sv
            dccar, gcar = rest
            xb = proj_ref[:, 0:TOK]
            lc = _lru_core(xb, cc_save[0], i * TS, p)
            P, B = _scan_fwd(lc["a"], lc["bt"])
            hin = hc_save[0, SUB - 1:SUB, :]
            tok = P * hin + B

        xo, ps = _xattn_fwd(proj_ref[:, off["qx"]:off["qx"] + XW], khT_ref, vh_ref)
        gate = proj_ref[:, off["gate"]:off["gate"] + D_MODEL]
        sgm = _sig(gate)
        sgate = gate * sgm
        cat = jnp.concatenate([tok, xo], axis=-1)
        mixed_ref[...] = (cat * sgate).astype(bf16)
        dcat = dmixed * sgate
        dproj_ref[:, off["gate"]:off["gate"] + D_MODEL] = (dmixed * cat * (sgm * (1.0 + gate * (1.0 - sgm)))).astype(bf16)
        dtok = dcat[:, 0:TOK]
        dxo_att = dcat[:, TOK:]

        qx = proj_ref[:, off["qx"]:off["qx"] + XW]
        dqx = jnp.zeros((TS, XW), f32)
        for h in range(XHEADS):
            dp = _mm(dxo_att, vhT_ref[h])
            ds = ps[h] * (dp - jnp.sum(dp * ps[h], axis=-1, keepdims=True)) * (XDIM ** -0.5)
            dqx = dqx + _mm(ds, kh_ref[h])
            dk_ref[h] += _mm_tn(ds, qx)
            dv_ref[h] += _mm_tn(ps[h], dxo_att)
        dproj_ref[:, off["qx"]:off["qx"] + XW] = dqx.astype(bf16)

        if kind == 0:
            tril = lax.broadcasted_iota(jnp.int32, (HD, HD), 1) <= lax.broadcasted_iota(jnp.int32, (HD, HD), 0)
            dgu = dtok * mx
            dmx = dtok * gu
            dgvs = []
            for h in range(NH):
                dmh = dmx[:, _hs(h)]
                blks = []
                for n in range(TS // HD):
                    rs_ = slice(n * HD, (n + 1) * HD)
                    blks.append(_mm(p["wtriT"][h], dmh[rs_]))
                    sg["dwtri"][h] += jnp.where(tril, _mm_nt(dmh[rs_], vns[h][rs_]), 0.0)
                    sg["dbacc"][h] += dmh[rs_]
                dgvs.append(_ln_bwd(jnp.concatenate(blks, axis=0), vns[h], rstds[h]))
            dgv = jnp.concatenate(dgvs, axis=-1)
            dproj_ref[:, 0:TOK] = (dgu * _gelu_grad(u_raw, tu)).astype(bf16)
            dproj_ref[:, TOK:2 * TOK] = (dgv * _gelu_grad(v_raw, tv)).astype(bf16)
        elif kind == 1:
            hp, v, o, on, rs = aux["hp"], aux["v"], aux["o"], aux["on"], aux["rs"]
            ng = p["ng"][...]
            sg["dng"][0:1, :] += _rowsum(dtok * on)
            dn = dtok * ng
            dos = []
            for h in range(NH):
                oh, r = o[:, _hs(h)], rs[h]
                dos.append(r * (dn[:, _hs(h)] - oh * (r * r) * _lmean(dn[:, _hs(h)] * oh)))
            do = jnp.concatenate(dos, axis=-1)
            do_s[...] = do
            _, tri = _chunk_mats(HD)
            dqd, dki, dvi = [], [], []
            for h in range(NH):
                bq, bk, bv = [], [], []
                for b in range(TS // HD):
                    rs_ = slice(b * HD, (b + 1) * HD)
                    qd, ki = hp["q_dec"][rs_, _hs(h)], hp["k_inv"][rs_, _hs(h)]
                    sc = jnp.where(tri, _mm_nt(qd, ki), 0.0)
                    dsc = jnp.where(tri, _mm_nt(do[rs_, _hs(h)], v[rs_, _hs(h)]), 0.0)
                    bv.append(_mm_tn(sc, do[rs_, _hs(h)]))
                    bq.append(_mm(dsc, ki))
                    bk.append(_mm_tn(dsc, qd))
                dqd.append(jnp.concatenate(bq, axis=0))
                dki.append(jnp.concatenate(bk, axis=0))
                dvi.append(jnp.concatenate(bv, axis=0))
            dqdec_s[...] = jnp.concatenate(dqd, axis=-1)
            dk_inv = jnp.concatenate(dki, axis=-1)
            dv_s[...] = jnp.concatenate(dvi, axis=-1)
            row16 = lax.broadcasted_iota(jnp.int32, (CHUNK, HD), 0)

            def bstep(cc, carry):
                c = TS // CHUNK - 1 - cc
                r = pl.multiple_of(c * CHUNK, CHUNK)
                for h in range(NH):
                    stp = states_s[c, h]
                    dst = dst_ref[h]
                    do_c = do_s[pl.ds(r, CHUNK), _hs(h)]
                    vv = v_s[pl.ds(r, CHUNK), _hs(h)]
                    ke = kend_s[pl.ds(r, CHUNK), _hs(h)]
                    arow = a_s[pl.ds(r, 1), _hs(h)]
                    dqdec_s[pl.ds(r, CHUNK), _hs(h)] += _mm(do_c, stp)
                    dkend_s[pl.ds(r, CHUNK), _hs(h)] = _mm(vv, dst)
                    dv_s[pl.ds(r, CHUNK), _hs(h)] += _mm_nt(ke, dst)
                    da = jnp.sum(dst * stp, axis=0, keepdims=True) * arow
                    dgl_s[pl.ds(r, CHUNK), _hs(h)] = jnp.where(row16 == 0, jnp.broadcast_to(da, (CHUNK, HD)), 0.0)
                    dst_ref[h] = dst * arow + _mm_tn(do_c, qdec_s[pl.ds(r, CHUNK), _hs(h)])
                return carry

            lax.fori_loop(0, TS // CHUNK, bstep, 0)
            dq_dec = dqdec_s[...]
            dk_end = dkend_s[...]
            same, _ = _chunk_mats(TS)
            triT = jnp.logical_and(same, lax.broadcasted_iota(jnp.int32, (TS, TS), 1) >= lax.broadcasted_iota(jnp.int32, (TS, TS), 0))
            dg = dq_dec * hp["q_dec"] - dk_inv * hp["k_inv"] - dk_end * hp["k_end"]
            dk = dk_inv * hp["eng"] + dk_end * hp["ee"]
            dglr = dk_end * hp["k_end"] + dgl_s[...]
            dlogf = _mm_exact(triT.astype(f32), dg) + _mm_exact(same.astype(f32), dglr)
            df = dlogf / hp["f"] - dk
            lb = p["lb"][...]
            sg["dlb"][0:1, :] += _rowsum(df * (1.0 - hp["sgm"]))
            q_raw = proj_ref[:, 0:TOK]
            dproj_ref[:, 0:TOK] = (dq_dec * hp["eg"] * (hp["sq"] * (1.0 + q_raw * (1.0 - hp["sq"])))).astype(bf16)
            dproj_ref[:, TOK:2 * TOK] = (df * (1.0 - lb) * hp["sgm"] * (1.0 - hp["sgm"])).astype(bf16)
            dproj_ref[:, 2 * TOK:3 * TOK] = dv_s[...].astype(bf16)
        elif kind == 2:
            @pl.when(step == 0)
            def _():
                dpcar[...] = jnp.zeros_like(dpcar)

            sg["dscale"][0:1, :] += _rowsum(dtok * y)
            dyp = dtok * p["scale"][...]
            sg["dwbd"][...] += _mm_tn(diff, dyp)
            ddiff = _mm(dyp, p["wbdT"][...])
            q = ddiff * inv_cnt
            ext = jnp.concatenate([q, dpcar[...]], axis=0)
            n = TS + 2 * SUB
            r1 = ext + pltpu.roll(ext, n - 1, 0)
            r2 = r1 + pltpu.roll(r1, n - 2, 0)
            r3 = r2 + pltpu.roll(r2, n - 4, 0)
            r4 = r3 + pltpu.roll(r3, n - 8, 0)
            dproj_ref[:, 0:TOK] = (_pool_pick(r1, r2, r3, r4)[:TS] - ddiff).astype(bf16)
            dpcar[...] = q[0:2 * SUB, :]
        else:
            @pl.when(step == 0)
            def _():
                dccar[...] = jnp.zeros_like(dccar)
                gcar[...] = jnp.zeros_like(gcar)

            a, mult, gx, ga, xc = lc["a"], lc["mult"], lc["gx"], lc["ga"], lc["xc"]
            row = lax.broadcasted_iota(jnp.int32, (TS, TOK), 0)
            an = jnp.where(row == TS - 1, 1.0, pltpu.roll(a, TS - 1, 0))
            Pb, Bb = _scan_bwd(an, dtok)
            lam = Pb * gcar[0:1, :] + Bb
            gcar[...] = (a * lam)[0:SUB, :]
            hprev = jnp.where(row == 0, jnp.broadcast_to(hin, (TS, TOK)), pltpu.roll(tok, 1, 0))
            dmult = lam * gx * xc
            dgx = lam * mult * xc
            dxc = lam * mult * gx
            dla = lam * hprev * a - jnp.where(lc["first"], 0.0, dmult * a * a / mult)
            sp = lc["sp"]
            dga = -LRU_C * sp * dla
            dsp = _rowsum(-LRU_C * ga * dla)
            sg["dvec"][0:1, :] += dsp * (-_sig(-p["ap"][...]))
            dpx = dgx * gx * (1.0 - gx)
            dpa = dga * ga * (1.0 - ga)
            sg["dvec"][1:2, :] += _rowsum(dpx)
            sg["dvec"][2:3, :] += _rowsum(dpa)
            dxcs = []
            for h in range(NH):
                dxcs.append(_mm(dpx[:, _hs(h)], p["wgxT"][h]) + _mm(dpa[:, _hs(h)], p["wgaT"][h]))
                sg["dwgx"][h] += _mm_tn(xc[:, _hs(h)], dpx[:, _hs(h)])
                sg["dwga"][h] += _mm_tn(xc[:, _hs(h)], dpa[:, _hs(h)])
            dxc = dxc + jnp.concatenate(dxcs, axis=-1)
            sg["dvec"][3:4, :] += _rowsum(dxc)
            sg["dcw"][3:4, :] += _rowsum(dxc * xb)
            sg["dcw"][2:3, :] += _rowsum(dxc * lc["x1"])
            sg["dcw"][1:2, :] += _rowsum(dxc * lc["x2"])
            sg["dcw"][0:1, :] += _rowsum(dxc * lc["x3"])
            ext = jnp.concatenate([dxc, dccar[...]], axis=0)
            n = TS + SUB
            cw = p["cw"]
            dproj_ref[:, 0:TOK] = (cw[3:4, :] * dxc + cw[2:3, :] * pltpu.roll(ext, n - 1, 0)[:TS]
                                   + cw[1:2, :] * pltpu.roll(ext, n - 2, 0)[:TS]
                                   + cw[0:1, :] * pltpu.roll(ext, n - 3, 0)[:TS]).astype(bf16)
            dccar[...] = dxc[0:SUB, :]

        dxin_ref[...] = ALPHA * dz + _mm(dproj_ref[...], winT_ref[...])

    rtile = lambda w: pl.BlockSpec((TS, w), lambda s: (nt - 1 - s, 0))
    consts = [w_inT, w_outT, lng, kh, khT, vh, vhT] + pvals
    in_arrays = [dxout, z, proj] + consts + list(saves)
    in_specs = [rtile(D_MODEL), rtile(D_MODEL), rtile(W)] + [_const_spec(a) for a in consts]
    for a in saves:
        in_specs.append(pl.BlockSpec((1,) + a.shape[1:], lambda s, _n=a.ndim - 1: (nt - 1 - s,) + (0,) * _n))
    out_shape = [jax.ShapeDtypeStruct((S, D_MODEL), f32), jax.ShapeDtypeStruct((S, W), bf16),
                 jax.ShapeDtypeStruct((S, D_MODEL), bf16), jax.ShapeDtypeStruct((S, D_MODEL), bf16),
                 jax.ShapeDtypeStruct((SUB, D_MODEL), f32), jax.ShapeDtypeStruct((XHEADS, XW, XW), f32),
                 jax.ShapeDtypeStruct((XHEADS, XW, XW), f32)]
    out_specs = [rtile(D_MODEL), rtile(W), rtile(D_MODEL), rtile(D_MODEL), _acc_spec((SUB, D_MODEL)),
                 _acc_spec((XHEADS, XW, XW)), _acc_spec((XHEADS, XW, XW))]
    for n in sg_names:
        out_shape.append(jax.ShapeDtypeStruct(sg_shapes[n], f32))
        out_specs.append(_acc_spec(sg_shapes[n]))
    if kind == 1:
        scratch = ([pltpu.VMEM((NH, HD, HD), f32)] * 2 + [pltpu.VMEM((TS // CHUNK, NH, HD, HD), f32)]
                   + [pltpu.VMEM((TS, TOK), f32)] * 10)
    elif kind == 2:
        scratch = [pltpu.VMEM((2 * SUB, TOK), f32)]
    elif kind == 3:
        scratch = [pltpu.VMEM((SUB, TOK), f32)] * 2
    else:
        scratch = []
    outs = pl.pallas_call(body, name=f"bwd_layer{kind}", grid=(nt,), in_specs=in_specs, out_specs=out_specs,
                          out_shape=out_shape, scratch_shapes=scratch, compiler_params=_params())(*in_arrays)
    return outs[:7], dict(zip(sg_names, outs[7:]))


def _prep(mem, w_kv, logits):
    def body(mem_ref, w_ref, lg_ref, kh_ref, khT_ref, vh_ref, vhT_ref, p_ref):
        kv = _mm(mem_ref[...], w_ref[...])
        k, v = kv[:, 0:XW], kv[:, XW:]
        kT, vT = k.T, v.T
        col = lax.broadcasted_iota(jnp.int32, (XW, XW), 1) // XDIM
        row = lax.broadcasted_iota(jnp.int32, (XW, XW), 0) // XDIM
        for h in range(XHEADS):
            kh_ref[h] = jnp.where(col == h, k, 0.0).astype(bf16)
            vh_ref[h] = jnp.where(col == h, v, 0.0).astype(bf16)
            khT_ref[h] = jnp.where(row == h, kT, 0.0).astype(bf16)
            vhT_ref[h] = jnp.where(row == h, vT, 0.0).astype(bf16)
        lg = lg_ref[...]
        e = jnp.exp(lg - jnp.max(lg, axis=0, keepdims=True))
        p_ref[...] = e / jnp.sum(e, axis=0, keepdims=True)

    vm = pl.BlockSpec(memory_space=pltpu.VMEM)
    hs = jax.ShapeDtypeStruct((XHEADS, XW, XW), bf16)
    return pl.pallas_call(body, name="prep_memory", in_specs=[vm] * 3, out_specs=[vm] * 5,
                          out_shape=[hs, hs, hs, hs, jax.ShapeDtypeStruct(logits.shape, f32)])(mem, w_kv, logits)


def _kv_bwd(mem, dks, dvs):
    def body(mem_ref, *refs):
        out_ref = refs[-1]
        col = lax.broadcasted_iota(jnp.int32, (XW, XW), 1) // XDIM
        dk = jnp.zeros((XW, XW), f32)
        dv = jnp.zeros((XW, XW), f32)
        for l in range(DEPTH):
            for h in range(XHEADS):
                dk = dk + jnp.where(col == h, refs[l][h], 0.0)
                dv = dv + jnp.where(col == h, refs[DEPTH + l][h], 0.0)
        out_ref[:, 0:XW] = _mm_tn(mem_ref[...], dk)
        out_ref[:, XW:] = _mm_tn(mem_ref[...], dv)

    vm = pl.BlockSpec(memory_space=pltpu.VMEM)
    return pl.pallas_call(body, name="kv_bwd", in_specs=[vm] * (1 + 2 * DEPTH), out_specs=vm,
                          out_shape=jax.ShapeDtypeStruct((D_MODEL, 2 * XW), f32))(mem, *dks, *dvs)


def _tn_gemm(a, b, name, nb):
    S, M = a.shape
    N = b.shape[1]
    NB = N // nb
    nk = S // TK

    def body(a_ref, b_ref, o_ref):
        @pl.when(pl.program_id(1) == 0)
        def _():
            o_ref[...] = jnp.zeros_like(o_ref)

        o_ref[...] += _mm_tn(a_ref[...], b_ref[...])

    return pl.pallas_call(body, name=name, grid=(nb, nk),
                          in_specs=[pl.BlockSpec((TK, M), lambda j, k: (k, 0)), pl.BlockSpec((TK, NB), lambda j, k: (k, j))],
                          out_specs=pl.BlockSpec((M, NB), lambda j, k: (0, j)),
                          out_shape=jax.ShapeDtypeStruct((M, N), f32),
                          compiler_params=pltpu.CompilerParams(dimension_semantics=("parallel", "arbitrary"),
                                                               vmem_limit_bytes=VMEM_LIMIT))(a, b)


def _rows_block(R, mult=16, cap=1024):
    best = R
    for d in range(mult, min(R, cap) + 1, mult):
        if R % d == 0:
            best = d
    return best


def _add2(a, b, name):
    R, C = a.shape
    br = _rows_block(R)

    def body(a_ref, b_ref, o_ref, ob_ref):
        s = a_ref[...] + b_ref[...]
        o_ref[...] = s
        ob_ref[...] = s.astype(bf16)

    spec = pl.BlockSpec((br, C), lambda i: (i, 0))
    return pl.pallas_call(body, name=name, grid=(R // br,), in_specs=[spec, spec], out_specs=[spec, spec],
                          out_shape=[jax.ShapeDtypeStruct((R, C), f32), jax.ShapeDtypeStruct((R, C), bf16)],
                          compiler_params=_params("parallel"))(a, b)


def _add4(own, r, name):
    R, C = own.shape
    br = _rows_block(R)

    def body(o_ref, r_ref, out_ref):
        out_ref[...] = ((o_ref[...] + r_ref[0].astype(f32)) + r_ref[1].astype(f32)) + r_ref[2].astype(f32)

    spec = pl.BlockSpec((br, C), lambda i: (i, 0))
    return pl.pallas_call(body, name=name, grid=(R // br,),
                          in_specs=[spec, pl.BlockSpec((3, br, C), lambda i: (0, i, 0))], out_specs=spec,
                          out_shape=jax.ShapeDtypeStruct((R, C), f32), compiler_params=_params("parallel"))(own, r)


def _adamw(w, g, m, v, name):
    R, C = w.shape
    br = _rows_block(R, mult=SUB, cap=512)
    c1 =1.0 / (1.0 - ADAM_B1 ** ADAM_STEP)
    c2 = 1.0 / (1.0 - ADAM_B2 ** ADAM_STEP)

    def body(w_ref, g_ref, m_ref, v_ref, d_ref, nm_ref, nv_ref):
        g_ = g_ref[...]
        nm = ADAM_B1 * m_ref[...] + (1.0 - ADAM_B1) * g_
        nv = ADAM_B2 * v_ref[...] + (1.0 - ADAM_B2) * (g_ * g_)
        nm_ref[...] = nm
        nv_ref[...] = nv
        d_ref[...] = -ADAM_LR * ((nm * c1) / (jnp.sqrt(nv * c2) + ADAM_EPS) + ADAM_WD * w_ref[...])

    spec = pl.BlockSpec((br, C), lambda i: (i, 0))
    sh = jax.ShapeDtypeStruct((R, C), f32)
    return pl.pallas_call(body, name=name, grid=(R // br,), in_specs=[spec] * 4, out_specs=[spec] * 3,
                          out_shape=[sh, sh, sh], compiler_params=_params("parallel"))(w, g, m, v)


def _small_finish(dbacc, p_soft, dlb):
    def body(db_ref, p_ref, dlb_ref, dbs_ref, dlg_ref):
        lane = lax.broadcasted_iota(jnp.int32, (HD, HD), 1)
        acc = jnp.zeros((HD, HD), f32)
        for h in range(NH):
            acc = acc + jnp.where(lane == h, jnp.sum(db_ref[h], axis=-1, keepdims=True), 0.0)
        dbs_ref[...] = acc
        p = p_ref[...]
        p1 = p[1:2, :]
        rowi = lax.broadcasted_iota(jnp.int32, p.shape, 0)
        dlg_ref[...] = dlb_ref[0:1, :] * p1 * (jnp.where(rowi == 1, 1.0, 0.0) - p)

    vm = pl.BlockSpec(memory_space=pltpu.VMEM)
    return pl.pallas_call(body, name="small_finish", in_specs=[vm] * 3, out_specs=[vm] * 2,
                          out_shape=[jax.ShapeDtypeStruct((HD, HD), f32), jax.ShapeDtypeStruct(p_soft.shape, f32)])(dbacc, p_soft, dlb)


def _where_am_i():
    return lax.axis_index("x"), lax.axis_index("y"), lax.axis_index("c")


def _all_gather_xy(shard):
    R, C = shard.shape
    H = R // 2

    def body(sh_ref, out_ref, send_sems, recv_sems, lsem):
        x, y, c = _where_am_i()
        j = 2 * x + y
        sib = (x, y, 1 - c)
        chips = [(1 - x, y), (x, 1 - y), (1 - x, 1 - y)]
        mine_rows = pl.ds(c * H, H)
        other_rows = pl.ds((1 - c) * H, H)

        def cp(k, src, dst, to):
            return pltpu.make_async_remote_copy(src_ref=src, dst_ref=dst, send_sem=send_sems.at[k], recv_sem=recv_sems.at[k],
                                                device_id=to, device_id_type=MESH)

        mine = pltpu.make_async_copy(sh_ref, out_ref.at[j], lsem)
        mine.start()
        first = [cp(k, sh_ref.at[mine_rows], out_ref.at[j, mine_rows], (cx, cy, c)) for k, (cx, cy) in enumerate(chips)]
        for d in first:
            d.start()
        passed = []
        for k, (cx, cy) in enumerate(chips):
            blk = out_ref.at[2 * cx + cy, mine_rows]
            cp(k, blk, blk, (cx, cy, c)).wait_recv()
            fwd = cp(3 + k, blk, blk, sib)
            fwd.start()
            passed.append(fwd)
        for k, (cx, cy) in enumerate(chips):
            blk = out_ref.at[2 * cx + cy, other_rows]
            cp(3 + k, blk, blk, sib).wait_recv()
        for d in first + passed:
            d.wait_send()
        mine.wait()

    anyspec = pl.BlockSpec(memory_space=pl.ANY)
    return pl.pallas_call(body, name="all_gather_weights", in_specs=[anyspec], out_specs=anyspec,
                          out_shape=jax.ShapeDtypeStruct((4, R, C), shard.dtype),
                          scratch_shapes=[pltpu.SemaphoreType.DMA((6,)), pltpu.SemaphoreType.DMA((6,)), pltpu.SemaphoreType.DMA],
                          compiler_params=pltpu.CompilerParams(has_side_effects=True))(shard)


def _rs_swap_halves(part):
    _, R, C = part.shape
    H = R // 2

    def body(p_ref, own_ref, got_ref, send_sem, recv_sem, lsem):
        x, y, c = _where_am_i()
        mine = pltpu.make_async_copy(p_ref.at[:, pl.ds(c * H, H)], own_ref, lsem)
        mine.start()
        d = pltpu.make_async_remote_copy(src_ref=p_ref.at[:, pl.ds((1 - c) * H, H)], dst_ref=got_ref, send_sem=send_sem,
                                         recv_sem=recv_sem, device_id=(x, y, 1 - c), device_id_type=MESH)
        d.start()
        d.wait()
        mine.wait()

    anyspec = pl.BlockSpec(memory_space=pl.ANY)
    sh = jax.ShapeDtypeStruct((4, H, C), part.dtype)
    return pl.pallas_call(body, name="rs_swap_halves", in_specs=[anyspec], out_specs=[anyspec, anyspec], out_shape=[sh, sh],
                          scratch_shapes=[pltpu.SemaphoreType.DMA, pltpu.SemaphoreType.DMA, pltpu.SemaphoreType.DMA],
                          compiler_params=pltpu.CompilerParams(has_side_effects=True))(part)


def _rs_to_owners(q32, qb):
    _, H, C = q32.shape

    def body(q_ref, qb_ref, own_ref, got_ref, send_sems, recv_sems, lsem):
        x, y, c = _where_am_i()
        j = 2 * x + y
        chips = [(1 - x, y), (x, 1 - y), (1 - x, 1 - y)]
        mine = pltpu.make_async_copy(q_ref.at[j], own_ref, lsem)
        mine.start()
        ds = [pltpu.make_async_remote_copy(src_ref=qb_ref.at[2 * cx + cy], dst_ref=got_ref.at[k], send_sem=send_sems.at[k],
                                           recv_sem=recv_sems.at[k], device_id=(cx, cy, c), device_id_type=MESH)
              for k, (cx, cy) in enumerate(chips)]
        for d in ds:
            d.start()
        for d in ds:
            d.wait()
        mine.wait()

    anyspec = pl.BlockSpec(memory_space=pl.ANY)
    return pl.pallas_call(body, name="rs_to_owners", in_specs=[anyspec, anyspec], out_specs=[anyspec, anyspec],
                          out_shape=[jax.ShapeDtypeStruct((H, C), f32), jax.ShapeDtypeStruct((3, H, C), bf16)],
                          scratch_shapes=[pltpu.SemaphoreType.DMA((3,)), pltpu.SemaphoreType.DMA((3,)), pltpu.SemaphoreType.DMA],
                          compiler_params=pltpu.CompilerParams(has_side_effects=True))(q32, qb)


def _rs_join_halves(t):
    H, C = t.shape

    def body(t_ref, out_ref, send_sem, recv_sem, lsem):
        x, y, c = _where_am_i()
        mine = pltpu.make_async_copy(t_ref, out_ref.at[pl.ds(c * H, H)], lsem)
        mine.start()
        d = pltpu.make_async_remote_copy(src_ref=t_ref, dst_ref=out_ref.at[pl.ds(c * H, H)], send_sem=send_sem,
                                         recv_sem=recv_sem, device_id=(x, y, 1 - c), device_id_type=MESH)
        d.start()
        d.wait()
        mine.wait()

    anyspec = pl.BlockSpec(memory_space=pl.ANY)
    return pl.pallas_call(body, name="rs_join_halves", in_specs=[anyspec], out_specs=anyspec,
                          out_shape=jax.ShapeDtypeStruct((2 * H, C), t.dtype),
                          scratch_shapes=[pltpu.SemaphoreType.DMA, pltpu.SemaphoreType.DMA, pltpu.SemaphoreType.DMA],
                          compiler_params=pltpu.CompilerParams(has_side_effects=True))(t)


def _all_reduce_small(g):
    R, C = g.shape
    H = R // 2

    def body(g_ref, out_ref, sib_ref, chip_ref, send_sems, recv_sems):
        x, y, c = _where_am_i()
        j = 2 * x + y
        sib = (x, y, 1 - c)
        chips = [(1 - x, y), (x, 1 - y), (1 - x, 1 - y)]
        rows = pl.ds(pl.multiple_of(c * H, SUB), H)

        def cp(k, src, dst, to):
            return pltpu.make_async_remote_copy(src_ref=src, dst_ref=dst, send_sem=send_sems.at[k], recv_sem=recv_sems.at[k],
                                                device_id=to, device_id_type=MESH)

        d0 = cp(0, g_ref, sib_ref, sib)
        d0.start()
        d0.wait()
        chip_ref[j] = g_ref[rows, :] + sib_ref[rows, :]
        ds = [cp(1 + k, chip_ref.at[j], chip_ref.at[j], (cx, cy, c)) for k, (cx, cy) in enumerate(chips)]
        for d in ds:
            d.start()
        for k, (cx, cy) in enumerate(chips):
            blk = chip_ref.at[2 * cx + cy]
            cp(1 + k, blk, blk, (cx, cy, c)).wait_recv()
        for d in ds:
            d.wait_send()
        out_ref[rows, :] = ((chip_ref[0] + chip_ref[1]) + chip_ref[2]) + chip_ref[3]
        d4 = cp(4, out_ref.at[rows], out_ref.at[rows], sib)
        d4.start()
        d4.wait_send()
        other = out_ref.at[pl.ds(pl.multiple_of((1 - c) * H, SUB), H)]
        cp(4, other, other, sib).wait_recv()

    vm = pl.BlockSpec(memory_space=pltpu.VMEM)
    return pl.pallas_call(body, name="all_reduce_small", in_specs=[vm], out_specs=vm,
                          out_shape=jax.ShapeDtypeStruct((R, C), f32),
                          scratch_shapes=[pltpu.VMEM((R, C), f32), pltpu.VMEM((4, H, C), f32),
                                          pltpu.SemaphoreType.DMA((5,)), pltpu.SemaphoreType.DMA((5,))],
                          compiler_params=pltpu.CompilerParams(has_side_effects=True, vmem_limit_bytes=VMEM_LIMIT))(g)


def _pack_flat(arrs, rows_mult):
    flat = jnp.concatenate([a.reshape(-1) for a in arrs])
    n = flat.shape[0]
    tot = -(-n // (rows_mult * LANE)) * rows_mult * LANE
    return jnp.pad(flat, (0, tot - n)).reshape(-1, LANE)


def _unpack_flat(buf, shapes):
    flat = buf.reshape(-1)
    out, o = [], 0
    for s in shapes:
        n = math.prod(s)
        out.append(flat[o:o + n].reshape(s))
        o += n
    return out


_BIG = ("mem_kv_w", "w_out", "a_w_in", "b_w_in", "c_w_in", "d_w_in")


def _pack_big_shards(sh):
    return jnp.concatenate([sh[n].reshape(-1, D_MODEL) for n in _BIG], axis=0)


def _unpack_big_full(buf):
    o = 0
    out = {}
    r = XW * 2 * XW // D_MODEL
    out["mem_kv_w"] = buf[:, o:o + r].reshape(D_MODEL, 2 * XW)
    o += r
    r = DEPTH * (D_MODEL // 4)
    out["w_out"] = buf[:, o:o + r].reshape(4, DEPTH, D_MODEL // 4, D_MODEL).transpose(1, 0, 2, 3).reshape(DEPTH, D_MODEL, D_MODEL)
    o += r
    for n, kind in (("a_w_in", 0), ("b_w_in", 1), ("c_w_in", 2), ("d_w_in", 3)):
        W = _OFFS[kind]["W"]
        r = W // 4
        out[n] = buf[:, o:o + r].reshape(4, D_MODEL, W // 4).transpose(1, 0, 2).reshape(D_MODEL, W)
        o += r
    return out


def _pack_big_full(full):
    parts = [full["mem_kv_w"].reshape(4, -1, D_MODEL),
             full["w_out"].reshape(DEPTH, 4, D_MODEL // 4, D_MODEL).transpose(1, 0, 2, 3).reshape(4, -1, D_MODEL)]
    for n, kind in (("a_w_in", 0), ("b_w_in", 1), ("c_w_in", 2), ("d_w_in", 3)):
        W = _OFFS[kind]["W"]
        parts.append(full[n].reshape(D_MODEL, 4, W // 4).transpose(1, 0, 2).reshape(4, -1, D_MODEL))
    return jnp.concatenate(parts, axis=1)


def _unpack_big_shards(buf, like):
    out, o = {}, 0
    for n in _BIG:
        r = like[n].size // D_MODEL
        out[n] = buf[o:o + r].reshape(like[n].shape)
        o += r
    return out


def _row8(v):
    v = v.reshape(-1, v.shape[-1])
    return jnp.pad(v, ((0, SUB - v.shape[0]), (0, 0)))


def kernel(x, mem, mem_kv_w, ln_g, ln_b, w_out, hgrn_lb_logits, a_w_in, a_w_s, a_b_s, b_w_in, b_norm_g, c_w_in, c_w_pool, c_scale, d_w_in, d_conv_w, d_conv_b, d_w_gx, d_b_gx, d_w_ga, d_b_ga, d_a_param, loss_target, m_mem_kv_w, m_ln_g, m_ln_b, m_w_out, m_hgrn_lb_logits, m_a_w_in, m_a_w_s, m_a_b_s, m_b_w_in, m_b_norm_g, m_c_w_in, m_c_w_pool, m_c_scale, m_d_w_in, m_d_conv_w, m_d_conv_b, m_d_w_gx, m_d_b_gx, m_d_w_ga, m_d_b_ga, m_d_a_param, v_mem_kv_w, v_ln_g, v_ln_b, v_w_out, v_hgrn_lb_logits, v_a_w_in, v_a_w_s, v_a_b_s, v_b_w_in, v_b_norm_g, v_c_w_in, v_c_w_pool, v_c_scale, v_d_w_in, v_d_conv_w, v_d_conv_b, v_d_w_gx, v_d_b_gx, v_d_w_ga, v_d_b_ga, v_d_a_param):
    names = ["mem_kv_w", "ln_g", "ln_b", "w_out", "hgrn_lb_logits", "a_w_in", "a_w_s", "a_b_s", "b_w_in", "b_norm_g", "c_w_in",
             "c_w_pool", "c_scale", "d_w_in", "d_conv_w", "d_conv_b", "d_w_gx", "d_b_gx", "d_w_ga", "d_b_ga", "d_a_param"]
    w = dict(mem_kv_w=mem_kv_w, ln_g=ln_g, ln_b=ln_b, w_out=w_out, hgrn_lb_logits=hgrn_lb_logits, a_w_in=a_w_in, a_w_s=a_w_s,
             a_b_s=a_b_s, b_w_in=b_w_in, b_norm_g=b_norm_g, c_w_in=c_w_in, c_w_pool=c_w_pool, c_scale=c_scale, d_w_in=d_w_in,
             d_conv_w=d_conv_w, d_conv_b=d_conv_b, d_w_gx=d_w_gx, d_b_gx=d_b_gx, d_w_ga=d_w_ga, d_b_ga=d_b_ga, d_a_param=d_a_param)
    m = dict(zip(names, [m_mem_kv_w, m_ln_g, m_ln_b, m_w_out, m_hgrn_lb_logits, m_a_w_in, m_a_w_s, m_a_b_s, m_b_w_in, m_b_norm_g,
                         m_c_w_in, m_c_w_pool, m_c_scale, m_d_w_in, m_d_conv_w, m_d_conv_b, m_d_w_gx, m_d_b_gx, m_d_w_ga,
                         m_d_b_ga, m_d_a_param]))
    v = dict(zip(names, [v_mem_kv_w, v_ln_g, v_ln_b, v_w_out, v_hgrn_lb_logits, v_a_w_in, v_a_w_s, v_a_b_s, v_b_w_in, v_b_norm_g,
                         v_c_w_in, v_c_w_pool, v_c_scale, v_d_w_in, v_d_conv_w, v_d_conv_b, v_d_w_gx, v_d_b_gx, v_d_w_ga,
                         v_d_b_ga, v_d_a_param]))
    xi, yi = lax.axis_index("x"), lax.axis_index("y")
    jshard = 2 * xi + yi
    x2 = x[0]
    mem2 = mem[0]
    tgt2 = loss_target[0]

    big_sh = {n: w[n] for n in _BIG}
    gathered = _all_gather_xy(_pack_big_shards(big_sh).astype(bf16))
    full = _unpack_big_full(gathered)
    w_in = [full["a_w_in"], full["b_w_in"], full["c_w_in"], full["d_w_in"]]
    w_inT = [a.T for a in w_in]
    w_outs = [full["w_out"][l] for l in range(DEPTH)]
    w_outT = [a.T for a in w_outs]

    def gather_small(shard):
        z = jnp.zeros((4, POOL_GROUP), f32)
        return lax.dynamic_update_slice(z, shard.reshape(1, POOL_GROUP), (jshard, 0))

    sm_sh = jnp.concatenate([gather_small(b_norm_g), gather_small(c_scale), gather_small(d_conv_b), gather_small(d_a_param)]
                            + [gather_small(d_conv_w[:, r]) for r in range(4)], axis=0)
    ci = lax.axis_index("c")
    sm_all = _all_reduce_small(_pack_flat([jnp.where(ci == 0, sm_sh, 0.0)], SUB * 2))
    sm = _unpack_flat(sm_all, [(8, 4 * POOL_GROUP)])[0]
    ng_full, scale_full, convb_full, ap_full = sm[0:1], sm[1:2], sm[2:3], sm[3:4]
    convw_full = sm[4:8]

    tril = jnp.tril(jnp.ones((HD, HD), bool))
    wtri = jnp.where(tril, a_w_s[0], 0.0)
    wbd = jnp.zeros((TOK, TOK), f32)
    for g in range(4):
        wbd = lax.dynamic_update_slice(wbd, c_w_pool[0, g], (g * POOL_GROUP, g * POOL_GROUP))
    kh, khT, vh, vhT, p_soft = _prep(mem2, full["mem_kv_w"], hgrn_lb_logits)
    prm = [
        dict(wtri=wtri.astype(bf16), wtriT=wtri.transpose(0, 2, 1).astype(bf16),
             bcolb=jnp.broadcast_to(a_b_s[0][:, :, None], (NH, HD, HD))),
        dict(lb=p_soft[1:2], ng=ng_full),
        dict(wbd=wbd.astype(bf16), wbdT=wbd.T.astype(bf16), scale=scale_full),
        dict(cw=_row8(convw_full), cb=convb_full, wgx=d_w_gx[0].astype(bf16), wgxT=d_w_gx[0].transpose(0, 2, 1).astype(bf16),
             bgx=d_b_gx.reshape(1, TOK), wga=d_w_ga[0].astype(bf16), wgaT=d_w_ga[0].transpose(0, 2, 1).astype(bf16),
             bga=d_b_ga.reshape(1, TOK), ap=ap_full),
    ]

    acts = []
    h = x2
    for l in range(DEPTH):
        outs = _fwd_layer(l, h, w_in[l], w_outs[l], ln_g[l:l + 1], ln_b[l:l + 1], khT, vh, prm[l],
                          tgt2 if l == DEPTH - 1 else None)
        nfix = 4 if l == DEPTH - 1 else 3
        acts.append(dict(xin=h, proj=outs[1], z=outs[2], saves=outs[nfix:]))
        if l == DEPTH - 1:
            loss_part = outs[3]
        h = outs[0]
    loss = lax.psum(0.5 / D_MODEL * jnp.sum(loss_part), ("x", "y", "c"))

    dh = h
    gfull = {}
    gw_out = [None] * DEPTH
    dln = [None] * DEPTH
    dks, dvs = [None] * DEPTH, [None] * DEPTH
    sgr = [None] * DEPTH
    for l in reversed(range(DEPTH)):
        a = acts[l]
        (dxin, dproj, mixedb, dyb, dln[l], dks[l], dvs[l]), sgr[l] = _bwd_layer(
            l, dh, a["z"], a["proj"], w_inT[l], w_outT[l], ln_g[l:l + 1], kh, khT, vh, vhT, prm[l], a["saves"])
        gfull["abcd"[l] + "_w_in"] = _tn_gemm(a["xin"], dproj, f"grad_w_in{l}", 2)
        gw_out[l] = _tn_gemm(mixedb, dyb, f"grad_w_out{l}", 1)
        dh = dxin
    grad_x = dh[None]
    gfull["w_out"] = jnp.stack(gw_out)
    gfull["mem_kv_w"] = _kv_bwd(mem2, dks, dvs)

    own, got = _rs_swap_halves(_pack_big_full(gfull))
    H = own.shape[1]
    q32, qb = _add2(own.reshape(4 * H, D_MODEL), got.reshape(4 * H, D_MODEL), "rs_add_sibling")
    own2, got2 = _rs_to_owners(q32.reshape(4, H, D_MODEL), qb.reshape(4, H, D_MODEL))
    gbig = _rs_join_halves(_add4(own2, got2, "rs_add_chips"))
    d_big, m_big, v_big = _adamw(_pack_big_shards(big_sh), gbig, _pack_big_shards({n: m[n] for n in _BIG}),
                                 _pack_big_shards({n: v[n] for n in _BIG}), "adamw_big")
    g_sh = _unpack_big_shards(gbig, big_sh)
    d_sh = _unpack_big_shards(d_big, big_sh)
    m_sh = _unpack_big_shards(m_big, big_sh)
    v_sh = _unpack_big_shards(v_big, big_sh)

    dbs, dlogits = _small_finish(sgr[0]["dbacc"], p_soft, sgr[1]["dlb"])
    gs = {
        "ln_g": jnp.concatenate([dln[l][0:1] for l in range(DEPTH)], axis=0),
        "ln_b": jnp.concatenate([dln[l][1:2] for l in range(DEPTH)], axis=0),
        "hgrn_lb_logits": dlogits,
        "a_w_s": sgr[0]["dwtri"][None],
        "a_b_s": dbs[:, 0:NH].T[None],
        "b_norm_g": sgr[1]["dng"][0:1],
        "c_w_pool": jnp.stack([sgr[2]["dwbd"][g * POOL_GROUP:(g + 1) * POOL_GROUP, g * POOL_GROUP:(g + 1) * POOL_GROUP]
                               for g in range(4)])[None],
        "c_scale": sgr[2]["dscale"][0:1],
        "d_conv_w": sgr[3]["dcw"][0:4][None],
        "d_conv_b": sgr[3]["dvec"][3:4],
        "d_w_gx": sgr[3]["dwgx"][None],
        "d_b_gx": sgr[3]["dvec"][1:2].reshape(1, NH, HD),
        "d_w_ga": sgr[3]["dwga"][None],
        "d_b_ga": sgr[3]["dvec"][2:3].reshape(1, NH, HD),
        "d_a_param": sgr[3]["dvec"][0:1],
    }
    small = [n for n in names if n not in _BIG]
    full_shapes = [gs[n].shape for n in small]
    gsum = dict(zip(small, _unpack_flat(_all_reduce_small(_pack_flat([gs[n] for n in small], SUB * 2)), full_shapes)))
    for n in ("b_norm_g", "c_scale", "d_conv_b", "d_a_param"):
        gsum[n] = lax.dynamic_slice(gsum[n], (0, jshard * POOL_GROUP), (1, POOL_GROUP))
    gsum["d_conv_w"] = lax.dynamic_slice(gsum["d_conv_w"], (0, 0, jshard * POOL_GROUP), (1, 4, POOL_GROUP))
    shapes = [w[n].shape for n in small]
    pk = lambda d: _pack_flat([d[n] for n in small], SUB)
    d_s, m_s, v_s = _adamw(pk(w), pk(gsum), pk(m), pk(v), "adamw_small")
    d_sm = dict(zip(small, _unpack_flat(d_s, shapes)))
    m_sm = dict(zip(small, _unpack_flat(m_s, shapes)))
    v_sm = dict(zip(small, _unpack_flat(v_s, shapes)))

    grads = {**gsum, **g_sh}
    deltas = {**d_sm, **d_sh}
    new_m = {**m_sm, **m_sh}
    new_v = {**v_sm, **v_sh}
    return (loss, grad_x, *[grads[n] for n in names], *[deltas[n] for n in names], *[new_m[n] for n in names],
            *[new_v[n] for n in names])
```

```python
import functools
import math

import jax
import jax.numpy as jnp
from jax import lax
from jax.experimental import pallas as pl
from jax.experimental.pallas import tpu as pltpu

f32 = jnp.float32
bf16 = jnp.bfloat16
MM = bf16

D_MODEL = 1024
TOK = 768
XW = 256
XHEADS = 4
XDIM = 64
HD = 128
NH = TOK // HD
CHUNK = 16
POOL_GROUP = 192
DEPTH = 4
ALPHA = (2 * DEPTH) ** 0.25
LN_EPS = 1e-5
RMS_EPS = 1e-6
LRU_C = 8.0
ADAM_LR, ADAM_B1, ADAM_B2, ADAM_EPS, ADAM_WD, ADAM_STEP = 0.001, 0.9, 0.999, 1e-08, 0.01, 10

_TS = (256, 128, 256, 256)
TK = 512
SUB = 8
LANE = 128
VMEM_LIMIT = 58 * 1024 * 1024

_OFFS = (
    dict(u=0, v=768, qx=1536, gate=1792, W=2816),
    dict(q=0, f=768, i=1536, qx=2304, gate=2560, W=3584),
    dict(p=0, qx=768, gate=1024, W=2048),
    dict(xb=0, qx=768, gate=1024, W=2048),
)
_PRM = (
    ("wtri", "wtriT", "bcolb"),
    ("lb", "ng"),
    ("wbd", "wbdT", "scale"),
    ("cw", "cb", "wgx", "wgxT", "bgx", "wga", "wgaT", "bga", "ap"),
)
MESH = pl.DeviceIdType.MESH


def _mm(a, b):
    return jnp.dot(a.astype(MM), b.astype(MM), preferred_element_type=f32)


def _mm_nt(a, b):
    return lax.dot_general(a.astype(MM), b.astype(MM), (((1,), (1,)), ((), ())), preferred_element_type=f32)


def _mm_tn(a, b):
    return lax.dot_general(a.astype(MM), b.astype(MM), (((0,), (0,)), ((), ())), preferred_element_type=f32)


def _mm_sel(sel, b):
    s = sel.astype(bf16)
    hi = b.astype(bf16)
    lo = (b - hi.astype(f32)).astype(bf16)
    return jnp.dot(s, hi, preferred_element_type=f32) + jnp.dot(s, lo, preferred_element_type=f32)


def _sig(x):
    return jax.nn.sigmoid(x)


_GC = math.sqrt(2.0 / math.pi)


def _gelu(x):
    t = jnp.tanh(_GC * (x + 0.044715 * x * x * x))
    return 0.5 * x * (1.0 + t), t


def _gelu_grad(x, t):
    return 0.5 * (1.0 + t) + 0.5 * x * (1.0 - t * t) * _GC * (1.0 + 3.0 * 0.044715 * x * x)


def _rowsum(x):
    return jnp.sum(x, axis=0, keepdims=True)


def _lmean(x):
    return jnp.mean(x, axis=-1, keepdims=True)


def _ln(z):
    mu = _lmean(z)
    zc = z - mu
    rstd = lax.rsqrt(_lmean(zc * zc) + LN_EPS)
    return zc * rstd, rstd


def _ln_bwd(dxh, xhat, rstd):
    return rstd * (dxh - _lmean(dxh) - xhat * _lmean(dxh * xhat))


def _hs(h):
    return slice(h * HD, (h + 1) * HD)


def _expm1(x):
    small = x * (1.0 + x * 0.5 * (1.0 + x * (1.0 / 3.0) * (1.0 + x * 0.25 * (1.0 + x * 0.2 * (1.0 + x * (1.0 / 6.0))))))
    return jnp.where(jnp.abs(x) < 0.25, small, jnp.exp(x) - 1.0)


def _softplus(x):
    e = jnp.exp(-jnp.abs(x))
    l1p = jnp.where(e < 1e-4, e - 0.5 * e * e, jnp.log(1.0 + e))
    return jnp.maximum(x, 0.0) + l1p


def _scan_fwd(a, b):
    n = a.shape[0]
    row = lax.broadcasted_iota(jnp.int32, a.shape, 0)
    d = 1
    while d < n:
        m = row >= d
        b = jnp.where(m, a * pltpu.roll(b, d, 0) + b, b)
        a = jnp.where(m, a * pltpu.roll(a, d, 0), a)
        d *= 2
    return a, b


def _scan_bwd(a, b):
    n = a.shape[0]
    row = lax.broadcasted_iota(jnp.int32, a.shape, 0)
    d = 1
    while d < n:
        m = row < n - d
        b = jnp.where(m, a * pltpu.roll(b, n - d, 0) + b, b)
        a = jnp.where(m, a * pltpu.roll(a, n - d, 0), a)
        d *= 2
    return a, b


def _chunk_mats(n):
    r = lax.broadcasted_iota(jnp.int32, (n, n), 0)
    c = lax.broadcasted_iota(jnp.int32, (n, n), 1)
    same = (r // CHUNK) == (c // CHUNK)
    return same, jnp.logical_and(same, c <= r)


def _pool_w(shape):
    lane = lax.broadcasted_iota(jnp.int32, shape, 1)
    return jnp.where(lane < POOL_GROUP, 2, jnp.where(lane < 2 * POOL_GROUP, 4, jnp.where(lane < 3 * POOL_GROUP, 8, 16)))


def _pool_pick(r1, r2, r3, r4):
    lane = lax.broadcasted_iota(jnp.int32, r1.shape, 1)
    return jnp.where(lane < POOL_GROUP, r1, jnp.where(lane < 2 * POOL_GROUP, r2, jnp.where(lane < 3 * POOL_GROUP, r3, r4)))


def _const_spec(a):
    nd = a.ndim
    return pl.BlockSpec(a.shape, lambda i, _nd=nd: (0,) * _nd, pipeline_mode=pl.Buffered(1))


def _acc_spec(shape):
    nd = len(shape)
    return pl.BlockSpec(shape, lambda i, _nd=nd: (0,) * _nd)


def _params(sem="arbitrary"):
    return pltpu.CompilerParams(dimension_semantics=(sem,), vmem_limit_bytes=VMEM_LIMIT)


def _xattn_fwd(qx, khT_ref, vh_ref):
    xo = jnp.zeros((qx.shape[0], XW), f32)
    ps = []
    for h in range(XHEADS):
        s = _mm(qx, khT_ref[h]) * (XDIM ** -0.5)
        e = jnp.exp(s - jnp.max(s, axis=-1, keepdims=True))
        p = e / jnp.sum(e, axis=-1, keepdims=True)
        xo = xo + _mm(p, vh_ref[h])
        ps.append(p)
    return xo, ps


def _hgrn_parallel(q_raw, fl, lb):
    n = q_raw.shape[0]
    same, tri = _chunk_mats(n)
    sq = _sig(q_raw)
    qf = q_raw * sq
    sgm = _sig(fl)
    f = lb + (1.0 - lb) * sgm
    logf = jnp.log(f)
    k = 1.0 - f
    g = _mm_sel(tri, logf)
    gl = _mm_sel(same, logf)
    eg = jnp.exp(g)
    eng = jnp.exp(-g)
    ee = jnp.exp(gl - g)
    return dict(sq=sq, qf=qf, sgm=sgm, f=f, k=k, eg=eg, eng=eng, ee=ee, q_dec=qf * eg, k_inv=k * eng, k_end=k * ee,
                a=jnp.exp(gl))


def _hgrn_intra(q_dec, k_inv, v):
    n = q_dec.shape[0]
    _, tri = _chunk_mats(HD)
    outs = []
    for h in range(NH):
        blks = []
        for b in range(n // HD):
            rs = slice(b * HD, (b + 1) * HD)
            sc = jnp.where(tri, _mm_nt(q_dec[rs, _hs(h)], k_inv[rs, _hs(h)]), 0.0)
            blks.append(_mm(sc, v[rs, _hs(h)]))
        outs.append(jnp.concatenate(blks, axis=0))
    return jnp.concatenate(outs, axis=-1)


def _cs(c):
    return slice(c * CHUNK, (c + 1) * CHUNK)


def _hgrn_inter_fwd(qdec_s, kend_s, v_s, a_s, oint_s, st_ref, states_s, u_s):
    n = qdec_s.shape[0] // CHUNK
    for c in range(n):
        for h in range(NH):
            u_s[c, h] = _mm_tn(v_s[_cs(c), _hs(h)], kend_s[_cs(c), _hs(h)])
    for h in range(NH):
        st = st_ref[h]
        for c in range(n):
            states_s[c, h] = st
            st = st * a_s[c * CHUNK:c * CHUNK + 1, _hs(h)] + u_s[c, h]
        st_ref[h] = st
    for c in range(n):
        for h in range(NH):
            oint_s[_cs(c), _hs(h)] = _mm_nt(qdec_s[_cs(c), _hs(h)], states_s[c, h])


def _rms(o):
    outs, rs = [], []
    for h in range(NH):
        oh = o[:, _hs(h)]
        r = lax.rsqrt(_lmean(oh * oh) + RMS_EPS)
        outs.append(oh * r)
        rs.append(r)
    return jnp.concatenate(outs, axis=-1), rs


def _gmlp_core(u_raw, v_raw, wtri_ref, bcolb_ref):
    gu, tu = _gelu(u_raw)
    gv, tv = _gelu(v_raw)
    vns, rstds, mixeds = [], [], []
    for h in range(NH):
        vn, rstd = _ln(gv[:, _hs(h)])
        blks = []
        for n in range(u_raw.shape[0] // HD):
            blks.append(_mm(wtri_ref[h], vn[n * HD:(n + 1) * HD]) + bcolb_ref[h])
        vns.append(vn)
        rstds.append(rstd)
        mixeds.append(jnp.concatenate(blks, axis=0))
    mixed = jnp.concatenate(mixeds, axis=-1)
    return gu, tu, tv, vns, rstds, mixed


def _pool_core(p, carry, row0, wbd_ref):
    ext = jnp.concatenate([carry, p], axis=0)
    r1 = ext + pltpu.roll(ext, 1, 0)
    r2 = r1 + pltpu.roll(r1, 2, 0)
    r3 = r2 + pltpu.roll(r2, 4, 0)
    r4 = r3 + pltpu.roll(r3, 8, 0)
    sel = _pool_pick(r1, r2, r3, r4)[2 * SUB:]
    grow = row0 + lax.broadcasted_iota(jnp.int32, p.shape, 0)
    inv_cnt = 1.0 / jnp.minimum(grow + 1, _pool_w(p.shape)).astype(f32)
    diff = sel * inv_cnt - p
    return diff, inv_cnt, _mm(diff, wbd_ref[...])


def _lru_core(xb, ccar, row0, p):
    ext = jnp.concatenate([ccar, xb], axis=0)
    cw = p["cw"]
    x1, x2, x3 = pltpu.roll(ext, 1, 0)[SUB:], pltpu.roll(ext, 2, 0)[SUB:], pltpu.roll(ext, 3, 0)[SUB:]
    xc = cw[3:4, :] * xb + cw[2:3, :] * x1 + cw[1:2, :] * x2 + cw[0:1, :] * x3 + p["cb"][...]
    gxs, gas = [], []
    for h in range(NH):
        gxs.append(_mm(xc[:, _hs(h)], p["wgx"][h]))
        gas.append(_mm(xc[:, _hs(h)], p["wga"][h]))
    gx = _sig(jnp.concatenate(gxs, axis=-1) + p["bgx"][...])
    ga = _sig(jnp.concatenate(gas, axis=-1) + p["bga"][...])
    sp = _softplus(-p["ap"][...])
    la = -LRU_C * ga * sp
    a = jnp.exp(la)
    grow = row0 + lax.broadcasted_iota(jnp.int32, xb.shape, 0)
    first = grow == 0
    mult = jnp.where(first, 1.0, jnp.sqrt(-_expm1(2.0 * la)))
    bt = mult * gx * xc
    return dict(x1=x1, x2=x2, x3=x3, xc=xc, gx=gx, ga=ga, sp=sp, a=a, mult=mult, bt=bt, first=first)


def _fwd_layer(kind, xin, w_in, w_out, lng, lnb, khT, vh, prm, tgt):
    S = xin.shape[0]
    TS = _TS[kind]
    nt = S // TS
    off = _OFFS[kind]
    W = off["W"]
    last = tgt is not None
    pnames = _PRM[kind]
    pvals = [prm[n] for n in pnames]

    def body(*refs):
        it = iter(refs)
        xin_ref, win_ref, wout_ref, lng_ref, lnb_ref, khT_ref, vh_ref = (next(it) for _ in range(7))
        p = {n: next(it) for n in pnames}
        tgt_ref = next(it) if last else None
        xout_ref, proj_ref, z_ref = next(it), next(it), next(it)
        loss_ref = next(it) if last else None
        rest = list(it)
        i = pl.program_id(0)
        x = xin_ref[...]
        proj_ref[...] = _mm(x, win_ref[...])

        if kind == 0:
            gu, _, _, _, _, mixed = _gmlp_core(proj_ref[:, 0:TOK], proj_ref[:, TOK:2 * TOK], p["wtri"], p["bcolb"])
            tok = gu * mixed
        elif kind == 1:
            st_save, st_ref, states_s, u_s, qdec_s, kend_s, v_s, a_s, oint_s = rest

            @pl.when(i == 0)
            def _():
                st_ref[...] = jnp.zeros_like(st_ref)

            st_save[0] = st_ref[...]
            v = proj_ref[:, 2 * TOK:3 * TOK]
            hp = _hgrn_parallel(proj_ref[:, 0:TOK], proj_ref[:, TOK:2 * TOK], p["lb"][...])
            qdec_s[...] = hp["q_dec"]
            kend_s[...] = hp["k_end"]
            v_s[...] = v
            a_s[...] = hp["a"]
            o_intra = _hgrn_intra(hp["q_dec"], hp["k_inv"], v)
            _hgrn_inter_fwd(qdec_s, kend_s, v_s, a_s, oint_s, st_ref, states_s, u_s)
            on, _ = _rms(o_intra + oint_s[...])
            tok = on * p["ng"][...]
        elif kind == 2:
            pc_save, pcar = rest

            @pl.when(i == 0)
            def _():
                pcar[...] = jnp.zeros_like(pcar)

            pc_save[0] = pcar[...]
            pp = proj_ref[:, 0:TOK]
            _, _, y = _pool_core(pp, pcar[...], i * TS, p["wbd"])
            pcar[...] = pp[TS - 2 * SUB:, :]
            tok = y * p["scale"][...]
        else:
            cc_save, hc_save, ccar, hcar = rest

            @pl.when(i == 0)
            def _():
                ccar[...] = jnp.zeros_like(ccar)
                hcar[...] = jnp.zeros_like(hcar)

            cc_save[0] = ccar[...]
            hc_save[0] = hcar[...]
            xb = proj_ref[:, 0:TOK]
            lc = _lru_core(xb, ccar[...], i * TS, p)
            P, B = _scan_fwd(lc["a"], lc["bt"])
            tok = P * hcar[SUB - 1:SUB, :] + B
            ccar[...] = xb[TS - SUB:, :]
            hcar[...] = tok[TS - SUB:, :]

        xo, _ = _xattn_fwd(proj_ref[:, off["qx"]:off["qx"] + XW], khT_ref, vh_ref)
        gate = proj_ref[:, off["gate"]:off["gate"] + D_MODEL]
        mixed = jnp.concatenate([tok, xo], axis=-1) * (gate * _sig(gate))
        z = ALPHA * x + _mm(mixed, wout_ref[...])
        z_ref[...] = z
        xhat, _ = _ln(z)
        xout = xhat * lng_ref[...] + lnb_ref[...]
        if last:
            e = xout - tgt_ref[...]
            xout_ref[...] = e * (1.0 / D_MODEL)
            es = _rowsum(e * e)
            tot = es[:, 0:LANE]
            for j in range(1, D_MODEL // LANE):
                tot = tot + es[:, j * LANE:(j + 1) * LANE]

            @pl.when(i == 0)
            def _():
                loss_ref[...] = jnp.zeros_like(loss_ref)

            loss_ref[0:1, :] += tot
        else:
            xout_ref[...] = xout

    tile = lambda w: pl.BlockSpec((TS, w), lambda i: (i, 0))
    in_arrays = [xin, w_in, w_out, lng, lnb, khT, vh] + pvals + ([tgt] if last else [])
    in_specs = [tile(D_MODEL)] + [_const_spec(a) for a in in_arrays[1:7 + len(pvals)]] + ([tile(D_MODEL)] if last else [])
    out_shape = [jax.ShapeDtypeStruct((S, D_MODEL), f32), jax.ShapeDtypeStruct((S, W), f32), jax.ShapeDtypeStruct((S, D_MODEL), f32)]
    out_specs = [tile(D_MODEL), tile(W), tile(D_MODEL)]
    if last:
        out_shape.append(jax.ShapeDtypeStruct((SUB, LANE), f32))
        out_specs.append(_acc_spec((SUB, LANE)))
    scratch = []
    save = lambda *s: (jax.ShapeDtypeStruct((nt,) + s, f32), pl.BlockSpec((1,) + s, lambda i, _n=len(s): (i,) + (0,) * _n))
    if kind == 1:
        sh, sp = save(NH, HD, HD)
        out_shape.append(sh)
        out_specs.append(sp)
        scratch = ([pltpu.VMEM((NH, HD, HD), f32)] + [pltpu.VMEM((TS // CHUNK, NH, HD, HD), f32)] * 2
                   + [pltpu.VMEM((TS, TOK), f32)] * 5)
    elif kind == 2:
        sh, sp = save(2 * SUB, TOK)
        out_shape.append(sh)
        out_specs.append(sp)
        scratch = [pltpu.VMEM((2 * SUB, TOK), f32)]
    elif kind == 3:
        for _ in range(2):
            sh, sp = save(SUB, TOK)
            out_shape.append(sh)
            out_specs.append(sp)
        scratch = [pltpu.VMEM((SUB, TOK), f32)] * 2
    return pl.pallas_call(body, name=f"fwd_layer{kind}", grid=(nt,), in_specs=in_specs, out_specs=out_specs,
                          out_shape=out_shape, scratch_shapes=scratch, compiler_params=_params())(*in_arrays)


def _small_grad_shapes(kind):
    if kind == 0:
        return dict(dwtri=(NH, HD, HD), dbacc=(NH, HD, HD))
    if kind == 1:
        return dict(dlb=(SUB, TOK), dng=(SUB, TOK))
    if kind == 2:
        return dict(dwbd=(TOK, TOK), dscale=(SUB, TOK))
    return dict(dcw=(SUB, TOK), dvec=(SUB, TOK), dwgx=(NH, HD, HD), dwga=(NH, HD, HD))


def _bwd_layer(kind, dxout, z, proj, w_inT, w_outT, lng, kh, khT, vh, vhT, prm, saves):
    S = dxout.shape[0]
    TS = _TS[kind]
    nt = S // TS
    off = _OFFS[kind]
    W = off["W"]
    pnames = _PRM[kind]
    pvals = [prm[n] for n in pnames]
    sg_shapes = _small_grad_shapes(kind)
    sg_names = list(sg_shapes)
    n_saves = len(saves)

    def body(*refs):
        it = iter(refs)
        dxo_ref, z_ref, proj_ref, winT_ref, woutT_ref, lng_ref, kh_ref, khT_ref, vh_ref, vhT_ref = (next(it) for _ in range(10))
        p = {n: next(it) for n in pnames}
        sv = [next(it) for _ in range(n_saves)]
        dxin_ref, dproj_ref, mixed_ref, dy_ref, dln_ref, dk_ref, dv_ref = (next(it) for _ in range(7))
        sg = {n: next(it) for n in sg_names}
        rest = list(it)
        step = pl.program_id(0)
        i = nt - 1 - step

        @pl.when(step == 0)
        def _():
            dln_ref[...] = jnp.zeros_like(dln_ref)
            dk_ref[...] = jnp.zeros_like(dk_ref)
            dv_ref[...] = jnp.zeros_like(dv_ref)
            for n in sg_names:
                sg[n][...] = jnp.zeros_like(sg[n])

        dxo = dxo_ref[...]
        xhat, rstd = _ln(z_ref[...])
        dln_ref[0:1, :] += _rowsum(dxo * xhat)
        dln_ref[1:2, :] += _rowsum(dxo)
        dz = _ln_bwd(dxo * lng_ref[...], xhat, rstd)
        dyb = dz.astype(bf16)
        dy_ref[...] = dyb
        dmixed = _mm(dyb, woutT_ref[...])

        aux = {}
        if kind == 0:
            u_raw, v_raw = proj_ref[:, 0:TOK], proj_ref[:, TOK:2 * TOK]
            gu, tu, tv, vns, rstds, mx = _gmlp_core(u_raw, v_raw, p["wtri"], p["bcolb"])
            tok = gu * mx
        elif kind == 1:
            st_save, = sv
            (dst_ref, fst_ref, states_s, dsts_s, u_s, qdec_s, kend_s, v_s, a_s, oint_s, do_s, dqdec_s, dkend_s, dv_s,
             dgl_s) = rest

            @pl.when(step == 0)
            def _():
                dst_ref[...] = jnp.zeros_like(dst_ref)

            fst_ref[...] = st_save[0]
            v = proj_ref[:, 2 * TOK:3 * TOK]
            hp = _hgrn_parallel(proj_ref[:, 0:TOK], proj_ref[:, TOK:2 * TOK], p["lb"][...])
            qdec_s[...] = hp["q_dec"]
            kend_s[...] = hp["k_end"]
            v_s[...] = v
            a_s[...] = hp["a"]
            o_intra = _hgrn_intra(hp["q_dec"], hp["k_inv"], v)
            _hgrn_inter_fwd(qdec_s, kend_s, v_s, a_s, oint_s, fst_ref, states_s, u_s)
            o = o_intra + oint_s[...]
            on, rs = _rms(o)
            tok = on * p["ng"][...]
            aux = dict(hp=hp, v=v, o=o, on=on, rs=rs)
        elif kind == 2:
            pc_save, = sv
            dpcar, = rest
            pp = proj_ref[:, 0:TOK]
            diff, inv_cnt, y = _pool_core(pp, pc_save[0], i * TS, p["wbd"])
            tok = y * p["scale"][...]
        else:
            cc_save, hc_save = sv
            dccar, gcar = rest
            xb = proj_ref[:, 0:TOK]
            lc = _lru_core(xb, cc_save[0], i * TS, p)
            P, B = _scan_fwd(lc["a"], lc["bt"])
            hin = hc_save[0, SUB - 1:SUB, :]
            tok = P * hin + B

        xo, ps = _xattn_fwd(proj_ref[:, off["qx"]:off["qx"] + XW], khT_ref, vh_ref)
        gate = proj_ref[:, off["gate"]:off["gate"] + D_MODEL]
        sgm = _sig(gate)
        sgate = gate * sgm
        cat = jnp.concatenate([tok, xo], axis=-1)
        mixed_ref[...] = (cat * sgate).astype(bf16)
        dcat = dmixed * sgate
        dproj_ref[:, off["gate"]:off["gate"] + D_MODEL] = (dmixed * cat * (sgm * (1.0 + gate * (1.0 - sgm)))).astype(bf16)
        dtok = dcat[:, 0:TOK]
        dxo_att = dcat[:, TOK:]

        qx = proj_ref[:, off["qx"]:off["qx"] + XW]
        dqx = jnp.zeros((TS, XW), f32)
        for h in range(XHEADS):
            dp = _mm(dxo_att, vhT_ref[h])
            ds = ps[h] * (dp - jnp.sum(dp * ps[h], axis=-1, keepdims=True)) * (XDIM ** -0.5)
            dqx = dqx + _mm(ds, kh_ref[h])
            dk_ref[h] += _mm_tn(ds, qx)
            dv_ref[h] += _mm_tn(ps[h], dxo_att)
        dproj_ref[:, off["qx"]:off["qx"] + XW] = dqx.astype(bf16)

        if kind == 0:
            tril = lax.broadcasted_iota(jnp.int32, (HD, HD), 1) <= lax.broadcasted_iota(jnp.int32, (HD, HD), 0)
            dgu = dtok * mx
            dmx = dtok * gu
            dgvs = []
            for h in range(NH):
                dmh = dmx[:, _hs(h)]
                blks = []
                for n in range(TS // HD):
                    rs_ = slice(n * HD, (n + 1) * HD)
                    blks.append(_mm(p["wtriT"][h], dmh[rs_]))
                    sg["dwtri"][h] += jnp.where(tril, _mm_nt(dmh[rs_], vns[h][rs_]), 0.0)
                    sg["dbacc"][h] += dmh[rs_]
                dgvs.append(_ln_bwd(jnp.concatenate(blks, axis=0), vns[h], rstds[h]))
            dgv = jnp.concatenate(dgvs, axis=-1)
            dproj_ref[:, 0:TOK] = (dgu * _gelu_grad(u_raw, tu)).astype(bf16)
            dproj_ref[:, TOK:2 * TOK] = (dgv * _gelu_grad(v_raw, tv)).astype(bf16)
        elif kind == 1:
            hp, v, o, on, rs = aux["hp"], aux["v"], aux["o"], aux["on"], aux["rs"]
            ng = p["ng"][...]
            sg["dng"][0:1, :] += _rowsum(dtok * on)
            dn = dtok * ng
            dos = []
            for h in range(NH):
                oh, r = o[:, _hs(h)], rs[h]
                dos.append(r * (dn[:, _hs(h)] - oh * (r * r) * _lmean(dn[:, _hs(h)] * oh)))
            do = jnp.concatenate(dos, axis=-1)
            do_s[...] = do
            _, tri = _chunk_mats(HD)
            dqd, dki, dvi = [], [], []
            for h in range(NH):
                bq, bk, bv = [], [], []
                for b in range(TS // HD):
                    rs_ = slice(b * HD, (b + 1) * HD)
                    qd, ki = hp["q_dec"][rs_, _hs(h)], hp["k_inv"][rs_, _hs(h)]
                    sc = jnp.where(tri, _mm_nt(qd, ki), 0.0)
                    dsc = jnp.where(tri, _mm_nt(do[rs_, _hs(h)], v[rs_, _hs(h)]), 0.0)
                    bv.append(_mm_tn(sc, do[rs_, _hs(h)]))
                    bq.append(_mm(dsc, ki))
                    bk.append(_mm_tn(dsc, qd))
                dqd.append(jnp.concatenate(bq, axis=0))
                dki.append(jnp.concatenate(bk, axis=0))
                dvi.append(jnp.concatenate(bv, axis=0))
            dqdec_s[...] = jnp.concatenate(dqd, axis=-1)
            dk_inv = jnp.concatenate(dki, axis=-1)
            dv_s[...] = jnp.concatenate(dvi, axis=-1)
            row16 = lax.broadcasted_iota(jnp.int32, (CHUNK, HD), 0)

            nch = TS // CHUNK
            for c in range(nch):
                for h in range(NH):
                    u_s[c, h] = _mm_tn(do_s[_cs(c), _hs(h)], qdec_s[_cs(c), _hs(h)])
            for h in range(NH):
                dst = dst_ref[h]
                for c in reversed(range(nch)):
                    dsts_s[c, h] = dst
                    dst = dst * a_s[c * CHUNK:c * CHUNK + 1, _hs(h)] + u_s[c, h]
                dst_ref[h] = dst
            for c in range(nch):
                for h in range(NH):
                    stp = states_s[c, h]
                    dst = dsts_s[c, h]
                    dqdec_s[_cs(c), _hs(h)] += _mm(do_s[_cs(c), _hs(h)], stp)
                    dkend_s[_cs(c), _hs(h)] = _mm(v_s[_cs(c), _hs(h)], dst)
                    dv_s[_cs(c), _hs(h)] += _mm_nt(kend_s[_cs(c), _hs(h)], dst)
                    da = jnp.sum(dst * stp, axis=0, keepdims=True) * a_s[c * CHUNK:c * CHUNK + 1, _hs(h)]
                    dgl_s[_cs(c), _hs(h)] = jnp.where(row16 == 0, jnp.broadcast_to(da, (CHUNK, HD)), 0.0)
            dq_dec = dqdec_s[...]
            dk_end = dkend_s[...]
            same, _ = _chunk_mats(TS)
            triT = jnp.logical_and(same, lax.broadcasted_iota(jnp.int32, (TS, TS), 1) >= lax.broadcasted_iota(jnp.int32, (TS, TS), 0))
            dg = dq_dec * hp["q_dec"] - dk_inv * hp["k_inv"] - dk_end * hp["k_end"]
            dk = dk_inv * hp["eng"] + dk_end * hp["ee"]
            dglr = dk_end * hp["k_end"] + dgl_s[...]
            dlogf = _mm_sel(triT, dg) + _mm_sel(same, dglr)
            df = dlogf / hp["f"] - dk
            lb = p["lb"][...]
            sg["dlb"][0:1, :] += _rowsum(df * (1.0 - hp["sgm"]))
            q_raw = proj_ref[:, 0:TOK]
            dproj_ref[:, 0:TOK] = (dq_dec * hp["eg"] * (hp["sq"] * (1.0 + q_raw * (1.0 - hp["sq"])))).astype(bf16)
            dproj_ref[:, TOK:2 * TOK] = (df * (1.0 - lb) * hp["sgm"] * (1.0 - hp["sgm"])).astype(bf16)
            dproj_ref[:, 2 * TOK:3 * TOK] = dv_s[...].astype(bf16)
        elif kind == 2:
            @pl.when(step == 0)
            def _():
                dpcar[...] = jnp.zeros_like(dpcar)

            sg["dscale"][0:1, :] += _rowsum(dtok * y)
            dyp = dtok * p["scale"][...]
            sg["dwbd"][...] += _mm_tn(diff, dyp)
            ddiff = _mm(dyp, p["wbdT"][...])
            q = ddiff * inv_cnt
            ext = jnp.concatenate([q, dpcar[...]], axis=0)
            n = TS + 2 * SUB
            r1 = ext + pltpu.roll(ext, n - 1, 0)
            r2 = r1 + pltpu.roll(r1, n - 2, 0)
            r3 = r2 + pltpu.roll(r2, n - 4, 0)
            r4 = r3 + pltpu.roll(r3, n - 8, 0)
            dproj_ref[:, 0:TOK] = (_pool_pick(r1, r2, r3, r4)[:TS] - ddiff).astype(bf16)
            dpcar[...] = q[0:2 * SUB, :]
        else:
            @pl.when(step == 0)
            def _():
                dccar[...] = jnp.zeros_like(dccar)
                gcar[...] = jnp.zeros_like(gcar)

            a, mult, gx, ga, xc = lc["a"], lc["mult"], lc["gx"], lc["ga"], lc["xc"]
            row = lax.broadcasted_iota(jnp.int32, (TS, TOK), 0)
            an = jnp.where(row == TS - 1, 1.0, pltpu.roll(a, TS - 1, 0))
            Pb, Bb = _scan_bwd(an, dtok)
            lam = Pb * gcar[0:1, :] + Bb
            gcar[...] = (a * lam)[0:SUB, :]
            hprev = jnp.where(row == 0, jnp.broadcast_to(hin, (TS, TOK)), pltpu.roll(tok, 1, 0))
            dmult = lam * gx * xc
            dgx = lam * mult * xc
            dxc = lam * mult * gx
            dla = lam * hprev * a - jnp.where(lc["first"], 0.0, dmult * a * a / mult)
            sp = lc["sp"]
            dga = -LRU_C * sp * dla
            dsp = _rowsum(-LRU_C * ga * dla)
            sg["dvec"][0:1, :] += dsp * (-_sig(-p["ap"][...]))
            dpx = dgx * gx * (1.0 - gx)
            dpa = dga * ga * (1.0 - ga)
            sg["dvec"][1:2, :] += _rowsum(dpx)
            sg["dvec"][2:3, :] += _rowsum(dpa)
            dxcs = []
            for h in range(NH):
                dxcs.append(_mm(dpx[:, _hs(h)], p["wgxT"][h]) + _mm(dpa[:, _hs(h)], p["wgaT"][h]))
                sg["dwgx"][h] += _mm_tn(xc[:, _hs(h)], dpx[:, _hs(h)])
                sg["dwga"][h] += _mm_tn(xc[:, _hs(h)], dpa[:, _hs(h)])
            dxc = dxc + jnp.concatenate(dxcs, axis=-1)
            sg["dvec"][3:4, :] += _rowsum(dxc)
            sg["dcw"][3:4, :] += _rowsum(dxc * xb)
            sg["dcw"][2:3, :] += _rowsum(dxc * lc["x1"])
            sg["dcw"][1:2, :] += _rowsum(dxc * lc["x2"])
            sg["dcw"][0:1, :] += _rowsum(dxc * lc["x3"])
            ext = jnp.concatenate([dxc, dccar[...]], axis=0)
            n = TS + SUB
            cw = p["cw"]
            dproj_ref[:, 0:TOK] = (cw[3:4, :] * dxc + cw[2:3, :] * pltpu.roll(ext, n - 1, 0)[:TS]
                                   + cw[1:2, :] * pltpu.roll(ext, n - 2, 0)[:TS]
                                   + cw[0:1, :] * pltpu.roll(ext, n - 3, 0)[:TS]).astype(bf16)
            dccar[...] = dxc[0:SUB, :]

        dxin_ref[...] = ALPHA * dz + _mm(dproj_ref[...], winT_ref[...])

    rtile = lambda w: pl.BlockSpec((TS, w), lambda s: (nt - 1 - s, 0))
    consts = [w_inT, w_outT, lng, kh, khT, vh, vhT] + pvals
    in_arrays = [dxout, z, proj] + consts + list(saves)
    in_specs = [rtile(D_MODEL), rtile(D_MODEL), rtile(W)] + [_const_spec(a) for a in consts]
    for a in saves:
        in_specs.append(pl.BlockSpec((1,) + a.shape[1:], lambda s, _n=a.ndim - 1: (nt - 1 - s,) + (0,) * _n))
    out_shape = [jax.ShapeDtypeStruct((S, D_MODEL), f32), jax.ShapeDtypeStruct((S, W), bf16),
                 jax.ShapeDtypeStruct((S, D_MODEL), bf16), jax.ShapeDtypeStruct((S, D_MODEL), bf16),
                 jax.ShapeDtypeStruct((SUB, D_MODEL), f32), jax.ShapeDtypeStruct((XHEADS, XW, XW), f32),
                 jax.ShapeDtypeStruct((XHEADS, XW, XW), f32)]
    out_specs = [rtile(D_MODEL), rtile(W), rtile(D_MODEL), rtile(D_MODEL), _acc_spec((SUB, D_MODEL)),
                 _acc_spec((XHEADS, XW, XW)), _acc_spec((XHEADS, XW, XW))]
    for n in sg_names:
        out_shape.append(jax.ShapeDtypeStruct(sg_shapes[n], f32))
        out_specs.append(_acc_spec(sg_shapes[n]))
    if kind == 1:
        scratch = ([pltpu.VMEM((NH, HD, HD), f32)] * 2 + [pltpu.VMEM((TS // CHUNK, NH, HD, HD), f32)] * 3
                   + [pltpu.VMEM((TS, TOK), f32)] * 10)
    elif kind == 2:
        scratch = [pltpu.VMEM((2 * SUB, TOK), f32)]
    elif kind == 3:
        scratch = [pltpu.VMEM((SUB, TOK), f32)] * 2
    else:
        scratch = []
    outs = pl.pallas_call(body, name=f"bwd_layer{kind}", grid=(nt,), in_specs=in_specs, out_specs=out_specs,
                          out_shape=out_shape, scratch_shapes=scratch, compiler_params=_params())(*in_arrays)
    return outs[:7], dict(zip(sg_names, outs[7:]))


def _prep(mem, w_kv, logits):
    def body(mem_ref, w_ref, lg_ref, kh_ref, khT_ref, vh_ref, vhT_ref, p_ref):
        kv = _mm(mem_ref[...], w_ref[...])
        k, v = kv[:, 0:XW], kv[:, XW:]
        kT, vT = k.T, v.T
        col = lax.broadcasted_iota(jnp.int32, (XW, XW), 1) // XDIM
        row = lax.broadcasted_iota(jnp.int32, (XW, XW), 0) // XDIM
        for h in range(XHEADS):
            kh_ref[h] = jnp.where(col == h, k, 0.0).astype(bf16)
            vh_ref[h] = jnp.where(col == h, v, 0.0).astype(bf16)
            khT_ref[h] = jnp.where(row == h, kT, 0.0).astype(bf16)
            vhT_ref[h] = jnp.where(row == h, vT, 0.0).astype(bf16)
        lg = lg_ref[...]
        e = jnp.exp(lg - jnp.max(lg, axis=0, keepdims=True))
        p_ref[...] = e / jnp.sum(e, axis=0, keepdims=True)

    vm = pl.BlockSpec(memory_space=pltpu.VMEM)
    hs = jax.ShapeDtypeStruct((XHEADS, XW, XW), bf16)
    return pl.pallas_call(body, name="prep_memory", in_specs=[vm] * 3, out_specs=[vm] * 5,
                          out_shape=[hs, hs, hs, hs, jax.ShapeDtypeStruct(logits.shape, f32)])(mem, w_kv, logits)


def _kv_bwd(mem, dks, dvs):
    def body(mem_ref, *refs):
        out_ref = refs[-1]
        col = lax.broadcasted_iota(jnp.int32, (XW, XW), 1) // XDIM
        dk = jnp.zeros((XW, XW), f32)
        dv = jnp.zeros((XW, XW), f32)
        for l in range(DEPTH):
            for h in range(XHEADS):
                dk = dk + jnp.where(col == h, refs[l][h], 0.0)
                dv = dv + jnp.where(col == h, refs[DEPTH + l][h], 0.0)
        out_ref[:, 0:XW] = _mm_tn(mem_ref[...], dk)
        out_ref[:, XW:] = _mm_tn(mem_ref[...], dv)

    vm = pl.BlockSpec(memory_space=pltpu.VMEM)
    return pl.pallas_call(body, name="kv_bwd", in_specs=[vm] * (1 + 2 * DEPTH), out_specs=vm,
                          out_shape=jax.ShapeDtypeStruct((D_MODEL, 2 * XW), f32))(mem, *dks, *dvs)


def _tn_gemm(a, b, name, nb):
    S, M = a.shape
    N = b.shape[1]
    NB = N // nb
    nk = S // TK

    def body(a_ref, b_ref, o_ref):
        @pl.when(pl.program_id(1) == 0)
        def _():
            o_ref[...] = jnp.zeros_like(o_ref)

        o_ref[...] += _mm_tn(a_ref[...], b_ref[...])

    return pl.pallas_call(body, name=name, grid=(nb, nk),
                          in_specs=[pl.BlockSpec((TK, M), lambda j, k: (k, 0)), pl.BlockSpec((TK, NB), lambda j, k: (k, j))],
                          out_specs=pl.BlockSpec((M, NB), lambda j, k: (0, j)),
                          out_shape=jax.ShapeDtypeStruct((M, N), f32),
                          compiler_params=pltpu.CompilerParams(dimension_semantics=("parallel", "arbitrary"),
                                                               vmem_limit_bytes=VMEM_LIMIT))(a, b)


def _rows_block(R, mult=16, cap=1024):
    best = R
    for d in range(mult, min(R, cap) + 1, mult):
        if R % d == 0:
            best = d
    return best


def _add2(a, b, name):
    R, C = a.shape
    br = _rows_block(R)

    def body(a_ref, b_ref, o_ref, ob_ref):
        s = a_ref[...] + b_ref[...]
        o_ref[...] = s
        ob_ref[...] = s.astype(bf16)

    spec = pl.BlockSpec((br, C), lambda i: (i, 0))
    return pl.pallas_call(body, name=name, grid=(R // br,), in_specs=[spec, spec], out_specs=[spec, spec],
                          out_shape=[jax.ShapeDtypeStruct((R, C), f32), jax.ShapeDtypeStruct((R, C), bf16)],
                          compiler_params=_params("parallel"))(a, b)


def _add4(own, r, name):
    R, C = own.shape
    br = _rows_block(R)

    def body(o_ref, r_ref, out_ref):
        out_ref[...] = ((o_ref[...] + r_ref[0].astype(f32)) + r_ref[1].astype(f32)) + r_ref[2].astype(f32)

    spec = pl.BlockSpec((br, C), lambda i: (i, 0))
    return pl.pallas_call(body, name=name, grid=(R // br,),
                          in_specs=[spec, pl.BlockSpec((3, br, C), lambda i: (0, i, 0))], out_specs=spec,
                          out_shape=jax.ShapeDtypeStruct((R, C), f32), compiler_params=_params("parallel"))(own, r)


def _adamw(w, g, m, v, name):
    R, C = w.shape
    br = _rows_block(R, mult=SUB, cap=512)
    c1 =1.0 / (1.0 - ADAM_B1 ** ADAM_STEP)
    c2 = 1.0 / (1.0 - ADAM_B2 ** ADAM_STEP)

    def body(w_ref, g_ref, m_ref, v_ref, d_ref, nm_ref, nv_ref):
        g_ = g_ref[...]
        nm = ADAM_B1 * m_ref[...] + (1.0 - ADAM_B1) * g_
        nv = ADAM_B2 * v_ref[...] + (1.0 - ADAM_B2) * (g_ * g_)
        nm_ref[...] = nm
        nv_ref[...] = nv
        d_ref[...] = -ADAM_LR * ((nm * c1) / (jnp.sqrt(nv * c2) + ADAM_EPS) + ADAM_WD * w_ref[...])

    spec = pl.BlockSpec((br, C), lambda i: (i, 0))
    sh = jax.ShapeDtypeStruct((R, C), f32)
    return pl.pallas_call(body, name=name, grid=(R // br,), in_specs=[spec] * 4, out_specs=[spec] * 3,
                          out_shape=[sh, sh, sh], compiler_params=_params("parallel"))(w, g, m, v)


def _small_finish(dbacc, p_soft, dlb):
    def body(db_ref, p_ref, dlb_ref, dbs_ref, dlg_ref):
        lane = lax.broadcasted_iota(jnp.int32, (HD, HD), 1)
        acc = jnp.zeros((HD, HD), f32)
        for h in range(NH):
            acc = acc + jnp.where(lane == h, jnp.sum(db_ref[h], axis=-1, keepdims=True), 0.0)
        dbs_ref[...] = acc
        p = p_ref[...]
        p1 = p[1:2, :]
        rowi = lax.broadcasted_iota(jnp.int32, p.shape, 0)
        dlg_ref[...] = dlb_ref[0:1, :] * p1 * (jnp.where(rowi == 1, 1.0, 0.0) - p)

    vm = pl.BlockSpec(memory_space=pltpu.VMEM)
    return pl.pallas_call(body, name="small_finish", in_specs=[vm] * 3, out_specs=[vm] * 2,
                          out_shape=[jax.ShapeDtypeStruct((HD, HD), f32), jax.ShapeDtypeStruct(p_soft.shape, f32)])(dbacc, p_soft, dlb)


def _where_am_i():
    return lax.axis_index("x"), lax.axis_index("y"), lax.axis_index("c")


MAX_PIECES = 8


def _nchunks(rows, mult):
    for n in range(MAX_PIECES, 0, -1):
        if rows % (n * mult) == 0:
            return n
    return 1


def _all_gather_xy(shard):
    R, C = shard.shape
    H = R // 2
    NP = _nchunks(H, 16)
    PR = H // NP

    def body(sh_ref, out_ref, send_sems, recv_sems, lsem):
        x, y, c = _where_am_i()
        j = 2 * x + y
        sib = (x, y, 1 - c)
        chips = [(1 - x, y), (x, 1 - y), (1 - x, 1 - y)]

        def cp(k, src, dst, to):
            return pltpu.make_async_remote_copy(src_ref=src, dst_ref=dst, send_sem=send_sems.at[k], recv_sem=recv_sems.at[k],
                                                device_id=to, device_id_type=MESH)

        def rows(half, q):
            return pl.ds(half * H + q * PR, PR)

        for q in range(2 * NP):
            pltpu.make_async_copy(sh_ref.at[pl.ds(q * PR, PR)], out_ref.at[j, pl.ds(q * PR, PR)], lsem).start()
        first = []
        for q in range(NP):
            for k, (cx, cy) in enumerate(chips):
                d = cp(k * NP + q, sh_ref.at[rows(c, q)], out_ref.at[j, rows(c, q)], (cx, cy, c))
                d.start()
                first.append(d)
        passed = []
        for q in range(NP):
            for k, (cx, cy) in enumerate(chips):
                blk = out_ref.at[2 * cx + cy, rows(c, q)]
                cp(k * NP + q, blk, blk, (cx, cy, c)).wait_recv()
                fwd = cp((3 + k) * NP + q, blk, blk, sib)
                fwd.start()
                passed.append(fwd)
        for q in range(NP):
            for k, (cx, cy) in enumerate(chips):
                blk = out_ref.at[2 * cx + cy, rows(1 - c, q)]
                cp((3 + k) * NP + q, blk, blk, sib).wait_recv()
        for d in first + passed:
            d.wait_send()
        pltpu.make_async_copy(sh_ref, out_ref.at[j], lsem).wait()

    anyspec = pl.BlockSpec(memory_space=pl.ANY)
    return pl.pallas_call(body, name="all_gather_weights", in_specs=[anyspec], out_specs=anyspec,
                          out_shape=jax.ShapeDtypeStruct((4, R, C), shard.dtype),
                          scratch_shapes=[pltpu.SemaphoreType.DMA((6 * NP,)), pltpu.SemaphoreType.DMA((6 * NP,)),
                                          pltpu.SemaphoreType.DMA],
                          compiler_params=pltpu.CompilerParams(has_side_effects=True))(shard)


def _rs_swap_halves(part):
    _, R, C = part.shape
    H = R // 2
    NP = _nchunks(H, SUB)
    PR = H // NP

    def body(p_ref, own_ref, got_ref, send_sem, recv_sem, lsem):
        x, y, c = _where_am_i()

        def remote(src, dst):
            return pltpu.make_async_remote_copy(src_ref=src, dst_ref=dst, send_sem=send_sem, recv_sem=recv_sem,
                                                device_id=(x, y, 1 - c), device_id_type=MESH)

        for s in range(4):
            for q in range(NP):
                remote(p_ref.at[s, pl.ds((1 - c) * H + q * PR, PR)], got_ref.at[s, pl.ds(q * PR, PR)]).start()
                pltpu.make_async_copy(p_ref.at[s, pl.ds(c * H + q * PR, PR)], own_ref.at[s, pl.ds(q * PR, PR)], lsem).start()
        remote(got_ref, got_ref).wait()
        pltpu.make_async_copy(got_ref, own_ref, lsem).wait()

    anyspec = pl.BlockSpec(memory_space=pl.ANY)
    sh = jax.ShapeDtypeStruct((4, H, C), part.dtype)
    return pl.pallas_call(body, name="rs_swap_halves", in_specs=[anyspec], out_specs=[anyspec, anyspec], out_shape=[sh, sh],
                          scratch_shapes=[pltpu.SemaphoreType.DMA, pltpu.SemaphoreType.DMA, pltpu.SemaphoreType.DMA],
                          compiler_params=pltpu.CompilerParams(has_side_effects=True))(part)


def _rs_to_owners(q32, qb):
    _, H, C = q32.shape
    NP = _nchunks(H, 16)
    PR = H // NP

    def body(q_ref, qb_ref, own_ref, got_ref, send_sems, recv_sems, lsem):
        x, y, c = _where_am_i()
        j = 2 * x + y
        chips = [(1 - x, y), (x, 1 - y), (1 - x, 1 - y)]

        def remote(k, src, dst):
            cx, cy = chips[k]
            return pltpu.make_async_remote_copy(src_ref=src, dst_ref=dst, send_sem=send_sems.at[k], recv_sem=recv_sems.at[k],
                                                device_id=(cx, cy, c), device_id_type=MESH)

        for q in range(NP):
            for k, (cx, cy) in enumerate(chips):
                remote(k, qb_ref.at[2 * cx + cy, pl.ds(q * PR, PR)], got_ref.at[k, pl.ds(q * PR, PR)]).start()
            pltpu.make_async_copy(q_ref.at[j, pl.ds(q * PR, PR)], own_ref.at[pl.ds(q * PR, PR)], lsem).start()
        for k in range(3):
            remote(k, got_ref.at[k], got_ref.at[k]).wait()
        pltpu.make_async_copy(q_ref.at[j], own_ref, lsem).wait()

    anyspec = pl.BlockSpec(memory_space=pl.ANY)
    return pl.pallas_call(body, name="rs_to_owners", in_specs=[anyspec, anyspec], out_specs=[anyspec, anyspec],
                          out_shape=[jax.ShapeDtypeStruct((H, C), f32), jax.ShapeDtypeStruct((3, H, C), bf16)],
                          scratch_shapes=[pltpu.SemaphoreType.DMA((3,)), pltpu.SemaphoreType.DMA((3,)), pltpu.SemaphoreType.DMA],
                          compiler_params=pltpu.CompilerParams(has_side_effects=True))(q32, qb)


def _rs_join_halves(t):
    H, C = t.shape
    NP = _nchunks(H, SUB)
    PR = H // NP

    def body(t_ref, out_ref, send_sem, recv_sem, lsem):
        x, y, c = _where_am_i()

        def remote(src, dst):
            return pltpu.make_async_remote_copy(src_ref=src, dst_ref=dst, send_sem=send_sem, recv_sem=recv_sem,
                                                device_id=(x, y, 1 - c), device_id_type=MESH)

        for q in range(NP):
            dst = out_ref.at[pl.ds(c * H + q * PR, PR)]
            remote(t_ref.at[pl.ds(q * PR, PR)], dst).start()
            pltpu.make_async_copy(t_ref.at[pl.ds(q * PR, PR)], dst, lsem).start()
        remote(t_ref, out_ref.at[pl.ds((1 - c) * H, H)]).wait()
        pltpu.make_async_copy(t_ref, out_ref.at[pl.ds(c * H, H)], lsem).wait()

    anyspec = pl.BlockSpec(memory_space=pl.ANY)
    return pl.pallas_call(body, name="rs_join_halves", in_specs=[anyspec], out_specs=anyspec,
                          out_shape=jax.ShapeDtypeStruct((2 * H, C), t.dtype),
                          scratch_shapes=[pltpu.SemaphoreType.DMA, pltpu.SemaphoreType.DMA, pltpu.SemaphoreType.DMA],
                          compiler_params=pltpu.CompilerParams(has_side_effects=True))(t)


def _all_reduce_small(g):
    R, C = g.shape
    H = R // 2
    NP = _nchunks(H, SUB)
    PR = H // NP

    def body(g_ref, out_ref, sib_ref, chip_ref, send_sems, recv_sems):
        x, y, c = _where_am_i()
        j = 2 * x + y
        sib = (x, y, 1 - c)
        chips = [(1 - x, y), (x, 1 - y), (1 - x, 1 - y)]
        rows = pl.ds(pl.multiple_of(c * H, SUB), H)

        def cp(k, src, dst, to):
            return pltpu.make_async_remote_copy(src_ref=src, dst_ref=dst, send_sem=send_sems.at[k], recv_sem=recv_sems.at[k],
                                                device_id=to, device_id_type=MESH)

        def pieces(k, src, dst, to):
            for q in range(NP):
                cp(k, src.at[pl.ds(q * PR, PR)], dst.at[pl.ds(q * PR, PR)], to).start()

        for half in range(2):
            pieces(0, g_ref.at[pl.ds(half * H, H)], sib_ref.at[pl.ds(half * H, H)], sib)
        cp(0, g_ref, sib_ref, sib).wait()
        chip_ref[j] = g_ref[rows, :] + sib_ref[rows, :]
        for k, (cx, cy) in enumerate(chips):
            pieces(1 + k, chip_ref.at[j], chip_ref.at[j], (cx, cy, c))
        for k, (cx, cy) in enumerate(chips):
            blk = chip_ref.at[2 * cx + cy]
            cp(1 + k, blk, blk, (cx, cy, c)).wait()
        out_ref[rows, :] = ((chip_ref[0] + chip_ref[1]) + chip_ref[2]) + chip_ref[3]
        other = out_ref.at[pl.ds(pl.multiple_of((1 - c) * H, SUB), H)]
        pieces(4, out_ref.at[rows], out_ref.at[rows], sib)
        cp(4, other, other, sib).wait()

    vm = pl.BlockSpec(memory_space=pltpu.VMEM)
    return pl.pallas_call(body, name="all_reduce_small", in_specs=[vm], out_specs=vm,
                          out_shape=jax.ShapeDtypeStruct((R, C), f32),
                          scratch_shapes=[pltpu.VMEM((R, C), f32), pltpu.VMEM((4, H, C), f32),
                                          pltpu.SemaphoreType.DMA((5,)), pltpu.SemaphoreType.DMA((5,))],
                          compiler_params=pltpu.CompilerParams(has_side_effects=True, vmem_limit_bytes=VMEM_LIMIT))(g)


def _pack_flat(arrs, rows_mult):
    flat = jnp.concatenate([a.reshape(-1) for a in arrs])
    n = flat.shape[0]
    tot = -(-n // (rows_mult * LANE)) * rows_mult * LANE
    return jnp.pad(flat, (0, tot - n)).reshape(-1, LANE)


def _unpack_flat(buf, shapes):
    flat = buf.reshape(-1)
    out, o = [], 0
    for s in shapes:
        n = math.prod(s)
        out.append(flat[o:o + n].reshape(s))
        o += n
    return out


_BIG = ("mem_kv_w", "w_out", "a_w_in", "b_w_in", "c_w_in", "d_w_in")
BIG_ROWS_MULT = 256


def _pack_big_shards(sh):
    buf = jnp.concatenate([sh[n].reshape(-1, D_MODEL) for n in _BIG], axis=0)
    return jnp.pad(buf, ((0, -buf.shape[0] % BIG_ROWS_MULT), (0, 0)))


def _unpack_big_full(buf):
    o = 0
    out = {}
    r = XW * 2 * XW // D_MODEL
    out["mem_kv_w"] = buf[:, o:o + r].reshape(D_MODEL, 2 * XW)
    o += r
    r = DEPTH * (D_MODEL // 4)
    out["w_out"] = buf[:, o:o + r].reshape(4, DEPTH, D_MODEL // 4, D_MODEL).transpose(1, 0, 2, 3).reshape(DEPTH, D_MODEL, D_MODEL)
    o += r
    for n, kind in (("a_w_in", 0), ("b_w_in", 1), ("c_w_in", 2), ("d_w_in", 3)):
        W = _OFFS[kind]["W"]
        r = W // 4
        out[n] = buf[:, o:o + r].reshape(4, D_MODEL, W // 4).transpose(1, 0, 2).reshape(D_MODEL, W)
        o += r
    return out


def _pack_big_full(full):
    parts = [full["mem_kv_w"].reshape(4, -1, D_MODEL),
             full["w_out"].reshape(DEPTH, 4, D_MODEL // 4, D_MODEL).transpose(1, 0, 2, 3).reshape(4, -1, D_MODEL)]
    for n, kind in (("a_w_in", 0), ("b_w_in", 1), ("c_w_in", 2), ("d_w_in", 3)):
        W = _OFFS[kind]["W"]
        parts.append(full[n].reshape(D_MODEL, 4, W // 4).transpose(1, 0, 2).reshape(4, -1, D_MODEL))
    rows = sum(p.shape[1] for p in parts)
    parts.append(jnp.zeros((4, -rows % BIG_ROWS_MULT, D_MODEL), f32))
    return jnp.concatenate(parts, axis=1)


def _unpack_big_shards(buf, like):
    out, o = {}, 0
    for n in _BIG:
        r = like[n].size // D_MODEL
        out[n] = buf[o:o + r].reshape(like[n].shape)
        o += r
    return out


def _row8(v):
    v = v.reshape(-1, v.shape[-1])
    return jnp.pad(v, ((0, SUB - v.shape[0]), (0, 0)))


def kernel(x, mem, mem_kv_w, ln_g, ln_b, w_out, hgrn_lb_logits, a_w_in, a_w_s, a_b_s, b_w_in, b_norm_g, c_w_in, c_w_pool, c_scale, d_w_in, d_conv_w, d_conv_b, d_w_gx, d_b_gx, d_w_ga, d_b_ga, d_a_param, loss_target, m_mem_kv_w, m_ln_g, m_ln_b, m_w_out, m_hgrn_lb_logits, m_a_w_in, m_a_w_s, m_a_b_s, m_b_w_in, m_b_norm_g, m_c_w_in, m_c_w_pool, m_c_scale, m_d_w_in, m_d_conv_w, m_d_conv_b, m_d_w_gx, m_d_b_gx, m_d_w_ga, m_d_b_ga, m_d_a_param, v_mem_kv_w, v_ln_g, v_ln_b, v_w_out, v_hgrn_lb_logits, v_a_w_in, v_a_w_s, v_a_b_s, v_b_w_in, v_b_norm_g, v_c_w_in, v_c_w_pool, v_c_scale, v_d_w_in, v_d_conv_w, v_d_conv_b, v_d_w_gx, v_d_b_gx, v_d_w_ga, v_d_b_ga, v_d_a_param):
    names = ["mem_kv_w", "ln_g", "ln_b", "w_out", "hgrn_lb_logits", "a_w_in", "a_w_s", "a_b_s", "b_w_in", "b_norm_g", "c_w_in",
             "c_w_pool", "c_scale", "d_w_in", "d_conv_w", "d_conv_b", "d_w_gx", "d_b_gx", "d_w_ga", "d_b_ga", "d_a_param"]
    w = dict(mem_kv_w=mem_kv_w, ln_g=ln_g, ln_b=ln_b, w_out=w_out, hgrn_lb_logits=hgrn_lb_logits, a_w_in=a_w_in, a_w_s=a_w_s,
             a_b_s=a_b_s, b_w_in=b_w_in, b_norm_g=b_norm_g, c_w_in=c_w_in, c_w_pool=c_w_pool, c_scale=c_scale, d_w_in=d_w_in,
             d_conv_w=d_conv_w, d_conv_b=d_conv_b, d_w_gx=d_w_gx, d_b_gx=d_b_gx, d_w_ga=d_w_ga, d_b_ga=d_b_ga, d_a_param=d_a_param)
    m = dict(zip(names, [m_mem_kv_w, m_ln_g, m_ln_b, m_w_out, m_hgrn_lb_logits, m_a_w_in, m_a_w_s, m_a_b_s, m_b_w_in, m_b_norm_g,
                         m_c_w_in, m_c_w_pool, m_c_scale, m_d_w_in, m_d_conv_w, m_d_conv_b, m_d_w_gx, m_d_b_gx, m_d_w_ga,
                         m_d_b_ga, m_d_a_param]))
    v = dict(zip(names, [v_mem_kv_w, v_ln_g, v_ln_b, v_w_out, v_hgrn_lb_logits, v_a_w_in, v_a_w_s, v_a_b_s, v_b_w_in, v_b_norm_g,
                         v_c_w_in, v_c_w_pool, v_c_scale, v_d_w_in, v_d_conv_w, v_d_conv_b, v_d_w_gx, v_d_b_gx, v_d_w_ga,
                         v_d_b_ga, v_d_a_param]))
    xi, yi = lax.axis_index("x"), lax.axis_index("y")
    jshard = 2 * xi + yi
    x2 = x[0]
    mem2 = mem[0]
    tgt2 = loss_target[0]

    big_sh = {n: w[n] for n in _BIG}
    gathered = _all_gather_xy(_pack_big_shards(big_sh).astype(bf16))
    full = _unpack_big_full(gathered)
    w_in = [full["a_w_in"], full["b_w_in"], full["c_w_in"], full["d_w_in"]]
    w_inT = [a.T for a in w_in]
    w_outs = [full["w_out"][l] for l in range(DEPTH)]
    w_outT = [a.T for a in w_outs]

    def gather_small(shard):
        z = jnp.zeros((4, POOL_GROUP), f32)
        return lax.dynamic_update_slice(z, shard.reshape(1, POOL_GROUP), (jshard, 0))

    sm_sh = jnp.concatenate([gather_small(b_norm_g), gather_small(c_scale), gather_small(d_conv_b), gather_small(d_a_param)]
                            + [gather_small(d_conv_w[:, r]) for r in range(4)], axis=0)
    ci = lax.axis_index("c")
    sm_all = _all_reduce_small(_pack_flat([jnp.where(ci == 0, sm_sh, 0.0)], SUB * 2))
    sm = _unpack_flat(sm_all, [(8, 4 * POOL_GROUP)])[0]
    ng_full, scale_full, convb_full, ap_full = sm[0:1], sm[1:2], sm[2:3], sm[3:4]
    convw_full = sm[4:8]

    tril = jnp.tril(jnp.ones((HD, HD), bool))
    wtri = jnp.where(tril, a_w_s[0], 0.0)
    wbd = jnp.zeros((TOK, TOK), f32)
    for g in range(4):
        wbd = lax.dynamic_update_slice(wbd, c_w_pool[0, g], (g * POOL_GROUP, g * POOL_GROUP))
    kh, khT, vh, vhT, p_soft = _prep(mem2, full["mem_kv_w"], hgrn_lb_logits)
    prm = [
        dict(wtri=wtri.astype(bf16), wtriT=wtri.transpose(0, 2, 1).astype(bf16),
             bcolb=jnp.broadcast_to(a_b_s[0][:, :, None], (NH, HD, HD))),
        dict(lb=p_soft[1:2], ng=ng_full),
        dict(wbd=wbd.astype(bf16), wbdT=wbd.T.astype(bf16), scale=scale_full),
        dict(cw=_row8(convw_full), cb=convb_full, wgx=d_w_gx[0].astype(bf16), wgxT=d_w_gx[0].transpose(0, 2, 1).astype(bf16),
             bgx=d_b_gx.reshape(1, TOK), wga=d_w_ga[0].astype(bf16), wgaT=d_w_ga[0].transpose(0, 2, 1).astype(bf16),
             bga=d_b_ga.reshape(1, TOK), ap=ap_full),
    ]

    acts = []
    h = x2
    for l in range(DEPTH):
        outs = _fwd_layer(l, h, w_in[l], w_outs[l], ln_g[l:l + 1], ln_b[l:l + 1], khT, vh, prm[l],
                          tgt2 if l == DEPTH - 1 else None)
        nfix = 4 if l == DEPTH - 1 else 3
        acts.append(dict(xin=h, proj=outs[1], z=outs[2], saves=outs[nfix:]))
        if l == DEPTH - 1:
            loss_part = outs[3]
        h = outs[0]
    loss = lax.psum(0.5 / D_MODEL * jnp.sum(loss_part), ("x", "y", "c"))

    dh = h
    gfull = {}
    gw_out = [None] * DEPTH
    dln = [None] * DEPTH
    dks, dvs = [None] * DEPTH, [None] * DEPTH
    sgr = [None] * DEPTH
    for l in reversed(range(DEPTH)):
        a = acts[l]
        (dxin, dproj, mixedb, dyb, dln[l], dks[l], dvs[l]), sgr[l] = _bwd_layer(
            l, dh, a["z"], a["proj"], w_inT[l], w_outT[l], ln_g[l:l + 1], kh, khT, vh, vhT, prm[l], a["saves"])
        gfull["abcd"[l] + "_w_in"] = _tn_gemm(a["xin"], dproj, f"grad_w_in{l}", 2)
        gw_out[l] = _tn_gemm(mixedb, dyb, f"grad_w_out{l}", 1)
        dh = dxin
    grad_x = dh[None]
    gfull["w_out"] = jnp.stack(gw_out)
    gfull["mem_kv_w"] = _kv_bwd(mem2, dks, dvs)

    own, got = _rs_swap_halves(_pack_big_full(gfull))
    H = own.shape[1]
    q32, qb = _add2(own.reshape(4 * H, D_MODEL), got.reshape(4 * H, D_MODEL), "rs_add_sibling")
    own2, got2 = _rs_to_owners(q32.reshape(4, H, D_MODEL), qb.reshape(4, H, D_MODEL))
    gbig = _rs_join_halves(_add4(own2, got2, "rs_add_chips"))
    d_big, m_big, v_big = _adamw(_pack_big_shards(big_sh), gbig, _pack_big_shards({n: m[n] for n in _BIG}),
                                 _pack_big_shards({n: v[n] for n in _BIG}), "adamw_big")
    g_sh = _unpack_big_shards(gbig, big_sh)
    d_sh = _unpack_big_shards(d_big, big_sh)
    m_sh = _unpack_big_shards(m_big, big_sh)
    v_sh = _unpack_big_shards(v_big, big_sh)

    dbs, dlogits = _small_finish(sgr[0]["dbacc"], p_soft, sgr[1]["dlb"])
    gs = {
        "ln_g": jnp.concatenate([dln[l][0:1] for l in range(DEPTH)], axis=0),
        "ln_b": jnp.concatenate([dln[l][1:2] for l in range(DEPTH)], axis=0),
        "hgrn_lb_logits": dlogits,
        "a_w_s": sgr[0]["dwtri"][None],
        "a_b_s": dbs[:, 0:NH].T[None],
        "b_norm_g": sgr[1]["dng"][0:1],
        "c_w_pool": jnp.stack([sgr[2]["dwbd"][g * POOL_GROUP:(g + 1) * POOL_GROUP, g * POOL_GROUP:(g + 1) * POOL_GROUP]
                               for g in range(4)])[None],
        "c_scale": sgr[2]["dscale"][0:1],
        "d_conv_w": sgr[3]["dcw"][0:4][None],
        "d_conv_b": sgr[3]["dvec"][3:4],
        "d_w_gx": sgr[3]["dwgx"][None],
        "d_b_gx": sgr[3]["dvec"][1:2].reshape(1, NH, HD),
        "d_w_ga": sgr[3]["dwga"][None],
        "d_b_ga": sgr[3]["dvec"][2:3].reshape(1, NH, HD),
        "d_a_param": sgr[3]["dvec"][0:1],
    }
    small = [n for n in names if n not in _BIG]
    full_shapes = [gs[n].shape for n in small]
    gsum = dict(zip(small, _unpack_flat(_all_reduce_small(_pack_flat([gs[n] for n in small], BIG_ROWS_MULT)), full_shapes)))
    for n in ("b_norm_g", "c_scale", "d_conv_b", "d_a_param"):
        gsum[n] = lax.dynamic_slice(gsum[n], (0, jshard * POOL_GROUP), (1, POOL_GROUP))
    gsum["d_conv_w"] = lax.dynamic_slice(gsum["d_conv_w"], (0, 0, jshard * POOL_GROUP), (1, 4, POOL_GROUP))
    shapes = [w[n].shape for n in small]
    pk = lambda d: _pack_flat([d[n] for n in small], 2 * BIG_ROWS_MULT)
    d_s, m_s, v_s = _adamw(pk(w), pk(gsum), pk(m), pk(v), "adamw_small")
    d_sm = dict(zip(small, _unpack_flat(d_s, shapes)))
    m_sm = dict(zip(small, _unpack_flat(m_s, shapes)))
    v_sm = dict(zip(small, _unpack_flat(v_s, shapes)))

    grads = {**gsum, **g_sh}
    deltas = {**d_sm, **d_sh}
    new_m = {**m_sm, **m_sh}
    new_v = {**v_sm, **v_sh}
    return (loss, grad_x, *[grads[n] for n in names], *[deltas[n] for n in names], *[new_m[n] for n in names],
            *[new_v[n] for n in names])
```

```python
import functools
import math

import jax
import jax.numpy as jnp
from jax import lax
from jax.experimental import pallas as pl
from jax.experimental.pallas import tpu as pltpu

f32 = jnp.float32
bf16 = jnp.bfloat16
MM = bf16

D_MODEL = 1024
TOK = 768
XW = 256
XHEADS = 4
XDIM = 64
HD = 128
NH = TOK // HD
CHUNK = 16
POOL_GROUP = 192
DEPTH = 4
ALPHA = (2 * DEPTH) ** 0.25
LN_EPS = 1e-5
RMS_EPS = 1e-6
LRU_C = 8.0
ADAM_LR, ADAM_B1, ADAM_B2, ADAM_EPS, ADAM_WD, ADAM_STEP = 0.001, 0.9, 0.999, 1e-08, 0.01, 10

_TS = (256, 128, 256, 256)
TK = 512
SUB = 8
LANE = 128
VMEM_LIMIT = 58 * 1024 * 1024

_OFFS = (
    dict(u=0, v=768, qx=1536, gate=1792, W=2816),
    dict(q=0, f=768, i=1536, qx=2304, gate=2560, W=3584),
    dict(p=0, qx=768, gate=1024, W=2048),
    dict(xb=0, qx=768, gate=1024, W=2048),
)
_PRM = (
    ("wtri", "wtriT", "bcolb"),
    ("lb", "ng"),
    ("wbd", "wbdT", "scale"),
    ("cw", "cb", "wgx", "wgxT", "bgx", "wga", "wgaT", "bga", "ap"),
)
MESH = pl.DeviceIdType.MESH


def _mm(a, b):
    return jnp.dot(a.astype(MM), b.astype(MM), preferred_element_type=f32)


def _mm_nt(a, b):
    return lax.dot_general(a.astype(MM), b.astype(MM), (((1,), (1,)), ((), ())), preferred_element_type=f32)


def _mm_tn(a, b):
    return lax.dot_general(a.astype(MM), b.astype(MM), (((0,), (0,)), ((), ())), preferred_element_type=f32)


def _mm_sel(sel, b):
    s = sel.astype(bf16)
    hi = b.astype(bf16)
    lo = (b - hi.astype(f32)).astype(bf16)
    return jnp.dot(s, hi, preferred_element_type=f32) + jnp.dot(s, lo, preferred_element_type=f32)


def _sig(x):
    return jax.nn.sigmoid(x)


_GC = math.sqrt(2.0 / math.pi)


def _gelu(x):
    t = jnp.tanh(_GC * (x + 0.044715 * x * x * x))
    return 0.5 * x * (1.0 + t), t


def _gelu_grad(x, t):
    return 0.5 * (1.0 + t) + 0.5 * x * (1.0 - t * t) * _GC * (1.0 + 3.0 * 0.044715 * x * x)


def _rowsum(x):
    return jnp.sum(x, axis=0, keepdims=True)


def _lmean(x):
    return jnp.mean(x, axis=-1, keepdims=True)


def _ln(z):
    mu = _lmean(z)
    zc = z - mu
    rstd = lax.rsqrt(_lmean(zc * zc) + LN_EPS)
    return zc * rstd, rstd


def _ln_bwd(dxh, xhat, rstd):
    return rstd * (dxh - _lmean(dxh) - xhat * _lmean(dxh * xhat))


def _hs(h):
    return slice(h * HD, (h + 1) * HD)


def _expm1(x):
    small = x * (1.0 + x * 0.5 * (1.0 + x * (1.0 / 3.0) * (1.0 + x * 0.25 * (1.0 + x * 0.2 * (1.0 + x * (1.0 / 6.0))))))
    return jnp.where(jnp.abs(x) < 0.25, small, jnp.exp(x) - 1.0)


def _softplus(x):
    e = jnp.exp(-jnp.abs(x))
    l1p = jnp.where(e < 1e-4, e - 0.5 * e * e, jnp.log(1.0 + e))
    return jnp.maximum(x, 0.0) + l1p


def _scan_fwd(a, b):
    n = a.shape[0]
    row = lax.broadcasted_iota(jnp.int32, a.shape, 0)
    d = 1
    while d < n:
        m = row >= d
        b = jnp.where(m, a * pltpu.roll(b, d, 0) + b, b)
        a = jnp.where(m, a * pltpu.roll(a, d, 0), a)
        d *= 2
    return a, b


def _scan_bwd(a, b):
    n = a.shape[0]
    row = lax.broadcasted_iota(jnp.int32, a.shape, 0)
    d = 1
    while d < n:
        m = row < n - d
        b = jnp.where(m, a * pltpu.roll(b, n - d, 0) + b, b)
        a = jnp.where(m, a * pltpu.roll(a, n - d, 0), a)
        d *= 2
    return a, b


def _chunk_mats(n):
    r = lax.broadcasted_iota(jnp.int32, (n, n), 0)
    c = lax.broadcasted_iota(jnp.int32, (n, n), 1)
    same = (r // CHUNK) == (c // CHUNK)
    return same, jnp.logical_and(same, c <= r)


def _pool_w(shape):
    lane = lax.broadcasted_iota(jnp.int32, shape, 1)
    return jnp.where(lane < POOL_GROUP, 2, jnp.where(lane < 2 * POOL_GROUP, 4, jnp.where(lane < 3 * POOL_GROUP, 8, 16)))


def _pool_pick(r1, r2, r3, r4):
    lane = lax.broadcasted_iota(jnp.int32, r1.shape, 1)
    return jnp.where(lane < POOL_GROUP, r1, jnp.where(lane < 2 * POOL_GROUP, r2, jnp.where(lane < 3 * POOL_GROUP, r3, r4)))


def _const_spec(a):
    nd = a.ndim
    return pl.BlockSpec(a.shape, lambda i, _nd=nd: (0,) * _nd, pipeline_mode=pl.Buffered(1))


def _acc_spec(shape):
    nd = len(shape)
    return pl.BlockSpec(shape, lambda i, _nd=nd: (0,) * _nd)


def _params(sem="arbitrary"):
    return pltpu.CompilerParams(dimension_semantics=(sem,), vmem_limit_bytes=VMEM_LIMIT)


def _xattn_fwd(qx, khT_ref, vh_ref):
    xo = jnp.zeros((qx.shape[0], XW), f32)
    ps = []
    for h in range(XHEADS):
        s = _mm(qx, khT_ref[h]) * (XDIM ** -0.5)
        e = jnp.exp(s - jnp.max(s, axis=-1, keepdims=True))
        p = e / jnp.sum(e, axis=-1, keepdims=True)
        xo = xo + _mm(p, vh_ref[h])
        ps.append(p)
    return xo, ps


def _hgrn_parallel(q_raw, fl, lb):
    n = q_raw.shape[0]
    same, tri = _chunk_mats(n)
    sq = _sig(q_raw)
    qf = q_raw * sq
    sgm = _sig(fl)
    f = lb + (1.0 - lb) * sgm
    logf = jnp.log(f)
    k = 1.0 - f
    g = _mm_sel(tri, logf)
    gl = _mm_sel(same, logf)
    eg = jnp.exp(g)
    eng = jnp.exp(-g)
    ee = jnp.exp(gl - g)
    return dict(sq=sq, qf=qf, sgm=sgm, f=f, k=k, eg=eg, eng=eng, ee=ee, q_dec=qf * eg, k_inv=k * eng, k_end=k * ee,
                a=jnp.exp(gl))


def _hgrn_intra(q_dec, k_inv, v):
    n = q_dec.shape[0]
    _, tri = _chunk_mats(HD)
    outs = []
    for h in range(NH):
        blks = []
        for b in range(n // HD):
            rs = slice(b * HD, (b + 1) * HD)
            sc = jnp.where(tri, _mm_nt(q_dec[rs, _hs(h)], k_inv[rs, _hs(h)]), 0.0)
            blks.append(_mm(sc, v[rs, _hs(h)]))
        outs.append(jnp.concatenate(blks, axis=0))
    return jnp.concatenate(outs, axis=-1)


def _cs(c):
    return slice(c * CHUNK, (c + 1) * CHUNK)


def _hgrn_inter_fwd(qdec_s, kend_s, v_s, a_s, oint_s, st_ref, states_s, u_s):
    n = qdec_s.shape[0] // CHUNK
    for c in range(n):
        for h in range(NH):
            u_s[c, h] = _mm_tn(v_s[_cs(c), _hs(h)], kend_s[_cs(c), _hs(h)])
    for h in range(NH):
        st = st_ref[h]
        for c in range(n):
            states_s[c, h] = st
            st = st * a_s[c * CHUNK:c * CHUNK + 1, _hs(h)] + u_s[c, h]
        st_ref[h] = st
    for c in range(n):
        for h in range(NH):
            oint_s[_cs(c), _hs(h)] = _mm_nt(qdec_s[_cs(c), _hs(h)], states_s[c, h])


def _rms(o):
    outs, rs = [], []
    for h in range(NH):
        oh = o[:, _hs(h)]
        r = lax.rsqrt(_lmean(oh * oh) + RMS_EPS)
        outs.append(oh * r)
        rs.append(r)
    return jnp.concatenate(outs, axis=-1), rs


def _gmlp_core(u_raw, v_raw, wtri_ref, bcolb_ref):
    gu, tu = _gelu(u_raw)
    gv, tv = _gelu(v_raw)
    vns, rstds, mixeds = [], [], []
    for h in range(NH):
        vn, rstd = _ln(gv[:, _hs(h)])
        blks = []
        for n in range(u_raw.shape[0] // HD):
            blks.append(_mm(wtri_ref[h], vn[n * HD:(n + 1) * HD]) + bcolb_ref[h])
        vns.append(vn)
        rstds.append(rstd)
        mixeds.append(jnp.concatenate(blks, axis=0))
    mixed = jnp.concatenate(mixeds, axis=-1)
    return gu, tu, tv, vns, rstds, mixed


def _pool_core(p, carry, row0, wbd_ref):
    ext = jnp.concatenate([carry, p], axis=0)
    r1 = ext + pltpu.roll(ext, 1, 0)
    r2 = r1 + pltpu.roll(r1, 2, 0)
    r3 = r2 + pltpu.roll(r2, 4, 0)
    r4 = r3 + pltpu.roll(r3, 8, 0)
    sel = _pool_pick(r1, r2, r3, r4)[2 * SUB:]
    grow = row0 + lax.broadcasted_iota(jnp.int32, p.shape, 0)
    inv_cnt = 1.0 / jnp.minimum(grow + 1, _pool_w(p.shape)).astype(f32)
    diff = sel * inv_cnt - p
    return diff, inv_cnt, _mm(diff, wbd_ref[...])


def _lru_core(xb, ccar, row0, p):
    ext = jnp.concatenate([ccar, xb], axis=0)
    cw = p["cw"]
    x1, x2, x3 = pltpu.roll(ext, 1, 0)[SUB:], pltpu.roll(ext, 2, 0)[SUB:], pltpu.roll(ext, 3, 0)[SUB:]
    xc = cw[3:4, :] * xb + cw[2:3, :] * x1 + cw[1:2, :] * x2 + cw[0:1, :] * x3 + p["cb"][...]
    gxs, gas = [], []
    for h in range(NH):
        gxs.append(_mm(xc[:, _hs(h)], p["wgx"][h]))
        gas.append(_mm(xc[:, _hs(h)], p["wga"][h]))
    gx = _sig(jnp.concatenate(gxs, axis=-1) + p["bgx"][...])
    ga = _sig(jnp.concatenate(gas, axis=-1) + p["bga"][...])
    sp = _softplus(-p["ap"][...])
    la = -LRU_C * ga * sp
    a = jnp.exp(la)
    grow = row0 + lax.broadcasted_iota(jnp.int32, xb.shape, 0)
    first = grow == 0
    mult = jnp.where(first, 1.0, jnp.sqrt(-_expm1(2.0 * la)))
    bt = mult * gx * xc
    return dict(x1=x1, x2=x2, x3=x3, xc=xc, gx=gx, ga=ga, sp=sp, a=a, mult=mult, bt=bt, first=first)


def _fwd_layer(kind, xin, w_in, w_out, lng, lnb, khT, vh, prm, tgt):
    S = xin.shape[0]
    TS = _TS[kind]
    nt = S // TS
    off = _OFFS[kind]
    W = off["W"]
    last = tgt is not None
    pnames = _PRM[kind]
    pvals = [prm[n] for n in pnames]

    def body(*refs):
        it = iter(refs)
        xin_ref, win_ref, wout_ref, lng_ref, lnb_ref, khT_ref, vh_ref = (next(it) for _ in range(7))
        p = {n: next(it) for n in pnames}
        tgt_ref = next(it) if last else None
        xout_ref, proj_ref, z_ref = next(it), next(it), next(it)
        loss_ref = next(it) if last else None
        rest = list(it)
        i = pl.program_id(0)
        x = xin_ref[...]
        proj_ref[...] = _mm(x, win_ref[...])

        if kind == 0:
            gu, _, _, _, _, mixed = _gmlp_core(proj_ref[:, 0:TOK], proj_ref[:, TOK:2 * TOK], p["wtri"], p["bcolb"])
            tok = gu * mixed
        elif kind == 1:
            st_save, st_ref, states_s, u_s, qdec_s, kend_s, v_s, a_s, oint_s = rest

            @pl.when(i == 0)
            def _():
                st_ref[...] = jnp.zeros_like(st_ref)

            st_save[0] = st_ref[...]
            v = proj_ref[:, 2 * TOK:3 * TOK]
            hp = _hgrn_parallel(proj_ref[:, 0:TOK], proj_ref[:, TOK:2 * TOK], p["lb"][...])
            qdec_s[...] = hp["q_dec"]
            kend_s[...] = hp["k_end"]
            v_s[...] = v
            a_s[...] = hp["a"]
            o_intra = _hgrn_intra(hp["q_dec"], hp["k_inv"], v)
            _hgrn_inter_fwd(qdec_s, kend_s, v_s, a_s, oint_s, st_ref, states_s, u_s)
            on, _ = _rms(o_intra + oint_s[...])
            tok = on * p["ng"][...]
        elif kind == 2:
            pc_save, pcar = rest

            @pl.when(i == 0)
            def _():
                pcar[...] = jnp.zeros_like(pcar)

            pc_save[0] = pcar[...]
            pp = proj_ref[:, 0:TOK]
            _, _, y = _pool_core(pp, pcar[...], i * TS, p["wbd"])
            pcar[...] = pp[TS - 2 * SUB:, :]
            tok = y * p["scale"][...]
        else:
            cc_save, hc_save, ccar, hcar = rest

            @pl.when(i == 0)
            def _():
                ccar[...] = jnp.zeros_like(ccar)
                hcar[...] = jnp.zeros_like(hcar)

            cc_save[0] = ccar[...]
            hc_save[0] = hcar[...]
            xb = proj_ref[:, 0:TOK]
            lc = _lru_core(xb, ccar[...], i * TS, p)
            P, B = _scan_fwd(lc["a"], lc["bt"])
            tok = P * hcar[SUB - 1:SUB, :] + B
            ccar[...] = xb[TS - SUB:, :]
            hcar[...] = tok[TS - SUB:, :]

        xo, _ = _xattn_fwd(proj_ref[:, off["qx"]:off["qx"] + XW], khT_ref, vh_ref)
        gate = proj_ref[:, off["gate"]:off["gate"] + D_MODEL]
        mixed = jnp.concatenate([tok, xo], axis=-1) * (gate * _sig(gate))
        z = ALPHA * x + _mm(mixed, wout_ref[...])
        z_ref[...] = z
        xhat, _ = _ln(z)
        xout = xhat * lng_ref[...] + lnb_ref[...]
        if last:
            e = xout - tgt_ref[...]
            xout_ref[...] = e * (1.0 / D_MODEL)
            es = _rowsum(e * e)
            tot = es[:, 0:LANE]
            for j in range(1, D_MODEL // LANE):
                tot = tot + es[:, j * LANE:(j + 1) * LANE]

            @pl.when(i == 0)
            def _():
                loss_ref[...] = jnp.zeros_like(loss_ref)

            loss_ref[0:1, :] += tot
        else:
            xout_ref[...] = xout

    tile = lambda w: pl.BlockSpec((TS, w), lambda i: (i, 0))
    in_arrays = [xin, w_in, w_out, lng, lnb, khT, vh] + pvals + ([tgt] if last else [])
    in_specs = [tile(D_MODEL)] + [_const_spec(a) for a in in_arrays[1:7 + len(pvals)]] + ([tile(D_MODEL)] if last else [])
    out_shape = [jax.ShapeDtypeStruct((S, D_MODEL), f32), jax.ShapeDtypeStruct((S, W), f32), jax.ShapeDtypeStruct((S, D_MODEL), f32)]
    out_specs = [tile(D_MODEL), tile(W), tile(D_MODEL)]
    if last:
        out_shape.append(jax.ShapeDtypeStruct((SUB, LANE), f32))
        out_specs.append(_acc_spec((SUB, LANE)))
    scratch = []
    save = lambda *s: (jax.ShapeDtypeStruct((nt,) + s, f32), pl.BlockSpec((1,) + s, lambda i, _n=len(s): (i,) + (0,) * _n))
    if kind == 1:
        sh, sp = save(NH, HD, HD)
        out_shape.append(sh)
        out_specs.append(sp)
        scratch = ([pltpu.VMEM((NH, HD, HD), f32)] + [pltpu.VMEM((TS // CHUNK, NH, HD, HD), f32)] * 2
                   + [pltpu.VMEM((TS, TOK), f32)] * 5)
    elif kind == 2:
        sh, sp = save(2 * SUB, TOK)
        out_shape.append(sh)
        out_specs.append(sp)
        scratch = [pltpu.VMEM((2 * SUB, TOK), f32)]
    elif kind == 3:
        for _ in range(2):
            sh, sp = save(SUB, TOK)
            out_shape.append(sh)
            out_specs.append(sp)
        scratch = [pltpu.VMEM((SUB, TOK), f32)] * 2
    return pl.pallas_call(body, name=f"fwd_layer{kind}", grid=(nt,), in_specs=in_specs, out_specs=out_specs,
                          out_shape=out_shape, scratch_shapes=scratch, compiler_params=_params())(*in_arrays)


def _small_grad_shapes(kind):
    if kind == 0:
        return dict(dwtri=(NH, HD, HD), dbacc=(NH, HD, HD))
    if kind == 1:
        return dict(dlb=(SUB, TOK), dng=(SUB, TOK))
    if kind == 2:
        return dict(dwbd=(TOK, TOK), dscale=(SUB, TOK))
    return dict(dcw=(SUB, TOK), dvec=(SUB, TOK), dwgx=(NH, HD, HD), dwga=(NH, HD, HD))


def _bwd_layer(kind, dxout, z, proj, w_inT, w_outT, lng, kh, khT, vh, vhT, prm, saves):
    S = dxout.shape[0]
    TS = _TS[kind]
    nt = S // TS
    off = _OFFS[kind]
    W = off["W"]
    pnames = _PRM[kind]
    pvals = [prm[n] for n in pnames]
    sg_shapes = _small_grad_shapes(kind)
    sg_names = list(sg_shapes)
    n_saves = len(saves)

    def body(*refs):
        it = iter(refs)
        dxo_ref, z_ref, proj_ref, winT_ref, woutT_ref, lng_ref, kh_ref, khT_ref, vh_ref, vhT_ref = (next(it) for _ in range(10))
        p = {n: next(it) for n in pnames}
        sv = [next(it) for _ in range(n_saves)]
        dxin_ref, dproj_ref, mixed_ref, dy_ref, dln_ref, dk_ref, dv_ref = (next(it) for _ in range(7))
        sg = {n: next(it) for n in sg_names}
        rest = list(it)
        step = pl.program_id(0)
        i = nt - 1 - step

        @pl.when(step == 0)
        def _():
            dln_ref[...] = jnp.zeros_like(dln_ref)
            dk_ref[...] = jnp.zeros_like(dk_ref)
            dv_ref[...] = jnp.zeros_like(dv_ref)
            for n in sg_names:
                sg[n][...] = jnp.zeros_like(sg[n])

        dxo = dxo_ref[...]
        xhat, rstd = _ln(z_ref[...])
        dln_ref[0:1, :] += _rowsum(dxo * xhat)
        dln_ref[1:2, :] += _rowsum(dxo)
        dz = _ln_bwd(dxo * lng_ref[...], xhat, rstd)
        dyb = dz.astype(bf16)
        dy_ref[...] = dyb
        dmixed = _mm(dyb, woutT_ref[...])

        aux = {}
        if kind == 0:
            u_raw, v_raw = proj_ref[:, 0:TOK], proj_ref[:, TOK:2 * TOK]
            gu, tu, tv, vns, rstds, mx = _gmlp_core(u_raw, v_raw, p["wtri"], p["bcolb"])
            tok = gu * mx
        elif kind == 1:
            st_save, = sv
            (dst_ref, fst_ref, states_s, dsts_s, u_s, qdec_s, kend_s, v_s, a_s, oint_s, do_s, dqdec_s, dkend_s, dv_s,
             dgl_s) = rest

            @pl.when(step == 0)
            def _():
                dst_ref[...] = jnp.zeros_like(dst_ref)

            fst_ref[...] = st_save[0]
            v = proj_ref[:, 2 * TOK:3 * TOK]
            hp = _hgrn_parallel(proj_ref[:, 0:TOK], proj_ref[:, TOK:2 * TOK], p["lb"][...])
            qdec_s[...] = hp["q_dec"]
            kend_s[...] = hp["k_end"]
            v_s[...] = v
            a_s[...] = hp["a"]
            o_intra = _hgrn_intra(hp["q_dec"], hp["k_inv"], v)
            _hgrn_inter_fwd(qdec_s, kend_s, v_s, a_s, oint_s, fst_ref, states_s, u_s)
            o = o_intra + oint_s[...]
            on, rs = _rms(o)
            tok = on * p["ng"][...]
            aux = dict(hp=hp, v=v, o=o, on=on, rs=rs)
        elif kind == 2:
            pc_save, = sv
            dpcar, = rest
            pp = proj_ref[:, 0:TOK]
            diff, inv_cnt, y = _pool_core(pp, pc_save[0], i * TS, p["wbd"])
            tok = y * p["scale"][...]
        else:
            cc_save, hc_save = sv
            dccar, gcar = rest
            xb = proj_ref[:, 0:TOK]
            lc = _lru_core(xb, cc_save[0], i * TS, p)
            P, B = _scan_fwd(lc["a"], lc["bt"])
            hin = hc_save[0, SUB - 1:SUB, :]
            tok = P * hin + B

        xo, ps = _xattn_fwd(proj_ref[:, off["qx"]:off["qx"] + XW], khT_ref, vh_ref)
        gate = proj_ref[:, off["gate"]:off["gate"] + D_MODEL]
        sgm = _sig(gate)
        sgate = gate * sgm
        cat = jnp.concatenate([tok, xo], axis=-1)
        mixed_ref[...] = (cat * sgate).astype(bf16)
        dcat = dmixed * sgate
        dproj_ref[:, off["gate"]:off["gate"] + D_MODEL] = (dmixed * cat * (sgm * (1.0 + gate * (1.0 - sgm)))).astype(bf16)
        dtok = dcat[:, 0:TOK]
        dxo_att = dcat[:, TOK:]

        qx = proj_ref[:, off["qx"]:off["qx"] + XW]
        dqx = jnp.zeros((TS, XW), f32)
        for h in range(XHEADS):
            dp = _mm(dxo_att, vhT_ref[h])
            ds = ps[h] * (dp - jnp.sum(dp * ps[h], axis=-1, keepdims=True)) * (XDIM ** -0.5)
            dqx = dqx + _mm(ds, kh_ref[h])
            dk_ref[h] += _mm_tn(ds, qx)
            dv_ref[h] += _mm_tn(ps[h], dxo_att)
        dproj_ref[:, off["qx"]:off["qx"] + XW] = dqx.astype(bf16)

        if kind == 0:
            tril = lax.broadcasted_iota(jnp.int32, (HD, HD), 1) <= lax.broadcasted_iota(jnp.int32, (HD, HD), 0)
            dgu = dtok * mx
            dmx = dtok * gu
            dgvs = []
            for h in range(NH):
                dmh = dmx[:, _hs(h)]
                blks = []
                for n in range(TS // HD):
                    rs_ = slice(n * HD, (n + 1) * HD)
                    blks.append(_mm(p["wtriT"][h], dmh[rs_]))
                    sg["dwtri"][h] += jnp.where(tril, _mm_nt(dmh[rs_], vns[h][rs_]), 0.0)
                    sg["dbacc"][h] += dmh[rs_]
                dgvs.append(_ln_bwd(jnp.concatenate(blks, axis=0), vns[h], rstds[h]))
            dgv = jnp.concatenate(dgvs, axis=-1)
            dproj_ref[:, 0:TOK] = (dgu * _gelu_grad(u_raw, tu)).astype(bf16)
            dproj_ref[:, TOK:2 * TOK] = (dgv * _gelu_grad(v_raw, tv)).astype(bf16)
        elif kind == 1:
            hp, v, o, on, rs = aux["hp"], aux["v"], aux["o"], aux["on"], aux["rs"]
            ng = p["ng"][...]
            sg["dng"][0:1, :] += _rowsum(dtok * on)
            dn = dtok * ng
            dos = []
            for h in range(NH):
                oh, r = o[:, _hs(h)], rs[h]
                dos.append(r * (dn[:, _hs(h)] - oh * (r * r) * _lmean(dn[:, _hs(h)] * oh)))
            do = jnp.concatenate(dos, axis=-1)
            do_s[...] = do
            _, tri = _chunk_mats(HD)
            dqd, dki, dvi = [], [], []
            for h in range(NH):
                bq, bk, bv = [], [], []
                for b in range(TS // HD):
                    rs_ = slice(b * HD, (b + 1) * HD)
                    qd, ki = hp["q_dec"][rs_, _hs(h)], hp["k_inv"][rs_, _hs(h)]
                    sc = jnp.where(tri, _mm_nt(qd, ki), 0.0)
                    dsc = jnp.where(tri, _mm_nt(do[rs_, _hs(h)], v[rs_, _hs(h)]), 0.0)
                    bv.append(_mm_tn(sc, do[rs_, _hs(h)]))
                    bq.append(_mm(dsc, ki))
                    bk.append(_mm_tn(dsc, qd))
                dqd.append(jnp.concatenate(bq, axis=0))
                dki.append(jnp.concatenate(bk, axis=0))
                dvi.append(jnp.concatenate(bv, axis=0))
            dqdec_s[...] = jnp.concatenate(dqd, axis=-1)
            dk_inv = jnp.concatenate(dki, axis=-1)
            dv_s[...] = jnp.concatenate(dvi, axis=-1)
            row16 = lax.broadcasted_iota(jnp.int32, (CHUNK, HD), 0)

            nch = TS // CHUNK
            for c in range(nch):
                for h in range(NH):
                    u_s[c, h] = _mm_tn(do_s[_cs(c), _hs(h)], qdec_s[_cs(c), _hs(h)])
            for h in range(NH):
                dst = dst_ref[h]
                for c in reversed(range(nch)):
                    dsts_s[c, h] = dst
                    dst = dst * a_s[c * CHUNK:c * CHUNK + 1, _hs(h)] + u_s[c, h]
                dst_ref[h] = dst
            for c in range(nch):
                for h in range(NH):
                    stp = states_s[c, h]
                    dst = dsts_s[c, h]
                    dqdec_s[_cs(c), _hs(h)] += _mm(do_s[_cs(c), _hs(h)], stp)
                    dkend_s[_cs(c), _hs(h)] = _mm(v_s[_cs(c), _hs(h)], dst)
                    dv_s[_cs(c), _hs(h)] += _mm_nt(kend_s[_cs(c), _hs(h)], dst)
                    da = jnp.sum(dst * stp, axis=0, keepdims=True) * a_s[c * CHUNK:c * CHUNK + 1, _hs(h)]
                    dgl_s[_cs(c), _hs(h)] = jnp.where(row16 == 0, jnp.broadcast_to(da, (CHUNK, HD)), 0.0)
            dq_dec = dqdec_s[...]
            dk_end = dkend_s[...]
            same, _ = _chunk_mats(TS)
            triT = jnp.logical_and(same, lax.broadcasted_iota(jnp.int32, (TS, TS), 1) >= lax.broadcasted_iota(jnp.int32, (TS, TS), 0))
            dg = dq_dec * hp["q_dec"] - dk_inv * hp["k_inv"] - dk_end * hp["k_end"]
            dk = dk_inv * hp["eng"] + dk_end * hp["ee"]
            dglr = dk_end * hp["k_end"] + dgl_s[...]
            dlogf = _mm_sel(triT, dg) + _mm_sel(same, dglr)
            df = dlogf / hp["f"] - dk
            lb = p["lb"][...]
            sg["dlb"][0:1, :] += _rowsum(df * (1.0 - hp["sgm"]))
            q_raw = proj_ref[:, 0:TOK]
            dproj_ref[:, 0:TOK] = (dq_dec * hp["eg"] * (hp["sq"] * (1.0 + q_raw * (1.0 - hp["sq"])))).astype(bf16)
            dproj_ref[:, TOK:2 * TOK] = (df * (1.0 - lb) * hp["sgm"] * (1.0 - hp["sgm"])).astype(bf16)
            dproj_ref[:, 2 * TOK:3 * TOK] = dv_s[...].astype(bf16)
        elif kind == 2:
            @pl.when(step == 0)
            def _():
                dpcar[...] = jnp.zeros_like(dpcar)

            sg["dscale"][0:1, :] += _rowsum(dtok * y)
            dyp = dtok * p["scale"][...]
            sg["dwbd"][...] += _mm_tn(diff, dyp)
            ddiff = _mm(dyp, p["wbdT"][...])
            q = ddiff * inv_cnt
            ext = jnp.concatenate([q, dpcar[...]], axis=0)
            n = TS + 2 * SUB
            r1 = ext + pltpu.roll(ext, n - 1, 0)
            r2 = r1 + pltpu.roll(r1, n - 2, 0)
            r3 = r2 + pltpu.roll(r2, n - 4, 0)
            r4 = r3 + pltpu.roll(r3, n - 8, 0)
            dproj_ref[:, 0:TOK] = (_pool_pick(r1, r2, r3, r4)[:TS] - ddiff).astype(bf16)
            dpcar[...] = q[0:2 * SUB, :]
        else:
            @pl.when(step == 0)
            def _():
                dccar[...] = jnp.zeros_like(dccar)
                gcar[...] = jnp.zeros_like(gcar)

            a, mult, gx, ga, xc = lc["a"], lc["mult"], lc["gx"], lc["ga"], lc["xc"]
            row = lax.broadcasted_iota(jnp.int32, (TS, TOK), 0)
            an = jnp.where(row == TS - 1, 1.0, pltpu.roll(a, TS - 1, 0))
            Pb, Bb = _scan_bwd(an, dtok)
            lam = Pb * gcar[0:1, :] + Bb
            gcar[...] = (a * lam)[0:SUB, :]
            hprev = jnp.where(row == 0, jnp.broadcast_to(hin, (TS, TOK)), pltpu.roll(tok, 1, 0))
            dmult = lam * gx * xc
            dgx = lam * mult * xc
            dxc = lam * mult * gx
            dla = lam * hprev * a - jnp.where(lc["first"], 0.0, dmult * a * a / mult)
            sp = lc["sp"]
            dga = -LRU_C * sp * dla
            dsp = _rowsum(-LRU_C * ga * dla)
            sg["dvec"][0:1, :] += dsp * (-_sig(-p["ap"][...]))
            dpx = dgx * gx * (1.0 - gx)
            dpa = dga * ga * (1.0 - ga)
            sg["dvec"][1:2, :] += _rowsum(dpx)
            sg["dvec"][2:3, :] += _rowsum(dpa)
            dxcs = []
            for h in range(NH):
                dxcs.append(_mm(dpx[:, _hs(h)], p["wgxT"][h]) + _mm(dpa[:, _hs(h)], p["wgaT"][h]))
                sg["dwgx"][h] += _mm_tn(xc[:, _hs(h)], dpx[:, _hs(h)])
                sg["dwga"][h] += _mm_tn(xc[:, _hs(h)], dpa[:, _hs(h)])
            dxc = dxc + jnp.concatenate(dxcs, axis=-1)
            sg["dvec"][3:4, :] += _rowsum(dxc)
            sg["dcw"][3:4, :] += _rowsum(dxc * xb)
            sg["dcw"][2:3, :] += _rowsum(dxc * lc["x1"])
            sg["dcw"][1:2, :] += _rowsum(dxc * lc["x2"])
            sg["dcw"][0:1, :] += _rowsum(dxc * lc["x3"])
            ext = jnp.concatenate([dxc, dccar[...]], axis=0)
            n = TS + SUB
            cw = p["cw"]
            dproj_ref[:, 0:TOK] = (cw[3:4, :] * dxc + cw[2:3, :] * pltpu.roll(ext, n - 1, 0)[:TS]
                                   + cw[1:2, :] * pltpu.roll(ext, n - 2, 0)[:TS]
                                   + cw[0:1, :] * pltpu.roll(ext, n - 3, 0)[:TS]).astype(bf16)
            dccar[...] = dxc[0:SUB, :]

        dxin_ref[...] = ALPHA * dz + _mm(dproj_ref[...], winT_ref[...])

    rtile = lambda w: pl.BlockSpec((TS, w), lambda s: (nt - 1 - s, 0))
    consts = [w_inT, w_outT, lng, kh, khT, vh, vhT] + pvals
    in_arrays = [dxout, z, proj] + consts + list(saves)
    in_specs = [rtile(D_MODEL), rtile(D_MODEL), rtile(W)] + [_const_spec(a) for a in consts]
    for a in saves:
        in_specs.append(pl.BlockSpec((1,) + a.shape[1:], lambda s, _n=a.ndim - 1: (nt - 1 - s,) + (0,) * _n))
    out_shape = [jax.ShapeDtypeStruct((S, D_MODEL), f32), jax.ShapeDtypeStruct((S, W), bf16),
                 jax.ShapeDtypeStruct((S, D_MODEL), bf16), jax.ShapeDtypeStruct((S, D_MODEL), bf16),
                 jax.ShapeDtypeStruct((SUB, D_MODEL), f32), jax.ShapeDtypeStruct((XHEADS, XW, XW), f32),
                 jax.ShapeDtypeStruct((XHEADS, XW, XW), f32)]
    out_specs = [rtile(D_MODEL), rtile(W), rtile(D_MODEL), rtile(D_MODEL), _acc_spec((SUB, D_MODEL)),
                 _acc_spec((XHEADS, XW, XW)), _acc_spec((XHEADS, XW, XW))]
    for n in sg_names:
        out_shape.append(jax.ShapeDtypeStruct(sg_shapes[n], f32))
        out_specs.append(_acc_spec(sg_shapes[n]))
    if kind == 1:
        scratch = ([pltpu.VMEM((NH, HD, HD), f32)] * 2 + [pltpu.VMEM((TS // CHUNK, NH, HD, HD), f32)] * 3
                   + [pltpu.VMEM((TS, TOK), f32)] * 10)
    elif kind == 2:
        scratch = [pltpu.VMEM((2 * SUB, TOK), f32)]
    elif kind == 3:
        scratch = [pltpu.VMEM((SUB, TOK), f32)] * 2
    else:
        scratch = []
    outs = pl.pallas_call(body, name=f"bwd_layer{kind}", grid=(nt,), in_specs=in_specs, out_specs=out_specs,
                          out_shape=out_shape, scratch_shapes=scratch, compiler_params=_params())(*in_arrays)
    return outs[:7], dict(zip(sg_names, outs[7:]))


def _prep(mem, w_kv, logits):
    def body(mem_ref, w_ref, lg_ref, kh_ref, khT_ref, vh_ref, vhT_ref, p_ref):
        kv = _mm(mem_ref[...], w_ref[...])
        k, v = kv[:, 0:XW], kv[:, XW:]
        kT, vT = k.T, v.T
        col = lax.broadcasted_iota(jnp.int32, (XW, XW), 1) // XDIM
        row = lax.broadcasted_iota(jnp.int32, (XW, XW), 0) // XDIM
        for h in range(XHEADS):
            kh_ref[h] = jnp.where(col == h, k, 0.0).astype(bf16)
            vh_ref[h] = jnp.where(col == h, v, 0.0).astype(bf16)
            khT_ref[h] = jnp.where(row == h, kT, 0.0).astype(bf16)
            vhT_ref[h] = jnp.where(row == h, vT, 0.0).astype(bf16)
        lg = lg_ref[...]
        e = jnp.exp(lg - jnp.max(lg, axis=0, keepdims=True))
        p_ref[...] = e / jnp.sum(e, axis=0, keepdims=True)

    vm = pl.BlockSpec(memory_space=pltpu.VMEM)
    hs = jax.ShapeDtypeStruct((XHEADS, XW, XW), bf16)
    return pl.pallas_call(body, name="prep_memory", in_specs=[vm] * 3, out_specs=[vm] * 5,
                          out_shape=[hs, hs, hs, hs, jax.ShapeDtypeStruct(logits.shape, f32)])(mem, w_kv, logits)


def _kv_bwd(mem, dks, dvs):
    def body(mem_ref, *refs):
        out_ref = refs[-1]
        col = lax.broadcasted_iota(jnp.int32, (XW, XW), 1) // XDIM
        dk = jnp.zeros((XW, XW), f32)
        dv = jnp.zeros((XW, XW), f32)
        for l in range(DEPTH):
            for h in range(XHEADS):
                dk = dk + jnp.where(col == h, refs[l][h], 0.0)
                dv = dv + jnp.where(col == h, refs[DEPTH + l][h], 0.0)
        out_ref[:, 0:XW] = _mm_tn(mem_ref[...], dk)
        out_ref[:, XW:] = _mm_tn(mem_ref[...], dv)

    vm = pl.BlockSpec(memory_space=pltpu.VMEM)
    return pl.pallas_call(body, name="kv_bwd", in_specs=[vm] * (1 + 2 * DEPTH), out_specs=vm,
                          out_shape=jax.ShapeDtypeStruct((D_MODEL, 2 * XW), f32))(mem, *dks, *dvs)


def _tn_gemm(a, b, name, nb):
    S, M = a.shape
    N = b.shape[1]
    NB = N // nb
    nk = S // TK

    def body(a_ref, b_ref, o_ref):
        @pl.when(pl.program_id(1) == 0)
        def _():
            o_ref[...] = jnp.zeros_like(o_ref)

        o_ref[...] += _mm_tn(a_ref[...], b_ref[...])

    return pl.pallas_call(body, name=name, grid=(nb, nk),
                          in_specs=[pl.BlockSpec((TK, M), lambda j, k: (k, 0)), pl.BlockSpec((TK, NB), lambda j, k: (k, j))],
                          out_specs=pl.BlockSpec((M, NB), lambda j, k: (0, j)),
                          out_shape=jax.ShapeDtypeStruct((M, N), f32),
                          compiler_params=pltpu.CompilerParams(dimension_semantics=("parallel", "arbitrary"),
                                                               vmem_limit_bytes=VMEM_LIMIT))(a, b)


def _rows_block(R, mult=16, cap=1024):
    best = R
    for d in range(mult, min(R, cap) + 1, mult):
        if R % d == 0:
            best = d
    return best


def _add_sibling(part, got, c, name):
    _, H, C = got.shape
    br = _rows_block(H)
    nb = H // br

    def body(c_ref, a_ref, b_ref, o_ref, ob_ref):
        s = a_ref[...] + b_ref[...]
        o_ref[...] = s
        ob_ref[...] = s.astype(bf16)

    mine = pl.BlockSpec((1, br, C), lambda s, i, c_ref: (s, c_ref[0] * nb + i, 0))
    spec = pl.BlockSpec((1, br, C), lambda s, i, c_ref: (s, i, 0))
    gs = pltpu.PrefetchScalarGridSpec(num_scalar_prefetch=1, grid=(4, nb), in_specs=[mine, spec], out_specs=[spec, spec])
    return pl.pallas_call(body, name=name, grid_spec=gs,
                          out_shape=[jax.ShapeDtypeStruct((4, H, C), f32), jax.ShapeDtypeStruct((4, H, C), bf16)],
                          compiler_params=pltpu.CompilerParams(dimension_semantics=("parallel", "parallel"),
                                                               vmem_limit_bytes=VMEM_LIMIT))(c, part, got)


def _add_chips(q32, r, jc, name):
    _, H, C = q32.shape
    br = _rows_block(H)
    nb = H // br

    def body(jc_ref, o_ref, r_ref, out_ref):
        out_ref[...] = ((o_ref[0] + r_ref[0].astype(f32)) + r_ref[1].astype(f32)) + r_ref[2].astype(f32)

    gs = pltpu.PrefetchScalarGridSpec(
        num_scalar_prefetch=1, grid=(nb,),
        in_specs=[pl.BlockSpec((1, br, C), lambda i, jc_ref: (jc_ref[0], i, 0)), pl.BlockSpec((3, br, C), lambda i, jc_ref: (0, i, 0))],
        out_specs=pl.BlockSpec((br, C), lambda i, jc_ref: (jc_ref[1] * nb + i, 0)))
    return pl.pallas_call(body, name=name, grid_spec=gs, out_shape=jax.ShapeDtypeStruct((2 * H, C), f32),
                          compiler_params=_params("parallel"))(jc, q32, r)


def _adamw(w, g, m, v, name):
    R, C = w.shape
    br = _rows_block(R, mult=SUB, cap=512)
    c1 =1.0 / (1.0 - ADAM_B1 ** ADAM_STEP)
    c2 = 1.0 / (1.0 - ADAM_B2 ** ADAM_STEP)

    def body(w_ref, g_ref, m_ref, v_ref, d_ref, nm_ref, nv_ref):
        g_ = g_ref[...]
        nm = ADAM_B1 * m_ref[...] + (1.0 - ADAM_B1) * g_
        nv = ADAM_B2 * v_ref[...] + (1.0 - ADAM_B2) * (g_ * g_)
        nm_ref[...] = nm
        nv_ref[...] = nv
        d_ref[...] = -ADAM_LR * ((nm * c1) / (jnp.sqrt(nv * c2) + ADAM_EPS) + ADAM_WD * w_ref[...])

    spec = pl.BlockSpec((br, C), lambda i: (i, 0))
    sh = jax.ShapeDtypeStruct((R, C), f32)
    return pl.pallas_call(body, name=name, grid=(R // br,), in_specs=[spec] * 4, out_specs=[spec] * 3,
                          out_shape=[sh, sh, sh], compiler_params=_params("parallel"))(w, g, m, v)


def _small_finish(dbacc, p_soft, dlb):
    def body(db_ref, p_ref, dlb_ref, dbs_ref, dlg_ref):
        lane = lax.broadcasted_iota(jnp.int32, (HD, HD), 1)
        acc = jnp.zeros((HD, HD), f32)
        for h in range(NH):
            acc = acc + jnp.where(lane == h, jnp.sum(db_ref[h], axis=-1, keepdims=True), 0.0)
        dbs_ref[...] = acc
        p = p_ref[...]
        p1 = p[1:2, :]
        rowi = lax.broadcasted_iota(jnp.int32, p.shape, 0)
        dlg_ref[...] = dlb_ref[0:1, :] * p1 * (jnp.where(rowi == 1, 1.0, 0.0) - p)

    vm = pl.BlockSpec(memory_space=pltpu.VMEM)
    return pl.pallas_call(body, name="small_finish", in_specs=[vm] * 3, out_specs=[vm] * 2,
                          out_shape=[jax.ShapeDtypeStruct((HD, HD), f32), jax.ShapeDtypeStruct(p_soft.shape, f32)])(dbacc, p_soft, dlb)


def _where_am_i():
    return lax.axis_index("x"), lax.axis_index("y"), lax.axis_index("c")


MAX_PIECES = 8


def _nchunks(rows, mult):
    for n in range(MAX_PIECES, 0, -1):
        if rows % (n * mult) == 0:
            return n
    return 1


def _all_gather_xy(shard, jshard):
    R, C = shard.shape
    H = R // 2
    NP = _nchunks(H, 16)
    PR = H // NP

    def body(sh_ref, placed_ref, out_ref, send_sems, recv_sems):
        del placed_ref
        x, y, c = _where_am_i()
        j = 2 * x + y
        sib = (x, y, 1 - c)
        chips = [(1 - x, y), (x, 1 - y), (1 - x, 1 - y)]

        def cp(k, src, dst, to):
            return pltpu.make_async_remote_copy(src_ref=src, dst_ref=dst, send_sem=send_sems.at[k], recv_sem=recv_sems.at[k],
                                                device_id=to, device_id_type=MESH)

        def rows(half, q):
            return pl.ds(half * H + q * PR, PR)

        first = []
        for q in range(NP):
            for k, (cx, cy) in enumerate(chips):
                d = cp(k * NP + q, sh_ref.at[rows(c, q)], out_ref.at[j, rows(c, q)], (cx, cy, c))
                d.start()
                first.append(d)
        passed = []
        for q in range(NP):
            for k, (cx, cy) in enumerate(chips):
                blk = out_ref.at[2 * cx + cy, rows(c, q)]
                cp(k * NP + q, blk, blk, (cx, cy, c)).wait_recv()
                fwd = cp((3 + k) * NP + q, blk, blk, sib)
                fwd.start()
                passed.append(fwd)
        for q in range(NP):
            for k, (cx, cy) in enumerate(chips):
                blk = out_ref.at[2 * cx + cy, rows(1 - c, q)]
                cp((3 + k) * NP + q, blk, blk, sib).wait_recv()
        for d in first + passed:
            d.wait_send()

    placed = lax.dynamic_update_slice(jnp.zeros((4, R, C), shard.dtype), shard[None], (jshard, 0, 0))
    anyspec = pl.BlockSpec(memory_space=pl.ANY)
    return pl.pallas_call(body, name="all_gather_weights", in_specs=[anyspec, anyspec], out_specs=anyspec,
                          out_shape=jax.ShapeDtypeStruct((4, R, C), shard.dtype), input_output_aliases={1: 0},
                          scratch_shapes=[pltpu.SemaphoreType.DMA((6 * NP,)), pltpu.SemaphoreType.DMA((6 * NP,))],
                          compiler_params=pltpu.CompilerParams(has_side_effects=True))(shard, placed)


def _rs_swap_halves(part):
    _, R, C = part.shape
    H = R // 2
    NP = _nchunks(H, SUB)
    PR = H // NP

    def body(p_ref, got_ref, send_sem, recv_sem):
        x, y, c = _where_am_i()

        def remote(src, dst):
            return pltpu.make_async_remote_copy(src_ref=src, dst_ref=dst, send_sem=send_sem, recv_sem=recv_sem,
                                                device_id=(x, y, 1 - c), device_id_type=MESH)

        for s in range(4):
            for q in range(NP):
                remote(p_ref.at[s, pl.ds((1 - c) * H + q * PR, PR)], got_ref.at[s, pl.ds(q * PR, PR)]).start()
        remote(got_ref, got_ref).wait()

    anyspec = pl.BlockSpec(memory_space=pl.ANY)
    return pl.pallas_call(body, name="rs_swap_halves", in_specs=[anyspec], out_specs=anyspec,
                          out_shape=jax.ShapeDtypeStruct((4, H, C), part.dtype),
                          scratch_shapes=[pltpu.SemaphoreType.DMA, pltpu.SemaphoreType.DMA],
                          compiler_params=pltpu.CompilerParams(has_side_effects=True))(part)


def _rs_to_owners(qb):
    _, H, C = qb.shape
    NP = _nchunks(H, 16)
    PR = H // NP

    def body(qb_ref, got_ref, send_sems, recv_sems):
        x, y, c = _where_am_i()
        chips = [(1 - x, y), (x, 1 - y), (1 - x, 1 - y)]

        def remote(k, src, dst):
            cx, cy = chips[k]
            return pltpu.make_async_remote_copy(src_ref=src, dst_ref=dst, send_sem=send_sems.at[k], recv_sem=recv_sems.at[k],
                                                device_id=(cx, cy, c), device_id_type=MESH)

        for q in range(NP):
            for k, (cx, cy) in enumerate(chips):
                remote(k, qb_ref.at[2 * cx + cy, pl.ds(q * PR, PR)], got_ref.at[k, pl.ds(q * PR, PR)]).start()
        for k in range(3):
            remote(k, got_ref.at[k], got_ref.at[k]).wait()

    anyspec = pl.BlockSpec(memory_space=pl.ANY)
    return pl.pallas_call(body, name="rs_to_owners", in_specs=[anyspec], out_specs=anyspec,
                          out_shape=jax.ShapeDtypeStruct((3, H, C), bf16),
                          scratch_shapes=[pltpu.SemaphoreType.DMA((3,)), pltpu.SemaphoreType.DMA((3,))],
                          compiler_params=pltpu.CompilerParams(has_side_effects=True))(qb)


def _rs_join_halves(buf):
    H, C = buf.shape[0] // 2, buf.shape[1]
    NP = _nchunks(H, SUB)
    PR = H // NP

    def body(in_ref, out_ref, send_sem, recv_sem):
        del in_ref
        x, y, c = _where_am_i()

        def remote(rows):
            return pltpu.make_async_remote_copy(src_ref=out_ref.at[rows], dst_ref=out_ref.at[rows], send_sem=send_sem,
                                                recv_sem=recv_sem, device_id=(x, y, 1 - c), device_id_type=MESH)

        for q in range(NP):
            remote(pl.ds(c * H + q * PR, PR)).start()
        done = remote(pl.ds((1 - c) * H, H))
        done.wait_send()
        done.wait_recv()

    anyspec = pl.BlockSpec(memory_space=pl.ANY)
    return pl.pallas_call(body, name="rs_join_halves", in_specs=[anyspec], out_specs=anyspec,
                          out_shape=jax.ShapeDtypeStruct(buf.shape, buf.dtype), input_output_aliases={0: 0},
                          scratch_shapes=[pltpu.SemaphoreType.DMA, pltpu.SemaphoreType.DMA],
                          compiler_params=pltpu.CompilerParams(has_side_effects=True))(buf)


def _all_reduce_small(g):
    R, C = g.shape
    H = R // 2
    NP = _nchunks(H, SUB)
    PR = H // NP

    def body(g_ref, out_ref, sib_ref, chip_ref, send_sems, recv_sems):
        x, y, c = _where_am_i()
        j = 2 * x + y
        sib = (x, y, 1 - c)
        chips = [(1 - x, y), (x, 1 - y), (1 - x, 1 - y)]
        rows = pl.ds(pl.multiple_of(c * H, SUB), H)

        def cp(k, src, dst, to):
            return pltpu.make_async_remote_copy(src_ref=src, dst_ref=dst, send_sem=send_sems.at[k], recv_sem=recv_sems.at[k],
                                                device_id=to, device_id_type=MESH)

        def pieces(k, src, dst, to):
            for q in range(NP):
                cp(k, src.at[pl.ds(q * PR, PR)], dst.at[pl.ds(q * PR, PR)], to).start()

        for half in range(2):
            pieces(0, g_ref.at[pl.ds(half * H, H)], sib_ref.at[pl.ds(half * H, H)], sib)
        cp(0, g_ref, sib_ref, sib).wait()
        chip_ref[j] = g_ref[rows, :] + sib_ref[rows, :]
        for k, (cx, cy) in enumerate(chips):
            pieces(1 + k, chip_ref.at[j], chip_ref.at[j], (cx, cy, c))
        for k, (cx, cy) in enumerate(chips):
            blk = chip_ref.at[2 * cx + cy]
            cp(1 + k, blk, blk, (cx, cy, c)).wait()
        out_ref[rows, :] = ((chip_ref[0] + chip_ref[1]) + chip_ref[2]) + chip_ref[3]
        other = out_ref.at[pl.ds(pl.multiple_of((1 - c) * H, SUB), H)]
        pieces(4, out_ref.at[rows], out_ref.at[rows], sib)
        cp(4, other, other, sib).wait()

    vm = pl.BlockSpec(memory_space=pltpu.VMEM)
    return pl.pallas_call(body, name="all_reduce_small", in_specs=[vm], out_specs=vm,
                          out_shape=jax.ShapeDtypeStruct((R, C), f32),
                          scratch_shapes=[pltpu.VMEM((R, C), f32), pltpu.VMEM((4, H, C), f32),
                                          pltpu.SemaphoreType.DMA((5,)), pltpu.SemaphoreType.DMA((5,))],
                          compiler_params=pltpu.CompilerParams(has_side_effects=True, vmem_limit_bytes=VMEM_LIMIT))(g)


def _pack_flat(arrs, rows_mult):
    flat = jnp.concatenate([a.reshape(-1) for a in arrs])
    n = flat.shape[0]
    tot = -(-n // (rows_mult * LANE)) * rows_mult * LANE
    return jnp.pad(flat, (0, tot - n)).reshape(-1, LANE)


def _unpack_flat(buf, shapes):
    flat = buf.reshape(-1)
    out, o = [], 0
    for s in shapes:
        n = math.prod(s)
        out.append(flat[o:o + n].reshape(s))
        o += n
    return out


_BIG = ("mem_kv_w", "w_out", "a_w_in", "b_w_in", "c_w_in", "d_w_in")
BIG_ROWS_MULT = 256


def _pack_big_shards(sh):
    buf = jnp.concatenate([sh[n].reshape(-1, D_MODEL) for n in _BIG], axis=0)
    return jnp.pad(buf, ((0, -buf.shape[0] % BIG_ROWS_MULT), (0, 0)))


def _unpack_big_full(buf):
    o = 0
    out = {}
    r = XW * 2 * XW // D_MODEL
    out["mem_kv_w"] = buf[:, o:o + r].reshape(D_MODEL, 2 * XW)
    o += r
    r = DEPTH * (D_MODEL // 4)
    out["w_out"] = buf[:, o:o + r].reshape(4, DEPTH, D_MODEL // 4, D_MODEL).transpose(1, 0, 2, 3).reshape(DEPTH, D_MODEL, D_MODEL)
    o += r
    for n, kind in (("a_w_in", 0), ("b_w_in", 1), ("c_w_in", 2), ("d_w_in", 3)):
        W = _OFFS[kind]["W"]
        r = W // 4
        out[n] = buf[:, o:o + r].reshape(4, D_MODEL, W // 4).transpose(1, 0, 2).reshape(D_MODEL, W)
        o += r
    return out


def _pack_big_full(full):
    parts = [full["mem_kv_w"].reshape(4, -1, D_MODEL),
             full["w_out"].reshape(DEPTH, 4, D_MODEL // 4, D_MODEL).transpose(1, 0, 2, 3).reshape(4, -1, D_MODEL)]
    for n, kind in (("a_w_in", 0), ("b_w_in", 1), ("c_w_in", 2), ("d_w_in", 3)):
        W = _OFFS[kind]["W"]
        parts.append(full[n].reshape(D_MODEL, 4, W // 4).transpose(1, 0, 2).reshape(4, -1, D_MODEL))
    rows = sum(p.shape[1] for p in parts)
    parts.append(jnp.zeros((4, -rows % BIG_ROWS_MULT, D_MODEL), f32))
    return jnp.concatenate(parts, axis=1)


def _unpack_big_shards(buf, like):
    out, o = {}, 0
    for n in _BIG:
        r = like[n].size // D_MODEL
        out[n] = buf[o:o + r].reshape(like[n].shape)
        o += r
    return out


def _row8(v):
    v = v.reshape(-1, v.shape[-1])
    return jnp.pad(v, ((0, SUB - v.shape[0]), (0, 0)))


def kernel(x, mem, mem_kv_w, ln_g, ln_b, w_out, hgrn_lb_logits, a_w_in, a_w_s, a_b_s, b_w_in, b_norm_g, c_w_in, c_w_pool, c_scale, d_w_in, d_conv_w, d_conv_b, d_w_gx, d_b_gx, d_w_ga, d_b_ga, d_a_param, loss_target, m_mem_kv_w, m_ln_g, m_ln_b, m_w_out, m_hgrn_lb_logits, m_a_w_in, m_a_w_s, m_a_b_s, m_b_w_in, m_b_norm_g, m_c_w_in, m_c_w_pool, m_c_scale, m_d_w_in, m_d_conv_w, m_d_conv_b, m_d_w_gx, m_d_b_gx, m_d_w_ga, m_d_b_ga, m_d_a_param, v_mem_kv_w, v_ln_g, v_ln_b, v_w_out, v_hgrn_lb_logits, v_a_w_in, v_a_w_s, v_a_b_s, v_b_w_in, v_b_norm_g, v_c_w_in, v_c_w_pool, v_c_scale, v_d_w_in, v_d_conv_w, v_d_conv_b, v_d_w_gx, v_d_b_gx, v_d_w_ga, v_d_b_ga, v_d_a_param):
    names = ["mem_kv_w", "ln_g", "ln_b", "w_out", "hgrn_lb_logits", "a_w_in", "a_w_s", "a_b_s", "b_w_in", "b_norm_g", "c_w_in",
             "c_w_pool", "c_scale", "d_w_in", "d_conv_w", "d_conv_b", "d_w_gx", "d_b_gx", "d_w_ga", "d_b_ga", "d_a_param"]
    w = dict(mem_kv_w=mem_kv_w, ln_g=ln_g, ln_b=ln_b, w_out=w_out, hgrn_lb_logits=hgrn_lb_logits, a_w_in=a_w_in, a_w_s=a_w_s,
             a_b_s=a_b_s, b_w_in=b_w_in, b_norm_g=b_norm_g, c_w_in=c_w_in, c_w_pool=c_w_pool, c_scale=c_scale, d_w_in=d_w_in,
             d_conv_w=d_conv_w, d_conv_b=d_conv_b, d_w_gx=d_w_gx, d_b_gx=d_b_gx, d_w_ga=d_w_ga, d_b_ga=d_b_ga, d_a_param=d_a_param)
    m = dict(zip(names, [m_mem_kv_w, m_ln_g, m_ln_b, m_w_out, m_hgrn_lb_logits, m_a_w_in, m_a_w_s, m_a_b_s, m_b_w_in, m_b_norm_g,
                         m_c_w_in, m_c_w_pool, m_c_scale, m_d_w_in, m_d_conv_w, m_d_conv_b, m_d_w_gx, m_d_b_gx, m_d_w_ga,
                         m_d_b_ga, m_d_a_param]))
    v = dict(zip(names, [v_mem_kv_w, v_ln_g, v_ln_b, v_w_out, v_hgrn_lb_logits, v_a_w_in, v_a_w_s, v_a_b_s, v_b_w_in, v_b_norm_g,
                         v_c_w_in, v_c_w_pool, v_c_scale, v_d_w_in, v_d_conv_w, v_d_conv_b, v_d_w_gx, v_d_b_gx, v_d_w_ga,
                         v_d_b_ga, v_d_a_param]))
    xi, yi = lax.axis_index("x"), lax.axis_index("y")
    jshard = 2 * xi + yi
    x2 = x[0]
    mem2 = mem[0]
    tgt2 = loss_target[0]

    big_sh = {n: w[n] for n in _BIG}
    gathered = _all_gather_xy(_pack_big_shards(big_sh).astype(bf16), jshard)
    full = _unpack_big_full(gathered)
    w_in = [full["a_w_in"], full["b_w_in"], full["c_w_in"], full["d_w_in"]]
    w_inT = [a.T for a in w_in]
    w_outs = [full["w_out"][l] for l in range(DEPTH)]
    w_outT = [a.T for a in w_outs]

    def gather_small(shard):
        z = jnp.zeros((4, POOL_GROUP), f32)
        return lax.dynamic_update_slice(z, shard.reshape(1, POOL_GROUP), (jshard, 0))

    sm_sh = jnp.concatenate([gather_small(b_norm_g), gather_small(c_scale), gather_small(d_conv_b), gather_small(d_a_param)]
                            + [gather_small(d_conv_w[:, r]) for r in range(4)], axis=0)
    ci = lax.axis_index("c")
    sm_all = _all_reduce_small(_pack_flat([jnp.where(ci == 0, sm_sh, 0.0)], SUB * 2))
    sm = _unpack_flat(sm_all, [(8, 4 * POOL_GROUP)])[0]
    ng_full, scale_full, convb_full, ap_full = sm[0:1], sm[1:2], sm[2:3], sm[3:4]
    convw_full = sm[4:8]

    tril = jnp.tril(jnp.ones((HD, HD), bool))
    wtri = jnp.where(tril, a_w_s[0], 0.0)
    wbd = jnp.zeros((TOK, TOK), f32)
    for g in range(4):
        wbd = lax.dynamic_update_slice(wbd, c_w_pool[0, g], (g * POOL_GROUP, g * POOL_GROUP))
    kh, khT, vh, vhT, p_soft = _prep(mem2, full["mem_kv_w"], hgrn_lb_logits)
    prm = [
        dict(wtri=wtri.astype(bf16), wtriT=wtri.transpose(0, 2, 1).astype(bf16),
             bcolb=jnp.broadcast_to(a_b_s[0][:, :, None], (NH, HD, HD))),
        dict(lb=p_soft[1:2], ng=ng_full),
        dict(wbd=wbd.astype(bf16), wbdT=wbd.T.astype(bf16), scale=scale_full),
        dict(cw=_row8(convw_full), cb=convb_full, wgx=d_w_gx[0].astype(bf16), wgxT=d_w_gx[0].transpose(0, 2, 1).astype(bf16),
             bgx=d_b_gx.reshape(1, TOK), wga=d_w_ga[0].astype(bf16), wgaT=d_w_ga[0].transpose(0, 2, 1).astype(bf16),
             bga=d_b_ga.reshape(1, TOK), ap=ap_full),
    ]

    acts = []
    h = x2
    for l in range(DEPTH):
        outs = _fwd_layer(l, h, w_in[l], w_outs[l], ln_g[l:l + 1], ln_b[l:l + 1], khT, vh, prm[l],
                          tgt2 if l == DEPTH - 1 else None)
        nfix = 4 if l == DEPTH - 1 else 3
        acts.append(dict(xin=h, proj=outs[1], z=outs[2], saves=outs[nfix:]))
        if l == DEPTH - 1:
            loss_part = outs[3]
        h = outs[0]
    loss = lax.psum(0.5 / D_MODEL * jnp.sum(loss_part), ("x", "y", "c"))

    dh = h
    gfull = {}
    gw_out = [None] * DEPTH
    dln = [None] * DEPTH
    dks, dvs = [None] * DEPTH, [None] * DEPTH
    sgr = [None] * DEPTH
    for l in reversed(range(DEPTH)):
        a = acts[l]
        (dxin, dproj, mixedb, dyb, dln[l], dks[l], dvs[l]), sgr[l] = _bwd_layer(
            l, dh, a["z"], a["proj"], w_inT[l], w_outT[l], ln_g[l:l + 1], kh, khT, vh, vhT, prm[l], a["saves"])
        gfull["abcd"[l] + "_w_in"] = _tn_gemm(a["xin"], dproj, f"grad_w_in{l}", 2)
        gw_out[l] = _tn_gemm(mixedb, dyb, f"grad_w_out{l}", 1)
        dh = dxin
    grad_x = dh[None]
    gfull["w_out"] = jnp.stack(gw_out)
    gfull["mem_kv_w"] = _kv_bwd(mem2, dks, dvs)

    part = _pack_big_full(gfull)
    ci32 = ci.astype(jnp.int32)
    q32, qb = _add_sibling(part, _rs_swap_halves(part), ci32.reshape(1), "rs_add_sibling")
    jc = jnp.stack([jshard.astype(jnp.int32), ci32])
    gbig = _rs_join_halves(_add_chips(q32, _rs_to_owners(qb), jc, "rs_add_chips"))
    d_big, m_big, v_big = _adamw(_pack_big_shards(big_sh), gbig, _pack_big_shards({n: m[n] for n in _BIG}),
                                 _pack_big_shards({n: v[n] for n in _BIG}), "adamw_big")
    g_sh = _unpack_big_shards(gbig, big_sh)
    d_sh = _unpack_big_shards(d_big, big_sh)
    m_sh = _unpack_big_shards(m_big, big_sh)
    v_sh = _unpack_big_shards(v_big, big_sh)

    dbs, dlogits = _small_finish(sgr[0]["dbacc"], p_soft, sgr[1]["dlb"])
    gs = {
        "ln_g": jnp.concatenate([dln[l][0:1] for l in range(DEPTH)], axis=0),
        "ln_b": jnp.concatenate([dln[l][1:2] for l in range(DEPTH)], axis=0),
        "hgrn_lb_logits": dlogits,
        "a_w_s": sgr[0]["dwtri"][None],
        "a_b_s": dbs[:, 0:NH].T[None],
        "b_norm_g": sgr[1]["dng"][0:1],
        "c_w_pool": jnp.stack([sgr[2]["dwbd"][g * POOL_GROUP:(g + 1) * POOL_GROUP, g * POOL_GROUP:(g + 1) * POOL_GROUP]
                               for g in range(4)])[None],
        "c_scale": sgr[2]["dscale"][0:1],
        "d_conv_w": sgr[3]["dcw"][0:4][None],
        "d_conv_b": sgr[3]["dvec"][3:4],
        "d_w_gx": sgr[3]["dwgx"][None],
        "d_b_gx": sgr[3]["dvec"][1:2].reshape(1, NH, HD),
        "d_w_ga": sgr[3]["dwga"][None],
        "d_b_ga": sgr[3]["dvec"][2:3].reshape(1, NH, HD),
        "d_a_param": sgr[3]["dvec"][0:1],
    }
    small = [n for n in names if n not in _BIG]
    full_shapes = [gs[n].shape for n in small]
    gsum = dict(zip(small, _unpack_flat(_all_reduce_small(_pack_flat([gs[n] for n in small], BIG_ROWS_MULT)), full_shapes)))
    for n in ("b_norm_g", "c_scale", "d_conv_b", "d_a_param"):
        gsum[n] = lax.dynamic_slice(gsum[n], (0, jshard * POOL_GROUP), (1, POOL_GROUP))
    gsum["d_conv_w"] = lax.dynamic_slice(gsum["d_conv_w"], (0, 0, jshard * POOL_GROUP), (1, 4, POOL_GROUP))
    shapes = [w[n].shape for n in small]
    pk = lambda d: _pack_flat([d[n] for n in small], 2 * BIG_ROWS_MULT)
    d_s, m_s, v_s = _adamw(pk(w), pk(gsum), pk(m), pk(v), "adamw_small")
    d_sm = dict(zip(small, _unpack_flat(d_s, shapes)))
    m_sm = dict(zip(small, _unpack_flat(m_s, shapes)))
    v_sm = dict(zip(small, _unpack_flat(v_s, shapes)))

    grads = {**gsum, **g_sh}
    deltas = {**d_sm, **d_sh}
    new_m = {**m_sm, **m_sh}
    new_v = {**v_sm, **v_sh}
    return (loss, grad_x, *[grads[n] for n in names], *[deltas[n] for n in names], *[new_m[n] for n in names],
            *[new_v[n] for n in names])
```

```python
import functools
import math

import jax
import jax.numpy as jnp
from jax import lax
from jax.experimental import pallas as pl
from jax.experimental.pallas import tpu as pltpu

f32 = jnp.float32
bf16 = jnp.bfloat16
MM = bf16

D_MODEL = 1024
TOK = 768
XW = 256
XHEADS = 4
XDIM = 64
HD = 128
NH = TOK // HD
CHUNK = 16
POOL_GROUP = 192
DEPTH = 4
ALPHA = (2 * DEPTH) ** 0.25
LN_EPS = 1e-5
RMS_EPS = 1e-6
LRU_C = 8.0
ADAM_LR, ADAM_B1, ADAM_B2, ADAM_EPS, ADAM_WD, ADAM_STEP = 0.001, 0.9, 0.999, 1e-08, 0.01, 10

_TS = (256, 128, 256, 256)
TK = 512
SUB = 8
LANE = 128
VMEM_LIMIT = 58 * 1024 * 1024

_OFFS = (
    dict(u=0, v=768, qx=1536, gate=1792, W=2816),
    dict(q=0, f=768, i=1536, qx=2304, gate=2560, W=3584),
    dict(p=0, qx=768, gate=1024, W=2048),
    dict(xb=0, qx=768, gate=1024, W=2048),
)
_PRM = (
    ("wtri", "wtriT", "bcolb"),
    ("lb", "ng"),
    ("wbd", "wbdT", "scale"),
    ("cw", "cb", "wgx", "wgxT", "bgx", "wga", "wgaT", "bga", "ap"),
)
MESH = pl.DeviceIdType.MESH


def _mm(a, b):
    return jnp.dot(a.astype(MM), b.astype(MM), preferred_element_type=f32)


def _mm_nt(a, b):
    return lax.dot_general(a.astype(MM), b.astype(MM), (((1,), (1,)), ((), ())), preferred_element_type=f32)


def _mm_tn(a, b):
    return lax.dot_general(a.astype(MM), b.astype(MM), (((0,), (0,)), ((), ())), preferred_element_type=f32)


def _mm_sel(sel, b):
    s = sel.astype(bf16)
    hi = b.astype(bf16)
    lo = (b - hi.astype(f32)).astype(bf16)
    return jnp.dot(s, hi, preferred_element_type=f32) + jnp.dot(s, lo, preferred_element_type=f32)


def _sig(x):
    return jax.nn.sigmoid(x)


_GC = math.sqrt(2.0 / math.pi)


def _gelu(x):
    t = jnp.tanh(_GC * (x + 0.044715 * x * x * x))
    return 0.5 * x * (1.0 + t), t


def _gelu_grad(x, t):
    return 0.5 * (1.0 + t) + 0.5 * x * (1.0 - t * t) * _GC * (1.0 + 3.0 * 0.044715 * x * x)


def _rowsum(x):
    return jnp.sum(x, axis=0, keepdims=True)


def _lmean(x):
    return jnp.mean(x, axis=-1, keepdims=True)


def _ln(z):
    mu = _lmean(z)
    zc = z - mu
    rstd = lax.rsqrt(_lmean(zc * zc) + LN_EPS)
    return zc * rstd, rstd


def _ln_bwd(dxh, xhat, rstd):
    return rstd * (dxh - _lmean(dxh) - xhat * _lmean(dxh * xhat))


def _hs(h):
    return slice(h * HD, (h + 1) * HD)


def _expm1(x):
    small = x * (1.0 + x * 0.5 * (1.0 + x * (1.0 / 3.0) * (1.0 + x * 0.25 * (1.0 + x * 0.2 * (1.0 + x * (1.0 / 6.0))))))
    return jnp.where(jnp.abs(x) < 0.25, small, jnp.exp(x) - 1.0)


def _softplus(x):
    e = jnp.exp(-jnp.abs(x))
    l1p = jnp.where(e < 1e-4, e - 0.5 * e * e, jnp.log(1.0 + e))
    return jnp.maximum(x, 0.0) + l1p


def _scan_fwd(a, b):
    n = a.shape[0]
    row = lax.broadcasted_iota(jnp.int32, a.shape, 0)
    d = 1
    while d < n:
        m = row >= d
        b = jnp.where(m, a * pltpu.roll(b, d, 0) + b, b)
        a = jnp.where(m, a * pltpu.roll(a, d, 0), a)
        d *= 2
    return a, b


def _scan_bwd(a, b):
    n = a.shape[0]
    row = lax.broadcasted_iota(jnp.int32, a.shape, 0)
    d = 1
    while d < n:
        m = row < n - d
        b = jnp.where(m, a * pltpu.roll(b, n - d, 0) + b, b)
        a = jnp.where(m, a * pltpu.roll(a, n - d, 0), a)
        d *= 2
    return a, b


def _chunk_mats(n):
    r = lax.broadcasted_iota(jnp.int32, (n, n), 0)
    c = lax.broadcasted_iota(jnp.int32, (n, n), 1)
    same = (r // CHUNK) == (c // CHUNK)
    return same, jnp.logical_and(same, c <= r)


def _pool_w(shape):
    lane = lax.broadcasted_iota(jnp.int32, shape, 1)
    return jnp.where(lane < POOL_GROUP, 2, jnp.where(lane < 2 * POOL_GROUP, 4, jnp.where(lane < 3 * POOL_GROUP, 8, 16)))


def _pool_pick(r1, r2, r3, r4):
    lane = lax.broadcasted_iota(jnp.int32, r1.shape, 1)
    return jnp.where(lane < POOL_GROUP, r1, jnp.where(lane < 2 * POOL_GROUP, r2, jnp.where(lane < 3 * POOL_GROUP, r3, r4)))


def _const_spec(a):
    nd = a.ndim
    return pl.BlockSpec(a.shape, lambda i, _nd=nd: (0,) * _nd, pipeline_mode=pl.Buffered(1))


def _acc_spec(shape):
    nd = len(shape)
    return pl.BlockSpec(shape, lambda i, _nd=nd: (0,) * _nd)


def _params(sem="arbitrary"):
    return pltpu.CompilerParams(dimension_semantics=(sem,), vmem_limit_bytes=VMEM_LIMIT)


def _xattn_fwd(qx, khT_ref, vh_ref):
    xo = jnp.zeros((qx.shape[0], XW), f32)
    ps = []
    for h in range(XHEADS):
        s = _mm(qx, khT_ref[h]) * (XDIM ** -0.5)
        e = jnp.exp(s - jnp.max(s, axis=-1, keepdims=True))
        p = e / jnp.sum(e, axis=-1, keepdims=True)
        xo = xo + _mm(p, vh_ref[h])
        ps.append(p)
    return xo, ps


def _hgrn_parallel(q_raw, fl, lb):
    n = q_raw.shape[0]
    same, tri = _chunk_mats(n)
    sq = _sig(q_raw)
    qf = q_raw * sq
    sgm = _sig(fl)
    f = lb + (1.0 - lb) * sgm
    logf = jnp.log(f)
    k = 1.0 - f
    g = _mm_sel(tri, logf)
    gl = _mm_sel(same, logf)
    eg = jnp.exp(g)
    eng = jnp.exp(-g)
    ee = jnp.exp(gl - g)
    return dict(sq=sq, qf=qf, sgm=sgm, f=f, k=k, eg=eg, eng=eng, ee=ee, q_dec=qf * eg, k_inv=k * eng, k_end=k * ee,
                a=jnp.exp(gl))


def _hgrn_intra(q_dec, k_inv, v):
    n = q_dec.shape[0]
    _, tri = _chunk_mats(HD)
    outs = []
    for h in range(NH):
        blks = []
        for b in range(n // HD):
            rs = slice(b * HD, (b + 1) * HD)
            sc = jnp.where(tri, _mm_nt(q_dec[rs, _hs(h)], k_inv[rs, _hs(h)]), 0.0)
            blks.append(_mm(sc, v[rs, _hs(h)]))
        outs.append(jnp.concatenate(blks, axis=0))
    return jnp.concatenate(outs, axis=-1)


def _cs(c):
    return slice(c * CHUNK, (c + 1) * CHUNK)


def _hgrn_inter_fwd(qdec_s, kend_s, v_s, a_s, oint_s, st_ref, states_s, u_s):
    n = qdec_s.shape[0] // CHUNK
    for c in range(n):
        for h in range(NH):
            u_s[c, h] = _mm_tn(v_s[_cs(c), _hs(h)], kend_s[_cs(c), _hs(h)])
    for h in range(NH):
        st = st_ref[h]
        for c in range(n):
            states_s[c, h] = st
            st = st * a_s[c * CHUNK:c * CHUNK + 1, _hs(h)] + u_s[c, h]
        st_ref[h] = st
    for c in range(n):
        for h in range(NH):
            oint_s[_cs(c), _hs(h)] = _mm_nt(qdec_s[_cs(c), _hs(h)], states_s[c, h])


def _rms(o):
    outs, rs = [], []
    for h in range(NH):
        oh = o[:, _hs(h)]
        r = lax.rsqrt(_lmean(oh * oh) + RMS_EPS)
        outs.append(oh * r)
        rs.append(r)
    return jnp.concatenate(outs, axis=-1), rs


def _gmlp_core(u_raw, v_raw, wtri_ref, bcolb_ref):
    gu, tu = _gelu(u_raw)
    gv, tv = _gelu(v_raw)
    vns, rstds, mixeds = [], [], []
    for h in range(NH):
        vn, rstd = _ln(gv[:, _hs(h)])
        blks = []
        for n in range(u_raw.shape[0] // HD):
            blks.append(_mm(wtri_ref[h], vn[n * HD:(n + 1) * HD]) + bcolb_ref[h])
        vns.append(vn)
        rstds.append(rstd)
        mixeds.append(jnp.concatenate(blks, axis=0))
    mixed = jnp.concatenate(mixeds, axis=-1)
    return gu, tu, tv, vns, rstds, mixed


def _pool_core(p, carry, row0, wbd_ref):
    ext = jnp.concatenate([carry, p], axis=0)
    r1 = ext + pltpu.roll(ext, 1, 0)
    r2 = r1 + pltpu.roll(r1, 2, 0)
    r3 = r2 + pltpu.roll(r2, 4, 0)
    r4 = r3 + pltpu.roll(r3, 8, 0)
    sel = _pool_pick(r1, r2, r3, r4)[2 * SUB:]
    grow = row0 + lax.broadcasted_iota(jnp.int32, p.shape, 0)
    inv_cnt = 1.0 / jnp.minimum(grow + 1, _pool_w(p.shape)).astype(f32)
    diff = sel * inv_cnt - p
    return diff, inv_cnt, _mm(diff, wbd_ref[...])


def _lru_core(xb, ccar, row0, p):
    ext = jnp.concatenate([ccar, xb], axis=0)
    cw = p["cw"]
    x1, x2, x3 = pltpu.roll(ext, 1, 0)[SUB:], pltpu.roll(ext, 2, 0)[SUB:], pltpu.roll(ext, 3, 0)[SUB:]
    xc = cw[3:4, :] * xb + cw[2:3, :] * x1 + cw[1:2, :] * x2 + cw[0:1, :] * x3 + p["cb"][...]
    gxs, gas = [], []
    for h in range(NH):
        gxs.append(_mm(xc[:, _hs(h)], p["wgx"][h]))
        gas.append(_mm(xc[:, _hs(h)], p["wga"][h]))
    gx = _sig(jnp.concatenate(gxs, axis=-1) + p["bgx"][...])
    ga = _sig(jnp.concatenate(gas, axis=-1) + p["bga"][...])
    sp = _softplus(-p["ap"][...])
    la = -LRU_C * ga * sp
    a = jnp.exp(la)
    grow = row0 + lax.broadcasted_iota(jnp.int32, xb.shape, 0)
    first = grow == 0
    mult = jnp.where(first, 1.0, jnp.sqrt(-_expm1(2.0 * la)))
    bt = mult * gx * xc
    return dict(x1=x1, x2=x2, x3=x3, xc=xc, gx=gx, ga=ga, sp=sp, a=a, mult=mult, bt=bt, first=first)


def _fwd_layer(kind, xin, w_in, w_out, lng, lnb, khT, vh, prm, tgt):
    S = xin.shape[0]
    TS = _TS[kind]
    nt = S // TS
    off = _OFFS[kind]
    W = off["W"]
    last = tgt is not None
    pnames = _PRM[kind]
    pvals = [prm[n] for n in pnames]

    def body(*refs):
        it = iter(refs)
        xin_ref, win_ref, wout_ref, lng_ref, lnb_ref, khT_ref, vh_ref = (next(it) for _ in range(7))
        p = {n: next(it) for n in pnames}
        tgt_ref = next(it) if last else None
        xout_ref, proj_ref, z_ref = next(it), next(it), next(it)
        loss_ref = next(it) if last else None
        rest = list(it)
        i = pl.program_id(0)
        x = xin_ref[...]
        proj_ref[...] = _mm(x, win_ref[...])

        if kind == 0:
            gu, _, _, _, _, mixed = _gmlp_core(proj_ref[:, 0:TOK], proj_ref[:, TOK:2 * TOK], p["wtri"], p["bcolb"])
            tok = gu * mixed
        elif kind == 1:
            st_save, st_ref, states_s, u_s, qdec_s, kend_s, v_s, a_s, oint_s = rest

            @pl.when(i == 0)
            def _():
                st_ref[...] = jnp.zeros_like(st_ref)

            st_save[0] = st_ref[...]
            v = proj_ref[:, 2 * TOK:3 * TOK]
            hp = _hgrn_parallel(proj_ref[:, 0:TOK], proj_ref[:, TOK:2 * TOK], p["lb"][...])
            qdec_s[...] = hp["q_dec"]
            kend_s[...] = hp["k_end"]
            v_s[...] = v
            a_s[...] = hp["a"]
            o_intra = _hgrn_intra(hp["q_dec"], hp["k_inv"], v)
            _hgrn_inter_fwd(qdec_s, kend_s, v_s, a_s, oint_s, st_ref, states_s, u_s)
            on, _ = _rms(o_intra + oint_s[...])
            tok = on * p["ng"][...]
        elif kind == 2:
            pc_save, pcar = rest

            @pl.when(i == 0)
            def _():
                pcar[...] = jnp.zeros_like(pcar)

            pc_save[0] = pcar[...]
            pp = proj_ref[:, 0:TOK]
            _, _, y = _pool_core(pp, pcar[...], i * TS, p["wbd"])
            pcar[...] = pp[TS - 2 * SUB:, :]
            tok = y * p["scale"][...]
        else:
            cc_save, hc_save, ccar, hcar = rest

            @pl.when(i == 0)
            def _():
                ccar[...] = jnp.zeros_like(ccar)
                hcar[...] = jnp.zeros_like(hcar)

            cc_save[0] = ccar[...]
            hc_save[0] = hcar[...]
            xb = proj_ref[:, 0:TOK]
            lc = _lru_core(xb, ccar[...], i * TS, p)
            P, B = _scan_fwd(lc["a"], lc["bt"])
            tok = P * hcar[SUB - 1:SUB, :] + B
            ccar[...] = xb[TS - SUB:, :]
            hcar[...] = tok[TS - SUB:, :]

        xo, _ = _xattn_fwd(proj_ref[:, off["qx"]:off["qx"] + XW], khT_ref, vh_ref)
        gate = proj_ref[:, off["gate"]:off["gate"] + D_MODEL]
        mixed = jnp.concatenate([tok, xo], axis=-1) * (gate * _sig(gate))
        z = ALPHA * x + _mm(mixed, wout_ref[...])
        z_ref[...] = z
        xhat, _ = _ln(z)
        xout = xhat * lng_ref[...] + lnb_ref[...]
        if last:
            e = xout - tgt_ref[...]
            xout_ref[...] = e * (1.0 / D_MODEL)
            es = _rowsum(e * e)
            tot = es[:, 0:LANE]
            for j in range(1, D_MODEL // LANE):
                tot = tot + es[:, j * LANE:(j + 1) * LANE]

            @pl.when(i == 0)
            def _():
                loss_ref[...] = jnp.zeros_like(loss_ref)

            loss_ref[0:1, :] += tot
        else:
            xout_ref[...] = xout

    tile = lambda w: pl.BlockSpec((TS, w), lambda i: (i, 0))
    in_arrays = [xin, w_in, w_out, lng, lnb, khT, vh] + pvals + ([tgt] if last else [])
    in_specs = [tile(D_MODEL)] + [_const_spec(a) for a in in_arrays[1:7 + len(pvals)]] + ([tile(D_MODEL)] if last else [])
    out_shape = [jax.ShapeDtypeStruct((S, D_MODEL), f32), jax.ShapeDtypeStruct((S, W), f32), jax.ShapeDtypeStruct((S, D_MODEL), f32)]
    out_specs = [tile(D_MODEL), tile(W), tile(D_MODEL)]
    if last:
        out_shape.append(jax.ShapeDtypeStruct((SUB, LANE), f32))
        out_specs.append(_acc_spec((SUB, LANE)))
    scratch = []
    save = lambda *s: (jax.ShapeDtypeStruct((nt,) + s, f32), pl.BlockSpec((1,) + s, lambda i, _n=len(s): (i,) + (0,) * _n))
    if kind == 1:
        sh, sp = save(NH, HD, HD)
        out_shape.append(sh)
        out_specs.append(sp)
        scratch = ([pltpu.VMEM((NH, HD, HD), f32)] + [pltpu.VMEM((TS // CHUNK, NH, HD, HD), f32)] * 2
                   + [pltpu.VMEM((TS, TOK), f32)] * 5)
    elif kind == 2:
        sh, sp = save(2 * SUB, TOK)
        out_shape.append(sh)
        out_specs.append(sp)
        scratch = [pltpu.VMEM((2 * SUB, TOK), f32)]
    elif kind == 3:
        for _ in range(2):
            sh, sp = save(SUB, TOK)
            out_shape.append(sh)
            out_specs.append(sp)
        scratch = [pltpu.VMEM((SUB, TOK), f32)] * 2
    return pl.pallas_call(body, name=f"fwd_layer{kind}", grid=(nt,), in_specs=in_specs, out_specs=out_specs,
                          out_shape=out_shape, scratch_shapes=scratch, compiler_params=_params())(*in_arrays)


def _small_grad_shapes(kind):
    if kind == 0:
        return dict(dwtri=(NH, HD, HD), dbacc=(NH, HD, HD))
    if kind == 1:
        return dict(dlb=(SUB, TOK), dng=(SUB, TOK))
    if kind == 2:
        return dict(dwbd=(TOK, TOK), dscale=(SUB, TOK))
    return dict(dcw=(SUB, TOK), dvec=(SUB, TOK), dwgx=(NH, HD, HD), dwga=(NH, HD, HD))


def _bwd_layer(kind, dxout, z, proj, w_inT, w_outT, lng, kh, khT, vh, vhT, prm, saves):
    S = dxout.shape[0]
    TS = _TS[kind]
    nt = S // TS
    off = _OFFS[kind]
    W = off["W"]
    pnames = _PRM[kind]
    pvals = [prm[n] for n in pnames]
    sg_shapes = _small_grad_shapes(kind)
    sg_names = list(sg_shapes)
    n_saves = len(saves)

    def body(*refs):
        it = iter(refs)
        dxo_ref, z_ref, proj_ref, winT_ref, woutT_ref, lng_ref, kh_ref, khT_ref, vh_ref, vhT_ref = (next(it) for _ in range(10))
        p = {n: next(it) for n in pnames}
        sv = [next(it) for _ in range(n_saves)]
        dxin_ref, dproj_ref, mixed_ref, dy_ref, dln_ref, dk_ref, dv_ref = (next(it) for _ in range(7))
        sg = {n: next(it) for n in sg_names}
        rest = list(it)
        step = pl.program_id(0)
        i = nt - 1 - step

        @pl.when(step == 0)
        def _():
            dln_ref[...] = jnp.zeros_like(dln_ref)
            dk_ref[...] = jnp.zeros_like(dk_ref)
            dv_ref[...] = jnp.zeros_like(dv_ref)
            for n in sg_names:
                sg[n][...] = jnp.zeros_like(sg[n])

        dxo = dxo_ref[...]
        xhat, rstd = _ln(z_ref[...])
        dln_ref[0:1, :] += _rowsum(dxo * xhat)
        dln_ref[1:2, :] += _rowsum(dxo)
        dz = _ln_bwd(dxo * lng_ref[...], xhat, rstd)
        dyb = dz.astype(bf16)
        dy_ref[...] = dyb
        dmixed = _mm(dyb, woutT_ref[...])

        aux = {}
        if kind == 0:
            u_raw, v_raw = proj_ref[:, 0:TOK], proj_ref[:, TOK:2 * TOK]
            gu, tu, tv, vns, rstds, mx = _gmlp_core(u_raw, v_raw, p["wtri"], p["bcolb"])
            tok = gu * mx
        elif kind == 1:
            st_save, = sv
            (dst_ref, fst_ref, states_s, dsts_s, u_s, qdec_s, kend_s, v_s, a_s, oint_s, do_s, dqdec_s, dkend_s, dv_s,
             dgl_s) = rest

            @pl.when(step == 0)
            def _():
                dst_ref[...] = jnp.zeros_like(dst_ref)

            fst_ref[...] = st_save[0]
            v = proj_ref[:, 2 * TOK:3 * TOK]
            hp = _hgrn_parallel(proj_ref[:, 0:TOK], proj_ref[:, TOK:2 * TOK], p["lb"][...])
            qdec_s[...] = hp["q_dec"]
            kend_s[...] = hp["k_end"]
            v_s[...] = v
            a_s[...] = hp["a"]
            o_intra = _hgrn_intra(hp["q_dec"], hp["k_inv"], v)
            _hgrn_inter_fwd(qdec_s, kend_s, v_s, a_s, oint_s, fst_ref, states_s, u_s)
            o = o_intra + oint_s[...]
            on, rs = _rms(o)
            tok = on * p["ng"][...]
            aux = dict(hp=hp, v=v, o=o, on=on, rs=rs)
        elif kind == 2:
            pc_save, = sv
            dpcar, = rest
            pp = proj_ref[:, 0:TOK]
            diff, inv_cnt, y = _pool_core(pp, pc_save[0], i * TS, p["wbd"])
            tok = y * p["scale"][...]
        else:
            cc_save, hc_save = sv
            dccar, gcar = rest
            xb = proj_ref[:, 0:TOK]
            lc = _lru_core(xb, cc_save[0], i * TS, p)
            P, B = _scan_fwd(lc["a"], lc["bt"])
            hin = hc_save[0, SUB - 1:SUB, :]
            tok = P * hin + B

        xo, ps = _xattn_fwd(proj_ref[:, off["qx"]:off["qx"] + XW], khT_ref, vh_ref)
        gate = proj_ref[:, off["gate"]:off["gate"] + D_MODEL]
        sgm = _sig(gate)
        sgate = gate * sgm
        cat = jnp.concatenate([tok, xo], axis=-1)
        mixed_ref[...] = (cat * sgate).astype(bf16)
        dcat = dmixed * sgate
        dproj_ref[:, off["gate"]:off["gate"] + D_MODEL] = (dmixed * cat * (sgm * (1.0 + gate * (1.0 - sgm)))).astype(bf16)
        dtok = dcat[:, 0:TOK]
        dxo_att = dcat[:, TOK:]

        qx = proj_ref[:, off["qx"]:off["qx"] + XW]
        dqx = jnp.zeros((TS, XW), f32)
        for h in range(XHEADS):
            dp = _mm(dxo_att, vhT_ref[h])
            ds = ps[h] * (dp - jnp.sum(dp * ps[h], axis=-1, keepdims=True)) * (XDIM ** -0.5)
            dqx = dqx + _mm(ds, kh_ref[h])
            dk_ref[h] += _mm_tn(ds, qx)
            dv_ref[h] += _mm_tn(ps[h], dxo_att)
        dproj_ref[:, off["qx"]:off["qx"] + XW] = dqx.astype(bf16)

        if kind == 0:
            tril = lax.broadcasted_iota(jnp.int32, (HD, HD), 1) <= lax.broadcasted_iota(jnp.int32, (HD, HD), 0)
            dgu = dtok * mx
            dmx = dtok * gu
            dgvs = []
            for h in range(NH):
                dmh = dmx[:, _hs(h)]
                blks = []
                for n in range(TS // HD):
                    rs_ = slice(n * HD, (n + 1) * HD)
                    blks.append(_mm(p["wtriT"][h], dmh[rs_]))
                    sg["dwtri"][h] += jnp.where(tril, _mm_nt(dmh[rs_], vns[h][rs_]), 0.0)
                    sg["dbacc"][h] += dmh[rs_]
                dgvs.append(_ln_bwd(jnp.concatenate(blks, axis=0), vns[h], rstds[h]))
            dgv = jnp.concatenate(dgvs, axis=-1)
            dproj_ref[:, 0:TOK] = (dgu * _gelu_grad(u_raw, tu)).astype(bf16)
            dproj_ref[:, TOK:2 * TOK] = (dgv * _gelu_grad(v_raw, tv)).astype(bf16)
        elif kind == 1:
            hp, v, o, on, rs = aux["hp"], aux["v"], aux["o"], aux["on"], aux["rs"]
            ng = p["ng"][...]
            sg["dng"][0:1, :] += _rowsum(dtok * on)
            dn = dtok * ng
            dos = []
            for h in range(NH):
                oh, r = o[:, _hs(h)], rs[h]
                dos.append(r * (dn[:, _hs(h)] - oh * (r * r) * _lmean(dn[:, _hs(h)] * oh)))
            do = jnp.concatenate(dos, axis=-1)
            do_s[...] = do
            _, tri = _chunk_mats(HD)
            dqd, dki, dvi = [], [], []
            for h in range(NH):
                bq, bk, bv = [], [], []
                for b in range(TS // HD):
                    rs_ = slice(b * HD, (b + 1) * HD)
                    qd, ki = hp["q_dec"][rs_, _hs(h)], hp["k_inv"][rs_, _hs(h)]
                    sc = jnp.where(tri, _mm_nt(qd, ki), 0.0)
                    dsc = jnp.where(tri, _mm_nt(do[rs_, _hs(h)], v[rs_, _hs(h)]), 0.0)
                    bv.append(_mm_tn(sc, do[rs_, _hs(h)]))
                    bq.append(_mm(dsc, ki))
                    bk.append(_mm_tn(dsc, qd))
                dqd.append(jnp.concatenate(bq, axis=0))
                dki.append(jnp.concatenate(bk, axis=0))
                dvi.append(jnp.concatenate(bv, axis=0))
            dqdec_s[...] = jnp.concatenate(dqd, axis=-1)
            dk_inv = jnp.concatenate(dki, axis=-1)
            dv_s[...] = jnp.concatenate(dvi, axis=-1)
            row16 = lax.broadcasted_iota(jnp.int32, (CHUNK, HD), 0)

            nch = TS // CHUNK
            for c in range(nch):
                for h in range(NH):
                    u_s[c, h] = _mm_tn(do_s[_cs(c), _hs(h)], qdec_s[_cs(c), _hs(h)])
            for h in range(NH):
                dst = dst_ref[h]
                for c in reversed(range(nch)):
                    dsts_s[c, h] = dst
                    dst = dst * a_s[c * CHUNK:c * CHUNK + 1, _hs(h)] + u_s[c, h]
                dst_ref[h] = dst
            for c in range(nch):
                for h in range(NH):
                    stp = states_s[c, h]
                    dst = dsts_s[c, h]
                    dqdec_s[_cs(c), _hs(h)] += _mm(do_s[_cs(c), _hs(h)], stp)
                    dkend_s[_cs(c), _hs(h)] = _mm(v_s[_cs(c), _hs(h)], dst)
                    dv_s[_cs(c), _hs(h)] += _mm_nt(kend_s[_cs(c), _hs(h)], dst)
                    da = jnp.sum(dst * stp, axis=0, keepdims=True) * a_s[c * CHUNK:c * CHUNK + 1, _hs(h)]
                    dgl_s[_cs(c), _hs(h)] = jnp.where(row16 == 0, jnp.broadcast_to(da, (CHUNK, HD)), 0.0)
            dq_dec = dqdec_s[...]
            dk_end = dkend_s[...]
            same, _ = _chunk_mats(TS)
            triT = jnp.logical_and(same, lax.broadcasted_iota(jnp.int32, (TS, TS), 1) >= lax.broadcasted_iota(jnp.int32, (TS, TS), 0))
            dg = dq_dec * hp["q_dec"] - dk_inv * hp["k_inv"] - dk_end * hp["k_end"]
            dk = dk_inv * hp["eng"] + dk_end * hp["ee"]
            dglr = dk_end * hp["k_end"] + dgl_s[...]
            dlogf = _mm_sel(triT, dg) + _mm_sel(same, dglr)
            df = dlogf / hp["f"] - dk
            lb = p["lb"][...]
            sg["dlb"][0:1, :] += _rowsum(df * (1.0 - hp["sgm"]))
            q_raw = proj_ref[:, 0:TOK]
            dproj_ref[:, 0:TOK] = (dq_dec * hp["eg"] * (hp["sq"] * (1.0 + q_raw * (1.0 - hp["sq"])))).astype(bf16)
            dproj_ref[:, TOK:2 * TOK] = (df * (1.0 - lb) * hp["sgm"] * (1.0 - hp["sgm"])).astype(bf16)
            dproj_ref[:, 2 * TOK:3 * TOK] = dv_s[...].astype(bf16)
        elif kind == 2:
            @pl.when(step == 0)
            def _():
                dpcar[...] = jnp.zeros_like(dpcar)

            sg["dscale"][0:1, :] += _rowsum(dtok * y)
            dyp = dtok * p["scale"][...]
            sg["dwbd"][...] += _mm_tn(diff, dyp)
            ddiff = _mm(dyp, p["wbdT"][...])
            q = ddiff * inv_cnt
            ext = jnp.concatenate([q, dpcar[...]], axis=0)
            n = TS + 2 * SUB
            r1 = ext + pltpu.roll(ext, n - 1, 0)
            r2 = r1 + pltpu.roll(r1, n - 2, 0)
            r3 = r2 + pltpu.roll(r2, n - 4, 0)
            r4 = r3 + pltpu.roll(r3, n - 8, 0)
            dproj_ref[:, 0:TOK] = (_pool_pick(r1, r2, r3, r4)[:TS] - ddiff).astype(bf16)
            dpcar[...] = q[0:2 * SUB, :]
        else:
            @pl.when(step == 0)
            def _():
                dccar[...] = jnp.zeros_like(dccar)
                gcar[...] = jnp.zeros_like(gcar)

            a, mult, gx, ga, xc = lc["a"], lc["mult"], lc["gx"], lc["ga"], lc["xc"]
            row = lax.broadcasted_iota(jnp.int32, (TS, TOK), 0)
            an = jnp.where(row == TS - 1, 1.0, pltpu.roll(a, TS - 1, 0))
            Pb, Bb = _scan_bwd(an, dtok)
            lam = Pb * gcar[0:1, :] + Bb
            gcar[...] = (a * lam)[0:SUB, :]
            hprev = jnp.where(row == 0, jnp.broadcast_to(hin, (TS, TOK)), pltpu.roll(tok, 1, 0))
            dmult = lam * gx * xc
            dgx = lam * mult * xc
            dxc = lam * mult * gx
            dla = lam * hprev * a - jnp.where(lc["first"], 0.0, dmult * a * a / mult)
            sp = lc["sp"]
            dga = -LRU_C * sp * dla
            dsp = _rowsum(-LRU_C * ga * dla)
            sg["dvec"][0:1, :] += dsp * (-_sig(-p["ap"][...]))
            dpx = dgx * gx * (1.0 - gx)
            dpa = dga * ga * (1.0 - ga)
            sg["dvec"][1:2, :] += _rowsum(dpx)
            sg["dvec"][2:3, :] += _rowsum(dpa)
            dxcs = []
            for h in range(NH):
                dxcs.append(_mm(dpx[:, _hs(h)], p["wgxT"][h]) + _mm(dpa[:, _hs(h)], p["wgaT"][h]))
                sg["dwgx"][h] += _mm_tn(xc[:, _hs(h)], dpx[:, _hs(h)])
                sg["dwga"][h] += _mm_tn(xc[:, _hs(h)], dpa[:, _hs(h)])
            dxc = dxc + jnp.concatenate(dxcs, axis=-1)
            sg["dvec"][3:4, :] += _rowsum(dxc)
            sg["dcw"][3:4, :] += _rowsum(dxc * xb)
            sg["dcw"][2:3, :] += _rowsum(dxc * lc["x1"])
            sg["dcw"][1:2, :] += _rowsum(dxc * lc["x2"])
            sg["dcw"][0:1, :] += _rowsum(dxc * lc["x3"])
            ext = jnp.concatenate([dxc, dccar[...]], axis=0)
            n = TS + SUB
            cw = p["cw"]
            dproj_ref[:, 0:TOK] = (cw[3:4, :] * dxc + cw[2:3, :] * pltpu.roll(ext, n - 1, 0)[:TS]
                                   + cw[1:2, :] * pltpu.roll(ext, n - 2, 0)[:TS]
                                   + cw[0:1, :] * pltpu.roll(ext, n - 3, 0)[:TS]).astype(bf16)
            dccar[...] = dxc[0:SUB, :]

        dxin_ref[...] = ALPHA * dz + _mm(dproj_ref[...], winT_ref[...])

    rtile = lambda w: pl.BlockSpec((TS, w), lambda s: (nt - 1 - s, 0))
    consts = [w_inT, w_outT, lng, kh, khT, vh, vhT] + pvals
    in_arrays = [dxout, z, proj] + consts + list(saves)
    in_specs = [rtile(D_MODEL), rtile(D_MODEL), rtile(W)] + [_const_spec(a) for a in consts]
    for a in saves:
        in_specs.append(pl.BlockSpec((1,) + a.shape[1:], lambda s, _n=a.ndim - 1: (nt - 1 - s,) + (0,) * _n))
    out_shape = [jax.ShapeDtypeStruct((S, D_MODEL), f32), jax.ShapeDtypeStruct((S, W), bf16),
                 jax.ShapeDtypeStruct((S, D_MODEL), bf16), jax.ShapeDtypeStruct((S, D_MODEL), bf16),
                 jax.ShapeDtypeStruct((SUB, D_MODEL), f32), jax.ShapeDtypeStruct((XHEADS, XW, XW), f32),
                 jax.ShapeDtypeStruct((XHEADS, XW, XW), f32)]
    out_specs = [rtile(D_MODEL), rtile(W), rtile(D_MODEL), rtile(D_MODEL), _acc_spec((SUB, D_MODEL)),
                 _acc_spec((XHEADS, XW, XW)), _acc_spec((XHEADS, XW, XW))]
    for n in sg_names:
        out_shape.append(jax.ShapeDtypeStruct(sg_shapes[n], f32))
        out_specs.append(_acc_spec(sg_shapes[n]))
    if kind == 1:
        scratch = ([pltpu.VMEM((NH, HD, HD), f32)] * 2 + [pltpu.VMEM((TS // CHUNK, NH, HD, HD), f32)] * 3
                   + [pltpu.VMEM((TS, TOK), f32)] * 10)
    elif kind == 2:
        scratch = [pltpu.VMEM((2 * SUB, TOK), f32)]
    elif kind == 3:
        scratch = [pltpu.VMEM((SUB, TOK), f32)] * 2
    else:
        scratch = []
    outs = pl.pallas_call(body, name=f"bwd_layer{kind}", grid=(nt,), in_specs=in_specs, out_specs=out_specs,
                          out_shape=out_shape, scratch_shapes=scratch, compiler_params=_params())(*in_arrays)
    return outs[:7], dict(zip(sg_names, outs[7:]))


def _prep(mem, w_kv, logits):
    def body(mem_ref, w_ref, lg_ref, kh_ref, khT_ref, vh_ref, vhT_ref, p_ref):
        kv = _mm(mem_ref[...], w_ref[...])
        k, v = kv[:, 0:XW], kv[:, XW:]
        kT, vT = k.T, v.T
        col = lax.broadcasted_iota(jnp.int32, (XW, XW), 1) // XDIM
        row = lax.broadcasted_iota(jnp.int32, (XW, XW), 0) // XDIM
        for h in range(XHEADS):
            kh_ref[h] = jnp.where(col == h, k, 0.0).astype(bf16)
            vh_ref[h] = jnp.where(col == h, v, 0.0).astype(bf16)
            khT_ref[h] = jnp.where(row == h, kT, 0.0).astype(bf16)
            vhT_ref[h] = jnp.where(row == h, vT, 0.0).astype(bf16)
        lg = lg_ref[...]
        e = jnp.exp(lg - jnp.max(lg, axis=0, keepdims=True))
        p_ref[...] = e / jnp.sum(e, axis=0, keepdims=True)

    vm = pl.BlockSpec(memory_space=pltpu.VMEM)
    hs = jax.ShapeDtypeStruct((XHEADS, XW, XW), bf16)
    return pl.pallas_call(body, name="prep_memory", in_specs=[vm] * 3, out_specs=[vm] * 5,
                          out_shape=[hs, hs, hs, hs, jax.ShapeDtypeStruct(logits.shape, f32)])(mem, w_kv, logits)


def _kv_bwd(mem, dks, dvs):
    def body(mem_ref, *refs):
        out_ref = refs[-1]
        col = lax.broadcasted_iota(jnp.int32, (XW, XW), 1) // XDIM
        dk = jnp.zeros((XW, XW), f32)
        dv = jnp.zeros((XW, XW), f32)
        for l in range(DEPTH):
            for h in range(XHEADS):
                dk = dk + jnp.where(col == h, refs[l][h], 0.0)
                dv = dv + jnp.where(col == h, refs[DEPTH + l][h], 0.0)
        out_ref[:, 0:XW] = _mm_tn(mem_ref[...], dk)
        out_ref[:, XW:] = _mm_tn(mem_ref[...], dv)

    vm = pl.BlockSpec(memory_space=pltpu.VMEM)
    return pl.pallas_call(body, name="kv_bwd", in_specs=[vm] * (1 + 2 * DEPTH), out_specs=vm,
                          out_shape=jax.ShapeDtypeStruct((D_MODEL, 2 * XW), f32))(mem, *dks, *dvs)


def _tn_gemm(a, b, name, nb):
    S, M = a.shape
    N = b.shape[1]
    NB = N // nb
    nk = S // TK

    def body(a_ref, b_ref, o_ref):
        @pl.when(pl.program_id(1) == 0)
        def _():
            o_ref[...] = jnp.zeros_like(o_ref)

        o_ref[...] += _mm_tn(a_ref[...], b_ref[...])

    return pl.pallas_call(body, name=name, grid=(nb, nk),
                          in_specs=[pl.BlockSpec((TK, M), lambda j, k: (k, 0)), pl.BlockSpec((TK, NB), lambda j, k: (k, j))],
                          out_specs=pl.BlockSpec((M, NB), lambda j, k: (0, j)),
                          out_shape=jax.ShapeDtypeStruct((M, N), f32),
                          compiler_params=pltpu.CompilerParams(dimension_semantics=("parallel", "arbitrary"),
                                                               vmem_limit_bytes=VMEM_LIMIT))(a, b)


def _rows_block(R, mult=16, cap=1024):
    best = R
    for d in range(mult, min(R, cap) + 1, mult):
        if R % d == 0:
            best = d
    return best


def _tn_gemm_sharded(a, b, name):
    S, M = a.shape
    Wq = b.shape[1] // 4
    nk = S // TK

    def body(a_ref, b_ref, o_ref):
        @pl.when(pl.program_id(1) == 0)
        def _():
            o_ref[...] = jnp.zeros_like(o_ref)

        o_ref[...] += _mm_tn(a_ref[...], b_ref[...])

    return pl.pallas_call(body, name=name, grid=(4, nk),
                          in_specs=[pl.BlockSpec((TK, M), lambda j, k: (k, 0)), pl.BlockSpec((TK, Wq), lambda j, k: (k, j))],
                          out_specs=pl.BlockSpec((None, M, Wq), lambda j, k: (j, 0, 0)),
                          out_shape=jax.ShapeDtypeStruct((4, M, Wq), f32),
                          compiler_params=pltpu.CompilerParams(dimension_semantics=("parallel", "arbitrary"),
                                                               vmem_limit_bytes=VMEM_LIMIT))(a, b)


def _tn_gemm_slab(a, b, acc, l, name):
    S, M = a.shape
    N = b.shape[1]
    nk = S // TK

    def body(a_ref, b_ref, *refs):
        o_ref = refs[-1]

        @pl.when(pl.program_id(0) == 0)
        def _():
            o_ref[...] = jnp.zeros_like(o_ref)

        o_ref[...] += _mm_tn(a_ref[...], b_ref[...])

    ins = [a, b] + ([] if acc is None else [acc])
    in_specs = [pl.BlockSpec((TK, M), lambda k: (k, 0)), pl.BlockSpec((TK, N), lambda k: (k, 0))]
    if acc is not None:
        in_specs.append(pl.BlockSpec(memory_space=pl.ANY))
    return pl.pallas_call(body, name=name, grid=(nk,), in_specs=in_specs,
                          out_specs=pl.BlockSpec((None, M, N), lambda k: (l, 0, 0)),
                          out_shape=jax.ShapeDtypeStruct((DEPTH, M, N), f32),
                          input_output_aliases={} if acc is None else {2: 0},
                          compiler_params=_params())(*ins)


HALF_ROWS = D_MODEL // 2
SHARD_ROWS = D_MODEL // 4


def _half_of_full(ref, kind, h):
    if kind == "kv":
        return ref.at[:, pl.ds(h * XW, XW)]
    if kind == "wout":
        return ref.at[pl.ds(2 * h, 2)]
    return ref.at[:, pl.ds(h * HALF_ROWS, HALF_ROWS)]


def _shard_of_half(ref, kind, j):
    if kind == "kv":
        return ref.at[pl.ds(j * SHARD_ROWS, SHARD_ROWS)]
    if kind == "wout":
        return ref.at[:, pl.ds(j * SHARD_ROWS, SHARD_ROWS)]
    return ref.at[j]


def _half_of_shard(ref, kind, h):
    if kind == "kv":
        return ref.at[:, pl.ds(h * XW, XW)]
    if kind == "wout":
        return ref.at[pl.ds(2 * h, 2)]
    return ref.at[pl.ds(h * HALF_ROWS, HALF_ROWS)]


def _half_shape(full_shape, kind):
    if kind == "kv":
        return (full_shape[0], XW)
    if kind == "wout":
        return (2,) + tuple(full_shape[1:])
    return (4, HALF_ROWS, full_shape[2])


def _shard_half_shape(full_shape, kind):
    if kind == "kv":
        return (SHARD_ROWS, XW)
    if kind == "wout":
        return (2, SHARD_ROWS, full_shape[2])
    return (HALF_ROWS, full_shape[2])


def _shard_shape(full_shape, kind):
    if kind == "kv":
        return (SHARD_ROWS, full_shape[1])
    if kind == "wout":
        return (DEPTH, SHARD_ROWS, full_shape[2])
    return (D_MODEL, full_shape[2])


def _ew_call(body, name, grid, jc, ins, in_specs, out_shape, out_specs):
    gs = pltpu.PrefetchScalarGridSpec(num_scalar_prefetch=1, grid=grid, in_specs=in_specs, out_specs=out_specs)
    return pl.pallas_call(body, name=name, grid_spec=gs, out_shape=out_shape,
                          compiler_params=pltpu.CompilerParams(dimension_semantics=("parallel",) * len(grid),
                                                               vmem_limit_bytes=VMEM_LIMIT))(jc, *ins)


def _add_sibling(part, got, kind, jc, name):
    def body(jc_ref, a_ref, b_ref, o_ref, ob_ref):
        s = a_ref[...] + b_ref[...]
        o_ref[...] = s
        ob_ref[...] = s.astype(bf16)

    if kind == "kv":
        R = part.shape[0]
        grid = (2,)
        mine = pl.BlockSpec((R // 2, XW), lambda i, jc_ref: (i, jc_ref[1]))
        spec = pl.BlockSpec((R // 2, XW), lambda i, jc_ref: (i, 0))
    elif kind == "wout":
        _, R, C = part.shape
        grid = (2, 2)
        mine = pl.BlockSpec((None, R // 2, C), lambda s, i, jc_ref: (2 * jc_ref[1] + s, i, 0))
        spec = pl.BlockSpec((None, R // 2, C), lambda s, i, jc_ref: (s, i, 0))
    else:
        C = part.shape[2]
        grid = (4, 2)
        mine = pl.BlockSpec((None, HALF_ROWS // 2, C), lambda s, i, jc_ref: (s, 2 * jc_ref[1] + i, 0))
        spec = pl.BlockSpec((None, HALF_ROWS // 2, C), lambda s, i, jc_ref: (s, i, 0))
    hs = _half_shape(part.shape, kind)
    return _ew_call(body, name, grid, jc, [part, got], [mine, spec],
                    [jax.ShapeDtypeStruct(hs, f32), jax.ShapeDtypeStruct(hs, bf16)], [spec, spec])


def _add_chips(q32, r, kind, jc, name):
    def body(jc_ref, q_ref, r_ref, out_ref):
        out_ref[...] = ((q_ref[...] + r_ref[0].astype(f32)) + r_ref[1].astype(f32)) + r_ref[2].astype(f32)

    if kind == "kv":
        grid = (1,)
        qs = pl.BlockSpec((SHARD_ROWS, XW), lambda i, jc_ref: (jc_ref[0], 0))
        rs = pl.BlockSpec((3, SHARD_ROWS, XW), lambda i, jc_ref: (0, 0, 0))
        os_ = pl.BlockSpec((SHARD_ROWS, XW), lambda i, jc_ref: (0, jc_ref[1]))
    elif kind == "wout":
        C = q32.shape[2]
        grid = (2,)
        qs = pl.BlockSpec((None, SHARD_ROWS, C), lambda s, jc_ref: (s, jc_ref[0], 0))
        rs = pl.BlockSpec((3, None, SHARD_ROWS, C), lambda s, jc_ref: (0, s, 0, 0))
        os_ = pl.BlockSpec((None, SHARD_ROWS, C), lambda s, jc_ref: (2 * jc_ref[1] + s, 0, 0))
    else:
        C = q32.shape[2]
        grid = (2,)
        qs = pl.BlockSpec((None, HALF_ROWS // 2, C), lambda i, jc_ref: (jc_ref[0], i, 0))
        rs = pl.BlockSpec((3, HALF_ROWS // 2, C), lambda i, jc_ref: (0, i, 0))
        os_ = pl.BlockSpec((HALF_ROWS // 2, C), lambda i, jc_ref: (2 * jc_ref[1] + i, 0))
    if kind == "kv":
        full_shape = (D_MODEL, 2 * XW)
    elif kind == "wout":
        full_shape = (DEPTH, D_MODEL, D_MODEL)
    else:
        full_shape = (4, D_MODEL, q32.shape[2])
    return _ew_call(body, name, grid, jc, [q32, r], [qs, rs], jax.ShapeDtypeStruct(_shard_shape(full_shape, kind), f32), os_)


def _adamw(w, g, m, v, name):
    R, C = w.shape
    br = _rows_block(R, mult=SUB, cap=512)
    c1 =1.0 / (1.0 - ADAM_B1 ** ADAM_STEP)
    c2 = 1.0 / (1.0 - ADAM_B2 ** ADAM_STEP)

    def body(w_ref, g_ref, m_ref, v_ref, d_ref, nm_ref, nv_ref):
        g_ = g_ref[...]
        nm = ADAM_B1 * m_ref[...] + (1.0 - ADAM_B1) * g_
        nv = ADAM_B2 * v_ref[...] + (1.0 - ADAM_B2) * (g_ * g_)
        nm_ref[...] = nm
        nv_ref[...] = nv
        d_ref[...] = -ADAM_LR * ((nm * c1) / (jnp.sqrt(nv * c2) + ADAM_EPS) + ADAM_WD * w_ref[...])

    spec = pl.BlockSpec((br, C), lambda i: (i, 0))
    sh = jax.ShapeDtypeStruct((R, C), f32)
    return pl.pallas_call(body, name=name, grid=(R // br,), in_specs=[spec] * 4, out_specs=[spec] * 3,
                          out_shape=[sh, sh, sh], compiler_params=_params("parallel"))(w, g, m, v)


def _small_finish(dbacc, p_soft, dlb):
    def body(db_ref, p_ref, dlb_ref, dbs_ref, dlg_ref):
        lane = lax.broadcasted_iota(jnp.int32, (HD, HD), 1)
        acc = jnp.zeros((HD, HD), f32)
        for h in range(NH):
            acc = acc + jnp.where(lane == h, jnp.sum(db_ref[h], axis=-1, keepdims=True), 0.0)
        dbs_ref[...] = acc
        p = p_ref[...]
        p1 = p[1:2, :]
        rowi = lax.broadcasted_iota(jnp.int32, p.shape, 0)
        dlg_ref[...] = dlb_ref[0:1, :] * p1 * (jnp.where(rowi == 1, 1.0, 0.0) - p)

    vm = pl.BlockSpec(memory_space=pltpu.VMEM)
    return pl.pallas_call(body, name="small_finish", in_specs=[vm] * 3, out_specs=[vm] * 2,
                          out_shape=[jax.ShapeDtypeStruct((HD, HD), f32), jax.ShapeDtypeStruct(p_soft.shape, f32)])(dbacc, p_soft, dlb)


def _where_am_i():
    return lax.axis_index("x"), lax.axis_index("y"), lax.axis_index("c")


MAX_PIECES = 8


def _nchunks(rows, mult):
    for n in range(MAX_PIECES, 0, -1):
        if rows % (n * mult) == 0:
            return n
    return 1


def _leading_pieces(src, dst):
    n = src.shape[0]
    if len(src.shape) >= 3 and n <= MAX_PIECES:
        return [(src.at[s], dst.at[s]) for s in range(n)]
    return [(src, dst)]


def _ag_weights(shards, kinds, jshard):
    n = len(shards)

    def body(*refs):
        sh_refs, out_refs = refs[:n], refs[2 * n:3 * n]
        send_sems, recv_sems = refs[3 * n:]
        x, y, c = _where_am_i()
        j = 2 * x + y
        sib = (x, y, 1 - c)
        chips = [(1 - x, y), (x, 1 - y), (1 - x, 1 - y)]

        def cp(k, src, dst, to):
            return pltpu.make_async_remote_copy(src_ref=src, dst_ref=dst, send_sem=send_sems.at[k], recv_sem=recv_sems.at[k],
                                                device_id=to, device_id_type=MESH)

        started = []
        for a in range(n):
            for k, (cx, cy) in enumerate(chips):
                d = cp(6 * a + k, _half_of_shard(sh_refs[a], kinds[a], c), _half_of_shard(out_refs[a].at[j], kinds[a], c), (cx, cy, c))
                d.start()
                started.append(d)
        for a in range(n):
            for k, (cx, cy) in enumerate(chips):
                blk = _half_of_shard(out_refs[a].at[2 * cx + cy], kinds[a], c)
                cp(6 * a + k, blk, blk, (cx, cy, c)).wait_recv()
                d = cp(6 * a + 3 + k, blk, blk, sib)
                d.start()
                started.append(d)
        for a in range(n):
            for k, (cx, cy) in enumerate(chips):
                blk = _half_of_shard(out_refs[a].at[2 * cx + cy], kinds[a], 1 - c)
                cp(6 * a + 3 + k, blk, blk, sib).wait_recv()
        for d in started:
            d.wait_send()

    placed = [lax.dynamic_update_slice(jnp.zeros((4,) + s.shape, s.dtype), s[None], (jshard,) + (0,) * s.ndim) for s in shards]
    anyspec = pl.BlockSpec(memory_space=pl.ANY)
    return pl.pallas_call(body, name="all_gather_weights", in_specs=[anyspec] * (2 * n), out_specs=[anyspec] * n,
                          out_shape=[jax.ShapeDtypeStruct(p.shape, p.dtype) for p in placed],
                          input_output_aliases={n + a: a for a in range(n)},
                          scratch_shapes=[pltpu.SemaphoreType.DMA((6 * n,)), pltpu.SemaphoreType.DMA((6 * n,))],
                          compiler_params=pltpu.CompilerParams(has_side_effects=True))(*shards, *placed)


def _rs_swap(parts, kinds):
    n = len(parts)

    def body(*refs):
        p_refs, got_refs = refs[:n], refs[n:2 * n]
        send_sems, recv_sems = refs[2 * n:]
        x, y, c = _where_am_i()

        def cp(a, src, dst):
            return pltpu.make_async_remote_copy(src_ref=src, dst_ref=dst, send_sem=send_sems.at[a], recv_sem=recv_sems.at[a],
                                                device_id=(x, y, 1 - c), device_id_type=MESH)

        for a in range(n):
            for src, dst in _leading_pieces(_half_of_full(p_refs[a], kinds[a], 1 - c), got_refs[a]):
                cp(a, src, dst).start()
        for a in range(n):
            cp(a, got_refs[a], got_refs[a]).wait()

    anyspec = pl.BlockSpec(memory_space=pl.ANY)
    return pl.pallas_call(body, name="rs_swap_halves", in_specs=[anyspec] * n, out_specs=[anyspec] * n,
                          out_shape=[jax.ShapeDtypeStruct(_half_shape(p.shape, k), p.dtype) for p, k in zip(parts, kinds)],
                          scratch_shapes=[pltpu.SemaphoreType.DMA((n,)), pltpu.SemaphoreType.DMA((n,))],
                          compiler_params=pltpu.CompilerParams(has_side_effects=True))(*parts)


def _rs_owners(qbs, kinds, full_shapes):
    n = len(qbs)

    def body(*refs):
        q_refs, got_refs = refs[:n], refs[n:2 * n]
        send_sems, recv_sems = refs[2 * n:]
        x, y, c = _where_am_i()
        chips = [(1 - x, y), (x, 1 - y), (1 - x, 1 - y)]
        ds = []
        for a in range(n):
            for k, (cx, cy) in enumerate(chips):
                d = pltpu.make_async_remote_copy(src_ref=_shard_of_half(q_refs[a], kinds[a], 2 * cx + cy), dst_ref=got_refs[a].at[k],
                                                 send_sem=send_sems.at[3 * a + k], recv_sem=recv_sems.at[3 * a + k],
                                                 device_id=(cx, cy, c), device_id_type=MESH)
                d.start()
                ds.append(d)
        for d in ds:
            d.wait()

    anyspec = pl.BlockSpec(memory_space=pl.ANY)
    return pl.pallas_call(body, name="rs_to_owners", in_specs=[anyspec] * n, out_specs=[anyspec] * n,
                          out_shape=[jax.ShapeDtypeStruct((3,) + _shard_half_shape(fs, k), bf16) for fs, k in zip(full_shapes, kinds)],
                          scratch_shapes=[pltpu.SemaphoreType.DMA((3 * n,)), pltpu.SemaphoreType.DMA((3 * n,))],
                          compiler_params=pltpu.CompilerParams(has_side_effects=True))(*qbs)


def _rs_join(bufs, kinds):
    n = len(bufs)

    def body(*refs):
        out_refs = refs[n:2 * n]
        send_sems, recv_sems = refs[2 * n:]
        x, y, c = _where_am_i()

        def cp(a, h):
            blk = _half_of_shard(out_refs[a], kinds[a], h)
            return pltpu.make_async_remote_copy(src_ref=blk, dst_ref=blk, send_sem=send_sems.at[a], recv_sem=recv_sems.at[a],
                                                device_id=(x, y, 1 - c), device_id_type=MESH)

        for a in range(n):
            cp(a, c).start()
        for a in range(n):
            cp(a, c).wait_send()
            cp(a, 1 - c).wait_recv()

    anyspec = pl.BlockSpec(memory_space=pl.ANY)
    return pl.pallas_call(body, name="rs_join_halves", in_specs=[anyspec] * n, out_specs=[anyspec] * n,
                          out_shape=[jax.ShapeDtypeStruct(b.shape, b.dtype) for b in bufs],
                          input_output_aliases={a: a for a in range(n)},
                          scratch_shapes=[pltpu.SemaphoreType.DMA((n,)), pltpu.SemaphoreType.DMA((n,))],
                          compiler_params=pltpu.CompilerParams(has_side_effects=True))(*bufs)


def _all_reduce_small(g):
    R, C = g.shape
    H = R // 2
    NP = _nchunks(H, SUB)
    PR = H // NP

    def body(g_ref, out_ref, sib_ref, chip_ref, send_sems, recv_sems):
        x, y, c = _where_am_i()
        j = 2 * x + y
        sib = (x, y, 1 - c)
        chips = [(1 - x, y), (x, 1 - y), (1 - x, 1 - y)]
        rows = pl.ds(pl.multiple_of(c * H, SUB), H)

        def cp(k, src, dst, to):
            return pltpu.make_async_remote_copy(src_ref=src, dst_ref=dst, send_sem=send_sems.at[k], recv_sem=recv_sems.at[k],
                                                device_id=to, device_id_type=MESH)

        def pieces(k, src, dst, to):
            for q in range(NP):
                cp(k, src.at[pl.ds(q * PR, PR)], dst.at[pl.ds(q * PR, PR)], to).start()

        for half in range(2):
            pieces(0, g_ref.at[pl.ds(half * H, H)], sib_ref.at[pl.ds(half * H, H)], sib)
        cp(0, g_ref, sib_ref, sib).wait()
        chip_ref[j] = g_ref[rows, :] + sib_ref[rows, :]
        for k, (cx, cy) in enumerate(chips):
            pieces(1 + k, chip_ref.at[j], chip_ref.at[j], (cx, cy, c))
        for k, (cx, cy) in enumerate(chips):
            blk = chip_ref.at[2 * cx + cy]
            cp(1 + k, blk, blk, (cx, cy, c)).wait()
        out_ref[rows, :] = ((chip_ref[0] + chip_ref[1]) + chip_ref[2]) + chip_ref[3]
        other = out_ref.at[pl.ds(pl.multiple_of((1 - c) * H, SUB), H)]
        pieces(4, out_ref.at[rows], out_ref.at[rows], sib)
        cp(4, other, other, sib).wait()

    vm = pl.BlockSpec(memory_space=pltpu.VMEM)
    return pl.pallas_call(body, name="all_reduce_small", in_specs=[vm], out_specs=vm,
                          out_shape=jax.ShapeDtypeStruct((R, C), f32),
                          scratch_shapes=[pltpu.VMEM((R, C), f32), pltpu.VMEM((4, H, C), f32),
                                          pltpu.SemaphoreType.DMA((5,)), pltpu.SemaphoreType.DMA((5,))],
                          compiler_params=pltpu.CompilerParams(has_side_effects=True, vmem_limit_bytes=VMEM_LIMIT))(g)


def _pack_flat(arrs, rows_mult):
    flat = jnp.concatenate([a.reshape(-1) for a in arrs])
    n = flat.shape[0]
    tot = -(-n // (rows_mult * LANE)) * rows_mult * LANE
    return jnp.pad(flat, (0, tot - n)).reshape(-1, LANE)


def _unpack_flat(buf, shapes):
    flat = buf.reshape(-1)
    out, o = [], 0
    for s in shapes:
        n = math.prod(s)
        out.append(flat[o:o + n].reshape(s))
        o += n
    return out


_BIG = ("mem_kv_w", "w_out", "a_w_in", "b_w_in", "c_w_in", "d_w_in")
SMALL_ROWS_MULT = 256


def _row8(v):
    v = v.reshape(-1, v.shape[-1])
    return jnp.pad(v, ((0, SUB - v.shape[0]), (0, 0)))


def kernel(x, mem, mem_kv_w, ln_g, ln_b, w_out, hgrn_lb_logits, a_w_in, a_w_s, a_b_s, b_w_in, b_norm_g, c_w_in, c_w_pool, c_scale, d_w_in, d_conv_w, d_conv_b, d_w_gx, d_b_gx, d_w_ga, d_b_ga, d_a_param, loss_target, m_mem_kv_w, m_ln_g, m_ln_b, m_w_out, m_hgrn_lb_logits, m_a_w_in, m_a_w_s, m_a_b_s, m_b_w_in, m_b_norm_g, m_c_w_in, m_c_w_pool, m_c_scale, m_d_w_in, m_d_conv_w, m_d_conv_b, m_d_w_gx, m_d_b_gx, m_d_w_ga, m_d_b_ga, m_d_a_param, v_mem_kv_w, v_ln_g, v_ln_b, v_w_out, v_hgrn_lb_logits, v_a_w_in, v_a_w_s, v_a_b_s, v_b_w_in, v_b_norm_g, v_c_w_in, v_c_w_pool, v_c_scale, v_d_w_in, v_d_conv_w, v_d_conv_b, v_d_w_gx, v_d_b_gx, v_d_w_ga, v_d_b_ga, v_d_a_param):
    names = ["mem_kv_w", "ln_g", "ln_b", "w_out", "hgrn_lb_logits", "a_w_in", "a_w_s", "a_b_s", "b_w_in", "b_norm_g", "c_w_in",
             "c_w_pool", "c_scale", "d_w_in", "d_conv_w", "d_conv_b", "d_w_gx", "d_b_gx", "d_w_ga", "d_b_ga", "d_a_param"]
    w = dict(mem_kv_w=mem_kv_w, ln_g=ln_g, ln_b=ln_b, w_out=w_out, hgrn_lb_logits=hgrn_lb_logits, a_w_in=a_w_in, a_w_s=a_w_s,
             a_b_s=a_b_s, b_w_in=b_w_in, b_norm_g=b_norm_g, c_w_in=c_w_in, c_w_pool=c_w_pool, c_scale=c_scale, d_w_in=d_w_in,
             d_conv_w=d_conv_w, d_conv_b=d_conv_b, d_w_gx=d_w_gx, d_b_gx=d_b_gx, d_w_ga=d_w_ga, d_b_ga=d_b_ga, d_a_param=d_a_param)
    m = dict(zip(names, [m_mem_kv_w, m_ln_g, m_ln_b, m_w_out, m_hgrn_lb_logits, m_a_w_in, m_a_w_s, m_a_b_s, m_b_w_in, m_b_norm_g,
                         m_c_w_in, m_c_w_pool, m_c_scale, m_d_w_in, m_d_conv_w, m_d_conv_b, m_d_w_gx, m_d_b_gx, m_d_w_ga,
                         m_d_b_ga, m_d_a_param]))
    v = dict(zip(names, [v_mem_kv_w, v_ln_g, v_ln_b, v_w_out, v_hgrn_lb_logits, v_a_w_in, v_a_w_s, v_a_b_s, v_b_w_in, v_b_norm_g,
                         v_c_w_in, v_c_w_pool, v_c_scale, v_d_w_in, v_d_conv_w, v_d_conv_b, v_d_w_gx, v_d_b_gx, v_d_w_ga,
                         v_d_b_ga, v_d_a_param]))
    xi, yi = lax.axis_index("x"), lax.axis_index("y")
    jshard = 2 * xi + yi
    x2 = x[0]
    mem2 = mem[0]
    tgt2 = loss_target[0]

    kinds = ("kv", "wout", "win", "win", "win", "win")
    big2d = lambda d: [d["mem_kv_w"], d["w_out"]] + [d[n][0] for n in _BIG[2:]]
    gath = _ag_weights([a.astype(bf16) for a in big2d(w)], kinds, jshard)
    w_kv = gath[0].reshape(D_MODEL, 2 * XW)
    w_outs = [gath[1][:, l].reshape(D_MODEL, D_MODEL) for l in range(DEPTH)]
    w_outT = gath[1].transpose(1, 3, 0, 2).reshape(DEPTH, D_MODEL, D_MODEL)
    w_in = [g.transpose(1, 0, 2).reshape(D_MODEL, -1) for g in gath[2:]]
    w_inT = [g.transpose(0, 2, 1).reshape(-1, D_MODEL) for g in gath[2:]]

    def gather_small(shard):
        z = jnp.zeros((4, POOL_GROUP), f32)
        return lax.dynamic_update_slice(z, shard.reshape(1, POOL_GROUP), (jshard, 0))

    sm_sh = jnp.concatenate([gather_small(b_norm_g), gather_small(c_scale), gather_small(d_conv_b), gather_small(d_a_param)]
                            + [gather_small(d_conv_w[:, r]) for r in range(4)], axis=0)
    ci = lax.axis_index("c")
    sm_all = _all_reduce_small(_pack_flat([jnp.where(ci == 0, sm_sh, 0.0)], SUB * 2))
    sm = _unpack_flat(sm_all, [(8, 4 * POOL_GROUP)])[0]
    ng_full, scale_full, convb_full, ap_full = sm[0:1], sm[1:2], sm[2:3], sm[3:4]
    convw_full = sm[4:8]

    tril = jnp.tril(jnp.ones((HD, HD), bool))
    wtri = jnp.where(tril, a_w_s[0], 0.0)
    wbd = jnp.zeros((TOK, TOK), f32)
    for g in range(4):
        wbd = lax.dynamic_update_slice(wbd, c_w_pool[0, g], (g * POOL_GROUP, g * POOL_GROUP))
    kh, khT, vh, vhT, p_soft = _prep(mem2, w_kv, hgrn_lb_logits)
    prm = [
        dict(wtri=wtri.astype(bf16), wtriT=wtri.transpose(0, 2, 1).astype(bf16),
             bcolb=jnp.broadcast_to(a_b_s[0][:, :, None], (NH, HD, HD))),
        dict(lb=p_soft[1:2], ng=ng_full),
        dict(wbd=wbd.astype(bf16), wbdT=wbd.T.astype(bf16), scale=scale_full),
        dict(cw=_row8(convw_full), cb=convb_full, wgx=d_w_gx[0].astype(bf16), wgxT=d_w_gx[0].transpose(0, 2, 1).astype(bf16),
             bgx=d_b_gx.reshape(1, TOK), wga=d_w_ga[0].astype(bf16), wgaT=d_w_ga[0].transpose(0, 2, 1).astype(bf16),
             bga=d_b_ga.reshape(1, TOK), ap=ap_full),
    ]

    acts = []
    h = x2
    for l in range(DEPTH):
        outs = _fwd_layer(l, h, w_in[l], w_outs[l], ln_g[l:l + 1], ln_b[l:l + 1], khT, vh, prm[l],
                          tgt2 if l == DEPTH - 1 else None)
        nfix = 4 if l == DEPTH - 1 else 3
        acts.append(dict(xin=h, proj=outs[1], z=outs[2], saves=outs[nfix:]))
        if l == DEPTH - 1:
            loss_part = outs[3]
        h = outs[0]
    loss = lax.psum(0.5 / D_MODEL * jnp.sum(loss_part), ("x", "y", "c"))

    dh = h
    gw_in = [None] * DEPTH
    gw_out = None
    dln = [None] * DEPTH
    dks, dvs = [None] * DEPTH, [None] * DEPTH
    sgr = [None] * DEPTH
    for l in reversed(range(DEPTH)):
        a = acts[l]
        (dxin, dproj, mixedb, dyb, dln[l], dks[l], dvs[l]), sgr[l] = _bwd_layer(
            l, dh, a["z"], a["proj"], w_inT[l], w_outT[l], ln_g[l:l + 1], kh, khT, vh, vhT, prm[l], a["saves"])
        if _OFFS[l]["W"] // 4 % LANE:
            gw_in[l] = _tn_gemm(a["xin"], dproj, f"grad_w_in{l}", 2).reshape(D_MODEL, 4, -1).transpose(1, 0, 2)
        else:
            gw_in[l] = _tn_gemm_sharded(a["xin"], dproj, f"grad_w_in{l}")
        gw_out = _tn_gemm_slab(mixedb, dyb, gw_out, l, f"grad_w_out{l}")
        dh = dxin
    grad_x = dh[None]

    parts = [_kv_bwd(mem2, dks, dvs), gw_out] + gw_in
    jc = jnp.stack([jshard, ci]).astype(jnp.int32)
    gots = _rs_swap(parts, kinds)
    sums = [_add_sibling(p, g, k, jc, f"rs_add_sibling{a}") for a, (p, g, k) in enumerate(zip(parts, gots, kinds))]
    gots2 = _rs_owners([s[1] for s in sums], kinds, [p.shape for p in parts])
    gbig = _rs_join([_add_chips(s[0], r, k, jc, f"rs_add_chips{a}") for a, (s, r, k) in enumerate(zip(sums, gots2, kinds))], kinds)
    g_sh, d_sh, m_sh, v_sh = {}, {}, {}, {}
    for a, n in enumerate(_BIG):
        as2d = lambda t: t.reshape(-1, t.shape[-1])
        upd = _adamw(as2d(w[n]), as2d(gbig[a]), as2d(m[n]), as2d(v[n]), f"adamw_{n}")
        g_sh[n] = gbig[a].reshape(w[n].shape)
        d_sh[n], m_sh[n], v_sh[n] = (u.reshape(w[n].shape) for u in upd)

    dbs, dlogits = _small_finish(sgr[0]["dbacc"], p_soft, sgr[1]["dlb"])
    gs = {
        "ln_g": jnp.concatenate([dln[l][0:1] for l in range(DEPTH)], axis=0),
        "ln_b": jnp.concatenate([dln[l][1:2] for l in range(DEPTH)], axis=0),
        "hgrn_lb_logits": dlogits,
        "a_w_s": sgr[0]["dwtri"][None],
        "a_b_s": dbs[:, 0:NH].T[None],
        "b_norm_g": sgr[1]["dng"][0:1],
        "c_w_pool": jnp.stack([sgr[2]["dwbd"][g * POOL_GROUP:(g + 1) * POOL_GROUP, g * POOL_GROUP:(g + 1) * POOL_GROUP]
                               for g in range(4)])[None],
        "c_scale": sgr[2]["dscale"][0:1],
        "d_conv_w": sgr[3]["dcw"][0:4][None],
        "d_conv_b": sgr[3]["dvec"][3:4],
        "d_w_gx": sgr[3]["dwgx"][None],
        "d_b_gx": sgr[3]["dvec"][1:2].reshape(1, NH, HD),
        "d_w_ga": sgr[3]["dwga"][None],
        "d_b_ga": sgr[3]["dvec"][2:3].reshape(1, NH, HD),
        "d_a_param": sgr[3]["dvec"][0:1],
    }
    small = [n for n in names if n not in _BIG]
    full_shapes = [gs[n].shape for n in small]
    gsum = dict(zip(small, _unpack_flat(_all_reduce_small(_pack_flat([gs[n] for n in small], SMALL_ROWS_MULT)), full_shapes)))
    for n in ("b_norm_g", "c_scale", "d_conv_b", "d_a_param"):
        gsum[n] = lax.dynamic_slice(gsum[n], (0, jshard * POOL_GROUP), (1, POOL_GROUP))
    gsum["d_conv_w"] = lax.dynamic_slice(gsum["d_conv_w"], (0, 0, jshard * POOL_GROUP), (1, 4, POOL_GROUP))
    shapes = [w[n].shape for n in small]
    pk = lambda d: _pack_flat([d[n] for n in small], 2 * SMALL_ROWS_MULT)
    d_s, m_s, v_s = _adamw(pk(w), pk(gsum), pk(m), pk(v), "adamw_small")
    d_sm = dict(zip(small, _unpack_flat(d_s, shapes)))
    m_sm = dict(zip(small, _unpack_flat(m_s, shapes)))
    v_sm = dict(zip(small, _unpack_flat(v_s, shapes)))

    grads = {**gsum, **g_sh}
    deltas = {**d_sm, **d_sh}
    new_m = {**m_sm, **m_sh}
    new_v = {**v_sm, **v_sh}
    return (loss, grad_x, *[grads[n] for n in names], *[deltas[n] for n in names], *[new_m[n] for n in names],
            *[new_v[n] for n in names])
```

```python
import functools
import math

import jax
import jax.numpy as jnp
from jax import lax
from jax.experimental import pallas as pl
from jax.experimental.pallas import tpu as pltpu

f32 = jnp.float32
bf16 = jnp.bfloat16
MM = bf16

D_MODEL = 1024
TOK = 768
XW = 256
XHEADS = 4
XDIM = 64
HD = 128
NH = TOK // HD
CHUNK = 16
POOL_GROUP = 192
DEPTH = 4
ALPHA = (2 * DEPTH) ** 0.25
LN_EPS = 1e-5
RMS_EPS = 1e-6
LRU_C = 8.0
ADAM_LR, ADAM_B1, ADAM_B2, ADAM_EPS, ADAM_WD, ADAM_STEP = 0.001, 0.9, 0.999, 1e-08, 0.01, 10

_TS = (256, 128, 256, 256)
TK = 512
SUB = 8
LANE = 128
VMEM_LIMIT = 58 * 1024 * 1024

_OFFS = (
    dict(u=0, v=768, qx=1536, gate=1792, W=2816),
    dict(q=0, f=768, i=1536, qx=2304, gate=2560, W=3584),
    dict(p=0, qx=768, gate=1024, W=2048),
    dict(xb=0, qx=768, gate=1024, W=2048),
)
_PRM = (
    ("wtri", "wtriT", "bcolb"),
    ("lb", "ng"),
    ("wbd", "wbdT", "scale"),
    ("cw", "cb", "wgx", "wgxT", "bgx", "wga", "wgaT", "bga", "ap"),
)
MESH = pl.DeviceIdType.MESH


def _mm(a, b):
    return jnp.dot(a.astype(MM), b.astype(MM), preferred_element_type=f32)


def _mm_nt(a, b):
    return lax.dot_general(a.astype(MM), b.astype(MM), (((1,), (1,)), ((), ())), preferred_element_type=f32)


def _mm_tn(a, b):
    return lax.dot_general(a.astype(MM), b.astype(MM), (((0,), (0,)), ((), ())), preferred_element_type=f32)


def _mm_sel(sel, b):
    s = sel.astype(bf16)
    hi = b.astype(bf16)
    lo = (b - hi.astype(f32)).astype(bf16)
    return jnp.dot(s, hi, preferred_element_type=f32) + jnp.dot(s, lo, preferred_element_type=f32)


def _sig(x):
    return jax.nn.sigmoid(x)


_GC = math.sqrt(2.0 / math.pi)


def _gelu(x):
    t = jnp.tanh(_GC * (x + 0.044715 * x * x * x))
    return 0.5 * x * (1.0 + t), t


def _gelu_grad(x, t):
    return 0.5 * (1.0 + t) + 0.5 * x * (1.0 - t * t) * _GC * (1.0 + 3.0 * 0.044715 * x * x)


def _rowsum(x):
    return jnp.sum(x, axis=0, keepdims=True)


def _lmean(x):
    return jnp.mean(x, axis=-1, keepdims=True)


def _ln(z):
    mu = _lmean(z)
    zc = z - mu
    rstd = lax.rsqrt(_lmean(zc * zc) + LN_EPS)
    return zc * rstd, rstd


def _ln_bwd(dxh, xhat, rstd):
    return rstd * (dxh - _lmean(dxh) - xhat * _lmean(dxh * xhat))


def _hs(h):
    return slice(h * HD, (h + 1) * HD)


def _expm1(x):
    small = x * (1.0 + x * 0.5 * (1.0 + x * (1.0 / 3.0) * (1.0 + x * 0.25 * (1.0 + x * 0.2 * (1.0 + x * (1.0 / 6.0))))))
    return jnp.where(jnp.abs(x) < 0.25, small, jnp.exp(x) - 1.0)


def _softplus(x):
    e = jnp.exp(-jnp.abs(x))
    l1p = jnp.where(e < 1e-4, e - 0.5 * e * e, jnp.log(1.0 + e))
    return jnp.maximum(x, 0.0) + l1p


def _scan_fwd(a, b):
    n = a.shape[0]
    row = lax.broadcasted_iota(jnp.int32, a.shape, 0)
    d = 1
    while d < n:
        if d % SUB:
            m = row >= d
            b = jnp.where(m, a * pltpu.roll(b, d, 0) + b, b)
            a = jnp.where(m, a * pltpu.roll(a, d, 0), a)
        else:
            b = a * jnp.concatenate([jnp.zeros((d,) + b.shape[1:], f32), b[:n - d]], axis=0) + b
            a = a * jnp.concatenate([jnp.ones((d,) + a.shape[1:], f32), a[:n - d]], axis=0)
        d *= 2
    return a, b


def _scan_bwd(a, b):
    n = a.shape[0]
    row = lax.broadcasted_iota(jnp.int32, a.shape, 0)
    d = 1
    while d < n:
        if d % SUB:
            m = row < n - d
            b = jnp.where(m, a * pltpu.roll(b, n - d, 0) + b, b)
            a = jnp.where(m, a * pltpu.roll(a, n - d, 0), a)
        else:
            b = a * jnp.concatenate([b[d:], jnp.zeros((d,) + b.shape[1:], f32)], axis=0) + b
            a = a * jnp.concatenate([a[d:], jnp.ones((d,) + a.shape[1:], f32)], axis=0)
        d *= 2
    return a, b


def _chunk_mats(n):
    r = lax.broadcasted_iota(jnp.int32, (n, n), 0)
    c = lax.broadcasted_iota(jnp.int32, (n, n), 1)
    same = (r // CHUNK) == (c // CHUNK)
    return same, jnp.logical_and(same, c <= r)


def _pool_w(shape):
    lane = lax.broadcasted_iota(jnp.int32, shape, 1)
    return jnp.where(lane < POOL_GROUP, 2, jnp.where(lane < 2 * POOL_GROUP, 4, jnp.where(lane < 3 * POOL_GROUP, 8, 16)))


def _pool_pick(r1, r2, r3, r4):
    lane = lax.broadcasted_iota(jnp.int32, r1.shape, 1)
    return jnp.where(lane < POOL_GROUP, r1, jnp.where(lane < 2 * POOL_GROUP, r2, jnp.where(lane < 3 * POOL_GROUP, r3, r4)))


def _const_spec(a):
    nd = a.ndim
    return pl.BlockSpec(a.shape, lambda i, _nd=nd: (0,) * _nd, pipeline_mode=pl.Buffered(1))


def _acc_spec(shape):
    nd = len(shape)
    return pl.BlockSpec(shape, lambda i, _nd=nd: (0,) * _nd)


def _params(sem="arbitrary"):
    return pltpu.CompilerParams(dimension_semantics=(sem,), vmem_limit_bytes=VMEM_LIMIT)


def _xattn_fwd(qx, khT_ref, vh_ref):
    xo = jnp.zeros((qx.shape[0], XW), f32)
    ps = []
    for h in range(XHEADS):
        s = _mm(qx, khT_ref[h]) * (XDIM ** -0.5)
        e = jnp.exp(s - jnp.max(s, axis=-1, keepdims=True))
        p = e / jnp.sum(e, axis=-1, keepdims=True)
        xo = xo + _mm(p, vh_ref[h])
        ps.append(p)
    return xo, ps


def _hgrn_parallel(q_raw, fl, lb):
    n = q_raw.shape[0]
    same, tri = _chunk_mats(n)
    sq = _sig(q_raw)
    qf = q_raw * sq
    sgm = _sig(fl)
    f = lb + (1.0 - lb) * sgm
    logf = jnp.log(f)
    k = 1.0 - f
    g = _mm_sel(tri, logf)
    gl = _mm_sel(same, logf)
    eg = jnp.exp(g)
    eng = jnp.exp(-g)
    ee = jnp.exp(gl - g)
    return dict(sq=sq, qf=qf, sgm=sgm, f=f, k=k, eg=eg, eng=eng, ee=ee, q_dec=qf * eg, k_inv=k * eng, k_end=k * ee,
                a=jnp.exp(gl))


def _hgrn_intra(q_dec, k_inv, v):
    n = q_dec.shape[0]
    _, tri = _chunk_mats(HD)
    outs = []
    for h in range(NH):
        blks = []
        for b in range(n // HD):
            rs = slice(b * HD, (b + 1) * HD)
            sc = jnp.where(tri, _mm_nt(q_dec[rs, _hs(h)], k_inv[rs, _hs(h)]), 0.0)
            blks.append(_mm(sc, v[rs, _hs(h)]))
        outs.append(jnp.concatenate(blks, axis=0))
    return jnp.concatenate(outs, axis=-1)


def _cs(c):
    return slice(c * CHUNK, (c + 1) * CHUNK)


def _hgrn_inter_fwd(qdec_s, kend_s, v_s, a_s, oint_s, st_ref, states_s, u_s):
    n = qdec_s.shape[0] // CHUNK
    for c in range(n):
        for h in range(NH):
            u_s[c, h] = _mm_tn(v_s[_cs(c), _hs(h)], kend_s[_cs(c), _hs(h)])
    for h in range(NH):
        st = st_ref[h]
        for c in range(n):
            states_s[c, h] = st
            st = st * a_s[c * CHUNK:c * CHUNK + 1, _hs(h)] + u_s[c, h]
        st_ref[h] = st
    if oint_s is None:
        return
    for c in range(n):
        for h in range(NH):
            oint_s[_cs(c), _hs(h)] = _mm_nt(qdec_s[_cs(c), _hs(h)], states_s[c, h])


def _rms(o):
    outs, rs = [], []
    for h in range(NH):
        oh = o[:, _hs(h)]
        r = lax.rsqrt(_lmean(oh * oh) + RMS_EPS)
        outs.append(oh * r)
        rs.append(r)
    return jnp.concatenate(outs, axis=-1), rs


def _gmlp_core(u_raw, v_raw, wtri_ref, bcolb_ref):
    gu, tu = _gelu(u_raw)
    gv, tv = _gelu(v_raw)
    vns, rstds, mixeds = [], [], []
    for h in range(NH):
        vn, rstd = _ln(gv[:, _hs(h)])
        blks = []
        for n in range(u_raw.shape[0] // HD):
            blks.append(_mm(wtri_ref[h], vn[n * HD:(n + 1) * HD]) + bcolb_ref[h])
        vns.append(vn)
        rstds.append(rstd)
        mixeds.append(jnp.concatenate(blks, axis=0))
    mixed = jnp.concatenate(mixeds, axis=-1)
    return gu, tu, tv, vns, rstds, mixed


def _pool_core(p, carry, row0, wbd_ref):
    ext = jnp.concatenate([carry, p], axis=0)
    r1 = ext + pltpu.roll(ext, 1, 0)
    r2 = r1 + pltpu.roll(r1, 2, 0)
    r3 = r2 + pltpu.roll(r2, 4, 0)
    r4 = r3 + pltpu.roll(r3, 8, 0)
    sel = _pool_pick(r1, r2, r3, r4)[2 * SUB:]
    grow = row0 + lax.broadcasted_iota(jnp.int32, p.shape, 0)
    inv_cnt = 1.0 / jnp.minimum(grow + 1, _pool_w(p.shape)).astype(f32)
    diff = sel * inv_cnt - p
    return diff, inv_cnt, _mm(diff, wbd_ref[...])


def _lru_core(xb, ccar, row0, p):
    ext = jnp.concatenate([ccar, xb], axis=0)
    cw = p["cw"]
    x1, x2, x3 = pltpu.roll(ext, 1, 0)[SUB:], pltpu.roll(ext, 2, 0)[SUB:], pltpu.roll(ext, 3, 0)[SUB:]
    xc = cw[3:4, :] * xb + cw[2:3, :] * x1 + cw[1:2, :] * x2 + cw[0:1, :] * x3 + p["cb"][...]
    gxs, gas = [], []
    for h in range(NH):
        gxs.append(_mm(xc[:, _hs(h)], p["wgx"][h]))
        gas.append(_mm(xc[:, _hs(h)], p["wga"][h]))
    gx = _sig(jnp.concatenate(gxs, axis=-1) + p["bgx"][...])
    ga = _sig(jnp.concatenate(gas, axis=-1) + p["bga"][...])
    sp = _softplus(-p["ap"][...])
    la = -LRU_C * ga * sp
    a = jnp.exp(la)
    grow = row0 + lax.broadcasted_iota(jnp.int32, xb.shape, 0)
    first = grow == 0
    mult = jnp.where(first, 1.0, jnp.sqrt(-_expm1(2.0 * la)))
    bt = mult * gx * xc
    return dict(x1=x1, x2=x2, x3=x3, xc=xc, gx=gx, ga=ga, sp=sp, a=a, mult=mult, bt=bt, first=first)


def _fwd_layer(kind, xin, w_in, w_out, lng, lnb, khT, vh, prm, tgt):
    S = xin.shape[0]
    TS = _TS[kind]
    nt = S // TS
    off = _OFFS[kind]
    W = off["W"]
    last = tgt is not None
    pnames = _PRM[kind]
    pvals = [prm[n] for n in pnames]

    def body(*refs):
        it = iter(refs)
        xin_ref, win_ref, wout_ref, lng_ref, lnb_ref, khT_ref, vh_ref = (next(it) for _ in range(7))
        p = {n: next(it) for n in pnames}
        tgt_ref = next(it) if last else None
        xout_ref, proj_ref, z_ref = next(it), next(it), next(it)
        loss_ref = next(it) if last else None
        rest = list(it)
        i = pl.program_id(0)
        x = xin_ref[...]
        proj_ref[...] = _mm(x, win_ref[...])

        if kind == 0:
            gu, _, _, _, _, mixed = _gmlp_core(proj_ref[:, 0:TOK], proj_ref[:, TOK:2 * TOK], p["wtri"], p["bcolb"])
            tok = gu * mixed
        elif kind == 1:
            st_save, o_save, st_ref, states_s, u_s, qdec_s, kend_s, v_s, a_s, oint_s = rest

            @pl.when(i == 0)
            def _():
                st_ref[...] = jnp.zeros_like(st_ref)

            st_save[0] = st_ref[...]
            v = proj_ref[:, 2 * TOK:3 * TOK]
            hp = _hgrn_parallel(proj_ref[:, 0:TOK], proj_ref[:, TOK:2 * TOK], p["lb"][...])
            qdec_s[...] = hp["q_dec"]
            kend_s[...] = hp["k_end"]
            v_s[...] = v
            a_s[...] = hp["a"]
            o_intra = _hgrn_intra(hp["q_dec"], hp["k_inv"], v)
            _hgrn_inter_fwd(qdec_s, kend_s, v_s, a_s, oint_s, st_ref, states_s, u_s)
            o = o_intra + oint_s[...]
            o_save[0] = o
            on, _ = _rms(o)
            tok = on * p["ng"][...]
        elif kind == 2:
            pc_save, pcar = rest

            @pl.when(i == 0)
            def _():
                pcar[...] = jnp.zeros_like(pcar)

            pc_save[0] = pcar[...]
            pp = proj_ref[:, 0:TOK]
            _, _, y = _pool_core(pp, pcar[...], i * TS, p["wbd"])
            pcar[...] = pp[TS - 2 * SUB:, :]
            tok = y * p["scale"][...]
        else:
            cc_save, hc_save, h_save, ccar, hcar = rest

            @pl.when(i == 0)
            def _():
                ccar[...] = jnp.zeros_like(ccar)
                hcar[...] = jnp.zeros_like(hcar)

            cc_save[0] = ccar[...]
            hc_save[0] = hcar[...]
            xb = proj_ref[:, 0:TOK]
            lc = _lru_core(xb, ccar[...], i * TS, p)
            P, B = _scan_fwd(lc["a"], lc["bt"])
            tok = P * hcar[SUB - 1:SUB, :] + B
            h_save[0] = tok
            ccar[...] = xb[TS - SUB:, :]
            hcar[...] = tok[TS - SUB:, :]

        xo, _ = _xattn_fwd(proj_ref[:, off["qx"]:off["qx"] + XW], khT_ref, vh_ref)
        gate = proj_ref[:, off["gate"]:off["gate"] + D_MODEL]
        mixed = jnp.concatenate([tok, xo], axis=-1) * (gate * _sig(gate))
        z = ALPHA * x + _mm(mixed, wout_ref[...])
        z_ref[...] = z
        xhat, _ = _ln(z)
        xout = xhat * lng_ref[...] + lnb_ref[...]
        if last:
            e = xout - tgt_ref[...]
            xout_ref[...] = e * (1.0 / D_MODEL)
            es = _rowsum(e * e)
            tot = es[:, 0:LANE]
            for j in range(1, D_MODEL // LANE):
                tot = tot + es[:, j * LANE:(j + 1) * LANE]

            @pl.when(i == 0)
            def _():
                loss_ref[...] = jnp.zeros_like(loss_ref)

            loss_ref[0:1, :] += tot
        else:
            xout_ref[...] = xout

    tile = lambda w: pl.BlockSpec((TS, w), lambda i: (i, 0))
    in_arrays = [xin, w_in, w_out, lng, lnb, khT, vh] + pvals + ([tgt] if last else [])
    in_specs = [tile(D_MODEL)] + [_const_spec(a) for a in in_arrays[1:7 + len(pvals)]] + ([tile(D_MODEL)] if last else [])
    out_shape = [jax.ShapeDtypeStruct((S, D_MODEL), f32), jax.ShapeDtypeStruct((S, W), f32), jax.ShapeDtypeStruct((S, D_MODEL), f32)]
    out_specs = [tile(D_MODEL), tile(W), tile(D_MODEL)]
    if last:
        out_shape.append(jax.ShapeDtypeStruct((SUB, LANE), f32))
        out_specs.append(_acc_spec((SUB, LANE)))
    scratch = []
    save = lambda *s: (jax.ShapeDtypeStruct((nt,) + s, f32), pl.BlockSpec((1,) + s, lambda i, _n=len(s): (i,) + (0,) * _n))
    if kind == 1:
        saved = [save(NH, HD, HD), save(TS, TOK)]
        scratch = ([pltpu.VMEM((NH, HD, HD), f32)] + [pltpu.VMEM((TS // CHUNK, NH, HD, HD), f32)] * 2
                   + [pltpu.VMEM((TS, TOK), f32)] * 5)
    elif kind == 2:
        saved = [save(2 * SUB, TOK)]
        scratch = [pltpu.VMEM((2 * SUB, TOK), f32)]
    elif kind == 3:
        saved = [save(SUB, TOK), save(SUB, TOK), save(TS, TOK)]
        scratch = [pltpu.VMEM((SUB, TOK), f32)] * 2
    else:
        saved = []
    for sh, sp in saved:
        out_shape.append(sh)
        out_specs.append(sp)
    return pl.pallas_call(body, name=f"fwd_layer{kind}", grid=(nt,), in_specs=in_specs, out_specs=out_specs,
                          out_shape=out_shape, scratch_shapes=scratch, compiler_params=_params())(*in_arrays)


def _small_grad_shapes(kind):
    if kind == 0:
        return dict(dwtri=(NH, HD, HD), dbacc=(NH, HD, HD))
    if kind == 1:
        return dict(dlb=(SUB, TOK), dng=(SUB, TOK))
    if kind == 2:
        return dict(dwbd=(TOK, TOK), dscale=(SUB, TOK))
    return dict(dcw=(SUB, TOK), dvec=(SUB, TOK), dwgx=(NH, HD, HD), dwga=(NH, HD, HD))


def _bwd_layer(kind, dxout, z, proj, w_inT, w_outT, lng, kh, khT, vh, vhT, prm, saves):
    S = dxout.shape[0]
    TS = _TS[kind]
    nt = S // TS
    off = _OFFS[kind]
    W = off["W"]
    pnames = _PRM[kind]
    pvals = [prm[n] for n in pnames]
    sg_shapes = _small_grad_shapes(kind)
    sg_names = list(sg_shapes)
    n_saves = len(saves)

    def body(*refs):
        it = iter(refs)
        dxo_ref, z_ref, proj_ref, winT_ref, woutT_ref, lng_ref, kh_ref, khT_ref, vh_ref, vhT_ref = (next(it) for _ in range(10))
        p = {n: next(it) for n in pnames}
        sv = [next(it) for _ in range(n_saves)]
        dxin_ref, dproj_ref, mixed_ref, dy_ref, dln_ref, dk_ref, dv_ref = (next(it) for _ in range(7))
        sg = {n: next(it) for n in sg_names}
        rest = list(it)
        step = pl.program_id(0)
        i = nt - 1 - step

        @pl.when(step == 0)
        def _():
            dln_ref[...] = jnp.zeros_like(dln_ref)
            dk_ref[...] = jnp.zeros_like(dk_ref)
            dv_ref[...] = jnp.zeros_like(dv_ref)
            for n in sg_names:
                sg[n][...] = jnp.zeros_like(sg[n])

        dxo = dxo_ref[...]
        xhat, rstd = _ln(z_ref[...])
        dln_ref[0:1, :] += _rowsum(dxo * xhat)
        dln_ref[1:2, :] += _rowsum(dxo)
        dz = _ln_bwd(dxo * lng_ref[...], xhat, rstd)
        dyb = dz.astype(bf16)
        dy_ref[...] = dyb
        dmixed = _mm(dyb, woutT_ref[...])

        aux = {}
        if kind == 0:
            u_raw, v_raw = proj_ref[:, 0:TOK], proj_ref[:, TOK:2 * TOK]
            gu, tu, tv, vns, rstds, mx = _gmlp_core(u_raw, v_raw, p["wtri"], p["bcolb"])
            tok = gu * mx
        elif kind == 1:
            st_save, o_save = sv
            (dst_ref, fst_ref, states_s, dsts_s, u_s, qdec_s, kend_s, v_s, a_s, do_s, dqdec_s, dkend_s, dv_s,
             dgl_s) = rest

            @pl.when(step == 0)
            def _():
                dst_ref[...] = jnp.zeros_like(dst_ref)

            fst_ref[...] = st_save[0]
            v = proj_ref[:, 2 * TOK:3 * TOK]
            hp = _hgrn_parallel(proj_ref[:, 0:TOK], proj_ref[:, TOK:2 * TOK], p["lb"][...])
            qdec_s[...] = hp["q_dec"]
            kend_s[...] = hp["k_end"]
            v_s[...] = v
            a_s[...] = hp["a"]
            _hgrn_inter_fwd(qdec_s, kend_s, v_s, a_s, None, fst_ref, states_s, u_s)
            o = o_save[0]
            on, rs = _rms(o)
            tok = on * p["ng"][...]
            aux = dict(hp=hp, v=v, o=o, on=on, rs=rs)
        elif kind == 2:
            pc_save, = sv
            dpcar, = rest
            pp = proj_ref[:, 0:TOK]
            diff, inv_cnt, y = _pool_core(pp, pc_save[0], i * TS, p["wbd"])
            tok = y * p["scale"][...]
        else:
            cc_save, hc_save, h_save = sv
            dccar, gcar = rest
            xb = proj_ref[:, 0:TOK]
            lc = _lru_core(xb, cc_save[0], i * TS, p)
            hin = hc_save[0, SUB - 1:SUB, :]
            tok = h_save[0]

        xo, ps = _xattn_fwd(proj_ref[:, off["qx"]:off["qx"] + XW], khT_ref, vh_ref)
        gate = proj_ref[:, off["gate"]:off["gate"] + D_MODEL]
        sgm = _sig(gate)
        sgate = gate * sgm
        cat = jnp.concatenate([tok, xo], axis=-1)
        mixed_ref[...] = (cat * sgate).astype(bf16)
        dcat = dmixed * sgate
        dproj_ref[:, off["gate"]:off["gate"] + D_MODEL] = (dmixed * cat * (sgm * (1.0 + gate * (1.0 - sgm)))).astype(bf16)
        dtok = dcat[:, 0:TOK]
        dxo_att = dcat[:, TOK:]

        qx = proj_ref[:, off["qx"]:off["qx"] + XW]
        dqx = jnp.zeros((TS, XW), f32)
        for h in range(XHEADS):
            dp = _mm(dxo_att, vhT_ref[h])
            ds = ps[h] * (dp - jnp.sum(dp * ps[h], axis=-1, keepdims=True)) * (XDIM ** -0.5)
            dqx = dqx + _mm(ds, kh_ref[h])
            dk_ref[h] += _mm_tn(ds, qx)
            dv_ref[h] += _mm_tn(ps[h], dxo_att)
        dproj_ref[:, off["qx"]:off["qx"] + XW] = dqx.astype(bf16)

        if kind == 0:
            tril = lax.broadcasted_iota(jnp.int32, (HD, HD), 1) <= lax.broadcasted_iota(jnp.int32, (HD, HD), 0)
            dgu = dtok * mx
            dmx = dtok * gu
            dgvs = []
            for h in range(NH):
                dmh = dmx[:, _hs(h)]
                blks = []
                for n in range(TS // HD):
                    rs_ = slice(n * HD, (n + 1) * HD)
                    blks.append(_mm(p["wtriT"][h], dmh[rs_]))
                    sg["dwtri"][h] += jnp.where(tril, _mm_nt(dmh[rs_], vns[h][rs_]), 0.0)
                    sg["dbacc"][h] += dmh[rs_]
                dgvs.append(_ln_bwd(jnp.concatenate(blks, axis=0), vns[h], rstds[h]))
            dgv = jnp.concatenate(dgvs, axis=-1)
            dproj_ref[:, 0:TOK] = (dgu * _gelu_grad(u_raw, tu)).astype(bf16)
            dproj_ref[:, TOK:2 * TOK] = (dgv * _gelu_grad(v_raw, tv)).astype(bf16)
        elif kind == 1:
            hp, v, o, on, rs = aux["hp"], aux["v"], aux["o"], aux["on"], aux["rs"]
            ng = p["ng"][...]
            sg["dng"][0:1, :] += _rowsum(dtok * on)
            dn = dtok * ng
            dos = []
            for h in range(NH):
                oh, r = o[:, _hs(h)], rs[h]
                dos.append(r * (dn[:, _hs(h)] - oh * (r * r) * _lmean(dn[:, _hs(h)] * oh)))
            do = jnp.concatenate(dos, axis=-1)
            do_s[...] = do
            _, tri = _chunk_mats(HD)
            dqd, dki, dvi = [], [], []
            for h in range(NH):
                bq, bk, bv = [], [], []
                for b in range(TS // HD):
                    rs_ = slice(b * HD, (b + 1) * HD)
                    qd, ki = hp["q_dec"][rs_, _hs(h)], hp["k_inv"][rs_, _hs(h)]
                    sc = jnp.where(tri, _mm_nt(qd, ki), 0.0)
                    dsc = jnp.where(tri, _mm_nt(do[rs_, _hs(h)], v[rs_, _hs(h)]), 0.0)
                    bv.append(_mm_tn(sc, do[rs_, _hs(h)]))
                    bq.append(_mm(dsc, ki))
                    bk.append(_mm_tn(dsc, qd))
                dqd.append(jnp.concatenate(bq, axis=0))
                dki.append(jnp.concatenate(bk, axis=0))
                dvi.append(jnp.concatenate(bv, axis=0))
            dqdec_s[...] = jnp.concatenate(dqd, axis=-1)
            dk_inv = jnp.concatenate(dki, axis=-1)
            dv_s[...] = jnp.concatenate(dvi, axis=-1)
            row16 = lax.broadcasted_iota(jnp.int32, (CHUNK, HD), 0)

            nch = TS // CHUNK
            for c in range(nch):
                for h in range(NH):
                    u_s[c, h] = _mm_tn(do_s[_cs(c), _hs(h)], qdec_s[_cs(c), _hs(h)])
            for h in range(NH):
                dst = dst_ref[h]
                for c in reversed(range(nch)):
                    dsts_s[c, h] = dst
                    dst = dst * a_s[c * CHUNK:c * CHUNK + 1, _hs(h)] + u_s[c, h]
                dst_ref[h] = dst
            for c in range(nch):
                for h in range(NH):
                    stp = states_s[c, h]
                    dst = dsts_s[c, h]
                    dqdec_s[_cs(c), _hs(h)] += _mm(do_s[_cs(c), _hs(h)], stp)
                    dkend_s[_cs(c), _hs(h)] = _mm(v_s[_cs(c), _hs(h)], dst)
                    dv_s[_cs(c), _hs(h)] += _mm_nt(kend_s[_cs(c), _hs(h)], dst)
                    da = jnp.sum(dst * stp, axis=0, keepdims=True) * a_s[c * CHUNK:c * CHUNK + 1, _hs(h)]
                    dgl_s[_cs(c), _hs(h)] = jnp.where(row16 == 0, jnp.broadcast_to(da, (CHUNK, HD)), 0.0)
            dq_dec = dqdec_s[...]
            dk_end = dkend_s[...]
            same, _ = _chunk_mats(TS)
            triT = jnp.logical_and(same, lax.broadcasted_iota(jnp.int32, (TS, TS), 1) >= lax.broadcasted_iota(jnp.int32, (TS, TS), 0))
            dg = dq_dec * hp["q_dec"] - dk_inv * hp["k_inv"] - dk_end * hp["k_end"]
            dk = dk_inv * hp["eng"] + dk_end * hp["ee"]
            dglr = dk_end * hp["k_end"] + dgl_s[...]
            dlogf = _mm_sel(triT, dg) + _mm_sel(same, dglr)
            df = dlogf / hp["f"] - dk
            lb = p["lb"][...]
            sg["dlb"][0:1, :] += _rowsum(df * (1.0 - hp["sgm"]))
            q_raw = proj_ref[:, 0:TOK]
            dproj_ref[:, 0:TOK] = (dq_dec * hp["eg"] * (hp["sq"] * (1.0 + q_raw * (1.0 - hp["sq"])))).astype(bf16)
            dproj_ref[:, TOK:2 * TOK] = (df * (1.0 - lb) * hp["sgm"] * (1.0 - hp["sgm"])).astype(bf16)
            dproj_ref[:, 2 * TOK:3 * TOK] = dv_s[...].astype(bf16)
        elif kind == 2:
            @pl.when(step == 0)
            def _():
                dpcar[...] = jnp.zeros_like(dpcar)

            sg["dscale"][0:1, :] += _rowsum(dtok * y)
            dyp = dtok * p["scale"][...]
            sg["dwbd"][...] += _mm_tn(diff, dyp)
            ddiff = _mm(dyp, p["wbdT"][...])
            q = ddiff * inv_cnt
            ext = jnp.concatenate([q, dpcar[...]], axis=0)
            n = TS + 2 * SUB
            r1 = ext + pltpu.roll(ext, n - 1, 0)
            r2 = r1 + pltpu.roll(r1, n - 2, 0)
            r3 = r2 + pltpu.roll(r2, n - 4, 0)
            r4 = r3 + pltpu.roll(r3, n - 8, 0)
            dproj_ref[:, 0:TOK] = (_pool_pick(r1, r2, r3, r4)[:TS] - ddiff).astype(bf16)
            dpcar[...] = q[0:2 * SUB, :]
        else:
            @pl.when(step == 0)
            def _():
                dccar[...] = jnp.zeros_like(dccar)
                gcar[...] = jnp.zeros_like(gcar)

            a, mult, gx, ga, xc = lc["a"], lc["mult"], lc["gx"], lc["ga"], lc["xc"]
            row = lax.broadcasted_iota(jnp.int32, (TS, TOK), 0)
            an = jnp.where(row == TS - 1, 1.0, pltpu.roll(a, TS - 1, 0))
            Pb, Bb = _scan_bwd(an, dtok)
            lam = Pb * gcar[0:1, :] + Bb
            gcar[...] = (a * lam)[0:SUB, :]
            hprev = jnp.where(row == 0, jnp.broadcast_to(hin, (TS, TOK)), pltpu.roll(tok, 1, 0))
            dmult = lam * gx * xc
            dgx = lam * mult * xc
            dxc = lam * mult * gx
            dla = lam * hprev * a - jnp.where(lc["first"], 0.0, dmult * a * a / mult)
            sp = lc["sp"]
            dga = -LRU_C * sp * dla
            dsp = _rowsum(-LRU_C * ga * dla)
            sg["dvec"][0:1, :] += dsp * (-_sig(-p["ap"][...]))
            dpx = dgx * gx * (1.0 - gx)
            dpa = dga * ga * (1.0 - ga)
            sg["dvec"][1:2, :] += _rowsum(dpx)
            sg["dvec"][2:3, :] += _rowsum(dpa)
            dxcs = []
            for h in range(NH):
                dxcs.append(_mm(dpx[:, _hs(h)], p["wgxT"][h]) + _mm(dpa[:, _hs(h)], p["wgaT"][h]))
                sg["dwgx"][h] += _mm_tn(xc[:, _hs(h)], dpx[:, _hs(h)])
                sg["dwga"][h] += _mm_tn(xc[:, _hs(h)], dpa[:, _hs(h)])
            dxc = dxc + jnp.concatenate(dxcs, axis=-1)
            sg["dvec"][3:4, :] += _rowsum(dxc)
            sg["dcw"][3:4, :] += _rowsum(dxc * xb)
            sg["dcw"][2:3, :] += _rowsum(dxc * lc["x1"])
            sg["dcw"][1:2, :] += _rowsum(dxc * lc["x2"])
            sg["dcw"][0:1, :] += _rowsum(dxc * lc["x3"])
            ext = jnp.concatenate([dxc, dccar[...]], axis=0)
            n = TS + SUB
            cw = p["cw"]
            dproj_ref[:, 0:TOK] = (cw[3:4, :] * dxc + cw[2:3, :] * pltpu.roll(ext, n - 1, 0)[:TS]
                                   + cw[1:2, :] * pltpu.roll(ext, n - 2, 0)[:TS]
                                   + cw[0:1, :] * pltpu.roll(ext, n - 3, 0)[:TS]).astype(bf16)
            dccar[...] = dxc[0:SUB, :]

        dxin_ref[...] = ALPHA * dz + _mm(dproj_ref[...], winT_ref[...])

    rtile = lambda w: pl.BlockSpec((TS, w), lambda s: (nt - 1 - s, 0))
    consts = [w_inT, w_outT, lng, kh, khT, vh, vhT] + pvals
    in_arrays = [dxout, z, proj] + consts + list(saves)
    in_specs = [rtile(D_MODEL), rtile(D_MODEL), rtile(W)] + [_const_spec(a) for a in consts]
    for a in saves:
        in_specs.append(pl.BlockSpec((1,) + a.shape[1:], lambda s, _n=a.ndim - 1: (nt - 1 - s,) + (0,) * _n))
    out_shape = [jax.ShapeDtypeStruct((S, D_MODEL), f32), jax.ShapeDtypeStruct((S, W), bf16),
                 jax.ShapeDtypeStruct((S, D_MODEL), bf16), jax.ShapeDtypeStruct((S, D_MODEL), bf16),
                 jax.ShapeDtypeStruct((SUB, D_MODEL), f32), jax.ShapeDtypeStruct((XHEADS, XW, XW), f32),
                 jax.ShapeDtypeStruct((XHEADS, XW, XW), f32)]
    out_specs = [rtile(D_MODEL), rtile(W), rtile(D_MODEL), rtile(D_MODEL), _acc_spec((SUB, D_MODEL)),
                 _acc_spec((XHEADS, XW, XW)), _acc_spec((XHEADS, XW, XW))]
    for n in sg_names:
        out_shape.append(jax.ShapeDtypeStruct(sg_shapes[n], f32))
        out_specs.append(_acc_spec(sg_shapes[n]))
    if kind == 1:
        scratch = ([pltpu.VMEM((NH, HD, HD), f32)] * 2 + [pltpu.VMEM((TS // CHUNK, NH, HD, HD), f32)] * 3
                   + [pltpu.VMEM((TS, TOK), f32)] * 9)
    elif kind == 2:
        scratch = [pltpu.VMEM((2 * SUB, TOK), f32)]
    elif kind == 3:
        scratch = [pltpu.VMEM((SUB, TOK), f32)] * 2
    else:
        scratch = []
    outs = pl.pallas_call(body, name=f"bwd_layer{kind}", grid=(nt,), in_specs=in_specs, out_specs=out_specs,
                          out_shape=out_shape, scratch_shapes=scratch, compiler_params=_params())(*in_arrays)
    return outs[:7], dict(zip(sg_names, outs[7:]))


def _prep(mem, w_kv, logits):
    def body(mem_ref, w_ref, lg_ref, kh_ref, khT_ref, vh_ref, vhT_ref, p_ref):
        kv = _mm(mem_ref[...], w_ref[...])
        k, v = kv[:, 0:XW], kv[:, XW:]
        kT, vT = k.T, v.T
        col = lax.broadcasted_iota(jnp.int32, (XW, XW), 1) // XDIM
        row = lax.broadcasted_iota(jnp.int32, (XW, XW), 0) // XDIM
        for h in range(XHEADS):
            kh_ref[h] = jnp.where(col == h, k, 0.0).astype(bf16)
            vh_ref[h] = jnp.where(col == h, v, 0.0).astype(bf16)
            khT_ref[h] = jnp.where(row == h, kT, 0.0).astype(bf16)
            vhT_ref[h] = jnp.where(row == h, vT, 0.0).astype(bf16)
        lg = lg_ref[...]
        e = jnp.exp(lg - jnp.max(lg, axis=0, keepdims=True))
        p_ref[...] = e / jnp.sum(e, axis=0, keepdims=True)

    vm = pl.BlockSpec(memory_space=pltpu.VMEM)
    hs = jax.ShapeDtypeStruct((XHEADS, XW, XW), bf16)
    return pl.pallas_call(body, name="prep_memory", in_specs=[vm] * 3, out_specs=[vm] * 5,
                          out_shape=[hs, hs, hs, hs, jax.ShapeDtypeStruct(logits.shape, f32)])(mem, w_kv, logits)


def _kv_bwd(mem, dks, dvs):
    def body(mem_ref, *refs):
        out_ref = refs[-1]
        col = lax.broadcasted_iota(jnp.int32, (XW, XW), 1) // XDIM
        dk = jnp.zeros((XW, XW), f32)
        dv = jnp.zeros((XW, XW), f32)
        for l in range(DEPTH):
            for h in range(XHEADS):
                dk = dk + jnp.where(col == h, refs[l][h], 0.0)
                dv = dv + jnp.where(col == h, refs[DEPTH + l][h], 0.0)
        out_ref[:, 0:XW] = _mm_tn(mem_ref[...], dk)
        out_ref[:, XW:] = _mm_tn(mem_ref[...], dv)

    vm = pl.BlockSpec(memory_space=pltpu.VMEM)
    return pl.pallas_call(body, name="kv_bwd", in_specs=[vm] * (1 + 2 * DEPTH), out_specs=vm,
                          out_shape=jax.ShapeDtypeStruct((D_MODEL, 2 * XW), f32))(mem, *dks, *dvs)


def _tn_gemm(a, b, name, nb):
    S, M = a.shape
    N = b.shape[1]
    NB = N // nb
    nk = S // TK

    def body(a_ref, b_ref, o_ref):
        @pl.when(pl.program_id(1) == 0)
        def _():
            o_ref[...] = jnp.zeros_like(o_ref)

        o_ref[...] += _mm_tn(a_ref[...], b_ref[...])

    return pl.pallas_call(body, name=name, grid=(nb, nk),
                          in_specs=[pl.BlockSpec((TK, M), lambda j, k: (k, 0)), pl.BlockSpec((TK, NB), lambda j, k: (k, j))],
                          out_specs=pl.BlockSpec((M, NB), lambda j, k: (0, j)),
                          out_shape=jax.ShapeDtypeStruct((M, N), f32),
                          compiler_params=pltpu.CompilerParams(dimension_semantics=("parallel", "arbitrary"),
                                                               vmem_limit_bytes=VMEM_LIMIT))(a, b)


def _rows_block(R, mult=16, cap=1024):
    best = R
    for d in range(mult, min(R, cap) + 1, mult):
        if R % d == 0:
            best = d
    return best


def _tn_gemm_sharded(a, b, name):
    S, M = a.shape
    Wq = b.shape[1] // 4
    nk = S // TK

    def body(a_ref, b_ref, o_ref):
        @pl.when(pl.program_id(0) == 0)
        def _():
            o_ref[...] = jnp.zeros_like(o_ref)

        at = a_ref[...].astype(MM)
        for j in range(4):
            o_ref[j] += _mm_tn(at, b_ref[:, j * Wq:(j + 1) * Wq])

    return pl.pallas_call(body, name=name, grid=(nk,),
                          in_specs=[pl.BlockSpec((TK, M), lambda k: (k, 0)), pl.BlockSpec((TK, 4 * Wq), lambda k: (k, 0))],
                          out_specs=pl.BlockSpec((4, M, Wq), lambda k: (0, 0, 0)),
                          out_shape=jax.ShapeDtypeStruct((4, M, Wq), f32), compiler_params=_params())(a, b)


def _tn_gemm_slab(a, b, acc, l, name):
    S, M = a.shape
    N = b.shape[1]
    nk = S // TK

    def body(a_ref, b_ref, *refs):
        o_ref = refs[-1]

        @pl.when(pl.program_id(0) == 0)
        def _():
            o_ref[...] = jnp.zeros_like(o_ref)

        o_ref[...] += _mm_tn(a_ref[...], b_ref[...])

    ins = [a, b] + ([] if acc is None else [acc])
    in_specs = [pl.BlockSpec((TK, M), lambda k: (k, 0)), pl.BlockSpec((TK, N), lambda k: (k, 0))]
    if acc is not None:
        in_specs.append(pl.BlockSpec(memory_space=pl.ANY))
    return pl.pallas_call(body, name=name, grid=(nk,), in_specs=in_specs,
                          out_specs=pl.BlockSpec((None, M, N), lambda k: (l, 0, 0)),
                          out_shape=jax.ShapeDtypeStruct((DEPTH, M, N), f32),
                          input_output_aliases={} if acc is None else {2: 0},
                          compiler_params=_params())(*ins)


HALF_ROWS = D_MODEL // 2
SHARD_ROWS = D_MODEL // 4


def _half_of_full(ref, kind, h):
    if kind == "kv":
        return ref.at[:, pl.ds(h * XW, XW)]
    if kind == "wout":
        return ref.at[pl.ds(2 * h, 2)]
    return ref.at[:, pl.ds(h * HALF_ROWS, HALF_ROWS)]


def _shard_of_half(ref, kind, j):
    if kind == "kv":
        return ref.at[pl.ds(j * SHARD_ROWS, SHARD_ROWS)]
    if kind == "wout":
        return ref.at[:, pl.ds(j * SHARD_ROWS, SHARD_ROWS)]
    return ref.at[j]


def _half_of_shard(ref, kind, h):
    if kind == "kv":
        return ref.at[:, pl.ds(h * XW, XW)]
    if kind == "wout":
        return ref.at[pl.ds(2 * h, 2)]
    return ref.at[pl.ds(h * HALF_ROWS, HALF_ROWS)]


def _half_shape(full_shape, kind):
    if kind == "kv":
        return (full_shape[0], XW)
    if kind == "wout":
        return (2,) + tuple(full_shape[1:])
    return (4, HALF_ROWS, full_shape[2])


def _shard_half_shape(full_shape, kind):
    if kind == "kv":
        return (SHARD_ROWS, XW)
    if kind == "wout":
        return (2, SHARD_ROWS, full_shape[2])
    return (HALF_ROWS, full_shape[2])


def _shard_shape(full_shape, kind):
    if kind == "kv":
        return (SHARD_ROWS, full_shape[1])
    if kind == "wout":
        return (DEPTH, SHARD_ROWS, full_shape[2])
    return (D_MODEL, full_shape[2])


def _ew_call(body, name, grid, jc, ins, in_specs, out_shape, out_specs):
    gs = pltpu.PrefetchScalarGridSpec(num_scalar_prefetch=1, grid=grid, in_specs=in_specs, out_specs=out_specs)
    return pl.pallas_call(body, name=name, grid_spec=gs, out_shape=out_shape,
                          compiler_params=pltpu.CompilerParams(dimension_semantics=("parallel",) * len(grid),
                                                               vmem_limit_bytes=VMEM_LIMIT))(jc, *ins)


def _add_sibling(part, got, kind, jc, name):
    def body(jc_ref, a_ref, b_ref, o_ref, ob_ref):
        s = a_ref[...] + b_ref[...]
        o_ref[...] = s
        ob_ref[...] = s.astype(bf16)

    if kind == "kv":
        R = part.shape[0]
        grid = (2,)
        mine = pl.BlockSpec((R // 2, XW), lambda i, jc_ref: (i, jc_ref[1]))
        spec = pl.BlockSpec((R // 2, XW), lambda i, jc_ref: (i, 0))
    elif kind == "wout":
        _, R, C = part.shape
        grid = (2, 2)
        mine = pl.BlockSpec((None, R // 2, C), lambda s, i, jc_ref: (2 * jc_ref[1] + s, i, 0))
        spec = pl.BlockSpec((None, R // 2, C), lambda s, i, jc_ref: (s, i, 0))
    else:
        C = part.shape[2]
        grid = (4, 2)
        mine = pl.BlockSpec((None, HALF_ROWS // 2, C), lambda s, i, jc_ref: (s, 2 * jc_ref[1] + i, 0))
        spec = pl.BlockSpec((None, HALF_ROWS // 2, C), lambda s, i, jc_ref: (s, i, 0))
    hs = _half_shape(part.shape, kind)
    return _ew_call(body, name, grid, jc, [part, got], [mine, spec],
                    [jax.ShapeDtypeStruct(hs, f32), jax.ShapeDtypeStruct(hs, bf16)], [spec, spec])


def _add_chips(q32, r, kind, jc, name):
    def body(jc_ref, q_ref, r_ref, out_ref):
        out_ref[...] = ((q_ref[...] + r_ref[0].astype(f32)) + r_ref[1].astype(f32)) + r_ref[2].astype(f32)

    if kind == "kv":
        grid = (1,)
        qs = pl.BlockSpec((SHARD_ROWS, XW), lambda i, jc_ref: (jc_ref[0], 0))
        rs = pl.BlockSpec((3, SHARD_ROWS, XW), lambda i, jc_ref: (0, 0, 0))
        os_ = pl.BlockSpec((SHARD_ROWS, XW), lambda i, jc_ref: (0, jc_ref[1]))
    elif kind == "wout":
        C = q32.shape[2]
        grid = (2,)
        qs = pl.BlockSpec((None, SHARD_ROWS, C), lambda s, jc_ref: (s, jc_ref[0], 0))
        rs = pl.BlockSpec((3, None, SHARD_ROWS, C), lambda s, jc_ref: (0, s, 0, 0))
        os_ = pl.BlockSpec((None, SHARD_ROWS, C), lambda s, jc_ref: (2 * jc_ref[1] + s, 0, 0))
    else:
        C = q32.shape[2]
        grid = (2,)
        qs = pl.BlockSpec((None, HALF_ROWS // 2, C), lambda i, jc_ref: (jc_ref[0], i, 0))
        rs = pl.BlockSpec((3, HALF_ROWS // 2, C), lambda i, jc_ref: (0, i, 0))
        os_ = pl.BlockSpec((HALF_ROWS // 2, C), lambda i, jc_ref: (2 * jc_ref[1] + i, 0))
    if kind == "kv":
        full_shape = (D_MODEL, 2 * XW)
    elif kind == "wout":
        full_shape = (DEPTH, D_MODEL, D_MODEL)
    else:
        full_shape = (4, D_MODEL, q32.shape[2])
    return _ew_call(body, name, grid, jc, [q32, r], [qs, rs], jax.ShapeDtypeStruct(_shard_shape(full_shape, kind), f32), os_)


def _adamw(w, g, m, v, name):
    R, C = w.shape
    br = _rows_block(R, mult=SUB, cap=512)
    c1 =1.0 / (1.0 - ADAM_B1 ** ADAM_STEP)
    c2 = 1.0 / (1.0 - ADAM_B2 ** ADAM_STEP)

    def body(w_ref, g_ref, m_ref, v_ref, d_ref, nm_ref, nv_ref):
        g_ = g_ref[...]
        nm = ADAM_B1 * m_ref[...] + (1.0 - ADAM_B1) * g_
        nv = ADAM_B2 * v_ref[...] + (1.0 - ADAM_B2) * (g_ * g_)
        nm_ref[...] = nm
        nv_ref[...] = nv
        d_ref[...] = -ADAM_LR * ((nm * c1) / (jnp.sqrt(nv * c2) + ADAM_EPS) + ADAM_WD * w_ref[...])

    spec = pl.BlockSpec((br, C), lambda i: (i, 0))
    sh = jax.ShapeDtypeStruct((R, C), f32)
    return pl.pallas_call(body, name=name, grid=(R // br,), in_specs=[spec] * 4, out_specs=[spec] * 3,
                          out_shape=[sh, sh, sh], compiler_params=_params("parallel"))(w, g, m, v)


def _small_finish(dbacc, p_soft, dlb):
    def body(db_ref, p_ref, dlb_ref, dbs_ref, dlg_ref):
        lane = lax.broadcasted_iota(jnp.int32, (HD, HD), 1)
        acc = jnp.zeros((HD, HD), f32)
        for h in range(NH):
            acc = acc + jnp.where(lane == h, jnp.sum(db_ref[h], axis=-1, keepdims=True), 0.0)
        dbs_ref[...] = acc
        p = p_ref[...]
        p1 = p[1:2, :]
        rowi = lax.broadcasted_iota(jnp.int32, p.shape, 0)
        dlg_ref[...] = dlb_ref[0:1, :] * p1 * (jnp.where(rowi == 1, 1.0, 0.0) - p)

    vm = pl.BlockSpec(memory_space=pltpu.VMEM)
    return pl.pallas_call(body, name="small_finish", in_specs=[vm] * 3, out_specs=[vm] * 2,
                          out_shape=[jax.ShapeDtypeStruct((HD, HD), f32), jax.ShapeDtypeStruct(p_soft.shape, f32)])(dbacc, p_soft, dlb)


def _where_am_i():
    return lax.axis_index("x"), lax.axis_index("y"), lax.axis_index("c")


MAX_PIECES = 8


def _nchunks(rows, mult):
    for n in range(MAX_PIECES, 0, -1):
        if rows % (n * mult) == 0:
            return n
    return 1


def _leading_pieces(src, dst):
    n = src.shape[0]
    if len(src.shape) >= 3 and n <= MAX_PIECES:
        return [(src.at[s], dst.at[s]) for s in range(n)]
    return [(src, dst)]


def _ag_weights(shards, kinds, jshard):
    n = len(shards)

    def body(*refs):
        sh_refs, out_refs = refs[:n], refs[2 * n:3 * n]
        send_sems, recv_sems = refs[3 * n:]
        x, y, c = _where_am_i()
        j = 2 * x + y
        sib = (x, y, 1 - c)
        chips = [(1 - x, y), (x, 1 - y), (1 - x, 1 - y)]

        def cp(k, src, dst, to):
            return pltpu.make_async_remote_copy(src_ref=src, dst_ref=dst, send_sem=send_sems.at[k], recv_sem=recv_sems.at[k],
                                                device_id=to, device_id_type=MESH)

        started = []
        for a in range(n):
            for k, (cx, cy) in enumerate(chips):
                d = cp(6 * a + k, _half_of_shard(sh_refs[a], kinds[a], c), _half_of_shard(out_refs[a].at[j], kinds[a], c), (cx, cy, c))
                d.start()
                started.append(d)
        for a in range(n):
            for k, (cx, cy) in enumerate(chips):
                blk = _half_of_shard(out_refs[a].at[2 * cx + cy], kinds[a], c)
                cp(6 * a + k, blk, blk, (cx, cy, c)).wait_recv()
                d = cp(6 * a + 3 + k, blk, blk, sib)
                d.start()
                started.append(d)
        for a in range(n):
            for k, (cx, cy) in enumerate(chips):
                blk = _half_of_shard(out_refs[a].at[2 * cx + cy], kinds[a], 1 - c)
                cp(6 * a + 3 + k, blk, blk, sib).wait_recv()
        for d in started:
            d.wait_send()

    placed = [lax.dynamic_update_slice(jnp.zeros((4,) + s.shape, s.dtype), s[None], (jshard,) + (0,) * s.ndim) for s in shards]
    anyspec = pl.BlockSpec(memory_space=pl.ANY)
    return pl.pallas_call(body, name="all_gather_weights", in_specs=[anyspec] * (2 * n), out_specs=[anyspec] * n,
                          out_shape=[jax.ShapeDtypeStruct(p.shape, p.dtype) for p in placed],
                          input_output_aliases={n + a: a for a in range(n)},
                          scratch_shapes=[pltpu.SemaphoreType.DMA((6 * n,)), pltpu.SemaphoreType.DMA((6 * n,))],
                          compiler_params=pltpu.CompilerParams(has_side_effects=True))(*shards, *placed)


def _rs_swap(parts, kinds):
    n = len(parts)

    def body(*refs):
        p_refs, got_refs = refs[:n], refs[n:2 * n]
        send_sems, recv_sems = refs[2 * n:]
        x, y, c = _where_am_i()

        def cp(a, src, dst):
            return pltpu.make_async_remote_copy(src_ref=src, dst_ref=dst, send_sem=send_sems.at[a], recv_sem=recv_sems.at[a],
                                                device_id=(x, y, 1 - c), device_id_type=MESH)

        for a in range(n):
            for src, dst in _leading_pieces(_half_of_full(p_refs[a], kinds[a], 1 - c), got_refs[a]):
                cp(a, src, dst).start()
        for a in range(n):
            cp(a, got_refs[a], got_refs[a]).wait()

    anyspec = pl.BlockSpec(memory_space=pl.ANY)
    return pl.pallas_call(body, name="rs_swap_halves", in_specs=[anyspec] * n, out_specs=[anyspec] * n,
                          out_shape=[jax.ShapeDtypeStruct(_half_shape(p.shape, k), p.dtype) for p, k in zip(parts, kinds)],
                          scratch_shapes=[pltpu.SemaphoreType.DMA((n,)), pltpu.SemaphoreType.DMA((n,))],
                          compiler_params=pltpu.CompilerParams(has_side_effects=True))(*parts)


def _rs_owners(qbs, kinds, full_shapes):
    n = len(qbs)

    def body(*refs):
        q_refs, got_refs = refs[:n], refs[n:2 * n]
        send_sems, recv_sems = refs[2 * n:]
        x, y, c = _where_am_i()
        chips = [(1 - x, y), (x, 1 - y), (1 - x, 1 - y)]
        ds = []
        for a in range(n):
            for k, (cx, cy) in enumerate(chips):
                d = pltpu.make_async_remote_copy(src_ref=_shard_of_half(q_refs[a], kinds[a], 2 * cx + cy), dst_ref=got_refs[a].at[k],
                                                 send_sem=send_sems.at[3 * a + k], recv_sem=recv_sems.at[3 * a + k],
                                                 device_id=(cx, cy, c), device_id_type=MESH)
                d.start()
                ds.append(d)
        for d in ds:
            d.wait()

    anyspec = pl.BlockSpec(memory_space=pl.ANY)
    return pl.pallas_call(body, name="rs_to_owners", in_specs=[anyspec] * n, out_specs=[anyspec] * n,
                          out_shape=[jax.ShapeDtypeStruct((3,) + _shard_half_shape(fs, k), bf16) for fs, k in zip(full_shapes, kinds)],
                          scratch_shapes=[pltpu.SemaphoreType.DMA((3 * n,)), pltpu.SemaphoreType.DMA((3 * n,))],
                          compiler_params=pltpu.CompilerParams(has_side_effects=True))(*qbs)


def _rs_join(bufs, kinds):
    n = len(bufs)

    def body(*refs):
        out_refs = refs[n:2 * n]
        send_sems, recv_sems = refs[2 * n:]
        x, y, c = _where_am_i()

        def cp(a, h):
            blk = _half_of_shard(out_refs[a], kinds[a], h)
            return pltpu.make_async_remote_copy(src_ref=blk, dst_ref=blk, send_sem=send_sems.at[a], recv_sem=recv_sems.at[a],
                                                device_id=(x, y, 1 - c), device_id_type=MESH)

        for a in range(n):
            cp(a, c).start()
        for a in range(n):
            cp(a, c).wait_send()
            cp(a, 1 - c).wait_recv()

    anyspec = pl.BlockSpec(memory_space=pl.ANY)
    return pl.pallas_call(body, name="rs_join_halves", in_specs=[anyspec] * n, out_specs=[anyspec] * n,
                          out_shape=[jax.ShapeDtypeStruct(b.shape, b.dtype) for b in bufs],
                          input_output_aliases={a: a for a in range(n)},
                          scratch_shapes=[pltpu.SemaphoreType.DMA((n,)), pltpu.SemaphoreType.DMA((n,))],
                          compiler_params=pltpu.CompilerParams(has_side_effects=True))(*bufs)


def _all_reduce_small(g):
    R, C = g.shape
    H = R // 2
    NP = _nchunks(H, SUB)
    PR = H // NP

    def body(g_ref, out_ref, sib_ref, chip_ref, send_sems, recv_sems):
        x, y, c = _where_am_i()
        j = 2 * x + y
        sib = (x, y, 1 - c)
        chips = [(1 - x, y), (x, 1 - y), (1 - x, 1 - y)]
        rows = pl.ds(pl.multiple_of(c * H, SUB), H)

        def cp(k, src, dst, to):
            return pltpu.make_async_remote_copy(src_ref=src, dst_ref=dst, send_sem=send_sems.at[k], recv_sem=recv_sems.at[k],
                                                device_id=to, device_id_type=MESH)

        def pieces(k, src, dst, to):
            for q in range(NP):
                cp(k, src.at[pl.ds(q * PR, PR)], dst.at[pl.ds(q * PR, PR)], to).start()

        for half in range(2):
            pieces(0, g_ref.at[pl.ds(half * H, H)], sib_ref.at[pl.ds(half * H, H)], sib)
        cp(0, g_ref, sib_ref, sib).wait()
        chip_ref[j] = g_ref[rows, :] + sib_ref[rows, :]
        for k, (cx, cy) in enumerate(chips):
            pieces(1 + k, chip_ref.at[j], chip_ref.at[j], (cx, cy, c))
        for k, (cx, cy) in enumerate(chips):
            blk = chip_ref.at[2 * cx + cy]
            cp(1 + k, blk, blk, (cx, cy, c)).wait()
        out_ref[rows, :] = ((chip_ref[0] + chip_ref[1]) + chip_ref[2]) + chip_ref[3]
        other = out_ref.at[pl.ds(pl.multiple_of((1 - c) * H, SUB), H)]
        pieces(4, out_ref.at[rows], out_ref.at[rows], sib)
        cp(4, other, other, sib).wait()

    vm = pl.BlockSpec(memory_space=pltpu.VMEM)
    return pl.pallas_call(body, name="all_reduce_small", in_specs=[vm], out_specs=vm,
                          out_shape=jax.ShapeDtypeStruct((R, C), f32),
                          scratch_shapes=[pltpu.VMEM((R, C), f32), pltpu.VMEM((4, H, C), f32),
                                          pltpu.SemaphoreType.DMA((5,)), pltpu.SemaphoreType.DMA((5,))],
                          compiler_params=pltpu.CompilerParams(has_side_effects=True, vmem_limit_bytes=VMEM_LIMIT))(g)


def _pack_flat(arrs, rows_mult):
    flat = jnp.concatenate([a.reshape(-1) for a in arrs])
    n = flat.shape[0]
    tot = -(-n // (rows_mult * LANE)) * rows_mult * LANE
    return jnp.pad(flat, (0, tot - n)).reshape(-1, LANE)


def _unpack_flat(buf, shapes):
    flat = buf.reshape(-1)
    out, o = [], 0
    for s in shapes:
        n = math.prod(s)
        out.append(flat[o:o + n].reshape(s))
        o += n
    return out


_BIG = ("mem_kv_w", "w_out", "a_w_in", "b_w_in", "c_w_in", "d_w_in")
SMALL_ROWS_MULT = 256


def _row8(v):
    v = v.reshape(-1, v.shape[-1])
    return jnp.pad(v, ((0, SUB - v.shape[0]), (0, 0)))


def kernel(x, mem, mem_kv_w, ln_g, ln_b, w_out, hgrn_lb_logits, a_w_in, a_w_s, a_b_s, b_w_in, b_norm_g, c_w_in, c_w_pool, c_scale, d_w_in, d_conv_w, d_conv_b, d_w_gx, d_b_gx, d_w_ga, d_b_ga, d_a_param, loss_target, m_mem_kv_w, m_ln_g, m_ln_b, m_w_out, m_hgrn_lb_logits, m_a_w_in, m_a_w_s, m_a_b_s, m_b_w_in, m_b_norm_g, m_c_w_in, m_c_w_pool, m_c_scale, m_d_w_in, m_d_conv_w, m_d_conv_b, m_d_w_gx, m_d_b_gx, m_d_w_ga, m_d_b_ga, m_d_a_param, v_mem_kv_w, v_ln_g, v_ln_b, v_w_out, v_hgrn_lb_logits, v_a_w_in, v_a_w_s, v_a_b_s, v_b_w_in, v_b_norm_g, v_c_w_in, v_c_w_pool, v_c_scale, v_d_w_in, v_d_conv_w, v_d_conv_b, v_d_w_gx, v_d_b_gx, v_d_w_ga, v_d_b_ga, v_d_a_param):
    names = ["mem_kv_w", "ln_g", "ln_b", "w_out", "hgrn_lb_logits", "a_w_in", "a_w_s", "a_b_s", "b_w_in", "b_norm_g", "c_w_in",
             "c_w_pool", "c_scale", "d_w_in", "d_conv_w", "d_conv_b", "d_w_gx", "d_b_gx", "d_w_ga", "d_b_ga", "d_a_param"]
    w = dict(mem_kv_w=mem_kv_w, ln_g=ln_g, ln_b=ln_b, w_out=w_out, hgrn_lb_logits=hgrn_lb_logits, a_w_in=a_w_in, a_w_s=a_w_s,
             a_b_s=a_b_s, b_w_in=b_w_in, b_norm_g=b_norm_g, c_w_in=c_w_in, c_w_pool=c_w_pool, c_scale=c_scale, d_w_in=d_w_in,
             d_conv_w=d_conv_w, d_conv_b=d_conv_b, d_w_gx=d_w_gx, d_b_gx=d_b_gx, d_w_ga=d_w_ga, d_b_ga=d_b_ga, d_a_param=d_a_param)
    m = dict(zip(names, [m_mem_kv_w, m_ln_g, m_ln_b, m_w_out, m_hgrn_lb_logits, m_a_w_in, m_a_w_s, m_a_b_s, m_b_w_in, m_b_norm_g,
                         m_c_w_in, m_c_w_pool, m_c_scale, m_d_w_in, m_d_conv_w, m_d_conv_b, m_d_w_gx, m_d_b_gx, m_d_w_ga,
                         m_d_b_ga, m_d_a_param]))
    v = dict(zip(names, [v_mem_kv_w, v_ln_g, v_ln_b, v_w_out, v_hgrn_lb_logits, v_a_w_in, v_a_w_s, v_a_b_s, v_b_w_in, v_b_norm_g,
                         v_c_w_in, v_c_w_pool, v_c_scale, v_d_w_in, v_d_conv_w, v_d_conv_b, v_d_w_gx, v_d_b_gx, v_d_w_ga,
                         v_d_b_ga, v_d_a_param]))
    xi, yi = lax.axis_index("x"), lax.axis_index("y")
    jshard = 2 * xi + yi
    x2 = x[0]
    mem2 = mem[0]
    tgt2 = loss_target[0]

    kinds = ("kv", "wout", "win", "win", "win", "win")
    big2d = lambda d: [d["mem_kv_w"], d["w_out"]] + [d[n][0] for n in _BIG[2:]]
    gath = _ag_weights([a.astype(bf16) for a in big2d(w)], kinds, jshard)
    w_kv = gath[0].reshape(D_MODEL, 2 * XW)
    w_outs = [gath[1][:, l].reshape(D_MODEL, D_MODEL) for l in range(DEPTH)]
    w_outT = gath[1].transpose(1, 3, 0, 2).reshape(DEPTH, D_MODEL, D_MODEL)
    w_in = [g.transpose(1, 0, 2).reshape(D_MODEL, -1) for g in gath[2:]]
    w_inT = [g.transpose(0, 2, 1).reshape(-1, D_MODEL) for g in gath[2:]]

    def gather_small(shard):
        z = jnp.zeros((4, POOL_GROUP), f32)
        return lax.dynamic_update_slice(z, shard.reshape(1, POOL_GROUP), (jshard, 0))

    sm_sh = jnp.concatenate([gather_small(b_norm_g), gather_small(c_scale), gather_small(d_conv_b), gather_small(d_a_param)]
                            + [gather_small(d_conv_w[:, r]) for r in range(4)], axis=0)
    ci = lax.axis_index("c")
    sm_all = _all_reduce_small(_pack_flat([jnp.where(ci == 0, sm_sh, 0.0)], SUB * 2))
    sm = _unpack_flat(sm_all, [(8, 4 * POOL_GROUP)])[0]
    ng_full, scale_full, convb_full, ap_full = sm[0:1], sm[1:2], sm[2:3], sm[3:4]
    convw_full = sm[4:8]

    tril = jnp.tril(jnp.ones((HD, HD), bool))
    wtri = jnp.where(tril, a_w_s[0], 0.0)
    wbd = jnp.zeros((TOK, TOK), f32)
    for g in range(4):
        wbd = lax.dynamic_update_slice(wbd, c_w_pool[0, g], (g * POOL_GROUP, g * POOL_GROUP))
    kh, khT, vh, vhT, p_soft = _prep(mem2, w_kv, hgrn_lb_logits)
    prm = [
        dict(wtri=wtri.astype(bf16), wtriT=wtri.transpose(0, 2, 1).astype(bf16),
             bcolb=jnp.broadcast_to(a_b_s[0][:, :, None], (NH, HD, HD))),
        dict(lb=p_soft[1:2], ng=ng_full),
        dict(wbd=wbd.astype(bf16), wbdT=wbd.T.astype(bf16), scale=scale_full),
        dict(cw=_row8(convw_full), cb=convb_full, wgx=d_w_gx[0].astype(bf16), wgxT=d_w_gx[0].transpose(0, 2, 1).astype(bf16),
             bgx=d_b_gx.reshape(1, TOK), wga=d_w_ga[0].astype(bf16), wgaT=d_w_ga[0].transpose(0, 2, 1).astype(bf16),
             bga=d_b_ga.reshape(1, TOK), ap=ap_full),
    ]

    acts = []
    h = x2
    for l in range(DEPTH):
        outs = _fwd_layer(l, h, w_in[l], w_outs[l], ln_g[l:l + 1], ln_b[l:l + 1], khT, vh, prm[l],
                          tgt2 if l == DEPTH - 1 else None)
        nfix = 4 if l == DEPTH - 1 else 3
        acts.append(dict(xin=h, proj=outs[1], z=outs[2], saves=outs[nfix:]))
        if l == DEPTH - 1:
            loss_part = outs[3]
        h = outs[0]
    loss = lax.psum(0.5 / D_MODEL * jnp.sum(loss_part), ("x", "y", "c"))

    dh = h
    gw_in = [None] * DEPTH
    gw_out = None
    dln = [None] * DEPTH
    dks, dvs = [None] * DEPTH, [None] * DEPTH
    sgr = [None] * DEPTH
    for l in reversed(range(DEPTH)):
        a = acts[l]
        (dxin, dproj, mixedb, dyb, dln[l], dks[l], dvs[l]), sgr[l] = _bwd_layer(
            l, dh, a["z"], a["proj"], w_inT[l], w_outT[l], ln_g[l:l + 1], kh, khT, vh, vhT, prm[l], a["saves"])
        if _OFFS[l]["W"] // 4 % LANE:
            gw_in[l] = _tn_gemm(a["xin"], dproj, f"grad_w_in{l}", 1).reshape(D_MODEL, 4, -1).transpose(1, 0, 2)
        else:
            gw_in[l] = _tn_gemm_sharded(a["xin"], dproj, f"grad_w_in{l}")
        gw_out = _tn_gemm_slab(mixedb, dyb, gw_out, l, f"grad_w_out{l}")
        dh = dxin
    grad_x = dh[None]

    parts = [_kv_bwd(mem2, dks, dvs), gw_out] + gw_in
    jc = jnp.stack([jshard, ci]).astype(jnp.int32)
    gots = _rs_swap(parts, kinds)
    sums = [_add_sibling(p, g, k, jc, f"rs_add_sibling{a}") for a, (p, g, k) in enumerate(zip(parts, gots, kinds))]
    gots2 = _rs_owners([s[1] for s in sums], kinds, [p.shape for p in parts])
    gbig = _rs_join([_add_chips(s[0], r, k, jc, f"rs_add_chips{a}") for a, (s, r, k) in enumerate(zip(sums, gots2, kinds))], kinds)
    g_sh, d_sh, m_sh, v_sh = {}, {}, {}, {}
    for a, n in enumerate(_BIG):
        as2d = lambda t: t.reshape(-1, t.shape[-1])
        upd = _adamw(as2d(w[n]), as2d(gbig[a]), as2d(m[n]), as2d(v[n]), f"adamw_{n}")
        g_sh[n] = gbig[a].reshape(w[n].shape)
        d_sh[n], m_sh[n], v_sh[n] = (u.reshape(w[n].shape) for u in upd)

    dbs, dlogits = _small_finish(sgr[0]["dbacc"], p_soft, sgr[1]["dlb"])
    gs = {
        "ln_g": jnp.concatenate([dln[l][0:1] for l in range(DEPTH)], axis=0),
        "ln_b": jnp.concatenate([dln[l][1:2] for l in range(DEPTH)], axis=0),
        "hgrn_lb_logits": dlogits,
        "a_w_s": sgr[0]["dwtri"][None],
        "a_b_s": dbs[:, 0:NH].T[None],
        "b_norm_g": sgr[1]["dng"][0:1],
        "c_w_pool": jnp.stack([sgr[2]["dwbd"][g * POOL_GROUP:(g + 1) * POOL_GROUP, g * POOL_GROUP:(g + 1) * POOL_GROUP]
                               for g in range(4)])[None],
        "c_scale": sgr[2]["dscale"][0:1],
        "d_conv_w": sgr[3]["dcw"][0:4][None],
        "d_conv_b": sgr[3]["dvec"][3:4],
        "d_w_gx": sgr[3]["dwgx"][None],
        "d_b_gx": sgr[3]["dvec"][1:2].reshape(1, NH, HD),
        "d_w_ga": sgr[3]["dwga"][None],
        "d_b_ga": sgr[3]["dvec"][2:3].reshape(1, NH, HD),
        "d_a_param": sgr[3]["dvec"][0:1],
    }
    small = [n for n in names if n not in _BIG]
    full_shapes = [gs[n].shape for n in small]
    gsum = dict(zip(small, _unpack_flat(_all_reduce_small(_pack_flat([gs[n] for n in small], SMALL_ROWS_MULT)), full_shapes)))
    for n in ("b_norm_g", "c_scale", "d_conv_b", "d_a_param"):
        gsum[n] = lax.dynamic_slice(gsum[n], (0, jshard * POOL_GROUP), (1, POOL_GROUP))
    gsum["d_conv_w"] = lax.dynamic_slice(gsum["d_conv_w"], (0, 0, jshard * POOL_GROUP), (1, 4, POOL_GROUP))
    shapes = [w[n].shape for n in small]
    pk = lambda d: _pack_flat([d[n] for n in small], 2 * SMALL_ROWS_MULT)
    d_s, m_s, v_s = _adamw(pk(w), pk(gsum), pk(m), pk(v), "adamw_small")
    d_sm = dict(zip(small, _unpack_flat(d_s, shapes)))
    m_sm = dict(zip(small, _unpack_flat(m_s, shapes)))
    v_sm = dict(zip(small, _unpack_flat(v_s, shapes)))

    grads = {**gsum, **g_sh}
    deltas = {**d_sm, **d_sh}
    new_m = {**m_sm, **m_sh}
    new_v = {**v_sm, **v_sh}
    return (loss, grad_x, *[grads[n] for n in names], *[deltas[n] for n in names], *[new_m[n] for n in names],
            *[new_v[n] for n in names])
```

```python
import functools
import math

import jax
import jax.numpy as jnp
from jax import lax
from jax.experimental import pallas as pl
from jax.experimental.pallas import tpu as pltpu

f32 = jnp.float32
bf16 = jnp.bfloat16
MM = bf16

D_MODEL = 1024
TOK = 768
XW = 256
XHEADS = 4
XDIM = 64
HD = 128
NH = TOK // HD
CHUNK = 16
POOL_GROUP = 192
DEPTH = 4
ALPHA = (2 * DEPTH) ** 0.25
LN_EPS = 1e-5
RMS_EPS = 1e-6
LRU_C = 8.0
ADAM_LR, ADAM_B1, ADAM_B2, ADAM_EPS, ADAM_WD, ADAM_STEP = 0.001, 0.9, 0.999, 1e-08, 0.01, 10

_TS = (256, 128, 256, 256)
TK = 512
SUB = 8
LANE = 128
VMEM_LIMIT = 58 * 1024 * 1024

_OFFS = (
    dict(u=0, v=768, qx=1536, gate=1792, W=2816),
    dict(q=0, f=768, i=1536, qx=2304, gate=2560, W=3584),
    dict(p=0, qx=768, gate=1024, W=2048),
    dict(xb=0, qx=768, gate=1024, W=2048),
)
_PRM = (
    ("wtri", "wtriT", "bcolb"),
    ("lb", "ng"),
    ("wbd", "wbdT", "scale"),
    ("cw", "cb", "wgx", "wgxT", "bgx", "wga", "wgaT", "bga", "ap"),
)
MESH = pl.DeviceIdType.MESH


def _mm(a, b):
    return jnp.dot(a.astype(MM), b.astype(MM), preferred_element_type=f32)


def _mm_nt(a, b):
    return lax.dot_general(a.astype(MM), b.astype(MM), (((1,), (1,)), ((), ())), preferred_element_type=f32)


def _mm_tn(a, b):
    return lax.dot_general(a.astype(MM), b.astype(MM), (((0,), (0,)), ((), ())), preferred_element_type=f32)


def _mm_sel(sel, b):
    s = sel.astype(bf16)
    hi = b.astype(bf16)
    lo = (b - hi.astype(f32)).astype(bf16)
    return jnp.dot(s, hi, preferred_element_type=f32) + jnp.dot(s, lo, preferred_element_type=f32)


def _sig(x):
    return jax.nn.sigmoid(x)


_GC = math.sqrt(2.0 / math.pi)


def _gelu(x):
    t = jnp.tanh(_GC * (x + 0.044715 * x * x * x))
    return 0.5 * x * (1.0 + t), t


def _gelu_grad(x, t):
    return 0.5 * (1.0 + t) + 0.5 * x * (1.0 - t * t) * _GC * (1.0 + 3.0 * 0.044715 * x * x)


def _rowsum(x):
    return jnp.sum(x, axis=0, keepdims=True)


def _lmean(x):
    return jnp.mean(x, axis=-1, keepdims=True)


def _ln(z):
    mu = _lmean(z)
    zc = z - mu
    rstd = lax.rsqrt(_lmean(zc * zc) + LN_EPS)
    return zc * rstd, rstd


def _ln_bwd(dxh, xhat, rstd):
    return rstd * (dxh - _lmean(dxh) - xhat * _lmean(dxh * xhat))


def _hs(h):
    return slice(h * HD, (h + 1) * HD)


def _expm1(x):
    small = x * (1.0 + x * 0.5 * (1.0 + x * (1.0 / 3.0) * (1.0 + x * 0.25 * (1.0 + x * 0.2 * (1.0 + x * (1.0 / 6.0))))))
    return jnp.where(jnp.abs(x) < 0.25, small, jnp.exp(x) - 1.0)


def _softplus(x):
    e = jnp.exp(-jnp.abs(x))
    l1p = jnp.where(e < 1e-4, e - 0.5 * e * e, jnp.log(1.0 + e))
    return jnp.maximum(x, 0.0) + l1p


def _scan_fwd(a, b):
    n = a.shape[0]
    row = lax.broadcasted_iota(jnp.int32, a.shape, 0)
    d = 1
    while d < n:
        if d % SUB:
            m = row >= d
            b = jnp.where(m, a * pltpu.roll(b, d, 0) + b, b)
            a = jnp.where(m, a * pltpu.roll(a, d, 0), a)
        else:
            b = a * jnp.concatenate([jnp.zeros((d,) + b.shape[1:], f32), b[:n - d]], axis=0) + b
            a = a * jnp.concatenate([jnp.ones((d,) + a.shape[1:], f32), a[:n - d]], axis=0)
        d *= 2
    return a, b


def _scan_bwd(a, b):
    n = a.shape[0]
    row = lax.broadcasted_iota(jnp.int32, a.shape, 0)
    d = 1
    while d < n:
        if d % SUB:
            m = row < n - d
            b = jnp.where(m, a * pltpu.roll(b, n - d, 0) + b, b)
            a = jnp.where(m, a * pltpu.roll(a, n - d, 0), a)
        else:
            b = a * jnp.concatenate([b[d:], jnp.zeros((d,) + b.shape[1:], f32)], axis=0) + b
            a = a * jnp.concatenate([a[d:], jnp.ones((d,) + a.shape[1:], f32)], axis=0)
        d *= 2
    return a, b


def _chunk_mats(n):
    r = lax.broadcasted_iota(jnp.int32, (n, n), 0)
    c = lax.broadcasted_iota(jnp.int32, (n, n), 1)
    same = (r // CHUNK) == (c // CHUNK)
    return same, jnp.logical_and(same, c <= r)


def _pool_w(shape):
    lane = lax.broadcasted_iota(jnp.int32, shape, 1)
    return jnp.where(lane < POOL_GROUP, 2, jnp.where(lane < 2 * POOL_GROUP, 4, jnp.where(lane < 3 * POOL_GROUP, 8, 16)))


def _pool_pick(r1, r2, r3, r4):
    lane = lax.broadcasted_iota(jnp.int32, r1.shape, 1)
    return jnp.where(lane < POOL_GROUP, r1, jnp.where(lane < 2 * POOL_GROUP, r2, jnp.where(lane < 3 * POOL_GROUP, r3, r4)))


def _const_spec(a):
    nd = a.ndim
    return pl.BlockSpec(a.shape, lambda i, _nd=nd: (0,) * _nd, pipeline_mode=pl.Buffered(1))


def _acc_spec(shape):
    nd = len(shape)
    return pl.BlockSpec(shape, lambda i, _nd=nd: (0,) * _nd)


def _params(sem="arbitrary"):
    return pltpu.CompilerParams(dimension_semantics=(sem,), vmem_limit_bytes=VMEM_LIMIT)


def _xattn_fwd(qx, khT_ref, vh_ref):
    xo = jnp.zeros((qx.shape[0], XW), f32)
    ps = []
    for h in range(XHEADS):
        s = _mm(qx, khT_ref[h]) * (XDIM ** -0.5)
        e = jnp.exp(s - jnp.max(s, axis=-1, keepdims=True))
        p = e / jnp.sum(e, axis=-1, keepdims=True)
        xo = xo + _mm(p, vh_ref[h])
        ps.append(p)
    return xo, ps


def _hgrn_parallel(q_raw, fl, lb):
    n = q_raw.shape[0]
    same, tri = _chunk_mats(n)
    sq = _sig(q_raw)
    qf = q_raw * sq
    sgm = _sig(fl)
    f = lb + (1.0 - lb) * sgm
    logf = jnp.log(f)
    k = 1.0 - f
    g = _mm_sel(tri, logf)
    gl = _mm_sel(same, logf)
    eg = jnp.exp(g)
    eng = jnp.exp(-g)
    ee = jnp.exp(gl - g)
    return dict(sq=sq, qf=qf, sgm=sgm, f=f, k=k, eg=eg, eng=eng, ee=ee, q_dec=qf * eg, k_inv=k * eng, k_end=k * ee,
                a=jnp.exp(gl))


def _hgrn_intra(q_dec, k_inv, v):
    n = q_dec.shape[0]
    _, tri = _chunk_mats(HD)
    outs = []
    for h in range(NH):
        blks = []
        for b in range(n // HD):
            rs = slice(b * HD, (b + 1) * HD)
            sc = jnp.where(tri, _mm_nt(q_dec[rs, _hs(h)], k_inv[rs, _hs(h)]), 0.0)
            blks.append(_mm(sc, v[rs, _hs(h)]))
        outs.append(jnp.concatenate(blks, axis=0))
    return jnp.concatenate(outs, axis=-1)


def _cs(c):
    return slice(c * CHUNK, (c + 1) * CHUNK)


def _hgrn_inter_fwd(qdec_s, kend_s, v_s, a_s, oint_s, st_ref, states_s, u_s):
    n = qdec_s.shape[0] // CHUNK
    for c in range(n):
        for h in range(NH):
            u_s[c, h] = _mm_tn(v_s[_cs(c), _hs(h)], kend_s[_cs(c), _hs(h)])
    for h in range(NH):
        st = st_ref[h]
        for c in range(n):
            states_s[c, h] = st
            st = st * a_s[c * CHUNK:c * CHUNK + 1, _hs(h)] + u_s[c, h]
        st_ref[h] = st
    if oint_s is None:
        return
    for c in range(n):
        for h in range(NH):
            oint_s[_cs(c), _hs(h)] = _mm_nt(qdec_s[_cs(c), _hs(h)], states_s[c, h])


def _rms(o):
    outs, rs = [], []
    for h in range(NH):
        oh = o[:, _hs(h)]
        r = lax.rsqrt(_lmean(oh * oh) + RMS_EPS)
        outs.append(oh * r)
        rs.append(r)
    return jnp.concatenate(outs, axis=-1), rs


def _gmlp_core(u_raw, v_raw, wtri_ref, bcolb_ref):
    gu, tu = _gelu(u_raw)
    gv, tv = _gelu(v_raw)
    vns, rstds, mixeds = [], [], []
    for h in range(NH):
        vn, rstd = _ln(gv[:, _hs(h)])
        blks = []
        for n in range(u_raw.shape[0] // HD):
            blks.append(_mm(wtri_ref[h], vn[n * HD:(n + 1) * HD]) + bcolb_ref[h])
        vns.append(vn)
        rstds.append(rstd)
        mixeds.append(jnp.concatenate(blks, axis=0))
    mixed = jnp.concatenate(mixeds, axis=-1)
    return gu, tu, tv, vns, rstds, mixed


def _pool_core(p, carry, row0, wbd_ref):
    ext = jnp.concatenate([carry, p], axis=0)
    r1 = ext + pltpu.roll(ext, 1, 0)
    r2 = r1 + pltpu.roll(r1, 2, 0)
    r3 = r2 + pltpu.roll(r2, 4, 0)
    r4 = r3 + pltpu.roll(r3, 8, 0)
    sel = _pool_pick(r1, r2, r3, r4)[2 * SUB:]
    grow = row0 + lax.broadcasted_iota(jnp.int32, p.shape, 0)
    inv_cnt = 1.0 / jnp.minimum(grow + 1, _pool_w(p.shape)).astype(f32)
    diff = sel * inv_cnt - p
    return diff, inv_cnt, _mm(diff, wbd_ref[...])


def _lru_core(xb, ccar, row0, p):
    ext = jnp.concatenate([ccar, xb], axis=0)
    cw = p["cw"]
    x1, x2, x3 = pltpu.roll(ext, 1, 0)[SUB:], pltpu.roll(ext, 2, 0)[SUB:], pltpu.roll(ext, 3, 0)[SUB:]
    xc = cw[3:4, :] * xb + cw[2:3, :] * x1 + cw[1:2, :] * x2 + cw[0:1, :] * x3 + p["cb"][...]
    gxs, gas = [], []
    for h in range(NH):
        gxs.append(_mm(xc[:, _hs(h)], p["wgx"][h]))
        gas.append(_mm(xc[:, _hs(h)], p["wga"][h]))
    gx = _sig(jnp.concatenate(gxs, axis=-1) + p["bgx"][...])
    ga = _sig(jnp.concatenate(gas, axis=-1) + p["bga"][...])
    sp = _softplus(-p["ap"][...])
    la = -LRU_C * ga * sp
    a = jnp.exp(la)
    grow = row0 + lax.broadcasted_iota(jnp.int32, xb.shape, 0)
    first = grow == 0
    mult = jnp.where(first, 1.0, jnp.sqrt(-_expm1(2.0 * la)))
    bt = mult * gx * xc
    return dict(x1=x1, x2=x2, x3=x3, xc=xc, gx=gx, ga=ga, sp=sp, a=a, mult=mult, bt=bt, first=first)


def _fwd_layer(kind, xin, w_in, w_out, lng, lnb, khT, vh, prm, tgt):
    S = xin.shape[0]
    TS = _TS[kind]
    nt = S // TS
    off = _OFFS[kind]
    W = off["W"]
    last = tgt is not None
    pnames = _PRM[kind]
    pvals = [prm[n] for n in pnames]

    def body(*refs):
        it = iter(refs)
        xin_ref, win_ref, wout_ref, lng_ref, lnb_ref, khT_ref, vh_ref = (next(it) for _ in range(7))
        p = {n: next(it) for n in pnames}
        tgt_ref = next(it) if last else None
        xout_ref, proj_ref, z_ref = next(it), next(it), next(it)
        loss_ref = next(it) if last else None
        rest = list(it)
        i = pl.program_id(0)
        x = xin_ref[...]
        proj_ref[...] = _mm(x, win_ref[...])

        if kind == 0:
            gu, _, _, _, _, mixed = _gmlp_core(proj_ref[:, 0:TOK], proj_ref[:, TOK:2 * TOK], p["wtri"], p["bcolb"])
            tok = gu * mixed
        elif kind == 1:
            st_save, o_save, st_ref, states_s, u_s, qdec_s, kend_s, v_s, a_s, oint_s = rest

            @pl.when(i == 0)
            def _():
                st_ref[...] = jnp.zeros_like(st_ref)

            st_save[0] = st_ref[...]
            v = proj_ref[:, 2 * TOK:3 * TOK]
            hp = _hgrn_parallel(proj_ref[:, 0:TOK], proj_ref[:, TOK:2 * TOK], p["lb"][...])
            qdec_s[...] = hp["q_dec"]
            kend_s[...] = hp["k_end"]
            v_s[...] = v
            a_s[...] = hp["a"]
            o_intra = _hgrn_intra(hp["q_dec"], hp["k_inv"], v)
            _hgrn_inter_fwd(qdec_s, kend_s, v_s, a_s, oint_s, st_ref, states_s, u_s)
            o = o_intra + oint_s[...]
            o_save[0] = o
            on, _ = _rms(o)
            tok = on * p["ng"][...]
        elif kind == 2:
            pc_save, pcar = rest

            @pl.when(i == 0)
            def _():
                pcar[...] = jnp.zeros_like(pcar)

            pc_save[0] = pcar[...]
            pp = proj_ref[:, 0:TOK]
            _, _, y = _pool_core(pp, pcar[...], i * TS, p["wbd"])
            pcar[...] = pp[TS - 2 * SUB:, :]
            tok = y * p["scale"][...]
        else:
            cc_save, hc_save, h_save, ccar, hcar = rest

            @pl.when(i == 0)
            def _():
                ccar[...] = jnp.zeros_like(ccar)
                hcar[...] = jnp.zeros_like(hcar)

            cc_save[0] = ccar[...]
            hc_save[0] = hcar[...]
            xb = proj_ref[:, 0:TOK]
            lc = _lru_core(xb, ccar[...], i * TS, p)
            P, B = _scan_fwd(lc["a"], lc["bt"])
            tok = P * hcar[SUB - 1:SUB, :] + B
            h_save[0] = tok
            ccar[...] = xb[TS - SUB:, :]
            hcar[...] = tok[TS - SUB:, :]

        xo, _ = _xattn_fwd(proj_ref[:, off["qx"]:off["qx"] + XW], khT_ref, vh_ref)
        gate = proj_ref[:, off["gate"]:off["gate"] + D_MODEL]
        mixed = jnp.concatenate([tok, xo], axis=-1) * (gate * _sig(gate))
        z = ALPHA * x + _mm(mixed, wout_ref[...])
        z_ref[...] = z
        xhat, _ = _ln(z)
        xout = xhat * lng_ref[...] + lnb_ref[...]
        if last:
            e = xout - tgt_ref[...]
            xout_ref[...] = e * (1.0 / D_MODEL)
            es = _rowsum(e * e)
            tot = es[:, 0:LANE]
            for j in range(1, D_MODEL // LANE):
                tot = tot + es[:, j * LANE:(j + 1) * LANE]

            @pl.when(i == 0)
            def _():
                loss_ref[...] = jnp.zeros_like(loss_ref)

            loss_ref[0:1, :] += tot
        else:
            xout_ref[...] = xout

    tile = lambda w: pl.BlockSpec((TS, w), lambda i: (i, 0))
    in_arrays = [xin, w_in, w_out, lng, lnb, khT, vh] + pvals + ([tgt] if last else [])
    in_specs = [tile(D_MODEL)] + [_const_spec(a) for a in in_arrays[1:7 + len(pvals)]] + ([tile(D_MODEL)] if last else [])
    out_shape = [jax.ShapeDtypeStruct((S, D_MODEL), f32), jax.ShapeDtypeStruct((S, W), f32), jax.ShapeDtypeStruct((S, D_MODEL), f32)]
    out_specs = [tile(D_MODEL), tile(W), tile(D_MODEL)]
    if last:
        out_shape.append(jax.ShapeDtypeStruct((SUB, LANE), f32))
        out_specs.append(_acc_spec((SUB, LANE)))
    scratch = []
    save = lambda *s: (jax.ShapeDtypeStruct((nt,) + s, f32), pl.BlockSpec((1,) + s, lambda i, _n=len(s): (i,) + (0,) * _n))
    if kind == 1:
        saved = [save(NH, HD, HD), save(TS, TOK)]
        scratch = ([pltpu.VMEM((NH, HD, HD), f32)] + [pltpu.VMEM((TS // CHUNK, NH, HD, HD), f32)] * 2
                   + [pltpu.VMEM((TS, TOK), f32)] * 5)
    elif kind == 2:
        saved = [save(2 * SUB, TOK)]
        scratch = [pltpu.VMEM((2 * SUB, TOK), f32)]
    elif kind == 3:
        saved = [save(SUB, TOK), save(SUB, TOK), save(TS, TOK)]
        scratch = [pltpu.VMEM((SUB, TOK), f32)] * 2
    else:
        saved = []
    for sh, sp in saved:
        out_shape.append(sh)
        out_specs.append(sp)
    return pl.pallas_call(body, name=f"fwd_layer{kind}", grid=(nt,), in_specs=in_specs, out_specs=out_specs,
                          out_shape=out_shape, scratch_shapes=scratch, compiler_params=_params())(*in_arrays)


def _small_grad_shapes(kind):
    if kind == 0:
        return dict(dwtri=(NH, HD, HD), dbacc=(NH, HD, HD))
    if kind == 1:
        return dict(dlb=(SUB, TOK), dng=(SUB, TOK))
    if kind == 2:
        return dict(dwbd=(TOK, TOK), dscale=(SUB, TOK))
    return dict(dcw=(SUB, TOK), dvec=(SUB, TOK), dwgx=(NH, HD, HD), dwga=(NH, HD, HD))


def _bwd_layer(kind, dxout, z, proj, w_inT, w_outT, lng, kh, khT, vh, vhT, prm, saves):
    S = dxout.shape[0]
    TS = _TS[kind]
    nt = S // TS
    off = _OFFS[kind]
    W = off["W"]
    pnames = _PRM[kind]
    pvals = [prm[n] for n in pnames]
    sg_shapes = _small_grad_shapes(kind)
    sg_names = list(sg_shapes)
    n_saves = len(saves)

    def body(*refs):
        it = iter(refs)
        dxo_ref, z_ref, proj_ref, winT_ref, woutT_ref, lng_ref, kh_ref, khT_ref, vh_ref, vhT_ref = (next(it) for _ in range(10))
        p = {n: next(it) for n in pnames}
        sv = [next(it) for _ in range(n_saves)]
        dxin_ref, dproj_ref, mixed_ref, dy_ref, dln_ref, dk_ref, dv_ref = (next(it) for _ in range(7))
        sg = {n: next(it) for n in sg_names}
        rest = list(it)
        step = pl.program_id(0)
        i = nt - 1 - step

        @pl.when(step == 0)
        def _():
            dln_ref[...] = jnp.zeros_like(dln_ref)
            dk_ref[...] = jnp.zeros_like(dk_ref)
            dv_ref[...] = jnp.zeros_like(dv_ref)
            for n in sg_names:
                sg[n][...] = jnp.zeros_like(sg[n])

        dxo = dxo_ref[...]
        xhat, rstd = _ln(z_ref[...])
        dln_ref[0:1, :] += _rowsum(dxo * xhat)
        dln_ref[1:2, :] += _rowsum(dxo)
        dz = _ln_bwd(dxo * lng_ref[...], xhat, rstd)
        dyb = dz.astype(bf16)
        dy_ref[...] = dyb
        dmixed = _mm(dyb, woutT_ref[...])

        aux = {}
        if kind == 0:
            u_raw, v_raw = proj_ref[:, 0:TOK], proj_ref[:, TOK:2 * TOK]
            gu, tu, tv, vns, rstds, mx = _gmlp_core(u_raw, v_raw, p["wtri"], p["bcolb"])
            tok = gu * mx
        elif kind == 1:
            st_save, o_save = sv
            (dst_ref, fst_ref, states_s, dsts_s, u_s, qdec_s, kend_s, v_s, a_s, do_s, dqdec_s, dkend_s, dv_s,
             dgl_s) = rest

            @pl.when(step == 0)
            def _():
                dst_ref[...] = jnp.zeros_like(dst_ref)

            fst_ref[...] = st_save[0]
            v = proj_ref[:, 2 * TOK:3 * TOK]
            hp = _hgrn_parallel(proj_ref[:, 0:TOK], proj_ref[:, TOK:2 * TOK], p["lb"][...])
            qdec_s[...] = hp["q_dec"]
            kend_s[...] = hp["k_end"]
            v_s[...] = v
            a_s[...] = hp["a"]
            _hgrn_inter_fwd(qdec_s, kend_s, v_s, a_s, None, fst_ref, states_s, u_s)
            o = o_save[0]
            on, rs = _rms(o)
            tok = on * p["ng"][...]
            aux = dict(hp=hp, v=v, o=o, on=on, rs=rs)
        elif kind == 2:
            pc_save, = sv
            dpcar, = rest
            pp = proj_ref[:, 0:TOK]
            diff, inv_cnt, y = _pool_core(pp, pc_save[0], i * TS, p["wbd"])
            tok = y * p["scale"][...]
        else:
            cc_save, hc_save, h_save = sv
            dccar, gcar = rest
            xb = proj_ref[:, 0:TOK]
            lc = _lru_core(xb, cc_save[0], i * TS, p)
            hin = hc_save[0, SUB - 1:SUB, :]
            tok = h_save[0]

        xo, ps = _xattn_fwd(proj_ref[:, off["qx"]:off["qx"] + XW], khT_ref, vh_ref)
        gate = proj_ref[:, off["gate"]:off["gate"] + D_MODEL]
        sgm = _sig(gate)
        sgate = gate * sgm
        cat = jnp.concatenate([tok, xo], axis=-1)
        mixed_ref[...] = (cat * sgate).astype(bf16)
        dcat = dmixed * sgate
        dproj_ref[:, off["gate"]:off["gate"] + D_MODEL] = (dmixed * cat * (sgm * (1.0 + gate * (1.0 - sgm)))).astype(bf16)
        dtok = dcat[:, 0:TOK]
        dxo_att = dcat[:, TOK:]

        qx = proj_ref[:, off["qx"]:off["qx"] + XW]
        dqx = jnp.zeros((TS, XW), f32)
        for h in range(XHEADS):
            dp = _mm(dxo_att, vhT_ref[h])
            ds = ps[h] * (dp - jnp.sum(dp * ps[h], axis=-1, keepdims=True)) * (XDIM ** -0.5)
            dqx = dqx + _mm(ds, kh_ref[h])
            dk_ref[h] += _mm_tn(ds, qx)
            dv_ref[h] += _mm_tn(ps[h], dxo_att)
        dproj_ref[:, off["qx"]:off["qx"] + XW] = dqx.astype(bf16)

        if kind == 0:
            tril = lax.broadcasted_iota(jnp.int32, (HD, HD), 1) <= lax.broadcasted_iota(jnp.int32, (HD, HD), 0)
            dgu = dtok * mx
            dmx = dtok * gu
            dgvs = []
            for h in range(NH):
                dmh = dmx[:, _hs(h)]
                blks = []
                for n in range(TS // HD):
                    rs_ = slice(n * HD, (n + 1) * HD)
                    blks.append(_mm(p["wtriT"][h], dmh[rs_]))
                    sg["dwtri"][h] += jnp.where(tril, _mm_nt(dmh[rs_], vns[h][rs_]), 0.0)
                    sg["dbacc"][h] += dmh[rs_]
                dgvs.append(_ln_bwd(jnp.concatenate(blks, axis=0), vns[h], rstds[h]))
            dgv = jnp.concatenate(dgvs, axis=-1)
            dproj_ref[:, 0:TOK] = (dgu * _gelu_grad(u_raw, tu)).astype(bf16)
            dproj_ref[:, TOK:2 * TOK] = (dgv * _gelu_grad(v_raw, tv)).astype(bf16)
        elif kind == 1:
            hp, v, o, on, rs = aux["hp"], aux["v"], aux["o"], aux["on"], aux["rs"]
            ng = p["ng"][...]
            sg["dng"][0:1, :] += _rowsum(dtok * on)
            dn = dtok * ng
            dos = []
            for h in range(NH):
                oh, r = o[:, _hs(h)], rs[h]
                dos.append(r * (dn[:, _hs(h)] - oh * (r * r) * _lmean(dn[:, _hs(h)] * oh)))
            do = jnp.concatenate(dos, axis=-1)
            do_s[...] = do
            _, tri = _chunk_mats(HD)
            dqd, dki, dvi = [], [], []
            for h in range(NH):
                bq, bk, bv = [], [], []
                for b in range(TS // HD):
                    rs_ = slice(b * HD, (b + 1) * HD)
                    qd, ki = hp["q_dec"][rs_, _hs(h)], hp["k_inv"][rs_, _hs(h)]
                    sc = jnp.where(tri, _mm_nt(qd, ki), 0.0)
                    dsc = jnp.where(tri, _mm_nt(do[rs_, _hs(h)], v[rs_, _hs(h)]), 0.0)
                    bv.append(_mm_tn(sc, do[rs_, _hs(h)]))
                    bq.append(_mm(dsc, ki))
                    bk.append(_mm_tn(dsc, qd))
                dqd.append(jnp.concatenate(bq, axis=0))
                dki.append(jnp.concatenate(bk, axis=0))
                dvi.append(jnp.concatenate(bv, axis=0))
            dqdec_s[...] = jnp.concatenate(dqd, axis=-1)
            dk_inv = jnp.concatenate(dki, axis=-1)
            dv_s[...] = jnp.concatenate(dvi, axis=-1)
            row16 = lax.broadcasted_iota(jnp.int32, (CHUNK, HD), 0)

            nch = TS // CHUNK
            for c in range(nch):
                for h in range(NH):
                    u_s[c, h] = _mm_tn(do_s[_cs(c), _hs(h)], qdec_s[_cs(c), _hs(h)])
            for h in range(NH):
                dst = dst_ref[h]
                for c in reversed(range(nch)):
                    dsts_s[c, h] = dst
                    dst = dst * a_s[c * CHUNK:c * CHUNK + 1, _hs(h)] + u_s[c, h]
                dst_ref[h] = dst
            for c in range(nch):
                for h in range(NH):
                    stp = states_s[c, h]
                    dst = dsts_s[c, h]
                    dqdec_s[_cs(c), _hs(h)] += _mm(do_s[_cs(c), _hs(h)], stp)
                    dkend_s[_cs(c), _hs(h)] = _mm(v_s[_cs(c), _hs(h)], dst)
                    dv_s[_cs(c), _hs(h)] += _mm_nt(kend_s[_cs(c), _hs(h)], dst)
                    da = jnp.sum(dst * stp, axis=0, keepdims=True) * a_s[c * CHUNK:c * CHUNK + 1, _hs(h)]
                    dgl_s[_cs(c), _hs(h)] = jnp.where(row16 == 0, jnp.broadcast_to(da, (CHUNK, HD)), 0.0)
            dq_dec = dqdec_s[...]
            dk_end = dkend_s[...]
            same, _ = _chunk_mats(TS)
            triT = jnp.logical_and(same, lax.broadcasted_iota(jnp.int32, (TS, TS), 1) >= lax.broadcasted_iota(jnp.int32, (TS, TS), 0))
            dg = dq_dec * hp["q_dec"] - dk_inv * hp["k_inv"] - dk_end * hp["k_end"]
            dk = dk_inv * hp["eng"] + dk_end * hp["ee"]
            dglr = dk_end * hp["k_end"] + dgl_s[...]
            dlogf = _mm_sel(triT, dg) + _mm_sel(same, dglr)
            df = dlogf / hp["f"] - dk
            lb = p["lb"][...]
            sg["dlb"][0:1, :] += _rowsum(df * (1.0 - hp["sgm"]))
            q_raw = proj_ref[:, 0:TOK]
            dproj_ref[:, 0:TOK] = (dq_dec * hp["eg"] * (hp["sq"] * (1.0 + q_raw * (1.0 - hp["sq"])))).astype(bf16)
            dproj_ref[:, TOK:2 * TOK] = (df * (1.0 - lb) * hp["sgm"] * (1.0 - hp["sgm"])).astype(bf16)
            dproj_ref[:, 2 * TOK:3 * TOK] = dv_s[...].astype(bf16)
        elif kind == 2:
            @pl.when(step == 0)
            def _():
                dpcar[...] = jnp.zeros_like(dpcar)

            sg["dscale"][0:1, :] += _rowsum(dtok * y)
            dyp = dtok * p["scale"][...]
            sg["dwbd"][...] += _mm_tn(diff, dyp)
            ddiff = _mm(dyp, p["wbdT"][...])
            q = ddiff * inv_cnt
            ext = jnp.concatenate([q, dpcar[...]], axis=0)
            n = TS + 2 * SUB
            r1 = ext + pltpu.roll(ext, n - 1, 0)
            r2 = r1 + pltpu.roll(r1, n - 2, 0)
            r3 = r2 + pltpu.roll(r2, n - 4, 0)
            r4 = r3 + pltpu.roll(r3, n - 8, 0)
            dproj_ref[:, 0:TOK] = (_pool_pick(r1, r2, r3, r4)[:TS] - ddiff).astype(bf16)
            dpcar[...] = q[0:2 * SUB, :]
        else:
            @pl.when(step == 0)
            def _():
                dccar[...] = jnp.zeros_like(dccar)
                gcar[...] = jnp.zeros_like(gcar)

            a, mult, gx, ga, xc = lc["a"], lc["mult"], lc["gx"], lc["ga"], lc["xc"]
            row = lax.broadcasted_iota(jnp.int32, (TS, TOK), 0)
            an = jnp.where(row == TS - 1, 1.0, pltpu.roll(a, TS - 1, 0))
            Pb, Bb = _scan_bwd(an, dtok)
            lam = Pb * gcar[0:1, :] + Bb
            gcar[...] = (a * lam)[0:SUB, :]
            hprev = jnp.where(row == 0, jnp.broadcast_to(hin, (TS, TOK)), pltpu.roll(tok, 1, 0))
            dmult = lam * gx * xc
            dgx = lam * mult * xc
            dxc = lam * mult * gx
            dla = lam * hprev * a - jnp.where(lc["first"], 0.0, dmult * a * a / mult)
            sp = lc["sp"]
            dga = -LRU_C * sp * dla
            dsp = _rowsum(-LRU_C * ga * dla)
            sg["dvec"][0:1, :] += dsp * (-_sig(-p["ap"][...]))
            dpx = dgx * gx * (1.0 - gx)
            dpa = dga * ga * (1.0 - ga)
            sg["dvec"][1:2, :] += _rowsum(dpx)
            sg["dvec"][2:3, :] += _rowsum(dpa)
            dxcs = []
            for h in range(NH):
                dxcs.append(_mm(dpx[:, _hs(h)], p["wgxT"][h]) + _mm(dpa[:, _hs(h)], p["wgaT"][h]))
                sg["dwgx"][h] += _mm_tn(xc[:, _hs(h)], dpx[:, _hs(h)])
                sg["dwga"][h] += _mm_tn(xc[:, _hs(h)], dpa[:, _hs(h)])
            dxc = dxc + jnp.concatenate(dxcs, axis=-1)
            sg["dvec"][3:4, :] += _rowsum(dxc)
            sg["dcw"][3:4, :] += _rowsum(dxc * xb)
            sg["dcw"][2:3, :] += _rowsum(dxc * lc["x1"])
            sg["dcw"][1:2, :] += _rowsum(dxc * lc["x2"])
            sg["dcw"][0:1, :] += _rowsum(dxc * lc["x3"])
            ext = jnp.concatenate([dxc, dccar[...]], axis=0)
            n = TS + SUB
            cw = p["cw"]
            dproj_ref[:, 0:TOK] = (cw[3:4, :] * dxc + cw[2:3, :] * pltpu.roll(ext, n - 1, 0)[:TS]
                                   + cw[1:2, :] * pltpu.roll(ext, n - 2, 0)[:TS]
                                   + cw[0:1, :] * pltpu.roll(ext, n - 3, 0)[:TS]).astype(bf16)
            dccar[...] = dxc[0:SUB, :]

        dxin_ref[...] = ALPHA * dz + _mm(dproj_ref[...], winT_ref[...])

    rtile = lambda w: pl.BlockSpec((TS, w), lambda s: (nt - 1 - s, 0))
    consts = [w_inT, w_outT, lng, kh, khT, vh, vhT] + pvals
    in_arrays = [dxout, z, proj] + consts + list(saves)
    in_specs = [rtile(D_MODEL), rtile(D_MODEL), rtile(W)] + [_const_spec(a) for a in consts]
    for a in saves:
        in_specs.append(pl.BlockSpec((1,) + a.shape[1:], lambda s, _n=a.ndim - 1: (nt - 1 - s,) + (0,) * _n))
    out_shape = [jax.ShapeDtypeStruct((S, D_MODEL), f32), jax.ShapeDtypeStruct((S, W), bf16),
                 jax.ShapeDtypeStruct((S, D_MODEL), bf16), jax.ShapeDtypeStruct((S, D_MODEL), bf16),
                 jax.ShapeDtypeStruct((SUB, D_MODEL), f32), jax.ShapeDtypeStruct((XHEADS, XW, XW), f32),
                 jax.ShapeDtypeStruct((XHEADS, XW, XW), f32)]
    out_specs = [rtile(D_MODEL), rtile(W), rtile(D_MODEL), rtile(D_MODEL), _acc_spec((SUB, D_MODEL)),
                 _acc_spec((XHEADS, XW, XW)), _acc_spec((XHEADS, XW, XW))]
    for n in sg_names:
        out_shape.append(jax.ShapeDtypeStruct(sg_shapes[n], f32))
        out_specs.append(_acc_spec(sg_shapes[n]))
    if kind == 1:
        scratch = ([pltpu.VMEM((NH, HD, HD), f32)] * 2 + [pltpu.VMEM((TS // CHUNK, NH, HD, HD), f32)] * 3
                   + [pltpu.VMEM((TS, TOK), f32)] * 9)
    elif kind == 2:
        scratch = [pltpu.VMEM((2 * SUB, TOK), f32)]
    elif kind == 3:
        scratch = [pltpu.VMEM((SUB, TOK), f32)] * 2
    else:
        scratch = []
    outs = pl.pallas_call(body, name=f"bwd_layer{kind}", grid=(nt,), in_specs=in_specs, out_specs=out_specs,
                          out_shape=out_shape, scratch_shapes=scratch, compiler_params=_params())(*in_arrays)
    return outs[:7], dict(zip(sg_names, outs[7:]))


def _prep(mem, w_kv, logits):
    def body(mem_ref, w_ref, lg_ref, kh_ref, khT_ref, vh_ref, vhT_ref, p_ref):
        kv = _mm(mem_ref[...], w_ref[...])
        k, v = kv[:, 0:XW], kv[:, XW:]
        kT, vT = k.T, v.T
        col = lax.broadcasted_iota(jnp.int32, (XW, XW), 1) // XDIM
        row = lax.broadcasted_iota(jnp.int32, (XW, XW), 0) // XDIM
        for h in range(XHEADS):
            kh_ref[h] = jnp.where(col == h, k, 0.0).astype(bf16)
            vh_ref[h] = jnp.where(col == h, v, 0.0).astype(bf16)
            khT_ref[h] = jnp.where(row == h, kT, 0.0).astype(bf16)
            vhT_ref[h] = jnp.where(row == h, vT, 0.0).astype(bf16)
        lg = lg_ref[...]
        e = jnp.exp(lg - jnp.max(lg, axis=0, keepdims=True))
        p_ref[...] = e / jnp.sum(e, axis=0, keepdims=True)

    vm = pl.BlockSpec(memory_space=pltpu.VMEM)
    hs = jax.ShapeDtypeStruct((XHEADS, XW, XW), bf16)
    return pl.pallas_call(body, name="prep_memory", in_specs=[vm] * 3, out_specs=[vm] * 5,
                          out_shape=[hs, hs, hs, hs, jax.ShapeDtypeStruct(logits.shape, f32)])(mem, w_kv, logits)


def _kv_bwd(mem, dks, dvs):
    def body(mem_ref, *refs):
        out_ref = refs[-1]
        col = lax.broadcasted_iota(jnp.int32, (XW, XW), 1) // XDIM
        dk = jnp.zeros((XW, XW), f32)
        dv = jnp.zeros((XW, XW), f32)
        for l in range(DEPTH):
            for h in range(XHEADS):
                dk = dk + jnp.where(col == h, refs[l][h], 0.0)
                dv = dv + jnp.where(col == h, refs[DEPTH + l][h], 0.0)
        out_ref[:, 0:XW] = _mm_tn(mem_ref[...], dk)
        out_ref[:, XW:] = _mm_tn(mem_ref[...], dv)

    vm = pl.BlockSpec(memory_space=pltpu.VMEM)
    return pl.pallas_call(body, name="kv_bwd", in_specs=[vm] * (1 + 2 * DEPTH), out_specs=vm,
                          out_shape=jax.ShapeDtypeStruct((D_MODEL, 2 * XW), f32))(mem, *dks, *dvs)


def _tn_gemm(a, b, name, nb):
    S, M = a.shape
    N = b.shape[1]
    NB = N // nb
    nk = S // TK

    def body(a_ref, b_ref, o_ref):
        @pl.when(pl.program_id(1) == 0)
        def _():
            o_ref[...] = jnp.zeros_like(o_ref)

        o_ref[...] += _mm_tn(a_ref[...], b_ref[...])

    return pl.pallas_call(body, name=name, grid=(nb, nk),
                          in_specs=[pl.BlockSpec((TK, M), lambda j, k: (k, 0)), pl.BlockSpec((TK, NB), lambda j, k: (k, j))],
                          out_specs=pl.BlockSpec((M, NB), lambda j, k: (0, j)),
                          out_shape=jax.ShapeDtypeStruct((M, N), f32),
                          compiler_params=pltpu.CompilerParams(dimension_semantics=("parallel", "arbitrary"),
                                                               vmem_limit_bytes=VMEM_LIMIT))(a, b)


def _rows_block(R, mult=16, cap=1024):
    best = R
    for d in range(mult, min(R, cap) + 1, mult):
        if R % d == 0:
            best = d
    return best


def _tn_gemm_sharded(a, b, name):
    S, M = a.shape
    Wq = b.shape[1] // 4
    nk = S // TK

    def body(a_ref, b_ref, o_ref):
        @pl.when(pl.program_id(0) == 0)
        def _():
            o_ref[...] = jnp.zeros_like(o_ref)

        at = a_ref[...].astype(MM)
        for j in range(4):
            o_ref[j] += _mm_tn(at, b_ref[:, j * Wq:(j + 1) * Wq])

    return pl.pallas_call(body, name=name, grid=(nk,),
                          in_specs=[pl.BlockSpec((TK, M), lambda k: (k, 0)), pl.BlockSpec((TK, 4 * Wq), lambda k: (k, 0))],
                          out_specs=pl.BlockSpec((4, M, Wq), lambda k: (0, 0, 0)),
                          out_shape=jax.ShapeDtypeStruct((4, M, Wq), f32), compiler_params=_params())(a, b)


def _tn_gemm_slab(a, b, acc, l, name):
    S, M = a.shape
    N = b.shape[1]
    nk = S // TK

    def body(a_ref, b_ref, *refs):
        o_ref = refs[-1]

        @pl.when(pl.program_id(0) == 0)
        def _():
            o_ref[...] = jnp.zeros_like(o_ref)

        o_ref[...] += _mm_tn(a_ref[...], b_ref[...])

    ins = [a, b] + ([] if acc is None else [acc])
    in_specs = [pl.BlockSpec((TK, M), lambda k: (k, 0)), pl.BlockSpec((TK, N), lambda k: (k, 0))]
    if acc is not None:
        in_specs.append(pl.BlockSpec(memory_space=pl.ANY))
    return pl.pallas_call(body, name=name, grid=(nk,), in_specs=in_specs,
                          out_specs=pl.BlockSpec((None, M, N), lambda k: (l, 0, 0)),
                          out_shape=jax.ShapeDtypeStruct((DEPTH, M, N), f32),
                          input_output_aliases={} if acc is None else {2: 0},
                          compiler_params=_params())(*ins)


HALF_ROWS = D_MODEL // 2
SHARD_ROWS = D_MODEL // 4


def _half_of_full(ref, kind, h):
    if kind == "kv":
        return ref.at[:, pl.ds(h * XW, XW)]
    if kind == "wout":
        return ref.at[pl.ds(2 * h, 2)]
    return ref.at[:, pl.ds(h * HALF_ROWS, HALF_ROWS)]


def _shard_of_half(ref, kind, j):
    if kind == "kv":
        return ref.at[pl.ds(j * SHARD_ROWS, SHARD_ROWS)]
    if kind == "wout":
        return ref.at[:, pl.ds(j * SHARD_ROWS, SHARD_ROWS)]
    return ref.at[j]


def _half_of_shard(ref, kind, h):
    if kind == "kv":
        return ref.at[:, pl.ds(h * XW, XW)]
    if kind == "wout":
        return ref.at[pl.ds(2 * h, 2)]
    return ref.at[pl.ds(h * HALF_ROWS, HALF_ROWS)]


def _half_shape(full_shape, kind):
    if kind == "kv":
        return (full_shape[0], XW)
    if kind == "wout":
        return (2,) + tuple(full_shape[1:])
    return (4, HALF_ROWS, full_shape[2])


def _shard_half_shape(full_shape, kind):
    if kind == "kv":
        return (SHARD_ROWS, XW)
    if kind == "wout":
        return (2, SHARD_ROWS, full_shape[2])
    return (HALF_ROWS, full_shape[2])


def _shard_shape(full_shape, kind):
    if kind == "kv":
        return (SHARD_ROWS, full_shape[1])
    if kind == "wout":
        return (DEPTH, SHARD_ROWS, full_shape[2])
    return (D_MODEL, full_shape[2])


def _ew_call(body, name, grid, jc, ins, in_specs, out_shape, out_specs):
    gs = pltpu.PrefetchScalarGridSpec(num_scalar_prefetch=1, grid=grid, in_specs=in_specs, out_specs=out_specs)
    return pl.pallas_call(body, name=name, grid_spec=gs, out_shape=out_shape,
                          compiler_params=pltpu.CompilerParams(dimension_semantics=("parallel",) * len(grid),
                                                               vmem_limit_bytes=VMEM_LIMIT))(jc, *ins)


def _add_sibling(part, got, kind, jc, name):
    def body(jc_ref, a_ref, b_ref, o_ref, ob_ref):
        s = a_ref[...] + b_ref[...]
        o_ref[...] = s
        ob_ref[...] = s.astype(bf16)

    if kind == "kv":
        R = part.shape[0]
        grid = (2,)
        mine = pl.BlockSpec((R // 2, XW), lambda i, jc_ref: (i, jc_ref[1]))
        spec = pl.BlockSpec((R // 2, XW), lambda i, jc_ref: (i, 0))
    elif kind == "wout":
        _, R, C = part.shape
        grid = (2, 2)
        mine = pl.BlockSpec((None, R // 2, C), lambda s, i, jc_ref: (2 * jc_ref[1] + s, i, 0))
        spec = pl.BlockSpec((None, R // 2, C), lambda s, i, jc_ref: (s, i, 0))
    else:
        C = part.shape[2]
        grid = (4, 2)
        mine = pl.BlockSpec((None, HALF_ROWS // 2, C), lambda s, i, jc_ref: (s, 2 * jc_ref[1] + i, 0))
        spec = pl.BlockSpec((None, HALF_ROWS // 2, C), lambda s, i, jc_ref: (s, i, 0))
    hs = _half_shape(part.shape, kind)
    return _ew_call(body, name, grid, jc, [part, got], [mine, spec],
                    [jax.ShapeDtypeStruct(hs, f32), jax.ShapeDtypeStruct(hs, bf16)], [spec, spec])


def _add_chips(q32, r, kind, jc, name):
    def body(jc_ref, q_ref, r_ref, out_ref):
        out_ref[...] = ((q_ref[...] + r_ref[0].astype(f32)) + r_ref[1].astype(f32)) + r_ref[2].astype(f32)

    if kind == "kv":
        grid = (1,)
        qs = pl.BlockSpec((SHARD_ROWS, XW), lambda i, jc_ref: (jc_ref[0], 0))
        rs = pl.BlockSpec((3, SHARD_ROWS, XW), lambda i, jc_ref: (0, 0, 0))
        os_ = pl.BlockSpec((SHARD_ROWS, XW), lambda i, jc_ref: (0, jc_ref[1]))
    elif kind == "wout":
        C = q32.shape[2]
        grid = (2,)
        qs = pl.BlockSpec((None, SHARD_ROWS, C), lambda s, jc_ref: (s, jc_ref[0], 0))
        rs = pl.BlockSpec((3, None, SHARD_ROWS, C), lambda s, jc_ref: (0, s, 0, 0))
        os_ = pl.BlockSpec((None, SHARD_ROWS, C), lambda s, jc_ref: (2 * jc_ref[1] + s, 0, 0))
    else:
        C = q32.shape[2]
        grid = (2,)
        qs = pl.BlockSpec((None, HALF_ROWS // 2, C), lambda i, jc_ref: (jc_ref[0], i, 0))
        rs = pl.BlockSpec((3, HALF_ROWS // 2, C), lambda i, jc_ref: (0, i, 0))
        os_ = pl.BlockSpec((HALF_ROWS // 2, C), lambda i, jc_ref: (2 * jc_ref[1] + i, 0))
    if kind == "kv":
        full_shape = (D_MODEL, 2 * XW)
    elif kind == "wout":
        full_shape = (DEPTH, D_MODEL, D_MODEL)
    else:
        full_shape = (4, D_MODEL, q32.shape[2])
    return _ew_call(body, name, grid, jc, [q32, r], [qs, rs], jax.ShapeDtypeStruct(_shard_shape(full_shape, kind), f32), os_)


def _adamw(w, g, m, v, name):
    R, C = w.shape
    br = _rows_block(R, mult=SUB, cap=512)
    c1 =1.0 / (1.0 - ADAM_B1 ** ADAM_STEP)
    c2 = 1.0 / (1.0 - ADAM_B2 ** ADAM_STEP)

    def body(w_ref, g_ref, m_ref, v_ref, d_ref, nm_ref, nv_ref):
        g_ = g_ref[...]
        nm = ADAM_B1 * m_ref[...] + (1.0 - ADAM_B1) * g_
        nv = ADAM_B2 * v_ref[...] + (1.0 - ADAM_B2) * (g_ * g_)
        nm_ref[...] = nm
        nv_ref[...] = nv
        d_ref[...] = -ADAM_LR * ((nm * c1) / (jnp.sqrt(nv * c2) + ADAM_EPS) + ADAM_WD * w_ref[...])

    spec = pl.BlockSpec((br, C), lambda i: (i, 0))
    sh = jax.ShapeDtypeStruct((R, C), f32)
    return pl.pallas_call(body, name=name, grid=(R // br,), in_specs=[spec] * 4, out_specs=[spec] * 3,
                          out_shape=[sh, sh, sh], compiler_params=_params("parallel"))(w, g, m, v)


def _small_finish(dbacc, p_soft, dlb):
    def body(db_ref, p_ref, dlb_ref, dbs_ref, dlg_ref):
        lane = lax.broadcasted_iota(jnp.int32, (HD, HD), 1)
        acc = jnp.zeros((HD, HD), f32)
        for h in range(NH):
            acc = acc + jnp.where(lane == h, jnp.sum(db_ref[h], axis=-1, keepdims=True), 0.0)
        dbs_ref[...] = acc
        p = p_ref[...]
        p1 = p[1:2, :]
        rowi = lax.broadcasted_iota(jnp.int32, p.shape, 0)
        dlg_ref[...] = dlb_ref[0:1, :] * p1 * (jnp.where(rowi == 1, 1.0, 0.0) - p)

    vm = pl.BlockSpec(memory_space=pltpu.VMEM)
    return pl.pallas_call(body, name="small_finish", in_specs=[vm] * 3, out_specs=[vm] * 2,
                          out_shape=[jax.ShapeDtypeStruct((HD, HD), f32), jax.ShapeDtypeStruct(p_soft.shape, f32)])(dbacc, p_soft, dlb)


def _where_am_i():
    return lax.axis_index("x"), lax.axis_index("y"), lax.axis_index("c")


MAX_PIECES = 8


def _nchunks(rows, mult):
    for n in range(MAX_PIECES, 0, -1):
        if rows % (n * mult) == 0:
            return n
    return 1


def _leading_pieces(src, dst):
    n = src.shape[0]
    if len(src.shape) >= 3 and n <= MAX_PIECES:
        return [(src.at[s], dst.at[s]) for s in range(n)]
    return [(src, dst)]


def _ag_weights(shards, kinds, jshard):
    n = len(shards)

    def body(*refs):
        sh_refs, out_refs = refs[:n], refs[2 * n:3 * n]
        send_sems, recv_sems = refs[3 * n:]
        x, y, c = _where_am_i()
        j = 2 * x + y
        sib = (x, y, 1 - c)
        chips = [(1 - x, y), (x, 1 - y), (1 - x, 1 - y)]

        def cp(k, src, dst, to):
            return pltpu.make_async_remote_copy(src_ref=src, dst_ref=dst, send_sem=send_sems.at[k], recv_sem=recv_sems.at[k],
                                                device_id=to, device_id_type=MESH)

        started = []
        for a in range(n):
            for k, (cx, cy) in enumerate(chips):
                d = cp(6 * a + k, _half_of_shard(sh_refs[a], kinds[a], c), _half_of_shard(out_refs[a].at[j], kinds[a], c), (cx, cy, c))
                d.start()
                started.append(d)
        for a in range(n):
            for k, (cx, cy) in enumerate(chips):
                blk = _half_of_shard(out_refs[a].at[2 * cx + cy], kinds[a], c)
                cp(6 * a + k, blk, blk, (cx, cy, c)).wait_recv()
                d = cp(6 * a + 3 + k, blk, blk, sib)
                d.start()
                started.append(d)
        for a in range(n):
            for k, (cx, cy) in enumerate(chips):
                blk = _half_of_shard(out_refs[a].at[2 * cx + cy], kinds[a], 1 - c)
                cp(6 * a + 3 + k, blk, blk, sib).wait_recv()
        for d in started:
            d.wait_send()

    placed = [lax.dynamic_update_slice(jnp.zeros((4,) + s.shape, s.dtype), s[None], (jshard,) + (0,) * s.ndim) for s in shards]
    anyspec = pl.BlockSpec(memory_space=pl.ANY)
    return pl.pallas_call(body, name="all_gather_weights", in_specs=[anyspec] * (2 * n), out_specs=[anyspec] * n,
                          out_shape=[jax.ShapeDtypeStruct(p.shape, p.dtype) for p in placed],
                          input_output_aliases={n + a: a for a in range(n)},
                          scratch_shapes=[pltpu.SemaphoreType.DMA((6 * n,)), pltpu.SemaphoreType.DMA((6 * n,))],
                          compiler_params=pltpu.CompilerParams(has_side_effects=True))(*shards, *placed)


def _rs_swap(parts, kinds):
    n = len(parts)

    def body(*refs):
        p_refs, got_refs = refs[:n], refs[n:2 * n]
        send_sems, recv_sems = refs[2 * n:]
        x, y, c = _where_am_i()

        def cp(a, src, dst):
            return pltpu.make_async_remote_copy(src_ref=src, dst_ref=dst, send_sem=send_sems.at[a], recv_sem=recv_sems.at[a],
                                                device_id=(x, y, 1 - c), device_id_type=MESH)

        for a in range(n):
            for src, dst in _leading_pieces(_half_of_full(p_refs[a], kinds[a], 1 - c), got_refs[a]):
                cp(a, src, dst).start()
        for a in range(n):
            cp(a, got_refs[a], got_refs[a]).wait()

    anyspec = pl.BlockSpec(memory_space=pl.ANY)
    return pl.pallas_call(body, name="rs_swap_halves", in_specs=[anyspec] * n, out_specs=[anyspec] * n,
                          out_shape=[jax.ShapeDtypeStruct(_half_shape(p.shape, k), p.dtype) for p, k in zip(parts, kinds)],
                          scratch_shapes=[pltpu.SemaphoreType.DMA((n,)), pltpu.SemaphoreType.DMA((n,))],
                          compiler_params=pltpu.CompilerParams(has_side_effects=True))(*parts)


def _rs_owners(qbs, kinds, full_shapes):
    n = len(qbs)

    def body(*refs):
        q_refs, got_refs = refs[:n], refs[n:2 * n]
        send_sems, recv_sems = refs[2 * n:]
        x, y, c = _where_am_i()
        chips = [(1 - x, y), (x, 1 - y), (1 - x, 1 - y)]
        ds = []
        for a in range(n):
            for k, (cx, cy) in enumerate(chips):
                d = pltpu.make_async_remote_copy(src_ref=_shard_of_half(q_refs[a], kinds[a], 2 * cx + cy), dst_ref=got_refs[a].at[k],
                                                 send_sem=send_sems.at[3 * a + k], recv_sem=recv_sems.at[3 * a + k],
                                                 device_id=(cx, cy, c), device_id_type=MESH)
                d.start()
                ds.append(d)
        for d in ds:
            d.wait()

    anyspec = pl.BlockSpec(memory_space=pl.ANY)
    return pl.pallas_call(body, name="rs_to_owners", in_specs=[anyspec] * n, out_specs=[anyspec] * n,
                          out_shape=[jax.ShapeDtypeStruct((3,) + _shard_half_shape(fs, k), bf16) for fs, k in zip(full_shapes, kinds)],
                          scratch_shapes=[pltpu.SemaphoreType.DMA((3 * n,)), pltpu.SemaphoreType.DMA((3 * n,))],
                          compiler_params=pltpu.CompilerParams(has_side_effects=True))(*qbs)


def _rs_join(bufs, kinds):
    n = len(bufs)

    def body(*refs):
        out_refs = refs[n:2 * n]
        send_sems, recv_sems = refs[2 * n:]
        x, y, c = _where_am_i()

        def cp(a, h):
            blk = _half_of_shard(out_refs[a], kinds[a], h)
            return pltpu.make_async_remote_copy(src_ref=blk, dst_ref=blk, send_sem=send_sems.at[a], recv_sem=recv_sems.at[a],
                                                device_id=(x, y, 1 - c), device_id_type=MESH)

        for a in range(n):
            cp(a, c).start()
        for a in range(n):
            cp(a, c).wait_send()
            cp(a, 1 - c).wait_recv()

    anyspec = pl.BlockSpec(memory_space=pl.ANY)
    return pl.pallas_call(body, name="rs_join_halves", in_specs=[anyspec] * n, out_specs=[anyspec] * n,
                          out_shape=[jax.ShapeDtypeStruct(b.shape, b.dtype) for b in bufs],
                          input_output_aliases={a: a for a in range(n)},
                          scratch_shapes=[pltpu.SemaphoreType.DMA((n,)), pltpu.SemaphoreType.DMA((n,))],
                          compiler_params=pltpu.CompilerParams(has_side_effects=True))(*bufs)


def _all_reduce_small(g):
    R, C = g.shape
    H = R // 2
    NP = _nchunks(H, SUB)
    PR = H // NP

    def body(g_ref, out_ref, sib_ref, chip_ref, send_sems, recv_sems):
        x, y, c = _where_am_i()
        j = 2 * x + y
        sib = (x, y, 1 - c)
        chips = [(1 - x, y), (x, 1 - y), (1 - x, 1 - y)]
        rows = pl.ds(pl.multiple_of(c * H, SUB), H)

        def cp(k, src, dst, to):
            return pltpu.make_async_remote_copy(src_ref=src, dst_ref=dst, send_sem=send_sems.at[k], recv_sem=recv_sems.at[k],
                                                device_id=to, device_id_type=MESH)

        def pieces(k, src, dst, to):
            for q in range(NP):
                cp(k, src.at[pl.ds(q * PR, PR)], dst.at[pl.ds(q * PR, PR)], to).start()

        for half in range(2):
            pieces(0, g_ref.at[pl.ds(half * H, H)], sib_ref.at[pl.ds(half * H, H)], sib)
        cp(0, g_ref, sib_ref, sib).wait()
        chip_ref[j] = g_ref[rows, :] + sib_ref[rows, :]
        for k, (cx, cy) in enumerate(chips):
            pieces(1 + k, chip_ref.at[j], chip_ref.at[j], (cx, cy, c))
        for k, (cx, cy) in enumerate(chips):
            blk = chip_ref.at[2 * cx + cy]
            cp(1 + k, blk, blk, (cx, cy, c)).wait()
        out_ref[rows, :] = ((chip_ref[0] + chip_ref[1]) + chip_ref[2]) + chip_ref[3]
        other = out_ref.at[pl.ds(pl.multiple_of((1 - c) * H, SUB), H)]
        pieces(4, out_ref.at[rows], out_ref.at[rows], sib)
        cp(4, other, other, sib).wait()

    vm = pl.BlockSpec(memory_space=pltpu.VMEM)
    return pl.pallas_call(body, name="all_reduce_small", in_specs=[vm], out_specs=vm,
                          out_shape=jax.ShapeDtypeStruct((R, C), f32),
                          scratch_shapes=[pltpu.VMEM((R, C), f32), pltpu.VMEM((4, H, C), f32),
                                          pltpu.SemaphoreType.DMA((5,)), pltpu.SemaphoreType.DMA((5,))],
                          compiler_params=pltpu.CompilerParams(has_side_effects=True, vmem_limit_bytes=VMEM_LIMIT))(g)


SPLIT_MIN_ELEMS = 1 << 16


def _all_reduce_many(gs):
    n = len(gs)
    split = [g.ndim == 3 and g.shape[0] % 2 == 0 and g.size >= SPLIT_MIN_ELEMS for g in gs]
    part_shape = [((g.shape[0] // 2,) + g.shape[1:]) if s else g.shape for g, s in zip(gs, split)]
    n_split = sum(split)

    def body(*refs):
        g, out, sibs, chipb = refs[:n], refs[n:2 * n], refs[2 * n:3 * n], refs[3 * n:4 * n]
        send_sems, recv_sems = refs[4 * n:]
        x, y, c = _where_am_i()
        j = 2 * x + y
        sib = (x, y, 1 - c)
        chips = [(1 - x, y), (x, 1 - y), (1 - x, 1 - y)]

        def cp(k, src, dst, to):
            return pltpu.make_async_remote_copy(src_ref=src, dst_ref=dst, send_sem=send_sems.at[k], recv_sem=recv_sems.at[k],
                                                device_id=to, device_id_type=MESH)

        def part(a, h):
            return pl.ds(h * part_shape[a][0], part_shape[a][0]) if split[a] else Ellipsis

        def mine(ref, a, h):
            return ref.at[part(a, h)] if split[a] else ref

        swaps = [cp(a, g[a], sibs[a], sib) for a in range(n)]
        for d in swaps:
            d.start()
        for a in range(n):
            swaps[a].wait()
            chipb[a][j] = g[a][part(a, c)] + sibs[a][part(a, c)]
        sends = [cp(n + 3 * a + k, chipb[a].at[j], chipb[a].at[j], (cx, cy, c)) for a in range(n) for k, (cx, cy) in enumerate(chips)]
        for d in sends:
            d.start()
        for a in range(n):
            for k, (cx, cy) in enumerate(chips):
                blk = chipb[a].at[2 * cx + cy]
                cp(n + 3 * a + k, blk, blk, (cx, cy, c)).wait_recv()
            out[a][part(a, c)] = ((chipb[a][0] + chipb[a][1]) + chipb[a][2]) + chipb[a][3]
        for d in sends:
            d.wait_send()
        backs = [(a, cp(4 * n + i, mine(out[a], a, c), mine(out[a], a, c), sib)) for i, a in enumerate([a for a in range(n) if split[a]])]
        for _, d in backs:
            d.start()
        for i, (a, d) in enumerate(backs):
            d.wait_send()
            cp(4 * n + i, mine(out[a], a, 1 - c), mine(out[a], a, 1 - c), sib).wait_recv()

    vm = pl.BlockSpec(memory_space=pltpu.VMEM)
    nsem = 4 * n + n_split
    return pl.pallas_call(body, name="all_reduce_small_grads", in_specs=[vm] * n, out_specs=[vm] * n,
                          out_shape=[jax.ShapeDtypeStruct(g.shape, f32) for g in gs],
                          scratch_shapes=([pltpu.VMEM(g.shape, f32) for g in gs] + [pltpu.VMEM((4,) + ps, f32) for ps in part_shape]
                                          + [pltpu.SemaphoreType.DMA((nsem,)), pltpu.SemaphoreType.DMA((nsem,))]),
                          compiler_params=pltpu.CompilerParams(has_side_effects=True, vmem_limit_bytes=VMEM_LIMIT))(*gs)


def _adamw_many(ws, gs, ms, vs, name):
    n = len(ws)
    c1 = 1.0 / (1.0 - ADAM_B1 ** ADAM_STEP)
    c2 = 1.0 / (1.0 - ADAM_B2 ** ADAM_STEP)

    def body(*refs):
        for a in range(n):
            w_ref, g_ref, m_ref, v_ref, d_ref, nm_ref, nv_ref = (refs[i * n + a] for i in range(7))
            g_ = g_ref[...]
            nm = ADAM_B1 * m_ref[...] + (1.0 - ADAM_B1) * g_
            nv = ADAM_B2 * v_ref[...] + (1.0 - ADAM_B2) * (g_ * g_)
            nm_ref[...] = nm
            nv_ref[...] = nv
            d_ref[...] = -ADAM_LR * ((nm * c1) / (jnp.sqrt(nv * c2) + ADAM_EPS) + ADAM_WD * w_ref[...])

    vm = pl.BlockSpec(memory_space=pltpu.VMEM)
    sh = [jax.ShapeDtypeStruct(w.shape, f32) for w in ws]
    outs = pl.pallas_call(body, name=name, in_specs=[vm] * (4 * n), out_specs=[vm] * (3 * n), out_shape=sh * 3,
                          compiler_params=pltpu.CompilerParams(vmem_limit_bytes=VMEM_LIMIT))(*ws, *gs, *ms, *vs)
    return outs[:n], outs[n:2 * n], outs[2 * n:]


def _pack_flat(arrs, rows_mult):
    flat = jnp.concatenate([a.reshape(-1) for a in arrs])
    n = flat.shape[0]
    tot = -(-n // (rows_mult * LANE)) * rows_mult * LANE
    return jnp.pad(flat, (0, tot - n)).reshape(-1, LANE)


def _unpack_flat(buf, shapes):
    flat = buf.reshape(-1)
    out, o = [], 0
    for s in shapes:
        n = math.prod(s)
        out.append(flat[o:o + n].reshape(s))
        o += n
    return out


_BIG = ("mem_kv_w", "w_out", "a_w_in", "b_w_in", "c_w_in", "d_w_in")
SMALL_ROWS_MULT = 256


def _row8(v):
    v = v.reshape(-1, v.shape[-1])
    return jnp.pad(v, ((0, SUB - v.shape[0]), (0, 0)))


def kernel(x, mem, mem_kv_w, ln_g, ln_b, w_out, hgrn_lb_logits, a_w_in, a_w_s, a_b_s, b_w_in, b_norm_g, c_w_in, c_w_pool, c_scale, d_w_in, d_conv_w, d_conv_b, d_w_gx, d_b_gx, d_w_ga, d_b_ga, d_a_param, loss_target, m_mem_kv_w, m_ln_g, m_ln_b, m_w_out, m_hgrn_lb_logits, m_a_w_in, m_a_w_s, m_a_b_s, m_b_w_in, m_b_norm_g, m_c_w_in, m_c_w_pool, m_c_scale, m_d_w_in, m_d_conv_w, m_d_conv_b, m_d_w_gx, m_d_b_gx, m_d_w_ga, m_d_b_ga, m_d_a_param, v_mem_kv_w, v_ln_g, v_ln_b, v_w_out, v_hgrn_lb_logits, v_a_w_in, v_a_w_s, v_a_b_s, v_b_w_in, v_b_norm_g, v_c_w_in, v_c_w_pool, v_c_scale, v_d_w_in, v_d_conv_w, v_d_conv_b, v_d_w_gx, v_d_b_gx, v_d_w_ga, v_d_b_ga, v_d_a_param):
    names = ["mem_kv_w", "ln_g", "ln_b", "w_out", "hgrn_lb_logits", "a_w_in", "a_w_s", "a_b_s", "b_w_in", "b_norm_g", "c_w_in",
             "c_w_pool", "c_scale", "d_w_in", "d_conv_w", "d_conv_b", "d_w_gx", "d_b_gx", "d_w_ga", "d_b_ga", "d_a_param"]
    w = dict(mem_kv_w=mem_kv_w, ln_g=ln_g, ln_b=ln_b, w_out=w_out, hgrn_lb_logits=hgrn_lb_logits, a_w_in=a_w_in, a_w_s=a_w_s,
             a_b_s=a_b_s, b_w_in=b_w_in, b_norm_g=b_norm_g, c_w_in=c_w_in, c_w_pool=c_w_pool, c_scale=c_scale, d_w_in=d_w_in,
             d_conv_w=d_conv_w, d_conv_b=d_conv_b, d_w_gx=d_w_gx, d_b_gx=d_b_gx, d_w_ga=d_w_ga, d_b_ga=d_b_ga, d_a_param=d_a_param)
    m = dict(zip(names, [m_mem_kv_w, m_ln_g, m_ln_b, m_w_out, m_hgrn_lb_logits, m_a_w_in, m_a_w_s, m_a_b_s, m_b_w_in, m_b_norm_g,
                         m_c_w_in, m_c_w_pool, m_c_scale, m_d_w_in, m_d_conv_w, m_d_conv_b, m_d_w_gx, m_d_b_gx, m_d_w_ga,
                         m_d_b_ga, m_d_a_param]))
    v = dict(zip(names, [v_mem_kv_w, v_ln_g, v_ln_b, v_w_out, v_hgrn_lb_logits, v_a_w_in, v_a_w_s, v_a_b_s, v_b_w_in, v_b_norm_g,
                         v_c_w_in, v_c_w_pool, v_c_scale, v_d_w_in, v_d_conv_w, v_d_conv_b, v_d_w_gx, v_d_b_gx, v_d_w_ga,
                         v_d_b_ga, v_d_a_param]))
    xi, yi = lax.axis_index("x"), lax.axis_index("y")
    jshard = 2 * xi + yi
    x2 = x[0]
    mem2 = mem[0]
    tgt2 = loss_target[0]

    kinds = ("kv", "wout", "win", "win", "win", "win")
    big2d = lambda d: [d["mem_kv_w"], d["w_out"]] + [d[n][0] for n in _BIG[2:]]
    gath = _ag_weights([a.astype(bf16) for a in big2d(w)], kinds, jshard)
    w_kv = gath[0].reshape(D_MODEL, 2 * XW)
    w_outs = [gath[1][:, l].reshape(D_MODEL, D_MODEL) for l in range(DEPTH)]
    w_outT = gath[1].transpose(1, 3, 0, 2).reshape(DEPTH, D_MODEL, D_MODEL)
    w_in = [g.transpose(1, 0, 2).reshape(D_MODEL, -1) for g in gath[2:]]
    w_inT = [g.transpose(0, 2, 1).reshape(-1, D_MODEL) for g in gath[2:]]

    def gather_small(shard):
        z = jnp.zeros((4, POOL_GROUP), f32)
        return lax.dynamic_update_slice(z, shard.reshape(1, POOL_GROUP), (jshard, 0))

    sm_sh = jnp.concatenate([gather_small(b_norm_g), gather_small(c_scale), gather_small(d_conv_b), gather_small(d_a_param)]
                            + [gather_small(d_conv_w[:, r]) for r in range(4)], axis=0)
    ci = lax.axis_index("c")
    sm_all = _all_reduce_small(_pack_flat([jnp.where(ci == 0, sm_sh, 0.0)], SUB * 2))
    sm = _unpack_flat(sm_all, [(8, 4 * POOL_GROUP)])[0]
    ng_full, scale_full, convb_full, ap_full = sm[0:1], sm[1:2], sm[2:3], sm[3:4]
    convw_full = sm[4:8]

    tril = jnp.tril(jnp.ones((HD, HD), bool))
    wtri = jnp.where(tril, a_w_s[0], 0.0)
    wbd = jnp.zeros((TOK, TOK), f32)
    for g in range(4):
        wbd = lax.dynamic_update_slice(wbd, c_w_pool[0, g], (g * POOL_GROUP, g * POOL_GROUP))
    kh, khT, vh, vhT, p_soft = _prep(mem2, w_kv, hgrn_lb_logits)
    prm = [
        dict(wtri=wtri.astype(bf16), wtriT=wtri.transpose(0, 2, 1).astype(bf16),
             bcolb=jnp.broadcast_to(a_b_s[0][:, :, None], (NH, HD, HD))),
        dict(lb=p_soft[1:2], ng=ng_full),
        dict(wbd=wbd.astype(bf16), wbdT=wbd.T.astype(bf16), scale=scale_full),
        dict(cw=_row8(convw_full), cb=convb_full, wgx=d_w_gx[0].astype(bf16), wgxT=d_w_gx[0].transpose(0, 2, 1).astype(bf16),
             bgx=d_b_gx.reshape(1, TOK), wga=d_w_ga[0].astype(bf16), wgaT=d_w_ga[0].transpose(0, 2, 1).astype(bf16),
             bga=d_b_ga.reshape(1, TOK), ap=ap_full),
    ]

    acts = []
    h = x2
    for l in range(DEPTH):
        outs = _fwd_layer(l, h, w_in[l], w_outs[l], ln_g[l:l + 1], ln_b[l:l + 1], khT, vh, prm[l],
                          tgt2 if l == DEPTH - 1 else None)
        nfix = 4 if l == DEPTH - 1 else 3
        acts.append(dict(xin=h, proj=outs[1], z=outs[2], saves=outs[nfix:]))
        if l == DEPTH - 1:
            loss_part = outs[3]
        h = outs[0]
    loss = lax.psum(0.5 / D_MODEL * jnp.sum(loss_part), ("x", "y", "c"))

    dh = h
    gw_in = [None] * DEPTH
    gw_out = None
    dln = [None] * DEPTH
    dks, dvs = [None] * DEPTH, [None] * DEPTH
    sgr = [None] * DEPTH
    for l in reversed(range(DEPTH)):
        a = acts[l]
        (dxin, dproj, mixedb, dyb, dln[l], dks[l], dvs[l]), sgr[l] = _bwd_layer(
            l, dh, a["z"], a["proj"], w_inT[l], w_outT[l], ln_g[l:l + 1], kh, khT, vh, vhT, prm[l], a["saves"])
        if _OFFS[l]["W"] // 4 % LANE:
            gw_in[l] = _tn_gemm(a["xin"], dproj, f"grad_w_in{l}", 1).reshape(D_MODEL, 4, -1).transpose(1, 0, 2)
        else:
            gw_in[l] = _tn_gemm_sharded(a["xin"], dproj, f"grad_w_in{l}")
        gw_out = _tn_gemm_slab(mixedb, dyb, gw_out, l, f"grad_w_out{l}")
        dh = dxin
    grad_x = dh[None]

    parts = [_kv_bwd(mem2, dks, dvs), gw_out] + gw_in
    jc = jnp.stack([jshard, ci]).astype(jnp.int32)
    gots = _rs_swap(parts, kinds)
    sums = [_add_sibling(p, g, k, jc, f"rs_add_sibling{a}") for a, (p, g, k) in enumerate(zip(parts, gots, kinds))]
    gots2 = _rs_owners([s[1] for s in sums], kinds, [p.shape for p in parts])
    gbig = _rs_join([_add_chips(s[0], r, k, jc, f"rs_add_chips{a}") for a, (s, r, k) in enumerate(zip(sums, gots2, kinds))], kinds)
    g_sh, d_sh, m_sh, v_sh = {}, {}, {}, {}
    for a, n in enumerate(_BIG):
        as2d = lambda t: t.reshape(-1, t.shape[-1])
        upd = _adamw(as2d(w[n]), as2d(gbig[a]), as2d(m[n]), as2d(v[n]), f"adamw_{n}")
        g_sh[n] = gbig[a].reshape(w[n].shape)
        d_sh[n], m_sh[n], v_sh[n] = (u.reshape(w[n].shape) for u in upd)

    dbs, dlogits = _small_finish(sgr[0]["dbacc"], p_soft, sgr[1]["dlb"])
    gs = {
        "ln_g": jnp.concatenate([dln[l][0:1] for l in range(DEPTH)], axis=0),
        "ln_b": jnp.concatenate([dln[l][1:2] for l in range(DEPTH)], axis=0),
        "hgrn_lb_logits": dlogits,
        "a_w_s": sgr[0]["dwtri"][None],
        "a_b_s": dbs[:, 0:NH].T[None],
        "b_norm_g": sgr[1]["dng"][0:1],
        "c_w_pool": jnp.stack([sgr[2]["dwbd"][g * POOL_GROUP:(g + 1) * POOL_GROUP, g * POOL_GROUP:(g + 1) * POOL_GROUP]
                               for g in range(4)])[None],
        "c_scale": sgr[2]["dscale"][0:1],
        "d_conv_w": sgr[3]["dcw"][0:4][None],
        "d_conv_b": sgr[3]["dvec"][3:4],
        "d_w_gx": sgr[3]["dwgx"][None],
        "d_b_gx": sgr[3]["dvec"][1:2].reshape(1, NH, HD),
        "d_w_ga": sgr[3]["dwga"][None],
        "d_b_ga": sgr[3]["dvec"][2:3].reshape(1, NH, HD),
        "d_a_param": sgr[3]["dvec"][0:1],
    }
    small = [n for n in names if n not in _BIG]
    drop1 = lambda t: t.reshape(t.shape[1:]) if t.ndim > 2 and t.shape[0] == 1 else t
    gsum = dict(zip(small, _all_reduce_many([drop1(gs[n]) for n in small])))
    for n in ("b_norm_g", "c_scale", "d_conv_b", "d_a_param"):
        gsum[n] = lax.dynamic_slice(gsum[n], (0, jshard * POOL_GROUP), (1, POOL_GROUP))
    gsum["d_conv_w"] = lax.dynamic_slice(gsum["d_conv_w"], (0, jshard * POOL_GROUP), (4, POOL_GROUP))
    upd = _adamw_many(*[[drop1(d[n]) for n in small] for d in (w, gsum, m, v)], "adamw_small")
    gsum = {n: gsum[n].reshape(w[n].shape) for n in small}
    d_sm, m_sm, v_sm = ({n: u.reshape(w[n].shape) for n, u in zip(small, us)} for us in upd)

    grads = {**gsum, **g_sh}
    deltas = {**d_sm, **d_sh}
    new_m = {**m_sm, **m_sh}
    new_v = {**v_sm, **v_sh}
    return (loss, grad_x, *[grads[n] for n in names], *[deltas[n] for n in names], *[new_m[n] for n in names],
            *[new_v[n] for n in names])
```

```python
import functools
import math

import jax
import jax.numpy as jnp
from jax import lax
from jax.experimental import pallas as pl
from jax.experimental.pallas import tpu as pltpu

f32 = jnp.float32
bf16 = jnp.bfloat16
MM = bf16

D_MODEL = 1024
TOK = 768
XW = 256
XHEADS = 4
XDIM = 64
HD = 128
NH = TOK // HD
CHUNK = 16
POOL_GROUP = 192
DEPTH = 4
ALPHA = (2 * DEPTH) ** 0.25
LN_EPS = 1e-5
RMS_EPS = 1e-6
LRU_C = 8.0
ADAM_LR, ADAM_B1, ADAM_B2, ADAM_EPS, ADAM_WD, ADAM_STEP = 0.001, 0.9, 0.999, 1e-08, 0.01, 10

_TS = (256, 256, 256, 256)
HGRN_SUB = 128
TK = 512
SUB = 8
LANE = 128
VMEM_LIMIT = 58 * 1024 * 1024

_OFFS = (
    dict(u=0, v=768, qx=1536, gate=1792, W=2816),
    dict(q=0, f=768, i=1536, qx=2304, gate=2560, W=3584),
    dict(p=0, qx=768, gate=1024, W=2048),
    dict(xb=0, qx=768, gate=1024, W=2048),
)
_PRM = (
    ("wtri", "wtriT", "bcolb"),
    ("lb", "ng"),
    ("wbd", "wbdT", "scale"),
    ("cw", "cb", "wgx", "wgxT", "bgx", "wga", "wgaT", "bga", "ap"),
)
MESH = pl.DeviceIdType.MESH


def _mm(a, b):
    return jnp.dot(a.astype(MM), b.astype(MM), preferred_element_type=f32)


def _mm_nt(a, b):
    return lax.dot_general(a.astype(MM), b.astype(MM), (((1,), (1,)), ((), ())), preferred_element_type=f32)


def _mm_tn(a, b):
    return lax.dot_general(a.astype(MM), b.astype(MM), (((0,), (0,)), ((), ())), preferred_element_type=f32)


def _mm_sel(sel, b):
    s = sel.astype(bf16)
    hi = b.astype(bf16)
    lo = (b - hi.astype(f32)).astype(bf16)
    return jnp.dot(s, hi, preferred_element_type=f32) + jnp.dot(s, lo, preferred_element_type=f32)


def _sig(x):
    return jax.nn.sigmoid(x)


_GC = math.sqrt(2.0 / math.pi)


def _gelu(x):
    t = jnp.tanh(_GC * (x + 0.044715 * x * x * x))
    return 0.5 * x * (1.0 + t), t


def _gelu_grad(x, t):
    return 0.5 * (1.0 + t) + 0.5 * x * (1.0 - t * t) * _GC * (1.0 + 3.0 * 0.044715 * x * x)


def _rowsum(x):
    return jnp.sum(x, axis=0, keepdims=True)


def _lmean(x):
    return jnp.mean(x, axis=-1, keepdims=True)


def _ln(z):
    mu = _lmean(z)
    zc = z - mu
    rstd = lax.rsqrt(_lmean(zc * zc) + LN_EPS)
    return zc * rstd, rstd


def _ln_bwd(dxh, xhat, rstd):
    return rstd * (dxh - _lmean(dxh) - xhat * _lmean(dxh * xhat))


def _hs(h):
    return slice(h * HD, (h + 1) * HD)


def _expm1(x):
    small = x * (1.0 + x * 0.5 * (1.0 + x * (1.0 / 3.0) * (1.0 + x * 0.25 * (1.0 + x * 0.2 * (1.0 + x * (1.0 / 6.0))))))
    return jnp.where(jnp.abs(x) < 0.25, small, jnp.exp(x) - 1.0)


def _softplus(x):
    e = jnp.exp(-jnp.abs(x))
    l1p = jnp.where(e < 1e-4, e - 0.5 * e * e, jnp.log(1.0 + e))
    return jnp.maximum(x, 0.0) + l1p


def _scan_fwd(a, b):
    n = a.shape[0]
    row = lax.broadcasted_iota(jnp.int32, a.shape, 0)
    d = 1
    while d < n:
        if d % SUB:
            m = row >= d
            b = jnp.where(m, a * pltpu.roll(b, d, 0) + b, b)
            a = jnp.where(m, a * pltpu.roll(a, d, 0), a)
        else:
            b = a * jnp.concatenate([jnp.zeros((d,) + b.shape[1:], f32), b[:n - d]], axis=0) + b
            a = a * jnp.concatenate([jnp.ones((d,) + a.shape[1:], f32), a[:n - d]], axis=0)
        d *= 2
    return a, b


def _scan_bwd(a, b):
    n = a.shape[0]
    row = lax.broadcasted_iota(jnp.int32, a.shape, 0)
    d = 1
    while d < n:
        if d % SUB:
            m = row < n - d
            b = jnp.where(m, a * pltpu.roll(b, n - d, 0) + b, b)
            a = jnp.where(m, a * pltpu.roll(a, n - d, 0), a)
        else:
            b = a * jnp.concatenate([b[d:], jnp.zeros((d,) + b.shape[1:], f32)], axis=0) + b
            a = a * jnp.concatenate([a[d:], jnp.ones((d,) + a.shape[1:], f32)], axis=0)
        d *= 2
    return a, b


def _chunk_mats(n):
    r = lax.broadcasted_iota(jnp.int32, (n, n), 0)
    c = lax.broadcasted_iota(jnp.int32, (n, n), 1)
    same = (r // CHUNK) == (c // CHUNK)
    return same, jnp.logical_and(same, c <= r)


def _pool_w(shape):
    lane = lax.broadcasted_iota(jnp.int32, shape, 1)
    return jnp.where(lane < POOL_GROUP, 2, jnp.where(lane < 2 * POOL_GROUP, 4, jnp.where(lane < 3 * POOL_GROUP, 8, 16)))


def _pool_pick(r1, r2, r3, r4):
    lane = lax.broadcasted_iota(jnp.int32, r1.shape, 1)
    return jnp.where(lane < POOL_GROUP, r1, jnp.where(lane < 2 * POOL_GROUP, r2, jnp.where(lane < 3 * POOL_GROUP, r3, r4)))


def _const_spec(a):
    nd = a.ndim
    return pl.BlockSpec(a.shape, lambda i, _nd=nd: (0,) * _nd, pipeline_mode=pl.Buffered(1))


def _acc_spec(shape):
    nd = len(shape)
    return pl.BlockSpec(shape, lambda i, _nd=nd: (0,) * _nd)


def _params(sem="arbitrary"):
    return pltpu.CompilerParams(dimension_semantics=(sem,), vmem_limit_bytes=VMEM_LIMIT)


def _xattn_fwd(qx, khT_ref, vh_ref):
    xo = jnp.zeros((qx.shape[0], XW), f32)
    ps = []
    for h in range(XHEADS):
        s = _mm(qx, khT_ref[h]) * (XDIM ** -0.5)
        e = jnp.exp(s - jnp.max(s, axis=-1, keepdims=True))
        p = e / jnp.sum(e, axis=-1, keepdims=True)
        xo = xo + _mm(p, vh_ref[h])
        ps.append(p)
    return xo, ps


def _hgrn_parallel(q_raw, fl, lb):
    n = q_raw.shape[0]
    same, tri = _chunk_mats(n)
    sq = _sig(q_raw)
    qf = q_raw * sq
    sgm = _sig(fl)
    f = lb + (1.0 - lb) * sgm
    logf = jnp.log(f)
    k = 1.0 - f
    g = _mm_sel(tri, logf)
    gl = _mm_sel(same, logf)
    eg = jnp.exp(g)
    eng = jnp.exp(-g)
    ee = jnp.exp(gl - g)
    return dict(sq=sq, qf=qf, sgm=sgm, f=f, k=k, eg=eg, eng=eng, ee=ee, q_dec=qf * eg, k_inv=k * eng, k_end=k * ee,
                a=jnp.exp(gl))


def _hgrn_intra(q_dec, k_inv, v):
    n = q_dec.shape[0]
    _, tri = _chunk_mats(HD)
    outs = []
    for h in range(NH):
        blks = []
        for b in range(n // HD):
            rs = slice(b * HD, (b + 1) * HD)
            sc = jnp.where(tri, _mm_nt(q_dec[rs, _hs(h)], k_inv[rs, _hs(h)]), 0.0)
            blks.append(_mm(sc, v[rs, _hs(h)]))
        outs.append(jnp.concatenate(blks, axis=0))
    return jnp.concatenate(outs, axis=-1)


def _cs(c):
    return slice(c * CHUNK, (c + 1) * CHUNK)


def _hgrn_inter_fwd(qdec_s, kend_s, v_s, a_s, oint_s, st_ref, states_s, u_s):
    n = qdec_s.shape[0] // CHUNK
    for c in range(n):
        for h in range(NH):
            u_s[c, h] = _mm_tn(v_s[_cs(c), _hs(h)], kend_s[_cs(c), _hs(h)])
    for h in range(NH):
        st = st_ref[h]
        for c in range(n):
            states_s[c, h] = st
            st = st * a_s[c * CHUNK:c * CHUNK + 1, _hs(h)] + u_s[c, h]
        st_ref[h] = st
    if oint_s is None:
        return
    for c in range(n):
        for h in range(NH):
            oint_s[_cs(c), _hs(h)] = _mm_nt(qdec_s[_cs(c), _hs(h)], states_s[c, h])


def _rms(o):
    outs, rs = [], []
    for h in range(NH):
        oh = o[:, _hs(h)]
        r = lax.rsqrt(_lmean(oh * oh) + RMS_EPS)
        outs.append(oh * r)
        rs.append(r)
    return jnp.concatenate(outs, axis=-1), rs


def _gmlp_core(u_raw, v_raw, wtri_ref, bcolb_ref):
    gu, tu = _gelu(u_raw)
    gv, tv = _gelu(v_raw)
    vns, rstds, mixeds = [], [], []
    for h in range(NH):
        vn, rstd = _ln(gv[:, _hs(h)])
        blks = []
        for n in range(u_raw.shape[0] // HD):
            blks.append(_mm(wtri_ref[h], vn[n * HD:(n + 1) * HD]) + bcolb_ref[h])
        vns.append(vn)
        rstds.append(rstd)
        mixeds.append(jnp.concatenate(blks, axis=0))
    mixed = jnp.concatenate(mixeds, axis=-1)
    return gu, tu, tv, vns, rstds, mixed


def _pool_core(p, carry, row0, wbd_ref):
    ext = jnp.concatenate([carry, p], axis=0)
    r1 = ext + pltpu.roll(ext, 1, 0)
    r2 = r1 + pltpu.roll(r1, 2, 0)
    r3 = r2 + pltpu.roll(r2, 4, 0)
    r4 = r3 + pltpu.roll(r3, 8, 0)
    sel = _pool_pick(r1, r2, r3, r4)[2 * SUB:]
    grow = row0 + lax.broadcasted_iota(jnp.int32, p.shape, 0)
    inv_cnt = 1.0 / jnp.minimum(grow + 1, _pool_w(p.shape)).astype(f32)
    diff = sel * inv_cnt - p
    return diff, inv_cnt, _mm(diff, wbd_ref[...])


def _lru_core(xb, ccar, row0, p):
    ext = jnp.concatenate([ccar, xb], axis=0)
    cw = p["cw"]
    x1, x2, x3 = pltpu.roll(ext, 1, 0)[SUB:], pltpu.roll(ext, 2, 0)[SUB:], pltpu.roll(ext, 3, 0)[SUB:]
    xc = cw[3:4, :] * xb + cw[2:3, :] * x1 + cw[1:2, :] * x2 + cw[0:1, :] * x3 + p["cb"][...]
    gxs, gas = [], []
    for h in range(NH):
        gxs.append(_mm(xc[:, _hs(h)], p["wgx"][h]))
        gas.append(_mm(xc[:, _hs(h)], p["wga"][h]))
    gx = _sig(jnp.concatenate(gxs, axis=-1) + p["bgx"][...])
    ga = _sig(jnp.concatenate(gas, axis=-1) + p["bga"][...])
    sp = _softplus(-p["ap"][...])
    la = -LRU_C * ga * sp
    a = jnp.exp(la)
    grow = row0 + lax.broadcasted_iota(jnp.int32, xb.shape, 0)
    first = grow == 0
    mult = jnp.where(first, 1.0, jnp.sqrt(-_expm1(2.0 * la)))
    bt = mult * gx * xc
    return dict(x1=x1, x2=x2, x3=x3, xc=xc, gx=gx, ga=ga, sp=sp, a=a, mult=mult, bt=bt, first=first)


def _fwd_layer(kind, xin, w_in, w_out, lng, lnb, khT, vh, prm, tgt):
    S = xin.shape[0]
    TS = _TS[kind]
    nt = S // TS
    off = _OFFS[kind]
    W = off["W"]
    last = tgt is not None
    pnames = _PRM[kind]
    pvals = [prm[n] for n in pnames]

    def body(*refs):
        it = iter(refs)
        xin_ref, win_ref, wout_ref, lng_ref, lnb_ref, khT_ref, vh_ref = (next(it) for _ in range(7))
        p = {n: next(it) for n in pnames}
        tgt_ref = next(it) if last else None
        xout_ref, proj_ref, z_ref = next(it), next(it), next(it)
        loss_ref = next(it) if last else None
        rest = list(it)
        i = pl.program_id(0)
        x = xin_ref[...]
        proj_ref[...] = _mm(x, win_ref[...])

        if kind == 0:
            gu, _, _, _, _, mixed = _gmlp_core(proj_ref[:, 0:TOK], proj_ref[:, TOK:2 * TOK], p["wtri"], p["bcolb"])
            tok = gu * mixed
        elif kind == 1:
            st_save, o_save, st_ref, states_s, u_s, qdec_s, kend_s, v_s, a_s, oint_s = rest

            @pl.when(i == 0)
            def _():
                st_ref[...] = jnp.zeros_like(st_ref)

            st_save[0, 0] = st_ref[...]
            v = proj_ref[:, 2 * TOK:3 * TOK]
            hp = _hgrn_parallel(proj_ref[:, 0:TOK], proj_ref[:, TOK:2 * TOK], p["lb"][...])
            qdec_s[...] = hp["q_dec"]
            kend_s[...] = hp["k_end"]
            v_s[...] = v
            a_s[...] = hp["a"]
            o_intra = _hgrn_intra(hp["q_dec"], hp["k_inv"], v)
            _hgrn_inter_fwd(qdec_s, kend_s, v_s, a_s, oint_s, st_ref, states_s, u_s)
            o = o_intra + oint_s[...]
            o_save[0] = o
            for sub in range(1, TS // HGRN_SUB):
                st_save[0, sub] = states_s[sub * HGRN_SUB // CHUNK]
            on, _ = _rms(o)
            tok = on * p["ng"][...]
        elif kind == 2:
            pc_save, pcar = rest

            @pl.when(i == 0)
            def _():
                pcar[...] = jnp.zeros_like(pcar)

            pc_save[0] = pcar[...]
            pp = proj_ref[:, 0:TOK]
            _, _, y = _pool_core(pp, pcar[...], i * TS, p["wbd"])
            pcar[...] = pp[TS - 2 * SUB:, :]
            tok = y * p["scale"][...]
        else:
            cc_save, hc_save, h_save, ccar, hcar = rest

            @pl.when(i == 0)
            def _():
                ccar[...] = jnp.zeros_like(ccar)
                hcar[...] = jnp.zeros_like(hcar)

            cc_save[0] = ccar[...]
            hc_save[0] = hcar[...]
            xb = proj_ref[:, 0:TOK]
            lc = _lru_core(xb, ccar[...], i * TS, p)
            P, B = _scan_fwd(lc["a"], lc["bt"])
            tok = P * hcar[SUB - 1:SUB, :] + B
            h_save[0] = tok
            ccar[...] = xb[TS - SUB:, :]
            hcar[...] = tok[TS - SUB:, :]

        xo, _ = _xattn_fwd(proj_ref[:, off["qx"]:off["qx"] + XW], khT_ref, vh_ref)
        gate = proj_ref[:, off["gate"]:off["gate"] + D_MODEL]
        mixed = jnp.concatenate([tok, xo], axis=-1) * (gate * _sig(gate))
        z = ALPHA * x + _mm(mixed, wout_ref[...])
        z_ref[...] = z
        xhat, _ = _ln(z)
        xout = xhat * lng_ref[...] + lnb_ref[...]
        if last:
            e = xout - tgt_ref[...]
            xout_ref[...] = e * (1.0 / D_MODEL)
            es = _rowsum(e * e)
            tot = es[:, 0:LANE]
            for j in range(1, D_MODEL // LANE):
                tot = tot + es[:, j * LANE:(j + 1) * LANE]

            @pl.when(i == 0)
            def _():
                loss_ref[...] = jnp.zeros_like(loss_ref)

            loss_ref[0:1, :] += tot
        else:
            xout_ref[...] = xout

    tile = lambda w: pl.BlockSpec((TS, w), lambda i: (i, 0))
    in_arrays = [xin, w_in, w_out, lng, lnb, khT, vh] + pvals + ([tgt] if last else [])
    in_specs = [tile(D_MODEL)] + [_const_spec(a) for a in in_arrays[1:7 + len(pvals)]] + ([tile(D_MODEL)] if last else [])
    out_shape = [jax.ShapeDtypeStruct((S, D_MODEL), f32), jax.ShapeDtypeStruct((S, W), f32), jax.ShapeDtypeStruct((S, D_MODEL), f32)]
    out_specs = [tile(D_MODEL), tile(W), tile(D_MODEL)]
    if last:
        out_shape.append(jax.ShapeDtypeStruct((SUB, LANE), f32))
        out_specs.append(_acc_spec((SUB, LANE)))
    scratch = []
    save = lambda *s: (jax.ShapeDtypeStruct((nt,) + s, f32), pl.BlockSpec((1,) + s, lambda i, _n=len(s): (i,) + (0,) * _n))
    if kind == 1:
        saved = [save(TS // HGRN_SUB, NH, HD, HD), save(TS, TOK)]
        scratch = ([pltpu.VMEM((NH, HD, HD), f32)] + [pltpu.VMEM((TS // CHUNK, NH, HD, HD), f32)] * 2
                   + [pltpu.VMEM((TS, TOK), f32)] * 5)
    elif kind == 2:
        saved = [save(2 * SUB, TOK)]
        scratch = [pltpu.VMEM((2 * SUB, TOK), f32)]
    elif kind == 3:
        saved = [save(SUB, TOK), save(SUB, TOK), save(TS, TOK)]
        scratch = [pltpu.VMEM((SUB, TOK), f32)] * 2
    else:
        saved = []
    for sh, sp in saved:
        out_shape.append(sh)
        out_specs.append(sp)
    return pl.pallas_call(body, name=f"fwd_layer{kind}", grid=(nt,), in_specs=in_specs, out_specs=out_specs,
                          out_shape=out_shape, scratch_shapes=scratch, compiler_params=_params())(*in_arrays)


def _small_grad_shapes(kind):
    if kind == 0:
        return dict(dwtri=(NH, HD, HD), dbacc=(NH, HD, HD))
    if kind == 1:
        return dict(dlb=(SUB, TOK), dng=(SUB, TOK))
    if kind == 2:
        return dict(dwbd=(TOK, TOK), dscale=(SUB, TOK))
    return dict(dcw=(SUB, TOK), dvec=(SUB, TOK), dwgx=(NH, HD, HD), dwga=(NH, HD, HD))


def _bwd_layer(kind, dxout, z, proj, w_inT, w_outT, lng, kh, khT, vh, vhT, prm, saves):
    S = dxout.shape[0]
    TS = _TS[kind]
    nt = S // TS
    off = _OFFS[kind]
    W = off["W"]
    pnames = _PRM[kind]
    pvals = [prm[n] for n in pnames]
    sg_shapes = _small_grad_shapes(kind)
    sg_names = list(sg_shapes)
    n_saves = len(saves)

    def body(*refs):
        it = iter(refs)
        dxo_ref, z_ref, proj_ref, winT_ref, woutT_ref, lng_ref, kh_ref, khT_ref, vh_ref, vhT_ref = (next(it) for _ in range(10))
        p = {n: next(it) for n in pnames}
        sv = [next(it) for _ in range(n_saves)]
        dxin_ref, dproj_ref, mixed_ref, dy_ref, dln_ref, dk_ref, dv_ref = (next(it) for _ in range(7))
        sg = {n: next(it) for n in sg_names}
        rest = list(it)
        step = pl.program_id(0)
        i = nt - 1 - step

        @pl.when(step == 0)
        def _():
            dln_ref[...] = jnp.zeros_like(dln_ref)
            dk_ref[...] = jnp.zeros_like(dk_ref)
            dv_ref[...] = jnp.zeros_like(dv_ref)
            for n in sg_names:
                sg[n][...] = jnp.zeros_like(sg[n])

        dxo = dxo_ref[...]
        xhat, rstd = _ln(z_ref[...])
        dln_ref[0:1, :] += _rowsum(dxo * xhat)
        dln_ref[1:2, :] += _rowsum(dxo)
        dz = _ln_bwd(dxo * lng_ref[...], xhat, rstd)
        dyb = dz.astype(bf16)
        dy_ref[...] = dyb
        dmixed = _mm(dyb, woutT_ref[...])

        aux = {}
        if kind == 0:
            u_raw, v_raw = proj_ref[:, 0:TOK], proj_ref[:, TOK:2 * TOK]
            gu, tu, tv, vns, rstds, mx = _gmlp_core(u_raw, v_raw, p["wtri"], p["bcolb"])
            tok = gu * mx
        elif kind == 1:
            st_save, o_save = sv
            (dst_ref, fst_ref, states_s, dsts_s, u_s, qdec_s, kend_s, v_s, a_s, do_s, dqdec_s, dkend_s, dv_s,
             dgl_s) = rest

            @pl.when(step == 0)
            def _():
                dst_ref[...] = jnp.zeros_like(dst_ref)

            o = o_save[0]
            on, rs = _rms(o)
            tok = on * p["ng"][...]
            aux = dict(o=o, on=on, rs=rs)
        elif kind == 2:
            pc_save, = sv
            dpcar, = rest
            pp = proj_ref[:, 0:TOK]
            diff, inv_cnt, y = _pool_core(pp, pc_save[0], i * TS, p["wbd"])
            tok = y * p["scale"][...]
        else:
            cc_save, hc_save, h_save = sv
            dccar, gcar = rest
            xb = proj_ref[:, 0:TOK]
            lc = _lru_core(xb, cc_save[0], i * TS, p)
            hin = hc_save[0, SUB - 1:SUB, :]
            tok = h_save[0]

        xo, ps = _xattn_fwd(proj_ref[:, off["qx"]:off["qx"] + XW], khT_ref, vh_ref)
        gate = proj_ref[:, off["gate"]:off["gate"] + D_MODEL]
        sgm = _sig(gate)
        sgate = gate * sgm
        cat = jnp.concatenate([tok, xo], axis=-1)
        mixed_ref[...] = (cat * sgate).astype(bf16)
        dcat = dmixed * sgate
        dproj_ref[:, off["gate"]:off["gate"] + D_MODEL] = (dmixed * cat * (sgm * (1.0 + gate * (1.0 - sgm)))).astype(bf16)
        dtok = dcat[:, 0:TOK]
        dxo_att = dcat[:, TOK:]

        qx = proj_ref[:, off["qx"]:off["qx"] + XW]
        dqx = jnp.zeros((TS, XW), f32)
        for h in range(XHEADS):
            dp = _mm(dxo_att, vhT_ref[h])
            ds = ps[h] * (dp - jnp.sum(dp * ps[h], axis=-1, keepdims=True)) * (XDIM ** -0.5)
            dqx = dqx + _mm(ds, kh_ref[h])
            dk_ref[h] += _mm_tn(ds, qx)
            dv_ref[h] += _mm_tn(ps[h], dxo_att)
        dproj_ref[:, off["qx"]:off["qx"] + XW] = dqx.astype(bf16)

        if kind == 0:
            tril = lax.broadcasted_iota(jnp.int32, (HD, HD), 1) <= lax.broadcasted_iota(jnp.int32, (HD, HD), 0)
            dgu = dtok * mx
            dmx = dtok * gu
            dgvs = []
            for h in range(NH):
                dmh = dmx[:, _hs(h)]
                blks = []
                for n in range(TS // HD):
                    rs_ = slice(n * HD, (n + 1) * HD)
                    blks.append(_mm(p["wtriT"][h], dmh[rs_]))
                    sg["dwtri"][h] += jnp.where(tril, _mm_nt(dmh[rs_], vns[h][rs_]), 0.0)
                    sg["dbacc"][h] += dmh[rs_]
                dgvs.append(_ln_bwd(jnp.concatenate(blks, axis=0), vns[h], rstds[h]))
            dgv = jnp.concatenate(dgvs, axis=-1)
            dproj_ref[:, 0:TOK] = (dgu * _gelu_grad(u_raw, tu)).astype(bf16)
            dproj_ref[:, TOK:2 * TOK] = (dgv * _gelu_grad(v_raw, tv)).astype(bf16)
        elif kind == 1:
            o, on, rs = aux["o"], aux["on"], aux["rs"]
            ng = p["ng"][...]
            lb = p["lb"][...]
            sg["dng"][0:1, :] += _rowsum(dtok * on)
            dn = dtok * ng
            dos = []
            for h in range(NH):
                oh, r = o[:, _hs(h)], rs[h]
                dos.append(r * (dn[:, _hs(h)] - oh * (r * r) * _lmean(dn[:, _hs(h)] * oh)))
            do_all = jnp.concatenate(dos, axis=-1)
            _, tri = _chunk_mats(HD)
            same, _ = _chunk_mats(HGRN_SUB)
            triT = jnp.logical_and(same, lax.broadcasted_iota(jnp.int32, (HGRN_SUB, HGRN_SUB), 1)
                                   >= lax.broadcasted_iota(jnp.int32, (HGRN_SUB, HGRN_SUB), 0))
            row16 = lax.broadcasted_iota(jnp.int32, (CHUNK, HD), 0)
            nch = HGRN_SUB // CHUNK
            for sub in reversed(range(TS // HGRN_SUB)):
                rr = slice(sub * HGRN_SUB, (sub + 1) * HGRN_SUB)
                q_raw, v = proj_ref[rr, 0:TOK], proj_ref[rr, 2 * TOK:3 * TOK]
                hp = _hgrn_parallel(q_raw, proj_ref[rr, TOK:2 * TOK], lb)
                qdec_s[...] = hp["q_dec"]
                kend_s[...] = hp["k_end"]
                v_s[...] = v
                a_s[...] = hp["a"]
                fst_ref[...] = st_save[0, sub]
                _hgrn_inter_fwd(qdec_s, kend_s, v_s, a_s, None, fst_ref, states_s, u_s)
                do = do_all[rr]
                do_s[...] = do
                dqd, dki, dvi = [], [], []
                for h in range(NH):
                    bq, bk, bv = [], [], []
                    for b in range(HGRN_SUB // HD):
                        rs_ = slice(b * HD, (b + 1) * HD)
                        qd, ki = hp["q_dec"][rs_, _hs(h)], hp["k_inv"][rs_, _hs(h)]
                        sc = jnp.where(tri, _mm_nt(qd, ki), 0.0)
                        dsc = jnp.where(tri, _mm_nt(do[rs_, _hs(h)], v[rs_, _hs(h)]), 0.0)
                        bv.append(_mm_tn(sc, do[rs_, _hs(h)]))
                        bq.append(_mm(dsc, ki))
                        bk.append(_mm_tn(dsc, qd))
                    dqd.append(jnp.concatenate(bq, axis=0))
                    dki.append(jnp.concatenate(bk, axis=0))
                    dvi.append(jnp.concatenate(bv, axis=0))
                dqdec_s[...] = jnp.concatenate(dqd, axis=-1)
                dk_inv = jnp.concatenate(dki, axis=-1)
                dv_s[...] = jnp.concatenate(dvi, axis=-1)
                for c in range(nch):
                    for h in range(NH):
                        u_s[c, h] = _mm_tn(do_s[_cs(c), _hs(h)], qdec_s[_cs(c), _hs(h)])
                for h in range(NH):
                    dst = dst_ref[h]
                    for c in reversed(range(nch)):
                        dsts_s[c, h] = dst
                        dst = dst * a_s[c * CHUNK:c * CHUNK + 1, _hs(h)] + u_s[c, h]
                    dst_ref[h] = dst
                for c in range(nch):
                    for h in range(NH):
                        stp = states_s[c, h]
                        dst = dsts_s[c, h]
                        dqdec_s[_cs(c), _hs(h)] += _mm(do_s[_cs(c), _hs(h)], stp)
                        dkend_s[_cs(c), _hs(h)] = _mm(v_s[_cs(c), _hs(h)], dst)
                        dv_s[_cs(c), _hs(h)] += _mm_nt(kend_s[_cs(c), _hs(h)], dst)
                        da = jnp.sum(dst * stp, axis=0, keepdims=True) * a_s[c * CHUNK:c * CHUNK + 1, _hs(h)]
                        dgl_s[_cs(c), _hs(h)] = jnp.where(row16 == 0, jnp.broadcast_to(da, (CHUNK, HD)), 0.0)
                dq_dec = dqdec_s[...]
                dk_end = dkend_s[...]
                dg = dq_dec * hp["q_dec"] - dk_inv * hp["k_inv"] - dk_end * hp["k_end"]
                dk = dk_inv * hp["eng"] + dk_end * hp["ee"]
                dglr = dk_end * hp["k_end"] + dgl_s[...]
                dlogf = _mm_sel(triT, dg) + _mm_sel(same, dglr)
                df = dlogf / hp["f"] - dk
                sg["dlb"][0:1, :] += _rowsum(df * (1.0 - hp["sgm"]))
                dproj_ref[rr, 0:TOK] = (dq_dec * hp["eg"] * (hp["sq"] * (1.0 + q_raw * (1.0 - hp["sq"])))).astype(bf16)
                dproj_ref[rr, TOK:2 * TOK] = (df * (1.0 - lb) * hp["sgm"] * (1.0 - hp["sgm"])).astype(bf16)
                dproj_ref[rr, 2 * TOK:3 * TOK] = dv_s[...].astype(bf16)
        elif kind == 2:
            @pl.when(step == 0)
            def _():
                dpcar[...] = jnp.zeros_like(dpcar)

            sg["dscale"][0:1, :] += _rowsum(dtok * y)
            dyp = dtok * p["scale"][...]
            sg["dwbd"][...] += _mm_tn(diff, dyp)
            ddiff = _mm(dyp, p["wbdT"][...])
            q = ddiff * inv_cnt
            ext = jnp.concatenate([q, dpcar[...]], axis=0)
            n = TS + 2 * SUB
            r1 = ext + pltpu.roll(ext, n - 1, 0)
            r2 = r1 + pltpu.roll(r1, n - 2, 0)
            r3 = r2 + pltpu.roll(r2, n - 4, 0)
            r4 = r3 + pltpu.roll(r3, n - 8, 0)
            dproj_ref[:, 0:TOK] = (_pool_pick(r1, r2, r3, r4)[:TS] - ddiff).astype(bf16)
            dpcar[...] = q[0:2 * SUB, :]
        else:
            @pl.when(step == 0)
            def _():
                dccar[...] = jnp.zeros_like(dccar)
                gcar[...] = jnp.zeros_like(gcar)

            a, mult, gx, ga, xc = lc["a"], lc["mult"], lc["gx"], lc["ga"], lc["xc"]
            row = lax.broadcasted_iota(jnp.int32, (TS, TOK), 0)
            an = jnp.where(row == TS - 1, 1.0, pltpu.roll(a, TS - 1, 0))
            Pb, Bb = _scan_bwd(an, dtok)
            lam = Pb * gcar[0:1, :] + Bb
            gcar[...] = (a * lam)[0:SUB, :]
            hprev = jnp.where(row == 0, jnp.broadcast_to(hin, (TS, TOK)), pltpu.roll(tok, 1, 0))
            dmult = lam * gx * xc
            dgx = lam * mult * xc
            dxc = lam * mult * gx
            dla = lam * hprev * a - jnp.where(lc["first"], 0.0, dmult * a * a / mult)
            sp = lc["sp"]
            dga = -LRU_C * sp * dla
            dsp = _rowsum(-LRU_C * ga * dla)
            sg["dvec"][0:1, :] += dsp * (-_sig(-p["ap"][...]))
            dpx = dgx * gx * (1.0 - gx)
            dpa = dga * ga * (1.0 - ga)
            sg["dvec"][1:2, :] += _rowsum(dpx)
            sg["dvec"][2:3, :] += _rowsum(dpa)
            dxcs = []
            for h in range(NH):
                dxcs.append(_mm(dpx[:, _hs(h)], p["wgxT"][h]) + _mm(dpa[:, _hs(h)], p["wgaT"][h]))
                sg["dwgx"][h] += _mm_tn(xc[:, _hs(h)], dpx[:, _hs(h)])
                sg["dwga"][h] += _mm_tn(xc[:, _hs(h)], dpa[:, _hs(h)])
            dxc = dxc + jnp.concatenate(dxcs, axis=-1)
            sg["dvec"][3:4, :] += _rowsum(dxc)
            sg["dcw"][3:4, :] += _rowsum(dxc * xb)
            sg["dcw"][2:3, :] += _rowsum(dxc * lc["x1"])
            sg["dcw"][1:2, :] += _rowsum(dxc * lc["x2"])
            sg["dcw"][0:1, :] += _rowsum(dxc * lc["x3"])
            ext = jnp.concatenate([dxc, dccar[...]], axis=0)
            n = TS + SUB
            cw = p["cw"]
            dproj_ref[:, 0:TOK] = (cw[3:4, :] * dxc + cw[2:3, :] * pltpu.roll(ext, n - 1, 0)[:TS]
                                   + cw[1:2, :] * pltpu.roll(ext, n - 2, 0)[:TS]
                                   + cw[0:1, :] * pltpu.roll(ext, n - 3, 0)[:TS]).astype(bf16)
            dccar[...] = dxc[0:SUB, :]

        dxin_ref[...] = ALPHA * dz + _mm(dproj_ref[...], winT_ref[...])

    rtile = lambda w: pl.BlockSpec((TS, w), lambda s: (nt - 1 - s, 0))
    consts = [w_inT, w_outT, lng, kh, khT, vh, vhT] + pvals
    in_arrays = [dxout, z, proj] + consts + list(saves)
    in_specs = [rtile(D_MODEL), rtile(D_MODEL), rtile(W)] + [_const_spec(a) for a in consts]
    for a in saves:
        in_specs.append(pl.BlockSpec((1,) + a.shape[1:], lambda s, _n=a.ndim - 1: (nt - 1 - s,) + (0,) * _n))
    out_shape = [jax.ShapeDtypeStruct((S, D_MODEL), f32), jax.ShapeDtypeStruct((S, W), bf16),
                 jax.ShapeDtypeStruct((S, D_MODEL), bf16), jax.ShapeDtypeStruct((S, D_MODEL), bf16),
                 jax.ShapeDtypeStruct((SUB, D_MODEL), f32), jax.ShapeDtypeStruct((XHEADS, XW, XW), f32),
                 jax.ShapeDtypeStruct((XHEADS, XW, XW), f32)]
    out_specs = [rtile(D_MODEL), rtile(W), rtile(D_MODEL), rtile(D_MODEL), _acc_spec((SUB, D_MODEL)),
                 _acc_spec((XHEADS, XW, XW)), _acc_spec((XHEADS, XW, XW))]
    for n in sg_names:
        out_shape.append(jax.ShapeDtypeStruct(sg_shapes[n], f32))
        out_specs.append(_acc_spec(sg_shapes[n]))
    if kind == 1:
        scratch = ([pltpu.VMEM((NH, HD, HD), f32)] * 2 + [pltpu.VMEM((HGRN_SUB // CHUNK, NH, HD, HD), f32)] * 3
                   + [pltpu.VMEM((HGRN_SUB, TOK), f32)] * 9)
    elif kind == 2:
        scratch = [pltpu.VMEM((2 * SUB, TOK), f32)]
    elif kind == 3:
        scratch = [pltpu.VMEM((SUB, TOK), f32)] * 2
    else:
        scratch = []
    outs = pl.pallas_call(body, name=f"bwd_layer{kind}", grid=(nt,), in_specs=in_specs, out_specs=out_specs,
                          out_shape=out_shape, scratch_shapes=scratch, compiler_params=_params())(*in_arrays)
    return outs[:7], dict(zip(sg_names, outs[7:]))


def _prep(mem, w_kv, logits):
    def body(mem_ref, w_ref, lg_ref, kh_ref, khT_ref, vh_ref, vhT_ref, p_ref):
        kv = _mm(mem_ref[...], w_ref[...])
        k, v = kv[:, 0:XW], kv[:, XW:]
        kT, vT = k.T, v.T
        col = lax.broadcasted_iota(jnp.int32, (XW, XW), 1) // XDIM
        row = lax.broadcasted_iota(jnp.int32, (XW, XW), 0) // XDIM
        for h in range(XHEADS):
            kh_ref[h] = jnp.where(col == h, k, 0.0).astype(bf16)
            vh_ref[h] = jnp.where(col == h, v, 0.0).astype(bf16)
            khT_ref[h] = jnp.where(row == h, kT, 0.0).astype(bf16)
            vhT_ref[h] = jnp.where(row == h, vT, 0.0).astype(bf16)
        lg = lg_ref[...]
        e = jnp.exp(lg - jnp.max(lg, axis=0, keepdims=True))
        p_ref[...] = e / jnp.sum(e, axis=0, keepdims=True)

    vm = pl.BlockSpec(memory_space=pltpu.VMEM)
    hs = jax.ShapeDtypeStruct((XHEADS, XW, XW), bf16)
    return pl.pallas_call(body, name="prep_memory", in_specs=[vm] * 3, out_specs=[vm] * 5,
                          out_shape=[hs, hs, hs, hs, jax.ShapeDtypeStruct(logits.shape, f32)])(mem, w_kv, logits)


def _kv_bwd(mem, dks, dvs):
    def body(mem_ref, *refs):
        out_ref = refs[-1]
        col = lax.broadcasted_iota(jnp.int32, (XW, XW), 1) // XDIM
        dk = jnp.zeros((XW, XW), f32)
        dv = jnp.zeros((XW, XW), f32)
        for l in range(DEPTH):
            for h in range(XHEADS):
                dk = dk + jnp.where(col == h, refs[l][h], 0.0)
                dv = dv + jnp.where(col == h, refs[DEPTH + l][h], 0.0)
        out_ref[:, 0:XW] = _mm_tn(mem_ref[...], dk)
        out_ref[:, XW:] = _mm_tn(mem_ref[...], dv)

    vm = pl.BlockSpec(memory_space=pltpu.VMEM)
    return pl.pallas_call(body, name="kv_bwd", in_specs=[vm] * (1 + 2 * DEPTH), out_specs=vm,
                          out_shape=jax.ShapeDtypeStruct((D_MODEL, 2 * XW), f32))(mem, *dks, *dvs)


def _tn_gemm(a, b, name, nb):
    S, M = a.shape
    N = b.shape[1]
    NB = N // nb
    nk = S // TK

    def body(a_ref, b_ref, o_ref):
        @pl.when(pl.program_id(1) == 0)
        def _():
            o_ref[...] = jnp.zeros_like(o_ref)

        o_ref[...] += _mm_tn(a_ref[...], b_ref[...])

    return pl.pallas_call(body, name=name, grid=(nb, nk),
                          in_specs=[pl.BlockSpec((TK, M), lambda j, k: (k, 0)), pl.BlockSpec((TK, NB), lambda j, k: (k, j))],
                          out_specs=pl.BlockSpec((M, NB), lambda j, k: (0, j)),
                          out_shape=jax.ShapeDtypeStruct((M, N), f32),
                          compiler_params=pltpu.CompilerParams(dimension_semantics=("parallel", "arbitrary"),
                                                               vmem_limit_bytes=VMEM_LIMIT))(a, b)


def _rows_block(R, mult=16, cap=1024):
    best = R
    for d in range(mult, min(R, cap) + 1, mult):
        if R % d == 0:
            best = d
    return best


def _tn_gemm_sharded(a, b, name):
    S, M = a.shape
    Wq = b.shape[1] // 4
    nk = S // TK

    def body(a_ref, b_ref, o_ref):
        @pl.when(pl.program_id(0) == 0)
        def _():
            o_ref[...] = jnp.zeros_like(o_ref)

        at = a_ref[...].astype(MM)
        for j in range(4):
            o_ref[j] += _mm_tn(at, b_ref[:, j * Wq:(j + 1) * Wq])

    return pl.pallas_call(body, name=name, grid=(nk,),
                          in_specs=[pl.BlockSpec((TK, M), lambda k: (k, 0)), pl.BlockSpec((TK, 4 * Wq), lambda k: (k, 0))],
                          out_specs=pl.BlockSpec((4, M, Wq), lambda k: (0, 0, 0)),
                          out_shape=jax.ShapeDtypeStruct((4, M, Wq), f32), compiler_params=_params())(a, b)


def _tn_gemm_slab(a, b, acc, l, name):
    S, M = a.shape
    N = b.shape[1]
    nk = S // TK

    def body(a_ref, b_ref, *refs):
        o_ref = refs[-1]

        @pl.when(pl.program_id(0) == 0)
        def _():
            o_ref[...] = jnp.zeros_like(o_ref)

        o_ref[...] += _mm_tn(a_ref[...], b_ref[...])

    ins = [a, b] + ([] if acc is None else [acc])
    in_specs = [pl.BlockSpec((TK, M), lambda k: (k, 0)), pl.BlockSpec((TK, N), lambda k: (k, 0))]
    if acc is not None:
        in_specs.append(pl.BlockSpec(memory_space=pl.ANY))
    return pl.pallas_call(body, name=name, grid=(nk,), in_specs=in_specs,
                          out_specs=pl.BlockSpec((None, M, N), lambda k: (l, 0, 0)),
                          out_shape=jax.ShapeDtypeStruct((DEPTH, M, N), f32),
                          input_output_aliases={} if acc is None else {2: 0},
                          compiler_params=_params())(*ins)


HALF_ROWS = D_MODEL // 2
SHARD_ROWS = D_MODEL // 4


def _half_of_full(ref, kind, h):
    if kind == "kv":
        return ref.at[:, pl.ds(h * XW, XW)]
    if kind == "wout":
        return ref.at[pl.ds(2 * h, 2)]
    return ref.at[:, pl.ds(h * HALF_ROWS, HALF_ROWS)]


def _shard_of_half(ref, kind, j):
    if kind == "kv":
        return ref.at[pl.ds(j * SHARD_ROWS, SHARD_ROWS)]
    if kind == "wout":
        return ref.at[:, pl.ds(j * SHARD_ROWS, SHARD_ROWS)]
    return ref.at[j]


def _half_of_shard(ref, kind, h):
    if kind == "kv":
        return ref.at[:, pl.ds(h * XW, XW)]
    if kind == "wout":
        return ref.at[pl.ds(2 * h, 2)]
    return ref.at[pl.ds(h * HALF_ROWS, HALF_ROWS)]


def _half_shape(full_shape, kind):
    if kind == "kv":
        return (full_shape[0], XW)
    if kind == "wout":
        return (2,) + tuple(full_shape[1:])
    return (4, HALF_ROWS, full_shape[2])


def _shard_half_shape(full_shape, kind):
    if kind == "kv":
        return (SHARD_ROWS, XW)
    if kind == "wout":
        return (2, SHARD_ROWS, full_shape[2])
    return (HALF_ROWS, full_shape[2])


def _shard_shape(full_shape, kind):
    if kind == "kv":
        return (SHARD_ROWS, full_shape[1])
    if kind == "wout":
        return (DEPTH, SHARD_ROWS, full_shape[2])
    return (D_MODEL, full_shape[2])


def _ew_call(body, name, grid, jc, ins, in_specs, out_shape, out_specs):
    gs = pltpu.PrefetchScalarGridSpec(num_scalar_prefetch=1, grid=grid, in_specs=in_specs, out_specs=out_specs)
    return pl.pallas_call(body, name=name, grid_spec=gs, out_shape=out_shape,
                          compiler_params=pltpu.CompilerParams(dimension_semantics=("parallel",) * len(grid),
                                                               vmem_limit_bytes=VMEM_LIMIT))(jc, *ins)


def _add_sibling(part, got, kind, jc, name):
    def body(jc_ref, a_ref, b_ref, o_ref, ob_ref):
        s = a_ref[...] + b_ref[...]
        o_ref[...] = s
        ob_ref[...] = s.astype(bf16)

    if kind == "kv":
        R = part.shape[0]
        grid = (2,)
        mine = pl.BlockSpec((R // 2, XW), lambda i, jc_ref: (i, jc_ref[1]))
        spec = pl.BlockSpec((R // 2, XW), lambda i, jc_ref: (i, 0))
    elif kind == "wout":
        _, R, C = part.shape
        grid = (2, 2)
        mine = pl.BlockSpec((None, R // 2, C), lambda s, i, jc_ref: (2 * jc_ref[1] + s, i, 0))
        spec = pl.BlockSpec((None, R // 2, C), lambda s, i, jc_ref: (s, i, 0))
    else:
        C = part.shape[2]
        grid = (4, 2)
        mine = pl.BlockSpec((None, HALF_ROWS // 2, C), lambda s, i, jc_ref: (s, 2 * jc_ref[1] + i, 0))
        spec = pl.BlockSpec((None, HALF_ROWS // 2, C), lambda s, i, jc_ref: (s, i, 0))
    hs = _half_shape(part.shape, kind)
    return _ew_call(body, name, grid, jc, [part, got], [mine, spec],
                    [jax.ShapeDtypeStruct(hs, f32), jax.ShapeDtypeStruct(hs, bf16)], [spec, spec])


def _add_chips(q32, r, kind, jc, name):
    def body(jc_ref, q_ref, r_ref, out_ref):
        out_ref[...] = ((q_ref[...] + r_ref[0].astype(f32)) + r_ref[1].astype(f32)) + r_ref[2].astype(f32)

    if kind == "kv":
        grid = (1,)
        qs = pl.BlockSpec((SHARD_ROWS, XW), lambda i, jc_ref: (jc_ref[0], 0))
        rs = pl.BlockSpec((3, SHARD_ROWS, XW), lambda i, jc_ref: (0, 0, 0))
        os_ = pl.BlockSpec((SHARD_ROWS, XW), lambda i, jc_ref: (0, jc_ref[1]))
    elif kind == "wout":
        C = q32.shape[2]
        grid = (2,)
        qs = pl.BlockSpec((None, SHARD_ROWS, C), lambda s, jc_ref: (s, jc_ref[0], 0))
        rs = pl.BlockSpec((3, None, SHARD_ROWS, C), lambda s, jc_ref: (0, s, 0, 0))
        os_ = pl.BlockSpec((None, SHARD_ROWS, C), lambda s, jc_ref: (2 * jc_ref[1] + s, 0, 0))
    else:
        C = q32.shape[2]
        grid = (2,)
        qs = pl.BlockSpec((None, HALF_ROWS // 2, C), lambda i, jc_ref: (jc_ref[0], i, 0))
        rs = pl.BlockSpec((3, HALF_ROWS // 2, C), lambda i, jc_ref: (0, i, 0))
        os_ = pl.BlockSpec((HALF_ROWS // 2, C), lambda i, jc_ref: (2 * jc_ref[1] + i, 0))
    if kind == "kv":
        full_shape = (D_MODEL, 2 * XW)
    elif kind == "wout":
        full_shape = (DEPTH, D_MODEL, D_MODEL)
    else:
        full_shape = (4, D_MODEL, q32.shape[2])
    return _ew_call(body, name, grid, jc, [q32, r], [qs, rs], jax.ShapeDtypeStruct(_shard_shape(full_shape, kind), f32), os_)


def _adamw(w, g, m, v, name):
    R, C = w.shape
    br = _rows_block(R, mult=SUB, cap=512)
    c1 =1.0 / (1.0 - ADAM_B1 ** ADAM_STEP)
    c2 = 1.0 / (1.0 - ADAM_B2 ** ADAM_STEP)

    def body(w_ref, g_ref, m_ref, v_ref, d_ref, nm_ref, nv_ref):
        g_ = g_ref[...]
        nm = ADAM_B1 * m_ref[...] + (1.0 - ADAM_B1) * g_
        nv = ADAM_B2 * v_ref[...] + (1.0 - ADAM_B2) * (g_ * g_)
        nm_ref[...] = nm
        nv_ref[...] = nv
        d_ref[...] = -ADAM_LR * ((nm * c1) / (jnp.sqrt(nv * c2) + ADAM_EPS) + ADAM_WD * w_ref[...])

    spec = pl.BlockSpec((br, C), lambda i: (i, 0))
    sh = jax.ShapeDtypeStruct((R, C), f32)
    return pl.pallas_call(body, name=name, grid=(R // br,), in_specs=[spec] * 4, out_specs=[spec] * 3,
                          out_shape=[sh, sh, sh], compiler_params=_params("parallel"))(w, g, m, v)


def _small_finish(dbacc, p_soft, dlb):
    def body(db_ref, p_ref, dlb_ref, dbs_ref, dlg_ref):
        lane = lax.broadcasted_iota(jnp.int32, (HD, HD), 1)
        acc = jnp.zeros((HD, HD), f32)
        for h in range(NH):
            acc = acc + jnp.where(lane == h, jnp.sum(db_ref[h], axis=-1, keepdims=True), 0.0)
        dbs_ref[...] = acc
        p = p_ref[...]
        p1 = p[1:2, :]
        rowi = lax.broadcasted_iota(jnp.int32, p.shape, 0)
        dlg_ref[...] = dlb_ref[0:1, :] * p1 * (jnp.where(rowi == 1, 1.0, 0.0) - p)

    vm = pl.BlockSpec(memory_space=pltpu.VMEM)
    return pl.pallas_call(body, name="small_finish", in_specs=[vm] * 3, out_specs=[vm] * 2,
                          out_shape=[jax.ShapeDtypeStruct((HD, HD), f32), jax.ShapeDtypeStruct(p_soft.shape, f32)])(dbacc, p_soft, dlb)


def _where_am_i():
    return lax.axis_index("x"), lax.axis_index("y"), lax.axis_index("c")


MAX_PIECES = 8


def _nchunks(rows, mult):
    for n in range(MAX_PIECES, 0, -1):
        if rows % (n * mult) == 0:
            return n
    return 1


def _leading_pieces(src, dst):
    n = src.shape[0]
    if len(src.shape) >= 3 and n <= MAX_PIECES:
        return [(src.at[s], dst.at[s]) for s in range(n)]
    return [(src, dst)]


def _ag_weights(shards, kinds, jshard):
    n = len(shards)

    def body(*refs):
        sh_refs, out_refs = refs[:n], refs[2 * n:3 * n]
        send_sems, recv_sems = refs[3 * n:]
        x, y, c = _where_am_i()
        j = 2 * x + y
        sib = (x, y, 1 - c)
        chips = [(1 - x, y), (x, 1 - y), (1 - x, 1 - y)]

        def cp(k, src, dst, to):
            return pltpu.make_async_remote_copy(src_ref=src, dst_ref=dst, send_sem=send_sems.at[k], recv_sem=recv_sems.at[k],
                                                device_id=to, device_id_type=MESH)

        started = []
        for a in range(n):
            for k, (cx, cy) in enumerate(chips):
                d = cp(6 * a + k, _half_of_shard(sh_refs[a], kinds[a], c), _half_of_shard(out_refs[a].at[j], kinds[a], c), (cx, cy, c))
                d.start()
                started.append(d)
        for a in range(n):
            for k, (cx, cy) in enumerate(chips):
                blk = _half_of_shard(out_refs[a].at[2 * cx + cy], kinds[a], c)
                cp(6 * a + k, blk, blk, (cx, cy, c)).wait_recv()
                d = cp(6 * a + 3 + k, blk, blk, sib)
                d.start()
                started.append(d)
        for a in range(n):
            for k, (cx, cy) in enumerate(chips):
                blk = _half_of_shard(out_refs[a].at[2 * cx + cy], kinds[a], 1 - c)
                cp(6 * a + 3 + k, blk, blk, sib).wait_recv()
        for d in started:
            d.wait_send()

    placed = [lax.dynamic_update_slice(jnp.zeros((4,) + s.shape, s.dtype), s[None], (jshard,) + (0,) * s.ndim) for s in shards]
    anyspec = pl.BlockSpec(memory_space=pl.ANY)
    return pl.pallas_call(body, name="all_gather_weights", in_specs=[anyspec] * (2 * n), out_specs=[anyspec] * n,
                          out_shape=[jax.ShapeDtypeStruct(p.shape, p.dtype) for p in placed],
                          input_output_aliases={n + a: a for a in range(n)},
                          scratch_shapes=[pltpu.SemaphoreType.DMA((6 * n,)), pltpu.SemaphoreType.DMA((6 * n,))],
                          compiler_params=pltpu.CompilerParams(has_side_effects=True))(*shards, *placed)


def _rs_swap(parts, kinds):
    n = len(parts)

    def body(*refs):
        p_refs, got_refs = refs[:n], refs[n:2 * n]
        send_sems, recv_sems = refs[2 * n:]
        x, y, c = _where_am_i()

        def cp(a, src, dst):
            return pltpu.make_async_remote_copy(src_ref=src, dst_ref=dst, send_sem=send_sems.at[a], recv_sem=recv_sems.at[a],
                                                device_id=(x, y, 1 - c), device_id_type=MESH)

        for a in range(n):
            for src, dst in _leading_pieces(_half_of_full(p_refs[a], kinds[a], 1 - c), got_refs[a]):
                cp(a, src, dst).start()
        for a in range(n):
            cp(a, got_refs[a], got_refs[a]).wait()

    anyspec = pl.BlockSpec(memory_space=pl.ANY)
    return pl.pallas_call(body, name="rs_swap_halves", in_specs=[anyspec] * n, out_specs=[anyspec] * n,
                          out_shape=[jax.ShapeDtypeStruct(_half_shape(p.shape, k), p.dtype) for p, k in zip(parts, kinds)],
                          scratch_shapes=[pltpu.SemaphoreType.DMA((n,)), pltpu.SemaphoreType.DMA((n,))],
                          compiler_params=pltpu.CompilerParams(has_side_effects=True))(*parts)


def _rs_owners(qbs, kinds, full_shapes):
    n = len(qbs)

    def body(*refs):
        q_refs, got_refs = refs[:n], refs[n:2 * n]
        send_sems, recv_sems = refs[2 * n:]
        x, y, c = _where_am_i()
        chips = [(1 - x, y), (x, 1 - y), (1 - x, 1 - y)]
        ds = []
        for a in range(n):
            for k, (cx, cy) in enumerate(chips):
                d = pltpu.make_async_remote_copy(src_ref=_shard_of_half(q_refs[a], kinds[a], 2 * cx + cy), dst_ref=got_refs[a].at[k],
                                                 send_sem=send_sems.at[3 * a + k], recv_sem=recv_sems.at[3 * a + k],
                                                 device_id=(cx, cy, c), device_id_type=MESH)
                d.start()
                ds.append(d)
        for d in ds:
            d.wait()

    anyspec = pl.BlockSpec(memory_space=pl.ANY)
    return pl.pallas_call(body, name="rs_to_owners", in_specs=[anyspec] * n, out_specs=[anyspec] * n,
                          out_shape=[jax.ShapeDtypeStruct((3,) + _shard_half_shape(fs, k), bf16) for fs, k in zip(full_shapes, kinds)],
                          scratch_shapes=[pltpu.SemaphoreType.DMA((3 * n,)), pltpu.SemaphoreType.DMA((3 * n,))],
                          compiler_params=pltpu.CompilerParams(has_side_effects=True))(*qbs)


def _rs_join(bufs, kinds):
    n = len(bufs)

    def body(*refs):
        out_refs = refs[n:2 * n]
        send_sems, recv_sems = refs[2 * n:]
        x, y, c = _where_am_i()

        def cp(a, h):
            blk = _half_of_shard(out_refs[a], kinds[a], h)
            return pltpu.make_async_remote_copy(src_ref=blk, dst_ref=blk, send_sem=send_sems.at[a], recv_sem=recv_sems.at[a],
                                                device_id=(x, y, 1 - c), device_id_type=MESH)

        for a in range(n):
            cp(a, c).start()
        for a in range(n):
            cp(a, c).wait_send()
            cp(a, 1 - c).wait_recv()

    anyspec = pl.BlockSpec(memory_space=pl.ANY)
    return pl.pallas_call(body, name="rs_join_halves", in_specs=[anyspec] * n, out_specs=[anyspec] * n,
                          out_shape=[jax.ShapeDtypeStruct(b.shape, b.dtype) for b in bufs],
                          input_output_aliases={a: a for a in range(n)},
                          scratch_shapes=[pltpu.SemaphoreType.DMA((n,)), pltpu.SemaphoreType.DMA((n,))],
                          compiler_params=pltpu.CompilerParams(has_side_effects=True))(*bufs)


def _all_reduce_small(g):
    R, C = g.shape
    H = R // 2
    NP = _nchunks(H, SUB)
    PR = H // NP

    def body(g_ref, out_ref, sib_ref, chip_ref, send_sems, recv_sems):
        x, y, c = _where_am_i()
        j = 2 * x + y
        sib = (x, y, 1 - c)
        chips = [(1 - x, y), (x, 1 - y), (1 - x, 1 - y)]
        rows = pl.ds(pl.multiple_of(c * H, SUB), H)

        def cp(k, src, dst, to):
            return pltpu.make_async_remote_copy(src_ref=src, dst_ref=dst, send_sem=send_sems.at[k], recv_sem=recv_sems.at[k],
                                                device_id=to, device_id_type=MESH)

        def pieces(k, src, dst, to):
            for q in range(NP):
                cp(k, src.at[pl.ds(q * PR, PR)], dst.at[pl.ds(q * PR, PR)], to).start()

        for half in range(2):
            pieces(0, g_ref.at[pl.ds(half * H, H)], sib_ref.at[pl.ds(half * H, H)], sib)
        cp(0, g_ref, sib_ref, sib).wait()
        chip_ref[j] = g_ref[rows, :] + sib_ref[rows, :]
        for k, (cx, cy) in enumerate(chips):
            pieces(1 + k, chip_ref.at[j], chip_ref.at[j], (cx, cy, c))
        for k, (cx, cy) in enumerate(chips):
            blk = chip_ref.at[2 * cx + cy]
            cp(1 + k, blk, blk, (cx, cy, c)).wait()
        out_ref[rows, :] = ((chip_ref[0] + chip_ref[1]) + chip_ref[2]) + chip_ref[3]
        other = out_ref.at[pl.ds(pl.multiple_of((1 - c) * H, SUB), H)]
        pieces(4, out_ref.at[rows], out_ref.at[rows], sib)
        cp(4, other, other, sib).wait()

    vm = pl.BlockSpec(memory_space=pltpu.VMEM)
    return pl.pallas_call(body, name="all_reduce_small", in_specs=[vm], out_specs=vm,
                          out_shape=jax.ShapeDtypeStruct((R, C), f32),
                          scratch_shapes=[pltpu.VMEM((R, C), f32), pltpu.VMEM((4, H, C), f32),
                                          pltpu.SemaphoreType.DMA((5,)), pltpu.SemaphoreType.DMA((5,))],
                          compiler_params=pltpu.CompilerParams(has_side_effects=True, vmem_limit_bytes=VMEM_LIMIT))(g)


SPLIT_MIN_ELEMS = 1 << 16


def _all_reduce_many(gs):
    n = len(gs)
    split = [g.ndim == 3 and g.shape[0] % 2 == 0 and g.size >= SPLIT_MIN_ELEMS for g in gs]
    part_shape = [((g.shape[0] // 2,) + g.shape[1:]) if s else g.shape for g, s in zip(gs, split)]
    n_split = sum(split)

    def body(*refs):
        g, out, sibs, chipb = refs[:n], refs[n:2 * n], refs[2 * n:3 * n], refs[3 * n:4 * n]
        send_sems, recv_sems = refs[4 * n:]
        x, y, c = _where_am_i()
        j = 2 * x + y
        sib = (x, y, 1 - c)
        chips = [(1 - x, y), (x, 1 - y), (1 - x, 1 - y)]

        def cp(k, src, dst, to):
            return pltpu.make_async_remote_copy(src_ref=src, dst_ref=dst, send_sem=send_sems.at[k], recv_sem=recv_sems.at[k],
                                                device_id=to, device_id_type=MESH)

        def part(a, h):
            return pl.ds(h * part_shape[a][0], part_shape[a][0]) if split[a] else Ellipsis

        def mine(ref, a, h):
            return ref.at[part(a, h)] if split[a] else ref

        swaps = [cp(a, g[a], sibs[a], sib) for a in range(n)]
        for d in swaps:
            d.start()
        for a in range(n):
            swaps[a].wait()
            chipb[a][j] = g[a][part(a, c)] + sibs[a][part(a, c)]
        sends = [cp(n + 3 * a + k, chipb[a].at[j], chipb[a].at[j], (cx, cy, c)) for a in range(n) for k, (cx, cy) in enumerate(chips)]
        for d in sends:
            d.start()
        for a in range(n):
            for k, (cx, cy) in enumerate(chips):
                blk = chipb[a].at[2 * cx + cy]
                cp(n + 3 * a + k, blk, blk, (cx, cy, c)).wait_recv()
            out[a][part(a, c)] = ((chipb[a][0] + chipb[a][1]) + chipb[a][2]) + chipb[a][3]
        for d in sends:
            d.wait_send()
        backs = [(a, cp(4 * n + i, mine(out[a], a, c), mine(out[a], a, c), sib)) for i, a in enumerate([a for a in range(n) if split[a]])]
        for _, d in backs:
            d.start()
        for i, (a, d) in enumerate(backs):
            d.wait_send()
            cp(4 * n + i, mine(out[a], a, 1 - c), mine(out[a], a, 1 - c), sib).wait_recv()

    vm = pl.BlockSpec(memory_space=pltpu.VMEM)
    nsem = 4 * n + n_split
    return pl.pallas_call(body, name="all_reduce_small_grads", in_specs=[vm] * n, out_specs=[vm] * n,
                          out_shape=[jax.ShapeDtypeStruct(g.shape, f32) for g in gs],
                          scratch_shapes=([pltpu.VMEM(g.shape, f32) for g in gs] + [pltpu.VMEM((4,) + ps, f32) for ps in part_shape]
                                          + [pltpu.SemaphoreType.DMA((nsem,)), pltpu.SemaphoreType.DMA((nsem,))]),
                          compiler_params=pltpu.CompilerParams(has_side_effects=True, vmem_limit_bytes=VMEM_LIMIT))(*gs)


def _adamw_many(ws, gs, ms, vs, name):
    n = len(ws)
    c1 = 1.0 / (1.0 - ADAM_B1 ** ADAM_STEP)
    c2 = 1.0 / (1.0 - ADAM_B2 ** ADAM_STEP)

    def body(*refs):
        for a in range(n):
            w_ref, g_ref, m_ref, v_ref, d_ref, nm_ref, nv_ref = (refs[i * n + a] for i in range(7))
            g_ = g_ref[...]
            nm = ADAM_B1 * m_ref[...] + (1.0 - ADAM_B1) * g_
            nv = ADAM_B2 * v_ref[...] + (1.0 - ADAM_B2) * (g_ * g_)
            nm_ref[...] = nm
            nv_ref[...] = nv
            d_ref[...] = -ADAM_LR * ((nm * c1) / (jnp.sqrt(nv * c2) + ADAM_EPS) + ADAM_WD * w_ref[...])

    vm = pl.BlockSpec(memory_space=pltpu.VMEM)
    sh = [jax.ShapeDtypeStruct(w.shape, f32) for w in ws]
    outs = pl.pallas_call(body, name=name, in_specs=[vm] * (4 * n), out_specs=[vm] * (3 * n), out_shape=sh * 3,
                          compiler_params=pltpu.CompilerParams(vmem_limit_bytes=VMEM_LIMIT))(*ws, *gs, *ms, *vs)
    return outs[:n], outs[n:2 * n], outs[2 * n:]


def _pack_flat(arrs, rows_mult):
    flat = jnp.concatenate([a.reshape(-1) for a in arrs])
    n = flat.shape[0]
    tot = -(-n // (rows_mult * LANE)) * rows_mult * LANE
    return jnp.pad(flat, (0, tot - n)).reshape(-1, LANE)


def _unpack_flat(buf, shapes):
    flat = buf.reshape(-1)
    out, o = [], 0
    for s in shapes:
        n = math.prod(s)
        out.append(flat[o:o + n].reshape(s))
        o += n
    return out


_BIG = ("mem_kv_w", "w_out", "a_w_in", "b_w_in", "c_w_in", "d_w_in")
SMALL_ROWS_MULT = 256


def _row8(v):
    v = v.reshape(-1, v.shape[-1])
    return jnp.pad(v, ((0, SUB - v.shape[0]), (0, 0)))


def kernel(x, mem, mem_kv_w, ln_g, ln_b, w_out, hgrn_lb_logits, a_w_in, a_w_s, a_b_s, b_w_in, b_norm_g, c_w_in, c_w_pool, c_scale, d_w_in, d_conv_w, d_conv_b, d_w_gx, d_b_gx, d_w_ga, d_b_ga, d_a_param, loss_target, m_mem_kv_w, m_ln_g, m_ln_b, m_w_out, m_hgrn_lb_logits, m_a_w_in, m_a_w_s, m_a_b_s, m_b_w_in, m_b_norm_g, m_c_w_in, m_c_w_pool, m_c_scale, m_d_w_in, m_d_conv_w, m_d_conv_b, m_d_w_gx, m_d_b_gx, m_d_w_ga, m_d_b_ga, m_d_a_param, v_mem_kv_w, v_ln_g, v_ln_b, v_w_out, v_hgrn_lb_logits, v_a_w_in, v_a_w_s, v_a_b_s, v_b_w_in, v_b_norm_g, v_c_w_in, v_c_w_pool, v_c_scale, v_d_w_in, v_d_conv_w, v_d_conv_b, v_d_w_gx, v_d_b_gx, v_d_w_ga, v_d_b_ga, v_d_a_param):
    names = ["mem_kv_w", "ln_g", "ln_b", "w_out", "hgrn_lb_logits", "a_w_in", "a_w_s", "a_b_s", "b_w_in", "b_norm_g", "c_w_in",
             "c_w_pool", "c_scale", "d_w_in", "d_conv_w", "d_conv_b", "d_w_gx", "d_b_gx", "d_w_ga", "d_b_ga", "d_a_param"]
    w = dict(mem_kv_w=mem_kv_w, ln_g=ln_g, ln_b=ln_b, w_out=w_out, hgrn_lb_logits=hgrn_lb_logits, a_w_in=a_w_in, a_w_s=a_w_s,
             a_b_s=a_b_s, b_w_in=b_w_in, b_norm_g=b_norm_g, c_w_in=c_w_in, c_w_pool=c_w_pool, c_scale=c_scale, d_w_in=d_w_in,
             d_conv_w=d_conv_w, d_conv_b=d_conv_b, d_w_gx=d_w_gx, d_b_gx=d_b_gx, d_w_ga=d_w_ga, d_b_ga=d_b_ga, d_a_param=d_a_param)
    m = dict(zip(names, [m_mem_kv_w, m_ln_g, m_ln_b, m_w_out, m_hgrn_lb_logits, m_a_w_in, m_a_w_s, m_a_b_s, m_b_w_in, m_b_norm_g,
                         m_c_w_in, m_c_w_pool, m_c_scale, m_d_w_in, m_d_conv_w, m_d_conv_b, m_d_w_gx, m_d_b_gx, m_d_w_ga,
                         m_d_b_ga, m_d_a_param]))
    v = dict(zip(names, [v_mem_kv_w, v_ln_g, v_ln_b, v_w_out, v_hgrn_lb_logits, v_a_w_in, v_a_w_s, v_a_b_s, v_b_w_in, v_b_norm_g,
                         v_c_w_in, v_c_w_pool, v_c_scale, v_d_w_in, v_d_conv_w, v_d_conv_b, v_d_w_gx, v_d_b_gx, v_d_w_ga,
                         v_d_b_ga, v_d_a_param]))
    xi, yi = lax.axis_index("x"), lax.axis_index("y")
    jshard = 2 * xi + yi
    x2 = x[0]
    mem2 = mem[0]
    tgt2 = loss_target[0]

    kinds = ("kv", "wout", "win", "win", "win", "win")
    big2d = lambda d: [d["mem_kv_w"], d["w_out"]] + [d[n][0] for n in _BIG[2:]]
    gath = _ag_weights([a.astype(bf16) for a in big2d(w)], kinds, jshard)
    w_kv = gath[0].reshape(D_MODEL, 2 * XW)
    w_outs = [gath[1][:, l].reshape(D_MODEL, D_MODEL) for l in range(DEPTH)]
    w_outT = gath[1].transpose(1, 3, 0, 2).reshape(DEPTH, D_MODEL, D_MODEL)
    w_in = [g.transpose(1, 0, 2).reshape(D_MODEL, -1) for g in gath[2:]]
    w_inT = [g.transpose(0, 2, 1).reshape(-1, D_MODEL) for g in gath[2:]]

    def gather_small(shard):
        z = jnp.zeros((4, POOL_GROUP), f32)
        return lax.dynamic_update_slice(z, shard.reshape(1, POOL_GROUP), (jshard, 0))

    sm_sh = jnp.concatenate([gather_small(b_norm_g), gather_small(c_scale), gather_small(d_conv_b), gather_small(d_a_param)]
                            + [gather_small(d_conv_w[:, r]) for r in range(4)], axis=0)
    ci = lax.axis_index("c")
    sm_all = _all_reduce_small(_pack_flat([jnp.where(ci == 0, sm_sh, 0.0)], SUB * 2))
    sm = _unpack_flat(sm_all, [(8, 4 * POOL_GROUP)])[0]
    ng_full, scale_full, convb_full, ap_full = sm[0:1], sm[1:2], sm[2:3], sm[3:4]
    convw_full = sm[4:8]

    tril = jnp.tril(jnp.ones((HD, HD), bool))
    wtri = jnp.where(tril, a_w_s[0], 0.0)
    wbd = jnp.zeros((TOK, TOK), f32)
    for g in range(4):
        wbd = lax.dynamic_update_slice(wbd, c_w_pool[0, g], (g * POOL_GROUP, g * POOL_GROUP))
    kh, khT, vh, vhT, p_soft = _prep(mem2, w_kv, hgrn_lb_logits)
    prm = [
        dict(wtri=wtri.astype(bf16), wtriT=wtri.transpose(0, 2, 1).astype(bf16),
             bcolb=jnp.broadcast_to(a_b_s[0][:, :, None], (NH, HD, HD))),
        dict(lb=p_soft[1:2], ng=ng_full),
        dict(wbd=wbd.astype(bf16), wbdT=wbd.T.astype(bf16), scale=scale_full),
        dict(cw=_row8(convw_full), cb=convb_full, wgx=d_w_gx[0].astype(bf16), wgxT=d_w_gx[0].transpose(0, 2, 1).astype(bf16),
             bgx=d_b_gx.reshape(1, TOK), wga=d_w_ga[0].astype(bf16), wgaT=d_w_ga[0].transpose(0, 2, 1).astype(bf16),
             bga=d_b_ga.reshape(1, TOK), ap=ap_full),
    ]

    acts = []
    h = x2
    for l in range(DEPTH):
        outs = _fwd_layer(l, h, w_in[l], w_outs[l], ln_g[l:l + 1], ln_b[l:l + 1], khT, vh, prm[l],
                          tgt2 if l == DEPTH - 1 else None)
        nfix = 4 if l == DEPTH - 1 else 3
        acts.append(dict(xin=h, proj=outs[1], z=outs[2], saves=outs[nfix:]))
        if l == DEPTH - 1:
            loss_part = outs[3]
        h = outs[0]
    loss = lax.psum(0.5 / D_MODEL * jnp.sum(loss_part), ("x", "y", "c"))

    dh = h
    gw_in = [None] * DEPTH
    gw_out = None
    dln = [None] * DEPTH
    dks, dvs = [None] * DEPTH, [None] * DEPTH
    sgr = [None] * DEPTH
    for l in reversed(range(DEPTH)):
        a = acts[l]
        (dxin, dproj, mixedb, dyb, dln[l], dks[l], dvs[l]), sgr[l] = _bwd_layer(
            l, dh, a["z"], a["proj"], w_inT[l], w_outT[l], ln_g[l:l + 1], kh, khT, vh, vhT, prm[l], a["saves"])
        if _OFFS[l]["W"] // 4 % LANE:
            gw_in[l] = _tn_gemm(a["xin"], dproj, f"grad_w_in{l}", 1).reshape(D_MODEL, 4, -1).transpose(1, 0, 2)
        else:
            gw_in[l] = _tn_gemm_sharded(a["xin"], dproj, f"grad_w_in{l}")
        gw_out = _tn_gemm_slab(mixedb, dyb, gw_out, l, f"grad_w_out{l}")
        dh = dxin
    grad_x = dh[None]

    parts = [_kv_bwd(mem2, dks, dvs), gw_out] + gw_in
    jc = jnp.stack([jshard, ci]).astype(jnp.int32)
    gots = _rs_swap(parts, kinds)
    sums = [_add_sibling(p, g, k, jc, f"rs_add_sibling{a}") for a, (p, g, k) in enumerate(zip(parts, gots, kinds))]
    gots2 = _rs_owners([s[1] for s in sums], kinds, [p.shape for p in parts])
    gbig = _rs_join([_add_chips(s[0], r, k, jc, f"rs_add_chips{a}") for a, (s, r, k) in enumerate(zip(sums, gots2, kinds))], kinds)
    g_sh, d_sh, m_sh, v_sh = {}, {}, {}, {}
    for a, n in enumerate(_BIG):
        as2d = lambda t: t.reshape(-1, t.shape[-1])
        upd = _adamw(as2d(w[n]), as2d(gbig[a]), as2d(m[n]), as2d(v[n]), f"adamw_{n}")
        g_sh[n] = gbig[a].reshape(w[n].shape)
        d_sh[n], m_sh[n], v_sh[n] = (u.reshape(w[n].shape) for u in upd)

    dbs, dlogits = _small_finish(sgr[0]["dbacc"], p_soft, sgr[1]["dlb"])
    gs = {
        "ln_g": jnp.concatenate([dln[l][0:1] for l in range(DEPTH)], axis=0),
        "ln_b": jnp.concatenate([dln[l][1:2] for l in range(DEPTH)], axis=0),
        "hgrn_lb_logits": dlogits,
        "a_w_s": sgr[0]["dwtri"][None],
        "a_b_s": dbs[:, 0:NH].T[None],
        "b_norm_g": sgr[1]["dng"][0:1],
        "c_w_pool": jnp.stack([sgr[2]["dwbd"][g * POOL_GROUP:(g + 1) * POOL_GROUP, g * POOL_GROUP:(g + 1) * POOL_GROUP]
                               for g in range(4)])[None],
        "c_scale": sgr[2]["dscale"][0:1],
        "d_conv_w": sgr[3]["dcw"][0:4][None],
        "d_conv_b": sgr[3]["dvec"][3:4],
        "d_w_gx": sgr[3]["dwgx"][None],
        "d_b_gx": sgr[3]["dvec"][1:2].reshape(1, NH, HD),
        "d_w_ga": sgr[3]["dwga"][None],
        "d_b_ga": sgr[3]["dvec"][2:3].reshape(1, NH, HD),
        "d_a_param": sgr[3]["dvec"][0:1],
    }
    small = [n for n in names if n not in _BIG]
    drop1 = lambda t: t.reshape(t.shape[1:]) if t.ndim > 2 and t.shape[0] == 1 else t
    gsum = dict(zip(small, _all_reduce_many([drop1(gs[n]) for n in small])))
    for n in ("b_norm_g", "c_scale", "d_conv_b", "d_a_param"):
        gsum[n] = lax.dynamic_slice(gsum[n], (0, jshard * POOL_GROUP), (1, POOL_GROUP))
    gsum["d_conv_w"] = lax.dynamic_slice(gsum["d_conv_w"], (0, jshard * POOL_GROUP), (4, POOL_GROUP))
    upd = _adamw_many(*[[drop1(d[n]) for n in small] for d in (w, gsum, m, v)], "adamw_small")
    gsum = {n: gsum[n].reshape(w[n].shape) for n in small}
    d_sm, m_sm, v_sm = ({n: u.reshape(w[n].shape) for n, u in zip(small, us)} for us in upd)

    grads = {**gsum, **g_sh}
    deltas = {**d_sm, **d_sh}
    new_m = {**m_sm, **m_sh}
    new_v = {**v_sm, **v_sh}
    return (loss, grad_x, *[grads[n] for n in names], *[deltas[n] for n in names], *[new_m[n] for n in names],
            *[new_v[n] for n in names])
```

```python
import functools
import math

import jax
import jax.numpy as jnp
from jax import lax
from jax.experimental import pallas as pl
from jax.experimental.pallas import tpu as pltpu

f32 = jnp.float32
bf16 = jnp.bfloat16
MM = bf16

D_MODEL = 1024
TOK = 768
XW = 256
XHEADS = 4
XDIM = 64
HD = 128
NH = TOK // HD
CHUNK = 16
POOL_GROUP = 192
DEPTH = 4
ALPHA = (2 * DEPTH) ** 0.25
LN_EPS = 1e-5
RMS_EPS = 1e-6
LRU_C = 8.0
ADAM_LR, ADAM_B1, ADAM_B2, ADAM_EPS, ADAM_WD, ADAM_STEP = 0.001, 0.9, 0.999, 1e-08, 0.01, 10

_TS = (256, 256, 256, 256)
HGRN_SUB = 128
TK = 512
SUB = 8
LANE = 128
VMEM_LIMIT = 58 * 1024 * 1024

_OFFS = (
    dict(u=0, v=768, qx=1536, gate=1792, W=2816),
    dict(q=0, f=768, i=1536, qx=2304, gate=2560, W=3584),
    dict(p=0, qx=768, gate=1024, W=2048),
    dict(xb=0, qx=768, gate=1024, W=2048),
)
_PRM = (
    ("wtri", "wtriT", "bcolb"),
    ("lb", "ng"),
    ("wbd", "wbdT", "scale"),
    ("cw", "cb", "wgx", "wgxT", "bgx", "wga", "wgaT", "bga", "ap"),
)
MESH = pl.DeviceIdType.MESH


def _mm(a, b):
    return jnp.dot(a.astype(MM), b.astype(MM), preferred_element_type=f32)


def _mm_nt(a, b):
    return lax.dot_general(a.astype(MM), b.astype(MM), (((1,), (1,)), ((), ())), preferred_element_type=f32)


def _mm_tn(a, b):
    return lax.dot_general(a.astype(MM), b.astype(MM), (((0,), (0,)), ((), ())), preferred_element_type=f32)


def _mm_sel(sel, b):
    s = sel.astype(bf16)
    hi = b.astype(bf16)
    lo = (b - hi.astype(f32)).astype(bf16)
    return jnp.dot(s, hi, preferred_element_type=f32) + jnp.dot(s, lo, preferred_element_type=f32)


def _sig(x):
    return jax.nn.sigmoid(x)


_GC = math.sqrt(2.0 / math.pi)


def _gelu(x):
    t = jnp.tanh(_GC * (x + 0.044715 * x * x * x))
    return 0.5 * x * (1.0 + t), t


def _gelu_grad(x, t):
    return 0.5 * (1.0 + t) + 0.5 * x * (1.0 - t * t) * _GC * (1.0 + 3.0 * 0.044715 * x * x)


def _rowsum(x):
    return jnp.sum(x, axis=0, keepdims=True)


def _lmean(x):
    return jnp.mean(x, axis=-1, keepdims=True)


def _ln(z):
    mu = _lmean(z)
    zc = z - mu
    rstd = lax.rsqrt(_lmean(zc * zc) + LN_EPS)
    return zc * rstd, rstd


def _ln_bwd(dxh, xhat, rstd):
    return rstd * (dxh - _lmean(dxh) - xhat * _lmean(dxh * xhat))


def _hs(h):
    return slice(h * HD, (h + 1) * HD)


def _expm1(x):
    small = x * (1.0 + x * 0.5 * (1.0 + x * (1.0 / 3.0) * (1.0 + x * 0.25 * (1.0 + x * 0.2 * (1.0 + x * (1.0 / 6.0))))))
    return jnp.where(jnp.abs(x) < 0.25, small, jnp.exp(x) - 1.0)


def _softplus(x):
    e = jnp.exp(-jnp.abs(x))
    l1p = jnp.where(e < 1e-4, e - 0.5 * e * e, jnp.log(1.0 + e))
    return jnp.maximum(x, 0.0) + l1p


def _scan_fwd(a, b):
    n = a.shape[0]
    row = lax.broadcasted_iota(jnp.int32, a.shape, 0)
    d = 1
    while d < n:
        if d % SUB:
            m = row >= d
            b = jnp.where(m, a * pltpu.roll(b, d, 0) + b, b)
            a = jnp.where(m, a * pltpu.roll(a, d, 0), a)
        else:
            b = a * jnp.concatenate([jnp.zeros((d,) + b.shape[1:], f32), b[:n - d]], axis=0) + b
            a = a * jnp.concatenate([jnp.ones((d,) + a.shape[1:], f32), a[:n - d]], axis=0)
        d *= 2
    return a, b


def _scan_bwd(a, b):
    n = a.shape[0]
    row = lax.broadcasted_iota(jnp.int32, a.shape, 0)
    d = 1
    while d < n:
        if d % SUB:
            m = row < n - d
            b = jnp.where(m, a * pltpu.roll(b, n - d, 0) + b, b)
            a = jnp.where(m, a * pltpu.roll(a, n - d, 0), a)
        else:
            b = a * jnp.concatenate([b[d:], jnp.zeros((d,) + b.shape[1:], f32)], axis=0) + b
            a = a * jnp.concatenate([a[d:], jnp.ones((d,) + a.shape[1:], f32)], axis=0)
        d *= 2
    return a, b


def _chunk_mats(n):
    r = lax.broadcasted_iota(jnp.int32, (n, n), 0)
    c = lax.broadcasted_iota(jnp.int32, (n, n), 1)
    same = (r // CHUNK) == (c // CHUNK)
    return same, jnp.logical_and(same, c <= r)


def _pool_w(shape):
    lane = lax.broadcasted_iota(jnp.int32, shape, 1)
    return jnp.where(lane < POOL_GROUP, 2, jnp.where(lane < 2 * POOL_GROUP, 4, jnp.where(lane < 3 * POOL_GROUP, 8, 16)))


def _pool_pick(r1, r2, r3, r4):
    lane = lax.broadcasted_iota(jnp.int32, r1.shape, 1)
    return jnp.where(lane < POOL_GROUP, r1, jnp.where(lane < 2 * POOL_GROUP, r2, jnp.where(lane < 3 * POOL_GROUP, r3, r4)))


def _const_spec(a):
    nd = a.ndim
    return pl.BlockSpec(a.shape, lambda i, _nd=nd: (0,) * _nd, pipeline_mode=pl.Buffered(1))


def _acc_spec(shape):
    nd = len(shape)
    return pl.BlockSpec(shape, lambda i, _nd=nd: (0,) * _nd)


def _params(sem="arbitrary"):
    return pltpu.CompilerParams(dimension_semantics=(sem,), vmem_limit_bytes=VMEM_LIMIT)


def _xattn_fwd(qx, khT_ref, vh_ref):
    xo = jnp.zeros((qx.shape[0], XW), f32)
    ps = []
    for h in range(XHEADS):
        s = _mm(qx, khT_ref[h]) * (XDIM ** -0.5)
        e = jnp.exp(s - jnp.max(s, axis=-1, keepdims=True))
        p = e / jnp.sum(e, axis=-1, keepdims=True)
        xo = xo + _mm(p, vh_ref[h])
        ps.append(p)
    return xo, ps


def _hgrn_parallel(q_raw, fl, lb):
    n = q_raw.shape[0]
    same, tri = _chunk_mats(n)
    sq = _sig(q_raw)
    qf = q_raw * sq
    sgm = _sig(fl)
    f = lb + (1.0 - lb) * sgm
    logf = jnp.log(f)
    k = 1.0 - f
    g = _mm_sel(tri, logf)
    gl = _mm_sel(same, logf)
    eg = jnp.exp(g)
    eng = jnp.exp(-g)
    ee = jnp.exp(gl - g)
    return dict(sq=sq, qf=qf, sgm=sgm, f=f, k=k, eg=eg, eng=eng, ee=ee, q_dec=qf * eg, k_inv=k * eng, k_end=k * ee,
                a=jnp.exp(gl))


def _hgrn_intra(q_dec, k_inv, v):
    n = q_dec.shape[0]
    _, tri = _chunk_mats(HD)
    outs = []
    for h in range(NH):
        blks = []
        for b in range(n // HD):
            rs = slice(b * HD, (b + 1) * HD)
            sc = jnp.where(tri, _mm_nt(q_dec[rs, _hs(h)], k_inv[rs, _hs(h)]), 0.0)
            blks.append(_mm(sc, v[rs, _hs(h)]))
        outs.append(jnp.concatenate(blks, axis=0))
    return jnp.concatenate(outs, axis=-1)


def _cs(c):
    return slice(c * CHUNK, (c + 1) * CHUNK)


def _hgrn_inter_fwd(qdec_s, kend_s, v_s, a_s, oint_s, st_ref, states_s, u_s):
    n = qdec_s.shape[0] // CHUNK
    for c in range(n):
        for h in range(NH):
            u_s[c, h] = _mm_tn(v_s[_cs(c), _hs(h)], kend_s[_cs(c), _hs(h)])
    for h in range(NH):
        st = st_ref[h]
        for c in range(n):
            states_s[c, h] = st
            st = st * a_s[c * CHUNK:c * CHUNK + 1, _hs(h)] + u_s[c, h]
        st_ref[h] = st
    if oint_s is None:
        return
    for c in range(n):
        for h in range(NH):
            oint_s[_cs(c), _hs(h)] = _mm_nt(qdec_s[_cs(c), _hs(h)], states_s[c, h])


def _rms(o):
    outs, rs = [], []
    for h in range(NH):
        oh = o[:, _hs(h)]
        r = lax.rsqrt(_lmean(oh * oh) + RMS_EPS)
        outs.append(oh * r)
        rs.append(r)
    return jnp.concatenate(outs, axis=-1), rs


def _gmlp_core(u_raw, v_raw, wtri_ref, bcolb_ref):
    gu, tu = _gelu(u_raw)
    gv, tv = _gelu(v_raw)
    vns, rstds, mixeds = [], [], []
    for h in range(NH):
        vn, rstd = _ln(gv[:, _hs(h)])
        blks = []
        for n in range(u_raw.shape[0] // HD):
            blks.append(_mm(wtri_ref[h], vn[n * HD:(n + 1) * HD]) + bcolb_ref[h])
        vns.append(vn)
        rstds.append(rstd)
        mixeds.append(jnp.concatenate(blks, axis=0))
    mixed = jnp.concatenate(mixeds, axis=-1)
    return gu, tu, tv, vns, rstds, mixed


def _pool_core(p, carry, row0, wbd_ref):
    ext = jnp.concatenate([carry, p], axis=0)
    r1 = ext + pltpu.roll(ext, 1, 0)
    r2 = r1 + pltpu.roll(r1, 2, 0)
    r3 = r2 + pltpu.roll(r2, 4, 0)
    r4 = r3 + pltpu.roll(r3, 8, 0)
    sel = _pool_pick(r1, r2, r3, r4)[2 * SUB:]
    grow = row0 + lax.broadcasted_iota(jnp.int32, p.shape, 0)
    inv_cnt = 1.0 / jnp.minimum(grow + 1, _pool_w(p.shape)).astype(f32)
    diff = sel * inv_cnt - p
    return diff, inv_cnt, _mm(diff, wbd_ref[...])


def _lru_core(xb, ccar, row0, p):
    ext = jnp.concatenate([ccar, xb], axis=0)
    cw = p["cw"]
    x1, x2, x3 = pltpu.roll(ext, 1, 0)[SUB:], pltpu.roll(ext, 2, 0)[SUB:], pltpu.roll(ext, 3, 0)[SUB:]
    xc = cw[3:4, :] * xb + cw[2:3, :] * x1 + cw[1:2, :] * x2 + cw[0:1, :] * x3 + p["cb"][...]
    gxs, gas = [], []
    for h in range(NH):
        gxs.append(_mm(xc[:, _hs(h)], p["wgx"][h]))
        gas.append(_mm(xc[:, _hs(h)], p["wga"][h]))
    gx = _sig(jnp.concatenate(gxs, axis=-1) + p["bgx"][...])
    ga = _sig(jnp.concatenate(gas, axis=-1) + p["bga"][...])
    sp = _softplus(-p["ap"][...])
    la = -LRU_C * ga * sp
    a = jnp.exp(la)
    grow = row0 + lax.broadcasted_iota(jnp.int32, xb.shape, 0)
    first = grow == 0
    mult = jnp.where(first, 1.0, jnp.sqrt(-_expm1(2.0 * la)))
    bt = mult * gx * xc
    return dict(x1=x1, x2=x2, x3=x3, xc=xc, gx=gx, ga=ga, sp=sp, a=a, mult=mult, bt=bt, first=first)


def _fwd_layer(kind, xin, w_in, w_out, lng, lnb, khT, vh, prm, tgt):
    S = xin.shape[0]
    TS = _TS[kind]
    nt = S // TS
    off = _OFFS[kind]
    W = off["W"]
    last = tgt is not None
    pnames = _PRM[kind]
    pvals = [prm[n] for n in pnames]

    def body(*refs):
        it = iter(refs)
        xin_ref, win_ref, wout_ref, lng_ref, lnb_ref, khT_ref, vh_ref = (next(it) for _ in range(7))
        p = {n: next(it) for n in pnames}
        tgt_ref = next(it) if last else None
        xout_ref, proj_ref, z_ref = next(it), next(it), next(it)
        loss_ref = next(it) if last else None
        rest = list(it)
        i = pl.program_id(0)
        x = xin_ref[...]
        proj_ref[...] = _mm(x, win_ref[...])

        if kind == 0:
            gu, _, _, _, _, mixed = _gmlp_core(proj_ref[:, 0:TOK], proj_ref[:, TOK:2 * TOK], p["wtri"], p["bcolb"])
            tok = gu * mixed
        elif kind == 1:
            st_save, o_save, st_ref, states_s, u_s, qdec_s, kend_s, v_s, a_s, oint_s = rest

            @pl.when(i == 0)
            def _():
                st_ref[...] = jnp.zeros_like(st_ref)

            st_save[0, 0] = st_ref[...]
            v = proj_ref[:, 2 * TOK:3 * TOK]
            hp = _hgrn_parallel(proj_ref[:, 0:TOK], proj_ref[:, TOK:2 * TOK], p["lb"][...])
            qdec_s[...] = hp["q_dec"]
            kend_s[...] = hp["k_end"]
            v_s[...] = v
            a_s[...] = hp["a"]
            o_intra = _hgrn_intra(hp["q_dec"], hp["k_inv"], v)
            _hgrn_inter_fwd(qdec_s, kend_s, v_s, a_s, oint_s, st_ref, states_s, u_s)
            o = o_intra + oint_s[...]
            o_save[0] = o
            for sub in range(1, TS // HGRN_SUB):
                st_save[0, sub] = states_s[sub * HGRN_SUB // CHUNK]
            on, _ = _rms(o)
            tok = on * p["ng"][...]
        elif kind == 2:
            pc_save, pcar = rest

            @pl.when(i == 0)
            def _():
                pcar[...] = jnp.zeros_like(pcar)

            pc_save[0] = pcar[...]
            pp = proj_ref[:, 0:TOK]
            _, _, y = _pool_core(pp, pcar[...], i * TS, p["wbd"])
            pcar[...] = pp[TS - 2 * SUB:, :]
            tok = y * p["scale"][...]
        else:
            cc_save, hc_save, h_save, ccar, hcar = rest

            @pl.when(i == 0)
            def _():
                ccar[...] = jnp.zeros_like(ccar)
                hcar[...] = jnp.zeros_like(hcar)

            cc_save[0] = ccar[...]
            hc_save[0] = hcar[...]
            xb = proj_ref[:, 0:TOK]
            lc = _lru_core(xb, ccar[...], i * TS, p)
            P, B = _scan_fwd(lc["a"], lc["bt"])
            tok = P * hcar[SUB - 1:SUB, :] + B
            h_save[0] = tok
            ccar[...] = xb[TS - SUB:, :]
            hcar[...] = tok[TS - SUB:, :]

        xo, _ = _xattn_fwd(proj_ref[:, off["qx"]:off["qx"] + XW], khT_ref, vh_ref)
        gate = proj_ref[:, off["gate"]:off["gate"] + D_MODEL]
        mixed = jnp.concatenate([tok, xo], axis=-1) * (gate * _sig(gate))
        z = ALPHA * x + _mm(mixed, wout_ref[...])
        z_ref[...] = z
        xhat, _ = _ln(z)
        xout = xhat * lng_ref[...] + lnb_ref[...]
        if last:
            e = xout - tgt_ref[...]
            xout_ref[...] = e * (1.0 / D_MODEL)
            es = _rowsum(e * e)
            tot = es[:, 0:LANE]
            for j in range(1, D_MODEL // LANE):
                tot = tot + es[:, j * LANE:(j + 1) * LANE]

            @pl.when(i == 0)
            def _():
                loss_ref[...] = jnp.zeros_like(loss_ref)

            loss_ref[0:1, :] += tot
        else:
            xout_ref[...] = xout

    tile = lambda w: pl.BlockSpec((TS, w), lambda i: (i, 0))
    in_arrays = [xin, w_in, w_out, lng, lnb, khT, vh] + pvals + ([tgt] if last else [])
    in_specs = [tile(D_MODEL)] + [_const_spec(a) for a in in_arrays[1:7 + len(pvals)]] + ([tile(D_MODEL)] if last else [])
    out_shape = [jax.ShapeDtypeStruct((S, D_MODEL), f32), jax.ShapeDtypeStruct((S, W), f32), jax.ShapeDtypeStruct((S, D_MODEL), f32)]
    out_specs = [tile(D_MODEL), tile(W), tile(D_MODEL)]
    if last:
        out_shape.append(jax.ShapeDtypeStruct((SUB, LANE), f32))
        out_specs.append(_acc_spec((SUB, LANE)))
    scratch = []
    save = lambda *s: (jax.ShapeDtypeStruct((nt,) + s, f32), pl.BlockSpec((1,) + s, lambda i, _n=len(s): (i,) + (0,) * _n))
    if kind == 1:
        saved = [save(TS // HGRN_SUB, NH, HD, HD), save(TS, TOK)]
        scratch = ([pltpu.VMEM((NH, HD, HD), f32)] + [pltpu.VMEM((TS // CHUNK, NH, HD, HD), f32)] * 2
                   + [pltpu.VMEM((TS, TOK), f32)] * 5)
    elif kind == 2:
        saved = [save(2 * SUB, TOK)]
        scratch = [pltpu.VMEM((2 * SUB, TOK), f32)]
    elif kind == 3:
        saved = [save(SUB, TOK), save(SUB, TOK), save(TS, TOK)]
        scratch = [pltpu.VMEM((SUB, TOK), f32)] * 2
    else:
        saved = []
    for sh, sp in saved:
        out_shape.append(sh)
        out_specs.append(sp)
    return pl.pallas_call(body, name=f"fwd_layer{kind}", grid=(nt,), in_specs=in_specs, out_specs=out_specs,
                          out_shape=out_shape, scratch_shapes=scratch, compiler_params=_params())(*in_arrays)


def _small_grad_shapes(kind):
    if kind == 0:
        return dict(dwtri=(NH, HD, HD), dbacc=(NH, HD, HD))
    if kind == 1:
        return dict(dlb=(SUB, TOK), dng=(SUB, TOK))
    if kind == 2:
        return dict(dwbd=(TOK, TOK), dscale=(SUB, TOK))
    return dict(dcw=(SUB, TOK), dvec=(SUB, TOK), dwgx=(NH, HD, HD), dwga=(NH, HD, HD))


def _bwd_layer(kind, dxout, z, proj, w_inT, w_outT, lng, kh, khT, vh, vhT, prm, saves):
    S = dxout.shape[0]
    TS = _TS[kind]
    nt = S // TS
    off = _OFFS[kind]
    W = off["W"]
    pnames = _PRM[kind]
    pvals = [prm[n] for n in pnames]
    sg_shapes = _small_grad_shapes(kind)
    sg_names = list(sg_shapes)
    n_saves = len(saves)

    def body(*refs):
        it = iter(refs)
        dxo_ref, z_ref, proj_ref, winT_ref, woutT_ref, lng_ref, kh_ref, khT_ref, vh_ref, vhT_ref = (next(it) for _ in range(10))
        p = {n: next(it) for n in pnames}
        sv = [next(it) for _ in range(n_saves)]
        dxin_ref, dproj_ref, mixed_ref, dy_ref, dln_ref, dk_ref, dv_ref = (next(it) for _ in range(7))
        sg = {n: next(it) for n in sg_names}
        rest = list(it)
        step = pl.program_id(0)
        i = nt - 1 - step

        @pl.when(step == 0)
        def _():
            dln_ref[...] = jnp.zeros_like(dln_ref)
            dk_ref[...] = jnp.zeros_like(dk_ref)
            dv_ref[...] = jnp.zeros_like(dv_ref)
            for n in sg_names:
                sg[n][...] = jnp.zeros_like(sg[n])

        dxo = dxo_ref[...]
        xhat, rstd = _ln(z_ref[...])
        dln_ref[0:1, :] += _rowsum(dxo * xhat)
        dln_ref[1:2, :] += _rowsum(dxo)
        dz = _ln_bwd(dxo * lng_ref[...], xhat, rstd)
        dyb = dz.astype(bf16)
        dy_ref[...] = dyb
        dmixed = _mm(dyb, woutT_ref[...])

        aux = {}
        if kind == 0:
            u_raw, v_raw = proj_ref[:, 0:TOK], proj_ref[:, TOK:2 * TOK]
            gu, tu, tv, vns, rstds, mx = _gmlp_core(u_raw, v_raw, p["wtri"], p["bcolb"])
            tok = gu * mx
        elif kind == 1:
            st_save, o_save = sv
            (dst_ref, fst_ref, states_s, dsts_s, u_s, qdec_s, kend_s, v_s, a_s, do_s, dqdec_s, dkend_s, dv_s,
             dgl_s) = rest

            @pl.when(step == 0)
            def _():
                dst_ref[...] = jnp.zeros_like(dst_ref)

            o = o_save[0]
            on, rs = _rms(o)
            tok = on * p["ng"][...]
            aux = dict(o=o, on=on, rs=rs)
        elif kind == 2:
            pc_save, = sv
            dpcar, = rest
            pp = proj_ref[:, 0:TOK]
            diff, inv_cnt, y = _pool_core(pp, pc_save[0], i * TS, p["wbd"])
            tok = y * p["scale"][...]
        else:
            cc_save, hc_save, h_save = sv
            dccar, gcar = rest
            xb = proj_ref[:, 0:TOK]
            lc = _lru_core(xb, cc_save[0], i * TS, p)
            hin = hc_save[0, SUB - 1:SUB, :]
            tok = h_save[0]

        xo, ps = _xattn_fwd(proj_ref[:, off["qx"]:off["qx"] + XW], khT_ref, vh_ref)
        gate = proj_ref[:, off["gate"]:off["gate"] + D_MODEL]
        sgm = _sig(gate)
        sgate = gate * sgm
        cat = jnp.concatenate([tok, xo], axis=-1)
        mixed_ref[...] = (cat * sgate).astype(bf16)
        dcat = dmixed * sgate
        dproj_ref[:, off["gate"]:off["gate"] + D_MODEL] = (dmixed * cat * (sgm * (1.0 + gate * (1.0 - sgm)))).astype(bf16)
        dtok = dcat[:, 0:TOK]
        dxo_att = dcat[:, TOK:]

        qx = proj_ref[:, off["qx"]:off["qx"] + XW]
        dqx = jnp.zeros((TS, XW), f32)
        for h in range(XHEADS):
            dp = _mm(dxo_att, vhT_ref[h])
            ds = ps[h] * (dp - jnp.sum(dp * ps[h], axis=-1, keepdims=True)) * (XDIM ** -0.5)
            dqx = dqx + _mm(ds, kh_ref[h])
            dk_ref[h] += _mm_tn(ds, qx)
            dv_ref[h] += _mm_tn(ps[h], dxo_att)
        dproj_ref[:, off["qx"]:off["qx"] + XW] = dqx.astype(bf16)

        if kind == 0:
            tril = lax.broadcasted_iota(jnp.int32, (HD, HD), 1) <= lax.broadcasted_iota(jnp.int32, (HD, HD), 0)
            dgu = dtok * mx
            dmx = dtok * gu
            dgvs = []
            for h in range(NH):
                dmh = dmx[:, _hs(h)]
                blks = []
                for n in range(TS // HD):
                    rs_ = slice(n * HD, (n + 1) * HD)
                    blks.append(_mm(p["wtriT"][h], dmh[rs_]))
                    sg["dwtri"][h] += jnp.where(tril, _mm_nt(dmh[rs_], vns[h][rs_]), 0.0)
                    sg["dbacc"][h] += dmh[rs_]
                dgvs.append(_ln_bwd(jnp.concatenate(blks, axis=0), vns[h], rstds[h]))
            dgv = jnp.concatenate(dgvs, axis=-1)
            dproj_ref[:, 0:TOK] = (dgu * _gelu_grad(u_raw, tu)).astype(bf16)
            dproj_ref[:, TOK:2 * TOK] = (dgv * _gelu_grad(v_raw, tv)).astype(bf16)
        elif kind == 1:
            o, on, rs = aux["o"], aux["on"], aux["rs"]
            ng = p["ng"][...]
            lb = p["lb"][...]
            sg["dng"][0:1, :] += _rowsum(dtok * on)
            dn = dtok * ng
            dos = []
            for h in range(NH):
                oh, r = o[:, _hs(h)], rs[h]
                dos.append(r * (dn[:, _hs(h)] - oh * (r * r) * _lmean(dn[:, _hs(h)] * oh)))
            do_all = jnp.concatenate(dos, axis=-1)
            _, tri = _chunk_mats(HD)
            same, _ = _chunk_mats(HGRN_SUB)
            triT = jnp.logical_and(same, lax.broadcasted_iota(jnp.int32, (HGRN_SUB, HGRN_SUB), 1)
                                   >= lax.broadcasted_iota(jnp.int32, (HGRN_SUB, HGRN_SUB), 0))
            row16 = lax.broadcasted_iota(jnp.int32, (CHUNK, HD), 0)
            nch = HGRN_SUB // CHUNK
            for sub in reversed(range(TS // HGRN_SUB)):
                rr = slice(sub * HGRN_SUB, (sub + 1) * HGRN_SUB)
                q_raw, v = proj_ref[rr, 0:TOK], proj_ref[rr, 2 * TOK:3 * TOK]
                hp = _hgrn_parallel(q_raw, proj_ref[rr, TOK:2 * TOK], lb)
                qdec_s[...] = hp["q_dec"]
                kend_s[...] = hp["k_end"]
                v_s[...] = v
                a_s[...] = hp["a"]
                fst_ref[...] = st_save[0, sub]
                _hgrn_inter_fwd(qdec_s, kend_s, v_s, a_s, None, fst_ref, states_s, u_s)
                do = do_all[rr]
                do_s[...] = do
                dqd, dki, dvi = [], [], []
                for h in range(NH):
                    bq, bk, bv = [], [], []
                    for b in range(HGRN_SUB // HD):
                        rs_ = slice(b * HD, (b + 1) * HD)
                        qd, ki = hp["q_dec"][rs_, _hs(h)], hp["k_inv"][rs_, _hs(h)]
                        sc = jnp.where(tri, _mm_nt(qd, ki), 0.0)
                        dsc = jnp.where(tri, _mm_nt(do[rs_, _hs(h)], v[rs_, _hs(h)]), 0.0)
                        bv.append(_mm_tn(sc, do[rs_, _hs(h)]))
                        bq.append(_mm(dsc, ki))
                        bk.append(_mm_tn(dsc, qd))
                    dqd.append(jnp.concatenate(bq, axis=0))
                    dki.append(jnp.concatenate(bk, axis=0))
                    dvi.append(jnp.concatenate(bv, axis=0))
                dqdec_s[...] = jnp.concatenate(dqd, axis=-1)
                dk_inv = jnp.concatenate(dki, axis=-1)
                dv_s[...] = jnp.concatenate(dvi, axis=-1)
                for c in range(nch):
                    for h in range(NH):
                        u_s[c, h] = _mm_tn(do_s[_cs(c), _hs(h)], qdec_s[_cs(c), _hs(h)])
                for h in range(NH):
                    dst = dst_ref[h]
                    for c in reversed(range(nch)):
                        dsts_s[c, h] = dst
                        dst = dst * a_s[c * CHUNK:c * CHUNK + 1, _hs(h)] + u_s[c, h]
                    dst_ref[h] = dst
                for c in range(nch):
                    for h in range(NH):
                        stp = states_s[c, h]
                        dst = dsts_s[c, h]
                        dqdec_s[_cs(c), _hs(h)] += _mm(do_s[_cs(c), _hs(h)], stp)
                        dkend_s[_cs(c), _hs(h)] = _mm(v_s[_cs(c), _hs(h)], dst)
                        dv_s[_cs(c), _hs(h)] += _mm_nt(kend_s[_cs(c), _hs(h)], dst)
                        da = jnp.sum(dst * stp, axis=0, keepdims=True) * a_s[c * CHUNK:c * CHUNK + 1, _hs(h)]
                        dgl_s[_cs(c), _hs(h)] = jnp.where(row16 == 0, jnp.broadcast_to(da, (CHUNK, HD)), 0.0)
                dq_dec = dqdec_s[...]
                dk_end = dkend_s[...]
                dg = dq_dec * hp["q_dec"] - dk_inv * hp["k_inv"] - dk_end * hp["k_end"]
                dk = dk_inv * hp["eng"] + dk_end * hp["ee"]
                dglr = dk_end * hp["k_end"] + dgl_s[...]
                dlogf = _mm_sel(triT, dg) + _mm_sel(same, dglr)
                df = dlogf / hp["f"] - dk
                sg["dlb"][0:1, :] += _rowsum(df * (1.0 - hp["sgm"]))
                dproj_ref[rr, 0:TOK] = (dq_dec * hp["eg"] * (hp["sq"] * (1.0 + q_raw * (1.0 - hp["sq"])))).astype(bf16)
                dproj_ref[rr, TOK:2 * TOK] = (df * (1.0 - lb) * hp["sgm"] * (1.0 - hp["sgm"])).astype(bf16)
                dproj_ref[rr, 2 * TOK:3 * TOK] = dv_s[...].astype(bf16)
        elif kind == 2:
            @pl.when(step == 0)
            def _():
                dpcar[...] = jnp.zeros_like(dpcar)

            sg["dscale"][0:1, :] += _rowsum(dtok * y)
            dyp = dtok * p["scale"][...]
            sg["dwbd"][...] += _mm_tn(diff, dyp)
            ddiff = _mm(dyp, p["wbdT"][...])
            q = ddiff * inv_cnt
            ext = jnp.concatenate([q, dpcar[...]], axis=0)
            n = TS + 2 * SUB
            r1 = ext + pltpu.roll(ext, n - 1, 0)
            r2 = r1 + pltpu.roll(r1, n - 2, 0)
            r3 = r2 + pltpu.roll(r2, n - 4, 0)
            r4 = r3 + pltpu.roll(r3, n - 8, 0)
            dproj_ref[:, 0:TOK] = (_pool_pick(r1, r2, r3, r4)[:TS] - ddiff).astype(bf16)
            dpcar[...] = q[0:2 * SUB, :]
        else:
            @pl.when(step == 0)
            def _():
                dccar[...] = jnp.zeros_like(dccar)
                gcar[...] = jnp.zeros_like(gcar)

            a, mult, gx, ga, xc = lc["a"], lc["mult"], lc["gx"], lc["ga"], lc["xc"]
            row = lax.broadcasted_iota(jnp.int32, (TS, TOK), 0)
            an = jnp.where(row == TS - 1, 1.0, pltpu.roll(a, TS - 1, 0))
            Pb, Bb = _scan_bwd(an, dtok)
            lam = Pb * gcar[0:1, :] + Bb
            gcar[...] = (a * lam)[0:SUB, :]
            hprev = jnp.where(row == 0, jnp.broadcast_to(hin, (TS, TOK)), pltpu.roll(tok, 1, 0))
            dmult = lam * gx * xc
            dgx = lam * mult * xc
            dxc = lam * mult * gx
            dla = lam * hprev * a - jnp.where(lc["first"], 0.0, dmult * a * a / mult)
            sp = lc["sp"]
            dga = -LRU_C * sp * dla
            dsp = _rowsum(-LRU_C * ga * dla)
            sg["dvec"][0:1, :] += dsp * (-_sig(-p["ap"][...]))
            dpx = dgx * gx * (1.0 - gx)
            dpa = dga * ga * (1.0 - ga)
            sg["dvec"][1:2, :] += _rowsum(dpx)
            sg["dvec"][2:3, :] += _rowsum(dpa)
            dxcs = []
            for h in range(NH):
                dxcs.append(_mm(dpx[:, _hs(h)], p["wgxT"][h]) + _mm(dpa[:, _hs(h)], p["wgaT"][h]))
                sg["dwgx"][h] += _mm_tn(xc[:, _hs(h)], dpx[:, _hs(h)])
                sg["dwga"][h] += _mm_tn(xc[:, _hs(h)], dpa[:, _hs(h)])
            dxc = dxc + jnp.concatenate(dxcs, axis=-1)
            sg["dvec"][3:4, :] += _rowsum(dxc)
            sg["dcw"][3:4, :] += _rowsum(dxc * xb)
            sg["dcw"][2:3, :] += _rowsum(dxc * lc["x1"])
            sg["dcw"][1:2, :] += _rowsum(dxc * lc["x2"])
            sg["dcw"][0:1, :] += _rowsum(dxc * lc["x3"])
            ext = jnp.concatenate([dxc, dccar[...]], axis=0)
            n = TS + SUB
            cw = p["cw"]
            dproj_ref[:, 0:TOK] = (cw[3:4, :] * dxc + cw[2:3, :] * pltpu.roll(ext, n - 1, 0)[:TS]
                                   + cw[1:2, :] * pltpu.roll(ext, n - 2, 0)[:TS]
                                   + cw[0:1, :] * pltpu.roll(ext, n - 3, 0)[:TS]).astype(bf16)
            dccar[...] = dxc[0:SUB, :]

        dxin_ref[...] = ALPHA * dz + _mm(dproj_ref[...], winT_ref[...])

    rtile = lambda w: pl.BlockSpec((TS, w), lambda s: (nt - 1 - s, 0))
    consts = [w_inT, w_outT, lng, kh, khT, vh, vhT] + pvals
    in_arrays = [dxout, z, proj] + consts + list(saves)
    in_specs = [rtile(D_MODEL), rtile(D_MODEL), rtile(W)] + [_const_spec(a) for a in consts]
    for a in saves:
        in_specs.append(pl.BlockSpec((1,) + a.shape[1:], lambda s, _n=a.ndim - 1: (nt - 1 - s,) + (0,) * _n))
    out_shape = [jax.ShapeDtypeStruct((S, D_MODEL), f32), jax.ShapeDtypeStruct((S, W), bf16),
                 jax.ShapeDtypeStruct((S, D_MODEL), bf16), jax.ShapeDtypeStruct((S, D_MODEL), bf16),
                 jax.ShapeDtypeStruct((SUB, D_MODEL), f32), jax.ShapeDtypeStruct((XHEADS, XW, XW), f32),
                 jax.ShapeDtypeStruct((XHEADS, XW, XW), f32)]
    out_specs = [rtile(D_MODEL), rtile(W), rtile(D_MODEL), rtile(D_MODEL), _acc_spec((SUB, D_MODEL)),
                 _acc_spec((XHEADS, XW, XW)), _acc_spec((XHEADS, XW, XW))]
    for n in sg_names:
        out_shape.append(jax.ShapeDtypeStruct(sg_shapes[n], f32))
        out_specs.append(_acc_spec(sg_shapes[n]))
    if kind == 1:
        scratch = ([pltpu.VMEM((NH, HD, HD), f32)] * 2 + [pltpu.VMEM((HGRN_SUB // CHUNK, NH, HD, HD), f32)] * 3
                   + [pltpu.VMEM((HGRN_SUB, TOK), f32)] * 9)
    elif kind == 2:
        scratch = [pltpu.VMEM((2 * SUB, TOK), f32)]
    elif kind == 3:
        scratch = [pltpu.VMEM((SUB, TOK), f32)] * 2
    else:
        scratch = []
    outs = pl.pallas_call(body, name=f"bwd_layer{kind}", grid=(nt,), in_specs=in_specs, out_specs=out_specs,
                          out_shape=out_shape, scratch_shapes=scratch, compiler_params=_params())(*in_arrays)
    return outs[:7], dict(zip(sg_names, outs[7:]))


def _prep(mem, w_kv, logits):
    def body(mem_ref, w_ref, lg_ref, kh_ref, khT_ref, vh_ref, vhT_ref, p_ref):
        kv = _mm(mem_ref[...], w_ref[...])
        k, v = kv[:, 0:XW], kv[:, XW:]
        kT, vT = k.T, v.T
        col = lax.broadcasted_iota(jnp.int32, (XW, XW), 1) // XDIM
        row = lax.broadcasted_iota(jnp.int32, (XW, XW), 0) // XDIM
        for h in range(XHEADS):
            kh_ref[h] = jnp.where(col == h, k, 0.0).astype(bf16)
            vh_ref[h] = jnp.where(col == h, v, 0.0).astype(bf16)
            khT_ref[h] = jnp.where(row == h, kT, 0.0).astype(bf16)
            vhT_ref[h] = jnp.where(row == h, vT, 0.0).astype(bf16)
        lg = lg_ref[...]
        e = jnp.exp(lg - jnp.max(lg, axis=0, keepdims=True))
        p_ref[...] = e / jnp.sum(e, axis=0, keepdims=True)

    vm = pl.BlockSpec(memory_space=pltpu.VMEM)
    hs = jax.ShapeDtypeStruct((XHEADS, XW, XW), bf16)
    return pl.pallas_call(body, name="prep_memory", in_specs=[vm] * 3, out_specs=[vm] * 5,
                          out_shape=[hs, hs, hs, hs, jax.ShapeDtypeStruct(logits.shape, f32)])(mem, w_kv, logits)


def _kv_bwd(mem, dks, dvs):
    def body(mem_ref, *refs):
        out_ref = refs[-1]
        col = lax.broadcasted_iota(jnp.int32, (XW, XW), 1) // XDIM
        dk = jnp.zeros((XW, XW), f32)
        dv = jnp.zeros((XW, XW), f32)
        for l in range(DEPTH):
            for h in range(XHEADS):
                dk = dk + jnp.where(col == h, refs[l][h], 0.0)
                dv = dv + jnp.where(col == h, refs[DEPTH + l][h], 0.0)
        out_ref[:, 0:XW] = _mm_tn(mem_ref[...], dk)
        out_ref[:, XW:] = _mm_tn(mem_ref[...], dv)

    vm = pl.BlockSpec(memory_space=pltpu.VMEM)
    return pl.pallas_call(body, name="kv_bwd", in_specs=[vm] * (1 + 2 * DEPTH), out_specs=vm,
                          out_shape=jax.ShapeDtypeStruct((D_MODEL, 2 * XW), f32))(mem, *dks, *dvs)


def _tn_gemm(a, b, name, nb):
    S, M = a.shape
    N = b.shape[1]
    NB = N // nb
    nk = S // TK

    def body(a_ref, b_ref, o_ref):
        @pl.when(pl.program_id(1) == 0)
        def _():
            o_ref[...] = jnp.zeros_like(o_ref)

        o_ref[...] += _mm_tn(a_ref[...], b_ref[...])

    return pl.pallas_call(body, name=name, grid=(nb, nk),
                          in_specs=[pl.BlockSpec((TK, M), lambda j, k: (k, 0)), pl.BlockSpec((TK, NB), lambda j, k: (k, j))],
                          out_specs=pl.BlockSpec((M, NB), lambda j, k: (0, j)),
                          out_shape=jax.ShapeDtypeStruct((M, N), f32),
                          compiler_params=pltpu.CompilerParams(dimension_semantics=("parallel", "arbitrary"),
                                                               vmem_limit_bytes=VMEM_LIMIT))(a, b)


def _rows_block(R, mult=16, cap=1024):
    best = R
    for d in range(mult, min(R, cap) + 1, mult):
        if R % d == 0:
            best = d
    return best


def _tn_gemm_sharded(a, b, name):
    S, M = a.shape
    Wq = b.shape[1] // 4
    nk = S // TK

    def body(a_ref, b_ref, o_ref):
        @pl.when(pl.program_id(0) == 0)
        def _():
            o_ref[...] = jnp.zeros_like(o_ref)

        at = a_ref[...].astype(MM)
        for j in range(4):
            o_ref[j] += _mm_tn(at, b_ref[:, j * Wq:(j + 1) * Wq])

    return pl.pallas_call(body, name=name, grid=(nk,),
                          in_specs=[pl.BlockSpec((TK, M), lambda k: (k, 0)), pl.BlockSpec((TK, 4 * Wq), lambda k: (k, 0))],
                          out_specs=pl.BlockSpec((4, M, Wq), lambda k: (0, 0, 0)),
                          out_shape=jax.ShapeDtypeStruct((4, M, Wq), f32), compiler_params=_params())(a, b)


def _tn_gemm_slab(a, b, acc, l, name):
    S, M = a.shape
    N = b.shape[1]
    nk = S // TK

    def body(a_ref, b_ref, *refs):
        o_ref = refs[-1]

        @pl.when(pl.program_id(0) == 0)
        def _():
            o_ref[...] = jnp.zeros_like(o_ref)

        o_ref[...] += _mm_tn(a_ref[...], b_ref[...])

    ins = [a, b] + ([] if acc is None else [acc])
    in_specs = [pl.BlockSpec((TK, M), lambda k: (k, 0)), pl.BlockSpec((TK, N), lambda k: (k, 0))]
    if acc is not None:
        in_specs.append(pl.BlockSpec(memory_space=pl.ANY))
    return pl.pallas_call(body, name=name, grid=(nk,), in_specs=in_specs,
                          out_specs=pl.BlockSpec((None, M, N), lambda k: (l, 0, 0)),
                          out_shape=jax.ShapeDtypeStruct((DEPTH, M, N), f32),
                          input_output_aliases={} if acc is None else {2: 0},
                          compiler_params=_params())(*ins)


HALF_ROWS = D_MODEL // 2
SHARD_ROWS = D_MODEL // 4


def _half_of_full(ref, kind, h):
    if kind == "kv":
        return ref.at[:, pl.ds(h * XW, XW)]
    if kind == "wout":
        return ref.at[pl.ds(2 * h, 2)]
    return ref.at[:, pl.ds(h * HALF_ROWS, HALF_ROWS)]


def _shard_of_half(ref, kind, j):
    if kind == "kv":
        return ref.at[pl.ds(j * SHARD_ROWS, SHARD_ROWS)]
    if kind == "wout":
        return ref.at[:, pl.ds(j * SHARD_ROWS, SHARD_ROWS)]
    return ref.at[j]


def _half_of_shard(ref, kind, h):
    if kind == "kv":
        return ref.at[:, pl.ds(h * XW, XW)]
    if kind == "wout":
        return ref.at[pl.ds(2 * h, 2)]
    rows = ref.shape[0] // 2
    return ref.at[pl.ds(h * rows, rows)]


def _half_shape(full_shape, kind):
    if kind == "kv":
        return (full_shape[0], XW)
    if kind == "wout":
        return (2,) + tuple(full_shape[1:])
    return (4, HALF_ROWS, full_shape[2])


def _shard_half_shape(full_shape, kind):
    if kind == "kv":
        return (SHARD_ROWS, XW)
    if kind == "wout":
        return (2, SHARD_ROWS, full_shape[2])
    return (HALF_ROWS, full_shape[2])


def _shard_shape(full_shape, kind):
    if kind == "kv":
        return (SHARD_ROWS, full_shape[1])
    if kind == "wout":
        return (DEPTH, SHARD_ROWS, full_shape[2])
    return (D_MODEL, full_shape[2])


def _ew_call(body, name, grid, jc, ins, in_specs, out_shape, out_specs):
    gs = pltpu.PrefetchScalarGridSpec(num_scalar_prefetch=1, grid=grid, in_specs=in_specs, out_specs=out_specs)
    return pl.pallas_call(body, name=name, grid_spec=gs, out_shape=out_shape,
                          compiler_params=pltpu.CompilerParams(dimension_semantics=("parallel",) * len(grid),
                                                               vmem_limit_bytes=VMEM_LIMIT))(jc, *ins)


def _add_sibling(part, got, kind, jc, name):
    def body(jc_ref, a_ref, b_ref, o_ref, ob_ref):
        s = a_ref[...] + b_ref[...]
        o_ref[...] = s
        ob_ref[...] = s.astype(bf16)

    if kind == "kv":
        R = part.shape[0]
        grid = (2,)
        mine = pl.BlockSpec((R // 2, XW), lambda i, jc_ref: (i, jc_ref[1]))
        spec = pl.BlockSpec((R // 2, XW), lambda i, jc_ref: (i, 0))
    elif kind == "wout":
        _, R, C = part.shape
        grid = (2, 2)
        mine = pl.BlockSpec((None, R // 2, C), lambda s, i, jc_ref: (2 * jc_ref[1] + s, i, 0))
        spec = pl.BlockSpec((None, R // 2, C), lambda s, i, jc_ref: (s, i, 0))
    else:
        C = part.shape[2]
        grid = (4, 2)
        mine = pl.BlockSpec((None, HALF_ROWS // 2, C), lambda s, i, jc_ref: (s, 2 * jc_ref[1] + i, 0))
        spec = pl.BlockSpec((None, HALF_ROWS // 2, C), lambda s, i, jc_ref: (s, i, 0))
    hs = _half_shape(part.shape, kind)
    return _ew_call(body, name, grid, jc, [part, got], [mine, spec],
                    [jax.ShapeDtypeStruct(hs, f32), jax.ShapeDtypeStruct(hs, bf16)], [spec, spec])


def _add_chips(q32, r, kind, jc, name):
    def body(jc_ref, q_ref, r_ref, out_ref):
        out_ref[...] = ((q_ref[...] + r_ref[0].astype(f32)) + r_ref[1].astype(f32)) + r_ref[2].astype(f32)

    if kind == "kv":
        grid = (1,)
        qs = pl.BlockSpec((SHARD_ROWS, XW), lambda i, jc_ref: (jc_ref[0], 0))
        rs = pl.BlockSpec((3, SHARD_ROWS, XW), lambda i, jc_ref: (0, 0, 0))
        os_ = pl.BlockSpec((SHARD_ROWS, XW), lambda i, jc_ref: (0, jc_ref[1]))
    elif kind == "wout":
        C = q32.shape[2]
        grid = (2,)
        qs = pl.BlockSpec((None, SHARD_ROWS, C), lambda s, jc_ref: (s, jc_ref[0], 0))
        rs = pl.BlockSpec((3, None, SHARD_ROWS, C), lambda s, jc_ref: (0, s, 0, 0))
        os_ = pl.BlockSpec((None, SHARD_ROWS, C), lambda s, jc_ref: (2 * jc_ref[1] + s, 0, 0))
    else:
        C = q32.shape[2]
        grid = (2,)
        qs = pl.BlockSpec((None, HALF_ROWS // 2, C), lambda i, jc_ref: (jc_ref[0], i, 0))
        rs = pl.BlockSpec((3, HALF_ROWS // 2, C), lambda i, jc_ref: (0, i, 0))
        os_ = pl.BlockSpec((HALF_ROWS // 2, C), lambda i, jc_ref: (2 * jc_ref[1] + i, 0))
    if kind == "kv":
        full_shape = (D_MODEL, 2 * XW)
    elif kind == "wout":
        full_shape = (DEPTH, D_MODEL, D_MODEL)
    else:
        full_shape = (4, D_MODEL, q32.shape[2])
    return _ew_call(body, name, grid, jc, [q32, r], [qs, rs], jax.ShapeDtypeStruct(_shard_shape(full_shape, kind), f32), os_)


def _adamw(w, g, m, v, name):
    R, C = w.shape
    br = _rows_block(R, mult=SUB, cap=512)
    c1 =1.0 / (1.0 - ADAM_B1 ** ADAM_STEP)
    c2 = 1.0 / (1.0 - ADAM_B2 ** ADAM_STEP)

    def body(w_ref, g_ref, m_ref, v_ref, d_ref, nm_ref, nv_ref):
        g_ = g_ref[...]
        nm = ADAM_B1 * m_ref[...] + (1.0 - ADAM_B1) * g_
        nv = ADAM_B2 * v_ref[...] + (1.0 - ADAM_B2) * (g_ * g_)
        nm_ref[...] = nm
        nv_ref[...] = nv
        d_ref[...] = -ADAM_LR * ((nm * c1) / (jnp.sqrt(nv * c2) + ADAM_EPS) + ADAM_WD * w_ref[...])

    spec = pl.BlockSpec((br, C), lambda i: (i, 0))
    sh = jax.ShapeDtypeStruct((R, C), f32)
    return pl.pallas_call(body, name=name, grid=(R // br,), in_specs=[spec] * 4, out_specs=[spec] * 3,
                          out_shape=[sh, sh, sh], compiler_params=_params("parallel"))(w, g, m, v)


def _small_finish(dbacc, p_soft, dlb):
    def body(db_ref, p_ref, dlb_ref, dbs_ref, dlg_ref):
        lane = lax.broadcasted_iota(jnp.int32, (HD, HD), 1)
        acc = jnp.zeros((HD, HD), f32)
        for h in range(NH):
            acc = acc + jnp.where(lane == h, jnp.sum(db_ref[h], axis=-1, keepdims=True), 0.0)
        dbs_ref[...] = acc
        p = p_ref[...]
        p1 = p[1:2, :]
        rowi = lax.broadcasted_iota(jnp.int32, p.shape, 0)
        dlg_ref[...] = dlb_ref[0:1, :] * p1 * (jnp.where(rowi == 1, 1.0, 0.0) - p)

    vm = pl.BlockSpec(memory_space=pltpu.VMEM)
    return pl.pallas_call(body, name="small_finish", in_specs=[vm] * 3, out_specs=[vm] * 2,
                          out_shape=[jax.ShapeDtypeStruct((HD, HD), f32), jax.ShapeDtypeStruct(p_soft.shape, f32)])(dbacc, p_soft, dlb)


def _where_am_i():
    return lax.axis_index("x"), lax.axis_index("y"), lax.axis_index("c")


MAX_PIECES = 8


def _nchunks(rows, mult):
    for n in range(MAX_PIECES, 0, -1):
        if rows % (n * mult) == 0:
            return n
    return 1


def _leading_pieces(src, dst):
    n = src.shape[0]
    if len(src.shape) >= 3 and n <= MAX_PIECES:
        return [(src.at[s], dst.at[s]) for s in range(n)]
    return [(src, dst)]


def _ag_weights(shards, kinds, jshard):
    n = len(shards)

    def body(*refs):
        sh_refs, out_refs = refs[:n], refs[2 * n:3 * n]
        send_sems, recv_sems = refs[3 * n:]
        x, y, c = _where_am_i()
        j = 2 * x + y
        sib = (x, y, 1 - c)
        chips = [(1 - x, y), (x, 1 - y), (1 - x, 1 - y)]

        def cp(k, src, dst, to):
            return pltpu.make_async_remote_copy(src_ref=src, dst_ref=dst, send_sem=send_sems.at[k], recv_sem=recv_sems.at[k],
                                                device_id=to, device_id_type=MESH)

        started = []
        for a in range(n):
            for k, (cx, cy) in enumerate(chips):
                d = cp(6 * a + k, _half_of_shard(sh_refs[a], kinds[a], c), _half_of_shard(out_refs[a].at[j], kinds[a], c), (cx, cy, c))
                d.start()
                started.append(d)
        for a in range(n):
            for k, (cx, cy) in enumerate(chips):
                blk = _half_of_shard(out_refs[a].at[2 * cx + cy], kinds[a], c)
                cp(6 * a + k, blk, blk, (cx, cy, c)).wait_recv()
                d = cp(6 * a + 3 + k, blk, blk, sib)
                d.start()
                started.append(d)
        for a in range(n):
            for k, (cx, cy) in enumerate(chips):
                blk = _half_of_shard(out_refs[a].at[2 * cx + cy], kinds[a], 1 - c)
                cp(6 * a + 3 + k, blk, blk, sib).wait_recv()
        for d in started:
            d.wait_send()

    placed = [lax.dynamic_update_slice(jnp.zeros((4,) + s.shape, s.dtype), s[None], (jshard,) + (0,) * s.ndim) for s in shards]
    anyspec = pl.BlockSpec(memory_space=pl.ANY)
    return pl.pallas_call(body, name="all_gather_weights", in_specs=[anyspec] * (2 * n), out_specs=[anyspec] * n,
                          out_shape=[jax.ShapeDtypeStruct(p.shape, p.dtype) for p in placed],
                          input_output_aliases={n + a: a for a in range(n)},
                          scratch_shapes=[pltpu.SemaphoreType.DMA((6 * n,)), pltpu.SemaphoreType.DMA((6 * n,))],
                          compiler_params=pltpu.CompilerParams(has_side_effects=True))(*shards, *placed)


_HBM = pl.BlockSpec(memory_space=pltpu.HBM)
_SEM = pl.BlockSpec(memory_space=pltpu.SEMAPHORE)
_FLOWING = pltpu.SideEffectType.DATAFLOW_SIDE_EFFECTING


def _peers6(x, y, c):
    chips = [(1 - x, y), (x, 1 - y), (1 - x, 1 - y)]
    return [(2 * k + e, chip, c if e == 0 else 1 - c) for k, chip in enumerate(chips) for e in range(2)]


def _ag_start(shards, jshard, name):
    n = len(shards)

    def body(*refs):
        out_refs = refs[2 * n:4 * n]
        send_sems, recv_sems, token = refs[4 * n:]
        x, y, c = _where_am_i()
        j = 2 * x + y
        for a in range(n):
            for slot, (cx, cy), tc in _peers6(x, y, c):
                pltpu.make_async_remote_copy(src_ref=_half_of_shard(out_refs[a], "win", c),
                                             dst_ref=_half_of_shard(out_refs[n + a].at[j], "win", c),
                                             send_sem=send_sems.at[6 * a + slot], recv_sem=recv_sems.at[6 * a + slot],
                                             device_id=(cx, cy, tc), device_id_type=MESH).start()
        token[...] = jnp.zeros_like(token)

    placed = [lax.dynamic_update_slice(jnp.zeros((4,) + s.shape, s.dtype), s[None], (jshard,) + (0,) * s.ndim) for s in shards]
    hbm = lambda t: pltpu.with_memory_space_constraint(t, pltpu.HBM)
    both = list(shards) + placed
    outs = pl.pallas_call(
        body, name=name, in_specs=[_HBM] * (2 * n), out_specs=[_HBM] * (2 * n) + [_SEM, _SEM, pl.BlockSpec(memory_space=pltpu.VMEM)],
        out_shape=[pltpu.HBM(p.shape, p.dtype) for p in both] + [pltpu.SemaphoreType.DMA((6 * n,)), pltpu.SemaphoreType.DMA((6 * n,)),
                                                                jax.ShapeDtypeStruct((SUB, LANE), f32)],
        input_output_aliases={a: a for a in range(2 * n)},
        compiler_params=pltpu.CompilerParams(has_side_effects=_FLOWING))(*[hbm(t) for t in both])
    return outs[:2 * n], outs[2 * n], outs[2 * n + 1], outs[2 * n + 2]


def _ag_wait(bufs, send_sems, recv_sems, after, name):
    n = len(bufs) // 2

    def body(*refs):
        sh_refs, g_refs = refs[:n], refs[n:2 * n]
        send_sems, recv_sems = refs[2 * n], refs[2 * n + 1]
        x, y, c = _where_am_i()
        for a in range(n):
            for slot, (cx, cy), tc in _peers6(x, y, c):
                cp = pltpu.make_async_remote_copy(src_ref=_half_of_shard(sh_refs[a], "win", c),
                                                  dst_ref=_half_of_shard(g_refs[a].at[2 * cx + cy], "win", tc),
                                                  send_sem=send_sems.at[6 * a + slot], recv_sem=recv_sems.at[6 * a + slot],
                                                  device_id=(cx, cy, tc), device_id_type=MESH)
                cp.wait_send()
                cp.wait_recv()

    outs = pl.pallas_call(body, name=name, in_specs=[_HBM] * (2 * n) + [_SEM, _SEM, pl.BlockSpec(memory_space=pl.ANY)],
                          out_specs=[_HBM] * (2 * n), out_shape=[pltpu.HBM(b.shape, b.dtype) for b in bufs],
                          input_output_aliases={a: a for a in range(2 * n)},
                          compiler_params=pltpu.CompilerParams(has_side_effects=_FLOWING))(*bufs, send_sems, recv_sems, after)
    return outs[n:]


def _rs_swap(parts, kinds):
    n = len(parts)

    def body(*refs):
        p_refs, got_refs = refs[:n], refs[n:2 * n]
        send_sems, recv_sems = refs[2 * n:]
        x, y, c = _where_am_i()

        def cp(a, src, dst):
            return pltpu.make_async_remote_copy(src_ref=src, dst_ref=dst, send_sem=send_sems.at[a], recv_sem=recv_sems.at[a],
                                                device_id=(x, y, 1 - c), device_id_type=MESH)

        for a in range(n):
            for src, dst in _leading_pieces(_half_of_full(p_refs[a], kinds[a], 1 - c), got_refs[a]):
                cp(a, src, dst).start()
        for a in range(n):
            cp(a, got_refs[a], got_refs[a]).wait()

    anyspec = pl.BlockSpec(memory_space=pl.ANY)
    return pl.pallas_call(body, name="rs_swap_halves", in_specs=[anyspec] * n, out_specs=[anyspec] * n,
                          out_shape=[jax.ShapeDtypeStruct(_half_shape(p.shape, k), p.dtype) for p, k in zip(parts, kinds)],
                          scratch_shapes=[pltpu.SemaphoreType.DMA((n,)), pltpu.SemaphoreType.DMA((n,))],
                          compiler_params=pltpu.CompilerParams(has_side_effects=True))(*parts)


def _rs_owners(qbs, kinds, full_shapes):
    n = len(qbs)

    def body(*refs):
        q_refs, got_refs = refs[:n], refs[n:2 * n]
        send_sems, recv_sems = refs[2 * n:]
        x, y, c = _where_am_i()
        chips = [(1 - x, y), (x, 1 - y), (1 - x, 1 - y)]
        ds = []
        for a in range(n):
            for k, (cx, cy) in enumerate(chips):
                d = pltpu.make_async_remote_copy(src_ref=_shard_of_half(q_refs[a], kinds[a], 2 * cx + cy), dst_ref=got_refs[a].at[k],
                                                 send_sem=send_sems.at[3 * a + k], recv_sem=recv_sems.at[3 * a + k],
                                                 device_id=(cx, cy, c), device_id_type=MESH)
                d.start()
                ds.append(d)
        for d in ds:
            d.wait()

    anyspec = pl.BlockSpec(memory_space=pl.ANY)
    return pl.pallas_call(body, name="rs_to_owners", in_specs=[anyspec] * n, out_specs=[anyspec] * n,
                          out_shape=[jax.ShapeDtypeStruct((3,) + _shard_half_shape(fs, k), bf16) for fs, k in zip(full_shapes, kinds)],
                          scratch_shapes=[pltpu.SemaphoreType.DMA((3 * n,)), pltpu.SemaphoreType.DMA((3 * n,))],
                          compiler_params=pltpu.CompilerParams(has_side_effects=True))(*qbs)


def _rs_join(bufs, kinds):
    n = len(bufs)

    def body(*refs):
        out_refs = refs[n:2 * n]
        send_sems, recv_sems = refs[2 * n:]
        x, y, c = _where_am_i()

        def cp(a, h):
            blk = _half_of_shard(out_refs[a], kinds[a], h)
            return pltpu.make_async_remote_copy(src_ref=blk, dst_ref=blk, send_sem=send_sems.at[a], recv_sem=recv_sems.at[a],
                                                device_id=(x, y, 1 - c), device_id_type=MESH)

        for a in range(n):
            cp(a, c).start()
        for a in range(n):
            cp(a, c).wait_send()
            cp(a, 1 - c).wait_recv()

    anyspec = pl.BlockSpec(memory_space=pl.ANY)
    return pl.pallas_call(body, name="rs_join_halves", in_specs=[anyspec] * n, out_specs=[anyspec] * n,
                          out_shape=[jax.ShapeDtypeStruct(b.shape, b.dtype) for b in bufs],
                          input_output_aliases={a: a for a in range(n)},
                          scratch_shapes=[pltpu.SemaphoreType.DMA((n,)), pltpu.SemaphoreType.DMA((n,))],
                          compiler_params=pltpu.CompilerParams(has_side_effects=True))(*bufs)


def _all_reduce_small(g):
    R, C = g.shape
    H = R // 2
    NP = _nchunks(H, SUB)
    PR = H // NP

    def body(g_ref, out_ref, sib_ref, chip_ref, send_sems, recv_sems):
        x, y, c = _where_am_i()
        j = 2 * x + y
        sib = (x, y, 1 - c)
        chips = [(1 - x, y), (x, 1 - y), (1 - x, 1 - y)]
        rows = pl.ds(pl.multiple_of(c * H, SUB), H)

        def cp(k, src, dst, to):
            return pltpu.make_async_remote_copy(src_ref=src, dst_ref=dst, send_sem=send_sems.at[k], recv_sem=recv_sems.at[k],
                                                device_id=to, device_id_type=MESH)

        def pieces(k, src, dst, to):
            for q in range(NP):
                cp(k, src.at[pl.ds(q * PR, PR)], dst.at[pl.ds(q * PR, PR)], to).start()

        for half in range(2):
            pieces(0, g_ref.at[pl.ds(half * H, H)], sib_ref.at[pl.ds(half * H, H)], sib)
        cp(0, g_ref, sib_ref, sib).wait()
        chip_ref[j] = g_ref[rows, :] + sib_ref[rows, :]
        for k, (cx, cy) in enumerate(chips):
            pieces(1 + k, chip_ref.at[j], chip_ref.at[j], (cx, cy, c))
        for k, (cx, cy) in enumerate(chips):
            blk = chip_ref.at[2 * cx + cy]
            cp(1 + k, blk, blk, (cx, cy, c)).wait()
        out_ref[rows, :] = ((chip_ref[0] + chip_ref[1]) + chip_ref[2]) + chip_ref[3]
        other = out_ref.at[pl.ds(pl.multiple_of((1 - c) * H, SUB), H)]
        pieces(4, out_ref.at[rows], out_ref.at[rows], sib)
        cp(4, other, other, sib).wait()

    vm = pl.BlockSpec(memory_space=pltpu.VMEM)
    return pl.pallas_call(body, name="all_reduce_small", in_specs=[vm], out_specs=vm,
                          out_shape=jax.ShapeDtypeStruct((R, C), f32),
                          scratch_shapes=[pltpu.VMEM((R, C), f32), pltpu.VMEM((4, H, C), f32),
                                          pltpu.SemaphoreType.DMA((5,)), pltpu.SemaphoreType.DMA((5,))],
                          compiler_params=pltpu.CompilerParams(has_side_effects=True, vmem_limit_bytes=VMEM_LIMIT))(g)


SPLIT_MIN_ELEMS = 1 << 16


def _all_reduce_many(gs):
    n = len(gs)
    split = [g.ndim == 3 and g.shape[0] % 2 == 0 and g.size >= SPLIT_MIN_ELEMS for g in gs]
    part_shape = [((g.shape[0] // 2,) + g.shape[1:]) if s else g.shape for g, s in zip(gs, split)]
    n_split = sum(split)

    def body(*refs):
        g, out, sibs, chipb = refs[:n], refs[n:2 * n], refs[2 * n:3 * n], refs[3 * n:4 * n]
        send_sems, recv_sems = refs[4 * n:]
        x, y, c = _where_am_i()
        j = 2 * x + y
        sib = (x, y, 1 - c)
        chips = [(1 - x, y), (x, 1 - y), (1 - x, 1 - y)]

        def cp(k, src, dst, to):
            return pltpu.make_async_remote_copy(src_ref=src, dst_ref=dst, send_sem=send_sems.at[k], recv_sem=recv_sems.at[k],
                                                device_id=to, device_id_type=MESH)

        def part(a, h):
            return pl.ds(h * part_shape[a][0], part_shape[a][0]) if split[a] else Ellipsis

        def mine(ref, a, h):
            return ref.at[part(a, h)] if split[a] else ref

        swaps = [cp(a, g[a], sibs[a], sib) for a in range(n)]
        for d in swaps:
            d.start()
        for a in range(n):
            swaps[a].wait()
            chipb[a][j] = g[a][part(a, c)] + sibs[a][part(a, c)]
        sends = [cp(n + 3 * a + k, chipb[a].at[j], chipb[a].at[j], (cx, cy, c)) for a in range(n) for k, (cx, cy) in enumerate(chips)]
        for d in sends:
            d.start()
        for a in range(n):
            for k, (cx, cy) in enumerate(chips):
                blk = chipb[a].at[2 * cx + cy]
                cp(n + 3 * a + k, blk, blk, (cx, cy, c)).wait_recv()
            out[a][part(a, c)] = ((chipb[a][0] + chipb[a][1]) + chipb[a][2]) + chipb[a][3]
        for d in sends:
            d.wait_send()
        backs = [(a, cp(4 * n + i, mine(out[a], a, c), mine(out[a], a, c), sib)) for i, a in enumerate([a for a in range(n) if split[a]])]
        for _, d in backs:
            d.start()
        for i, (a, d) in enumerate(backs):
            d.wait_send()
            cp(4 * n + i, mine(out[a], a, 1 - c), mine(out[a], a, 1 - c), sib).wait_recv()

    vm = pl.BlockSpec(memory_space=pltpu.VMEM)
    nsem = 4 * n + n_split
    return pl.pallas_call(body, name="all_reduce_small_grads", in_specs=[vm] * n, out_specs=[vm] * n,
                          out_shape=[jax.ShapeDtypeStruct(g.shape, f32) for g in gs],
                          scratch_shapes=([pltpu.VMEM(g.shape, f32) for g in gs] + [pltpu.VMEM((4,) + ps, f32) for ps in part_shape]
                                          + [pltpu.SemaphoreType.DMA((nsem,)), pltpu.SemaphoreType.DMA((nsem,))]),
                          compiler_params=pltpu.CompilerParams(has_side_effects=True, vmem_limit_bytes=VMEM_LIMIT))(*gs)


def _adamw_many(ws, gs, ms, vs, name):
    n = len(ws)
    c1 = 1.0 / (1.0 - ADAM_B1 ** ADAM_STEP)
    c2 = 1.0 / (1.0 - ADAM_B2 ** ADAM_STEP)

    def body(*refs):
        for a in range(n):
            w_ref, g_ref, m_ref, v_ref, d_ref, nm_ref, nv_ref = (refs[i * n + a] for i in range(7))
            g_ = g_ref[...]
            nm = ADAM_B1 * m_ref[...] + (1.0 - ADAM_B1) * g_
            nv = ADAM_B2 * v_ref[...] + (1.0 - ADAM_B2) * (g_ * g_)
            nm_ref[...] = nm
            nv_ref[...] = nv
            d_ref[...] = -ADAM_LR * ((nm * c1) / (jnp.sqrt(nv * c2) + ADAM_EPS) + ADAM_WD * w_ref[...])

    vm = pl.BlockSpec(memory_space=pltpu.VMEM)
    sh = [jax.ShapeDtypeStruct(w.shape, f32) for w in ws]
    outs = pl.pallas_call(body, name=name, in_specs=[vm] * (4 * n), out_specs=[vm] * (3 * n), out_shape=sh * 3,
                          compiler_params=pltpu.CompilerParams(vmem_limit_bytes=VMEM_LIMIT))(*ws, *gs, *ms, *vs)
    return outs[:n], outs[n:2 * n], outs[2 * n:]


def _pack_flat(arrs, rows_mult):
    flat = jnp.concatenate([a.reshape(-1) for a in arrs])
    n = flat.shape[0]
    tot = -(-n // (rows_mult * LANE)) * rows_mult * LANE
    return jnp.pad(flat, (0, tot - n)).reshape(-1, LANE)


def _unpack_flat(buf, shapes):
    flat = buf.reshape(-1)
    out, o = [], 0
    for s in shapes:
        n = math.prod(s)
        out.append(flat[o:o + n].reshape(s))
        o += n
    return out


_BIG = ("mem_kv_w", "w_out", "a_w_in", "b_w_in", "c_w_in", "d_w_in")
SMALL_ROWS_MULT = 256


def _row8(v):
    v = v.reshape(-1, v.shape[-1])
    return jnp.pad(v, ((0, SUB - v.shape[0]), (0, 0)))


def kernel(x, mem, mem_kv_w, ln_g, ln_b, w_out, hgrn_lb_logits, a_w_in, a_w_s, a_b_s, b_w_in, b_norm_g, c_w_in, c_w_pool, c_scale, d_w_in, d_conv_w, d_conv_b, d_w_gx, d_b_gx, d_w_ga, d_b_ga, d_a_param, loss_target, m_mem_kv_w, m_ln_g, m_ln_b, m_w_out, m_hgrn_lb_logits, m_a_w_in, m_a_w_s, m_a_b_s, m_b_w_in, m_b_norm_g, m_c_w_in, m_c_w_pool, m_c_scale, m_d_w_in, m_d_conv_w, m_d_conv_b, m_d_w_gx, m_d_b_gx, m_d_w_ga, m_d_b_ga, m_d_a_param, v_mem_kv_w, v_ln_g, v_ln_b, v_w_out, v_hgrn_lb_logits, v_a_w_in, v_a_w_s, v_a_b_s, v_b_w_in, v_b_norm_g, v_c_w_in, v_c_w_pool, v_c_scale, v_d_w_in, v_d_conv_w, v_d_conv_b, v_d_w_gx, v_d_b_gx, v_d_w_ga, v_d_b_ga, v_d_a_param):
    names = ["mem_kv_w", "ln_g", "ln_b", "w_out", "hgrn_lb_logits", "a_w_in", "a_w_s", "a_b_s", "b_w_in", "b_norm_g", "c_w_in",
             "c_w_pool", "c_scale", "d_w_in", "d_conv_w", "d_conv_b", "d_w_gx", "d_b_gx", "d_w_ga", "d_b_ga", "d_a_param"]
    w = dict(mem_kv_w=mem_kv_w, ln_g=ln_g, ln_b=ln_b, w_out=w_out, hgrn_lb_logits=hgrn_lb_logits, a_w_in=a_w_in, a_w_s=a_w_s,
             a_b_s=a_b_s, b_w_in=b_w_in, b_norm_g=b_norm_g, c_w_in=c_w_in, c_w_pool=c_w_pool, c_scale=c_scale, d_w_in=d_w_in,
             d_conv_w=d_conv_w, d_conv_b=d_conv_b, d_w_gx=d_w_gx, d_b_gx=d_b_gx, d_w_ga=d_w_ga, d_b_ga=d_b_ga, d_a_param=d_a_param)
    m = dict(zip(names, [m_mem_kv_w, m_ln_g, m_ln_b, m_w_out, m_hgrn_lb_logits, m_a_w_in, m_a_w_s, m_a_b_s, m_b_w_in, m_b_norm_g,
                         m_c_w_in, m_c_w_pool, m_c_scale, m_d_w_in, m_d_conv_w, m_d_conv_b, m_d_w_gx, m_d_b_gx, m_d_w_ga,
                         m_d_b_ga, m_d_a_param]))
    v = dict(zip(names, [v_mem_kv_w, v_ln_g, v_ln_b, v_w_out, v_hgrn_lb_logits, v_a_w_in, v_a_w_s, v_a_b_s, v_b_w_in, v_b_norm_g,
                         v_c_w_in, v_c_w_pool, v_c_scale, v_d_w_in, v_d_conv_w, v_d_conv_b, v_d_w_gx, v_d_b_gx, v_d_w_ga,
                         v_d_b_ga, v_d_a_param]))
    xi, yi = lax.axis_index("x"), lax.axis_index("y")
    jshard = 2 * xi + yi
    x2 = x[0]
    mem2 = mem[0]
    tgt2 = loss_target[0]

    kinds = ("kv", "wout", "win", "win", "win", "win")
    w_in_sh = [w[n][0].astype(bf16) for n in _BIG[2:]]
    w_out_sh = w_out.astype(bf16)
    gath0 = _ag_weights([mem_kv_w.astype(bf16), w_out_sh[0], w_in_sh[0]], ("kv", "win", "win"), jshard)
    w_kv = gath0[0].reshape(D_MODEL, 2 * XW)
    pending = [None]
    tie = gath0[0]
    for l in range(1, DEPTH):
        shards, _ = lax.optimization_barrier(([w_in_sh[l], w_out_sh[l]], tie))
        bufs, ssem, rsem, tie = _ag_start(shards, jshard, f"gather_start{l}")
        pending.append((bufs, ssem, rsem))
    x2, _ = lax.optimization_barrier((x2, tie))

    def layer_weights(g_in, g_out):
        return (g_in.transpose(1, 0, 2).reshape(D_MODEL, -1), g_in.transpose(0, 2, 1).reshape(-1, D_MODEL),
                g_out.reshape(D_MODEL, D_MODEL), g_out.transpose(2, 0, 1).reshape(D_MODEL, D_MODEL))

    lw = [layer_weights(gath0[2], gath0[1])]

    def gather_small(shard):
        z = jnp.zeros((4, POOL_GROUP), f32)
        return lax.dynamic_update_slice(z, shard.reshape(1, POOL_GROUP), (jshard, 0))

    sm_sh = jnp.concatenate([gather_small(b_norm_g), gather_small(c_scale), gather_small(d_conv_b), gather_small(d_a_param)]
                            + [gather_small(d_conv_w[:, r]) for r in range(4)], axis=0)
    ci = lax.axis_index("c")
    sm_all = _all_reduce_small(_pack_flat([jnp.where(ci == 0, sm_sh, 0.0)], SUB * 2))
    sm = _unpack_flat(sm_all, [(8, 4 * POOL_GROUP)])[0]
    ng_full, scale_full, convb_full, ap_full = sm[0:1], sm[1:2], sm[2:3], sm[3:4]
    convw_full = sm[4:8]

    tril = jnp.tril(jnp.ones((HD, HD), bool))
    wtri = jnp.where(tril, a_w_s[0], 0.0)
    wbd = jnp.zeros((TOK, TOK), f32)
    for g in range(4):
        wbd = lax.dynamic_update_slice(wbd, c_w_pool[0, g], (g * POOL_GROUP, g * POOL_GROUP))
    kh, khT, vh, vhT, p_soft = _prep(mem2, w_kv, hgrn_lb_logits)
    prm = [
        dict(wtri=wtri.astype(bf16), wtriT=wtri.transpose(0, 2, 1).astype(bf16),
             bcolb=jnp.broadcast_to(a_b_s[0][:, :, None], (NH, HD, HD))),
        dict(lb=p_soft[1:2], ng=ng_full),
        dict(wbd=wbd.astype(bf16), wbdT=wbd.T.astype(bf16), scale=scale_full),
        dict(cw=_row8(convw_full), cb=convb_full, wgx=d_w_gx[0].astype(bf16), wgxT=d_w_gx[0].transpose(0, 2, 1).astype(bf16),
             bgx=d_b_gx.reshape(1, TOK), wga=d_w_ga[0].astype(bf16), wgaT=d_w_ga[0].transpose(0, 2, 1).astype(bf16),
             bga=d_b_ga.reshape(1, TOK), ap=ap_full),
    ]

    acts = []
    h = x2
    for l in range(DEPTH):
        if l:
            bufs, ssem, rsem = pending[l]
            lw.append(layer_weights(*_ag_wait(bufs, ssem, rsem, h, f"gather_wait{l}")))
        outs = _fwd_layer(l, h, lw[l][0], lw[l][2], ln_g[l:l + 1], ln_b[l:l + 1], khT, vh, prm[l],
                          tgt2 if l == DEPTH - 1 else None)
        nfix = 4 if l == DEPTH - 1 else 3
        acts.append(dict(xin=h, proj=outs[1], z=outs[2], saves=outs[nfix:]))
        if l == DEPTH - 1:
            loss_part = outs[3]
        h = outs[0]
    loss = lax.psum(0.5 / D_MODEL * jnp.sum(loss_part), ("x", "y", "c"))

    dh = h
    gw_in = [None] * DEPTH
    gw_out = None
    dln = [None] * DEPTH
    dks, dvs = [None] * DEPTH, [None] * DEPTH
    sgr = [None] * DEPTH
    for l in reversed(range(DEPTH)):
        a = acts[l]
        (dxin, dproj, mixedb, dyb, dln[l], dks[l], dvs[l]), sgr[l] = _bwd_layer(
            l, dh, a["z"], a["proj"], lw[l][1], lw[l][3], ln_g[l:l + 1], kh, khT, vh, vhT, prm[l], a["saves"])
        if _OFFS[l]["W"] // 4 % LANE:
            gw_in[l] = _tn_gemm(a["xin"], dproj, f"grad_w_in{l}", 1).reshape(D_MODEL, 4, -1).transpose(1, 0, 2)
        else:
            gw_in[l] = _tn_gemm_sharded(a["xin"], dproj, f"grad_w_in{l}")
        gw_out = _tn_gemm_slab(mixedb, dyb, gw_out, l, f"grad_w_out{l}")
        dh = dxin
    grad_x = dh[None]

    parts = [_kv_bwd(mem2, dks, dvs), gw_out] + gw_in
    jc = jnp.stack([jshard, ci]).astype(jnp.int32)
    gots = _rs_swap(parts, kinds)
    sums = [_add_sibling(p, g, k, jc, f"rs_add_sibling{a}") for a, (p, g, k) in enumerate(zip(parts, gots, kinds))]
    gots2 = _rs_owners([s[1] for s in sums], kinds, [p.shape for p in parts])
    gbig = _rs_join([_add_chips(s[0], r, k, jc, f"rs_add_chips{a}") for a, (s, r, k) in enumerate(zip(sums, gots2, kinds))], kinds)
    g_sh, d_sh, m_sh, v_sh = {}, {}, {}, {}
    for a, n in enumerate(_BIG):
        as2d = lambda t: t.reshape(-1, t.shape[-1])
        upd = _adamw(as2d(w[n]), as2d(gbig[a]), as2d(m[n]), as2d(v[n]), f"adamw_{n}")
        g_sh[n] = gbig[a].reshape(w[n].shape)
        d_sh[n], m_sh[n], v_sh[n] = (u.reshape(w[n].shape) for u in upd)

    dbs, dlogits = _small_finish(sgr[0]["dbacc"], p_soft, sgr[1]["dlb"])
    gs = {
        "ln_g": jnp.concatenate([dln[l][0:1] for l in range(DEPTH)], axis=0),
        "ln_b": jnp.concatenate([dln[l][1:2] for l in range(DEPTH)], axis=0),
        "hgrn_lb_logits": dlogits,
        "a_w_s": sgr[0]["dwtri"][None],
        "a_b_s": dbs[:, 0:NH].T[None],
        "b_norm_g": sgr[1]["dng"][0:1],
        "c_w_pool": jnp.stack([sgr[2]["dwbd"][g * POOL_GROUP:(g + 1) * POOL_GROUP, g * POOL_GROUP:(g + 1) * POOL_GROUP]
                               for g in range(4)])[None],
        "c_scale": sgr[2]["dscale"][0:1],
        "d_conv_w": sgr[3]["dcw"][0:4][None],
        "d_conv_b": sgr[3]["dvec"][3:4],
        "d_w_gx": sgr[3]["dwgx"][None],
        "d_b_gx": sgr[3]["dvec"][1:2].reshape(1, NH, HD),
        "d_w_ga": sgr[3]["dwga"][None],
        "d_b_ga": sgr[3]["dvec"][2:3].reshape(1, NH, HD),
        "d_a_param": sgr[3]["dvec"][0:1],
    }
    small = [n for n in names if n not in _BIG]
    drop1 = lambda t: t.reshape(t.shape[1:]) if t.ndim > 2 and t.shape[0] == 1 else t
    gsum = dict(zip(small, _all_reduce_many([drop1(gs[n]) for n in small])))
    for n in ("b_norm_g", "c_scale", "d_conv_b", "d_a_param"):
        gsum[n] = lax.dynamic_slice(gsum[n], (0, jshard * POOL_GROUP), (1, POOL_GROUP))
    gsum["d_conv_w"] = lax.dynamic_slice(gsum["d_conv_w"], (0, jshard * POOL_GROUP), (4, POOL_GROUP))
    upd = _adamw_many(*[[drop1(d[n]) for n in small] for d in (w, gsum, m, v)], "adamw_small")
    gsum = {n: gsum[n].reshape(w[n].shape) for n in small}
    d_sm, m_sm, v_sm = ({n: u.reshape(w[n].shape) for n, u in zip(small, us)} for us in upd)

    grads = {**gsum, **g_sh}
    deltas = {**d_sm, **d_sh}
    new_m = {**m_sm, **m_sh}
    new_v = {**v_sm, **v_sh}
    return (loss, grad_x, *[grads[n] for n in names], *[deltas[n] for n in names], *[new_m[n] for n in names],
            *[new_v[n] for n in names])
```

```python
import functools
import math

import jax
import jax.numpy as jnp
from jax import lax
from jax.experimental import pallas as pl
from jax.experimental.pallas import tpu as pltpu

f32 = jnp.float32
bf16 = jnp.bfloat16
MM = bf16

D_MODEL = 1024
TOK = 768
XW = 256
XHEADS = 4
XDIM = 64
HD = 128
NH = TOK // HD
CHUNK = 16
POOL_GROUP = 192
DEPTH = 4
ALPHA = (2 * DEPTH) ** 0.25
LN_EPS = 1e-5
RMS_EPS = 1e-6
LRU_C = 8.0
ADAM_LR, ADAM_B1, ADAM_B2, ADAM_EPS, ADAM_WD, ADAM_STEP = 0.001, 0.9, 0.999, 1e-08, 0.01, 10

_TS = (256, 256, 256, 256)
HGRN_SUB = 128
TK = 512
SUB = 8
LANE = 128
VMEM_LIMIT = 58 * 1024 * 1024

_OFFS = (
    dict(u=0, v=768, qx=1536, gate=1792, W=2816),
    dict(q=0, f=768, i=1536, qx=2304, gate=2560, W=3584),
    dict(p=0, qx=768, gate=1024, W=2048),
    dict(xb=0, qx=768, gate=1024, W=2048),
)
_PRM = (
    ("wtri", "wtriT", "bcolb"),
    ("lb", "ng"),
    ("wbd", "wbdT", "scale"),
    ("cw", "cb", "wgx", "wgxT", "bgx", "wga", "wgaT", "bga", "ap"),
)
MESH = pl.DeviceIdType.MESH


def _mm(a, b):
    return jnp.dot(a.astype(MM), b.astype(MM), preferred_element_type=f32)


def _mm_nt(a, b):
    return lax.dot_general(a.astype(MM), b.astype(MM), (((1,), (1,)), ((), ())), preferred_element_type=f32)


def _mm_tn(a, b):
    return lax.dot_general(a.astype(MM), b.astype(MM), (((0,), (0,)), ((), ())), preferred_element_type=f32)


def _mm_sel(sel, b):
    s = sel.astype(bf16)
    hi = b.astype(bf16)
    lo = (b - hi.astype(f32)).astype(bf16)
    return jnp.dot(s, hi, preferred_element_type=f32) + jnp.dot(s, lo, preferred_element_type=f32)


def _sig(x):
    return jax.nn.sigmoid(x)


_GC = math.sqrt(2.0 / math.pi)


def _gelu(x):
    t = jnp.tanh(_GC * (x + 0.044715 * x * x * x))
    return 0.5 * x * (1.0 + t), t


def _gelu_grad(x, t):
    return 0.5 * (1.0 + t) + 0.5 * x * (1.0 - t * t) * _GC * (1.0 + 3.0 * 0.044715 * x * x)


def _rowsum(x):
    return jnp.sum(x, axis=0, keepdims=True)


def _lmean(x):
    return jnp.mean(x, axis=-1, keepdims=True)


def _ln(z):
    mu = _lmean(z)
    zc = z - mu
    rstd = lax.rsqrt(_lmean(zc * zc) + LN_EPS)
    return zc * rstd, rstd


def _ln_bwd(dxh, xhat, rstd):
    return rstd * (dxh - _lmean(dxh) - xhat * _lmean(dxh * xhat))


def _hs(h):
    return slice(h * HD, (h + 1) * HD)


def _expm1(x):
    small = x * (1.0 + x * 0.5 * (1.0 + x * (1.0 / 3.0) * (1.0 + x * 0.25 * (1.0 + x * 0.2 * (1.0 + x * (1.0 / 6.0))))))
    return jnp.where(jnp.abs(x) < 0.25, small, jnp.exp(x) - 1.0)


def _softplus(x):
    e = jnp.exp(-jnp.abs(x))
    l1p = jnp.where(e < 1e-4, e - 0.5 * e * e, jnp.log(1.0 + e))
    return jnp.maximum(x, 0.0) + l1p


def _scan_fwd(a, b):
    n = a.shape[0]
    row = lax.broadcasted_iota(jnp.int32, a.shape, 0)
    d = 1
    while d < n:
        if d % SUB:
            m = row >= d
            b = jnp.where(m, a * pltpu.roll(b, d, 0) + b, b)
            a = jnp.where(m, a * pltpu.roll(a, d, 0), a)
        else:
            b = a * jnp.concatenate([jnp.zeros((d,) + b.shape[1:], f32), b[:n - d]], axis=0) + b
            a = a * jnp.concatenate([jnp.ones((d,) + a.shape[1:], f32), a[:n - d]], axis=0)
        d *= 2
    return a, b


def _scan_bwd(a, b):
    n = a.shape[0]
    row = lax.broadcasted_iota(jnp.int32, a.shape, 0)
    d = 1
    while d < n:
        if d % SUB:
            m = row < n - d
            b = jnp.where(m, a * pltpu.roll(b, n - d, 0) + b, b)
            a = jnp.where(m, a * pltpu.roll(a, n - d, 0), a)
        else:
            b = a * jnp.concatenate([b[d:], jnp.zeros((d,) + b.shape[1:], f32)], axis=0) + b
            a = a * jnp.concatenate([a[d:], jnp.ones((d,) + a.shape[1:], f32)], axis=0)
        d *= 2
    return a, b


def _chunk_mats(n):
    r = lax.broadcasted_iota(jnp.int32, (n, n), 0)
    c = lax.broadcasted_iota(jnp.int32, (n, n), 1)
    same = (r // CHUNK) == (c // CHUNK)
    return same, jnp.logical_and(same, c <= r)


def _pool_w(shape):
    lane = lax.broadcasted_iota(jnp.int32, shape, 1)
    return jnp.where(lane < POOL_GROUP, 2, jnp.where(lane < 2 * POOL_GROUP, 4, jnp.where(lane < 3 * POOL_GROUP, 8, 16)))


def _pool_pick(r1, r2, r3, r4):
    lane = lax.broadcasted_iota(jnp.int32, r1.shape, 1)
    return jnp.where(lane < POOL_GROUP, r1, jnp.where(lane < 2 * POOL_GROUP, r2, jnp.where(lane < 3 * POOL_GROUP, r3, r4)))


def _const_spec(a):
    nd = a.ndim
    return pl.BlockSpec(a.shape, lambda i, _nd=nd: (0,) * _nd, pipeline_mode=pl.Buffered(1))


def _acc_spec(shape):
    nd = len(shape)
    return pl.BlockSpec(shape, lambda i, _nd=nd: (0,) * _nd)


def _params(sem="arbitrary"):
    return pltpu.CompilerParams(dimension_semantics=(sem,), vmem_limit_bytes=VMEM_LIMIT)


def _xattn_fwd(qx, khT_ref, vh_ref):
    xo = jnp.zeros((qx.shape[0], XW), f32)
    ps = []
    for h in range(XHEADS):
        s = _mm(qx, khT_ref[h]) * (XDIM ** -0.5)
        e = jnp.exp(s - jnp.max(s, axis=-1, keepdims=True))
        p = e / jnp.sum(e, axis=-1, keepdims=True)
        xo = xo + _mm(p, vh_ref[h])
        ps.append(p)
    return xo, ps


def _hgrn_parallel(q_raw, fl, lb):
    n = q_raw.shape[0]
    same, tri = _chunk_mats(n)
    sq = _sig(q_raw)
    qf = q_raw * sq
    sgm = _sig(fl)
    f = lb + (1.0 - lb) * sgm
    logf = jnp.log(f)
    k = 1.0 - f
    g = _mm_sel(tri, logf)
    gl = _mm_sel(same, logf)
    eg = jnp.exp(g)
    eng = jnp.exp(-g)
    ee = jnp.exp(gl - g)
    return dict(sq=sq, qf=qf, sgm=sgm, f=f, k=k, eg=eg, eng=eng, ee=ee, q_dec=qf * eg, k_inv=k * eng, k_end=k * ee,
                a=jnp.exp(gl))


def _hgrn_intra(q_dec, k_inv, v):
    n = q_dec.shape[0]
    _, tri = _chunk_mats(HD)
    outs = []
    for h in range(NH):
        blks = []
        for b in range(n // HD):
            rs = slice(b * HD, (b + 1) * HD)
            sc = jnp.where(tri, _mm_nt(q_dec[rs, _hs(h)], k_inv[rs, _hs(h)]), 0.0)
            blks.append(_mm(sc, v[rs, _hs(h)]))
        outs.append(jnp.concatenate(blks, axis=0))
    return jnp.concatenate(outs, axis=-1)


def _cs(c):
    return slice(c * CHUNK, (c + 1) * CHUNK)


def _hgrn_inter_fwd(qdec_s, kend_s, v_s, a_s, oint_s, st_ref, states_s, u_s):
    n = qdec_s.shape[0] // CHUNK
    for c in range(n):
        for h in range(NH):
            u_s[c, h] = _mm_tn(v_s[_cs(c), _hs(h)], kend_s[_cs(c), _hs(h)])
    for h in range(NH):
        st = st_ref[h]
        for c in range(n):
            states_s[c, h] = st
            st = st * a_s[c * CHUNK:c * CHUNK + 1, _hs(h)] + u_s[c, h]
        st_ref[h] = st
    if oint_s is None:
        return
    for c in range(n):
        for h in range(NH):
            oint_s[_cs(c), _hs(h)] = _mm_nt(qdec_s[_cs(c), _hs(h)], states_s[c, h])


def _rms(o):
    outs, rs = [], []
    for h in range(NH):
        oh = o[:, _hs(h)]
        r = lax.rsqrt(_lmean(oh * oh) + RMS_EPS)
        outs.append(oh * r)
        rs.append(r)
    return jnp.concatenate(outs, axis=-1), rs


def _gmlp_core(u_raw, v_raw, wtri_ref, bcolb_ref):
    gu, tu = _gelu(u_raw)
    gv, tv = _gelu(v_raw)
    vns, rstds, mixeds = [], [], []
    for h in range(NH):
        vn, rstd = _ln(gv[:, _hs(h)])
        blks = []
        for n in range(u_raw.shape[0] // HD):
            blks.append(_mm(wtri_ref[h], vn[n * HD:(n + 1) * HD]) + bcolb_ref[h])
        vns.append(vn)
        rstds.append(rstd)
        mixeds.append(jnp.concatenate(blks, axis=0))
    mixed = jnp.concatenate(mixeds, axis=-1)
    return gu, tu, tv, vns, rstds, mixed


def _pool_core(p, carry, row0, wbd_ref):
    ext = jnp.concatenate([carry, p], axis=0)
    r1 = ext + pltpu.roll(ext, 1, 0)
    r2 = r1 + pltpu.roll(r1, 2, 0)
    r3 = r2 + pltpu.roll(r2, 4, 0)
    r4 = r3 + pltpu.roll(r3, 8, 0)
    sel = _pool_pick(r1, r2, r3, r4)[2 * SUB:]
    grow = row0 + lax.broadcasted_iota(jnp.int32, p.shape, 0)
    inv_cnt = 1.0 / jnp.minimum(grow + 1, _pool_w(p.shape)).astype(f32)
    diff = sel * inv_cnt - p
    return diff, inv_cnt, _mm(diff, wbd_ref[...])


def _lru_core(xb, ccar, row0, p):
    ext = jnp.concatenate([ccar, xb], axis=0)
    cw = p["cw"]
    x1, x2, x3 = pltpu.roll(ext, 1, 0)[SUB:], pltpu.roll(ext, 2, 0)[SUB:], pltpu.roll(ext, 3, 0)[SUB:]
    xc = cw[3:4, :] * xb + cw[2:3, :] * x1 + cw[1:2, :] * x2 + cw[0:1, :] * x3 + p["cb"][...]
    gxs, gas = [], []
    for h in range(NH):
        gxs.append(_mm(xc[:, _hs(h)], p["wgx"][h]))
        gas.append(_mm(xc[:, _hs(h)], p["wga"][h]))
    gx = _sig(jnp.concatenate(gxs, axis=-1) + p["bgx"][...])
    ga = _sig(jnp.concatenate(gas, axis=-1) + p["bga"][...])
    sp = _softplus(-p["ap"][...])
    la = -LRU_C * ga * sp
    a = jnp.exp(la)
    grow = row0 + lax.broadcasted_iota(jnp.int32, xb.shape, 0)
    first = grow == 0
    mult = jnp.where(first, 1.0, jnp.sqrt(-_expm1(2.0 * la)))
    bt = mult * gx * xc
    return dict(x1=x1, x2=x2, x3=x3, xc=xc, gx=gx, ga=ga, sp=sp, a=a, mult=mult, bt=bt, first=first)


def _fwd_layer(kind, xin, w_in, w_out, lng, lnb, khT, vh, prm, tgt):
    S = xin.shape[0]
    TS = _TS[kind]
    nt = S // TS
    off = _OFFS[kind]
    W = off["W"]
    last = tgt is not None
    pnames = _PRM[kind]
    pvals = [prm[n] for n in pnames]

    def body(*refs):
        it = iter(refs)
        xin_ref, win_ref, wout_ref, lng_ref, lnb_ref, khT_ref, vh_ref = (next(it) for _ in range(7))
        p = {n: next(it) for n in pnames}
        tgt_ref = next(it) if last else None
        xout_ref, proj_ref, z_ref = next(it), next(it), next(it)
        loss_ref = next(it) if last else None
        rest = list(it)
        i = pl.program_id(0)
        x = xin_ref[...]
        proj_ref[...] = _mm(x, win_ref[...])

        if kind == 0:
            gu, _, _, _, _, mixed = _gmlp_core(proj_ref[:, 0:TOK], proj_ref[:, TOK:2 * TOK], p["wtri"], p["bcolb"])
            tok = gu * mixed
        elif kind == 1:
            st_save, o_save, st_ref, states_s, u_s, qdec_s, kend_s, v_s, a_s, oint_s = rest

            @pl.when(i == 0)
            def _():
                st_ref[...] = jnp.zeros_like(st_ref)

            st_save[0, 0] = st_ref[...]
            v = proj_ref[:, 2 * TOK:3 * TOK]
            hp = _hgrn_parallel(proj_ref[:, 0:TOK], proj_ref[:, TOK:2 * TOK], p["lb"][...])
            qdec_s[...] = hp["q_dec"]
            kend_s[...] = hp["k_end"]
            v_s[...] = v
            a_s[...] = hp["a"]
            o_intra = _hgrn_intra(hp["q_dec"], hp["k_inv"], v)
            _hgrn_inter_fwd(qdec_s, kend_s, v_s, a_s, oint_s, st_ref, states_s, u_s)
            o = o_intra + oint_s[...]
            o_save[0] = o
            for sub in range(1, TS // HGRN_SUB):
                st_save[0, sub] = states_s[sub * HGRN_SUB // CHUNK]
            on, _ = _rms(o)
            tok = on * p["ng"][...]
        elif kind == 2:
            pc_save, pcar = rest

            @pl.when(i == 0)
            def _():
                pcar[...] = jnp.zeros_like(pcar)

            pc_save[0] = pcar[...]
            pp = proj_ref[:, 0:TOK]
            _, _, y = _pool_core(pp, pcar[...], i * TS, p["wbd"])
            pcar[...] = pp[TS - 2 * SUB:, :]
            tok = y * p["scale"][...]
        else:
            cc_save, hc_save, h_save, ccar, hcar = rest

            @pl.when(i == 0)
            def _():
                ccar[...] = jnp.zeros_like(ccar)
                hcar[...] = jnp.zeros_like(hcar)

            cc_save[0] = ccar[...]
            hc_save[0] = hcar[...]
            xb = proj_ref[:, 0:TOK]
            lc = _lru_core(xb, ccar[...], i * TS, p)
            P, B = _scan_fwd(lc["a"], lc["bt"])
            tok = P * hcar[SUB - 1:SUB, :] + B
            h_save[0] = tok
            ccar[...] = xb[TS - SUB:, :]
            hcar[...] = tok[TS - SUB:, :]

        xo, _ = _xattn_fwd(proj_ref[:, off["qx"]:off["qx"] + XW], khT_ref, vh_ref)
        gate = proj_ref[:, off["gate"]:off["gate"] + D_MODEL]
        mixed = jnp.concatenate([tok, xo], axis=-1) * (gate * _sig(gate))
        z = ALPHA * x + _mm(mixed, wout_ref[...])
        z_ref[...] = z
        xhat, _ = _ln(z)
        xout = xhat * lng_ref[...] + lnb_ref[...]
        if last:
            e = xout - tgt_ref[...]
            xout_ref[...] = e * (1.0 / D_MODEL)
            es = _rowsum(e * e)
            tot = es[:, 0:LANE]
            for j in range(1, D_MODEL // LANE):
                tot = tot + es[:, j * LANE:(j + 1) * LANE]

            @pl.when(i == 0)
            def _():
                loss_ref[...] = jnp.zeros_like(loss_ref)

            loss_ref[0:1, :] += tot
        else:
            xout_ref[...] = xout

    tile = lambda w: pl.BlockSpec((TS, w), lambda i: (i, 0))
    in_arrays = [xin, w_in, w_out, lng, lnb, khT, vh] + pvals + ([tgt] if last else [])
    in_specs = [tile(D_MODEL)] + [_const_spec(a) for a in in_arrays[1:7 + len(pvals)]] + ([tile(D_MODEL)] if last else [])
    out_shape = [jax.ShapeDtypeStruct((S, D_MODEL), f32), jax.ShapeDtypeStruct((S, W), f32), jax.ShapeDtypeStruct((S, D_MODEL), f32)]
    out_specs = [tile(D_MODEL), tile(W), tile(D_MODEL)]
    if last:
        out_shape.append(jax.ShapeDtypeStruct((SUB, LANE), f32))
        out_specs.append(_acc_spec((SUB, LANE)))
    scratch = []
    save = lambda *s: (jax.ShapeDtypeStruct((nt,) + s, f32), pl.BlockSpec((1,) + s, lambda i, _n=len(s): (i,) + (0,) * _n))
    if kind == 1:
        saved = [save(TS // HGRN_SUB, NH, HD, HD), save(TS, TOK)]
        scratch = ([pltpu.VMEM((NH, HD, HD), f32)] + [pltpu.VMEM((TS // CHUNK, NH, HD, HD), f32)] * 2
                   + [pltpu.VMEM((TS, TOK), f32)] * 5)
    elif kind == 2:
        saved = [save(2 * SUB, TOK)]
        scratch = [pltpu.VMEM((2 * SUB, TOK), f32)]
    elif kind == 3:
        saved = [save(SUB, TOK), save(SUB, TOK), save(TS, TOK)]
        scratch = [pltpu.VMEM((SUB, TOK), f32)] * 2
    else:
        saved = []
    for sh, sp in saved:
        out_shape.append(sh)
        out_specs.append(sp)
    return pl.pallas_call(body, name=f"fwd_layer{kind}", grid=(nt,), in_specs=in_specs, out_specs=out_specs,
                          out_shape=out_shape, scratch_shapes=scratch, compiler_params=_params())(*in_arrays)


def _small_grad_shapes(kind):
    if kind == 0:
        return dict(dwtri=(NH, HD, HD), dbacc=(NH, HD, HD))
    if kind == 1:
        return dict(dlb=(SUB, TOK), dng=(SUB, TOK))
    if kind == 2:
        return dict(dwbd=(TOK, TOK), dscale=(SUB, TOK))
    return dict(dcw=(SUB, TOK), dvec=(SUB, TOK), dwgx=(NH, HD, HD), dwga=(NH, HD, HD))


def _bwd_layer(kind, dxout, z, proj, w_inT, w_outT, lng, kh, khT, vh, vhT, prm, saves):
    S = dxout.shape[0]
    TS = _TS[kind]
    nt = S // TS
    off = _OFFS[kind]
    W = off["W"]
    pnames = _PRM[kind]
    pvals = [prm[n] for n in pnames]
    sg_shapes = _small_grad_shapes(kind)
    sg_names = list(sg_shapes)
    n_saves = len(saves)

    def body(*refs):
        it = iter(refs)
        dxo_ref, z_ref, proj_ref, winT_ref, woutT_ref, lng_ref, kh_ref, khT_ref, vh_ref, vhT_ref = (next(it) for _ in range(10))
        p = {n: next(it) for n in pnames}
        sv = [next(it) for _ in range(n_saves)]
        dxin_ref, dproj_ref, mixed_ref, dy_ref, dln_ref, dk_ref, dv_ref = (next(it) for _ in range(7))
        sg = {n: next(it) for n in sg_names}
        rest = list(it)
        step = pl.program_id(0)
        i = nt - 1 - step

        @pl.when(step == 0)
        def _():
            dln_ref[...] = jnp.zeros_like(dln_ref)
            dk_ref[...] = jnp.zeros_like(dk_ref)
            dv_ref[...] = jnp.zeros_like(dv_ref)
            for n in sg_names:
                sg[n][...] = jnp.zeros_like(sg[n])

        dxo = dxo_ref[...]
        xhat, rstd = _ln(z_ref[...])
        dln_ref[0:1, :] += _rowsum(dxo * xhat)
        dln_ref[1:2, :] += _rowsum(dxo)
        dz = _ln_bwd(dxo * lng_ref[...], xhat, rstd)
        dyb = dz.astype(bf16)
        dy_ref[...] = dyb
        dmixed = _mm(dyb, woutT_ref[...])

        aux = {}
        if kind == 0:
            u_raw, v_raw = proj_ref[:, 0:TOK], proj_ref[:, TOK:2 * TOK]
            gu, tu, tv, vns, rstds, mx = _gmlp_core(u_raw, v_raw, p["wtri"], p["bcolb"])
            tok = gu * mx
        elif kind == 1:
            st_save, o_save = sv
            (dst_ref, fst_ref, states_s, dsts_s, u_s, qdec_s, kend_s, v_s, a_s, do_s, dqdec_s, dkend_s, dv_s,
             dgl_s) = rest

            @pl.when(step == 0)
            def _():
                dst_ref[...] = jnp.zeros_like(dst_ref)

            o = o_save[0]
            on, rs = _rms(o)
            tok = on * p["ng"][...]
            aux = dict(o=o, on=on, rs=rs)
        elif kind == 2:
            pc_save, = sv
            dpcar, = rest
            pp = proj_ref[:, 0:TOK]
            diff, inv_cnt, y = _pool_core(pp, pc_save[0], i * TS, p["wbd"])
            tok = y * p["scale"][...]
        else:
            cc_save, hc_save, h_save = sv
            dccar, gcar = rest
            xb = proj_ref[:, 0:TOK]
            lc = _lru_core(xb, cc_save[0], i * TS, p)
            hin = hc_save[0, SUB - 1:SUB, :]
            tok = h_save[0]

        xo, ps = _xattn_fwd(proj_ref[:, off["qx"]:off["qx"] + XW], khT_ref, vh_ref)
        gate = proj_ref[:, off["gate"]:off["gate"] + D_MODEL]
        sgm = _sig(gate)
        sgate = gate * sgm
        cat = jnp.concatenate([tok, xo], axis=-1)
        mixed_ref[...] = (cat * sgate).astype(bf16)
        dcat = dmixed * sgate
        dproj_ref[:, off["gate"]:off["gate"] + D_MODEL] = (dmixed * cat * (sgm * (1.0 + gate * (1.0 - sgm)))).astype(bf16)
        dtok = dcat[:, 0:TOK]
        dxo_att = dcat[:, TOK:]

        qx = proj_ref[:, off["qx"]:off["qx"] + XW]
        dqx = jnp.zeros((TS, XW), f32)
        for h in range(XHEADS):
            dp = _mm(dxo_att, vhT_ref[h])
            ds = ps[h] * (dp - jnp.sum(dp * ps[h], axis=-1, keepdims=True)) * (XDIM ** -0.5)
            dqx = dqx + _mm(ds, kh_ref[h])
            dk_ref[h] += _mm_tn(ds, qx)
            dv_ref[h] += _mm_tn(ps[h], dxo_att)
        dproj_ref[:, off["qx"]:off["qx"] + XW] = dqx.astype(bf16)

        if kind == 0:
            tril = lax.broadcasted_iota(jnp.int32, (HD, HD), 1) <= lax.broadcasted_iota(jnp.int32, (HD, HD), 0)
            dgu = dtok * mx
            dmx = dtok * gu
            dgvs = []
            for h in range(NH):
                dmh = dmx[:, _hs(h)]
                blks = []
                for n in range(TS // HD):
                    rs_ = slice(n * HD, (n + 1) * HD)
                    blks.append(_mm(p["wtriT"][h], dmh[rs_]))
                    sg["dwtri"][h] += jnp.where(tril, _mm_nt(dmh[rs_], vns[h][rs_]), 0.0)
                    sg["dbacc"][h] += dmh[rs_]
                dgvs.append(_ln_bwd(jnp.concatenate(blks, axis=0), vns[h], rstds[h]))
            dgv = jnp.concatenate(dgvs, axis=-1)
            dproj_ref[:, 0:TOK] = (dgu * _gelu_grad(u_raw, tu)).astype(bf16)
            dproj_ref[:, TOK:2 * TOK] = (dgv * _gelu_grad(v_raw, tv)).astype(bf16)
        elif kind == 1:
            o, on, rs = aux["o"], aux["on"], aux["rs"]
            ng = p["ng"][...]
            lb = p["lb"][...]
            sg["dng"][0:1, :] += _rowsum(dtok * on)
            dn = dtok * ng
            dos = []
            for h in range(NH):
                oh, r = o[:, _hs(h)], rs[h]
                dos.append(r * (dn[:, _hs(h)] - oh * (r * r) * _lmean(dn[:, _hs(h)] * oh)))
            do_all = jnp.concatenate(dos, axis=-1)
            _, tri = _chunk_mats(HD)
            same, _ = _chunk_mats(HGRN_SUB)
            triT = jnp.logical_and(same, lax.broadcasted_iota(jnp.int32, (HGRN_SUB, HGRN_SUB), 1)
                                   >= lax.broadcasted_iota(jnp.int32, (HGRN_SUB, HGRN_SUB), 0))
            row16 = lax.broadcasted_iota(jnp.int32, (CHUNK, HD), 0)
            nch = HGRN_SUB // CHUNK
            for sub in reversed(range(TS // HGRN_SUB)):
                rr = slice(sub * HGRN_SUB, (sub + 1) * HGRN_SUB)
                q_raw, v = proj_ref[rr, 0:TOK], proj_ref[rr, 2 * TOK:3 * TOK]
                hp = _hgrn_parallel(q_raw, proj_ref[rr, TOK:2 * TOK], lb)
                qdec_s[...] = hp["q_dec"]
                kend_s[...] = hp["k_end"]
                v_s[...] = v
                a_s[...] = hp["a"]
                fst_ref[...] = st_save[0, sub]
                _hgrn_inter_fwd(qdec_s, kend_s, v_s, a_s, None, fst_ref, states_s, u_s)
                do = do_all[rr]
                do_s[...] = do
                dqd, dki, dvi = [], [], []
                for h in range(NH):
                    bq, bk, bv = [], [], []
                    for b in range(HGRN_SUB // HD):
                        rs_ = slice(b * HD, (b + 1) * HD)
                        qd, ki = hp["q_dec"][rs_, _hs(h)], hp["k_inv"][rs_, _hs(h)]
                        sc = jnp.where(tri, _mm_nt(qd, ki), 0.0)
                        dsc = jnp.where(tri, _mm_nt(do[rs_, _hs(h)], v[rs_, _hs(h)]), 0.0)
                        bv.append(_mm_tn(sc, do[rs_, _hs(h)]))
                        bq.append(_mm(dsc, ki))
                        bk.append(_mm_tn(dsc, qd))
                    dqd.append(jnp.concatenate(bq, axis=0))
                    dki.append(jnp.concatenate(bk, axis=0))
                    dvi.append(jnp.concatenate(bv, axis=0))
                dqdec_s[...] = jnp.concatenate(dqd, axis=-1)
                dk_inv = jnp.concatenate(dki, axis=-1)
                dv_s[...] = jnp.concatenate(dvi, axis=-1)
                for c in range(nch):
                    for h in range(NH):
                        u_s[c, h] = _mm_tn(do_s[_cs(c), _hs(h)], qdec_s[_cs(c), _hs(h)])
                for h in range(NH):
                    dst = dst_ref[h]
                    for c in reversed(range(nch)):
                        dsts_s[c, h] = dst
                        dst = dst * a_s[c * CHUNK:c * CHUNK + 1, _hs(h)] + u_s[c, h]
                    dst_ref[h] = dst
                for c in range(nch):
                    for h in range(NH):
                        stp = states_s[c, h]
                        dst = dsts_s[c, h]
                        dqdec_s[_cs(c), _hs(h)] += _mm(do_s[_cs(c), _hs(h)], stp)
                        dkend_s[_cs(c), _hs(h)] = _mm(v_s[_cs(c), _hs(h)], dst)
                        dv_s[_cs(c), _hs(h)] += _mm_nt(kend_s[_cs(c), _hs(h)], dst)
                        da = jnp.sum(dst * stp, axis=0, keepdims=True) * a_s[c * CHUNK:c * CHUNK + 1, _hs(h)]
                        dgl_s[_cs(c), _hs(h)] = jnp.where(row16 == 0, jnp.broadcast_to(da, (CHUNK, HD)), 0.0)
                dq_dec = dqdec_s[...]
                dk_end = dkend_s[...]
                dg = dq_dec * hp["q_dec"] - dk_inv * hp["k_inv"] - dk_end * hp["k_end"]
                dk = dk_inv * hp["eng"] + dk_end * hp["ee"]
                dglr = dk_end * hp["k_end"] + dgl_s[...]
                dlogf = _mm_sel(triT, dg) + _mm_sel(same, dglr)
                df = dlogf / hp["f"] - dk
                sg["dlb"][0:1, :] += _rowsum(df * (1.0 - hp["sgm"]))
                dproj_ref[rr, 0:TOK] = (dq_dec * hp["eg"] * (hp["sq"] * (1.0 + q_raw * (1.0 - hp["sq"])))).astype(bf16)
                dproj_ref[rr, TOK:2 * TOK] = (df * (1.0 - lb) * hp["sgm"] * (1.0 - hp["sgm"])).astype(bf16)
                dproj_ref[rr, 2 * TOK:3 * TOK] = dv_s[...].astype(bf16)
        elif kind == 2:
            @pl.when(step == 0)
            def _():
                dpcar[...] = jnp.zeros_like(dpcar)

            sg["dscale"][0:1, :] += _rowsum(dtok * y)
            dyp = dtok * p["scale"][...]
            sg["dwbd"][...] += _mm_tn(diff, dyp)
            ddiff = _mm(dyp, p["wbdT"][...])
            q = ddiff * inv_cnt
            ext = jnp.concatenate([q, dpcar[...]], axis=0)
            n = TS + 2 * SUB
            r1 = ext + pltpu.roll(ext, n - 1, 0)
            r2 = r1 + pltpu.roll(r1, n - 2, 0)
            r3 = r2 + pltpu.roll(r2, n - 4, 0)
            r4 = r3 + pltpu.roll(r3, n - 8, 0)
            dproj_ref[:, 0:TOK] = (_pool_pick(r1, r2, r3, r4)[:TS] - ddiff).astype(bf16)
            dpcar[...] = q[0:2 * SUB, :]
        else:
            @pl.when(step == 0)
            def _():
                dccar[...] = jnp.zeros_like(dccar)
                gcar[...] = jnp.zeros_like(gcar)

            a, mult, gx, ga, xc = lc["a"], lc["mult"], lc["gx"], lc["ga"], lc["xc"]
            row = lax.broadcasted_iota(jnp.int32, (TS, TOK), 0)
            an = jnp.where(row == TS - 1, 1.0, pltpu.roll(a, TS - 1, 0))
            Pb, Bb = _scan_bwd(an, dtok)
            lam = Pb * gcar[0:1, :] + Bb
            gcar[...] = (a * lam)[0:SUB, :]
            hprev = jnp.where(row == 0, jnp.broadcast_to(hin, (TS, TOK)), pltpu.roll(tok, 1, 0))
            dmult = lam * gx * xc
            dgx = lam * mult * xc
            dxc = lam * mult * gx
            dla = lam * hprev * a - jnp.where(lc["first"], 0.0, dmult * a * a / mult)
            sp = lc["sp"]
            dga = -LRU_C * sp * dla
            dsp = _rowsum(-LRU_C * ga * dla)
            sg["dvec"][0:1, :] += dsp * (-_sig(-p["ap"][...]))
            dpx = dgx * gx * (1.0 - gx)
            dpa = dga * ga * (1.0 - ga)
            sg["dvec"][1:2, :] += _rowsum(dpx)
            sg["dvec"][2:3, :] += _rowsum(dpa)
            dxcs = []
            for h in range(NH):
                dxcs.append(_mm(dpx[:, _hs(h)], p["wgxT"][h]) + _mm(dpa[:, _hs(h)], p["wgaT"][h]))
                sg["dwgx"][h] += _mm_tn(xc[:, _hs(h)], dpx[:, _hs(h)])
                sg["dwga"][h] += _mm_tn(xc[:, _hs(h)], dpa[:, _hs(h)])
            dxc = dxc + jnp.concatenate(dxcs, axis=-1)
            sg["dvec"][3:4, :] += _rowsum(dxc)
            sg["dcw"][3:4, :] += _rowsum(dxc * xb)
            sg["dcw"][2:3, :] += _rowsum(dxc * lc["x1"])
            sg["dcw"][1:2, :] += _rowsum(dxc * lc["x2"])
            sg["dcw"][0:1, :] += _rowsum(dxc * lc["x3"])
            ext = jnp.concatenate([dxc, dccar[...]], axis=0)
            n = TS + SUB
            cw = p["cw"]
            dproj_ref[:, 0:TOK] = (cw[3:4, :] * dxc + cw[2:3, :] * pltpu.roll(ext, n - 1, 0)[:TS]
                                   + cw[1:2, :] * pltpu.roll(ext, n - 2, 0)[:TS]
                                   + cw[0:1, :] * pltpu.roll(ext, n - 3, 0)[:TS]).astype(bf16)
            dccar[...] = dxc[0:SUB, :]

        dxin_ref[...] = ALPHA * dz + _mm(dproj_ref[...], winT_ref[...])

    rtile = lambda w: pl.BlockSpec((TS, w), lambda s: (nt - 1 - s, 0))
    consts = [w_inT, w_outT, lng, kh, khT, vh, vhT] + pvals
    in_arrays = [dxout, z, proj] + consts + list(saves)
    in_specs = [rtile(D_MODEL), rtile(D_MODEL), rtile(W)] + [_const_spec(a) for a in consts]
    for a in saves:
        in_specs.append(pl.BlockSpec((1,) + a.shape[1:], lambda s, _n=a.ndim - 1: (nt - 1 - s,) + (0,) * _n))
    out_shape = [jax.ShapeDtypeStruct((S, D_MODEL), f32), jax.ShapeDtypeStruct((S, W), bf16),
                 jax.ShapeDtypeStruct((S, D_MODEL), bf16), jax.ShapeDtypeStruct((S, D_MODEL), bf16),
                 jax.ShapeDtypeStruct((SUB, D_MODEL), f32), jax.ShapeDtypeStruct((XHEADS, XW, XW), f32),
                 jax.ShapeDtypeStruct((XHEADS, XW, XW), f32)]
    out_specs = [rtile(D_MODEL), rtile(W), rtile(D_MODEL), rtile(D_MODEL), _acc_spec((SUB, D_MODEL)),
                 _acc_spec((XHEADS, XW, XW)), _acc_spec((XHEADS, XW, XW))]
    for n in sg_names:
        out_shape.append(jax.ShapeDtypeStruct(sg_shapes[n], f32))
        out_specs.append(_acc_spec(sg_shapes[n]))
    if kind == 1:
        scratch = ([pltpu.VMEM((NH, HD, HD), f32)] * 2 + [pltpu.VMEM((HGRN_SUB // CHUNK, NH, HD, HD), f32)] * 3
                   + [pltpu.VMEM((HGRN_SUB, TOK), f32)] * 9)
    elif kind == 2:
        scratch = [pltpu.VMEM((2 * SUB, TOK), f32)]
    elif kind == 3:
        scratch = [pltpu.VMEM((SUB, TOK), f32)] * 2
    else:
        scratch = []
    outs = pl.pallas_call(body, name=f"bwd_layer{kind}", grid=(nt,), in_specs=in_specs, out_specs=out_specs,
                          out_shape=out_shape, scratch_shapes=scratch, compiler_params=_params())(*in_arrays)
    return outs[:7], dict(zip(sg_names, outs[7:]))


def _prep(mem, w_kv, logits):
    def body(mem_ref, w_ref, lg_ref, kh_ref, khT_ref, vh_ref, vhT_ref, p_ref):
        kv = _mm(mem_ref[...], w_ref[...])
        k, v = kv[:, 0:XW], kv[:, XW:]
        kT, vT = k.T, v.T
        col = lax.broadcasted_iota(jnp.int32, (XW, XW), 1) // XDIM
        row = lax.broadcasted_iota(jnp.int32, (XW, XW), 0) // XDIM
        for h in range(XHEADS):
            kh_ref[h] = jnp.where(col == h, k, 0.0).astype(bf16)
            vh_ref[h] = jnp.where(col == h, v, 0.0).astype(bf16)
            khT_ref[h] = jnp.where(row == h, kT, 0.0).astype(bf16)
            vhT_ref[h] = jnp.where(row == h, vT, 0.0).astype(bf16)
        lg = lg_ref[...]
        e = jnp.exp(lg - jnp.max(lg, axis=0, keepdims=True))
        p_ref[...] = e / jnp.sum(e, axis=0, keepdims=True)

    vm = pl.BlockSpec(memory_space=pltpu.VMEM)
    hs = jax.ShapeDtypeStruct((XHEADS, XW, XW), bf16)
    return pl.pallas_call(body, name="prep_memory", in_specs=[vm] * 3, out_specs=[vm] * 5,
                          out_shape=[hs, hs, hs, hs, jax.ShapeDtypeStruct(logits.shape, f32)])(mem, w_kv, logits)


def _kv_bwd(mem, dks, dvs):
    def body(mem_ref, *refs):
        out_ref = refs[-1]
        col = lax.broadcasted_iota(jnp.int32, (XW, XW), 1) // XDIM
        dk = jnp.zeros((XW, XW), f32)
        dv = jnp.zeros((XW, XW), f32)
        for l in range(DEPTH):
            for h in range(XHEADS):
                dk = dk + jnp.where(col == h, refs[l][h], 0.0)
                dv = dv + jnp.where(col == h, refs[DEPTH + l][h], 0.0)
        out_ref[:, 0:XW] = _mm_tn(mem_ref[...], dk)
        out_ref[:, XW:] = _mm_tn(mem_ref[...], dv)

    vm = pl.BlockSpec(memory_space=pltpu.VMEM)
    return pl.pallas_call(body, name="kv_bwd", in_specs=[vm] * (1 + 2 * DEPTH), out_specs=vm,
                          out_shape=jax.ShapeDtypeStruct((D_MODEL, 2 * XW), f32))(mem, *dks, *dvs)


def _tn_gemm(a, b, name, nb):
    S, M = a.shape
    N = b.shape[1]
    NB = N // nb
    nk = S // TK

    def body(a_ref, b_ref, o_ref):
        @pl.when(pl.program_id(1) == 0)
        def _():
            o_ref[...] = jnp.zeros_like(o_ref)

        o_ref[...] += _mm_tn(a_ref[...], b_ref[...])

    return pl.pallas_call(body, name=name, grid=(nb, nk),
                          in_specs=[pl.BlockSpec((TK, M), lambda j, k: (k, 0)), pl.BlockSpec((TK, NB), lambda j, k: (k, j))],
                          out_specs=pl.BlockSpec((M, NB), lambda j, k: (0, j)),
                          out_shape=jax.ShapeDtypeStruct((M, N), f32),
                          compiler_params=pltpu.CompilerParams(dimension_semantics=("parallel", "arbitrary"),
                                                               vmem_limit_bytes=VMEM_LIMIT))(a, b)


def _rows_block(R, mult=16, cap=1024):
    best = R
    for d in range(mult, min(R, cap) + 1, mult):
        if R % d == 0:
            best = d
    return best


def _tn_gemm_sharded(a, b, name):
    S, M = a.shape
    Wq = b.shape[1] // 4
    nk = S // TK

    def body(a_ref, b_ref, o_ref):
        @pl.when(pl.program_id(0) == 0)
        def _():
            o_ref[...] = jnp.zeros_like(o_ref)

        at = a_ref[...].astype(MM)
        for j in range(4):
            o_ref[j] += _mm_tn(at, b_ref[:, j * Wq:(j + 1) * Wq])

    return pl.pallas_call(body, name=name, grid=(nk,),
                          in_specs=[pl.BlockSpec((TK, M), lambda k: (k, 0)), pl.BlockSpec((TK, 4 * Wq), lambda k: (k, 0))],
                          out_specs=pl.BlockSpec((4, M, Wq), lambda k: (0, 0, 0)),
                          out_shape=jax.ShapeDtypeStruct((4, M, Wq), f32), compiler_params=_params())(a, b)


HALF_ROWS = D_MODEL // 2
SHARD_ROWS = D_MODEL // 4


def _half_of_full(ref, kind, h):
    if kind == "rows":
        cols = ref.shape[1] // 2
        return ref.at[:, pl.ds(h * cols, cols)]
    return ref.at[:, pl.ds(h * HALF_ROWS, HALF_ROWS)]


def _shard_of_half(ref, kind, j):
    if kind == "rows":
        return ref.at[pl.ds(j * SHARD_ROWS, SHARD_ROWS)]
    return ref.at[j]


def _half_of_shard(ref, kind, h):
    if kind == "rows":
        cols = ref.shape[1] // 2
        return ref.at[:, pl.ds(h * cols, cols)]
    rows = ref.shape[0] // 2
    return ref.at[pl.ds(h * rows, rows)]


def _half_shape(full_shape, kind):
    if kind == "rows":
        return (full_shape[0], full_shape[1] // 2)
    return (4, HALF_ROWS, full_shape[2])


def _shard_half_shape(full_shape, kind):
    if kind == "rows":
        return (SHARD_ROWS, full_shape[1] // 2)
    return (HALF_ROWS, full_shape[2])


def _shard_shape(full_shape, kind):
    if kind == "rows":
        return (SHARD_ROWS, full_shape[1])
    return (D_MODEL, full_shape[2])


def _ew_call(body, name, grid, jc, ins, in_specs, out_shape, out_specs):
    gs = pltpu.PrefetchScalarGridSpec(num_scalar_prefetch=1, grid=grid, in_specs=in_specs, out_specs=out_specs)
    return pl.pallas_call(body, name=name, grid_spec=gs, out_shape=out_shape,
                          compiler_params=pltpu.CompilerParams(dimension_semantics=("parallel",) * len(grid),
                                                               vmem_limit_bytes=VMEM_LIMIT))(jc, *ins)


def _add_sibling(part, got, kind, jc, name):
    def body(jc_ref, a_ref, b_ref, o_ref, ob_ref):
        s = a_ref[...] + b_ref[...]
        o_ref[...] = s
        ob_ref[...] = s.astype(bf16)

    if kind == "rows":
        R, C = part.shape[0], part.shape[1] // 2
        grid = (2,)
        mine = pl.BlockSpec((R // 2, C), lambda i, jc_ref: (i, jc_ref[1]))
        spec = pl.BlockSpec((R // 2, C), lambda i, jc_ref: (i, 0))
    else:
        C = part.shape[2]
        grid = (4, 2)
        mine = pl.BlockSpec((None, HALF_ROWS // 2, C), lambda s, i, jc_ref: (s, 2 * jc_ref[1] + i, 0))
        spec = pl.BlockSpec((None, HALF_ROWS // 2, C), lambda s, i, jc_ref: (s, i, 0))
    hs = _half_shape(part.shape, kind)
    return _ew_call(body, name, grid, jc, [part, got], [mine, spec],
                    [jax.ShapeDtypeStruct(hs, f32), jax.ShapeDtypeStruct(hs, bf16)], [spec, spec])


def _add_chips(q32, r, kind, jc, name):
    def body(jc_ref, q_ref, r_ref, out_ref):
        out_ref[...] = ((q_ref[...] + r_ref[0].astype(f32)) + r_ref[1].astype(f32)) + r_ref[2].astype(f32)

    if kind == "rows":
        C = q32.shape[1]
        grid = (1,)
        qs = pl.BlockSpec((SHARD_ROWS, C), lambda i, jc_ref: (jc_ref[0], 0))
        rs = pl.BlockSpec((3, SHARD_ROWS, C), lambda i, jc_ref: (0, 0, 0))
        os_ = pl.BlockSpec((SHARD_ROWS, C), lambda i, jc_ref: (0, jc_ref[1]))
        full_shape = (D_MODEL, 2 * C)
    else:
        C = q32.shape[2]
        grid = (2,)
        qs = pl.BlockSpec((None, HALF_ROWS // 2, C), lambda i, jc_ref: (jc_ref[0], i, 0))
        rs = pl.BlockSpec((3, HALF_ROWS // 2, C), lambda i, jc_ref: (0, i, 0))
        os_ = pl.BlockSpec((HALF_ROWS // 2, C), lambda i, jc_ref: (2 * jc_ref[1] + i, 0))
        full_shape = (4, D_MODEL, C)
    return _ew_call(body, name, grid, jc, [q32, r], [qs, rs], jax.ShapeDtypeStruct(_shard_shape(full_shape, kind), f32), os_)


def _adamw(w, g, m, v, name):
    R, C = w.shape
    br = _rows_block(R, mult=SUB, cap=512)
    c1 =1.0 / (1.0 - ADAM_B1 ** ADAM_STEP)
    c2 = 1.0 / (1.0 - ADAM_B2 ** ADAM_STEP)

    def body(w_ref, g_ref, m_ref, v_ref, d_ref, nm_ref, nv_ref):
        g_ = g_ref[...]
        nm = ADAM_B1 * m_ref[...] + (1.0 - ADAM_B1) * g_
        nv = ADAM_B2 * v_ref[...] + (1.0 - ADAM_B2) * (g_ * g_)
        nm_ref[...] = nm
        nv_ref[...] = nv
        d_ref[...] = -ADAM_LR * ((nm * c1) / (jnp.sqrt(nv * c2) + ADAM_EPS) + ADAM_WD * w_ref[...])

    spec = pl.BlockSpec((br, C), lambda i: (i, 0))
    sh = jax.ShapeDtypeStruct((R, C), f32)
    return pl.pallas_call(body, name=name, grid=(R // br,), in_specs=[spec] * 4, out_specs=[spec] * 3,
                          out_shape=[sh, sh, sh], compiler_params=_params("parallel"))(w, g, m, v)


def _small_finish(dbacc, p_soft, dlb):
    def body(db_ref, p_ref, dlb_ref, dbs_ref, dlg_ref):
        lane = lax.broadcasted_iota(jnp.int32, (HD, HD), 1)
        acc = jnp.zeros((HD, HD), f32)
        for h in range(NH):
            acc = acc + jnp.where(lane == h, jnp.sum(db_ref[h], axis=-1, keepdims=True), 0.0)
        dbs_ref[...] = acc
        p = p_ref[...]
        p1 = p[1:2, :]
        rowi = lax.broadcasted_iota(jnp.int32, p.shape, 0)
        dlg_ref[...] = dlb_ref[0:1, :] * p1 * (jnp.where(rowi == 1, 1.0, 0.0) - p)

    vm = pl.BlockSpec(memory_space=pltpu.VMEM)
    return pl.pallas_call(body, name="small_finish", in_specs=[vm] * 3, out_specs=[vm] * 2,
                          out_shape=[jax.ShapeDtypeStruct((HD, HD), f32), jax.ShapeDtypeStruct(p_soft.shape, f32)])(dbacc, p_soft, dlb)


def _where_am_i():
    return lax.axis_index("x"), lax.axis_index("y"), lax.axis_index("c")


MAX_PIECES = 8


def _nchunks(rows, mult):
    for n in range(MAX_PIECES, 0, -1):
        if rows % (n * mult) == 0:
            return n
    return 1


def _leading_pieces(src, dst):
    n = src.shape[0]
    if len(src.shape) >= 3 and n <= MAX_PIECES:
        return [(src.at[s], dst.at[s]) for s in range(n)]
    return [(src, dst)]


def _ag_weights(shards, kinds, jshard):
    n = len(shards)

    def body(*refs):
        sh_refs, out_refs = refs[:n], refs[2 * n:3 * n]
        send_sems, recv_sems = refs[3 * n:]
        x, y, c = _where_am_i()
        j = 2 * x + y
        sib = (x, y, 1 - c)
        chips = [(1 - x, y), (x, 1 - y), (1 - x, 1 - y)]

        def cp(k, src, dst, to):
            return pltpu.make_async_remote_copy(src_ref=src, dst_ref=dst, send_sem=send_sems.at[k], recv_sem=recv_sems.at[k],
                                                device_id=to, device_id_type=MESH)

        started = []
        for a in range(n):
            for k, (cx, cy) in enumerate(chips):
                d = cp(6 * a + k, _half_of_shard(sh_refs[a], kinds[a], c), _half_of_shard(out_refs[a].at[j], kinds[a], c), (cx, cy, c))
                d.start()
                started.append(d)
        for a in range(n):
            for k, (cx, cy) in enumerate(chips):
                blk = _half_of_shard(out_refs[a].at[2 * cx + cy], kinds[a], c)
                cp(6 * a + k, blk, blk, (cx, cy, c)).wait_recv()
                d = cp(6 * a + 3 + k, blk, blk, sib)
                d.start()
                started.append(d)
        for a in range(n):
            for k, (cx, cy) in enumerate(chips):
                blk = _half_of_shard(out_refs[a].at[2 * cx + cy], kinds[a], 1 - c)
                cp(6 * a + 3 + k, blk, blk, sib).wait_recv()
        for d in started:
            d.wait_send()

    placed = [lax.dynamic_update_slice(jnp.zeros((4,) + s.shape, s.dtype), s[None], (jshard,) + (0,) * s.ndim) for s in shards]
    anyspec = pl.BlockSpec(memory_space=pl.ANY)
    return pl.pallas_call(body, name="all_gather_weights", in_specs=[anyspec] * (2 * n), out_specs=[anyspec] * n,
                          out_shape=[jax.ShapeDtypeStruct(p.shape, p.dtype) for p in placed],
                          input_output_aliases={n + a: a for a in range(n)},
                          scratch_shapes=[pltpu.SemaphoreType.DMA((6 * n,)), pltpu.SemaphoreType.DMA((6 * n,))],
                          compiler_params=pltpu.CompilerParams(has_side_effects=True))(*shards, *placed)


_HBM = pl.BlockSpec(memory_space=pltpu.HBM)
_SEM = pl.BlockSpec(memory_space=pltpu.SEMAPHORE)
_FLOWING = pltpu.SideEffectType.DATAFLOW_SIDE_EFFECTING


def _peers6(x, y, c):
    chips = [(1 - x, y), (x, 1 - y), (1 - x, 1 - y)]
    return [(2 * k + e, chip, c if e == 0 else 1 - c) for k, chip in enumerate(chips) for e in range(2)]


def _ag_start(shards, jshard, name):
    n = len(shards)

    def body(*refs):
        out_refs = refs[2 * n:4 * n]
        send_sems, recv_sems, token = refs[4 * n:]
        x, y, c = _where_am_i()
        j = 2 * x + y
        for a in range(n):
            for slot, (cx, cy), tc in _peers6(x, y, c):
                pltpu.make_async_remote_copy(src_ref=_half_of_shard(out_refs[a], "win", c),
                                             dst_ref=_half_of_shard(out_refs[n + a].at[j], "win", c),
                                             send_sem=send_sems.at[6 * a + slot], recv_sem=recv_sems.at[6 * a + slot],
                                             device_id=(cx, cy, tc), device_id_type=MESH).start()
        token[...] = jnp.zeros_like(token)

    placed = [lax.dynamic_update_slice(jnp.zeros((4,) + s.shape, s.dtype), s[None], (jshard,) + (0,) * s.ndim) for s in shards]
    hbm = lambda t: pltpu.with_memory_space_constraint(t, pltpu.HBM)
    both = list(shards) + placed
    outs = pl.pallas_call(
        body, name=name, in_specs=[_HBM] * (2 * n), out_specs=[_HBM] * (2 * n) + [_SEM, _SEM, pl.BlockSpec(memory_space=pltpu.VMEM)],
        out_shape=[pltpu.HBM(p.shape, p.dtype) for p in both] + [pltpu.SemaphoreType.DMA((6 * n,)), pltpu.SemaphoreType.DMA((6 * n,)),
                                                                jax.ShapeDtypeStruct((SUB, LANE), f32)],
        input_output_aliases={a: a for a in range(2 * n)},
        compiler_params=pltpu.CompilerParams(has_side_effects=_FLOWING))(*[hbm(t) for t in both])
    return outs[:2 * n], outs[2 * n], outs[2 * n + 1], outs[2 * n + 2]


def _ag_wait(bufs, send_sems, recv_sems, after, name):
    n = len(bufs) // 2

    def body(*refs):
        sh_refs, g_refs = refs[:n], refs[n:2 * n]
        send_sems, recv_sems = refs[2 * n], refs[2 * n + 1]
        x, y, c = _where_am_i()
        for a in range(n):
            for slot, (cx, cy), tc in _peers6(x, y, c):
                cp = pltpu.make_async_remote_copy(src_ref=_half_of_shard(sh_refs[a], "win", c),
                                                  dst_ref=_half_of_shard(g_refs[a].at[2 * cx + cy], "win", tc),
                                                  send_sem=send_sems.at[6 * a + slot], recv_sem=recv_sems.at[6 * a + slot],
                                                  device_id=(cx, cy, tc), device_id_type=MESH)
                cp.wait_send()
                cp.wait_recv()

    outs = pl.pallas_call(body, name=name, in_specs=[_HBM] * (2 * n) + [_SEM, _SEM, pl.BlockSpec(memory_space=pl.ANY)],
                          out_specs=[_HBM] * (2 * n), out_shape=[pltpu.HBM(b.shape, b.dtype) for b in bufs],
                          input_output_aliases={a: a for a in range(2 * n)},
                          compiler_params=pltpu.CompilerParams(has_side_effects=_FLOWING))(*bufs, send_sems, recv_sems, after)
    return outs[n:]


def _rs_swap(parts, kinds, name):
    n = len(parts)

    def body(*refs):
        p_refs, got_refs = refs[:n], refs[n:2 * n]
        send_sems, recv_sems = refs[2 * n:]
        x, y, c = _where_am_i()

        def cp(a, src, dst):
            return pltpu.make_async_remote_copy(src_ref=src, dst_ref=dst, send_sem=send_sems.at[a], recv_sem=recv_sems.at[a],
                                                device_id=(x, y, 1 - c), device_id_type=MESH)

        for a in range(n):
            for src, dst in _leading_pieces(_half_of_full(p_refs[a], kinds[a], 1 - c), got_refs[a]):
                cp(a, src, dst).start()
        for a in range(n):
            cp(a, got_refs[a], got_refs[a]).wait()

    anyspec = pl.BlockSpec(memory_space=pl.ANY)
    return pl.pallas_call(body, name=name, in_specs=[anyspec] * n, out_specs=[anyspec] * n,
                          out_shape=[jax.ShapeDtypeStruct(_half_shape(p.shape, k), p.dtype) for p, k in zip(parts, kinds)],
                          scratch_shapes=[pltpu.SemaphoreType.DMA((n,)), pltpu.SemaphoreType.DMA((n,))],
                          compiler_params=pltpu.CompilerParams(has_side_effects=True))(*parts)


def _rs_owners(qbs, kinds, full_shapes):
    n = len(qbs)

    def body(*refs):
        q_refs, got_refs = refs[:n], refs[n:2 * n]
        send_sems, recv_sems = refs[2 * n:]
        x, y, c = _where_am_i()
        chips = [(1 - x, y), (x, 1 - y), (1 - x, 1 - y)]
        ds = []
        for a in range(n):
            for k, (cx, cy) in enumerate(chips):
                d = pltpu.make_async_remote_copy(src_ref=_shard_of_half(q_refs[a], kinds[a], 2 * cx + cy), dst_ref=got_refs[a].at[k],
                                                 send_sem=send_sems.at[3 * a + k], recv_sem=recv_sems.at[3 * a + k],
                                                 device_id=(cx, cy, c), device_id_type=MESH)
                d.start()
                ds.append(d)
        for d in ds:
            d.wait()

    anyspec = pl.BlockSpec(memory_space=pl.ANY)
    return pl.pallas_call(body, name="rs_to_owners", in_specs=[anyspec] * n, out_specs=[anyspec] * n,
                          out_shape=[jax.ShapeDtypeStruct((3,) + _shard_half_shape(fs, k), bf16) for fs, k in zip(full_shapes, kinds)],
                          scratch_shapes=[pltpu.SemaphoreType.DMA((3 * n,)), pltpu.SemaphoreType.DMA((3 * n,))],
                          compiler_params=pltpu.CompilerParams(has_side_effects=True))(*qbs)


def _rs_owners_start(qbs, kinds, full_shapes, name):
    n = len(qbs)

    def body(*refs):
        q_refs, got_refs = refs[2 * n:3 * n], refs[3 * n:4 * n]
        send_sems, recv_sems, token = refs[4 * n:]
        x, y, c = _where_am_i()
        for a in range(n):
            for k, (cx, cy) in enumerate([(1 - x, y), (x, 1 - y), (1 - x, 1 - y)]):
                pltpu.make_async_remote_copy(src_ref=_shard_of_half(q_refs[a], kinds[a], 2 * cx + cy), dst_ref=got_refs[a].at[k],
                                             send_sem=send_sems.at[3 * a + k], recv_sem=recv_sems.at[3 * a + k],
                                             device_id=(cx, cy, c), device_id_type=MESH).start()
        token[...] = jnp.zeros_like(token)

    hbm = lambda t: pltpu.with_memory_space_constraint(t, pltpu.HBM)
    lands = [lax.empty((3,) + _shard_half_shape(fs, k), bf16) for fs, k in zip(full_shapes, kinds)]
    both = list(qbs) + lands
    outs = pl.pallas_call(
        body, name=name, in_specs=[_HBM] * (2 * n), out_specs=[_HBM] * (2 * n) + [_SEM, _SEM, pl.BlockSpec(memory_space=pltpu.VMEM)],
        out_shape=[pltpu.HBM(t.shape, t.dtype) for t in both] + [pltpu.SemaphoreType.DMA((3 * n,)), pltpu.SemaphoreType.DMA((3 * n,)),
                                                                jax.ShapeDtypeStruct((SUB, LANE), f32)],
        input_output_aliases={a: a for a in range(2 * n)},
        compiler_params=pltpu.CompilerParams(has_side_effects=_FLOWING))(*[hbm(t) for t in both])
    return outs[:2 * n], outs[2 * n], outs[2 * n + 1], outs[2 * n + 2]


def _rs_owners_wait(bufs, send_sems, recv_sems, kinds, after, name):
    n = len(bufs) // 2

    def body(*refs):
        q_refs, got_refs = refs[:n], refs[n:2 * n]
        send_sems, recv_sems = refs[2 * n], refs[2 * n + 1]
        x, y, c = _where_am_i()
        for a in range(n):
            for k, (cx, cy) in enumerate([(1 - x, y), (x, 1 - y), (1 - x, 1 - y)]):
                cp = pltpu.make_async_remote_copy(src_ref=_shard_of_half(q_refs[a], kinds[a], 2 * cx + cy), dst_ref=got_refs[a].at[k],
                                                  send_sem=send_sems.at[3 * a + k], recv_sem=recv_sems.at[3 * a + k],
                                                  device_id=(cx, cy, c), device_id_type=MESH)
                cp.wait_send()
                cp.wait_recv()

    outs = pl.pallas_call(body, name=name, in_specs=[_HBM] * (2 * n) + [_SEM, _SEM, pl.BlockSpec(memory_space=pl.ANY)],
                          out_specs=[_HBM] * (2 * n), out_shape=[pltpu.HBM(b.shape, b.dtype) for b in bufs],
                          input_output_aliases={a: a for a in range(2 * n)},
                          compiler_params=pltpu.CompilerParams(has_side_effects=_FLOWING))(*bufs, send_sems, recv_sems, after)
    return outs[n:]


def _rs_join(bufs, kinds):
    n = len(bufs)

    def body(*refs):
        out_refs = refs[n:2 * n]
        send_sems, recv_sems = refs[2 * n:]
        x, y, c = _where_am_i()

        def cp(a, h):
            blk = _half_of_shard(out_refs[a], kinds[a], h)
            return pltpu.make_async_remote_copy(src_ref=blk, dst_ref=blk, send_sem=send_sems.at[a], recv_sem=recv_sems.at[a],
                                                device_id=(x, y, 1 - c), device_id_type=MESH)

        for a in range(n):
            cp(a, c).start()
        for a in range(n):
            cp(a, c).wait_send()
            cp(a, 1 - c).wait_recv()

    anyspec = pl.BlockSpec(memory_space=pl.ANY)
    return pl.pallas_call(body, name="rs_join_halves", in_specs=[anyspec] * n, out_specs=[anyspec] * n,
                          out_shape=[jax.ShapeDtypeStruct(b.shape, b.dtype) for b in bufs],
                          input_output_aliases={a: a for a in range(n)},
                          scratch_shapes=[pltpu.SemaphoreType.DMA((n,)), pltpu.SemaphoreType.DMA((n,))],
                          compiler_params=pltpu.CompilerParams(has_side_effects=True))(*bufs)


def _all_reduce_small(g):
    R, C = g.shape
    H = R // 2
    NP = _nchunks(H, SUB)
    PR = H // NP

    def body(g_ref, out_ref, sib_ref, chip_ref, send_sems, recv_sems):
        x, y, c = _where_am_i()
        j = 2 * x + y
        sib = (x, y, 1 - c)
        chips = [(1 - x, y), (x, 1 - y), (1 - x, 1 - y)]
        rows = pl.ds(pl.multiple_of(c * H, SUB), H)

        def cp(k, src, dst, to):
            return pltpu.make_async_remote_copy(src_ref=src, dst_ref=dst, send_sem=send_sems.at[k], recv_sem=recv_sems.at[k],
                                                device_id=to, device_id_type=MESH)

        def pieces(k, src, dst, to):
            for q in range(NP):
                cp(k, src.at[pl.ds(q * PR, PR)], dst.at[pl.ds(q * PR, PR)], to).start()

        for half in range(2):
            pieces(0, g_ref.at[pl.ds(half * H, H)], sib_ref.at[pl.ds(half * H, H)], sib)
        cp(0, g_ref, sib_ref, sib).wait()
        chip_ref[j] = g_ref[rows, :] + sib_ref[rows, :]
        for k, (cx, cy) in enumerate(chips):
            pieces(1 + k, chip_ref.at[j], chip_ref.at[j], (cx, cy, c))
        for k, (cx, cy) in enumerate(chips):
            blk = chip_ref.at[2 * cx + cy]
            cp(1 + k, blk, blk, (cx, cy, c)).wait()
        out_ref[rows, :] = ((chip_ref[0] + chip_ref[1]) + chip_ref[2]) + chip_ref[3]
        other = out_ref.at[pl.ds(pl.multiple_of((1 - c) * H, SUB), H)]
        pieces(4, out_ref.at[rows], out_ref.at[rows], sib)
        cp(4, other, other, sib).wait()

    vm = pl.BlockSpec(memory_space=pltpu.VMEM)
    return pl.pallas_call(body, name="all_reduce_small", in_specs=[vm], out_specs=vm,
                          out_shape=jax.ShapeDtypeStruct((R, C), f32),
                          scratch_shapes=[pltpu.VMEM((R, C), f32), pltpu.VMEM((4, H, C), f32),
                                          pltpu.SemaphoreType.DMA((5,)), pltpu.SemaphoreType.DMA((5,))],
                          compiler_params=pltpu.CompilerParams(has_side_effects=True, vmem_limit_bytes=VMEM_LIMIT))(g)


SPLIT_MIN_ELEMS = 1 << 16


def _all_reduce_many(gs):
    n = len(gs)
    split = [g.ndim == 3 and g.shape[0] % 2 == 0 and g.size >= SPLIT_MIN_ELEMS for g in gs]
    part_shape = [((g.shape[0] // 2,) + g.shape[1:]) if s else g.shape for g, s in zip(gs, split)]
    n_split = sum(split)

    def body(*refs):
        g, out, sibs, chipb = refs[:n], refs[n:2 * n], refs[2 * n:3 * n], refs[3 * n:4 * n]
        send_sems, recv_sems = refs[4 * n:]
        x, y, c = _where_am_i()
        j = 2 * x + y
        sib = (x, y, 1 - c)
        chips = [(1 - x, y), (x, 1 - y), (1 - x, 1 - y)]

        def cp(k, src, dst, to):
            return pltpu.make_async_remote_copy(src_ref=src, dst_ref=dst, send_sem=send_sems.at[k], recv_sem=recv_sems.at[k],
                                                device_id=to, device_id_type=MESH)

        def part(a, h):
            return pl.ds(h * part_shape[a][0], part_shape[a][0]) if split[a] else Ellipsis

        def mine(ref, a, h):
            return ref.at[part(a, h)] if split[a] else ref

        swaps = [cp(a, g[a], sibs[a], sib) for a in range(n)]
        for d in swaps:
            d.start()
        for a in range(n):
            swaps[a].wait()
            chipb[a][j] = g[a][part(a, c)] + sibs[a][part(a, c)]
        sends = [cp(n + 3 * a + k, chipb[a].at[j], chipb[a].at[j], (cx, cy, c)) for a in range(n) for k, (cx, cy) in enumerate(chips)]
        for d in sends:
            d.start()
        for a in range(n):
            for k, (cx, cy) in enumerate(chips):
                blk = chipb[a].at[2 * cx + cy]
                cp(n + 3 * a + k, blk, blk, (cx, cy, c)).wait_recv()
            out[a][part(a, c)] = ((chipb[a][0] + chipb[a][1]) + chipb[a][2]) + chipb[a][3]
        for d in sends:
            d.wait_send()
        backs = [(a, cp(4 * n + i, mine(out[a], a, c), mine(out[a], a, c), sib)) for i, a in enumerate([a for a in range(n) if split[a]])]
        for _, d in backs:
            d.start()
        for i, (a, d) in enumerate(backs):
            d.wait_send()
            cp(4 * n + i, mine(out[a], a, 1 - c), mine(out[a], a, 1 - c), sib).wait_recv()

    vm = pl.BlockSpec(memory_space=pltpu.VMEM)
    nsem = 4 * n + n_split
    return pl.pallas_call(body, name="all_reduce_small_grads", in_specs=[vm] * n, out_specs=[vm] * n,
                          out_shape=[jax.ShapeDtypeStruct(g.shape, f32) for g in gs],
                          scratch_shapes=([pltpu.VMEM(g.shape, f32) for g in gs] + [pltpu.VMEM((4,) + ps, f32) for ps in part_shape]
                                          + [pltpu.SemaphoreType.DMA((nsem,)), pltpu.SemaphoreType.DMA((nsem,))]),
                          compiler_params=pltpu.CompilerParams(has_side_effects=True, vmem_limit_bytes=VMEM_LIMIT))(*gs)


def _adamw_many(ws, gs, ms, vs, name):
    n = len(ws)
    c1 = 1.0 / (1.0 - ADAM_B1 ** ADAM_STEP)
    c2 = 1.0 / (1.0 - ADAM_B2 ** ADAM_STEP)

    def body(*refs):
        for a in range(n):
            w_ref, g_ref, m_ref, v_ref, d_ref, nm_ref, nv_ref = (refs[i * n + a] for i in range(7))
            g_ = g_ref[...]
            nm = ADAM_B1 * m_ref[...] + (1.0 - ADAM_B1) * g_
            nv = ADAM_B2 * v_ref[...] + (1.0 - ADAM_B2) * (g_ * g_)
            nm_ref[...] = nm
            nv_ref[...] = nv
            d_ref[...] = -ADAM_LR * ((nm * c1) / (jnp.sqrt(nv * c2) + ADAM_EPS) + ADAM_WD * w_ref[...])

    vm = pl.BlockSpec(memory_space=pltpu.VMEM)
    sh = [jax.ShapeDtypeStruct(w.shape, f32) for w in ws]
    outs = pl.pallas_call(body, name=name, in_specs=[vm] * (4 * n), out_specs=[vm] * (3 * n), out_shape=sh * 3,
                          compiler_params=pltpu.CompilerParams(vmem_limit_bytes=VMEM_LIMIT))(*ws, *gs, *ms, *vs)
    return outs[:n], outs[n:2 * n], outs[2 * n:]


def _pack_flat(arrs, rows_mult):
    flat = jnp.concatenate([a.reshape(-1) for a in arrs])
    n = flat.shape[0]
    tot = -(-n // (rows_mult * LANE)) * rows_mult * LANE
    return jnp.pad(flat, (0, tot - n)).reshape(-1, LANE)


def _unpack_flat(buf, shapes):
    flat = buf.reshape(-1)
    out, o = [], 0
    for s in shapes:
        n = math.prod(s)
        out.append(flat[o:o + n].reshape(s))
        o += n
    return out


_BIG = ("mem_kv_w", "w_out", "a_w_in", "b_w_in", "c_w_in", "d_w_in")
SMALL_ROWS_MULT = 256


def _row8(v):
    v = v.reshape(-1, v.shape[-1])
    return jnp.pad(v, ((0, SUB - v.shape[0]), (0, 0)))


def kernel(x, mem, mem_kv_w, ln_g, ln_b, w_out, hgrn_lb_logits, a_w_in, a_w_s, a_b_s, b_w_in, b_norm_g, c_w_in, c_w_pool, c_scale, d_w_in, d_conv_w, d_conv_b, d_w_gx, d_b_gx, d_w_ga, d_b_ga, d_a_param, loss_target, m_mem_kv_w, m_ln_g, m_ln_b, m_w_out, m_hgrn_lb_logits, m_a_w_in, m_a_w_s, m_a_b_s, m_b_w_in, m_b_norm_g, m_c_w_in, m_c_w_pool, m_c_scale, m_d_w_in, m_d_conv_w, m_d_conv_b, m_d_w_gx, m_d_b_gx, m_d_w_ga, m_d_b_ga, m_d_a_param, v_mem_kv_w, v_ln_g, v_ln_b, v_w_out, v_hgrn_lb_logits, v_a_w_in, v_a_w_s, v_a_b_s, v_b_w_in, v_b_norm_g, v_c_w_in, v_c_w_pool, v_c_scale, v_d_w_in, v_d_conv_w, v_d_conv_b, v_d_w_gx, v_d_b_gx, v_d_w_ga, v_d_b_ga, v_d_a_param):
    names = ["mem_kv_w", "ln_g", "ln_b", "w_out", "hgrn_lb_logits", "a_w_in", "a_w_s", "a_b_s", "b_w_in", "b_norm_g", "c_w_in",
             "c_w_pool", "c_scale", "d_w_in", "d_conv_w", "d_conv_b", "d_w_gx", "d_b_gx", "d_w_ga", "d_b_ga", "d_a_param"]
    w = dict(mem_kv_w=mem_kv_w, ln_g=ln_g, ln_b=ln_b, w_out=w_out, hgrn_lb_logits=hgrn_lb_logits, a_w_in=a_w_in, a_w_s=a_w_s,
             a_b_s=a_b_s, b_w_in=b_w_in, b_norm_g=b_norm_g, c_w_in=c_w_in, c_w_pool=c_w_pool, c_scale=c_scale, d_w_in=d_w_in,
             d_conv_w=d_conv_w, d_conv_b=d_conv_b, d_w_gx=d_w_gx, d_b_gx=d_b_gx, d_w_ga=d_w_ga, d_b_ga=d_b_ga, d_a_param=d_a_param)
    m = dict(zip(names, [m_mem_kv_w, m_ln_g, m_ln_b, m_w_out, m_hgrn_lb_logits, m_a_w_in, m_a_w_s, m_a_b_s, m_b_w_in, m_b_norm_g,
                         m_c_w_in, m_c_w_pool, m_c_scale, m_d_w_in, m_d_conv_w, m_d_conv_b, m_d_w_gx, m_d_b_gx, m_d_w_ga,
                         m_d_b_ga, m_d_a_param]))
    v = dict(zip(names, [v_mem_kv_w, v_ln_g, v_ln_b, v_w_out, v_hgrn_lb_logits, v_a_w_in, v_a_w_s, v_a_b_s, v_b_w_in, v_b_norm_g,
                         v_c_w_in, v_c_w_pool, v_c_scale, v_d_w_in, v_d_conv_w, v_d_conv_b, v_d_w_gx, v_d_b_gx, v_d_w_ga,
                         v_d_b_ga, v_d_a_param]))
    xi, yi = lax.axis_index("x"), lax.axis_index("y")
    jshard = 2 * xi + yi
    x2 = x[0]
    mem2 = mem[0]
    tgt2 = loss_target[0]

    w_in_sh = [w[n][0].astype(bf16) for n in _BIG[2:]]
    w_out_sh = w_out.astype(bf16)
    gath0 = _ag_weights([mem_kv_w.astype(bf16), w_out_sh[0], w_in_sh[0]], ("rows", "win", "win"), jshard)
    w_kv = gath0[0].reshape(D_MODEL, 2 * XW)
    pending = [None]
    tie = gath0[0]
    for l in range(1, DEPTH):
        shards, _ = lax.optimization_barrier(([w_in_sh[l], w_out_sh[l]], tie))
        bufs, ssem, rsem, tie = _ag_start(shards, jshard, f"gather_start{l}")
        pending.append((bufs, ssem, rsem))
    x2, _ = lax.optimization_barrier((x2, tie))

    def layer_weights(g_in, g_out):
        return (g_in.transpose(1, 0, 2).reshape(D_MODEL, -1), g_in.transpose(0, 2, 1).reshape(-1, D_MODEL),
                g_out.reshape(D_MODEL, D_MODEL), g_out.transpose(2, 0, 1).reshape(D_MODEL, D_MODEL))

    lw = [layer_weights(gath0[2], gath0[1])]

    def gather_small(shard):
        z = jnp.zeros((4, POOL_GROUP), f32)
        return lax.dynamic_update_slice(z, shard.reshape(1, POOL_GROUP), (jshard, 0))

    sm_sh = jnp.concatenate([gather_small(b_norm_g), gather_small(c_scale), gather_small(d_conv_b), gather_small(d_a_param)]
                            + [gather_small(d_conv_w[:, r]) for r in range(4)], axis=0)
    ci = lax.axis_index("c")
    sm_all = _all_reduce_small(_pack_flat([jnp.where(ci == 0, sm_sh, 0.0)], SUB * 2))
    sm = _unpack_flat(sm_all, [(8, 4 * POOL_GROUP)])[0]
    ng_full, scale_full, convb_full, ap_full = sm[0:1], sm[1:2], sm[2:3], sm[3:4]
    convw_full = sm[4:8]

    tril = jnp.tril(jnp.ones((HD, HD), bool))
    wtri = jnp.where(tril, a_w_s[0], 0.0)
    wbd = jnp.zeros((TOK, TOK), f32)
    for g in range(4):
        wbd = lax.dynamic_update_slice(wbd, c_w_pool[0, g], (g * POOL_GROUP, g * POOL_GROUP))
    kh, khT, vh, vhT, p_soft = _prep(mem2, w_kv, hgrn_lb_logits)
    prm = [
        dict(wtri=wtri.astype(bf16), wtriT=wtri.transpose(0, 2, 1).astype(bf16),
             bcolb=jnp.broadcast_to(a_b_s[0][:, :, None], (NH, HD, HD))),
        dict(lb=p_soft[1:2], ng=ng_full),
        dict(wbd=wbd.astype(bf16), wbdT=wbd.T.astype(bf16), scale=scale_full),
        dict(cw=_row8(convw_full), cb=convb_full, wgx=d_w_gx[0].astype(bf16), wgxT=d_w_gx[0].transpose(0, 2, 1).astype(bf16),
             bgx=d_b_gx.reshape(1, TOK), wga=d_w_ga[0].astype(bf16), wgaT=d_w_ga[0].transpose(0, 2, 1).astype(bf16),
             bga=d_b_ga.reshape(1, TOK), ap=ap_full),
    ]

    acts = []
    h = x2
    for l in range(DEPTH):
        if l:
            bufs, ssem, rsem = pending[l]
            lw.append(layer_weights(*_ag_wait(bufs, ssem, rsem, h, f"gather_wait{l}")))
        outs = _fwd_layer(l, h, lw[l][0], lw[l][2], ln_g[l:l + 1], ln_b[l:l + 1], khT, vh, prm[l],
                          tgt2 if l == DEPTH - 1 else None)
        nfix = 4 if l == DEPTH - 1 else 3
        acts.append(dict(xin=h, proj=outs[1], z=outs[2], saves=outs[nfix:]))
        if l == DEPTH - 1:
            loss_part = outs[3]
        h = outs[0]
    loss = lax.psum(0.5 / D_MODEL * jnp.sum(loss_part), ("x", "y", "c"))

    dh = h
    dln = [None] * DEPTH
    dks, dvs = [None] * DEPTH, [None] * DEPTH
    sgr = [None] * DEPTH
    jc = jnp.stack([jshard, ci]).astype(jnp.int32)
    lkinds = ("win", "rows")
    q32s, flying = [None] * DEPTH, [None] * DEPTH
    for l in reversed(range(DEPTH)):
        a = acts[l]
        (dxin, dproj, mixedb, dyb, dln[l], dks[l], dvs[l]), sgr[l] = _bwd_layer(
            l, dh, a["z"], a["proj"], lw[l][1], lw[l][3], ln_g[l:l + 1], kh, khT, vh, vhT, prm[l], a["saves"])
        if _OFFS[l]["W"] // 4 % LANE:
            gw_in = _tn_gemm(a["xin"], dproj, f"grad_w_in{l}", 1).reshape(D_MODEL, 4, -1).transpose(1, 0, 2)
        else:
            gw_in = _tn_gemm_sharded(a["xin"], dproj, f"grad_w_in{l}")
        parts = [gw_in, _tn_gemm(mixedb, dyb, f"grad_w_out{l}", 1)]
        lk = lkinds
        if l == 0:
            parts.append(_kv_bwd(mem2, dks, dvs))
            lk = lkinds + ("rows",)
        gots = _rs_swap(parts, lk, f"rs_swap_halves{l}")
        sums = [_add_sibling(p, g, k, jc, f"rs_add_sibling{l}_{i}") for i, (p, g, k) in enumerate(zip(parts, gots, lk))]
        q32s[l] = [s[0] for s in sums]
        shapes = [p.shape for p in parts]
        if l:
            bufs, ssem, rsem, tok = _rs_owners_start([s[1] for s in sums], lk, shapes, f"rs_owners_start{l}")
            flying[l] = (bufs, ssem, rsem)
            dxin, _ = lax.optimization_barrier((dxin, tok))
        else:
            last_got = _rs_owners([s[1] for s in sums], lk, shapes)
        dh = dxin
    grad_x = dh[None]
    fin, fin_kinds = {}, []
    for l in range(DEPTH):
        lk = lkinds + (("rows",) if l == 0 else ())
        got = last_got if l == 0 else _rs_owners_wait(*flying[l], lk, grad_x, f"rs_owners_wait{l}")
        fin[l] = [_add_chips(q, r, k, jc, f"rs_add_chips{l}_{i}") for i, (q, r, k) in enumerate(zip(q32s[l], got, lk))]
        fin_kinds += list(lk)
    joined = _rs_join([t for l in range(DEPTH) for t in fin[l]], tuple(fin_kinds))
    by_layer, o = [], 0
    for l in range(DEPTH):
        by_layer.append(joined[o:o + len(fin[l])])
        o += len(fin[l])
    gbig = {"mem_kv_w": by_layer[0][2], "w_out": jnp.stack([by_layer[l][1] for l in range(DEPTH)])}
    for l, n in enumerate(_BIG[2:]):
        gbig[n] = by_layer[l][0]
    g_sh, d_sh, m_sh, v_sh = {}, {}, {}, {}
    for n in _BIG:
        as2d = lambda t: t.reshape(-1, t.shape[-1])
        upd = _adamw(as2d(w[n]), as2d(gbig[n]), as2d(m[n]), as2d(v[n]), f"adamw_{n}")
        g_sh[n] = gbig[n].reshape(w[n].shape)
        d_sh[n], m_sh[n], v_sh[n] = (u.reshape(w[n].shape) for u in upd)

    dbs, dlogits = _small_finish(sgr[0]["dbacc"], p_soft, sgr[1]["dlb"])
    gs = {
        "ln_g": jnp.concatenate([dln[l][0:1] for l in range(DEPTH)], axis=0),
        "ln_b": jnp.concatenate([dln[l][1:2] for l in range(DEPTH)], axis=0),
        "hgrn_lb_logits": dlogits,
        "a_w_s": sgr[0]["dwtri"][None],
        "a_b_s": dbs[:, 0:NH].T[None],
        "b_norm_g": sgr[1]["dng"][0:1],
        "c_w_pool": jnp.stack([sgr[2]["dwbd"][g * POOL_GROUP:(g + 1) * POOL_GROUP, g * POOL_GROUP:(g + 1) * POOL_GROUP]
                               for g in range(4)])[None],
        "c_scale": sgr[2]["dscale"][0:1],
        "d_conv_w": sgr[3]["dcw"][0:4][None],
        "d_conv_b": sgr[3]["dvec"][3:4],
        "d_w_gx": sgr[3]["dwgx"][None],
        "d_b_gx": sgr[3]["dvec"][1:2].reshape(1, NH, HD),
        "d_w_ga": sgr[3]["dwga"][None],
        "d_b_ga": sgr[3]["dvec"][2:3].reshape(1, NH, HD),
        "d_a_param": sgr[3]["dvec"][0:1],
    }
    small = [n for n in names if n not in _BIG]
    drop1 = lambda t: t.reshape(t.shape[1:]) if t.ndim > 2 and t.shape[0] == 1 else t
    gsum = dict(zip(small, _all_reduce_many([drop1(gs[n]) for n in small])))
    for n in ("b_norm_g", "c_scale", "d_conv_b", "d_a_param"):
        gsum[n] = lax.dynamic_slice(gsum[n], (0, jshard * POOL_GROUP), (1, POOL_GROUP))
    gsum["d_conv_w"] = lax.dynamic_slice(gsum["d_conv_w"], (0, jshard * POOL_GROUP), (4, POOL_GROUP))
    upd = _adamw_many(*[[drop1(d[n]) for n in small] for d in (w, gsum, m, v)], "adamw_small")
    gsum = {n: gsum[n].reshape(w[n].shape) for n in small}
    d_sm, m_sm, v_sm = ({n: u.reshape(w[n].shape) for n, u in zip(small, us)} for us in upd)

    grads = {**gsum, **g_sh}
    deltas = {**d_sm, **d_sh}
    new_m = {**m_sm, **m_sh}
    new_v = {**v_sm, **v_sh}
    return (loss, grad_x, *[grads[n] for n in names], *[deltas[n] for n in names], *[new_m[n] for n in names],
            *[new_v[n] for n in names])
```

```python
import functools
import math

import jax
import jax.numpy as jnp
from jax import lax
from jax.experimental import pallas as pl
from jax.experimental.pallas import tpu as pltpu

f32 = jnp.float32
bf16 = jnp.bfloat16
MM = bf16

D_MODEL = 1024
TOK = 768
XW = 256
XHEADS = 4
XDIM = 64
HD = 128
NH = TOK // HD
CHUNK = 16
POOL_GROUP = 192
DEPTH = 4
ALPHA = (2 * DEPTH) ** 0.25
LN_EPS = 1e-5
RMS_EPS = 1e-6
LRU_C = 8.0
ADAM_LR, ADAM_B1, ADAM_B2, ADAM_EPS, ADAM_WD, ADAM_STEP = 0.001, 0.9, 0.999, 1e-08, 0.01, 10

_TS = (256, 256, 256, 256)
HGRN_SUB = 128
TK = 512
SUB = 8
LANE = 128
VMEM_LIMIT = 58 * 1024 * 1024

_OFFS = (
    dict(u=0, v=768, qx=1536, gate=1792, W=2816),
    dict(q=0, f=768, i=1536, qx=2304, gate=2560, W=3584),
    dict(p=0, qx=768, gate=1024, W=2048),
    dict(xb=0, qx=768, gate=1024, W=2048),
)
_PRM = (
    ("wtri", "wtriT", "bcolb"),
    ("lb", "ng"),
    ("wbd", "wbdT", "scale"),
    ("cw", "cb", "wgx", "wgxT", "bgx", "wga", "wgaT", "bga", "ap"),
)
MESH = pl.DeviceIdType.MESH


def _mm(a, b):
    return jnp.dot(a.astype(MM), b.astype(MM), preferred_element_type=f32)


def _mm_nt(a, b):
    return lax.dot_general(a.astype(MM), b.astype(MM), (((1,), (1,)), ((), ())), preferred_element_type=f32)


def _mm_tn(a, b):
    return lax.dot_general(a.astype(MM), b.astype(MM), (((0,), (0,)), ((), ())), preferred_element_type=f32)


def _mm_sel(sel, b):
    s = sel.astype(bf16)
    hi = b.astype(bf16)
    lo = (b - hi.astype(f32)).astype(bf16)
    return jnp.dot(s, hi, preferred_element_type=f32) + jnp.dot(s, lo, preferred_element_type=f32)


def _sig(x):
    return jax.nn.sigmoid(x)


_GC = math.sqrt(2.0 / math.pi)


def _gelu(x):
    t = jnp.tanh(_GC * (x + 0.044715 * x * x * x))
    return 0.5 * x * (1.0 + t), t


def _gelu_grad(x, t):
    return 0.5 * (1.0 + t) + 0.5 * x * (1.0 - t * t) * _GC * (1.0 + 3.0 * 0.044715 * x * x)


def _rowsum(x):
    return jnp.sum(x, axis=0, keepdims=True)


def _lmean(x):
    return jnp.mean(x, axis=-1, keepdims=True)


def _ln(z):
    mu = _lmean(z)
    zc = z - mu
    rstd = lax.rsqrt(_lmean(zc * zc) + LN_EPS)
    return zc * rstd, rstd


def _ln_bwd(dxh, xhat, rstd):
    return rstd * (dxh - _lmean(dxh) - xhat * _lmean(dxh * xhat))


def _hs(h):
    return slice(h * HD, (h + 1) * HD)


def _expm1(x):
    small = x * (1.0 + x * 0.5 * (1.0 + x * (1.0 / 3.0) * (1.0 + x * 0.25 * (1.0 + x * 0.2 * (1.0 + x * (1.0 / 6.0))))))
    return jnp.where(jnp.abs(x) < 0.25, small, jnp.exp(x) - 1.0)


def _softplus(x):
    e = jnp.exp(-jnp.abs(x))
    l1p = jnp.where(e < 1e-4, e - 0.5 * e * e, jnp.log(1.0 + e))
    return jnp.maximum(x, 0.0) + l1p


def _scan_fwd(a, b):
    n = a.shape[0]
    row = lax.broadcasted_iota(jnp.int32, a.shape, 0)
    d = 1
    while d < n:
        if d % SUB:
            m = row >= d
            b = jnp.where(m, a * pltpu.roll(b, d, 0) + b, b)
            a = jnp.where(m, a * pltpu.roll(a, d, 0), a)
        else:
            b = a * jnp.concatenate([jnp.zeros((d,) + b.shape[1:], f32), b[:n - d]], axis=0) + b
            a = a * jnp.concatenate([jnp.ones((d,) + a.shape[1:], f32), a[:n - d]], axis=0)
        d *= 2
    return a, b


def _scan_bwd(a, b):
    n = a.shape[0]
    row = lax.broadcasted_iota(jnp.int32, a.shape, 0)
    d = 1
    while d < n:
        if d % SUB:
            m = row < n - d
            b = jnp.where(m, a * pltpu.roll(b, n - d, 0) + b, b)
            a = jnp.where(m, a * pltpu.roll(a, n - d, 0), a)
        else:
            b = a * jnp.concatenate([b[d:], jnp.zeros((d,) + b.shape[1:], f32)], axis=0) + b
            a = a * jnp.concatenate([a[d:], jnp.ones((d,) + a.shape[1:], f32)], axis=0)
        d *= 2
    return a, b


def _chunk_mats(n):
    r = lax.broadcasted_iota(jnp.int32, (n, n), 0)
    c = lax.broadcasted_iota(jnp.int32, (n, n), 1)
    same = (r // CHUNK) == (c // CHUNK)
    return same, jnp.logical_and(same, c <= r)


def _pool_w(shape):
    lane = lax.broadcasted_iota(jnp.int32, shape, 1)
    return jnp.where(lane < POOL_GROUP, 2, jnp.where(lane < 2 * POOL_GROUP, 4, jnp.where(lane < 3 * POOL_GROUP, 8, 16)))


def _pool_pick(r1, r2, r3, r4):
    lane = lax.broadcasted_iota(jnp.int32, r1.shape, 1)
    return jnp.where(lane < POOL_GROUP, r1, jnp.where(lane < 2 * POOL_GROUP, r2, jnp.where(lane < 3 * POOL_GROUP, r3, r4)))


def _const_spec(a):
    nd = a.ndim
    return pl.BlockSpec(a.shape, lambda i, _nd=nd: (0,) * _nd, pipeline_mode=pl.Buffered(1))


def _acc_spec(shape):
    nd = len(shape)
    return pl.BlockSpec(shape, lambda i, _nd=nd: (0,) * _nd)


def _params(sem="arbitrary"):
    return pltpu.CompilerParams(dimension_semantics=(sem,), vmem_limit_bytes=VMEM_LIMIT)


def _xattn_fwd(qx, khT_ref, vh_ref):
    xo = jnp.zeros((qx.shape[0], XW), f32)
    ps = []
    for h in range(XHEADS):
        s = _mm(qx, khT_ref[h]) * (XDIM ** -0.5)
        e = jnp.exp(s - jnp.max(s, axis=-1, keepdims=True))
        p = e / jnp.sum(e, axis=-1, keepdims=True)
        xo = xo + _mm(p, vh_ref[h])
        ps.append(p)
    return xo, ps


def _hgrn_parallel(q_raw, fl, lb):
    n = q_raw.shape[0]
    same, tri = _chunk_mats(n)
    sq = _sig(q_raw)
    qf = q_raw * sq
    sgm = _sig(fl)
    f = lb + (1.0 - lb) * sgm
    logf = jnp.log(f)
    k = 1.0 - f
    g = _mm_sel(tri, logf)
    gl = _mm_sel(same, logf)
    eg = jnp.exp(g)
    eng = jnp.exp(-g)
    ee = jnp.exp(gl - g)
    return dict(sq=sq, qf=qf, sgm=sgm, f=f, k=k, eg=eg, eng=eng, ee=ee, q_dec=qf * eg, k_inv=k * eng, k_end=k * ee,
                a=jnp.exp(gl))


def _hgrn_intra(q_dec, k_inv, v):
    n = q_dec.shape[0]
    _, tri = _chunk_mats(HD)
    outs = []
    for h in range(NH):
        blks = []
        for b in range(n // HD):
            rs = slice(b * HD, (b + 1) * HD)
            sc = jnp.where(tri, _mm_nt(q_dec[rs, _hs(h)], k_inv[rs, _hs(h)]), 0.0)
            blks.append(_mm(sc, v[rs, _hs(h)]))
        outs.append(jnp.concatenate(blks, axis=0))
    return jnp.concatenate(outs, axis=-1)


def _cs(c):
    return slice(c * CHUNK, (c + 1) * CHUNK)


def _hgrn_inter_fwd(qdec_s, kend_s, v_s, a_s, oint_s, st_ref, states_s, u_s):
    n = qdec_s.shape[0] // CHUNK
    for c in range(n):
        for h in range(NH):
            u_s[c, h] = _mm_tn(v_s[_cs(c), _hs(h)], kend_s[_cs(c), _hs(h)])
    for h in range(NH):
        st = st_ref[h]
        for c in range(n):
            states_s[c, h] = st
            st = st * a_s[c * CHUNK:c * CHUNK + 1, _hs(h)] + u_s[c, h]
        st_ref[h] = st
    if oint_s is None:
        return
    for c in range(n):
        for h in range(NH):
            oint_s[_cs(c), _hs(h)] = _mm_nt(qdec_s[_cs(c), _hs(h)], states_s[c, h])


def _rms(o):
    outs, rs = [], []
    for h in range(NH):
        oh = o[:, _hs(h)]
        r = lax.rsqrt(_lmean(oh * oh) + RMS_EPS)
        outs.append(oh * r)
        rs.append(r)
    return jnp.concatenate(outs, axis=-1), rs


def _gmlp_core(u_raw, v_raw, wtri_ref, bcolb_ref):
    gu, tu = _gelu(u_raw)
    gv, tv = _gelu(v_raw)
    vns, rstds, mixeds = [], [], []
    for h in range(NH):
        vn, rstd = _ln(gv[:, _hs(h)])
        blks = []
        for n in range(u_raw.shape[0] // HD):
            blks.append(_mm(wtri_ref[h], vn[n * HD:(n + 1) * HD]) + bcolb_ref[h])
        vns.append(vn)
        rstds.append(rstd)
        mixeds.append(jnp.concatenate(blks, axis=0))
    mixed = jnp.concatenate(mixeds, axis=-1)
    return gu, tu, tv, vns, rstds, mixed


def _pool_core(p, carry, row0, wbd_ref):
    ext = jnp.concatenate([carry, p], axis=0)
    r1 = ext + pltpu.roll(ext, 1, 0)
    r2 = r1 + pltpu.roll(r1, 2, 0)
    r3 = r2 + pltpu.roll(r2, 4, 0)
    r4 = r3 + pltpu.roll(r3, 8, 0)
    sel = _pool_pick(r1, r2, r3, r4)[2 * SUB:]
    grow = row0 + lax.broadcasted_iota(jnp.int32, p.shape, 0)
    inv_cnt = 1.0 / jnp.minimum(grow + 1, _pool_w(p.shape)).astype(f32)
    diff = sel * inv_cnt - p
    return diff, inv_cnt, _mm(diff, wbd_ref[...])


def _lru_core(xb, ccar, row0, p):
    ext = jnp.concatenate([ccar, xb], axis=0)
    cw = p["cw"]
    x1, x2, x3 = pltpu.roll(ext, 1, 0)[SUB:], pltpu.roll(ext, 2, 0)[SUB:], pltpu.roll(ext, 3, 0)[SUB:]
    xc = cw[3:4, :] * xb + cw[2:3, :] * x1 + cw[1:2, :] * x2 + cw[0:1, :] * x3 + p["cb"][...]
    gxs, gas = [], []
    for h in range(NH):
        gxs.append(_mm(xc[:, _hs(h)], p["wgx"][h]))
        gas.append(_mm(xc[:, _hs(h)], p["wga"][h]))
    gx = _sig(jnp.concatenate(gxs, axis=-1) + p["bgx"][...])
    ga = _sig(jnp.concatenate(gas, axis=-1) + p["bga"][...])
    sp = _softplus(-p["ap"][...])
    la = -LRU_C * ga * sp
    a = jnp.exp(la)
    grow = row0 + lax.broadcasted_iota(jnp.int32, xb.shape, 0)
    first = grow == 0
    mult = jnp.where(first, 1.0, jnp.sqrt(-_expm1(2.0 * la)))
    bt = mult * gx * xc
    return dict(x1=x1, x2=x2, x3=x3, xc=xc, gx=gx, ga=ga, sp=sp, a=a, mult=mult, bt=bt, first=first)


def _fwd_layer(kind, xin, w_in, w_out, lng, lnb, khT, vh, prm, tgt):
    S = xin.shape[0]
    TS = _TS[kind]
    nt = S // TS
    off = _OFFS[kind]
    W = off["W"]
    last = tgt is not None
    pnames = _PRM[kind]
    pvals = [prm[n] for n in pnames]

    def body(*refs):
        it = iter(refs)
        xin_ref, win_ref, wout_ref, lng_ref, lnb_ref, khT_ref, vh_ref = (next(it) for _ in range(7))
        p = {n: next(it) for n in pnames}
        tgt_ref = next(it) if last else None
        xout_ref, proj_ref, z_ref = next(it), next(it), next(it)
        loss_ref = next(it) if last else None
        rest = list(it)
        i = pl.program_id(0)
        x = xin_ref[...]
        proj_ref[...] = _mm(x, win_ref[...])

        if kind == 0:
            gu, _, _, _, _, mixed = _gmlp_core(proj_ref[:, 0:TOK], proj_ref[:, TOK:2 * TOK], p["wtri"], p["bcolb"])
            tok = gu * mixed
        elif kind == 1:
            st_save, o_save, st_ref, states_s, u_s, qdec_s, kend_s, v_s, a_s, oint_s = rest

            @pl.when(i == 0)
            def _():
                st_ref[...] = jnp.zeros_like(st_ref)

            st_save[0, 0] = st_ref[...]
            v = proj_ref[:, 2 * TOK:3 * TOK]
            hp = _hgrn_parallel(proj_ref[:, 0:TOK], proj_ref[:, TOK:2 * TOK], p["lb"][...])
            qdec_s[...] = hp["q_dec"]
            kend_s[...] = hp["k_end"]
            v_s[...] = v
            a_s[...] = hp["a"]
            o_intra = _hgrn_intra(hp["q_dec"], hp["k_inv"], v)
            _hgrn_inter_fwd(qdec_s, kend_s, v_s, a_s, oint_s, st_ref, states_s, u_s)
            o = o_intra + oint_s[...]
            o_save[0] = o
            for sub in range(1, TS // HGRN_SUB):
                st_save[0, sub] = states_s[sub * HGRN_SUB // CHUNK]
            on, _ = _rms(o)
            tok = on * p["ng"][...]
        elif kind == 2:
            pc_save, pcar = rest

            @pl.when(i == 0)
            def _():
                pcar[...] = jnp.zeros_like(pcar)

            pc_save[0] = pcar[...]
            pp = proj_ref[:, 0:TOK]
            _, _, y = _pool_core(pp, pcar[...], i * TS, p["wbd"])
            pcar[...] = pp[TS - 2 * SUB:, :]
            tok = y * p["scale"][...]
        else:
            cc_save, hc_save, h_save, ccar, hcar = rest

            @pl.when(i == 0)
            def _():
                ccar[...] = jnp.zeros_like(ccar)
                hcar[...] = jnp.zeros_like(hcar)

            cc_save[0] = ccar[...]
            hc_save[0] = hcar[...]
            xb = proj_ref[:, 0:TOK]
            lc = _lru_core(xb, ccar[...], i * TS, p)
            P, B = _scan_fwd(lc["a"], lc["bt"])
            tok = P * hcar[SUB - 1:SUB, :] + B
            h_save[0] = tok
            ccar[...] = xb[TS - SUB:, :]
            hcar[...] = tok[TS - SUB:, :]

        xo, _ = _xattn_fwd(proj_ref[:, off["qx"]:off["qx"] + XW], khT_ref, vh_ref)
        gate = proj_ref[:, off["gate"]:off["gate"] + D_MODEL]
        mixed = jnp.concatenate([tok, xo], axis=-1) * (gate * _sig(gate))
        z = ALPHA * x + _mm(mixed, wout_ref[...])
        z_ref[...] = z
        xhat, _ = _ln(z)
        xout = xhat * lng_ref[...] + lnb_ref[...]
        if last:
            e = xout - tgt_ref[...]
            xout_ref[...] = e * (1.0 / D_MODEL)
            es = _rowsum(e * e)
            tot = es[:, 0:LANE]
            for j in range(1, D_MODEL // LANE):
                tot = tot + es[:, j * LANE:(j + 1) * LANE]

            @pl.when(i == 0)
            def _():
                loss_ref[...] = jnp.zeros_like(loss_ref)

            loss_ref[0:1, :] += tot
        else:
            xout_ref[...] = xout

    tile = lambda w: pl.BlockSpec((TS, w), lambda i: (i, 0))
    in_arrays = [xin, w_in, w_out, lng, lnb, khT, vh] + pvals + ([tgt] if last else [])
    in_specs = [tile(D_MODEL)] + [_const_spec(a) for a in in_arrays[1:7 + len(pvals)]] + ([tile(D_MODEL)] if last else [])
    out_shape = [jax.ShapeDtypeStruct((S, D_MODEL), f32), jax.ShapeDtypeStruct((S, W), f32), jax.ShapeDtypeStruct((S, D_MODEL), f32)]
    out_specs = [tile(D_MODEL), tile(W), tile(D_MODEL)]
    if last:
        out_shape.append(jax.ShapeDtypeStruct((SUB, LANE), f32))
        out_specs.append(_acc_spec((SUB, LANE)))
    scratch = []
    save = lambda *s: (jax.ShapeDtypeStruct((nt,) + s, f32), pl.BlockSpec((1,) + s, lambda i, _n=len(s): (i,) + (0,) * _n))
    if kind == 1:
        saved = [save(TS // HGRN_SUB, NH, HD, HD), save(TS, TOK)]
        scratch = ([pltpu.VMEM((NH, HD, HD), f32)] + [pltpu.VMEM((TS // CHUNK, NH, HD, HD), f32)] * 2
                   + [pltpu.VMEM((TS, TOK), f32)] * 5)
    elif kind == 2:
        saved = [save(2 * SUB, TOK)]
        scratch = [pltpu.VMEM((2 * SUB, TOK), f32)]
    elif kind == 3:
        saved = [save(SUB, TOK), save(SUB, TOK), save(TS, TOK)]
        scratch = [pltpu.VMEM((SUB, TOK), f32)] * 2
    else:
        saved = []
    for sh, sp in saved:
        out_shape.append(sh)
        out_specs.append(sp)
    return pl.pallas_call(body, name=f"fwd_layer{kind}", grid=(nt,), in_specs=in_specs, out_specs=out_specs,
                          out_shape=out_shape, scratch_shapes=scratch, compiler_params=_params())(*in_arrays)


def _small_grad_shapes(kind):
    if kind == 0:
        return dict(dwtri=(NH, HD, HD), dbacc=(NH, HD, HD))
    if kind == 1:
        return dict(dlb=(SUB, TOK), dng=(SUB, TOK))
    if kind == 2:
        return dict(dwbd=(TOK, TOK), dscale=(SUB, TOK))
    return dict(dcw=(SUB, TOK), dvec=(SUB, TOK), dwgx=(NH, HD, HD), dwga=(NH, HD, HD))


def _bwd_layer(kind, dxout, z, proj, w_inT, w_outT, lng, kh, khT, vh, vhT, prm, saves):
    S = dxout.shape[0]
    TS = _TS[kind]
    nt = S // TS
    off = _OFFS[kind]
    W = off["W"]
    pnames = _PRM[kind]
    pvals = [prm[n] for n in pnames]
    sg_shapes = _small_grad_shapes(kind)
    sg_names = list(sg_shapes)
    n_saves = len(saves)

    def body(*refs):
        it = iter(refs)
        dxo_ref, z_ref, proj_ref, winT_ref, woutT_ref, lng_ref, kh_ref, khT_ref, vh_ref, vhT_ref = (next(it) for _ in range(10))
        p = {n: next(it) for n in pnames}
        sv = [next(it) for _ in range(n_saves)]
        dxin_ref, dproj_ref, mixed_ref, dy_ref, dln_ref, dk_ref, dv_ref = (next(it) for _ in range(7))
        sg = {n: next(it) for n in sg_names}
        rest = list(it)
        step = pl.program_id(0)
        i = nt - 1 - step

        @pl.when(step == 0)
        def _():
            dln_ref[...] = jnp.zeros_like(dln_ref)
            dk_ref[...] = jnp.zeros_like(dk_ref)
            dv_ref[...] = jnp.zeros_like(dv_ref)
            for n in sg_names:
                sg[n][...] = jnp.zeros_like(sg[n])

        dxo = dxo_ref[...]
        xhat, rstd = _ln(z_ref[...])
        dln_ref[0:1, :] += _rowsum(dxo * xhat)
        dln_ref[1:2, :] += _rowsum(dxo)
        dz = _ln_bwd(dxo * lng_ref[...], xhat, rstd)
        dyb = dz.astype(bf16)
        dy_ref[...] = dyb
        dmixed = _mm(dyb, woutT_ref[...])

        aux = {}
        if kind == 0:
            u_raw, v_raw = proj_ref[:, 0:TOK], proj_ref[:, TOK:2 * TOK]
            gu, tu, tv, vns, rstds, mx = _gmlp_core(u_raw, v_raw, p["wtri"], p["bcolb"])
            tok = gu * mx
        elif kind == 1:
            st_save, o_save = sv
            (dst_ref, fst_ref, states_s, dsts_s, u_s, qdec_s, kend_s, v_s, a_s, do_s, dqdec_s, dkend_s, dv_s,
             dgl_s) = rest

            @pl.when(step == 0)
            def _():
                dst_ref[...] = jnp.zeros_like(dst_ref)

            o = o_save[0]
            on, rs = _rms(o)
            tok = on * p["ng"][...]
            aux = dict(o=o, on=on, rs=rs)
        elif kind == 2:
            pc_save, = sv
            dpcar, = rest
            pp = proj_ref[:, 0:TOK]
            diff, inv_cnt, y = _pool_core(pp, pc_save[0], i * TS, p["wbd"])
            tok = y * p["scale"][...]
        else:
            cc_save, hc_save, h_save = sv
            dccar, gcar = rest
            xb = proj_ref[:, 0:TOK]
            lc = _lru_core(xb, cc_save[0], i * TS, p)
            hin = hc_save[0, SUB - 1:SUB, :]
            tok = h_save[0]

        xo, ps = _xattn_fwd(proj_ref[:, off["qx"]:off["qx"] + XW], khT_ref, vh_ref)
        gate = proj_ref[:, off["gate"]:off["gate"] + D_MODEL]
        sgm = _sig(gate)
        sgate = gate * sgm
        cat = jnp.concatenate([tok, xo], axis=-1)
        mixed_ref[...] = (cat * sgate).astype(bf16)
        dcat = dmixed * sgate
        dproj_ref[:, off["gate"]:off["gate"] + D_MODEL] = (dmixed * cat * (sgm * (1.0 + gate * (1.0 - sgm)))).astype(bf16)
        dtok = dcat[:, 0:TOK]
        dxo_att = dcat[:, TOK:]

        qx = proj_ref[:, off["qx"]:off["qx"] + XW]
        dqx = jnp.zeros((TS, XW), f32)
        for h in range(XHEADS):
            dp = _mm(dxo_att, vhT_ref[h])
            ds = ps[h] * (dp - jnp.sum(dp * ps[h], axis=-1, keepdims=True)) * (XDIM ** -0.5)
            dqx = dqx + _mm(ds, kh_ref[h])
            dk_ref[h] += _mm_tn(ds, qx)
            dv_ref[h] += _mm_tn(ps[h], dxo_att)
        dproj_ref[:, off["qx"]:off["qx"] + XW] = dqx.astype(bf16)

        if kind == 0:
            tril = lax.broadcasted_iota(jnp.int32, (HD, HD), 1) <= lax.broadcasted_iota(jnp.int32, (HD, HD), 0)
            dgu = dtok * mx
            dmx = dtok * gu
            dgvs = []
            for h in range(NH):
                dmh = dmx[:, _hs(h)]
                blks = []
                for n in range(TS // HD):
                    rs_ = slice(n * HD, (n + 1) * HD)
                    blks.append(_mm(p["wtriT"][h], dmh[rs_]))
                    sg["dwtri"][h] += jnp.where(tril, _mm_nt(dmh[rs_], vns[h][rs_]), 0.0)
                    sg["dbacc"][h] += dmh[rs_]
                dgvs.append(_ln_bwd(jnp.concatenate(blks, axis=0), vns[h], rstds[h]))
            dgv = jnp.concatenate(dgvs, axis=-1)
            dproj_ref[:, 0:TOK] = (dgu * _gelu_grad(u_raw, tu)).astype(bf16)
            dproj_ref[:, TOK:2 * TOK] = (dgv * _gelu_grad(v_raw, tv)).astype(bf16)
        elif kind == 1:
            o, on, rs = aux["o"], aux["on"], aux["rs"]
            ng = p["ng"][...]
            lb = p["lb"][...]
            sg["dng"][0:1, :] += _rowsum(dtok * on)
            dn = dtok * ng
            dos = []
            for h in range(NH):
                oh, r = o[:, _hs(h)], rs[h]
                dos.append(r * (dn[:, _hs(h)] - oh * (r * r) * _lmean(dn[:, _hs(h)] * oh)))
            do_all = jnp.concatenate(dos, axis=-1)
            _, tri = _chunk_mats(HD)
            same, _ = _chunk_mats(HGRN_SUB)
            triT = jnp.logical_and(same, lax.broadcasted_iota(jnp.int32, (HGRN_SUB, HGRN_SUB), 1)
                                   >= lax.broadcasted_iota(jnp.int32, (HGRN_SUB, HGRN_SUB), 0))
            row16 = lax.broadcasted_iota(jnp.int32, (CHUNK, HD), 0)
            nch = HGRN_SUB // CHUNK
            for sub in reversed(range(TS // HGRN_SUB)):
                rr = slice(sub * HGRN_SUB, (sub + 1) * HGRN_SUB)
                q_raw, v = proj_ref[rr, 0:TOK], proj_ref[rr, 2 * TOK:3 * TOK]
                hp = _hgrn_parallel(q_raw, proj_ref[rr, TOK:2 * TOK], lb)
                qdec_s[...] = hp["q_dec"]
                kend_s[...] = hp["k_end"]
                v_s[...] = v
                a_s[...] = hp["a"]
                fst_ref[...] = st_save[0, sub]
                _hgrn_inter_fwd(qdec_s, kend_s, v_s, a_s, None, fst_ref, states_s, u_s)
                do = do_all[rr]
                do_s[...] = do
                dqd, dki, dvi = [], [], []
                for h in range(NH):
                    bq, bk, bv = [], [], []
                    for b in range(HGRN_SUB // HD):
                        rs_ = slice(b * HD, (b + 1) * HD)
                        qd, ki = hp["q_dec"][rs_, _hs(h)], hp["k_inv"][rs_, _hs(h)]
                        sc = jnp.where(tri, _mm_nt(qd, ki), 0.0)
                        dsc = jnp.where(tri, _mm_nt(do[rs_, _hs(h)], v[rs_, _hs(h)]), 0.0)
                        bv.append(_mm_tn(sc, do[rs_, _hs(h)]))
                        bq.append(_mm(dsc, ki))
                        bk.append(_mm_tn(dsc, qd))
                    dqd.append(jnp.concatenate(bq, axis=0))
                    dki.append(jnp.concatenate(bk, axis=0))
                    dvi.append(jnp.concatenate(bv, axis=0))
                dqdec_s[...] = jnp.concatenate(dqd, axis=-1)
                dk_inv = jnp.concatenate(dki, axis=-1)
                dv_s[...] = jnp.concatenate(dvi, axis=-1)
                for c in range(nch):
                    for h in range(NH):
                        u_s[c, h] = _mm_tn(do_s[_cs(c), _hs(h)], qdec_s[_cs(c), _hs(h)])
                for h in range(NH):
                    dst = dst_ref[h]
                    for c in reversed(range(nch)):
                        dsts_s[c, h] = dst
                        dst = dst * a_s[c * CHUNK:c * CHUNK + 1, _hs(h)] + u_s[c, h]
                    dst_ref[h] = dst
                for c in range(nch):
                    for h in range(NH):
                        stp = states_s[c, h]
                        dst = dsts_s[c, h]
                        dqdec_s[_cs(c), _hs(h)] += _mm(do_s[_cs(c), _hs(h)], stp)
                        dkend_s[_cs(c), _hs(h)] = _mm(v_s[_cs(c), _hs(h)], dst)
                        dv_s[_cs(c), _hs(h)] += _mm_nt(kend_s[_cs(c), _hs(h)], dst)
                        da = jnp.sum(dst * stp, axis=0, keepdims=True) * a_s[c * CHUNK:c * CHUNK + 1, _hs(h)]
                        dgl_s[_cs(c), _hs(h)] = jnp.where(row16 == 0, jnp.broadcast_to(da, (CHUNK, HD)), 0.0)
                dq_dec = dqdec_s[...]
                dk_end = dkend_s[...]
                dg = dq_dec * hp["q_dec"] - dk_inv * hp["k_inv"] - dk_end * hp["k_end"]
                dk = dk_inv * hp["eng"] + dk_end * hp["ee"]
                dglr = dk_end * hp["k_end"] + dgl_s[...]
                dlogf = _mm_sel(triT, dg) + _mm_sel(same, dglr)
                df = dlogf / hp["f"] - dk
                sg["dlb"][0:1, :] += _rowsum(df * (1.0 - hp["sgm"]))
                dproj_ref[rr, 0:TOK] = (dq_dec * hp["eg"] * (hp["sq"] * (1.0 + q_raw * (1.0 - hp["sq"])))).astype(bf16)
                dproj_ref[rr, TOK:2 * TOK] = (df * (1.0 - lb) * hp["sgm"] * (1.0 - hp["sgm"])).astype(bf16)
                dproj_ref[rr, 2 * TOK:3 * TOK] = dv_s[...].astype(bf16)
        elif kind == 2:
            @pl.when(step == 0)
            def _():
                dpcar[...] = jnp.zeros_like(dpcar)

            sg["dscale"][0:1, :] += _rowsum(dtok * y)
            dyp = dtok * p["scale"][...]
            sg["dwbd"][...] += _mm_tn(diff, dyp)
            ddiff = _mm(dyp, p["wbdT"][...])
            q = ddiff * inv_cnt
            ext = jnp.concatenate([q, dpcar[...]], axis=0)
            n = TS + 2 * SUB
            r1 = ext + pltpu.roll(ext, n - 1, 0)
            r2 = r1 + pltpu.roll(r1, n - 2, 0)
            r3 = r2 + pltpu.roll(r2, n - 4, 0)
            r4 = r3 + pltpu.roll(r3, n - 8, 0)
            dproj_ref[:, 0:TOK] = (_pool_pick(r1, r2, r3, r4)[:TS] - ddiff).astype(bf16)
            dpcar[...] = q[0:2 * SUB, :]
        else:
            @pl.when(step == 0)
            def _():
                dccar[...] = jnp.zeros_like(dccar)
                gcar[...] = jnp.zeros_like(gcar)

            a, mult, gx, ga, xc = lc["a"], lc["mult"], lc["gx"], lc["ga"], lc["xc"]
            row = lax.broadcasted_iota(jnp.int32, (TS, TOK), 0)
            an = jnp.where(row == TS - 1, 1.0, pltpu.roll(a, TS - 1, 0))
            Pb, Bb = _scan_bwd(an, dtok)
            lam = Pb * gcar[0:1, :] + Bb
            gcar[...] = (a * lam)[0:SUB, :]
            hprev = jnp.where(row == 0, jnp.broadcast_to(hin, (TS, TOK)), pltpu.roll(tok, 1, 0))
            dmult = lam * gx * xc
            dgx = lam * mult * xc
            dxc = lam * mult * gx
            dla = lam * hprev * a - jnp.where(lc["first"], 0.0, dmult * a * a / mult)
            sp = lc["sp"]
            dga = -LRU_C * sp * dla
            dsp = _rowsum(-LRU_C * ga * dla)
            sg["dvec"][0:1, :] += dsp * (-_sig(-p["ap"][...]))
            dpx = dgx * gx * (1.0 - gx)
            dpa = dga * ga * (1.0 - ga)
            sg["dvec"][1:2, :] += _rowsum(dpx)
            sg["dvec"][2:3, :] += _rowsum(dpa)
            dxcs = []
            for h in range(NH):
                dxcs.append(_mm(dpx[:, _hs(h)], p["wgxT"][h]) + _mm(dpa[:, _hs(h)], p["wgaT"][h]))
                sg["dwgx"][h] += _mm_tn(xc[:, _hs(h)], dpx[:, _hs(h)])
                sg["dwga"][h] += _mm_tn(xc[:, _hs(h)], dpa[:, _hs(h)])
            dxc = dxc + jnp.concatenate(dxcs, axis=-1)
            sg["dvec"][3:4, :] += _rowsum(dxc)
            sg["dcw"][3:4, :] += _rowsum(dxc * xb)
            sg["dcw"][2:3, :] += _rowsum(dxc * lc["x1"])
            sg["dcw"][1:2, :] += _rowsum(dxc * lc["x2"])
            sg["dcw"][0:1, :] += _rowsum(dxc * lc["x3"])
            ext = jnp.concatenate([dxc, dccar[...]], axis=0)
            n = TS + SUB
            cw = p["cw"]
            dproj_ref[:, 0:TOK] = (cw[3:4, :] * dxc + cw[2:3, :] * pltpu.roll(ext, n - 1, 0)[:TS]
                                   + cw[1:2, :] * pltpu.roll(ext, n - 2, 0)[:TS]
                                   + cw[0:1, :] * pltpu.roll(ext, n - 3, 0)[:TS]).astype(bf16)
            dccar[...] = dxc[0:SUB, :]

        dxin_ref[...] = ALPHA * dz + _mm(dproj_ref[...], winT_ref[...])

    rtile = lambda w: pl.BlockSpec((TS, w), lambda s: (nt - 1 - s, 0))
    consts = [w_inT, w_outT, lng, kh, khT, vh, vhT] + pvals
    in_arrays = [dxout, z, proj] + consts + list(saves)
    in_specs = [rtile(D_MODEL), rtile(D_MODEL), rtile(W)] + [_const_spec(a) for a in consts]
    for a in saves:
        in_specs.append(pl.BlockSpec((1,) + a.shape[1:], lambda s, _n=a.ndim - 1: (nt - 1 - s,) + (0,) * _n))
    out_shape = [jax.ShapeDtypeStruct((S, D_MODEL), f32), jax.ShapeDtypeStruct((S, W), bf16),
                 jax.ShapeDtypeStruct((S, D_MODEL), bf16), jax.ShapeDtypeStruct((S, D_MODEL), bf16),
                 jax.ShapeDtypeStruct((SUB, D_MODEL), f32), jax.ShapeDtypeStruct((XHEADS, XW, XW), f32),
                 jax.ShapeDtypeStruct((XHEADS, XW, XW), f32)]
    out_specs = [rtile(D_MODEL), rtile(W), rtile(D_MODEL), rtile(D_MODEL), _acc_spec((SUB, D_MODEL)),
                 _acc_spec((XHEADS, XW, XW)), _acc_spec((XHEADS, XW, XW))]
    for n in sg_names:
        out_shape.append(jax.ShapeDtypeStruct(sg_shapes[n], f32))
        out_specs.append(_acc_spec(sg_shapes[n]))
    if kind == 1:
        scratch = ([pltpu.VMEM((NH, HD, HD), f32)] * 2 + [pltpu.VMEM((HGRN_SUB // CHUNK, NH, HD, HD), f32)] * 3
                   + [pltpu.VMEM((HGRN_SUB, TOK), f32)] * 9)
    elif kind == 2:
        scratch = [pltpu.VMEM((2 * SUB, TOK), f32)]
    elif kind == 3:
        scratch = [pltpu.VMEM((SUB, TOK), f32)] * 2
    else:
        scratch = []
    outs = pl.pallas_call(body, name=f"bwd_layer{kind}", grid=(nt,), in_specs=in_specs, out_specs=out_specs,
                          out_shape=out_shape, scratch_shapes=scratch, compiler_params=_params())(*in_arrays)
    return outs[:7], dict(zip(sg_names, outs[7:]))


def _prep(mem, w_kv, logits):
    def body(mem_ref, w_ref, lg_ref, kh_ref, khT_ref, vh_ref, vhT_ref, p_ref):
        kv = _mm(mem_ref[...], w_ref[...])
        k, v = kv[:, 0:XW], kv[:, XW:]
        kT, vT = k.T, v.T
        col = lax.broadcasted_iota(jnp.int32, (XW, XW), 1) // XDIM
        row = lax.broadcasted_iota(jnp.int32, (XW, XW), 0) // XDIM
        for h in range(XHEADS):
            kh_ref[h] = jnp.where(col == h, k, 0.0).astype(bf16)
            vh_ref[h] = jnp.where(col == h, v, 0.0).astype(bf16)
            khT_ref[h] = jnp.where(row == h, kT, 0.0).astype(bf16)
            vhT_ref[h] = jnp.where(row == h, vT, 0.0).astype(bf16)
        lg = lg_ref[...]
        e = jnp.exp(lg - jnp.max(lg, axis=0, keepdims=True))
        p_ref[...] = e / jnp.sum(e, axis=0, keepdims=True)

    vm = pl.BlockSpec(memory_space=pltpu.VMEM)
    hs = jax.ShapeDtypeStruct((XHEADS, XW, XW), bf16)
    return pl.pallas_call(body, name="prep_memory", in_specs=[vm] * 3, out_specs=[vm] * 5,
                          out_shape=[hs, hs, hs, hs, jax.ShapeDtypeStruct(logits.shape, f32)])(mem, w_kv, logits)


def _kv_bwd(mem, dks, dvs):
    def body(mem_ref, *refs):
        out_ref = refs[-1]
        col = lax.broadcasted_iota(jnp.int32, (XW, XW), 1) // XDIM
        dk = jnp.zeros((XW, XW), f32)
        dv = jnp.zeros((XW, XW), f32)
        for l in range(DEPTH):
            for h in range(XHEADS):
                dk = dk + jnp.where(col == h, refs[l][h], 0.0)
                dv = dv + jnp.where(col == h, refs[DEPTH + l][h], 0.0)
        out_ref[:, 0:XW] = _mm_tn(mem_ref[...], dk)
        out_ref[:, XW:] = _mm_tn(mem_ref[...], dv)

    vm = pl.BlockSpec(memory_space=pltpu.VMEM)
    return pl.pallas_call(body, name="kv_bwd", in_specs=[vm] * (1 + 2 * DEPTH), out_specs=vm,
                          out_shape=jax.ShapeDtypeStruct((D_MODEL, 2 * XW), f32))(mem, *dks, *dvs)


def _tn_gemm(a, b, name, nb):
    S, M = a.shape
    N = b.shape[1]
    NB = N // nb
    nk = S // TK

    def body(a_ref, b_ref, o_ref):
        @pl.when(pl.program_id(1) == 0)
        def _():
            o_ref[...] = jnp.zeros_like(o_ref)

        o_ref[...] += _mm_tn(a_ref[...], b_ref[...])

    return pl.pallas_call(body, name=name, grid=(nb, nk),
                          in_specs=[pl.BlockSpec((TK, M), lambda j, k: (k, 0)), pl.BlockSpec((TK, NB), lambda j, k: (k, j))],
                          out_specs=pl.BlockSpec((M, NB), lambda j, k: (0, j)),
                          out_shape=jax.ShapeDtypeStruct((M, N), f32),
                          compiler_params=pltpu.CompilerParams(dimension_semantics=("parallel", "arbitrary"),
                                                               vmem_limit_bytes=VMEM_LIMIT))(a, b)


def _rows_block(R, mult=16, cap=1024):
    best = R
    for d in range(mult, min(R, cap) + 1, mult):
        if R % d == 0:
            best = d
    return best


def _tn_gemm_sharded(a, b, name):
    S, M = a.shape
    Wq = b.shape[1] // 4
    nk = S // TK

    def body(a_ref, b_ref, o_ref):
        @pl.when(pl.program_id(0) == 0)
        def _():
            o_ref[...] = jnp.zeros_like(o_ref)

        at = a_ref[...].astype(MM)
        for j in range(4):
            o_ref[j] += _mm_tn(at, b_ref[:, j * Wq:(j + 1) * Wq])

    return pl.pallas_call(body, name=name, grid=(nk,),
                          in_specs=[pl.BlockSpec((TK, M), lambda k: (k, 0)), pl.BlockSpec((TK, 4 * Wq), lambda k: (k, 0))],
                          out_specs=pl.BlockSpec((4, M, Wq), lambda k: (0, 0, 0)),
                          out_shape=jax.ShapeDtypeStruct((4, M, Wq), f32), compiler_params=_params())(a, b)


HALF_ROWS = D_MODEL // 2
SHARD_ROWS = D_MODEL // 4


def _half_of_full(ref, kind, h):
    if kind == "rows":
        cols = ref.shape[1] // 2
        return ref.at[:, pl.ds(h * cols, cols)]
    return ref.at[:, pl.ds(h * HALF_ROWS, HALF_ROWS)]


def _shard_of_half(ref, kind, j):
    if kind == "rows":
        return ref.at[pl.ds(j * SHARD_ROWS, SHARD_ROWS)]
    return ref.at[j]


def _half_of_shard(ref, kind, h):
    if kind == "rows":
        cols = ref.shape[1] // 2
        return ref.at[:, pl.ds(h * cols, cols)]
    rows = ref.shape[0] // 2
    return ref.at[pl.ds(h * rows, rows)]


def _half_shape(full_shape, kind):
    if kind == "rows":
        return (full_shape[0], full_shape[1] // 2)
    return (4, HALF_ROWS, full_shape[2])


def _shard_half_shape(full_shape, kind):
    if kind == "rows":
        return (SHARD_ROWS, full_shape[1] // 2)
    return (HALF_ROWS, full_shape[2])


def _shard_shape(full_shape, kind):
    if kind == "rows":
        return (SHARD_ROWS, full_shape[1])
    return (D_MODEL, full_shape[2])


def _ew_call(body, name, grid, jc, ins, in_specs, out_shape, out_specs):
    gs = pltpu.PrefetchScalarGridSpec(num_scalar_prefetch=1, grid=grid, in_specs=in_specs, out_specs=out_specs)
    return pl.pallas_call(body, name=name, grid_spec=gs, out_shape=out_shape,
                          compiler_params=pltpu.CompilerParams(dimension_semantics=("parallel",) * len(grid),
                                                               vmem_limit_bytes=VMEM_LIMIT))(jc, *ins)


def _add_sibling(part, got, kind, jc, name):
    def body(jc_ref, a_ref, b_ref, o_ref, ob_ref):
        s = a_ref[...] + b_ref[...]
        o_ref[...] = s
        ob_ref[...] = s.astype(bf16)

    if kind == "rows":
        R, C = part.shape[0], part.shape[1] // 2
        grid = (2,)
        mine = pl.BlockSpec((R // 2, C), lambda i, jc_ref: (i, jc_ref[1]))
        spec = pl.BlockSpec((R // 2, C), lambda i, jc_ref: (i, 0))
    else:
        C = part.shape[2]
        grid = (4, 2)
        mine = pl.BlockSpec((None, HALF_ROWS // 2, C), lambda s, i, jc_ref: (s, 2 * jc_ref[1] + i, 0))
        spec = pl.BlockSpec((None, HALF_ROWS // 2, C), lambda s, i, jc_ref: (s, i, 0))
    hs = _half_shape(part.shape, kind)
    return _ew_call(body, name, grid, jc, [part, got], [mine, spec],
                    [jax.ShapeDtypeStruct(hs, f32), jax.ShapeDtypeStruct(hs, bf16)], [spec, spec])


def _add_chips(q32, r, kind, jc, name):
    def body(jc_ref, q_ref, r_ref, out_ref):
        out_ref[...] = ((q_ref[...] + r_ref[0].astype(f32)) + r_ref[1].astype(f32)) + r_ref[2].astype(f32)

    if kind == "rows":
        C = q32.shape[1]
        grid = (1,)
        qs = pl.BlockSpec((SHARD_ROWS, C), lambda i, jc_ref: (jc_ref[0], 0))
        rs = pl.BlockSpec((3, SHARD_ROWS, C), lambda i, jc_ref: (0, 0, 0))
        os_ = pl.BlockSpec((SHARD_ROWS, C), lambda i, jc_ref: (0, jc_ref[1]))
        full_shape = (D_MODEL, 2 * C)
    else:
        C = q32.shape[2]
        grid = (2,)
        qs = pl.BlockSpec((None, HALF_ROWS // 2, C), lambda i, jc_ref: (jc_ref[0], i, 0))
        rs = pl.BlockSpec((3, HALF_ROWS // 2, C), lambda i, jc_ref: (0, i, 0))
        os_ = pl.BlockSpec((HALF_ROWS // 2, C), lambda i, jc_ref: (2 * jc_ref[1] + i, 0))
        full_shape = (4, D_MODEL, C)
    return _ew_call(body, name, grid, jc, [q32, r], [qs, rs], jax.ShapeDtypeStruct(_shard_shape(full_shape, kind), f32), os_)


def _adamw(w, g, m, v, name):
    R, C = w.shape
    br = _rows_block(R, mult=SUB, cap=512)
    c1 =1.0 / (1.0 - ADAM_B1 ** ADAM_STEP)
    c2 = 1.0 / (1.0 - ADAM_B2 ** ADAM_STEP)

    def body(w_ref, g_ref, m_ref, v_ref, d_ref, nm_ref, nv_ref):
        g_ = g_ref[...]
        nm = ADAM_B1 * m_ref[...] + (1.0 - ADAM_B1) * g_
        nv = ADAM_B2 * v_ref[...] + (1.0 - ADAM_B2) * (g_ * g_)
        nm_ref[...] = nm
        nv_ref[...] = nv
        d_ref[...] = -ADAM_LR * ((nm * c1) / (jnp.sqrt(nv * c2) + ADAM_EPS) + ADAM_WD * w_ref[...])

    spec = pl.BlockSpec((br, C), lambda i: (i, 0))
    sh = jax.ShapeDtypeStruct((R, C), f32)
    return pl.pallas_call(body, name=name, grid=(R // br,), in_specs=[spec] * 4, out_specs=[spec] * 3,
                          out_shape=[sh, sh, sh], compiler_params=_params("parallel"))(w, g, m, v)


def _small_finish(dbacc, p_soft, dlb):
    def body(db_ref, p_ref, dlb_ref, dbs_ref, dlg_ref):
        lane = lax.broadcasted_iota(jnp.int32, (HD, HD), 1)
        acc = jnp.zeros((HD, HD), f32)
        for h in range(NH):
            acc = acc + jnp.where(lane == h, jnp.sum(db_ref[h], axis=-1, keepdims=True), 0.0)
        dbs_ref[...] = acc
        p = p_ref[...]
        p1 = p[1:2, :]
        rowi = lax.broadcasted_iota(jnp.int32, p.shape, 0)
        dlg_ref[...] = dlb_ref[0:1, :] * p1 * (jnp.where(rowi == 1, 1.0, 0.0) - p)

    vm = pl.BlockSpec(memory_space=pltpu.VMEM)
    return pl.pallas_call(body, name="small_finish", in_specs=[vm] * 3, out_specs=[vm] * 2,
                          out_shape=[jax.ShapeDtypeStruct((HD, HD), f32), jax.ShapeDtypeStruct(p_soft.shape, f32)])(dbacc, p_soft, dlb)


def _where_am_i():
    return lax.axis_index("x"), lax.axis_index("y"), lax.axis_index("c")


MAX_PIECES = 8


def _nchunks(rows, mult):
    for n in range(MAX_PIECES, 0, -1):
        if rows % (n * mult) == 0:
            return n
    return 1


def _leading_pieces(src, dst):
    n = src.shape[0]
    if len(src.shape) >= 3 and n <= MAX_PIECES:
        return [(src.at[s], dst.at[s]) for s in range(n)]
    return [(src, dst)]


def _ag_weights(shards, kinds, jshard):
    n = len(shards)

    def body(*refs):
        sh_refs, out_refs = refs[:n], refs[2 * n:3 * n]
        send_sems, recv_sems = refs[3 * n:]
        x, y, c = _where_am_i()
        j = 2 * x + y
        sib = (x, y, 1 - c)
        chips = [(1 - x, y), (x, 1 - y), (1 - x, 1 - y)]

        def cp(k, src, dst, to):
            return pltpu.make_async_remote_copy(src_ref=src, dst_ref=dst, send_sem=send_sems.at[k], recv_sem=recv_sems.at[k],
                                                device_id=to, device_id_type=MESH)

        started = []
        for a in range(n):
            for k, (cx, cy) in enumerate(chips):
                d = cp(6 * a + k, _half_of_shard(sh_refs[a], kinds[a], c), _half_of_shard(out_refs[a].at[j], kinds[a], c), (cx, cy, c))
                d.start()
                started.append(d)
        for a in range(n):
            for k, (cx, cy) in enumerate(chips):
                blk = _half_of_shard(out_refs[a].at[2 * cx + cy], kinds[a], c)
                cp(6 * a + k, blk, blk, (cx, cy, c)).wait_recv()
                d = cp(6 * a + 3 + k, blk, blk, sib)
                d.start()
                started.append(d)
        for a in range(n):
            for k, (cx, cy) in enumerate(chips):
                blk = _half_of_shard(out_refs[a].at[2 * cx + cy], kinds[a], 1 - c)
                cp(6 * a + 3 + k, blk, blk, sib).wait_recv()
        for d in started:
            d.wait_send()

    placed = [lax.dynamic_update_slice(jnp.zeros((4,) + s.shape, s.dtype), s[None], (jshard,) + (0,) * s.ndim) for s in shards]
    anyspec = pl.BlockSpec(memory_space=pl.ANY)
    return pl.pallas_call(body, name="all_gather_weights", in_specs=[anyspec] * (2 * n), out_specs=[anyspec] * n,
                          out_shape=[jax.ShapeDtypeStruct(p.shape, p.dtype) for p in placed],
                          input_output_aliases={n + a: a for a in range(n)},
                          scratch_shapes=[pltpu.SemaphoreType.DMA((6 * n,)), pltpu.SemaphoreType.DMA((6 * n,))],
                          compiler_params=pltpu.CompilerParams(has_side_effects=True))(*shards, *placed)


_HBM = pl.BlockSpec(memory_space=pltpu.HBM)
_SEM = pl.BlockSpec(memory_space=pltpu.SEMAPHORE)
_FLOWING = pltpu.SideEffectType.DATAFLOW_SIDE_EFFECTING


def _peers6(x, y, c):
    chips = [(1 - x, y), (x, 1 - y), (1 - x, 1 - y)]
    return [(2 * k + e, chip, c if e == 0 else 1 - c) for k, chip in enumerate(chips) for e in range(2)]


def _ag_start(shards, jshard, name, after=None):
    n = len(shards)

    def body(*refs):
        out_refs = refs[2 * n:4 * n]
        send_sems, recv_sems, token = refs[4 * n:]
        x, y, c = _where_am_i()
        j = 2 * x + y
        for a in range(n):
            for slot, (cx, cy), tc in _peers6(x, y, c):
                pltpu.make_async_remote_copy(src_ref=_half_of_shard(out_refs[a], "win", c),
                                             dst_ref=_half_of_shard(out_refs[n + a].at[j], "win", c),
                                             send_sem=send_sems.at[6 * a + slot], recv_sem=recv_sems.at[6 * a + slot],
                                             device_id=(cx, cy, tc), device_id_type=MESH).start()
        token[...] = jnp.zeros_like(token)

    fill = jnp.zeros((), f32) if after is None else after[0, 0]
    placed = [lax.dynamic_update_slice(jnp.broadcast_to(fill.astype(s.dtype), (4,) + s.shape), s[None], (jshard,) + (0,) * s.ndim)
              for s in shards]
    hbm = lambda t: pltpu.with_memory_space_constraint(t, pltpu.HBM)
    both = list(shards) + placed
    outs = pl.pallas_call(
        body, name=name, in_specs=[_HBM] * (2 * n), out_specs=[_HBM] * (2 * n) + [_SEM, _SEM, pl.BlockSpec(memory_space=pltpu.VMEM)],
        out_shape=[pltpu.HBM(p.shape, p.dtype) for p in both] + [pltpu.SemaphoreType.DMA((6 * n,)), pltpu.SemaphoreType.DMA((6 * n,)),
                                                                jax.ShapeDtypeStruct((SUB, LANE), f32)],
        input_output_aliases={a: a for a in range(2 * n)},
        compiler_params=pltpu.CompilerParams(has_side_effects=_FLOWING))(*[hbm(t) for t in both])
    return outs[:2 * n], outs[2 * n], outs[2 * n + 1], outs[2 * n + 2]


def _ag_wait(bufs, send_sems, recv_sems, after, name):
    n = len(bufs) // 2

    def body(*refs):
        sh_refs, g_refs = refs[:n], refs[n:2 * n]
        send_sems, recv_sems = refs[2 * n], refs[2 * n + 1]
        x, y, c = _where_am_i()
        for a in range(n):
            for slot, (cx, cy), tc in _peers6(x, y, c):
                cp = pltpu.make_async_remote_copy(src_ref=_half_of_shard(sh_refs[a], "win", c),
                                                  dst_ref=_half_of_shard(g_refs[a].at[2 * cx + cy], "win", tc),
                                                  send_sem=send_sems.at[6 * a + slot], recv_sem=recv_sems.at[6 * a + slot],
                                                  device_id=(cx, cy, tc), device_id_type=MESH)
                cp.wait_send()
                cp.wait_recv()

    outs = pl.pallas_call(body, name=name, in_specs=[_HBM] * (2 * n) + [_SEM, _SEM, pl.BlockSpec(memory_space=pl.ANY)],
                          out_specs=[_HBM] * (2 * n), out_shape=[pltpu.HBM(b.shape, b.dtype) for b in bufs],
                          input_output_aliases={a: a for a in range(2 * n)},
                          compiler_params=pltpu.CompilerParams(has_side_effects=_FLOWING))(*bufs, send_sems, recv_sems, after)
    return outs[n:]


def _rs_swap(parts, kinds, name):
    n = len(parts)

    def body(*refs):
        p_refs, got_refs = refs[:n], refs[n:2 * n]
        send_sems, recv_sems = refs[2 * n:]
        x, y, c = _where_am_i()

        def cp(a, src, dst):
            return pltpu.make_async_remote_copy(src_ref=src, dst_ref=dst, send_sem=send_sems.at[a], recv_sem=recv_sems.at[a],
                                                device_id=(x, y, 1 - c), device_id_type=MESH)

        for a in range(n):
            for src, dst in _leading_pieces(_half_of_full(p_refs[a], kinds[a], 1 - c), got_refs[a]):
                cp(a, src, dst).start()
        for a in range(n):
            cp(a, got_refs[a], got_refs[a]).wait()

    anyspec = pl.BlockSpec(memory_space=pl.ANY)
    return pl.pallas_call(body, name=name, in_specs=[anyspec] * n, out_specs=[anyspec] * n,
                          out_shape=[jax.ShapeDtypeStruct(_half_shape(p.shape, k), p.dtype) for p, k in zip(parts, kinds)],
                          scratch_shapes=[pltpu.SemaphoreType.DMA((n,)), pltpu.SemaphoreType.DMA((n,))],
                          compiler_params=pltpu.CompilerParams(has_side_effects=True))(*parts)


def _rs_owners(qbs, kinds, full_shapes):
    n = len(qbs)

    def body(*refs):
        q_refs, got_refs = refs[:n], refs[n:2 * n]
        send_sems, recv_sems = refs[2 * n:]
        x, y, c = _where_am_i()
        chips = [(1 - x, y), (x, 1 - y), (1 - x, 1 - y)]
        ds = []
        for a in range(n):
            for k, (cx, cy) in enumerate(chips):
                d = pltpu.make_async_remote_copy(src_ref=_shard_of_half(q_refs[a], kinds[a], 2 * cx + cy), dst_ref=got_refs[a].at[k],
                                                 send_sem=send_sems.at[3 * a + k], recv_sem=recv_sems.at[3 * a + k],
                                                 device_id=(cx, cy, c), device_id_type=MESH)
                d.start()
                ds.append(d)
        for d in ds:
            d.wait()

    anyspec = pl.BlockSpec(memory_space=pl.ANY)
    return pl.pallas_call(body, name="rs_to_owners", in_specs=[anyspec] * n, out_specs=[anyspec] * n,
                          out_shape=[jax.ShapeDtypeStruct((3,) + _shard_half_shape(fs, k), bf16) for fs, k in zip(full_shapes, kinds)],
                          scratch_shapes=[pltpu.SemaphoreType.DMA((3 * n,)), pltpu.SemaphoreType.DMA((3 * n,))],
                          compiler_params=pltpu.CompilerParams(has_side_effects=True))(*qbs)


def _rs_owners_start(qbs, kinds, full_shapes, name):
    n = len(qbs)

    def body(*refs):
        q_refs, got_refs = refs[2 * n:3 * n], refs[3 * n:4 * n]
        send_sems, recv_sems, token = refs[4 * n:]
        x, y, c = _where_am_i()
        for a in range(n):
            for k, (cx, cy) in enumerate([(1 - x, y), (x, 1 - y), (1 - x, 1 - y)]):
                pltpu.make_async_remote_copy(src_ref=_shard_of_half(q_refs[a], kinds[a], 2 * cx + cy), dst_ref=got_refs[a].at[k],
                                             send_sem=send_sems.at[3 * a + k], recv_sem=recv_sems.at[3 * a + k],
                                             device_id=(cx, cy, c), device_id_type=MESH).start()
        token[...] = jnp.zeros_like(token)

    hbm = lambda t: pltpu.with_memory_space_constraint(t, pltpu.HBM)
    lands = [lax.empty((3,) + _shard_half_shape(fs, k), bf16) for fs, k in zip(full_shapes, kinds)]
    both = list(qbs) + lands
    outs = pl.pallas_call(
        body, name=name, in_specs=[_HBM] * (2 * n), out_specs=[_HBM] * (2 * n) + [_SEM, _SEM, pl.BlockSpec(memory_space=pltpu.VMEM)],
        out_shape=[pltpu.HBM(t.shape, t.dtype) for t in both] + [pltpu.SemaphoreType.DMA((3 * n,)), pltpu.SemaphoreType.DMA((3 * n,)),
                                                                jax.ShapeDtypeStruct((SUB, LANE), f32)],
        input_output_aliases={a: a for a in range(2 * n)},
        compiler_params=pltpu.CompilerParams(has_side_effects=_FLOWING))(*[hbm(t) for t in both])
    return outs[:2 * n], outs[2 * n], outs[2 * n + 1], outs[2 * n + 2]


def _rs_owners_wait(bufs, send_sems, recv_sems, kinds, after, name):
    n = len(bufs) // 2

    def body(*refs):
        q_refs, got_refs = refs[:n], refs[n:2 * n]
        send_sems, recv_sems = refs[2 * n], refs[2 * n + 1]
        x, y, c = _where_am_i()
        for a in range(n):
            for k, (cx, cy) in enumerate([(1 - x, y), (x, 1 - y), (1 - x, 1 - y)]):
                cp = pltpu.make_async_remote_copy(src_ref=_shard_of_half(q_refs[a], kinds[a], 2 * cx + cy), dst_ref=got_refs[a].at[k],
                                                  send_sem=send_sems.at[3 * a + k], recv_sem=recv_sems.at[3 * a + k],
                                                  device_id=(cx, cy, c), device_id_type=MESH)
                cp.wait_send()
                cp.wait_recv()

    outs = pl.pallas_call(body, name=name, in_specs=[_HBM] * (2 * n) + [_SEM, _SEM, pl.BlockSpec(memory_space=pl.ANY)],
                          out_specs=[_HBM] * (2 * n), out_shape=[pltpu.HBM(b.shape, b.dtype) for b in bufs],
                          input_output_aliases={a: a for a in range(2 * n)},
                          compiler_params=pltpu.CompilerParams(has_side_effects=_FLOWING))(*bufs, send_sems, recv_sems, after)
    return outs[n:]


def _rs_join(bufs, kinds):
    n = len(bufs)

    def body(*refs):
        out_refs = refs[n:2 * n]
        send_sems, recv_sems = refs[2 * n:]
        x, y, c = _where_am_i()

        def cp(a, h):
            blk = _half_of_shard(out_refs[a], kinds[a], h)
            return pltpu.make_async_remote_copy(src_ref=blk, dst_ref=blk, send_sem=send_sems.at[a], recv_sem=recv_sems.at[a],
                                                device_id=(x, y, 1 - c), device_id_type=MESH)

        for a in range(n):
            cp(a, c).start()
        for a in range(n):
            cp(a, c).wait_send()
            cp(a, 1 - c).wait_recv()

    anyspec = pl.BlockSpec(memory_space=pl.ANY)
    return pl.pallas_call(body, name="rs_join_halves", in_specs=[anyspec] * n, out_specs=[anyspec] * n,
                          out_shape=[jax.ShapeDtypeStruct(b.shape, b.dtype) for b in bufs],
                          input_output_aliases={a: a for a in range(n)},
                          scratch_shapes=[pltpu.SemaphoreType.DMA((n,)), pltpu.SemaphoreType.DMA((n,))],
                          compiler_params=pltpu.CompilerParams(has_side_effects=True))(*bufs)


def _all_reduce_small(g):
    R, C = g.shape
    H = R // 2
    NP = _nchunks(H, SUB)
    PR = H // NP

    def body(g_ref, out_ref, sib_ref, chip_ref, send_sems, recv_sems):
        x, y, c = _where_am_i()
        j = 2 * x + y
        sib = (x, y, 1 - c)
        chips = [(1 - x, y), (x, 1 - y), (1 - x, 1 - y)]
        rows = pl.ds(pl.multiple_of(c * H, SUB), H)

        def cp(k, src, dst, to):
            return pltpu.make_async_remote_copy(src_ref=src, dst_ref=dst, send_sem=send_sems.at[k], recv_sem=recv_sems.at[k],
                                                device_id=to, device_id_type=MESH)

        def pieces(k, src, dst, to):
            for q in range(NP):
                cp(k, src.at[pl.ds(q * PR, PR)], dst.at[pl.ds(q * PR, PR)], to).start()

        for half in range(2):
            pieces(0, g_ref.at[pl.ds(half * H, H)], sib_ref.at[pl.ds(half * H, H)], sib)
        cp(0, g_ref, sib_ref, sib).wait()
        chip_ref[j] = g_ref[rows, :] + sib_ref[rows, :]
        for k, (cx, cy) in enumerate(chips):
            pieces(1 + k, chip_ref.at[j], chip_ref.at[j], (cx, cy, c))
        for k, (cx, cy) in enumerate(chips):
            blk = chip_ref.at[2 * cx + cy]
            cp(1 + k, blk, blk, (cx, cy, c)).wait()
        out_ref[rows, :] = ((chip_ref[0] + chip_ref[1]) + chip_ref[2]) + chip_ref[3]
        other = out_ref.at[pl.ds(pl.multiple_of((1 - c) * H, SUB), H)]
        pieces(4, out_ref.at[rows], out_ref.at[rows], sib)
        cp(4, other, other, sib).wait()

    vm = pl.BlockSpec(memory_space=pltpu.VMEM)
    return pl.pallas_call(body, name="all_reduce_small", in_specs=[vm], out_specs=vm,
                          out_shape=jax.ShapeDtypeStruct((R, C), f32),
                          scratch_shapes=[pltpu.VMEM((R, C), f32), pltpu.VMEM((4, H, C), f32),
                                          pltpu.SemaphoreType.DMA((5,)), pltpu.SemaphoreType.DMA((5,))],
                          compiler_params=pltpu.CompilerParams(has_side_effects=True, vmem_limit_bytes=VMEM_LIMIT))(g)


SPLIT_MIN_ELEMS = 1 << 16


def _all_reduce_many(gs):
    n = len(gs)
    split = [g.ndim == 3 and g.shape[0] % 2 == 0 and g.size >= SPLIT_MIN_ELEMS for g in gs]
    part_shape = [((g.shape[0] // 2,) + g.shape[1:]) if s else g.shape for g, s in zip(gs, split)]
    n_split = sum(split)

    def body(*refs):
        g, out, sibs, chipb = refs[:n], refs[n:2 * n], refs[2 * n:3 * n], refs[3 * n:4 * n]
        send_sems, recv_sems = refs[4 * n:]
        x, y, c = _where_am_i()
        j = 2 * x + y
        sib = (x, y, 1 - c)
        chips = [(1 - x, y), (x, 1 - y), (1 - x, 1 - y)]

        def cp(k, src, dst, to):
            return pltpu.make_async_remote_copy(src_ref=src, dst_ref=dst, send_sem=send_sems.at[k], recv_sem=recv_sems.at[k],
                                                device_id=to, device_id_type=MESH)

        def part(a, h):
            return pl.ds(h * part_shape[a][0], part_shape[a][0]) if split[a] else Ellipsis

        def mine(ref, a, h):
            return ref.at[part(a, h)] if split[a] else ref

        swaps = [cp(a, g[a], sibs[a], sib) for a in range(n)]
        for d in swaps:
            d.start()
        for a in range(n):
            swaps[a].wait()
            chipb[a][j] = g[a][part(a, c)] + sibs[a][part(a, c)]
        sends = [cp(n + 3 * a + k, chipb[a].at[j], chipb[a].at[j], (cx, cy, c)) for a in range(n) for k, (cx, cy) in enumerate(chips)]
        for d in sends:
            d.start()
        for a in range(n):
            for k, (cx, cy) in enumerate(chips):
                blk = chipb[a].at[2 * cx + cy]
                cp(n + 3 * a + k, blk, blk, (cx, cy, c)).wait_recv()
            out[a][part(a, c)] = ((chipb[a][0] + chipb[a][1]) + chipb[a][2]) + chipb[a][3]
        for d in sends:
            d.wait_send()
        backs = [(a, cp(4 * n + i, mine(out[a], a, c), mine(out[a], a, c), sib)) for i, a in enumerate([a for a in range(n) if split[a]])]
        for _, d in backs:
            d.start()
        for i, (a, d) in enumerate(backs):
            d.wait_send()
            cp(4 * n + i, mine(out[a], a, 1 - c), mine(out[a], a, 1 - c), sib).wait_recv()

    vm = pl.BlockSpec(memory_space=pltpu.VMEM)
    nsem = 4 * n + n_split
    return pl.pallas_call(body, name="all_reduce_small_grads", in_specs=[vm] * n, out_specs=[vm] * n,
                          out_shape=[jax.ShapeDtypeStruct(g.shape, f32) for g in gs],
                          scratch_shapes=([pltpu.VMEM(g.shape, f32) for g in gs] + [pltpu.VMEM((4,) + ps, f32) for ps in part_shape]
                                          + [pltpu.SemaphoreType.DMA((nsem,)), pltpu.SemaphoreType.DMA((nsem,))]),
                          compiler_params=pltpu.CompilerParams(has_side_effects=True, vmem_limit_bytes=VMEM_LIMIT))(*gs)


def _adamw_many(ws, gs, ms, vs, name):
    n = len(ws)
    c1 = 1.0 / (1.0 - ADAM_B1 ** ADAM_STEP)
    c2 = 1.0 / (1.0 - ADAM_B2 ** ADAM_STEP)

    def body(*refs):
        for a in range(n):
            w_ref, g_ref, m_ref, v_ref, d_ref, nm_ref, nv_ref = (refs[i * n + a] for i in range(7))
            g_ = g_ref[...]
            nm = ADAM_B1 * m_ref[...] + (1.0 - ADAM_B1) * g_
            nv = ADAM_B2 * v_ref[...] + (1.0 - ADAM_B2) * (g_ * g_)
            nm_ref[...] = nm
            nv_ref[...] = nv
            d_ref[...] = -ADAM_LR * ((nm * c1) / (jnp.sqrt(nv * c2) + ADAM_EPS) + ADAM_WD * w_ref[...])

    vm = pl.BlockSpec(memory_space=pltpu.VMEM)
    sh = [jax.ShapeDtypeStruct(w.shape, f32) for w in ws]
    outs = pl.pallas_call(body, name=name, in_specs=[vm] * (4 * n), out_specs=[vm] * (3 * n), out_shape=sh * 3,
                          compiler_params=pltpu.CompilerParams(vmem_limit_bytes=VMEM_LIMIT))(*ws, *gs, *ms, *vs)
    return outs[:n], outs[n:2 * n], outs[2 * n:]


def _pack_flat(arrs, rows_mult):
    flat = jnp.concatenate([a.reshape(-1) for a in arrs])
    n = flat.shape[0]
    tot = -(-n // (rows_mult * LANE)) * rows_mult * LANE
    return jnp.pad(flat, (0, tot - n)).reshape(-1, LANE)


def _unpack_flat(buf, shapes):
    flat = buf.reshape(-1)
    out, o = [], 0
    for s in shapes:
        n = math.prod(s)
        out.append(flat[o:o + n].reshape(s))
        o += n
    return out


_BIG = ("mem_kv_w", "w_out", "a_w_in", "b_w_in", "c_w_in", "d_w_in")
SMALL_ROWS_MULT = 256


def _row8(v):
    v = v.reshape(-1, v.shape[-1])
    return jnp.pad(v, ((0, SUB - v.shape[0]), (0, 0)))


def kernel(x, mem, mem_kv_w, ln_g, ln_b, w_out, hgrn_lb_logits, a_w_in, a_w_s, a_b_s, b_w_in, b_norm_g, c_w_in, c_w_pool, c_scale, d_w_in, d_conv_w, d_conv_b, d_w_gx, d_b_gx, d_w_ga, d_b_ga, d_a_param, loss_target, m_mem_kv_w, m_ln_g, m_ln_b, m_w_out, m_hgrn_lb_logits, m_a_w_in, m_a_w_s, m_a_b_s, m_b_w_in, m_b_norm_g, m_c_w_in, m_c_w_pool, m_c_scale, m_d_w_in, m_d_conv_w, m_d_conv_b, m_d_w_gx, m_d_b_gx, m_d_w_ga, m_d_b_ga, m_d_a_param, v_mem_kv_w, v_ln_g, v_ln_b, v_w_out, v_hgrn_lb_logits, v_a_w_in, v_a_w_s, v_a_b_s, v_b_w_in, v_b_norm_g, v_c_w_in, v_c_w_pool, v_c_scale, v_d_w_in, v_d_conv_w, v_d_conv_b, v_d_w_gx, v_d_b_gx, v_d_w_ga, v_d_b_ga, v_d_a_param):
    names = ["mem_kv_w", "ln_g", "ln_b", "w_out", "hgrn_lb_logits", "a_w_in", "a_w_s", "a_b_s", "b_w_in", "b_norm_g", "c_w_in",
             "c_w_pool", "c_scale", "d_w_in", "d_conv_w", "d_conv_b", "d_w_gx", "d_b_gx", "d_w_ga", "d_b_ga", "d_a_param"]
    w = dict(mem_kv_w=mem_kv_w, ln_g=ln_g, ln_b=ln_b, w_out=w_out, hgrn_lb_logits=hgrn_lb_logits, a_w_in=a_w_in, a_w_s=a_w_s,
             a_b_s=a_b_s, b_w_in=b_w_in, b_norm_g=b_norm_g, c_w_in=c_w_in, c_w_pool=c_w_pool, c_scale=c_scale, d_w_in=d_w_in,
             d_conv_w=d_conv_w, d_conv_b=d_conv_b, d_w_gx=d_w_gx, d_b_gx=d_b_gx, d_w_ga=d_w_ga, d_b_ga=d_b_ga, d_a_param=d_a_param)
    m = dict(zip(names, [m_mem_kv_w, m_ln_g, m_ln_b, m_w_out, m_hgrn_lb_logits, m_a_w_in, m_a_w_s, m_a_b_s, m_b_w_in, m_b_norm_g,
                         m_c_w_in, m_c_w_pool, m_c_scale, m_d_w_in, m_d_conv_w, m_d_conv_b, m_d_w_gx, m_d_b_gx, m_d_w_ga,
                         m_d_b_ga, m_d_a_param]))
    v = dict(zip(names, [v_mem_kv_w, v_ln_g, v_ln_b, v_w_out, v_hgrn_lb_logits, v_a_w_in, v_a_w_s, v_a_b_s, v_b_w_in, v_b_norm_g,
                         v_c_w_in, v_c_w_pool, v_c_scale, v_d_w_in, v_d_conv_w, v_d_conv_b, v_d_w_gx, v_d_b_gx, v_d_w_ga,
                         v_d_b_ga, v_d_a_param]))
    xi, yi = lax.axis_index("x"), lax.axis_index("y")
    jshard = 2 * xi + yi
    x2 = x[0]
    mem2 = mem[0]
    tgt2 = loss_target[0]

    w_in_sh = [w[n][0].astype(bf16) for n in _BIG[2:]]
    w_out_sh = w_out.astype(bf16)
    gath0 = _ag_weights([mem_kv_w.astype(bf16), w_out_sh[0], w_in_sh[0]], ("rows", "win", "win"), jshard)
    w_kv = gath0[0].reshape(D_MODEL, 2 * XW)
    pending = [None]
    tie = None
    for l in range(1, DEPTH):
        bufs, ssem, rsem, tie = _ag_start([w_in_sh[l], w_out_sh[l]], jshard, f"gather_start{l}", tie)
        pending.append((bufs, ssem, rsem))
    tied_gain = {0: ln_g[0:1] + tie[0:1, 0:1]}

    def layer_weights(g_in, g_out):
        return (g_in.transpose(1, 0, 2).reshape(D_MODEL, -1), g_in.transpose(0, 2, 1).reshape(-1, D_MODEL),
                g_out.reshape(D_MODEL, D_MODEL), g_out.transpose(2, 0, 1).reshape(D_MODEL, D_MODEL))

    lw = [layer_weights(gath0[2], gath0[1])]

    def gather_small(shard):
        z = jnp.zeros((4, POOL_GROUP), f32)
        return lax.dynamic_update_slice(z, shard.reshape(1, POOL_GROUP), (jshard, 0))

    sm_sh = jnp.concatenate([gather_small(b_norm_g), gather_small(c_scale), gather_small(d_conv_b), gather_small(d_a_param)]
                            + [gather_small(d_conv_w[:, r]) for r in range(4)], axis=0)
    ci = lax.axis_index("c")
    sm_all = _all_reduce_small(_pack_flat([jnp.where(ci == 0, sm_sh, 0.0)], SUB * 2))
    sm = _unpack_flat(sm_all, [(8, 4 * POOL_GROUP)])[0]
    ng_full, scale_full, convb_full, ap_full = sm[0:1], sm[1:2], sm[2:3], sm[3:4]
    convw_full = sm[4:8]

    tril = jnp.tril(jnp.ones((HD, HD), bool))
    wtri = jnp.where(tril, a_w_s[0], 0.0)
    wbd = jnp.zeros((TOK, TOK), f32)
    for g in range(4):
        wbd = lax.dynamic_update_slice(wbd, c_w_pool[0, g], (g * POOL_GROUP, g * POOL_GROUP))
    kh, khT, vh, vhT, p_soft = _prep(mem2, w_kv, hgrn_lb_logits)
    prm = [
        dict(wtri=wtri.astype(bf16), wtriT=wtri.transpose(0, 2, 1).astype(bf16),
             bcolb=jnp.broadcast_to(a_b_s[0][:, :, None], (NH, HD, HD))),
        dict(lb=p_soft[1:2], ng=ng_full),
        dict(wbd=wbd.astype(bf16), wbdT=wbd.T.astype(bf16), scale=scale_full),
        dict(cw=_row8(convw_full), cb=convb_full, wgx=d_w_gx[0].astype(bf16), wgxT=d_w_gx[0].transpose(0, 2, 1).astype(bf16),
             bgx=d_b_gx.reshape(1, TOK), wga=d_w_ga[0].astype(bf16), wgaT=d_w_ga[0].transpose(0, 2, 1).astype(bf16),
             bga=d_b_ga.reshape(1, TOK), ap=ap_full),
    ]

    acts = []
    h = x2
    for l in range(DEPTH):
        if l:
            bufs, ssem, rsem = pending[l]
            lw.append(layer_weights(*_ag_wait(bufs, ssem, rsem, h, f"gather_wait{l}")))
        outs = _fwd_layer(l, h, lw[l][0], lw[l][2], tied_gain.get(l, ln_g[l:l + 1]), ln_b[l:l + 1], khT, vh, prm[l],
                          tgt2 if l == DEPTH - 1 else None)
        nfix = 4 if l == DEPTH - 1 else 3
        acts.append(dict(xin=h, proj=outs[1], z=outs[2], saves=outs[nfix:]))
        if l == DEPTH - 1:
            loss_part = outs[3]
        h = outs[0]
    loss = lax.psum(0.5 / D_MODEL * jnp.sum(loss_part), ("x", "y", "c"))

    dh = h
    dln = [None] * DEPTH
    dks, dvs = [None] * DEPTH, [None] * DEPTH
    sgr = [None] * DEPTH
    jc = jnp.stack([jshard, ci]).astype(jnp.int32)
    lkinds = ("win", "rows")
    q32s, flying, back_gain = [None] * DEPTH, [None] * DEPTH, {}
    for l in reversed(range(DEPTH)):
        a = acts[l]
        (dxin, dproj, mixedb, dyb, dln[l], dks[l], dvs[l]), sgr[l] = _bwd_layer(
            l, dh, a["z"], a["proj"], lw[l][1], lw[l][3], back_gain.get(l, ln_g[l:l + 1]), kh, khT, vh, vhT, prm[l], a["saves"])
        if _OFFS[l]["W"] // 4 % LANE:
            gw_in = _tn_gemm(a["xin"], dproj, f"grad_w_in{l}", 1).reshape(D_MODEL, 4, -1).transpose(1, 0, 2)
        else:
            gw_in = _tn_gemm_sharded(a["xin"], dproj, f"grad_w_in{l}")
        parts = [gw_in, _tn_gemm(mixedb, dyb, f"grad_w_out{l}", 1)]
        lk = lkinds
        if l == 0:
            parts.append(_kv_bwd(mem2, dks, dvs))
            lk = lkinds + ("rows",)
        gots = _rs_swap(parts, lk, f"rs_swap_halves{l}")
        sums = [_add_sibling(p, g, k, jc, f"rs_add_sibling{l}_{i}") for i, (p, g, k) in enumerate(zip(parts, gots, lk))]
        q32s[l] = [s[0] for s in sums]
        shapes = [p.shape for p in parts]
        if l:
            bufs, ssem, rsem, tok = _rs_owners_start([s[1] for s in sums], lk, shapes, f"rs_owners_start{l}")
            flying[l] = (bufs, ssem, rsem)
            back_gain[l - 1] = ln_g[l - 1:l] + tok[0:1, 0:1]
        else:
            last_got = _rs_owners([s[1] for s in sums], lk, shapes)
        dh = dxin
    grad_x = dh[None]
    fin, fin_kinds = {}, []
    for l in range(DEPTH):
        lk = lkinds + (("rows",) if l == 0 else ())
        got = last_got if l == 0 else _rs_owners_wait(*flying[l], lk, grad_x, f"rs_owners_wait{l}")
        fin[l] = [_add_chips(q, r, k, jc, f"rs_add_chips{l}_{i}") for i, (q, r, k) in enumerate(zip(q32s[l], got, lk))]
        fin_kinds += list(lk)
    joined = _rs_join([t for l in range(DEPTH) for t in fin[l]], tuple(fin_kinds))
    by_layer, o = [], 0
    for l in range(DEPTH):
        by_layer.append(joined[o:o + len(fin[l])])
        o += len(fin[l])
    gbig = {"mem_kv_w": by_layer[0][2], "w_out": jnp.stack([by_layer[l][1] for l in range(DEPTH)])}
    for l, n in enumerate(_BIG[2:]):
        gbig[n] = by_layer[l][0]
    g_sh, d_sh, m_sh, v_sh = {}, {}, {}, {}
    for n in _BIG:
        as2d = lambda t: t.reshape(-1, t.shape[-1])
        upd = _adamw(as2d(w[n]), as2d(gbig[n]), as2d(m[n]), as2d(v[n]), f"adamw_{n}")
        g_sh[n] = gbig[n].reshape(w[n].shape)
        d_sh[n], m_sh[n], v_sh[n] = (u.reshape(w[n].shape) for u in upd)

    dbs, dlogits = _small_finish(sgr[0]["dbacc"], p_soft, sgr[1]["dlb"])
    gs = {
        "ln_g": jnp.concatenate([dln[l][0:1] for l in range(DEPTH)], axis=0),
        "ln_b": jnp.concatenate([dln[l][1:2] for l in range(DEPTH)], axis=0),
        "hgrn_lb_logits": dlogits,
        "a_w_s": sgr[0]["dwtri"][None],
        "a_b_s": dbs[:, 0:NH].T[None],
        "b_norm_g": sgr[1]["dng"][0:1],
        "c_w_pool": jnp.stack([sgr[2]["dwbd"][g * POOL_GROUP:(g + 1) * POOL_GROUP, g * POOL_GROUP:(g + 1) * POOL_GROUP]
                               for g in range(4)])[None],
        "c_scale": sgr[2]["dscale"][0:1],
        "d_conv_w": sgr[3]["dcw"][0:4][None],
        "d_conv_b": sgr[3]["dvec"][3:4],
        "d_w_gx": sgr[3]["dwgx"][None],
        "d_b_gx": sgr[3]["dvec"][1:2].reshape(1, NH, HD),
        "d_w_ga": sgr[3]["dwga"][None],
        "d_b_ga": sgr[3]["dvec"][2:3].reshape(1, NH, HD),
        "d_a_param": sgr[3]["dvec"][0:1],
    }
    small = [n for n in names if n not in _BIG]
    drop1 = lambda t: t.reshape(t.shape[1:]) if t.ndim > 2 and t.shape[0] == 1 else t
    gsum = dict(zip(small, _all_reduce_many([drop1(gs[n]) for n in small])))
    for n in ("b_norm_g", "c_scale", "d_conv_b", "d_a_param"):
        gsum[n] = lax.dynamic_slice(gsum[n], (0, jshard * POOL_GROUP), (1, POOL_GROUP))
    gsum["d_conv_w"] = lax.dynamic_slice(gsum["d_conv_w"], (0, jshard * POOL_GROUP), (4, POOL_GROUP))
    upd = _adamw_many(*[[drop1(d[n]) for n in small] for d in (w, gsum, m, v)], "adamw_small")
    gsum = {n: gsum[n].reshape(w[n].shape) for n in small}
    d_sm, m_sm, v_sm = ({n: u.reshape(w[n].shape) for n, u in zip(small, us)} for us in upd)

    grads = {**gsum, **g_sh}
    deltas = {**d_sm, **d_sh}
    new_m = {**m_sm, **m_sh}
    new_v = {**v_sm, **v_sh}
    return (loss, grad_x, *[grads[n] for n in names], *[deltas[n] for n in names], *[new_m[n] for n in names],
            *[new_v[n] for n in names])
```

```python
import functools
import math

import jax
import jax.numpy as jnp
from jax import lax
from jax.experimental import pallas as pl
from jax.experimental.pallas import tpu as pltpu

f32 = jnp.float32
bf16 = jnp.bfloat16
MM = bf16

D_MODEL = 1024
TOK = 768
XW = 256
XHEADS = 4
XDIM = 64
HD = 128
NH = TOK // HD
CHUNK = 16
POOL_GROUP = 192
DEPTH = 4
ALPHA = (2 * DEPTH) ** 0.25
LN_EPS = 1e-5
RMS_EPS = 1e-6
LRU_C = 8.0
ADAM_LR, ADAM_B1, ADAM_B2, ADAM_EPS, ADAM_WD, ADAM_STEP = 0.001, 0.9, 0.999, 1e-08, 0.01, 10

_TS = (256, 256, 256, 256)
HGRN_SUB = 128
TK = 512
SUB = 8
LANE = 128
VMEM_LIMIT = 58 * 1024 * 1024

_OFFS = (
    dict(u=0, v=768, qx=1536, gate=1792, W=2816),
    dict(q=0, f=768, i=1536, qx=2304, gate=2560, W=3584),
    dict(p=0, qx=768, gate=1024, W=2048),
    dict(xb=0, qx=768, gate=1024, W=2048),
)
_PRM = (
    ("wtri", "wtriT", "bcolb"),
    ("lb", "ng"),
    ("wbd", "wbdT", "scale"),
    ("cw", "cb", "wgx", "wgxT", "bgx", "wga", "wgaT", "bga", "ap"),
)
MESH = pl.DeviceIdType.MESH


def _mm(a, b):
    return jnp.dot(a.astype(MM), b.astype(MM), preferred_element_type=f32)


def _mm_nt(a, b):
    return lax.dot_general(a.astype(MM), b.astype(MM), (((1,), (1,)), ((), ())), preferred_element_type=f32)


def _mm_tn(a, b):
    return lax.dot_general(a.astype(MM), b.astype(MM), (((0,), (0,)), ((), ())), preferred_element_type=f32)


def _mm_sel(sel, b):
    s = sel.astype(bf16)
    hi = b.astype(bf16)
    lo = (b - hi.astype(f32)).astype(bf16)
    return jnp.dot(s, hi, preferred_element_type=f32) + jnp.dot(s, lo, preferred_element_type=f32)


def _sig(x):
    return jax.nn.sigmoid(x)


_GC = math.sqrt(2.0 / math.pi)


def _gelu(x):
    t = jnp.tanh(_GC * (x + 0.044715 * x * x * x))
    return 0.5 * x * (1.0 + t), t


def _gelu_grad(x, t):
    return 0.5 * (1.0 + t) + 0.5 * x * (1.0 - t * t) * _GC * (1.0 + 3.0 * 0.044715 * x * x)


def _rowsum(x):
    return jnp.sum(x, axis=0, keepdims=True)


def _lmean(x):
    return jnp.mean(x, axis=-1, keepdims=True)


def _ln(z):
    mu = _lmean(z)
    zc = z - mu
    rstd = lax.rsqrt(_lmean(zc * zc) + LN_EPS)
    return zc * rstd, rstd


def _ln_bwd(dxh, xhat, rstd):
    return rstd * (dxh - _lmean(dxh) - xhat * _lmean(dxh * xhat))


def _hs(h):
    return slice(h * HD, (h + 1) * HD)


def _expm1(x):
    small = x * (1.0 + x * 0.5 * (1.0 + x * (1.0 / 3.0) * (1.0 + x * 0.25 * (1.0 + x * 0.2 * (1.0 + x * (1.0 / 6.0))))))
    return jnp.where(jnp.abs(x) < 0.25, small, jnp.exp(x) - 1.0)


def _softplus(x):
    e = jnp.exp(-jnp.abs(x))
    l1p = jnp.where(e < 1e-4, e - 0.5 * e * e, jnp.log(1.0 + e))
    return jnp.maximum(x, 0.0) + l1p


def _scan_fwd(a, b):
    n = a.shape[0]
    row = lax.broadcasted_iota(jnp.int32, a.shape, 0)
    d = 1
    while d < n:
        if d % SUB:
            m = row >= d
            b = jnp.where(m, a * pltpu.roll(b, d, 0) + b, b)
            a = jnp.where(m, a * pltpu.roll(a, d, 0), a)
        else:
            b = a * jnp.concatenate([jnp.zeros((d,) + b.shape[1:], f32), b[:n - d]], axis=0) + b
            a = a * jnp.concatenate([jnp.ones((d,) + a.shape[1:], f32), a[:n - d]], axis=0)
        d *= 2
    return a, b


def _scan_bwd(a, b):
    n = a.shape[0]
    row = lax.broadcasted_iota(jnp.int32, a.shape, 0)
    d = 1
    while d < n:
        if d % SUB:
            m = row < n - d
            b = jnp.where(m, a * pltpu.roll(b, n - d, 0) + b, b)
            a = jnp.where(m, a * pltpu.roll(a, n - d, 0), a)
        else:
            b = a * jnp.concatenate([b[d:], jnp.zeros((d,) + b.shape[1:], f32)], axis=0) + b
            a = a * jnp.concatenate([a[d:], jnp.ones((d,) + a.shape[1:], f32)], axis=0)
        d *= 2
    return a, b


def _chunk_mats(n):
    r = lax.broadcasted_iota(jnp.int32, (n, n), 0)
    c = lax.broadcasted_iota(jnp.int32, (n, n), 1)
    same = (r // CHUNK) == (c // CHUNK)
    return same, jnp.logical_and(same, c <= r)


def _pool_w(shape):
    lane = lax.broadcasted_iota(jnp.int32, shape, 1)
    return jnp.where(lane < POOL_GROUP, 2, jnp.where(lane < 2 * POOL_GROUP, 4, jnp.where(lane < 3 * POOL_GROUP, 8, 16)))


def _pool_pick(r1, r2, r3, r4):
    lane = lax.broadcasted_iota(jnp.int32, r1.shape, 1)
    return jnp.where(lane < POOL_GROUP, r1, jnp.where(lane < 2 * POOL_GROUP, r2, jnp.where(lane < 3 * POOL_GROUP, r3, r4)))


def _const_spec(a):
    nd = a.ndim
    return pl.BlockSpec(a.shape, lambda i, _nd=nd: (0,) * _nd, pipeline_mode=pl.Buffered(1))


def _acc_spec(shape):
    nd = len(shape)
    return pl.BlockSpec(shape, lambda i, _nd=nd: (0,) * _nd)


def _params(sem="arbitrary"):
    return pltpu.CompilerParams(dimension_semantics=(sem,), vmem_limit_bytes=VMEM_LIMIT)


def _xattn_fwd(qx, khT_ref, vh_ref):
    xo = jnp.zeros((qx.shape[0], XW), f32)
    ps = []
    for h in range(XHEADS):
        s = _mm(qx, khT_ref[h]) * (XDIM ** -0.5)
        e = jnp.exp(s - jnp.max(s, axis=-1, keepdims=True))
        p = e / jnp.sum(e, axis=-1, keepdims=True)
        xo = xo + _mm(p, vh_ref[h])
        ps.append(p)
    return xo, ps


def _hgrn_parallel(q_raw, fl, lb):
    n = q_raw.shape[0]
    same, tri = _chunk_mats(n)
    sq = _sig(q_raw)
    qf = q_raw * sq
    sgm = _sig(fl)
    f = lb + (1.0 - lb) * sgm
    logf = jnp.log(f)
    k = 1.0 - f
    g = _mm_sel(tri, logf)
    gl = _mm_sel(same, logf)
    eg = jnp.exp(g)
    eng = jnp.exp(-g)
    ee = jnp.exp(gl - g)
    return dict(sq=sq, qf=qf, sgm=sgm, f=f, k=k, eg=eg, eng=eng, ee=ee, q_dec=qf * eg, k_inv=k * eng, k_end=k * ee,
                a=jnp.exp(gl))


def _hgrn_intra(q_dec, k_inv, v):
    n = q_dec.shape[0]
    _, tri = _chunk_mats(HD)
    outs = []
    for h in range(NH):
        blks = []
        for b in range(n // HD):
            rs = slice(b * HD, (b + 1) * HD)
            sc = jnp.where(tri, _mm_nt(q_dec[rs, _hs(h)], k_inv[rs, _hs(h)]), 0.0)
            blks.append(_mm(sc, v[rs, _hs(h)]))
        outs.append(jnp.concatenate(blks, axis=0))
    return jnp.concatenate(outs, axis=-1)


def _cs(c):
    return slice(c * CHUNK, (c + 1) * CHUNK)


def _hgrn_inter_fwd(qdec_s, kend_s, v_s, a_s, oint_s, st_ref, states_s, u_s):
    n = qdec_s.shape[0] // CHUNK
    for c in range(n):
        for h in range(NH):
            u_s[c, h] = _mm_tn(v_s[_cs(c), _hs(h)], kend_s[_cs(c), _hs(h)])
    for h in range(NH):
        st = st_ref[h]
        for c in range(n):
            states_s[c, h] = st
            st = st * a_s[c * CHUNK:c * CHUNK + 1, _hs(h)] + u_s[c, h]
        st_ref[h] = st
    if oint_s is None:
        return
    for c in range(n):
        for h in range(NH):
            oint_s[_cs(c), _hs(h)] = _mm_nt(qdec_s[_cs(c), _hs(h)], states_s[c, h])


def _rms(o):
    outs, rs = [], []
    for h in range(NH):
        oh = o[:, _hs(h)]
        r = lax.rsqrt(_lmean(oh * oh) + RMS_EPS)
        outs.append(oh * r)
        rs.append(r)
    return jnp.concatenate(outs, axis=-1), rs


def _gmlp_core(u_raw, v_raw, wtri_ref, bcolb_ref):
    gu, tu = _gelu(u_raw)
    gv, tv = _gelu(v_raw)
    vns, rstds, mixeds = [], [], []
    for h in range(NH):
        vn, rstd = _ln(gv[:, _hs(h)])
        blks = []
        for n in range(u_raw.shape[0] // HD):
            blks.append(_mm(wtri_ref[h], vn[n * HD:(n + 1) * HD]) + bcolb_ref[h])
        vns.append(vn)
        rstds.append(rstd)
        mixeds.append(jnp.concatenate(blks, axis=0))
    mixed = jnp.concatenate(mixeds, axis=-1)
    return gu, tu, tv, vns, rstds, mixed


def _pool_core(p, carry, row0, wbd_ref):
    ext = jnp.concatenate([carry, p], axis=0)
    r1 = ext + pltpu.roll(ext, 1, 0)
    r2 = r1 + pltpu.roll(r1, 2, 0)
    r3 = r2 + pltpu.roll(r2, 4, 0)
    r4 = r3 + pltpu.roll(r3, 8, 0)
    sel = _pool_pick(r1, r2, r3, r4)[2 * SUB:]
    grow = row0 + lax.broadcasted_iota(jnp.int32, p.shape, 0)
    inv_cnt = 1.0 / jnp.minimum(grow + 1, _pool_w(p.shape)).astype(f32)
    diff = sel * inv_cnt - p
    return diff, inv_cnt, _mm(diff, wbd_ref[...])


def _lru_core(xb, ccar, row0, p):
    ext = jnp.concatenate([ccar, xb], axis=0)
    cw = p["cw"]
    x1, x2, x3 = pltpu.roll(ext, 1, 0)[SUB:], pltpu.roll(ext, 2, 0)[SUB:], pltpu.roll(ext, 3, 0)[SUB:]
    xc = cw[3:4, :] * xb + cw[2:3, :] * x1 + cw[1:2, :] * x2 + cw[0:1, :] * x3 + p["cb"][...]
    gxs, gas = [], []
    for h in range(NH):
        gxs.append(_mm(xc[:, _hs(h)], p["wgx"][h]))
        gas.append(_mm(xc[:, _hs(h)], p["wga"][h]))
    gx = _sig(jnp.concatenate(gxs, axis=-1) + p["bgx"][...])
    ga = _sig(jnp.concatenate(gas, axis=-1) + p["bga"][...])
    sp = _softplus(-p["ap"][...])
    la = -LRU_C * ga * sp
    a = jnp.exp(la)
    grow = row0 + lax.broadcasted_iota(jnp.int32, xb.shape, 0)
    first = grow == 0
    mult = jnp.where(first, 1.0, jnp.sqrt(-_expm1(2.0 * la)))
    bt = mult * gx * xc
    return dict(x1=x1, x2=x2, x3=x3, xc=xc, gx=gx, ga=ga, sp=sp, a=a, mult=mult, bt=bt, first=first)


def _fwd_layer(kind, xin, w_in, w_out, lng, lnb, khT, vh, prm, tgt):
    S = xin.shape[0]
    TS = _TS[kind]
    nt = S // TS
    off = _OFFS[kind]
    W = off["W"]
    last = tgt is not None
    pnames = _PRM[kind]
    pvals = [prm[n] for n in pnames]

    def body(*refs):
        it = iter(refs)
        xin_ref, win_ref, wout_ref, lng_ref, lnb_ref, khT_ref, vh_ref = (next(it) for _ in range(7))
        p = {n: next(it) for n in pnames}
        tgt_ref = next(it) if last else None
        xout_ref, proj_ref, z_ref = next(it), next(it), next(it)
        loss_ref = next(it) if last else None
        rest = list(it)
        i = pl.program_id(0)
        x = xin_ref[...]
        proj_ref[...] = _mm(x, win_ref[...])

        if kind == 0:
            gu, _, _, _, _, mixed = _gmlp_core(proj_ref[:, 0:TOK], proj_ref[:, TOK:2 * TOK], p["wtri"], p["bcolb"])
            tok = gu * mixed
        elif kind == 1:
            st_save, o_save, st_ref, states_s, u_s, qdec_s, kend_s, v_s, a_s, oint_s = rest

            @pl.when(i == 0)
            def _():
                st_ref[...] = jnp.zeros_like(st_ref)

            st_save[0, 0] = st_ref[...]
            v = proj_ref[:, 2 * TOK:3 * TOK]
            hp = _hgrn_parallel(proj_ref[:, 0:TOK], proj_ref[:, TOK:2 * TOK], p["lb"][...])
            qdec_s[...] = hp["q_dec"]
            kend_s[...] = hp["k_end"]
            v_s[...] = v
            a_s[...] = hp["a"]
            o_intra = _hgrn_intra(hp["q_dec"], hp["k_inv"], v)
            _hgrn_inter_fwd(qdec_s, kend_s, v_s, a_s, oint_s, st_ref, states_s, u_s)
            o = o_intra + oint_s[...]
            o_save[0] = o
            for sub in range(1, TS // HGRN_SUB):
                st_save[0, sub] = states_s[sub * HGRN_SUB // CHUNK]
            on, _ = _rms(o)
            tok = on * p["ng"][...]
        elif kind == 2:
            pc_save, pcar = rest

            @pl.when(i == 0)
            def _():
                pcar[...] = jnp.zeros_like(pcar)

            pc_save[0] = pcar[...]
            pp = proj_ref[:, 0:TOK]
            _, _, y = _pool_core(pp, pcar[...], i * TS, p["wbd"])
            pcar[...] = pp[TS - 2 * SUB:, :]
            tok = y * p["scale"][...]
        else:
            cc_save, hc_save, h_save, ccar, hcar = rest

            @pl.when(i == 0)
            def _():
                ccar[...] = jnp.zeros_like(ccar)
                hcar[...] = jnp.zeros_like(hcar)

            cc_save[0] = ccar[...]
            hc_save[0] = hcar[...]
            xb = proj_ref[:, 0:TOK]
            lc = _lru_core(xb, ccar[...], i * TS, p)
            P, B = _scan_fwd(lc["a"], lc["bt"])
            tok = P * hcar[SUB - 1:SUB, :] + B
            h_save[0] = tok
            ccar[...] = xb[TS - SUB:, :]
            hcar[...] = tok[TS - SUB:, :]

        xo, _ = _xattn_fwd(proj_ref[:, off["qx"]:off["qx"] + XW], khT_ref, vh_ref)
        gate = proj_ref[:, off["gate"]:off["gate"] + D_MODEL]
        mixed = jnp.concatenate([tok, xo], axis=-1) * (gate * _sig(gate))
        z = ALPHA * x + _mm(mixed, wout_ref[...])
        z_ref[...] = z
        xhat, _ = _ln(z)
        xout = xhat * lng_ref[...] + lnb_ref[...]
        if last:
            e = xout - tgt_ref[...]
            xout_ref[...] = e * (1.0 / D_MODEL)
            es = _rowsum(e * e)
            tot = es[:, 0:LANE]
            for j in range(1, D_MODEL // LANE):
                tot = tot + es[:, j * LANE:(j + 1) * LANE]

            @pl.when(i == 0)
            def _():
                loss_ref[...] = jnp.zeros_like(loss_ref)

            loss_ref[0:1, :] += tot
        else:
            xout_ref[...] = xout

    tile = lambda w: pl.BlockSpec((TS, w), lambda i: (i, 0))
    in_arrays = [xin, w_in, w_out, lng, lnb, khT, vh] + pvals + ([tgt] if last else [])
    in_specs = [tile(D_MODEL)] + [_const_spec(a) for a in in_arrays[1:7 + len(pvals)]] + ([tile(D_MODEL)] if last else [])
    out_shape = [jax.ShapeDtypeStruct((S, D_MODEL), f32), jax.ShapeDtypeStruct((S, W), f32), jax.ShapeDtypeStruct((S, D_MODEL), f32)]
    out_specs = [tile(D_MODEL), tile(W), tile(D_MODEL)]
    if last:
        out_shape.append(jax.ShapeDtypeStruct((SUB, LANE), f32))
        out_specs.append(_acc_spec((SUB, LANE)))
    scratch = []
    save = lambda *s: (jax.ShapeDtypeStruct((nt,) + s, f32), pl.BlockSpec((1,) + s, lambda i, _n=len(s): (i,) + (0,) * _n))
    if kind == 1:
        saved = [save(TS // HGRN_SUB, NH, HD, HD), save(TS, TOK)]
        scratch = ([pltpu.VMEM((NH, HD, HD), f32)] + [pltpu.VMEM((TS // CHUNK, NH, HD, HD), f32)] * 2
                   + [pltpu.VMEM((TS, TOK), f32)] * 5)
    elif kind == 2:
        saved = [save(2 * SUB, TOK)]
        scratch = [pltpu.VMEM((2 * SUB, TOK), f32)]
    elif kind == 3:
        saved = [save(SUB, TOK), save(SUB, TOK), save(TS, TOK)]
        scratch = [pltpu.VMEM((SUB, TOK), f32)] * 2
    else:
        saved = []
    for sh, sp in saved:
        out_shape.append(sh)
        out_specs.append(sp)
    return pl.pallas_call(body, name=f"fwd_layer{kind}", grid=(nt,), in_specs=in_specs, out_specs=out_specs,
                          out_shape=out_shape, scratch_shapes=scratch, compiler_params=_params())(*in_arrays)


def _small_grad_shapes(kind):
    if kind == 0:
        return dict(dwtri=(NH, HD, HD), dbacc=(NH, HD, HD))
    if kind == 1:
        return dict(dlb=(SUB, TOK), dng=(SUB, TOK))
    if kind == 2:
        return dict(dwbd=(TOK, TOK), dscale=(SUB, TOK))
    return dict(dcw=(SUB, TOK), dvec=(SUB, TOK), dwgx=(NH, HD, HD), dwga=(NH, HD, HD))


def _bwd_layer(kind, dxout, z, proj, w_inT, w_outT, lng, kh, khT, vh, vhT, prm, saves):
    S = dxout.shape[0]
    TS = _TS[kind]
    nt = S // TS
    off = _OFFS[kind]
    W = off["W"]
    pnames = _PRM[kind]
    pvals = [prm[n] for n in pnames]
    sg_shapes = _small_grad_shapes(kind)
    sg_names = list(sg_shapes)
    n_saves = len(saves)

    def body(*refs):
        it = iter(refs)
        dxo_ref, z_ref, proj_ref, winT_ref, woutT_ref, lng_ref, kh_ref, khT_ref, vh_ref, vhT_ref = (next(it) for _ in range(10))
        p = {n: next(it) for n in pnames}
        sv = [next(it) for _ in range(n_saves)]
        dxin_ref, dproj_ref, mixed_ref, dy_ref, dln_ref, dk_ref, dv_ref = (next(it) for _ in range(7))
        sg = {n: next(it) for n in sg_names}
        rest = list(it)
        step = pl.program_id(0)
        i = nt - 1 - step

        @pl.when(step == 0)
        def _():
            dln_ref[...] = jnp.zeros_like(dln_ref)
            dk_ref[...] = jnp.zeros_like(dk_ref)
            dv_ref[...] = jnp.zeros_like(dv_ref)
            for n in sg_names:
                sg[n][...] = jnp.zeros_like(sg[n])

        dxo = dxo_ref[...]
        xhat, rstd = _ln(z_ref[...])
        dln_ref[0:1, :] += _rowsum(dxo * xhat)
        dln_ref[1:2, :] += _rowsum(dxo)
        dz = _ln_bwd(dxo * lng_ref[...], xhat, rstd)
        dyb = dz.astype(bf16)
        dy_ref[...] = dyb
        dmixed = _mm(dyb, woutT_ref[...])

        aux = {}
        if kind == 0:
            u_raw, v_raw = proj_ref[:, 0:TOK], proj_ref[:, TOK:2 * TOK]
            gu, tu, tv, vns, rstds, mx = _gmlp_core(u_raw, v_raw, p["wtri"], p["bcolb"])
            tok = gu * mx
        elif kind == 1:
            st_save, o_save = sv
            (dst_ref, fst_ref, states_s, dsts_s, u_s, qdec_s, kend_s, v_s, a_s, do_s, dqdec_s, dkend_s, dv_s,
             dgl_s) = rest

            @pl.when(step == 0)
            def _():
                dst_ref[...] = jnp.zeros_like(dst_ref)

            o = o_save[0]
            on, rs = _rms(o)
            tok = on * p["ng"][...]
            aux = dict(o=o, on=on, rs=rs)
        elif kind == 2:
            pc_save, = sv
            dpcar, = rest
            pp = proj_ref[:, 0:TOK]
            diff, inv_cnt, y = _pool_core(pp, pc_save[0], i * TS, p["wbd"])
            tok = y * p["scale"][...]
        else:
            cc_save, hc_save, h_save = sv
            dccar, gcar = rest
            xb = proj_ref[:, 0:TOK]
            lc = _lru_core(xb, cc_save[0], i * TS, p)
            hin = hc_save[0, SUB - 1:SUB, :]
            tok = h_save[0]

        xo, ps = _xattn_fwd(proj_ref[:, off["qx"]:off["qx"] + XW], khT_ref, vh_ref)
        gate = proj_ref[:, off["gate"]:off["gate"] + D_MODEL]
        sgm = _sig(gate)
        sgate = gate * sgm
        cat = jnp.concatenate([tok, xo], axis=-1)
        mixed_ref[...] = (cat * sgate).astype(bf16)
        dcat = dmixed * sgate
        dproj_ref[:, off["gate"]:off["gate"] + D_MODEL] = (dmixed * cat * (sgm * (1.0 + gate * (1.0 - sgm)))).astype(bf16)
        dtok = dcat[:, 0:TOK]
        dxo_att = dcat[:, TOK:]

        qx = proj_ref[:, off["qx"]:off["qx"] + XW]
        dqx = jnp.zeros((TS, XW), f32)
        for h in range(XHEADS):
            dp = _mm(dxo_att, vhT_ref[h])
            ds = ps[h] * (dp - jnp.sum(dp * ps[h], axis=-1, keepdims=True)) * (XDIM ** -0.5)
            dqx = dqx + _mm(ds, kh_ref[h])
            dk_ref[h] += _mm_tn(ds, qx)
            dv_ref[h] += _mm_tn(ps[h], dxo_att)
        dproj_ref[:, off["qx"]:off["qx"] + XW] = dqx.astype(bf16)

        if kind == 0:
            tril = lax.broadcasted_iota(jnp.int32, (HD, HD), 1) <= lax.broadcasted_iota(jnp.int32, (HD, HD), 0)
            dgu = dtok * mx
            dmx = dtok * gu
            dgvs = []
            for h in range(NH):
                dmh = dmx[:, _hs(h)]
                blks = []
                for n in range(TS // HD):
                    rs_ = slice(n * HD, (n + 1) * HD)
                    blks.append(_mm(p["wtriT"][h], dmh[rs_]))
                    sg["dwtri"][h] += jnp.where(tril, _mm_nt(dmh[rs_], vns[h][rs_]), 0.0)
                    sg["dbacc"][h] += dmh[rs_]
                dgvs.append(_ln_bwd(jnp.concatenate(blks, axis=0), vns[h], rstds[h]))
            dgv = jnp.concatenate(dgvs, axis=-1)
            dproj_ref[:, 0:TOK] = (dgu * _gelu_grad(u_raw, tu)).astype(bf16)
            dproj_ref[:, TOK:2 * TOK] = (dgv * _gelu_grad(v_raw, tv)).astype(bf16)
        elif kind == 1:
            o, on, rs = aux["o"], aux["on"], aux["rs"]
            ng = p["ng"][...]
            lb = p["lb"][...]
            sg["dng"][0:1, :] += _rowsum(dtok * on)
            dn = dtok * ng
            dos = []
            for h in range(NH):
                oh, r = o[:, _hs(h)], rs[h]
                dos.append(r * (dn[:, _hs(h)] - oh * (r * r) * _lmean(dn[:, _hs(h)] * oh)))
            do_all = jnp.concatenate(dos, axis=-1)
            _, tri = _chunk_mats(HD)
            same, _ = _chunk_mats(HGRN_SUB)
            triT = jnp.logical_and(same, lax.broadcasted_iota(jnp.int32, (HGRN_SUB, HGRN_SUB), 1)
                                   >= lax.broadcasted_iota(jnp.int32, (HGRN_SUB, HGRN_SUB), 0))
            row16 = lax.broadcasted_iota(jnp.int32, (CHUNK, HD), 0)
            nch = HGRN_SUB // CHUNK
            for sub in reversed(range(TS // HGRN_SUB)):
                rr = slice(sub * HGRN_SUB, (sub + 1) * HGRN_SUB)
                q_raw, v = proj_ref[rr, 0:TOK], proj_ref[rr, 2 * TOK:3 * TOK]
                hp = _hgrn_parallel(q_raw, proj_ref[rr, TOK:2 * TOK], lb)
                qdec_s[...] = hp["q_dec"]
                kend_s[...] = hp["k_end"]
                v_s[...] = v
                a_s[...] = hp["a"]
                fst_ref[...] = st_save[0, sub]
                _hgrn_inter_fwd(qdec_s, kend_s, v_s, a_s, None, fst_ref, states_s, u_s)
                do = do_all[rr]
                do_s[...] = do
                dqd, dki, dvi = [], [], []
                for h in range(NH):
                    bq, bk, bv = [], [], []
                    for b in range(HGRN_SUB // HD):
                        rs_ = slice(b * HD, (b + 1) * HD)
                        qd, ki = hp["q_dec"][rs_, _hs(h)], hp["k_inv"][rs_, _hs(h)]
                        sc = jnp.where(tri, _mm_nt(qd, ki), 0.0)
                        dsc = jnp.where(tri, _mm_nt(do[rs_, _hs(h)], v[rs_, _hs(h)]), 0.0)
                        bv.append(_mm_tn(sc, do[rs_, _hs(h)]))
                        bq.append(_mm(dsc, ki))
                        bk.append(_mm_tn(dsc, qd))
                    dqd.append(jnp.concatenate(bq, axis=0))
                    dki.append(jnp.concatenate(bk, axis=0))
                    dvi.append(jnp.concatenate(bv, axis=0))
                dqdec_s[...] = jnp.concatenate(dqd, axis=-1)
                dk_inv = jnp.concatenate(dki, axis=-1)
                dv_s[...] = jnp.concatenate(dvi, axis=-1)
                for c in range(nch):
                    for h in range(NH):
                        u_s[c, h] = _mm_tn(do_s[_cs(c), _hs(h)], qdec_s[_cs(c), _hs(h)])
                for h in range(NH):
                    dst = dst_ref[h]
                    for c in reversed(range(nch)):
                        dsts_s[c, h] = dst
                        dst = dst * a_s[c * CHUNK:c * CHUNK + 1, _hs(h)] + u_s[c, h]
                    dst_ref[h] = dst
                for c in range(nch):
                    for h in range(NH):
                        stp = states_s[c, h]
                        dst = dsts_s[c, h]
                        dqdec_s[_cs(c), _hs(h)] += _mm(do_s[_cs(c), _hs(h)], stp)
                        dkend_s[_cs(c), _hs(h)] = _mm(v_s[_cs(c), _hs(h)], dst)
                        dv_s[_cs(c), _hs(h)] += _mm_nt(kend_s[_cs(c), _hs(h)], dst)
                        da = jnp.sum(dst * stp, axis=0, keepdims=True) * a_s[c * CHUNK:c * CHUNK + 1, _hs(h)]
                        dgl_s[_cs(c), _hs(h)] = jnp.where(row16 == 0, jnp.broadcast_to(da, (CHUNK, HD)), 0.0)
                dq_dec = dqdec_s[...]
                dk_end = dkend_s[...]
                dg = dq_dec * hp["q_dec"] - dk_inv * hp["k_inv"] - dk_end * hp["k_end"]
                dk = dk_inv * hp["eng"] + dk_end * hp["ee"]
                dglr = dk_end * hp["k_end"] + dgl_s[...]
                dlogf = _mm_sel(triT, dg) + _mm_sel(same, dglr)
                df = dlogf / hp["f"] - dk
                sg["dlb"][0:1, :] += _rowsum(df * (1.0 - hp["sgm"]))
                dproj_ref[rr, 0:TOK] = (dq_dec * hp["eg"] * (hp["sq"] * (1.0 + q_raw * (1.0 - hp["sq"])))).astype(bf16)
                dproj_ref[rr, TOK:2 * TOK] = (df * (1.0 - lb) * hp["sgm"] * (1.0 - hp["sgm"])).astype(bf16)
                dproj_ref[rr, 2 * TOK:3 * TOK] = dv_s[...].astype(bf16)
        elif kind == 2:
            @pl.when(step == 0)
            def _():
                dpcar[...] = jnp.zeros_like(dpcar)

            sg["dscale"][0:1, :] += _rowsum(dtok * y)
            dyp = dtok * p["scale"][...]
            sg["dwbd"][...] += _mm_tn(diff, dyp)
            ddiff = _mm(dyp, p["wbdT"][...])
            q = ddiff * inv_cnt
            ext = jnp.concatenate([q, dpcar[...]], axis=0)
            n = TS + 2 * SUB
            r1 = ext + pltpu.roll(ext, n - 1, 0)
            r2 = r1 + pltpu.roll(r1, n - 2, 0)
            r3 = r2 + pltpu.roll(r2, n - 4, 0)
            r4 = r3 + pltpu.roll(r3, n - 8, 0)
            dproj_ref[:, 0:TOK] = (_pool_pick(r1, r2, r3, r4)[:TS] - ddiff).astype(bf16)
            dpcar[...] = q[0:2 * SUB, :]
        else:
            @pl.when(step == 0)
            def _():
                dccar[...] = jnp.zeros_like(dccar)
                gcar[...] = jnp.zeros_like(gcar)

            a, mult, gx, ga, xc = lc["a"], lc["mult"], lc["gx"], lc["ga"], lc["xc"]
            row = lax.broadcasted_iota(jnp.int32, (TS, TOK), 0)
            an = jnp.where(row == TS - 1, 1.0, pltpu.roll(a, TS - 1, 0))
            Pb, Bb = _scan_bwd(an, dtok)
            lam = Pb * gcar[0:1, :] + Bb
            gcar[...] = (a * lam)[0:SUB, :]
            hprev = jnp.where(row == 0, jnp.broadcast_to(hin, (TS, TOK)), pltpu.roll(tok, 1, 0))
            dmult = lam * gx * xc
            dgx = lam * mult * xc
            dxc = lam * mult * gx
            dla = lam * hprev * a - jnp.where(lc["first"], 0.0, dmult * a * a / mult)
            sp = lc["sp"]
            dga = -LRU_C * sp * dla
            dsp = _rowsum(-LRU_C * ga * dla)
            sg["dvec"][0:1, :] += dsp * (-_sig(-p["ap"][...]))
            dpx = dgx * gx * (1.0 - gx)
            dpa = dga * ga * (1.0 - ga)
            sg["dvec"][1:2, :] += _rowsum(dpx)
            sg["dvec"][2:3, :] += _rowsum(dpa)
            dxcs = []
            for h in range(NH):
                dxcs.append(_mm(dpx[:, _hs(h)], p["wgxT"][h]) + _mm(dpa[:, _hs(h)], p["wgaT"][h]))
                sg["dwgx"][h] += _mm_tn(xc[:, _hs(h)], dpx[:, _hs(h)])
                sg["dwga"][h] += _mm_tn(xc[:, _hs(h)], dpa[:, _hs(h)])
            dxc = dxc + jnp.concatenate(dxcs, axis=-1)
            sg["dvec"][3:4, :] += _rowsum(dxc)
            sg["dcw"][3:4, :] += _rowsum(dxc * xb)
            sg["dcw"][2:3, :] += _rowsum(dxc * lc["x1"])
            sg["dcw"][1:2, :] += _rowsum(dxc * lc["x2"])
            sg["dcw"][0:1, :] += _rowsum(dxc * lc["x3"])
            ext = jnp.concatenate([dxc, dccar[...]], axis=0)
            n = TS + SUB
            cw = p["cw"]
            dproj_ref[:, 0:TOK] = (cw[3:4, :] * dxc + cw[2:3, :] * pltpu.roll(ext, n - 1, 0)[:TS]
                                   + cw[1:2, :] * pltpu.roll(ext, n - 2, 0)[:TS]
                                   + cw[0:1, :] * pltpu.roll(ext, n - 3, 0)[:TS]).astype(bf16)
            dccar[...] = dxc[0:SUB, :]

        dxin_ref[...] = ALPHA * dz + _mm(dproj_ref[...], winT_ref[...])

    rtile = lambda w: pl.BlockSpec((TS, w), lambda s: (nt - 1 - s, 0))
    consts = [w_inT, w_outT, lng, kh, khT, vh, vhT] + pvals
    in_arrays = [dxout, z, proj] + consts + list(saves)
    in_specs = [rtile(D_MODEL), rtile(D_MODEL), rtile(W)] + [_const_spec(a) for a in consts]
    for a in saves:
        in_specs.append(pl.BlockSpec((1,) + a.shape[1:], lambda s, _n=a.ndim - 1: (nt - 1 - s,) + (0,) * _n))
    out_shape = [jax.ShapeDtypeStruct((S, D_MODEL), f32), jax.ShapeDtypeStruct((S, W), bf16),
                 jax.ShapeDtypeStruct((S, D_MODEL), bf16), jax.ShapeDtypeStruct((S, D_MODEL), bf16),
                 jax.ShapeDtypeStruct((SUB, D_MODEL), f32), jax.ShapeDtypeStruct((XHEADS, XW, XW), f32),
                 jax.ShapeDtypeStruct((XHEADS, XW, XW), f32)]
    out_specs = [rtile(D_MODEL), rtile(W), rtile(D_MODEL), rtile(D_MODEL), _acc_spec((SUB, D_MODEL)),
                 _acc_spec((XHEADS, XW, XW)), _acc_spec((XHEADS, XW, XW))]
    for n in sg_names:
        out_shape.append(jax.ShapeDtypeStruct(sg_shapes[n], f32))
        out_specs.append(_acc_spec(sg_shapes[n]))
    if kind == 1:
        scratch = ([pltpu.VMEM((NH, HD, HD), f32)] * 2 + [pltpu.VMEM((HGRN_SUB // CHUNK, NH, HD, HD), f32)] * 3
                   + [pltpu.VMEM((HGRN_SUB, TOK), f32)] * 9)
    elif kind == 2:
        scratch = [pltpu.VMEM((2 * SUB, TOK), f32)]
    elif kind == 3:
        scratch = [pltpu.VMEM((SUB, TOK), f32)] * 2
    else:
        scratch = []
    outs = pl.pallas_call(body, name=f"bwd_layer{kind}", grid=(nt,), in_specs=in_specs, out_specs=out_specs,
                          out_shape=out_shape, scratch_shapes=scratch, compiler_params=_params())(*in_arrays)
    return outs[:7], dict(zip(sg_names, outs[7:]))


def _prep(mem, w_kv, logits):
    def body(mem_ref, w_ref, lg_ref, kh_ref, khT_ref, vh_ref, vhT_ref, p_ref):
        kv = _mm(mem_ref[...], w_ref[...])
        k, v = kv[:, 0:XW], kv[:, XW:]
        kT, vT = k.T, v.T
        col = lax.broadcasted_iota(jnp.int32, (XW, XW), 1) // XDIM
        row = lax.broadcasted_iota(jnp.int32, (XW, XW), 0) // XDIM
        for h in range(XHEADS):
            kh_ref[h] = jnp.where(col == h, k, 0.0).astype(bf16)
            vh_ref[h] = jnp.where(col == h, v, 0.0).astype(bf16)
            khT_ref[h] = jnp.where(row == h, kT, 0.0).astype(bf16)
            vhT_ref[h] = jnp.where(row == h, vT, 0.0).astype(bf16)
        lg = lg_ref[...]
        e = jnp.exp(lg - jnp.max(lg, axis=0, keepdims=True))
        p_ref[...] = e / jnp.sum(e, axis=0, keepdims=True)

    vm = pl.BlockSpec(memory_space=pltpu.VMEM)
    hs = jax.ShapeDtypeStruct((XHEADS, XW, XW), bf16)
    return pl.pallas_call(body, name="prep_memory", in_specs=[vm] * 3, out_specs=[vm] * 5,
                          out_shape=[hs, hs, hs, hs, jax.ShapeDtypeStruct(logits.shape, f32)])(mem, w_kv, logits)


def _kv_bwd(mem, dks, dvs):
    def body(mem_ref, *refs):
        out_ref = refs[-1]
        col = lax.broadcasted_iota(jnp.int32, (XW, XW), 1) // XDIM
        dk = jnp.zeros((XW, XW), f32)
        dv = jnp.zeros((XW, XW), f32)
        for l in range(DEPTH):
            for h in range(XHEADS):
                dk = dk + jnp.where(col == h, refs[l][h], 0.0)
                dv = dv + jnp.where(col == h, refs[DEPTH + l][h], 0.0)
        out_ref[:, 0:XW] = _mm_tn(mem_ref[...], dk)
        out_ref[:, XW:] = _mm_tn(mem_ref[...], dv)

    vm = pl.BlockSpec(memory_space=pltpu.VMEM)
    return pl.pallas_call(body, name="kv_bwd", in_specs=[vm] * (1 + 2 * DEPTH), out_specs=vm,
                          out_shape=jax.ShapeDtypeStruct((D_MODEL, 2 * XW), f32))(mem, *dks, *dvs)


def _tn_gemm(a, b, name, nb):
    S, M = a.shape
    N = b.shape[1]
    NB = N // nb
    nk = S // TK

    def body(a_ref, b_ref, o_ref):
        @pl.when(pl.program_id(1) == 0)
        def _():
            o_ref[...] = jnp.zeros_like(o_ref)

        o_ref[...] += _mm_tn(a_ref[...], b_ref[...])

    return pl.pallas_call(body, name=name, grid=(nb, nk),
                          in_specs=[pl.BlockSpec((TK, M), lambda j, k: (k, 0)), pl.BlockSpec((TK, NB), lambda j, k: (k, j))],
                          out_specs=pl.BlockSpec((M, NB), lambda j, k: (0, j)),
                          out_shape=jax.ShapeDtypeStruct((M, N), f32),
                          compiler_params=pltpu.CompilerParams(dimension_semantics=("parallel", "arbitrary"),
                                                               vmem_limit_bytes=VMEM_LIMIT))(a, b)


def _rows_block(R, mult=16, cap=1024):
    best = R
    for d in range(mult, min(R, cap) + 1, mult):
        if R % d == 0:
            best = d
    return best


def _tn_gemm_sharded(a, b, name):
    S, M = a.shape
    Wq = b.shape[1] // 4
    nk = S // TK

    def body(a_ref, b_ref, o_ref):
        @pl.when(pl.program_id(0) == 0)
        def _():
            o_ref[...] = jnp.zeros_like(o_ref)

        at = a_ref[...].astype(MM)
        for j in range(4):
            o_ref[j] += _mm_tn(at, b_ref[:, j * Wq:(j + 1) * Wq])

    return pl.pallas_call(body, name=name, grid=(nk,),
                          in_specs=[pl.BlockSpec((TK, M), lambda k: (k, 0)), pl.BlockSpec((TK, 4 * Wq), lambda k: (k, 0))],
                          out_specs=pl.BlockSpec((4, M, Wq), lambda k: (0, 0, 0)),
                          out_shape=jax.ShapeDtypeStruct((4, M, Wq), f32), compiler_params=_params())(a, b)


HALF_ROWS = D_MODEL // 2
SHARD_ROWS = D_MODEL // 4


def _half_of_full(ref, kind, h):
    if kind == "rows":
        cols = ref.shape[1] // 2
        return ref.at[:, pl.ds(h * cols, cols)]
    return ref.at[:, pl.ds(h * HALF_ROWS, HALF_ROWS)]


def _shard_of_half(ref, kind, j):
    if kind == "rows":
        return ref.at[pl.ds(j * SHARD_ROWS, SHARD_ROWS)]
    return ref.at[j]


def _half_of_shard(ref, kind, h):
    if kind == "rows":
        cols = ref.shape[1] // 2
        return ref.at[:, pl.ds(h * cols, cols)]
    rows = ref.shape[0] // 2
    return ref.at[pl.ds(h * rows, rows)]


def _half_shape(full_shape, kind):
    if kind == "rows":
        return (full_shape[0], full_shape[1] // 2)
    return (4, HALF_ROWS, full_shape[2])


def _shard_half_shape(full_shape, kind):
    if kind == "rows":
        return (SHARD_ROWS, full_shape[1] // 2)
    return (HALF_ROWS, full_shape[2])


def _shard_shape(full_shape, kind):
    if kind == "rows":
        return (SHARD_ROWS, full_shape[1])
    return (D_MODEL, full_shape[2])


def _ew_call(body, name, grid, jc, ins, in_specs, out_shape, out_specs):
    gs = pltpu.PrefetchScalarGridSpec(num_scalar_prefetch=1, grid=grid, in_specs=in_specs, out_specs=out_specs)
    return pl.pallas_call(body, name=name, grid_spec=gs, out_shape=out_shape,
                          compiler_params=pltpu.CompilerParams(dimension_semantics=("parallel",) * len(grid),
                                                               vmem_limit_bytes=VMEM_LIMIT))(jc, *ins)


def _add_sibling(part, got, kind, jc, name):
    def body(jc_ref, a_ref, b_ref, o_ref, ob_ref):
        s = a_ref[...] + b_ref[...]
        o_ref[...] = s
        ob_ref[...] = s.astype(bf16)

    if kind == "rows":
        R, C = part.shape[0], part.shape[1] // 2
        grid = (2,)
        mine = pl.BlockSpec((R // 2, C), lambda i, jc_ref: (i, jc_ref[1]))
        spec = pl.BlockSpec((R // 2, C), lambda i, jc_ref: (i, 0))
    else:
        C = part.shape[2]
        grid = (4, 2)
        mine = pl.BlockSpec((None, HALF_ROWS // 2, C), lambda s, i, jc_ref: (s, 2 * jc_ref[1] + i, 0))
        spec = pl.BlockSpec((None, HALF_ROWS // 2, C), lambda s, i, jc_ref: (s, i, 0))
    hs = _half_shape(part.shape, kind)
    return _ew_call(body, name, grid, jc, [part, got], [mine, spec],
                    [jax.ShapeDtypeStruct(hs, f32), jax.ShapeDtypeStruct(hs, bf16)], [spec, spec])


def _add_chips(q32, r, kind, jc, name):
    def body(jc_ref, q_ref, r_ref, out_ref):
        out_ref[...] = ((q_ref[...] + r_ref[0].astype(f32)) + r_ref[1].astype(f32)) + r_ref[2].astype(f32)

    if kind == "rows":
        C = q32.shape[1]
        grid = (1,)
        qs = pl.BlockSpec((SHARD_ROWS, C), lambda i, jc_ref: (jc_ref[0], 0))
        rs = pl.BlockSpec((3, SHARD_ROWS, C), lambda i, jc_ref: (0, 0, 0))
        os_ = pl.BlockSpec((SHARD_ROWS, C), lambda i, jc_ref: (0, jc_ref[1]))
        full_shape = (D_MODEL, 2 * C)
    else:
        C = q32.shape[2]
        grid = (2,)
        qs = pl.BlockSpec((None, HALF_ROWS // 2, C), lambda i, jc_ref: (jc_ref[0], i, 0))
        rs = pl.BlockSpec((3, HALF_ROWS // 2, C), lambda i, jc_ref: (0, i, 0))
        os_ = pl.BlockSpec((HALF_ROWS // 2, C), lambda i, jc_ref: (2 * jc_ref[1] + i, 0))
        full_shape = (4, D_MODEL, C)
    return _ew_call(body, name, grid, jc, [q32, r], [qs, rs], jax.ShapeDtypeStruct(_shard_shape(full_shape, kind), f32), os_)


def _adamw(w, g, m, v, name):
    R, C = w.shape
    br = _rows_block(R, mult=SUB, cap=512)
    c1 =1.0 / (1.0 - ADAM_B1 ** ADAM_STEP)
    c2 = 1.0 / (1.0 - ADAM_B2 ** ADAM_STEP)

    def body(w_ref, g_ref, m_ref, v_ref, d_ref, nm_ref, nv_ref):
        g_ = g_ref[...]
        nm = ADAM_B1 * m_ref[...] + (1.0 - ADAM_B1) * g_
        nv = ADAM_B2 * v_ref[...] + (1.0 - ADAM_B2) * (g_ * g_)
        nm_ref[...] = nm
        nv_ref[...] = nv
        d_ref[...] = -ADAM_LR * ((nm * c1) / (jnp.sqrt(nv * c2) + ADAM_EPS) + ADAM_WD * w_ref[...])

    spec = pl.BlockSpec((br, C), lambda i: (i, 0))
    sh = jax.ShapeDtypeStruct((R, C), f32)
    return pl.pallas_call(body, name=name, grid=(R // br,), in_specs=[spec] * 4, out_specs=[spec] * 3,
                          out_shape=[sh, sh, sh], compiler_params=_params("parallel"))(w, g, m, v)


def _small_finish(dbacc, p_soft, dlb):
    def body(db_ref, p_ref, dlb_ref, dbs_ref, dlg_ref):
        lane = lax.broadcasted_iota(jnp.int32, (HD, HD), 1)
        acc = jnp.zeros((HD, HD), f32)
        for h in range(NH):
            acc = acc + jnp.where(lane == h, jnp.sum(db_ref[h], axis=-1, keepdims=True), 0.0)
        dbs_ref[...] = acc
        p = p_ref[...]
        p1 = p[1:2, :]
        rowi = lax.broadcasted_iota(jnp.int32, p.shape, 0)
        dlg_ref[...] = dlb_ref[0:1, :] * p1 * (jnp.where(rowi == 1, 1.0, 0.0) - p)

    vm = pl.BlockSpec(memory_space=pltpu.VMEM)
    return pl.pallas_call(body, name="small_finish", in_specs=[vm] * 3, out_specs=[vm] * 2,
                          out_shape=[jax.ShapeDtypeStruct((HD, HD), f32), jax.ShapeDtypeStruct(p_soft.shape, f32)])(dbacc, p_soft, dlb)


def _where_am_i():
    return lax.axis_index("x"), lax.axis_index("y"), lax.axis_index("c")


MAX_PIECES = 8


def _nchunks(rows, mult):
    for n in range(MAX_PIECES, 0, -1):
        if rows % (n * mult) == 0:
            return n
    return 1


def _leading_pieces(src, dst):
    n = src.shape[0]
    if len(src.shape) >= 3 and n <= MAX_PIECES:
        return [(src.at[s], dst.at[s]) for s in range(n)]
    return [(src, dst)]


def _ag_weights(shards, kinds, jshard):
    n = len(shards)

    def body(*refs):
        sh_refs, out_refs = refs[:n], refs[2 * n:3 * n]
        send_sems, recv_sems = refs[3 * n:]
        x, y, c = _where_am_i()
        j = 2 * x + y
        sib = (x, y, 1 - c)
        chips = [(1 - x, y), (x, 1 - y), (1 - x, 1 - y)]

        def cp(k, src, dst, to):
            return pltpu.make_async_remote_copy(src_ref=src, dst_ref=dst, send_sem=send_sems.at[k], recv_sem=recv_sems.at[k],
                                                device_id=to, device_id_type=MESH)

        started = []
        for a in range(n):
            for k, (cx, cy) in enumerate(chips):
                d = cp(6 * a + k, _half_of_shard(sh_refs[a], kinds[a], c), _half_of_shard(out_refs[a].at[j], kinds[a], c), (cx, cy, c))
                d.start()
                started.append(d)
        for a in range(n):
            for k, (cx, cy) in enumerate(chips):
                blk = _half_of_shard(out_refs[a].at[2 * cx + cy], kinds[a], c)
                cp(6 * a + k, blk, blk, (cx, cy, c)).wait_recv()
                d = cp(6 * a + 3 + k, blk, blk, sib)
                d.start()
                started.append(d)
        for a in range(n):
            for k, (cx, cy) in enumerate(chips):
                blk = _half_of_shard(out_refs[a].at[2 * cx + cy], kinds[a], 1 - c)
                cp(6 * a + 3 + k, blk, blk, sib).wait_recv()
        for d in started:
            d.wait_send()

    placed = [lax.dynamic_update_slice(jnp.zeros((4,) + s.shape, s.dtype), s[None], (jshard,) + (0,) * s.ndim) for s in shards]
    anyspec = pl.BlockSpec(memory_space=pl.ANY)
    return pl.pallas_call(body, name="all_gather_weights", in_specs=[anyspec] * (2 * n), out_specs=[anyspec] * n,
                          out_shape=[jax.ShapeDtypeStruct(p.shape, p.dtype) for p in placed],
                          input_output_aliases={n + a: a for a in range(n)},
                          scratch_shapes=[pltpu.SemaphoreType.DMA((6 * n,)), pltpu.SemaphoreType.DMA((6 * n,))],
                          compiler_params=pltpu.CompilerParams(has_side_effects=True))(*shards, *placed)


_HBM = pl.BlockSpec(memory_space=pltpu.HBM)
_SEM = pl.BlockSpec(memory_space=pltpu.SEMAPHORE)
_FLOWING = pltpu.SideEffectType.DATAFLOW_SIDE_EFFECTING


def _peers6(x, y, c):
    chips = [(1 - x, y), (x, 1 - y), (1 - x, 1 - y)]
    return [(2 * k + e, chip, c if e == 0 else 1 - c) for k, chip in enumerate(chips) for e in range(2)]


def _ag_start(shards, jshard, name, after=None):
    n = len(shards)

    def body(*refs):
        out_refs = refs[2 * n:4 * n]
        send_sems, recv_sems, token = refs[4 * n:]
        x, y, c = _where_am_i()
        j = 2 * x + y
        for a in range(n):
            for slot, (cx, cy), tc in _peers6(x, y, c):
                pltpu.make_async_remote_copy(src_ref=_half_of_shard(out_refs[a], "win", c),
                                             dst_ref=_half_of_shard(out_refs[n + a].at[j], "win", c),
                                             send_sem=send_sems.at[6 * a + slot], recv_sem=recv_sems.at[6 * a + slot],
                                             device_id=(cx, cy, tc), device_id_type=MESH).start()
        token[...] = jnp.zeros_like(token)

    fill = jnp.zeros((), f32) if after is None else after[0, 0]
    placed = [lax.dynamic_update_slice(jnp.broadcast_to(fill.astype(s.dtype), (4,) + s.shape), s[None], (jshard,) + (0,) * s.ndim)
              for s in shards]
    hbm = lambda t: pltpu.with_memory_space_constraint(t, pltpu.HBM)
    both = list(shards) + placed
    outs = pl.pallas_call(
        body, name=name, in_specs=[_HBM] * (2 * n), out_specs=[_HBM] * (2 * n) + [_SEM, _SEM, pl.BlockSpec(memory_space=pltpu.VMEM)],
        out_shape=[pltpu.HBM(p.shape, p.dtype) for p in both] + [pltpu.SemaphoreType.DMA((6 * n,)), pltpu.SemaphoreType.DMA((6 * n,)),
                                                                jax.ShapeDtypeStruct((SUB, LANE), f32)],
        input_output_aliases={a: a for a in range(2 * n)},
        compiler_params=pltpu.CompilerParams(has_side_effects=_FLOWING))(*[hbm(t) for t in both])
    return outs[:2 * n], outs[2 * n], outs[2 * n + 1], outs[2 * n + 2]


def _ag_wait(bufs, send_sems, recv_sems, after, name):
    n = len(bufs) // 2

    def body(*refs):
        sh_refs, g_refs = refs[:n], refs[n:2 * n]
        send_sems, recv_sems = refs[2 * n], refs[2 * n + 1]
        x, y, c = _where_am_i()
        for a in range(n):
            for slot, (cx, cy), tc in _peers6(x, y, c):
                cp = pltpu.make_async_remote_copy(src_ref=_half_of_shard(sh_refs[a], "win", c),
                                                  dst_ref=_half_of_shard(g_refs[a].at[2 * cx + cy], "win", tc),
                                                  send_sem=send_sems.at[6 * a + slot], recv_sem=recv_sems.at[6 * a + slot],
                                                  device_id=(cx, cy, tc), device_id_type=MESH)
                cp.wait_send()
                cp.wait_recv()

    outs = pl.pallas_call(body, name=name, in_specs=[_HBM] * (2 * n) + [_SEM, _SEM, pl.BlockSpec(memory_space=pl.ANY)],
                          out_specs=[_HBM] * (2 * n), out_shape=[pltpu.HBM(b.shape, b.dtype) for b in bufs],
                          input_output_aliases={a: a for a in range(2 * n)},
                          compiler_params=pltpu.CompilerParams(has_side_effects=_FLOWING))(*bufs, send_sems, recv_sems, after)
    return outs[n:]


def _rs_swap(parts, kinds, name):
    n = len(parts)

    def body(*refs):
        p_refs, got_refs = refs[:n], refs[n:2 * n]
        send_sems, recv_sems = refs[2 * n:]
        x, y, c = _where_am_i()

        def cp(a, src, dst):
            return pltpu.make_async_remote_copy(src_ref=src, dst_ref=dst, send_sem=send_sems.at[a], recv_sem=recv_sems.at[a],
                                                device_id=(x, y, 1 - c), device_id_type=MESH)

        for a in range(n):
            for src, dst in _leading_pieces(_half_of_full(p_refs[a], kinds[a], 1 - c), got_refs[a]):
                cp(a, src, dst).start()
        for a in range(n):
            cp(a, got_refs[a], got_refs[a]).wait()

    anyspec = pl.BlockSpec(memory_space=pl.ANY)
    return pl.pallas_call(body, name=name, in_specs=[anyspec] * n, out_specs=[anyspec] * n,
                          out_shape=[jax.ShapeDtypeStruct(_half_shape(p.shape, k), p.dtype) for p, k in zip(parts, kinds)],
                          scratch_shapes=[pltpu.SemaphoreType.DMA((n,)), pltpu.SemaphoreType.DMA((n,))],
                          compiler_params=pltpu.CompilerParams(has_side_effects=True))(*parts)


def _rs_owners(qbs, kinds, full_shapes):
    n = len(qbs)

    def body(*refs):
        q_refs, got_refs = refs[:n], refs[n:2 * n]
        send_sems, recv_sems = refs[2 * n:]
        x, y, c = _where_am_i()
        chips = [(1 - x, y), (x, 1 - y), (1 - x, 1 - y)]
        ds = []
        for a in range(n):
            for k, (cx, cy) in enumerate(chips):
                d = pltpu.make_async_remote_copy(src_ref=_shard_of_half(q_refs[a], kinds[a], 2 * cx + cy), dst_ref=got_refs[a].at[k],
                                                 send_sem=send_sems.at[3 * a + k], recv_sem=recv_sems.at[3 * a + k],
                                                 device_id=(cx, cy, c), device_id_type=MESH)
                d.start()
                ds.append(d)
        for d in ds:
            d.wait()

    anyspec = pl.BlockSpec(memory_space=pl.ANY)
    return pl.pallas_call(body, name="rs_to_owners", in_specs=[anyspec] * n, out_specs=[anyspec] * n,
                          out_shape=[jax.ShapeDtypeStruct((3,) + _shard_half_shape(fs, k), bf16) for fs, k in zip(full_shapes, kinds)],
                          scratch_shapes=[pltpu.SemaphoreType.DMA((3 * n,)), pltpu.SemaphoreType.DMA((3 * n,))],
                          compiler_params=pltpu.CompilerParams(has_side_effects=True))(*qbs)


def _rs_owners_start(qbs, kinds, full_shapes, name):
    n = len(qbs)

    def body(*refs):
        q_refs, got_refs = refs[2 * n:3 * n], refs[3 * n:4 * n]
        send_sems, recv_sems, token = refs[4 * n:]
        x, y, c = _where_am_i()
        for a in range(n):
            for k, (cx, cy) in enumerate([(1 - x, y), (x, 1 - y), (1 - x, 1 - y)]):
                pltpu.make_async_remote_copy(src_ref=_shard_of_half(q_refs[a], kinds[a], 2 * cx + cy), dst_ref=got_refs[a].at[k],
                                             send_sem=send_sems.at[3 * a + k], recv_sem=recv_sems.at[3 * a + k],
                                             device_id=(cx, cy, c), device_id_type=MESH).start()
        token[...] = jnp.zeros_like(token)

    hbm = lambda t: pltpu.with_memory_space_constraint(t, pltpu.HBM)
    lands = [lax.empty((3,) + _shard_half_shape(fs, k), bf16) for fs, k in zip(full_shapes, kinds)]
    both = list(qbs) + lands
    outs = pl.pallas_call(
        body, name=name, in_specs=[_HBM] * (2 * n), out_specs=[_HBM] * (2 * n) + [_SEM, _SEM, pl.BlockSpec(memory_space=pltpu.VMEM)],
        out_shape=[pltpu.HBM(t.shape, t.dtype) for t in both] + [pltpu.SemaphoreType.DMA((3 * n,)), pltpu.SemaphoreType.DMA((3 * n,)),
                                                                jax.ShapeDtypeStruct((SUB, LANE), f32)],
        input_output_aliases={a: a for a in range(2 * n)},
        compiler_params=pltpu.CompilerParams(has_side_effects=_FLOWING))(*[hbm(t) for t in both])
    return outs[:2 * n], outs[2 * n], outs[2 * n + 1], outs[2 * n + 2]


def _rs_owners_wait(bufs, send_sems, recv_sems, kinds, after, name):
    n = len(bufs) // 2

    def body(*refs):
        q_refs, got_refs = refs[:n], refs[n:2 * n]
        send_sems, recv_sems = refs[2 * n], refs[2 * n + 1]
        x, y, c = _where_am_i()
        for a in range(n):
            for k, (cx, cy) in enumerate([(1 - x, y), (x, 1 - y), (1 - x, 1 - y)]):
                cp = pltpu.make_async_remote_copy(src_ref=_shard_of_half(q_refs[a], kinds[a], 2 * cx + cy), dst_ref=got_refs[a].at[k],
                                                  send_sem=send_sems.at[3 * a + k], recv_sem=recv_sems.at[3 * a + k],
                                                  device_id=(cx, cy, c), device_id_type=MESH)
                cp.wait_send()
                cp.wait_recv()

    outs = pl.pallas_call(body, name=name, in_specs=[_HBM] * (2 * n) + [_SEM, _SEM, pl.BlockSpec(memory_space=pl.ANY)],
                          out_specs=[_HBM] * (2 * n), out_shape=[pltpu.HBM(b.shape, b.dtype) for b in bufs],
                          input_output_aliases={a: a for a in range(2 * n)},
                          compiler_params=pltpu.CompilerParams(has_side_effects=_FLOWING))(*bufs, send_sems, recv_sems, after)
    return outs[n:]


def _rs_join(bufs, kinds):
    n = len(bufs)

    def body(*refs):
        out_refs = refs[n:2 * n]
        send_sems, recv_sems = refs[2 * n:]
        x, y, c = _where_am_i()

        def cp(a, h):
            blk = _half_of_shard(out_refs[a], kinds[a], h)
            return pltpu.make_async_remote_copy(src_ref=blk, dst_ref=blk, send_sem=send_sems.at[a], recv_sem=recv_sems.at[a],
                                                device_id=(x, y, 1 - c), device_id_type=MESH)

        for a in range(n):
            cp(a, c).start()
        for a in range(n):
            cp(a, c).wait_send()
            cp(a, 1 - c).wait_recv()

    anyspec = pl.BlockSpec(memory_space=pl.ANY)
    return pl.pallas_call(body, name="rs_join_halves", in_specs=[anyspec] * n, out_specs=[anyspec] * n,
                          out_shape=[jax.ShapeDtypeStruct(b.shape, b.dtype) for b in bufs],
                          input_output_aliases={a: a for a in range(n)},
                          scratch_shapes=[pltpu.SemaphoreType.DMA((n,)), pltpu.SemaphoreType.DMA((n,))],
                          compiler_params=pltpu.CompilerParams(has_side_effects=True))(*bufs)


def _all_reduce_small(g):
    R, C = g.shape
    H = R // 2
    NP = _nchunks(H, SUB)
    PR = H // NP

    def body(g_ref, out_ref, sib_ref, chip_ref, send_sems, recv_sems):
        x, y, c = _where_am_i()
        j = 2 * x + y
        sib = (x, y, 1 - c)
        chips = [(1 - x, y), (x, 1 - y), (1 - x, 1 - y)]
        rows = pl.ds(pl.multiple_of(c * H, SUB), H)

        def cp(k, src, dst, to):
            return pltpu.make_async_remote_copy(src_ref=src, dst_ref=dst, send_sem=send_sems.at[k], recv_sem=recv_sems.at[k],
                                                device_id=to, device_id_type=MESH)

        def pieces(k, src, dst, to):
            for q in range(NP):
                cp(k, src.at[pl.ds(q * PR, PR)], dst.at[pl.ds(q * PR, PR)], to).start()

        for half in range(2):
            pieces(0, g_ref.at[pl.ds(half * H, H)], sib_ref.at[pl.ds(half * H, H)], sib)
        cp(0, g_ref, sib_ref, sib).wait()
        chip_ref[j] = g_ref[rows, :] + sib_ref[rows, :]
        for k, (cx, cy) in enumerate(chips):
            pieces(1 + k, chip_ref.at[j], chip_ref.at[j], (cx, cy, c))
        for k, (cx, cy) in enumerate(chips):
            blk = chip_ref.at[2 * cx + cy]
            cp(1 + k, blk, blk, (cx, cy, c)).wait()
        out_ref[rows, :] = ((chip_ref[0] + chip_ref[1]) + chip_ref[2]) + chip_ref[3]
        other = out_ref.at[pl.ds(pl.multiple_of((1 - c) * H, SUB), H)]
        pieces(4, out_ref.at[rows], out_ref.at[rows], sib)
        cp(4, other, other, sib).wait()

    vm = pl.BlockSpec(memory_space=pltpu.VMEM)
    return pl.pallas_call(body, name="all_reduce_small", in_specs=[vm], out_specs=vm,
                          out_shape=jax.ShapeDtypeStruct((R, C), f32),
                          scratch_shapes=[pltpu.VMEM((R, C), f32), pltpu.VMEM((4, H, C), f32),
                                          pltpu.SemaphoreType.DMA((5,)), pltpu.SemaphoreType.DMA((5,))],
                          compiler_params=pltpu.CompilerParams(has_side_effects=True, vmem_limit_bytes=VMEM_LIMIT))(g)


SPLIT_MIN_ELEMS = 1 << 16


def _all_reduce_many(gs):
    n = len(gs)
    split = [g.ndim == 3 and g.shape[0] % 2 == 0 and g.size >= SPLIT_MIN_ELEMS for g in gs]
    part_shape = [((g.shape[0] // 2,) + g.shape[1:]) if s else g.shape for g, s in zip(gs, split)]
    n_split = sum(split)

    def body(*refs):
        g, out, sibs, chipb = refs[:n], refs[n:2 * n], refs[2 * n:3 * n], refs[3 * n:4 * n]
        send_sems, recv_sems = refs[4 * n:]
        x, y, c = _where_am_i()
        j = 2 * x + y
        sib = (x, y, 1 - c)
        chips = [(1 - x, y), (x, 1 - y), (1 - x, 1 - y)]

        def cp(k, src, dst, to):
            return pltpu.make_async_remote_copy(src_ref=src, dst_ref=dst, send_sem=send_sems.at[k], recv_sem=recv_sems.at[k],
                                                device_id=to, device_id_type=MESH)

        def part(a, h):
            return pl.ds(h * part_shape[a][0], part_shape[a][0]) if split[a] else Ellipsis

        def mine(ref, a, h):
            return ref.at[part(a, h)] if split[a] else ref

        swaps = [cp(a, g[a], sibs[a], sib) for a in range(n)]
        for d in swaps:
            d.start()
        for a in range(n):
            swaps[a].wait()
            chipb[a][j] = g[a][part(a, c)] + sibs[a][part(a, c)]
        sends = [cp(n + 3 * a + k, chipb[a].at[j], chipb[a].at[j], (cx, cy, c)) for a in range(n) for k, (cx, cy) in enumerate(chips)]
        for d in sends:
            d.start()
        for a in range(n):
            for k, (cx, cy) in enumerate(chips):
                blk = chipb[a].at[2 * cx + cy]
                cp(n + 3 * a + k, blk, blk, (cx, cy, c)).wait_recv()
            out[a][part(a, c)] = ((chipb[a][0] + chipb[a][1]) + chipb[a][2]) + chipb[a][3]
        for d in sends:
            d.wait_send()
        backs = [(a, cp(4 * n + i, mine(out[a], a, c), mine(out[a], a, c), sib)) for i, a in enumerate([a for a in range(n) if split[a]])]
        for _, d in backs:
            d.start()
        for i, (a, d) in enumerate(backs):
            d.wait_send()
            cp(4 * n + i, mine(out[a], a, 1 - c), mine(out[a], a, 1 - c), sib).wait_recv()

    vm = pl.BlockSpec(memory_space=pltpu.VMEM)
    nsem = 4 * n + n_split
    return pl.pallas_call(body, name="all_reduce_small_grads", in_specs=[vm] * n, out_specs=[vm] * n,
                          out_shape=[jax.ShapeDtypeStruct(g.shape, f32) for g in gs],
                          scratch_shapes=([pltpu.VMEM(g.shape, f32) for g in gs] + [pltpu.VMEM((4,) + ps, f32) for ps in part_shape]
                                          + [pltpu.SemaphoreType.DMA((nsem,)), pltpu.SemaphoreType.DMA((nsem,))]),
                          compiler_params=pltpu.CompilerParams(has_side_effects=True, vmem_limit_bytes=VMEM_LIMIT))(*gs)


def _adamw_many(ws, gs, ms, vs, name):
    n = len(ws)
    c1 = 1.0 / (1.0 - ADAM_B1 ** ADAM_STEP)
    c2 = 1.0 / (1.0 - ADAM_B2 ** ADAM_STEP)

    def body(*refs):
        for a in range(n):
            w_ref, g_ref, m_ref, v_ref, d_ref, nm_ref, nv_ref = (refs[i * n + a] for i in range(7))
            g_ = g_ref[...]
            nm = ADAM_B1 * m_ref[...] + (1.0 - ADAM_B1) * g_
            nv = ADAM_B2 * v_ref[...] + (1.0 - ADAM_B2) * (g_ * g_)
            nm_ref[...] = nm
            nv_ref[...] = nv
            d_ref[...] = -ADAM_LR * ((nm * c1) / (jnp.sqrt(nv * c2) + ADAM_EPS) + ADAM_WD * w_ref[...])

    vm = pl.BlockSpec(memory_space=pltpu.VMEM)
    sh = [jax.ShapeDtypeStruct(w.shape, f32) for w in ws]
    outs = pl.pallas_call(body, name=name, in_specs=[vm] * (4 * n), out_specs=[vm] * (3 * n), out_shape=sh * 3,
                          compiler_params=pltpu.CompilerParams(vmem_limit_bytes=VMEM_LIMIT))(*ws, *gs, *ms, *vs)
    return outs[:n], outs[n:2 * n], outs[2 * n:]


def _pack_flat(arrs, rows_mult):
    flat = jnp.concatenate([a.reshape(-1) for a in arrs])
    n = flat.shape[0]
    tot = -(-n // (rows_mult * LANE)) * rows_mult * LANE
    return jnp.pad(flat, (0, tot - n)).reshape(-1, LANE)


def _unpack_flat(buf, shapes):
    flat = buf.reshape(-1)
    out, o = [], 0
    for s in shapes:
        n = math.prod(s)
        out.append(flat[o:o + n].reshape(s))
        o += n
    return out


_BIG = ("mem_kv_w", "w_out", "a_w_in", "b_w_in", "c_w_in", "d_w_in")
SMALL_ROWS_MULT = 256


def _row8(v):
    v = v.reshape(-1, v.shape[-1])
    return jnp.pad(v, ((0, SUB - v.shape[0]), (0, 0)))


def kernel(x, mem, mem_kv_w, ln_g, ln_b, w_out, hgrn_lb_logits, a_w_in, a_w_s, a_b_s, b_w_in, b_norm_g, c_w_in, c_w_pool, c_scale, d_w_in, d_conv_w, d_conv_b, d_w_gx, d_b_gx, d_w_ga, d_b_ga, d_a_param, loss_target, m_mem_kv_w, m_ln_g, m_ln_b, m_w_out, m_hgrn_lb_logits, m_a_w_in, m_a_w_s, m_a_b_s, m_b_w_in, m_b_norm_g, m_c_w_in, m_c_w_pool, m_c_scale, m_d_w_in, m_d_conv_w, m_d_conv_b, m_d_w_gx, m_d_b_gx, m_d_w_ga, m_d_b_ga, m_d_a_param, v_mem_kv_w, v_ln_g, v_ln_b, v_w_out, v_hgrn_lb_logits, v_a_w_in, v_a_w_s, v_a_b_s, v_b_w_in, v_b_norm_g, v_c_w_in, v_c_w_pool, v_c_scale, v_d_w_in, v_d_conv_w, v_d_conv_b, v_d_w_gx, v_d_b_gx, v_d_w_ga, v_d_b_ga, v_d_a_param):
    names = ["mem_kv_w", "ln_g", "ln_b", "w_out", "hgrn_lb_logits", "a_w_in", "a_w_s", "a_b_s", "b_w_in", "b_norm_g", "c_w_in",
             "c_w_pool", "c_scale", "d_w_in", "d_conv_w", "d_conv_b", "d_w_gx", "d_b_gx", "d_w_ga", "d_b_ga", "d_a_param"]
    w = dict(mem_kv_w=mem_kv_w, ln_g=ln_g, ln_b=ln_b, w_out=w_out, hgrn_lb_logits=hgrn_lb_logits, a_w_in=a_w_in, a_w_s=a_w_s,
             a_b_s=a_b_s, b_w_in=b_w_in, b_norm_g=b_norm_g, c_w_in=c_w_in, c_w_pool=c_w_pool, c_scale=c_scale, d_w_in=d_w_in,
             d_conv_w=d_conv_w, d_conv_b=d_conv_b, d_w_gx=d_w_gx, d_b_gx=d_b_gx, d_w_ga=d_w_ga, d_b_ga=d_b_ga, d_a_param=d_a_param)
    m = dict(zip(names, [m_mem_kv_w, m_ln_g, m_ln_b, m_w_out, m_hgrn_lb_logits, m_a_w_in, m_a_w_s, m_a_b_s, m_b_w_in, m_b_norm_g,
                         m_c_w_in, m_c_w_pool, m_c_scale, m_d_w_in, m_d_conv_w, m_d_conv_b, m_d_w_gx, m_d_b_gx, m_d_w_ga,
                         m_d_b_ga, m_d_a_param]))
    v = dict(zip(names, [v_mem_kv_w, v_ln_g, v_ln_b, v_w_out, v_hgrn_lb_logits, v_a_w_in, v_a_w_s, v_a_b_s, v_b_w_in, v_b_norm_g,
                         v_c_w_in, v_c_w_pool, v_c_scale, v_d_w_in, v_d_conv_w, v_d_conv_b, v_d_w_gx, v_d_b_gx, v_d_w_ga,
                         v_d_b_ga, v_d_a_param]))
    xi, yi = lax.axis_index("x"), lax.axis_index("y")
    jshard = 2 * xi + yi
    x2 = x[0]
    mem2 = mem[0]
    tgt2 = loss_target[0]

    w_in_sh = [w[n][0].astype(bf16) for n in _BIG[2:]]
    w_out_sh = w_out.astype(bf16)
    gath0 = _ag_weights([mem_kv_w.astype(bf16), w_out_sh[0], w_in_sh[0]], ("rows", "win", "win"), jshard)
    w_kv = gath0[0].reshape(D_MODEL, 2 * XW)
    pending = [None]
    tie = (lax.bitcast_convert_type(gath0[0][0:1, 0:1, 0], jnp.uint16) & 0).astype(f32)
    for l in range(1, DEPTH):
        bufs, ssem, rsem, tie = _ag_start([w_in_sh[l], w_out_sh[l]], jshard, f"gather_start{l}", tie)
        pending.append((bufs, ssem, rsem))
    tied_gain = {0: ln_g[0:1] + tie[0:1, 0:1]}

    def layer_weights(g_in, g_out):
        return (g_in.transpose(1, 0, 2).reshape(D_MODEL, -1), g_in.transpose(0, 2, 1).reshape(-1, D_MODEL),
                g_out.reshape(D_MODEL, D_MODEL), g_out.transpose(2, 0, 1).reshape(D_MODEL, D_MODEL))

    lw = [layer_weights(gath0[2], gath0[1])]

    def gather_small(shard):
        z = jnp.zeros((4, POOL_GROUP), f32)
        return lax.dynamic_update_slice(z, shard.reshape(1, POOL_GROUP), (jshard, 0))

    sm_sh = jnp.concatenate([gather_small(b_norm_g), gather_small(c_scale), gather_small(d_conv_b), gather_small(d_a_param)]
                            + [gather_small(d_conv_w[:, r]) for r in range(4)], axis=0)
    ci = lax.axis_index("c")
    sm_all = _all_reduce_small(_pack_flat([jnp.where(ci == 0, sm_sh, 0.0)], SUB * 2))
    sm = _unpack_flat(sm_all, [(8, 4 * POOL_GROUP)])[0]
    ng_full, scale_full, convb_full, ap_full = sm[0:1], sm[1:2], sm[2:3], sm[3:4]
    convw_full = sm[4:8]

    tril = jnp.tril(jnp.ones((HD, HD), bool))
    wtri = jnp.where(tril, a_w_s[0], 0.0)
    wbd = jnp.zeros((TOK, TOK), f32)
    for g in range(4):
        wbd = lax.dynamic_update_slice(wbd, c_w_pool[0, g], (g * POOL_GROUP, g * POOL_GROUP))
    kh, khT, vh, vhT, p_soft = _prep(mem2, w_kv, hgrn_lb_logits)
    prm = [
        dict(wtri=wtri.astype(bf16), wtriT=wtri.transpose(0, 2, 1).astype(bf16),
             bcolb=jnp.broadcast_to(a_b_s[0][:, :, None], (NH, HD, HD))),
        dict(lb=p_soft[1:2], ng=ng_full),
        dict(wbd=wbd.astype(bf16), wbdT=wbd.T.astype(bf16), scale=scale_full),
        dict(cw=_row8(convw_full), cb=convb_full, wgx=d_w_gx[0].astype(bf16), wgxT=d_w_gx[0].transpose(0, 2, 1).astype(bf16),
             bgx=d_b_gx.reshape(1, TOK), wga=d_w_ga[0].astype(bf16), wgaT=d_w_ga[0].transpose(0, 2, 1).astype(bf16),
             bga=d_b_ga.reshape(1, TOK), ap=ap_full),
    ]

    acts = []
    h = x2
    for l in range(DEPTH):
        if l:
            bufs, ssem, rsem = pending[l]
            lw.append(layer_weights(*_ag_wait(bufs, ssem, rsem, h, f"gather_wait{l}")))
        outs = _fwd_layer(l, h, lw[l][0], lw[l][2], tied_gain.get(l, ln_g[l:l + 1]), ln_b[l:l + 1], khT, vh, prm[l],
                          tgt2 if l == DEPTH - 1 else None)
        nfix = 4 if l == DEPTH - 1 else 3
        acts.append(dict(xin=h, proj=outs[1], z=outs[2], saves=outs[nfix:]))
        if l == DEPTH - 1:
            loss_part = outs[3]
        h = outs[0]
    loss = lax.psum(0.5 / D_MODEL * jnp.sum(loss_part), ("x", "y", "c"))

    dh = h
    dln = [None] * DEPTH
    dks, dvs = [None] * DEPTH, [None] * DEPTH
    sgr = [None] * DEPTH
    jc = jnp.stack([jshard, ci]).astype(jnp.int32)
    lkinds = ("win", "rows")
    q32s, flying, back_gain = [None] * DEPTH, [None] * DEPTH, {}
    for l in reversed(range(DEPTH)):
        a = acts[l]
        (dxin, dproj, mixedb, dyb, dln[l], dks[l], dvs[l]), sgr[l] = _bwd_layer(
            l, dh, a["z"], a["proj"], lw[l][1], lw[l][3], back_gain.get(l, ln_g[l:l + 1]), kh, khT, vh, vhT, prm[l], a["saves"])
        if _OFFS[l]["W"] // 4 % LANE:
            gw_in = _tn_gemm(a["xin"], dproj, f"grad_w_in{l}", 1).reshape(D_MODEL, 4, -1).transpose(1, 0, 2)
        else:
            gw_in = _tn_gemm_sharded(a["xin"], dproj, f"grad_w_in{l}")
        parts = [gw_in, _tn_gemm(mixedb, dyb, f"grad_w_out{l}", 1)]
        lk = lkinds
        if l == 0:
            parts.append(_kv_bwd(mem2, dks, dvs))
            lk = lkinds + ("rows",)
        gots = _rs_swap(parts, lk, f"rs_swap_halves{l}")
        sums = [_add_sibling(p, g, k, jc, f"rs_add_sibling{l}_{i}") for i, (p, g, k) in enumerate(zip(parts, gots, lk))]
        q32s[l] = [s[0] for s in sums]
        shapes = [p.shape for p in parts]
        if l:
            bufs, ssem, rsem, tok = _rs_owners_start([s[1] for s in sums], lk, shapes, f"rs_owners_start{l}")
            flying[l] = (bufs, ssem, rsem)
            back_gain[l - 1] = ln_g[l - 1:l] + tok[0:1, 0:1]
        else:
            last_got = _rs_owners([s[1] for s in sums], lk, shapes)
        dh = dxin
    grad_x = dh[None]
    fin, fin_kinds = {}, []
    for l in range(DEPTH):
        lk = lkinds + (("rows",) if l == 0 else ())
        got = last_got if l == 0 else _rs_owners_wait(*flying[l], lk, grad_x, f"rs_owners_wait{l}")
        fin[l] = [_add_chips(q, r, k, jc, f"rs_add_chips{l}_{i}") for i, (q, r, k) in enumerate(zip(q32s[l], got, lk))]
        fin_kinds += list(lk)
    joined = _rs_join([t for l in range(DEPTH) for t in fin[l]], tuple(fin_kinds))
    by_layer, o = [], 0
    for l in range(DEPTH):
        by_layer.append(joined[o:o + len(fin[l])])
        o += len(fin[l])
    gbig = {"mem_kv_w": by_layer[0][2], "w_out": jnp.stack([by_layer[l][1] for l in range(DEPTH)])}
    for l, n in enumerate(_BIG[2:]):
        gbig[n] = by_layer[l][0]
    g_sh, d_sh, m_sh, v_sh = {}, {}, {}, {}
    for n in _BIG:
        as2d = lambda t: t.reshape(-1, t.shape[-1])
        upd = _adamw(as2d(w[n]), as2d(gbig[n]), as2d(m[n]), as2d(v[n]), f"adamw_{n}")
        g_sh[n] = gbig[n].reshape(w[n].shape)
        d_sh[n], m_sh[n], v_sh[n] = (u.reshape(w[n].shape) for u in upd)

    dbs, dlogits = _small_finish(sgr[0]["dbacc"], p_soft, sgr[1]["dlb"])
    gs = {
        "ln_g": jnp.concatenate([dln[l][0:1] for l in range(DEPTH)], axis=0),
        "ln_b": jnp.concatenate([dln[l][1:2] for l in range(DEPTH)], axis=0),
        "hgrn_lb_logits": dlogits,
        "a_w_s": sgr[0]["dwtri"][None],
        "a_b_s": dbs[:, 0:NH].T[None],
        "b_norm_g": sgr[1]["dng"][0:1],
        "c_w_pool": jnp.stack([sgr[2]["dwbd"][g * POOL_GROUP:(g + 1) * POOL_GROUP, g * POOL_GROUP:(g + 1) * POOL_GROUP]
                               for g in range(4)])[None],
        "c_scale": sgr[2]["dscale"][0:1],
        "d_conv_w": sgr[3]["dcw"][0:4][None],
        "d_conv_b": sgr[3]["dvec"][3:4],
        "d_w_gx": sgr[3]["dwgx"][None],
        "d_b_gx": sgr[3]["dvec"][1:2].reshape(1, NH, HD),
        "d_w_ga": sgr[3]["dwga"][None],
        "d_b_ga": sgr[3]["dvec"][2:3].reshape(1, NH, HD),
        "d_a_param": sgr[3]["dvec"][0:1],
    }
    small = [n for n in names if n not in _BIG]
    drop1 = lambda t: t.reshape(t.shape[1:]) if t.ndim > 2 and t.shape[0] == 1 else t
    gsum = dict(zip(small, _all_reduce_many([drop1(gs[n]) for n in small])))
    for n in ("b_norm_g", "c_scale", "d_conv_b", "d_a_param"):
        gsum[n] = lax.dynamic_slice(gsum[n], (0, jshard * POOL_GROUP), (1, POOL_GROUP))
    gsum["d_conv_w"] = lax.dynamic_slice(gsum["d_conv_w"], (0, jshard * POOL_GROUP), (4, POOL_GROUP))
    upd = _adamw_many(*[[drop1(d[n]) for n in small] for d in (w, gsum, m, v)], "adamw_small")
    gsum = {n: gsum[n].reshape(w[n].shape) for n in small}
    d_sm, m_sm, v_sm = ({n: u.reshape(w[n].shape) for n, u in zip(small, us)} for us in upd)

    grads = {**gsum, **g_sh}
    deltas = {**d_sm, **d_sh}
    new_m = {**m_sm, **m_sh}
    new_v = {**v_sm, **v_sh}
    return (loss, grad_x, *[grads[n] for n in names], *[deltas[n] for n in names], *[new_m[n] for n in names],
            *[new_v[n] for n in names])
```

```python
import functools
import math

import jax
import jax.numpy as jnp
from jax import lax
from jax.experimental import pallas as pl
from jax.experimental.pallas import tpu as pltpu

f32 = jnp.float32
bf16 = jnp.bfloat16
MM = bf16

D_MODEL = 1024
TOK = 768
XW = 256
XHEADS = 4
XDIM = 64
HD = 128
NH = TOK // HD
CHUNK = 16
POOL_GROUP = 192
DEPTH = 4
ALPHA = (2 * DEPTH) ** 0.25
LN_EPS = 1e-5
RMS_EPS = 1e-6
LRU_C = 8.0
ADAM_LR, ADAM_B1, ADAM_B2, ADAM_EPS, ADAM_WD, ADAM_STEP = 0.001, 0.9, 0.999, 1e-08, 0.01, 10

_TS = (256, 256, 256, 256)
HGRN_SUB = 128
TK = 512
SUB = 8
LANE = 128
VMEM_LIMIT = 58 * 1024 * 1024

_OFFS = (
    dict(u=0, v=768, qx=1536, gate=1792, W=2816),
    dict(q=0, f=768, i=1536, qx=2304, gate=2560, W=3584),
    dict(p=0, qx=768, gate=1024, W=2048),
    dict(xb=0, qx=768, gate=1024, W=2048),
)
_PRM = (
    ("wtri", "wtriT", "bcolb"),
    ("lb", "ng"),
    ("wbd", "wbdT", "scale"),
    ("cw", "cb", "wgx", "wgxT", "bgx", "wga", "wgaT", "bga", "ap"),
)
MESH = pl.DeviceIdType.MESH


def _mm(a, b):
    return jnp.dot(a.astype(MM), b.astype(MM), preferred_element_type=f32)


def _mm_nt(a, b):
    return lax.dot_general(a.astype(MM), b.astype(MM), (((1,), (1,)), ((), ())), preferred_element_type=f32)


def _mm_tn(a, b):
    return lax.dot_general(a.astype(MM), b.astype(MM), (((0,), (0,)), ((), ())), preferred_element_type=f32)


def _mm_sel(sel, b):
    s = sel.astype(bf16)
    hi = b.astype(bf16)
    lo = (b - hi.astype(f32)).astype(bf16)
    return jnp.dot(s, hi, preferred_element_type=f32) + jnp.dot(s, lo, preferred_element_type=f32)


def _sig(x):
    return jax.nn.sigmoid(x)


_GC = math.sqrt(2.0 / math.pi)


def _gelu(x):
    t = jnp.tanh(_GC * (x + 0.044715 * x * x * x))
    return 0.5 * x * (1.0 + t), t


def _gelu_grad(x, t):
    return 0.5 * (1.0 + t) + 0.5 * x * (1.0 - t * t) * _GC * (1.0 + 3.0 * 0.044715 * x * x)


def _rowsum(x):
    return jnp.sum(x, axis=0, keepdims=True)


def _lmean(x):
    return jnp.mean(x, axis=-1, keepdims=True)


def _ln(z):
    mu = _lmean(z)
    zc = z - mu
    rstd = lax.rsqrt(_lmean(zc * zc) + LN_EPS)
    return zc * rstd, rstd


def _ln_bwd(dxh, xhat, rstd):
    return rstd * (dxh - _lmean(dxh) - xhat * _lmean(dxh * xhat))


def _hs(h):
    return slice(h * HD, (h + 1) * HD)


def _expm1(x):
    small = x * (1.0 + x * 0.5 * (1.0 + x * (1.0 / 3.0) * (1.0 + x * 0.25 * (1.0 + x * 0.2 * (1.0 + x * (1.0 / 6.0))))))
    return jnp.where(jnp.abs(x) < 0.25, small, jnp.exp(x) - 1.0)


def _softplus(x):
    e = jnp.exp(-jnp.abs(x))
    l1p = jnp.where(e < 1e-4, e - 0.5 * e * e, jnp.log(1.0 + e))
    return jnp.maximum(x, 0.0) + l1p


def _scan_fwd(a, b):
    n = a.shape[0]
    row = lax.broadcasted_iota(jnp.int32, a.shape, 0)
    d = 1
    while d < n:
        if d % SUB:
            m = row >= d
            b = jnp.where(m, a * pltpu.roll(b, d, 0) + b, b)
            a = jnp.where(m, a * pltpu.roll(a, d, 0), a)
        else:
            b = a * jnp.concatenate([jnp.zeros((d,) + b.shape[1:], f32), b[:n - d]], axis=0) + b
            a = a * jnp.concatenate([jnp.ones((d,) + a.shape[1:], f32), a[:n - d]], axis=0)
        d *= 2
    return a, b


def _scan_bwd(a, b):
    n = a.shape[0]
    row = lax.broadcasted_iota(jnp.int32, a.shape, 0)
    d = 1
    while d < n:
        if d % SUB:
            m = row < n - d
            b = jnp.where(m, a * pltpu.roll(b, n - d, 0) + b, b)
            a = jnp.where(m, a * pltpu.roll(a, n - d, 0), a)
        else:
            b = a * jnp.concatenate([b[d:], jnp.zeros((d,) + b.shape[1:], f32)], axis=0) + b
            a = a * jnp.concatenate([a[d:], jnp.ones((d,) + a.shape[1:], f32)], axis=0)
        d *= 2
    return a, b


def _chunk_mats(n):
    r = lax.broadcasted_iota(jnp.int32, (n, n), 0)
    c = lax.broadcasted_iota(jnp.int32, (n, n), 1)
    same = (r // CHUNK) == (c // CHUNK)
    return same, jnp.logical_and(same, c <= r)


def _pool_w(shape):
    lane = lax.broadcasted_iota(jnp.int32, shape, 1)
    return jnp.where(lane < POOL_GROUP, 2, jnp.where(lane < 2 * POOL_GROUP, 4, jnp.where(lane < 3 * POOL_GROUP, 8, 16)))


def _pool_pick(r1, r2, r3, r4):
    lane = lax.broadcasted_iota(jnp.int32, r1.shape, 1)
    return jnp.where(lane < POOL_GROUP, r1, jnp.where(lane < 2 * POOL_GROUP, r2, jnp.where(lane < 3 * POOL_GROUP, r3, r4)))


def _const_spec(a):
    nd = a.ndim
    return pl.BlockSpec(a.shape, lambda i, _nd=nd: (0,) * _nd, pipeline_mode=pl.Buffered(1))


def _acc_spec(shape):
    nd = len(shape)
    return pl.BlockSpec(shape, lambda i, _nd=nd: (0,) * _nd)


def _params(sem="arbitrary"):
    return pltpu.CompilerParams(dimension_semantics=(sem,), vmem_limit_bytes=VMEM_LIMIT)


def _xattn_fwd(qx, khT_ref, vh_ref):
    xo = jnp.zeros((qx.shape[0], XW), f32)
    ps = []
    for h in range(XHEADS):
        s = _mm(qx, khT_ref[h]) * (XDIM ** -0.5)
        e = jnp.exp(s - jnp.max(s, axis=-1, keepdims=True))
        p = e / jnp.sum(e, axis=-1, keepdims=True)
        xo = xo + _mm(p, vh_ref[h])
        ps.append(p)
    return xo, ps


def _hgrn_parallel(q_raw, fl, lb):
    n = q_raw.shape[0]
    same, tri = _chunk_mats(n)
    sq = _sig(q_raw)
    qf = q_raw * sq
    sgm = _sig(fl)
    f = lb + (1.0 - lb) * sgm
    logf = jnp.log(f)
    k = 1.0 - f
    g = _mm_sel(tri, logf)
    gl = _mm_sel(same, logf)
    eg = jnp.exp(g)
    eng = jnp.exp(-g)
    ee = jnp.exp(gl - g)
    return dict(sq=sq, qf=qf, sgm=sgm, f=f, k=k, eg=eg, eng=eng, ee=ee, q_dec=qf * eg, k_inv=k * eng, k_end=k * ee,
                a=jnp.exp(gl))


def _hgrn_intra(q_dec, k_inv, v):
    n = q_dec.shape[0]
    _, tri = _chunk_mats(HD)
    outs = []
    for h in range(NH):
        blks = []
        for b in range(n // HD):
            rs = slice(b * HD, (b + 1) * HD)
            sc = jnp.where(tri, _mm_nt(q_dec[rs, _hs(h)], k_inv[rs, _hs(h)]), 0.0)
            blks.append(_mm(sc, v[rs, _hs(h)]))
        outs.append(jnp.concatenate(blks, axis=0))
    return jnp.concatenate(outs, axis=-1)


def _cs(c):
    return slice(c * CHUNK, (c + 1) * CHUNK)


def _hgrn_inter_fwd(qdec_s, kend_s, v_s, a_s, oint_s, st_ref, states_s, u_s):
    n = qdec_s.shape[0] // CHUNK
    for c in range(n):
        for h in range(NH):
            u_s[c, h] = _mm_tn(v_s[_cs(c), _hs(h)], kend_s[_cs(c), _hs(h)])
    for h in range(NH):
        st = st_ref[h]
        for c in range(n):
            states_s[c, h] = st
            st = st * a_s[c * CHUNK:c * CHUNK + 1, _hs(h)] + u_s[c, h]
        st_ref[h] = st
    if oint_s is None:
        return
    for c in range(n):
        for h in range(NH):
            oint_s[_cs(c), _hs(h)] = _mm_nt(qdec_s[_cs(c), _hs(h)], states_s[c, h])


def _rms(o):
    outs, rs = [], []
    for h in range(NH):
        oh = o[:, _hs(h)]
        r = lax.rsqrt(_lmean(oh * oh) + RMS_EPS)
        outs.append(oh * r)
        rs.append(r)
    return jnp.concatenate(outs, axis=-1), rs


def _gmlp_core(u_raw, v_raw, wtri_ref, bcolb_ref):
    gu, tu = _gelu(u_raw)
    gv, tv = _gelu(v_raw)
    vns, rstds, mixeds = [], [], []
    for h in range(NH):
        vn, rstd = _ln(gv[:, _hs(h)])
        blks = []
        for n in range(u_raw.shape[0] // HD):
            blks.append(_mm(wtri_ref[h], vn[n * HD:(n + 1) * HD]) + bcolb_ref[h])
        vns.append(vn)
        rstds.append(rstd)
        mixeds.append(jnp.concatenate(blks, axis=0))
    mixed = jnp.concatenate(mixeds, axis=-1)
    return gu, tu, tv, vns, rstds, mixed


def _pool_core(p, carry, row0, wbd_ref):
    ext = jnp.concatenate([carry, p], axis=0)
    r1 = ext + pltpu.roll(ext, 1, 0)
    r2 = r1 + pltpu.roll(r1, 2, 0)
    r3 = r2 + pltpu.roll(r2, 4, 0)
    r4 = r3 + pltpu.roll(r3, 8, 0)
    sel = _pool_pick(r1, r2, r3, r4)[2 * SUB:]
    grow = row0 + lax.broadcasted_iota(jnp.int32, p.shape, 0)
    inv_cnt = 1.0 / jnp.minimum(grow + 1, _pool_w(p.shape)).astype(f32)
    diff = sel * inv_cnt - p
    return diff, inv_cnt, _mm(diff, wbd_ref[...])


def _lru_core(xb, ccar, row0, p):
    ext = jnp.concatenate([ccar, xb], axis=0)
    cw = p["cw"]
    x1, x2, x3 = pltpu.roll(ext, 1, 0)[SUB:], pltpu.roll(ext, 2, 0)[SUB:], pltpu.roll(ext, 3, 0)[SUB:]
    xc = cw[3:4, :] * xb + cw[2:3, :] * x1 + cw[1:2, :] * x2 + cw[0:1, :] * x3 + p["cb"][...]
    gxs, gas = [], []
    for h in range(NH):
        gxs.append(_mm(xc[:, _hs(h)], p["wgx"][h]))
        gas.append(_mm(xc[:, _hs(h)], p["wga"][h]))
    gx = _sig(jnp.concatenate(gxs, axis=-1) + p["bgx"][...])
    ga = _sig(jnp.concatenate(gas, axis=-1) + p["bga"][...])
    sp = _softplus(-p["ap"][...])
    la = -LRU_C * ga * sp
    a = jnp.exp(la)
    grow = row0 + lax.broadcasted_iota(jnp.int32, xb.shape, 0)
    first = grow == 0
    mult = jnp.where(first, 1.0, jnp.sqrt(-_expm1(2.0 * la)))
    bt = mult * gx * xc
    return dict(x1=x1, x2=x2, x3=x3, xc=xc, gx=gx, ga=ga, sp=sp, a=a, mult=mult, bt=bt, first=first)


def _fwd_layer(kind, xin, w_in, w_out, lng, lnb, khT, vh, prm, tgt):
    S = xin.shape[0]
    TS = _TS[kind]
    nt = S // TS
    off = _OFFS[kind]
    W = off["W"]
    last = tgt is not None
    pnames = _PRM[kind]
    pvals = [prm[n] for n in pnames]

    def body(*refs):
        it = iter(refs)
        xin_ref, win_ref, wout_ref, lng_ref, lnb_ref, khT_ref, vh_ref = (next(it) for _ in range(7))
        p = {n: next(it) for n in pnames}
        tgt_ref = next(it) if last else None
        xout_ref, proj_ref, z_ref = next(it), next(it), next(it)
        loss_ref = next(it) if last else None
        rest = list(it)
        i = pl.program_id(0)
        x = xin_ref[...]
        proj_ref[...] = _mm(x, win_ref[...])

        if kind == 0:
            gu, _, _, _, _, mixed = _gmlp_core(proj_ref[:, 0:TOK], proj_ref[:, TOK:2 * TOK], p["wtri"], p["bcolb"])
            tok = gu * mixed
        elif kind == 1:
            st_save, o_save, st_ref, states_s, u_s, qdec_s, kend_s, v_s, a_s, oint_s = rest

            @pl.when(i == 0)
            def _():
                st_ref[...] = jnp.zeros_like(st_ref)

            st_save[0, 0] = st_ref[...]
            v = proj_ref[:, 2 * TOK:3 * TOK]
            hp = _hgrn_parallel(proj_ref[:, 0:TOK], proj_ref[:, TOK:2 * TOK], p["lb"][...])
            qdec_s[...] = hp["q_dec"]
            kend_s[...] = hp["k_end"]
            v_s[...] = v
            a_s[...] = hp["a"]
            o_intra = _hgrn_intra(hp["q_dec"], hp["k_inv"], v)
            _hgrn_inter_fwd(qdec_s, kend_s, v_s, a_s, oint_s, st_ref, states_s, u_s)
            o = o_intra + oint_s[...]
            o_save[0] = o
            for sub in range(1, TS // HGRN_SUB):
                st_save[0, sub] = states_s[sub * HGRN_SUB // CHUNK]
            on, _ = _rms(o)
            tok = on * p["ng"][...]
        elif kind == 2:
            pc_save, pcar = rest

            @pl.when(i == 0)
            def _():
                pcar[...] = jnp.zeros_like(pcar)

            pc_save[0] = pcar[...]
            pp = proj_ref[:, 0:TOK]
            _, _, y = _pool_core(pp, pcar[...], i * TS, p["wbd"])
            pcar[...] = pp[TS - 2 * SUB:, :]
            tok = y * p["scale"][...]
        else:
            cc_save, hc_save, h_save, ccar, hcar = rest

            @pl.when(i == 0)
            def _():
                ccar[...] = jnp.zeros_like(ccar)
                hcar[...] = jnp.zeros_like(hcar)

            cc_save[0] = ccar[...]
            hc_save[0] = hcar[...]
            xb = proj_ref[:, 0:TOK]
            lc = _lru_core(xb, ccar[...], i * TS, p)
            P, B = _scan_fwd(lc["a"], lc["bt"])
            tok = P * hcar[SUB - 1:SUB, :] + B
            h_save[0] = tok
            ccar[...] = xb[TS - SUB:, :]
            hcar[...] = tok[TS - SUB:, :]

        xo, _ = _xattn_fwd(proj_ref[:, off["qx"]:off["qx"] + XW], khT_ref, vh_ref)
        gate = proj_ref[:, off["gate"]:off["gate"] + D_MODEL]
        mixed = jnp.concatenate([tok, xo], axis=-1) * (gate * _sig(gate))
        z = ALPHA * x + _mm(mixed, wout_ref[...])
        z_ref[...] = z
        xhat, _ = _ln(z)
        xout = xhat * lng_ref[...] + lnb_ref[...]
        if last:
            e = xout - tgt_ref[...]
            xout_ref[...] = e * (1.0 / D_MODEL)
            es = _rowsum(e * e)
            tot = es[:, 0:LANE]
            for j in range(1, D_MODEL // LANE):
                tot = tot + es[:, j * LANE:(j + 1) * LANE]

            @pl.when(i == 0)
            def _():
                loss_ref[...] = jnp.zeros_like(loss_ref)

            loss_ref[0:1, :] += tot
        else:
            xout_ref[...] = xout

    tile = lambda w: pl.BlockSpec((TS, w), lambda i: (i, 0))
    in_arrays = [xin, w_in, w_out, lng, lnb, khT, vh] + pvals + ([tgt] if last else [])
    in_specs = [tile(D_MODEL)] + [_const_spec(a) for a in in_arrays[1:7 + len(pvals)]] + ([tile(D_MODEL)] if last else [])
    out_shape = [jax.ShapeDtypeStruct((S, D_MODEL), f32), jax.ShapeDtypeStruct((S, W), f32), jax.ShapeDtypeStruct((S, D_MODEL), f32)]
    out_specs = [tile(D_MODEL), tile(W), tile(D_MODEL)]
    if last:
        out_shape.append(jax.ShapeDtypeStruct((SUB, LANE), f32))
        out_specs.append(_acc_spec((SUB, LANE)))
    scratch = []
    save = lambda *s: (jax.ShapeDtypeStruct((nt,) + s, f32), pl.BlockSpec((1,) + s, lambda i, _n=len(s): (i,) + (0,) * _n))
    if kind == 1:
        saved = [save(TS // HGRN_SUB, NH, HD, HD), save(TS, TOK)]
        scratch = ([pltpu.VMEM((NH, HD, HD), f32)] + [pltpu.VMEM((TS // CHUNK, NH, HD, HD), f32)] * 2
                   + [pltpu.VMEM((TS, TOK), f32)] * 5)
    elif kind == 2:
        saved = [save(2 * SUB, TOK)]
        scratch = [pltpu.VMEM((2 * SUB, TOK), f32)]
    elif kind == 3:
        saved = [save(SUB, TOK), save(SUB, TOK), save(TS, TOK)]
        scratch = [pltpu.VMEM((SUB, TOK), f32)] * 2
    else:
        saved = []
    for sh, sp in saved:
        out_shape.append(sh)
        out_specs.append(sp)
    return pl.pallas_call(body, name=f"fwd_layer{kind}", grid=(nt,), in_specs=in_specs, out_specs=out_specs,
                          out_shape=out_shape, scratch_shapes=scratch, compiler_params=_params())(*in_arrays)


def _small_grad_shapes(kind):
    if kind == 0:
        return dict(dwtri=(NH, HD, HD), dbacc=(NH, HD, HD))
    if kind == 1:
        return dict(dlb=(SUB, TOK), dng=(SUB, TOK))
    if kind == 2:
        return dict(dwbd=(TOK, TOK), dscale=(SUB, TOK))
    return dict(dcw=(SUB, TOK), dvec=(SUB, TOK), dwgx=(NH, HD, HD), dwga=(NH, HD, HD))


def _bwd_layer(kind, dxout, z, proj, w_inT, w_outT, lng, kh, khT, vh, vhT, prm, saves):
    S = dxout.shape[0]
    TS = _TS[kind]
    nt = S // TS
    off = _OFFS[kind]
    W = off["W"]
    pnames = _PRM[kind]
    pvals = [prm[n] for n in pnames]
    sg_shapes = _small_grad_shapes(kind)
    sg_names = list(sg_shapes)
    n_saves = len(saves)

    def body(*refs):
        it = iter(refs)
        dxo_ref, z_ref, proj_ref, winT_ref, woutT_ref, lng_ref, kh_ref, khT_ref, vh_ref, vhT_ref = (next(it) for _ in range(10))
        p = {n: next(it) for n in pnames}
        sv = [next(it) for _ in range(n_saves)]
        dxin_ref, dproj_ref, mixed_ref, dy_ref, dln_ref, dk_ref, dv_ref = (next(it) for _ in range(7))
        sg = {n: next(it) for n in sg_names}
        rest = list(it)
        step = pl.program_id(0)
        i = nt - 1 - step

        @pl.when(step == 0)
        def _():
            dln_ref[...] = jnp.zeros_like(dln_ref)
            dk_ref[...] = jnp.zeros_like(dk_ref)
            dv_ref[...] = jnp.zeros_like(dv_ref)
            for n in sg_names:
                sg[n][...] = jnp.zeros_like(sg[n])

        dxo = dxo_ref[...]
        xhat, rstd = _ln(z_ref[...])
        dln_ref[0:1, :] += _rowsum(dxo * xhat)
        dln_ref[1:2, :] += _rowsum(dxo)
        dz = _ln_bwd(dxo * lng_ref[...], xhat, rstd)
        dyb = dz.astype(bf16)
        dy_ref[...] = dyb
        dmixed = _mm(dyb, woutT_ref[...])

        aux = {}
        if kind == 0:
            u_raw, v_raw = proj_ref[:, 0:TOK], proj_ref[:, TOK:2 * TOK]
            gu, tu, tv, vns, rstds, mx = _gmlp_core(u_raw, v_raw, p["wtri"], p["bcolb"])
            tok = gu * mx
        elif kind == 1:
            st_save, o_save = sv
            (dst_ref, fst_ref, states_s, dsts_s, u_s, qdec_s, kend_s, v_s, a_s, do_s, dqdec_s, dkend_s, dv_s,
             dgl_s) = rest

            @pl.when(step == 0)
            def _():
                dst_ref[...] = jnp.zeros_like(dst_ref)

            o = o_save[0]
            on, rs = _rms(o)
            tok = on * p["ng"][...]
            aux = dict(o=o, on=on, rs=rs)
        elif kind == 2:
            pc_save, = sv
            dpcar, = rest
            pp = proj_ref[:, 0:TOK]
            diff, inv_cnt, y = _pool_core(pp, pc_save[0], i * TS, p["wbd"])
            tok = y * p["scale"][...]
        else:
            cc_save, hc_save, h_save = sv
            dccar, gcar = rest
            xb = proj_ref[:, 0:TOK]
            lc = _lru_core(xb, cc_save[0], i * TS, p)
            hin = hc_save[0, SUB - 1:SUB, :]
            tok = h_save[0]

        xo, ps = _xattn_fwd(proj_ref[:, off["qx"]:off["qx"] + XW], khT_ref, vh_ref)
        gate = proj_ref[:, off["gate"]:off["gate"] + D_MODEL]
        sgm = _sig(gate)
        sgate = gate * sgm
        cat = jnp.concatenate([tok, xo], axis=-1)
        mixed_ref[...] = (cat * sgate).astype(bf16)
        dcat = dmixed * sgate
        dproj_ref[:, off["gate"]:off["gate"] + D_MODEL] = (dmixed * cat * (sgm * (1.0 + gate * (1.0 - sgm)))).astype(bf16)
        dtok = dcat[:, 0:TOK]
        dxo_att = dcat[:, TOK:]

        qx = proj_ref[:, off["qx"]:off["qx"] + XW]
        dqx = jnp.zeros((TS, XW), f32)
        for h in range(XHEADS):
            dp = _mm(dxo_att, vhT_ref[h])
            ds = ps[h] * (dp - jnp.sum(dp * ps[h], axis=-1, keepdims=True)) * (XDIM ** -0.5)
            dqx = dqx + _mm(ds, kh_ref[h])
            dk_ref[h] += _mm_tn(ds, qx)
            dv_ref[h] += _mm_tn(ps[h], dxo_att)
        dproj_ref[:, off["qx"]:off["qx"] + XW] = dqx.astype(bf16)

        if kind == 0:
            tril = lax.broadcasted_iota(jnp.int32, (HD, HD), 1) <= lax.broadcasted_iota(jnp.int32, (HD, HD), 0)
            dgu = dtok * mx
            dmx = dtok * gu
            dgvs = []
            for h in range(NH):
                dmh = dmx[:, _hs(h)]
                blks = []
                for n in range(TS // HD):
                    rs_ = slice(n * HD, (n + 1) * HD)
                    blks.append(_mm(p["wtriT"][h], dmh[rs_]))
                    sg["dwtri"][h] += jnp.where(tril, _mm_nt(dmh[rs_], vns[h][rs_]), 0.0)
                    sg["dbacc"][h] += dmh[rs_]
                dgvs.append(_ln_bwd(jnp.concatenate(blks, axis=0), vns[h], rstds[h]))
            dgv = jnp.concatenate(dgvs, axis=-1)
            dproj_ref[:, 0:TOK] = (dgu * _gelu_grad(u_raw, tu)).astype(bf16)
            dproj_ref[:, TOK:2 * TOK] = (dgv * _gelu_grad(v_raw, tv)).astype(bf16)
        elif kind == 1:
            o, on, rs = aux["o"], aux["on"], aux["rs"]
            ng = p["ng"][...]
            lb = p["lb"][...]
            sg["dng"][0:1, :] += _rowsum(dtok * on)
            dn = dtok * ng
            dos = []
            for h in range(NH):
                oh, r = o[:, _hs(h)], rs[h]
                dos.append(r * (dn[:, _hs(h)] - oh * (r * r) * _lmean(dn[:, _hs(h)] * oh)))
            do_all = jnp.concatenate(dos, axis=-1)
            _, tri = _chunk_mats(HD)
            same, _ = _chunk_mats(HGRN_SUB)
            triT = jnp.logical_and(same, lax.broadcasted_iota(jnp.int32, (HGRN_SUB, HGRN_SUB), 1)
                                   >= lax.broadcasted_iota(jnp.int32, (HGRN_SUB, HGRN_SUB), 0))
            row16 = lax.broadcasted_iota(jnp.int32, (CHUNK, HD), 0)
            nch = HGRN_SUB // CHUNK
            for sub in reversed(range(TS // HGRN_SUB)):
                rr = slice(sub * HGRN_SUB, (sub + 1) * HGRN_SUB)
                q_raw, v = proj_ref[rr, 0:TOK], proj_ref[rr, 2 * TOK:3 * TOK]
                hp = _hgrn_parallel(q_raw, proj_ref[rr, TOK:2 * TOK], lb)
                qdec_s[...] = hp["q_dec"]
                kend_s[...] = hp["k_end"]
                v_s[...] = v
                a_s[...] = hp["a"]
                fst_ref[...] = st_save[0, sub]
                _hgrn_inter_fwd(qdec_s, kend_s, v_s, a_s, None, fst_ref, states_s, u_s)
                do = do_all[rr]
                do_s[...] = do
                dqd, dki, dvi = [], [], []
                for h in range(NH):
                    bq, bk, bv = [], [], []
                    for b in range(HGRN_SUB // HD):
                        rs_ = slice(b * HD, (b + 1) * HD)
                        qd, ki = hp["q_dec"][rs_, _hs(h)], hp["k_inv"][rs_, _hs(h)]
                        sc = jnp.where(tri, _mm_nt(qd, ki), 0.0)
                        dsc = jnp.where(tri, _mm_nt(do[rs_, _hs(h)], v[rs_, _hs(h)]), 0.0)
                        bv.append(_mm_tn(sc, do[rs_, _hs(h)]))
                        bq.append(_mm(dsc, ki))
                        bk.append(_mm_tn(dsc, qd))
                    dqd.append(jnp.concatenate(bq, axis=0))
                    dki.append(jnp.concatenate(bk, axis=0))
                    dvi.append(jnp.concatenate(bv, axis=0))
                dqdec_s[...] = jnp.concatenate(dqd, axis=-1)
                dk_inv = jnp.concatenate(dki, axis=-1)
                dv_s[...] = jnp.concatenate(dvi, axis=-1)
                for c in range(nch):
                    for h in range(NH):
                        u_s[c, h] = _mm_tn(do_s[_cs(c), _hs(h)], qdec_s[_cs(c), _hs(h)])
                for h in range(NH):
                    dst = dst_ref[h]
                    for c in reversed(range(nch)):
                        dsts_s[c, h] = dst
                        dst = dst * a_s[c * CHUNK:c * CHUNK + 1, _hs(h)] + u_s[c, h]
                    dst_ref[h] = dst
                for c in range(nch):
                    for h in range(NH):
                        stp = states_s[c, h]
                        dst = dsts_s[c, h]
                        dqdec_s[_cs(c), _hs(h)] += _mm(do_s[_cs(c), _hs(h)], stp)
                        dkend_s[_cs(c), _hs(h)] = _mm(v_s[_cs(c), _hs(h)], dst)
                        dv_s[_cs(c), _hs(h)] += _mm_nt(kend_s[_cs(c), _hs(h)], dst)
                        da = jnp.sum(dst * stp, axis=0, keepdims=True) * a_s[c * CHUNK:c * CHUNK + 1, _hs(h)]
                        dgl_s[_cs(c), _hs(h)] = jnp.where(row16 == 0, jnp.broadcast_to(da, (CHUNK, HD)), 0.0)
                dq_dec = dqdec_s[...]
                dk_end = dkend_s[...]
                dg = dq_dec * hp["q_dec"] - dk_inv * hp["k_inv"] - dk_end * hp["k_end"]
                dk = dk_inv * hp["eng"] + dk_end * hp["ee"]
                dglr = dk_end * hp["k_end"] + dgl_s[...]
                dlogf = _mm_sel(triT, dg) + _mm_sel(same, dglr)
                df = dlogf / hp["f"] - dk
                sg["dlb"][0:1, :] += _rowsum(df * (1.0 - hp["sgm"]))
                dproj_ref[rr, 0:TOK] = (dq_dec * hp["eg"] * (hp["sq"] * (1.0 + q_raw * (1.0 - hp["sq"])))).astype(bf16)
                dproj_ref[rr, TOK:2 * TOK] = (df * (1.0 - lb) * hp["sgm"] * (1.0 - hp["sgm"])).astype(bf16)
                dproj_ref[rr, 2 * TOK:3 * TOK] = dv_s[...].astype(bf16)
        elif kind == 2:
            @pl.when(step == 0)
            def _():
                dpcar[...] = jnp.zeros_like(dpcar)

            sg["dscale"][0:1, :] += _rowsum(dtok * y)
            dyp = dtok * p["scale"][...]
            sg["dwbd"][...] += _mm_tn(diff, dyp)
            ddiff = _mm(dyp, p["wbdT"][...])
            q = ddiff * inv_cnt
            ext = jnp.concatenate([q, dpcar[...]], axis=0)
            n = TS + 2 * SUB
            r1 = ext + pltpu.roll(ext, n - 1, 0)
            r2 = r1 + pltpu.roll(r1, n - 2, 0)
            r3 = r2 + pltpu.roll(r2, n - 4, 0)
            r4 = r3 + pltpu.roll(r3, n - 8, 0)
            dproj_ref[:, 0:TOK] = (_pool_pick(r1, r2, r3, r4)[:TS] - ddiff).astype(bf16)
            dpcar[...] = q[0:2 * SUB, :]
        else:
            @pl.when(step == 0)
            def _():
                dccar[...] = jnp.zeros_like(dccar)
                gcar[...] = jnp.zeros_like(gcar)

            a, mult, gx, ga, xc = lc["a"], lc["mult"], lc["gx"], lc["ga"], lc["xc"]
            row = lax.broadcasted_iota(jnp.int32, (TS, TOK), 0)
            an = jnp.where(row == TS - 1, 1.0, pltpu.roll(a, TS - 1, 0))
            Pb, Bb = _scan_bwd(an, dtok)
            lam = Pb * gcar[0:1, :] + Bb
            gcar[...] = (a * lam)[0:SUB, :]
            hprev = jnp.where(row == 0, jnp.broadcast_to(hin, (TS, TOK)), pltpu.roll(tok, 1, 0))
            dmult = lam * gx * xc
            dgx = lam * mult * xc
            dxc = lam * mult * gx
            dla = lam * hprev * a - jnp.where(lc["first"], 0.0, dmult * a * a / mult)
            sp = lc["sp"]
            dga = -LRU_C * sp * dla
            dsp = _rowsum(-LRU_C * ga * dla)
            sg["dvec"][0:1, :] += dsp * (-_sig(-p["ap"][...]))
            dpx = dgx * gx * (1.0 - gx)
            dpa = dga * ga * (1.0 - ga)
            sg["dvec"][1:2, :] += _rowsum(dpx)
            sg["dvec"][2:3, :] += _rowsum(dpa)
            dxcs = []
            for h in range(NH):
                dxcs.append(_mm(dpx[:, _hs(h)], p["wgxT"][h]) + _mm(dpa[:, _hs(h)], p["wgaT"][h]))
                sg["dwgx"][h] += _mm_tn(xc[:, _hs(h)], dpx[:, _hs(h)])
                sg["dwga"][h] += _mm_tn(xc[:, _hs(h)], dpa[:, _hs(h)])
            dxc = dxc + jnp.concatenate(dxcs, axis=-1)
            sg["dvec"][3:4, :] += _rowsum(dxc)
            sg["dcw"][3:4, :] += _rowsum(dxc * xb)
            sg["dcw"][2:3, :] += _rowsum(dxc * lc["x1"])
            sg["dcw"][1:2, :] += _rowsum(dxc * lc["x2"])
            sg["dcw"][0:1, :] += _rowsum(dxc * lc["x3"])
            ext = jnp.concatenate([dxc, dccar[...]], axis=0)
            n = TS + SUB
            cw = p["cw"]
            dproj_ref[:, 0:TOK] = (cw[3:4, :] * dxc + cw[2:3, :] * pltpu.roll(ext, n - 1, 0)[:TS]
                                   + cw[1:2, :] * pltpu.roll(ext, n - 2, 0)[:TS]
                                   + cw[0:1, :] * pltpu.roll(ext, n - 3, 0)[:TS]).astype(bf16)
            dccar[...] = dxc[0:SUB, :]

        dxin_ref[...] = ALPHA * dz + _mm(dproj_ref[...], winT_ref[...])

    rtile = lambda w: pl.BlockSpec((TS, w), lambda s: (nt - 1 - s, 0))
    consts = [w_inT, w_outT, lng, kh, khT, vh, vhT] + pvals
    in_arrays = [dxout, z, proj] + consts + list(saves)
    in_specs = [rtile(D_MODEL), rtile(D_MODEL), rtile(W)] + [_const_spec(a) for a in consts]
    for a in saves:
        in_specs.append(pl.BlockSpec((1,) + a.shape[1:], lambda s, _n=a.ndim - 1: (nt - 1 - s,) + (0,) * _n))
    out_shape = [jax.ShapeDtypeStruct((S, D_MODEL), f32), jax.ShapeDtypeStruct((S, W), bf16),
                 jax.ShapeDtypeStruct((S, D_MODEL), bf16), jax.ShapeDtypeStruct((S, D_MODEL), bf16),
                 jax.ShapeDtypeStruct((SUB, D_MODEL), f32), jax.ShapeDtypeStruct((XHEADS, XW, XW), f32),
                 jax.ShapeDtypeStruct((XHEADS, XW, XW), f32)]
    out_specs = [rtile(D_MODEL), rtile(W), rtile(D_MODEL), rtile(D_MODEL), _acc_spec((SUB, D_MODEL)),
                 _acc_spec((XHEADS, XW, XW)), _acc_spec((XHEADS, XW, XW))]
    for n in sg_names:
        out_shape.append(jax.ShapeDtypeStruct(sg_shapes[n], f32))
        out_specs.append(_acc_spec(sg_shapes[n]))
    if kind == 1:
        scratch = ([pltpu.VMEM((NH, HD, HD), f32)] * 2 + [pltpu.VMEM((HGRN_SUB // CHUNK, NH, HD, HD), f32)] * 3
                   + [pltpu.VMEM((HGRN_SUB, TOK), f32)] * 9)
    elif kind == 2:
        scratch = [pltpu.VMEM((2 * SUB, TOK), f32)]
    elif kind == 3:
        scratch = [pltpu.VMEM((SUB, TOK), f32)] * 2
    else:
        scratch = []
    outs = pl.pallas_call(body, name=f"bwd_layer{kind}", grid=(nt,), in_specs=in_specs, out_specs=out_specs,
                          out_shape=out_shape, scratch_shapes=scratch, compiler_params=_params())(*in_arrays)
    return outs[:7], dict(zip(sg_names, outs[7:]))


def _prep(mem, w_kv, logits):
    def body(mem_ref, w_ref, lg_ref, kh_ref, khT_ref, vh_ref, vhT_ref, p_ref):
        kv = _mm(mem_ref[...], w_ref[...])
        k, v = kv[:, 0:XW], kv[:, XW:]
        kT, vT = k.T, v.T
        col = lax.broadcasted_iota(jnp.int32, (XW, XW), 1) // XDIM
        row = lax.broadcasted_iota(jnp.int32, (XW, XW), 0) // XDIM
        for h in range(XHEADS):
            kh_ref[h] = jnp.where(col == h, k, 0.0).astype(bf16)
            vh_ref[h] = jnp.where(col == h, v, 0.0).astype(bf16)
            khT_ref[h] = jnp.where(row == h, kT, 0.0).astype(bf16)
            vhT_ref[h] = jnp.where(row == h, vT, 0.0).astype(bf16)
        lg = lg_ref[...]
        e = jnp.exp(lg - jnp.max(lg, axis=0, keepdims=True))
        p_ref[...] = e / jnp.sum(e, axis=0, keepdims=True)

    vm = pl.BlockSpec(memory_space=pltpu.VMEM)
    hs = jax.ShapeDtypeStruct((XHEADS, XW, XW), bf16)
    return pl.pallas_call(body, name="prep_memory", in_specs=[vm] * 3, out_specs=[vm] * 5,
                          out_shape=[hs, hs, hs, hs, jax.ShapeDtypeStruct(logits.shape, f32)])(mem, w_kv, logits)


def _kv_bwd(mem, dks, dvs):
    def body(mem_ref, *refs):
        out_ref = refs[-1]
        col = lax.broadcasted_iota(jnp.int32, (XW, XW), 1) // XDIM
        dk = jnp.zeros((XW, XW), f32)
        dv = jnp.zeros((XW, XW), f32)
        for l in range(DEPTH):
            for h in range(XHEADS):
                dk = dk + jnp.where(col == h, refs[l][h], 0.0)
                dv = dv + jnp.where(col == h, refs[DEPTH + l][h], 0.0)
        out_ref[:, 0:XW] = _mm_tn(mem_ref[...], dk)
        out_ref[:, XW:] = _mm_tn(mem_ref[...], dv)

    vm = pl.BlockSpec(memory_space=pltpu.VMEM)
    return pl.pallas_call(body, name="kv_bwd", in_specs=[vm] * (1 + 2 * DEPTH), out_specs=vm,
                          out_shape=jax.ShapeDtypeStruct((D_MODEL, 2 * XW), f32))(mem, *dks, *dvs)


def _tn_gemm(a, b, name, nb):
    S, M = a.shape
    N = b.shape[1]
    NB = N // nb
    nk = S // TK

    def body(a_ref, b_ref, o_ref):
        @pl.when(pl.program_id(1) == 0)
        def _():
            o_ref[...] = jnp.zeros_like(o_ref)

        o_ref[...] += _mm_tn(a_ref[...], b_ref[...])

    return pl.pallas_call(body, name=name, grid=(nb, nk),
                          in_specs=[pl.BlockSpec((TK, M), lambda j, k: (k, 0)), pl.BlockSpec((TK, NB), lambda j, k: (k, j))],
                          out_specs=pl.BlockSpec((M, NB), lambda j, k: (0, j)),
                          out_shape=jax.ShapeDtypeStruct((M, N), f32),
                          compiler_params=pltpu.CompilerParams(dimension_semantics=("parallel", "arbitrary"),
                                                               vmem_limit_bytes=VMEM_LIMIT))(a, b)


def _rows_block(R, mult=16, cap=1024):
    best = R
    for d in range(mult, min(R, cap) + 1, mult):
        if R % d == 0:
            best = d
    return best


def _tn_gemm_sharded(a, b, name):
    S, M = a.shape
    Wq = b.shape[1] // 4
    nk = S // TK

    def body(a_ref, b_ref, o_ref):
        @pl.when(pl.program_id(0) == 0)
        def _():
            o_ref[...] = jnp.zeros_like(o_ref)

        at = a_ref[...].astype(MM)
        for j in range(4):
            o_ref[j] += _mm_tn(at, b_ref[:, j * Wq:(j + 1) * Wq])

    return pl.pallas_call(body, name=name, grid=(nk,),
                          in_specs=[pl.BlockSpec((TK, M), lambda k: (k, 0)), pl.BlockSpec((TK, 4 * Wq), lambda k: (k, 0))],
                          out_specs=pl.BlockSpec((4, M, Wq), lambda k: (0, 0, 0)),
                          out_shape=jax.ShapeDtypeStruct((4, M, Wq), f32), compiler_params=_params())(a, b)


HALF_ROWS = D_MODEL // 2
SHARD_ROWS = D_MODEL // 4


def _half_of_full(ref, kind, h):
    if kind == "rows":
        cols = ref.shape[1] // 2
        return ref.at[:, pl.ds(h * cols, cols)]
    return ref.at[:, pl.ds(h * HALF_ROWS, HALF_ROWS)]


def _shard_of_half(ref, kind, j):
    if kind == "rows":
        return ref.at[pl.ds(j * SHARD_ROWS, SHARD_ROWS)]
    return ref.at[j]


def _half_of_shard(ref, kind, h):
    if kind == "rows":
        cols = ref.shape[1] // 2
        return ref.at[:, pl.ds(h * cols, cols)]
    rows = ref.shape[0] // 2
    return ref.at[pl.ds(h * rows, rows)]


def _half_shape(full_shape, kind):
    if kind == "rows":
        return (full_shape[0], full_shape[1] // 2)
    return (4, HALF_ROWS, full_shape[2])


def _shard_half_shape(full_shape, kind):
    if kind == "rows":
        return (SHARD_ROWS, full_shape[1] // 2)
    return (HALF_ROWS, full_shape[2])


def _shard_shape(full_shape, kind):
    if kind == "rows":
        return (SHARD_ROWS, full_shape[1])
    return (D_MODEL, full_shape[2])


def _ew_call(body, name, grid, jc, ins, in_specs, out_shape, out_specs):
    gs = pltpu.PrefetchScalarGridSpec(num_scalar_prefetch=1, grid=grid, in_specs=in_specs, out_specs=out_specs)
    return pl.pallas_call(body, name=name, grid_spec=gs, out_shape=out_shape,
                          compiler_params=pltpu.CompilerParams(dimension_semantics=("parallel",) * len(grid),
                                                               vmem_limit_bytes=VMEM_LIMIT))(jc, *ins)


def _add_sibling(part, got, kind, jc, name):
    def body(jc_ref, a_ref, b_ref, o_ref, ob_ref):
        s = a_ref[...] + b_ref[...]
        o_ref[...] = s
        ob_ref[...] = s.astype(bf16)

    if kind == "rows":
        R, C = part.shape[0], part.shape[1] // 2
        grid = (2,)
        mine = pl.BlockSpec((R // 2, C), lambda i, jc_ref: (i, jc_ref[1]))
        spec = pl.BlockSpec((R // 2, C), lambda i, jc_ref: (i, 0))
    else:
        C = part.shape[2]
        grid = (4, 2)
        mine = pl.BlockSpec((None, HALF_ROWS // 2, C), lambda s, i, jc_ref: (s, 2 * jc_ref[1] + i, 0))
        spec = pl.BlockSpec((None, HALF_ROWS // 2, C), lambda s, i, jc_ref: (s, i, 0))
    hs = _half_shape(part.shape, kind)
    return _ew_call(body, name, grid, jc, [part, got], [mine, spec],
                    [jax.ShapeDtypeStruct(hs, f32), jax.ShapeDtypeStruct(hs, bf16)], [spec, spec])


def _add_chips(q32, r, kind, jc, name):
    def body(jc_ref, q_ref, r_ref, out_ref):
        out_ref[...] = ((q_ref[...] + r_ref[0].astype(f32)) + r_ref[1].astype(f32)) + r_ref[2].astype(f32)

    if kind == "rows":
        C = q32.shape[1]
        grid = (1,)
        qs = pl.BlockSpec((SHARD_ROWS, C), lambda i, jc_ref: (jc_ref[0], 0))
        rs = pl.BlockSpec((3, SHARD_ROWS, C), lambda i, jc_ref: (0, 0, 0))
        os_ = pl.BlockSpec((SHARD_ROWS, C), lambda i, jc_ref: (0, jc_ref[1]))
        full_shape = (D_MODEL, 2 * C)
    else:
        C = q32.shape[2]
        grid = (2,)
        qs = pl.BlockSpec((None, HALF_ROWS // 2, C), lambda i, jc_ref: (jc_ref[0], i, 0))
        rs = pl.BlockSpec((3, HALF_ROWS // 2, C), lambda i, jc_ref: (0, i, 0))
        os_ = pl.BlockSpec((HALF_ROWS // 2, C), lambda i, jc_ref: (2 * jc_ref[1] + i, 0))
        full_shape = (4, D_MODEL, C)
    return _ew_call(body, name, grid, jc, [q32, r], [qs, rs], jax.ShapeDtypeStruct(_shard_shape(full_shape, kind), f32), os_)


def _adamw(w, g, m, v, name):
    R, C = w.shape
    br = _rows_block(R, mult=SUB, cap=512)
    c1 =1.0 / (1.0 - ADAM_B1 ** ADAM_STEP)
    c2 = 1.0 / (1.0 - ADAM_B2 ** ADAM_STEP)

    def body(w_ref, g_ref, m_ref, v_ref, d_ref, nm_ref, nv_ref):
        g_ = g_ref[...]
        nm = ADAM_B1 * m_ref[...] + (1.0 - ADAM_B1) * g_
        nv = ADAM_B2 * v_ref[...] + (1.0 - ADAM_B2) * (g_ * g_)
        nm_ref[...] = nm
        nv_ref[...] = nv
        d_ref[...] = -ADAM_LR * ((nm * c1) / (jnp.sqrt(nv * c2) + ADAM_EPS) + ADAM_WD * w_ref[...])

    spec = pl.BlockSpec((br, C), lambda i: (i, 0))
    sh = jax.ShapeDtypeStruct((R, C), f32)
    return pl.pallas_call(body, name=name, grid=(R // br,), in_specs=[spec] * 4, out_specs=[spec] * 3,
                          out_shape=[sh, sh, sh], compiler_params=_params("parallel"))(w, g, m, v)


def _small_finish(dbacc, p_soft, dlb):
    def body(db_ref, p_ref, dlb_ref, dbs_ref, dlg_ref):
        lane = lax.broadcasted_iota(jnp.int32, (HD, HD), 1)
        acc = jnp.zeros((HD, HD), f32)
        for h in range(NH):
            acc = acc + jnp.where(lane == h, jnp.sum(db_ref[h], axis=-1, keepdims=True), 0.0)
        dbs_ref[...] = acc
        p = p_ref[...]
        p1 = p[1:2, :]
        rowi = lax.broadcasted_iota(jnp.int32, p.shape, 0)
        dlg_ref[...] = dlb_ref[0:1, :] * p1 * (jnp.where(rowi == 1, 1.0, 0.0) - p)

    vm = pl.BlockSpec(memory_space=pltpu.VMEM)
    return pl.pallas_call(body, name="small_finish", in_specs=[vm] * 3, out_specs=[vm] * 2,
                          out_shape=[jax.ShapeDtypeStruct((HD, HD), f32), jax.ShapeDtypeStruct(p_soft.shape, f32)])(dbacc, p_soft, dlb)


def _where_am_i():
    return lax.axis_index("x"), lax.axis_index("y"), lax.axis_index("c")


MAX_PIECES = 8


def _nchunks(rows, mult):
    for n in range(MAX_PIECES, 0, -1):
        if rows % (n * mult) == 0:
            return n
    return 1


def _leading_pieces(src, dst):
    n = src.shape[0]
    if len(src.shape) >= 3 and n <= MAX_PIECES:
        return [(src.at[s], dst.at[s]) for s in range(n)]
    return [(src, dst)]


def _ag_weights(shards, kinds, jshard):
    n = len(shards)

    def body(*refs):
        sh_refs, out_refs, token = refs[:n], refs[2 * n:3 * n], refs[3 * n]
        send_sems, recv_sems = refs[3 * n + 1:]
        x, y, c = _where_am_i()
        j = 2 * x + y
        sib = (x, y, 1 - c)
        chips = [(1 - x, y), (x, 1 - y), (1 - x, 1 - y)]
        token[...] = jnp.zeros_like(token)

        def cp(k, src, dst, to):
            return pltpu.make_async_remote_copy(src_ref=src, dst_ref=dst, send_sem=send_sems.at[k], recv_sem=recv_sems.at[k],
                                                device_id=to, device_id_type=MESH)

        started = []
        for a in range(n):
            for k, (cx, cy) in enumerate(chips):
                d = cp(6 * a + k, _half_of_shard(sh_refs[a], kinds[a], c), _half_of_shard(out_refs[a].at[j], kinds[a], c), (cx, cy, c))
                d.start()
                started.append(d)
        for a in range(n):
            for k, (cx, cy) in enumerate(chips):
                blk = _half_of_shard(out_refs[a].at[2 * cx + cy], kinds[a], c)
                cp(6 * a + k, blk, blk, (cx, cy, c)).wait_recv()
                d = cp(6 * a + 3 + k, blk, blk, sib)
                d.start()
                started.append(d)
        for a in range(n):
            for k, (cx, cy) in enumerate(chips):
                blk = _half_of_shard(out_refs[a].at[2 * cx + cy], kinds[a], 1 - c)
                cp(6 * a + 3 + k, blk, blk, sib).wait_recv()
        for d in started:
            d.wait_send()

    placed = [lax.dynamic_update_slice(jnp.zeros((4,) + s.shape, s.dtype), s[None], (jshard,) + (0,) * s.ndim) for s in shards]
    anyspec = pl.BlockSpec(memory_space=pl.ANY)
    outs = pl.pallas_call(body, name="all_gather_weights", in_specs=[anyspec] * (2 * n),
                          out_specs=[anyspec] * n + [pl.BlockSpec(memory_space=pltpu.VMEM)],
                          out_shape=[jax.ShapeDtypeStruct(p.shape, p.dtype) for p in placed] + [jax.ShapeDtypeStruct((SUB, LANE), f32)],
                          input_output_aliases={n + a: a for a in range(n)},
                          scratch_shapes=[pltpu.SemaphoreType.DMA((6 * n,)), pltpu.SemaphoreType.DMA((6 * n,))],
                          compiler_params=pltpu.CompilerParams(has_side_effects=True))(*shards, *placed)
    return outs[:n], outs[n]


_HBM = pl.BlockSpec(memory_space=pltpu.HBM)
_SEM = pl.BlockSpec(memory_space=pltpu.SEMAPHORE)
_FLOWING = pltpu.SideEffectType.DATAFLOW_SIDE_EFFECTING


def _peers6(x, y, c):
    chips = [(1 - x, y), (x, 1 - y), (1 - x, 1 - y)]
    return [(2 * k + e, chip, c if e == 0 else 1 - c) for k, chip in enumerate(chips) for e in range(2)]


def _ag_start(shards, jshard, name, after=None):
    n = len(shards)

    def body(*refs):
        out_refs = refs[2 * n:4 * n]
        send_sems, recv_sems, token = refs[4 * n:]
        x, y, c = _where_am_i()
        j = 2 * x + y
        for a in range(n):
            for slot, (cx, cy), tc in _peers6(x, y, c):
                pltpu.make_async_remote_copy(src_ref=_half_of_shard(out_refs[a], "win", c),
                                             dst_ref=_half_of_shard(out_refs[n + a].at[j], "win", c),
                                             send_sem=send_sems.at[6 * a + slot], recv_sem=recv_sems.at[6 * a + slot],
                                             device_id=(cx, cy, tc), device_id_type=MESH).start()
        token[...] = jnp.zeros_like(token)

    fill = jnp.zeros((), f32) if after is None else after[0, 0]
    placed = [lax.dynamic_update_slice(jnp.broadcast_to(fill.astype(s.dtype), (4,) + s.shape), s[None], (jshard,) + (0,) * s.ndim)
              for s in shards]
    hbm = lambda t: pltpu.with_memory_space_constraint(t, pltpu.HBM)
    both = list(shards) + placed
    outs = pl.pallas_call(
        body, name=name, in_specs=[_HBM] * (2 * n), out_specs=[_HBM] * (2 * n) + [_SEM, _SEM, pl.BlockSpec(memory_space=pltpu.VMEM)],
        out_shape=[pltpu.HBM(p.shape, p.dtype) for p in both] + [pltpu.SemaphoreType.DMA((6 * n,)), pltpu.SemaphoreType.DMA((6 * n,)),
                                                                jax.ShapeDtypeStruct((SUB, LANE), f32)],
        input_output_aliases={a: a for a in range(2 * n)},
        compiler_params=pltpu.CompilerParams(has_side_effects=_FLOWING))(*[hbm(t) for t in both])
    return outs[:2 * n], outs[2 * n], outs[2 * n + 1], outs[2 * n + 2]


def _ag_wait(bufs, send_sems, recv_sems, after, name):
    n = len(bufs) // 2

    def body(*refs):
        sh_refs, g_refs = refs[:n], refs[n:2 * n]
        send_sems, recv_sems = refs[2 * n], refs[2 * n + 1]
        x, y, c = _where_am_i()
        for a in range(n):
            for slot, (cx, cy), tc in _peers6(x, y, c):
                cp = pltpu.make_async_remote_copy(src_ref=_half_of_shard(sh_refs[a], "win", c),
                                                  dst_ref=_half_of_shard(g_refs[a].at[2 * cx + cy], "win", tc),
                                                  send_sem=send_sems.at[6 * a + slot], recv_sem=recv_sems.at[6 * a + slot],
                                                  device_id=(cx, cy, tc), device_id_type=MESH)
                cp.wait_send()
                cp.wait_recv()

    outs = pl.pallas_call(body, name=name, in_specs=[_HBM] * (2 * n) + [_SEM, _SEM, pl.BlockSpec(memory_space=pl.ANY)],
                          out_specs=[_HBM] * (2 * n), out_shape=[pltpu.HBM(b.shape, b.dtype) for b in bufs],
                          input_output_aliases={a: a for a in range(2 * n)},
                          compiler_params=pltpu.CompilerParams(has_side_effects=_FLOWING))(*bufs, send_sems, recv_sems, after)
    return outs[n:]


def _rs_swap(parts, kinds, name):
    n = len(parts)

    def body(*refs):
        p_refs, got_refs = refs[:n], refs[n:2 * n]
        send_sems, recv_sems = refs[2 * n:]
        x, y, c = _where_am_i()

        def cp(a, src, dst):
            return pltpu.make_async_remote_copy(src_ref=src, dst_ref=dst, send_sem=send_sems.at[a], recv_sem=recv_sems.at[a],
                                                device_id=(x, y, 1 - c), device_id_type=MESH)

        for a in range(n):
            for src, dst in _leading_pieces(_half_of_full(p_refs[a], kinds[a], 1 - c), got_refs[a]):
                cp(a, src, dst).start()
        for a in range(n):
            cp(a, got_refs[a], got_refs[a]).wait()

    anyspec = pl.BlockSpec(memory_space=pl.ANY)
    return pl.pallas_call(body, name=name, in_specs=[anyspec] * n, out_specs=[anyspec] * n,
                          out_shape=[jax.ShapeDtypeStruct(_half_shape(p.shape, k), p.dtype) for p, k in zip(parts, kinds)],
                          scratch_shapes=[pltpu.SemaphoreType.DMA((n,)), pltpu.SemaphoreType.DMA((n,))],
                          compiler_params=pltpu.CompilerParams(has_side_effects=True))(*parts)


def _rs_owners(qbs, kinds, full_shapes):
    n = len(qbs)

    def body(*refs):
        q_refs, got_refs = refs[:n], refs[n:2 * n]
        send_sems, recv_sems = refs[2 * n:]
        x, y, c = _where_am_i()
        chips = [(1 - x, y), (x, 1 - y), (1 - x, 1 - y)]
        ds = []
        for a in range(n):
            for k, (cx, cy) in enumerate(chips):
                d = pltpu.make_async_remote_copy(src_ref=_shard_of_half(q_refs[a], kinds[a], 2 * cx + cy), dst_ref=got_refs[a].at[k],
                                                 send_sem=send_sems.at[3 * a + k], recv_sem=recv_sems.at[3 * a + k],
                                                 device_id=(cx, cy, c), device_id_type=MESH)
                d.start()
                ds.append(d)
        for d in ds:
            d.wait()

    anyspec = pl.BlockSpec(memory_space=pl.ANY)
    return pl.pallas_call(body, name="rs_to_owners", in_specs=[anyspec] * n, out_specs=[anyspec] * n,
                          out_shape=[jax.ShapeDtypeStruct((3,) + _shard_half_shape(fs, k), bf16) for fs, k in zip(full_shapes, kinds)],
                          scratch_shapes=[pltpu.SemaphoreType.DMA((3 * n,)), pltpu.SemaphoreType.DMA((3 * n,))],
                          compiler_params=pltpu.CompilerParams(has_side_effects=True))(*qbs)


def _rs_owners_start(qbs, kinds, full_shapes, name):
    n = len(qbs)

    def body(*refs):
        q_refs, got_refs = refs[2 * n:3 * n], refs[3 * n:4 * n]
        send_sems, recv_sems, token = refs[4 * n:]
        x, y, c = _where_am_i()
        for a in range(n):
            for k, (cx, cy) in enumerate([(1 - x, y), (x, 1 - y), (1 - x, 1 - y)]):
                pltpu.make_async_remote_copy(src_ref=_shard_of_half(q_refs[a], kinds[a], 2 * cx + cy), dst_ref=got_refs[a].at[k],
                                             send_sem=send_sems.at[3 * a + k], recv_sem=recv_sems.at[3 * a + k],
                                             device_id=(cx, cy, c), device_id_type=MESH).start()
        token[...] = jnp.zeros_like(token)

    hbm = lambda t: pltpu.with_memory_space_constraint(t, pltpu.HBM)
    lands = [lax.empty((3,) + _shard_half_shape(fs, k), bf16) for fs, k in zip(full_shapes, kinds)]
    both = list(qbs) + lands
    outs = pl.pallas_call(
        body, name=name, in_specs=[_HBM] * (2 * n), out_specs=[_HBM] * (2 * n) + [_SEM, _SEM, pl.BlockSpec(memory_space=pltpu.VMEM)],
        out_shape=[pltpu.HBM(t.shape, t.dtype) for t in both] + [pltpu.SemaphoreType.DMA((3 * n,)), pltpu.SemaphoreType.DMA((3 * n,)),
                                                                jax.ShapeDtypeStruct((SUB, LANE), f32)],
        input_output_aliases={a: a for a in range(2 * n)},
        compiler_params=pltpu.CompilerParams(has_side_effects=_FLOWING))(*[hbm(t) for t in both])
    return outs[:2 * n], outs[2 * n], outs[2 * n + 1], outs[2 * n + 2]


def _rs_owners_wait(bufs, send_sems, recv_sems, kinds, after, name):
    n = len(bufs) // 2

    def body(*refs):
        q_refs, got_refs = refs[:n], refs[n:2 * n]
        send_sems, recv_sems = refs[2 * n], refs[2 * n + 1]
        x, y, c = _where_am_i()
        for a in range(n):
            for k, (cx, cy) in enumerate([(1 - x, y), (x, 1 - y), (1 - x, 1 - y)]):
                cp = pltpu.make_async_remote_copy(src_ref=_shard_of_half(q_refs[a], kinds[a], 2 * cx + cy), dst_ref=got_refs[a].at[k],
                                                  send_sem=send_sems.at[3 * a + k], recv_sem=recv_sems.at[3 * a + k],
                                                  device_id=(cx, cy, c), device_id_type=MESH)
                cp.wait_send()
                cp.wait_recv()

    outs = pl.pallas_call(body, name=name, in_specs=[_HBM] * (2 * n) + [_SEM, _SEM, pl.BlockSpec(memory_space=pl.ANY)],
                          out_specs=[_HBM] * (2 * n), out_shape=[pltpu.HBM(b.shape, b.dtype) for b in bufs],
                          input_output_aliases={a: a for a in range(2 * n)},
                          compiler_params=pltpu.CompilerParams(has_side_effects=_FLOWING))(*bufs, send_sems, recv_sems, after)
    return outs[n:]


def _rs_join(bufs, kinds):
    n = len(bufs)

    def body(*refs):
        out_refs = refs[n:2 * n]
        send_sems, recv_sems = refs[2 * n:]
        x, y, c = _where_am_i()

        def cp(a, h):
            blk = _half_of_shard(out_refs[a], kinds[a], h)
            return pltpu.make_async_remote_copy(src_ref=blk, dst_ref=blk, send_sem=send_sems.at[a], recv_sem=recv_sems.at[a],
                                                device_id=(x, y, 1 - c), device_id_type=MESH)

        for a in range(n):
            cp(a, c).start()
        for a in range(n):
            cp(a, c).wait_send()
            cp(a, 1 - c).wait_recv()

    anyspec = pl.BlockSpec(memory_space=pl.ANY)
    return pl.pallas_call(body, name="rs_join_halves", in_specs=[anyspec] * n, out_specs=[anyspec] * n,
                          out_shape=[jax.ShapeDtypeStruct(b.shape, b.dtype) for b in bufs],
                          input_output_aliases={a: a for a in range(n)},
                          scratch_shapes=[pltpu.SemaphoreType.DMA((n,)), pltpu.SemaphoreType.DMA((n,))],
                          compiler_params=pltpu.CompilerParams(has_side_effects=True))(*bufs)


def _all_reduce_small(g):
    R, C = g.shape
    H = R // 2
    NP = _nchunks(H, SUB)
    PR = H // NP

    def body(g_ref, out_ref, sib_ref, chip_ref, send_sems, recv_sems):
        x, y, c = _where_am_i()
        j = 2 * x + y
        sib = (x, y, 1 - c)
        chips = [(1 - x, y), (x, 1 - y), (1 - x, 1 - y)]
        rows = pl.ds(pl.multiple_of(c * H, SUB), H)

        def cp(k, src, dst, to):
            return pltpu.make_async_remote_copy(src_ref=src, dst_ref=dst, send_sem=send_sems.at[k], recv_sem=recv_sems.at[k],
                                                device_id=to, device_id_type=MESH)

        def pieces(k, src, dst, to):
            for q in range(NP):
                cp(k, src.at[pl.ds(q * PR, PR)], dst.at[pl.ds(q * PR, PR)], to).start()

        for half in range(2):
            pieces(0, g_ref.at[pl.ds(half * H, H)], sib_ref.at[pl.ds(half * H, H)], sib)
        cp(0, g_ref, sib_ref, sib).wait()
        chip_ref[j] = g_ref[rows, :] + sib_ref[rows, :]
        for k, (cx, cy) in enumerate(chips):
            pieces(1 + k, chip_ref.at[j], chip_ref.at[j], (cx, cy, c))
        for k, (cx, cy) in enumerate(chips):
            blk = chip_ref.at[2 * cx + cy]
            cp(1 + k, blk, blk, (cx, cy, c)).wait()
        out_ref[rows, :] = ((chip_ref[0] + chip_ref[1]) + chip_ref[2]) + chip_ref[3]
        other = out_ref.at[pl.ds(pl.multiple_of((1 - c) * H, SUB), H)]
        pieces(4, out_ref.at[rows], out_ref.at[rows], sib)
        cp(4, other, other, sib).wait()

    vm = pl.BlockSpec(memory_space=pltpu.VMEM)
    return pl.pallas_call(body, name="all_reduce_small", in_specs=[vm], out_specs=vm,
                          out_shape=jax.ShapeDtypeStruct((R, C), f32),
                          scratch_shapes=[pltpu.VMEM((R, C), f32), pltpu.VMEM((4, H, C), f32),
                                          pltpu.SemaphoreType.DMA((5,)), pltpu.SemaphoreType.DMA((5,))],
                          compiler_params=pltpu.CompilerParams(has_side_effects=True, vmem_limit_bytes=VMEM_LIMIT))(g)


SPLIT_MIN_ELEMS = 1 << 16


def _all_reduce_many(gs):
    n = len(gs)
    split = [g.ndim == 3 and g.shape[0] % 2 == 0 and g.size >= SPLIT_MIN_ELEMS for g in gs]
    part_shape = [((g.shape[0] // 2,) + g.shape[1:]) if s else g.shape for g, s in zip(gs, split)]
    n_split = sum(split)

    def body(*refs):
        g, out, sibs, chipb = refs[:n], refs[n:2 * n], refs[2 * n:3 * n], refs[3 * n:4 * n]
        send_sems, recv_sems = refs[4 * n:]
        x, y, c = _where_am_i()
        j = 2 * x + y
        sib = (x, y, 1 - c)
        chips = [(1 - x, y), (x, 1 - y), (1 - x, 1 - y)]

        def cp(k, src, dst, to):
            return pltpu.make_async_remote_copy(src_ref=src, dst_ref=dst, send_sem=send_sems.at[k], recv_sem=recv_sems.at[k],
                                                device_id=to, device_id_type=MESH)

        def part(a, h):
            return pl.ds(h * part_shape[a][0], part_shape[a][0]) if split[a] else Ellipsis

        def mine(ref, a, h):
            return ref.at[part(a, h)] if split[a] else ref

        swaps = [cp(a, g[a], sibs[a], sib) for a in range(n)]
        for d in swaps:
            d.start()
        for a in range(n):
            swaps[a].wait()
            chipb[a][j] = g[a][part(a, c)] + sibs[a][part(a, c)]
        sends = [cp(n + 3 * a + k, chipb[a].at[j], chipb[a].at[j], (cx, cy, c)) for a in range(n) for k, (cx, cy) in enumerate(chips)]
        for d in sends:
            d.start()
        for a in range(n):
            for k, (cx, cy) in enumerate(chips):
                blk = chipb[a].at[2 * cx + cy]
                cp(n + 3 * a + k, blk, blk, (cx, cy, c)).wait_recv()
            out[a][part(a, c)] = ((chipb[a][0] + chipb[a][1]) + chipb[a][2]) + chipb[a][3]
        for d in sends:
            d.wait_send()
        backs = [(a, cp(4 * n + i, mine(out[a], a, c), mine(out[a], a, c), sib)) for i, a in enumerate([a for a in range(n) if split[a]])]
        for _, d in backs:
            d.start()
        for i, (a, d) in enumerate(backs):
            d.wait_send()
            cp(4 * n + i, mine(out[a], a, 1 - c), mine(out[a], a, 1 - c), sib).wait_recv()

    vm = pl.BlockSpec(memory_space=pltpu.VMEM)
    nsem = 4 * n + n_split
    return pl.pallas_call(body, name="all_reduce_small_grads", in_specs=[vm] * n, out_specs=[vm] * n,
                          out_shape=[jax.ShapeDtypeStruct(g.shape, f32) for g in gs],
                          scratch_shapes=([pltpu.VMEM(g.shape, f32) for g in gs] + [pltpu.VMEM((4,) + ps, f32) for ps in part_shape]
                                          + [pltpu.SemaphoreType.DMA((nsem,)), pltpu.SemaphoreType.DMA((nsem,))]),
                          compiler_params=pltpu.CompilerParams(has_side_effects=True, vmem_limit_bytes=VMEM_LIMIT))(*gs)


def _adamw_many(ws, gs, ms, vs, name):
    n = len(ws)
    c1 = 1.0 / (1.0 - ADAM_B1 ** ADAM_STEP)
    c2 = 1.0 / (1.0 - ADAM_B2 ** ADAM_STEP)

    def body(*refs):
        for a in range(n):
            w_ref, g_ref, m_ref, v_ref, d_ref, nm_ref, nv_ref = (refs[i * n + a] for i in range(7))
            g_ = g_ref[...]
            nm = ADAM_B1 * m_ref[...] + (1.0 - ADAM_B1) * g_
            nv = ADAM_B2 * v_ref[...] + (1.0 - ADAM_B2) * (g_ * g_)
            nm_ref[...] = nm
            nv_ref[...] = nv
            d_ref[...] = -ADAM_LR * ((nm * c1) / (jnp.sqrt(nv * c2) + ADAM_EPS) + ADAM_WD * w_ref[...])

    vm = pl.BlockSpec(memory_space=pltpu.VMEM)
    sh = [jax.ShapeDtypeStruct(w.shape, f32) for w in ws]
    outs = pl.pallas_call(body, name=name, in_specs=[vm] * (4 * n), out_specs=[vm] * (3 * n), out_shape=sh * 3,
                          compiler_params=pltpu.CompilerParams(vmem_limit_bytes=VMEM_LIMIT))(*ws, *gs, *ms, *vs)
    return outs[:n], outs[n:2 * n], outs[2 * n:]


def _pack_flat(arrs, rows_mult):
    flat = jnp.concatenate([a.reshape(-1) for a in arrs])
    n = flat.shape[0]
    tot = -(-n // (rows_mult * LANE)) * rows_mult * LANE
    return jnp.pad(flat, (0, tot - n)).reshape(-1, LANE)


def _unpack_flat(buf, shapes):
    flat = buf.reshape(-1)
    out, o = [], 0
    for s in shapes:
        n = math.prod(s)
        out.append(flat[o:o + n].reshape(s))
        o += n
    return out


_BIG = ("mem_kv_w", "w_out", "a_w_in", "b_w_in", "c_w_in", "d_w_in")
SMALL_ROWS_MULT = 256


def _row8(v):
    v = v.reshape(-1, v.shape[-1])
    return jnp.pad(v, ((0, SUB - v.shape[0]), (0, 0)))


def kernel(x, mem, mem_kv_w, ln_g, ln_b, w_out, hgrn_lb_logits, a_w_in, a_w_s, a_b_s, b_w_in, b_norm_g, c_w_in, c_w_pool, c_scale, d_w_in, d_conv_w, d_conv_b, d_w_gx, d_b_gx, d_w_ga, d_b_ga, d_a_param, loss_target, m_mem_kv_w, m_ln_g, m_ln_b, m_w_out, m_hgrn_lb_logits, m_a_w_in, m_a_w_s, m_a_b_s, m_b_w_in, m_b_norm_g, m_c_w_in, m_c_w_pool, m_c_scale, m_d_w_in, m_d_conv_w, m_d_conv_b, m_d_w_gx, m_d_b_gx, m_d_w_ga, m_d_b_ga, m_d_a_param, v_mem_kv_w, v_ln_g, v_ln_b, v_w_out, v_hgrn_lb_logits, v_a_w_in, v_a_w_s, v_a_b_s, v_b_w_in, v_b_norm_g, v_c_w_in, v_c_w_pool, v_c_scale, v_d_w_in, v_d_conv_w, v_d_conv_b, v_d_w_gx, v_d_b_gx, v_d_w_ga, v_d_b_ga, v_d_a_param):
    names = ["mem_kv_w", "ln_g", "ln_b", "w_out", "hgrn_lb_logits", "a_w_in", "a_w_s", "a_b_s", "b_w_in", "b_norm_g", "c_w_in",
             "c_w_pool", "c_scale", "d_w_in", "d_conv_w", "d_conv_b", "d_w_gx", "d_b_gx", "d_w_ga", "d_b_ga", "d_a_param"]
    w = dict(mem_kv_w=mem_kv_w, ln_g=ln_g, ln_b=ln_b, w_out=w_out, hgrn_lb_logits=hgrn_lb_logits, a_w_in=a_w_in, a_w_s=a_w_s,
             a_b_s=a_b_s, b_w_in=b_w_in, b_norm_g=b_norm_g, c_w_in=c_w_in, c_w_pool=c_w_pool, c_scale=c_scale, d_w_in=d_w_in,
             d_conv_w=d_conv_w, d_conv_b=d_conv_b, d_w_gx=d_w_gx, d_b_gx=d_b_gx, d_w_ga=d_w_ga, d_b_ga=d_b_ga, d_a_param=d_a_param)
    m = dict(zip(names, [m_mem_kv_w, m_ln_g, m_ln_b, m_w_out, m_hgrn_lb_logits, m_a_w_in, m_a_w_s, m_a_b_s, m_b_w_in, m_b_norm_g,
                         m_c_w_in, m_c_w_pool, m_c_scale, m_d_w_in, m_d_conv_w, m_d_conv_b, m_d_w_gx, m_d_b_gx, m_d_w_ga,
                         m_d_b_ga, m_d_a_param]))
    v = dict(zip(names, [v_mem_kv_w, v_ln_g, v_ln_b, v_w_out, v_hgrn_lb_logits, v_a_w_in, v_a_w_s, v_a_b_s, v_b_w_in, v_b_norm_g,
                         v_c_w_in, v_c_w_pool, v_c_scale, v_d_w_in, v_d_conv_w, v_d_conv_b, v_d_w_gx, v_d_b_gx, v_d_w_ga,
                         v_d_b_ga, v_d_a_param]))
    xi, yi = lax.axis_index("x"), lax.axis_index("y")
    jshard = 2 * xi + yi
    x2 = x[0]
    mem2 = mem[0]
    tgt2 = loss_target[0]

    w_in_sh = [w[n][0].astype(bf16) for n in _BIG[2:]]
    w_out_sh = w_out.astype(bf16)
    gath0, tie = _ag_weights([mem_kv_w.astype(bf16), w_out_sh[0], w_in_sh[0]], ("rows", "win", "win"), jshard)
    w_kv = gath0[0].reshape(D_MODEL, 2 * XW)
    pending = [None]
    for l in range(1, DEPTH):
        bufs, ssem, rsem, tie = _ag_start([w_in_sh[l], w_out_sh[l]], jshard, f"gather_start{l}", tie)
        pending.append((bufs, ssem, rsem))
    tied_gain = {0: ln_g[0:1] + tie[0:1, 0:1]}

    def layer_weights(g_in, g_out):
        return (g_in.transpose(1, 0, 2).reshape(D_MODEL, -1), g_in.transpose(0, 2, 1).reshape(-1, D_MODEL),
                g_out.reshape(D_MODEL, D_MODEL), g_out.transpose(2, 0, 1).reshape(D_MODEL, D_MODEL))

    lw = [layer_weights(gath0[2], gath0[1])]

    def gather_small(shard):
        z = jnp.zeros((4, POOL_GROUP), f32)
        return lax.dynamic_update_slice(z, shard.reshape(1, POOL_GROUP), (jshard, 0))

    sm_sh = jnp.concatenate([gather_small(b_norm_g), gather_small(c_scale), gather_small(d_conv_b), gather_small(d_a_param)]
                            + [gather_small(d_conv_w[:, r]) for r in range(4)], axis=0)
    ci = lax.axis_index("c")
    sm_all = _all_reduce_small(_pack_flat([jnp.where(ci == 0, sm_sh, 0.0)], SUB * 2))
    sm = _unpack_flat(sm_all, [(8, 4 * POOL_GROUP)])[0]
    ng_full, scale_full, convb_full, ap_full = sm[0:1], sm[1:2], sm[2:3], sm[3:4]
    convw_full = sm[4:8]

    tril = jnp.tril(jnp.ones((HD, HD), bool))
    wtri = jnp.where(tril, a_w_s[0], 0.0)
    wbd = jnp.zeros((TOK, TOK), f32)
    for g in range(4):
        wbd = lax.dynamic_update_slice(wbd, c_w_pool[0, g], (g * POOL_GROUP, g * POOL_GROUP))
    kh, khT, vh, vhT, p_soft = _prep(mem2, w_kv, hgrn_lb_logits)
    prm = [
        dict(wtri=wtri.astype(bf16), wtriT=wtri.transpose(0, 2, 1).astype(bf16),
             bcolb=jnp.broadcast_to(a_b_s[0][:, :, None], (NH, HD, HD))),
        dict(lb=p_soft[1:2], ng=ng_full),
        dict(wbd=wbd.astype(bf16), wbdT=wbd.T.astype(bf16), scale=scale_full),
        dict(cw=_row8(convw_full), cb=convb_full, wgx=d_w_gx[0].astype(bf16), wgxT=d_w_gx[0].transpose(0, 2, 1).astype(bf16),
             bgx=d_b_gx.reshape(1, TOK), wga=d_w_ga[0].astype(bf16), wgaT=d_w_ga[0].transpose(0, 2, 1).astype(bf16),
             bga=d_b_ga.reshape(1, TOK), ap=ap_full),
    ]

    acts = []
    h = x2
    for l in range(DEPTH):
        if l:
            bufs, ssem, rsem = pending[l]
            lw.append(layer_weights(*_ag_wait(bufs, ssem, rsem, h, f"gather_wait{l}")))
        outs = _fwd_layer(l, h, lw[l][0], lw[l][2], tied_gain.get(l, ln_g[l:l + 1]), ln_b[l:l + 1], khT, vh, prm[l],
                          tgt2 if l == DEPTH - 1 else None)
        nfix = 4 if l == DEPTH - 1 else 3
        acts.append(dict(xin=h, proj=outs[1], z=outs[2], saves=outs[nfix:]))
        if l == DEPTH - 1:
            loss_part = outs[3]
        h = outs[0]
    loss = lax.psum(0.5 / D_MODEL * jnp.sum(loss_part), ("x", "y", "c"))

    dh = h
    dln = [None] * DEPTH
    dks, dvs = [None] * DEPTH, [None] * DEPTH
    sgr = [None] * DEPTH
    jc = jnp.stack([jshard, ci]).astype(jnp.int32)
    lkinds = ("win", "rows")
    q32s, flying, back_gain = [None] * DEPTH, [None] * DEPTH, {}
    for l in reversed(range(DEPTH)):
        a = acts[l]
        (dxin, dproj, mixedb, dyb, dln[l], dks[l], dvs[l]), sgr[l] = _bwd_layer(
            l, dh, a["z"], a["proj"], lw[l][1], lw[l][3], back_gain.get(l, ln_g[l:l + 1]), kh, khT, vh, vhT, prm[l], a["saves"])
        if _OFFS[l]["W"] // 4 % LANE:
            gw_in = _tn_gemm(a["xin"], dproj, f"grad_w_in{l}", 1).reshape(D_MODEL, 4, -1).transpose(1, 0, 2)
        else:
            gw_in = _tn_gemm_sharded(a["xin"], dproj, f"grad_w_in{l}")
        parts = [gw_in, _tn_gemm(mixedb, dyb, f"grad_w_out{l}", 1)]
        lk = lkinds
        if l == 0:
            parts.append(_kv_bwd(mem2, dks, dvs))
            lk = lkinds + ("rows",)
        gots = _rs_swap(parts, lk, f"rs_swap_halves{l}")
        sums = [_add_sibling(p, g, k, jc, f"rs_add_sibling{l}_{i}") for i, (p, g, k) in enumerate(zip(parts, gots, lk))]
        q32s[l] = [s[0] for s in sums]
        shapes = [p.shape for p in parts]
        if l:
            bufs, ssem, rsem, tok = _rs_owners_start([s[1] for s in sums], lk, shapes, f"rs_owners_start{l}")
            flying[l] = (bufs, ssem, rsem)
            back_gain[l - 1] = ln_g[l - 1:l] + tok[0:1, 0:1]
        else:
            last_got = _rs_owners([s[1] for s in sums], lk, shapes)
        dh = dxin
    grad_x = dh[None]
    fin, fin_kinds = {}, []
    for l in range(DEPTH):
        lk = lkinds + (("rows",) if l == 0 else ())
        got = last_got if l == 0 else _rs_owners_wait(*flying[l], lk, grad_x, f"rs_owners_wait{l}")
        fin[l] = [_add_chips(q, r, k, jc, f"rs_add_chips{l}_{i}") for i, (q, r, k) in enumerate(zip(q32s[l], got, lk))]
        fin_kinds += list(lk)
    joined = _rs_join([t for l in range(DEPTH) for t in fin[l]], tuple(fin_kinds))
    by_layer, o = [], 0
    for l in range(DEPTH):
        by_layer.append(joined[o:o + len(fin[l])])
        o += len(fin[l])
    gbig = {"mem_kv_w": by_layer[0][2], "w_out": jnp.stack([by_layer[l][1] for l in range(DEPTH)])}
    for l, n in enumerate(_BIG[2:]):
        gbig[n] = by_layer[l][0]
    g_sh, d_sh, m_sh, v_sh = {}, {}, {}, {}
    for n in _BIG:
        as2d = lambda t: t.reshape(-1, t.shape[-1])
        upd = _adamw(as2d(w[n]), as2d(gbig[n]), as2d(m[n]), as2d(v[n]), f"adamw_{n}")
        g_sh[n] = gbig[n].reshape(w[n].shape)
        d_sh[n], m_sh[n], v_sh[n] = (u.reshape(w[n].shape) for u in upd)

    dbs, dlogits = _small_finish(sgr[0]["dbacc"], p_soft, sgr[1]["dlb"])
    gs = {
        "ln_g": jnp.concatenate([dln[l][0:1] for l in range(DEPTH)], axis=0),
        "ln_b": jnp.concatenate([dln[l][1:2] for l in range(DEPTH)], axis=0),
        "hgrn_lb_logits": dlogits,
        "a_w_s": sgr[0]["dwtri"][None],
        "a_b_s": dbs[:, 0:NH].T[None],
        "b_norm_g": sgr[1]["dng"][0:1],
        "c_w_pool": jnp.stack([sgr[2]["dwbd"][g * POOL_GROUP:(g + 1) * POOL_GROUP, g * POOL_GROUP:(g + 1) * POOL_GROUP]
                               for g in range(4)])[None],
        "c_scale": sgr[2]["dscale"][0:1],
        "d_conv_w": sgr[3]["dcw"][0:4][None],
        "d_conv_b": sgr[3]["dvec"][3:4],
        "d_w_gx": sgr[3]["dwgx"][None],
        "d_b_gx": sgr[3]["dvec"][1:2].reshape(1, NH, HD),
        "d_w_ga": sgr[3]["dwga"][None],
        "d_b_ga": sgr[3]["dvec"][2:3].reshape(1, NH, HD),
        "d_a_param": sgr[3]["dvec"][0:1],
    }
    small = [n for n in names if n not in _BIG]
    drop1 = lambda t: t.reshape(t.shape[1:]) if t.ndim > 2 and t.shape[0] == 1 else t
    gsum = dict(zip(small, _all_reduce_many([drop1(gs[n]) for n in small])))
    for n in ("b_norm_g", "c_scale", "d_conv_b", "d_a_param"):
        gsum[n] = lax.dynamic_slice(gsum[n], (0, jshard * POOL_GROUP), (1, POOL_GROUP))
    gsum["d_conv_w"] = lax.dynamic_slice(gsum["d_conv_w"], (0, jshard * POOL_GROUP), (4, POOL_GROUP))
    upd = _adamw_many(*[[drop1(d[n]) for n in small] for d in (w, gsum, m, v)], "adamw_small")
    gsum = {n: gsum[n].reshape(w[n].shape) for n in small}
    d_sm, m_sm, v_sm = ({n: u.reshape(w[n].shape) for n, u in zip(small, us)} for us in upd)

    grads = {**gsum, **g_sh}
    deltas = {**d_sm, **d_sh}
    new_m = {**m_sm, **m_sh}
    new_v = {**v_sm, **v_sh}
    return (loss, grad_x, *[grads[n] for n in names], *[deltas[n] for n in names], *[new_m[n] for n in names],
            *[new_v[n] for n in names])
```

```python
import functools
import math

import jax
import jax.numpy as jnp
from jax import lax
from jax.experimental import pallas as pl
from jax.experimental.pallas import tpu as pltpu

f32 = jnp.float32
bf16 = jnp.bfloat16
MM = bf16

D_MODEL = 1024
TOK = 768
XW = 256
XHEADS = 4
XDIM = 64
HD = 128
NH = TOK // HD
CHUNK = 16
POOL_GROUP = 192
DEPTH = 4
ALPHA = (2 * DEPTH) ** 0.25
LN_EPS = 1e-5
RMS_EPS = 1e-6
LRU_C = 8.0
ADAM_LR, ADAM_B1, ADAM_B2, ADAM_EPS, ADAM_WD, ADAM_STEP = 0.001, 0.9, 0.999, 1e-08, 0.01, 10

_TS = (256, 256, 256, 256)
HGRN_SUB = 128
TK = 512
SUB = 8
LANE = 128
VMEM_LIMIT = 58 * 1024 * 1024

_OFFS = (
    dict(u=0, v=768, qx=1536, gate=1792, W=2816),
    dict(q=0, f=768, i=1536, qx=2304, gate=2560, W=3584),
    dict(p=0, qx=768, gate=1024, W=2048),
    dict(xb=0, qx=768, gate=1024, W=2048),
)
_PRM = (
    ("wtri", "wtriT", "bcolb"),
    ("lb", "ng"),
    ("wbd", "wbdT", "scale"),
    ("cw", "cb", "wgx", "wgxT", "bgx", "wga", "wgaT", "bga", "ap"),
)
MESH = pl.DeviceIdType.MESH


def _mm(a, b):
    return jnp.dot(a.astype(MM), b.astype(MM), preferred_element_type=f32)


def _mm_nt(a, b):
    return lax.dot_general(a.astype(MM), b.astype(MM), (((1,), (1,)), ((), ())), preferred_element_type=f32)


def _mm_tn(a, b):
    return lax.dot_general(a.astype(MM), b.astype(MM), (((0,), (0,)), ((), ())), preferred_element_type=f32)


def _mm_sel(sel, b):
    s = sel.astype(bf16)
    hi = b.astype(bf16)
    lo = (b - hi.astype(f32)).astype(bf16)
    return jnp.dot(s, hi, preferred_element_type=f32) + jnp.dot(s, lo, preferred_element_type=f32)


def _sig(x):
    return jax.nn.sigmoid(x)


_GC = math.sqrt(2.0 / math.pi)


def _gelu(x):
    t = jnp.tanh(_GC * (x + 0.044715 * x * x * x))
    return 0.5 * x * (1.0 + t), t


def _gelu_grad(x, t):
    return 0.5 * (1.0 + t) + 0.5 * x * (1.0 - t * t) * _GC * (1.0 + 3.0 * 0.044715 * x * x)


def _rowsum(x):
    return jnp.sum(x, axis=0, keepdims=True)


def _lmean(x):
    return jnp.mean(x, axis=-1, keepdims=True)


def _ln(z):
    mu = _lmean(z)
    zc = z - mu
    rstd = lax.rsqrt(_lmean(zc * zc) + LN_EPS)
    return zc * rstd, rstd


def _ln_bwd(dxh, xhat, rstd):
    return rstd * (dxh - _lmean(dxh) - xhat * _lmean(dxh * xhat))


def _hs(h):
    return slice(h * HD, (h + 1) * HD)


def _expm1(x):
    small = x * (1.0 + x * 0.5 * (1.0 + x * (1.0 / 3.0) * (1.0 + x * 0.25 * (1.0 + x * 0.2 * (1.0 + x * (1.0 / 6.0))))))
    return jnp.where(jnp.abs(x) < 0.25, small, jnp.exp(x) - 1.0)


def _softplus(x):
    e = jnp.exp(-jnp.abs(x))
    l1p = jnp.where(e < 1e-4, e - 0.5 * e * e, jnp.log(1.0 + e))
    return jnp.maximum(x, 0.0) + l1p


def _scan_fwd(a, b):
    n = a.shape[0]
    row = lax.broadcasted_iota(jnp.int32, a.shape, 0)
    d = 1
    while d < n:
        if d % SUB:
            m = row >= d
            b = jnp.where(m, a * pltpu.roll(b, d, 0) + b, b)
            a = jnp.where(m, a * pltpu.roll(a, d, 0), a)
        else:
            b = a * jnp.concatenate([jnp.zeros((d,) + b.shape[1:], f32), b[:n - d]], axis=0) + b
            a = a * jnp.concatenate([jnp.ones((d,) + a.shape[1:], f32), a[:n - d]], axis=0)
        d *= 2
    return a, b


def _scan_bwd(a, b):
    n = a.shape[0]
    row = lax.broadcasted_iota(jnp.int32, a.shape, 0)
    d = 1
    while d < n:
        if d % SUB:
            m = row < n - d
            b = jnp.where(m, a * pltpu.roll(b, n - d, 0) + b, b)
            a = jnp.where(m, a * pltpu.roll(a, n - d, 0), a)
        else:
            b = a * jnp.concatenate([b[d:], jnp.zeros((d,) + b.shape[1:], f32)], axis=0) + b
            a = a * jnp.concatenate([a[d:], jnp.ones((d,) + a.shape[1:], f32)], axis=0)
        d *= 2
    return a, b


def _chunk_mats(n):
    r = lax.broadcasted_iota(jnp.int32, (n, n), 0)
    c = lax.broadcasted_iota(jnp.int32, (n, n), 1)
    same = (r // CHUNK) == (c // CHUNK)
    return same, jnp.logical_and(same, c <= r)


def _pool_w(shape):
    lane = lax.broadcasted_iota(jnp.int32, shape, 1)
    return jnp.where(lane < POOL_GROUP, 2, jnp.where(lane < 2 * POOL_GROUP, 4, jnp.where(lane < 3 * POOL_GROUP, 8, 16)))


def _pool_pick(r1, r2, r3, r4):
    lane = lax.broadcasted_iota(jnp.int32, r1.shape, 1)
    return jnp.where(lane < POOL_GROUP, r1, jnp.where(lane < 2 * POOL_GROUP, r2, jnp.where(lane < 3 * POOL_GROUP, r3, r4)))


def _const_spec(a):
    nd = a.ndim
    return pl.BlockSpec(a.shape, lambda i, _nd=nd: (0,) * _nd, pipeline_mode=pl.Buffered(1))


def _acc_spec(shape):
    nd = len(shape)
    return pl.BlockSpec(shape, lambda i, _nd=nd: (0,) * _nd)


def _params(sem="arbitrary"):
    return pltpu.CompilerParams(dimension_semantics=(sem,), vmem_limit_bytes=VMEM_LIMIT)


def _xattn_fwd(qx, khT_ref, vh_ref):
    xo = jnp.zeros((qx.shape[0], XW), f32)
    ps = []
    for h in range(XHEADS):
        s = _mm(qx, khT_ref[h]) * (XDIM ** -0.5)
        e = jnp.exp(s - jnp.max(s, axis=-1, keepdims=True))
        p = e / jnp.sum(e, axis=-1, keepdims=True)
        xo = xo + _mm(p, vh_ref[h])
        ps.append(p)
    return xo, ps


def _hgrn_parallel(q_raw, fl, lb):
    n = q_raw.shape[0]
    same, tri = _chunk_mats(n)
    sq = _sig(q_raw)
    qf = q_raw * sq
    sgm = _sig(fl)
    f = lb + (1.0 - lb) * sgm
    logf = jnp.log(f)
    k = 1.0 - f
    g = _mm_sel(tri, logf)
    gl = _mm_sel(same, logf)
    eg = jnp.exp(g)
    eng = jnp.exp(-g)
    ee = jnp.exp(gl - g)
    return dict(sq=sq, qf=qf, sgm=sgm, f=f, k=k, eg=eg, eng=eng, ee=ee, q_dec=qf * eg, k_inv=k * eng, k_end=k * ee,
                a=jnp.exp(gl))


def _hgrn_intra(q_dec, k_inv, v):
    n = q_dec.shape[0]
    _, tri = _chunk_mats(HD)
    outs = []
    for h in range(NH):
        blks = []
        for b in range(n // HD):
            rs = slice(b * HD, (b + 1) * HD)
            sc = jnp.where(tri, _mm_nt(q_dec[rs, _hs(h)], k_inv[rs, _hs(h)]), 0.0)
            blks.append(_mm(sc, v[rs, _hs(h)]))
        outs.append(jnp.concatenate(blks, axis=0))
    return jnp.concatenate(outs, axis=-1)


def _cs(c):
    return slice(c * CHUNK, (c + 1) * CHUNK)


def _hgrn_inter_fwd(qdec_s, kend_s, v_s, a_s, oint_s, st_ref, states_s, u_s):
    n = qdec_s.shape[0] // CHUNK
    for c in range(n):
        for h in range(NH):
            u_s[c, h] = _mm_tn(v_s[_cs(c), _hs(h)], kend_s[_cs(c), _hs(h)])
    for h in range(NH):
        st = st_ref[h]
        for c in range(n):
            states_s[c, h] = st
            st = st * a_s[c * CHUNK:c * CHUNK + 1, _hs(h)] + u_s[c, h]
        st_ref[h] = st
    if oint_s is None:
        return
    for c in range(n):
        for h in range(NH):
            oint_s[_cs(c), _hs(h)] = _mm_nt(qdec_s[_cs(c), _hs(h)], states_s[c, h])


def _rms(o):
    outs, rs = [], []
    for h in range(NH):
        oh = o[:, _hs(h)]
        r = lax.rsqrt(_lmean(oh * oh) + RMS_EPS)
        outs.append(oh * r)
        rs.append(r)
    return jnp.concatenate(outs, axis=-1), rs


def _gmlp_core(u_raw, v_raw, wtri_ref, bcolb_ref):
    gu, tu = _gelu(u_raw)
    gv, tv = _gelu(v_raw)
    vns, rstds, mixeds = [], [], []
    for h in range(NH):
        vn, rstd = _ln(gv[:, _hs(h)])
        blks = []
        for n in range(u_raw.shape[0] // HD):
            blks.append(_mm(wtri_ref[h], vn[n * HD:(n + 1) * HD]) + bcolb_ref[h])
        vns.append(vn)
        rstds.append(rstd)
        mixeds.append(jnp.concatenate(blks, axis=0))
    mixed = jnp.concatenate(mixeds, axis=-1)
    return gu, tu, tv, vns, rstds, mixed


def _pool_core(p, carry, row0, wbd_ref):
    ext = jnp.concatenate([carry, p], axis=0)
    r1 = ext + pltpu.roll(ext, 1, 0)
    r2 = r1 + pltpu.roll(r1, 2, 0)
    r3 = r2 + pltpu.roll(r2, 4, 0)
    r4 = r3 + pltpu.roll(r3, 8, 0)
    sel = _pool_pick(r1, r2, r3, r4)[2 * SUB:]
    grow = row0 + lax.broadcasted_iota(jnp.int32, p.shape, 0)
    inv_cnt = 1.0 / jnp.minimum(grow + 1, _pool_w(p.shape)).astype(f32)
    diff = sel * inv_cnt - p
    return diff, inv_cnt, _mm(diff, wbd_ref[...])


def _lru_core(xb, ccar, row0, p):
    ext = jnp.concatenate([ccar, xb], axis=0)
    cw = p["cw"]
    x1, x2, x3 = pltpu.roll(ext, 1, 0)[SUB:], pltpu.roll(ext, 2, 0)[SUB:], pltpu.roll(ext, 3, 0)[SUB:]
    xc = cw[3:4, :] * xb + cw[2:3, :] * x1 + cw[1:2, :] * x2 + cw[0:1, :] * x3 + p["cb"][...]
    gxs, gas = [], []
    for h in range(NH):
        gxs.append(_mm(xc[:, _hs(h)], p["wgx"][h]))
        gas.append(_mm(xc[:, _hs(h)], p["wga"][h]))
    gx = _sig(jnp.concatenate(gxs, axis=-1) + p["bgx"][...])
    ga = _sig(jnp.concatenate(gas, axis=-1) + p["bga"][...])
    sp = _softplus(-p["ap"][...])
    la = -LRU_C * ga * sp
    a = jnp.exp(la)
    grow = row0 + lax.broadcasted_iota(jnp.int32, xb.shape, 0)
    first = grow == 0
    mult = jnp.where(first, 1.0, jnp.sqrt(-_expm1(2.0 * la)))
    bt = mult * gx * xc
    return dict(x1=x1, x2=x2, x3=x3, xc=xc, gx=gx, ga=ga, sp=sp, a=a, mult=mult, bt=bt, first=first)


def _fwd_layer(kind, xin, w_in, w_out, lng, lnb, khT, vh, prm, tgt):
    S = xin.shape[0]
    TS = _TS[kind]
    nt = S // TS
    off = _OFFS[kind]
    W = off["W"]
    last = tgt is not None
    pnames = _PRM[kind]
    pvals = [prm[n] for n in pnames]

    def body(*refs):
        it = iter(refs)
        xin_ref, win_ref, wout_ref, lng_ref, lnb_ref, khT_ref, vh_ref = (next(it) for _ in range(7))
        p = {n: next(it) for n in pnames}
        tgt_ref = next(it) if last else None
        xout_ref, proj_ref, z_ref = next(it), next(it), next(it)
        loss_ref = next(it) if last else None
        rest = list(it)
        i = pl.program_id(0)
        x = xin_ref[...]
        proj_ref[...] = _mm(x, win_ref[...])

        if kind == 0:
            gu, _, _, _, _, mixed = _gmlp_core(proj_ref[:, 0:TOK], proj_ref[:, TOK:2 * TOK], p["wtri"], p["bcolb"])
            tok = gu * mixed
        elif kind == 1:
            st_save, o_save, st_ref, states_s, u_s, qdec_s, kend_s, v_s, a_s, oint_s = rest

            @pl.when(i == 0)
            def _():
                st_ref[...] = jnp.zeros_like(st_ref)

            st_save[0, 0] = st_ref[...]
            v = proj_ref[:, 2 * TOK:3 * TOK]
            hp = _hgrn_parallel(proj_ref[:, 0:TOK], proj_ref[:, TOK:2 * TOK], p["lb"][...])
            qdec_s[...] = hp["q_dec"]
            kend_s[...] = hp["k_end"]
            v_s[...] = v
            a_s[...] = hp["a"]
            o_intra = _hgrn_intra(hp["q_dec"], hp["k_inv"], v)
            _hgrn_inter_fwd(qdec_s, kend_s, v_s, a_s, oint_s, st_ref, states_s, u_s)
            o = o_intra + oint_s[...]
            o_save[0] = o
            for sub in range(1, TS // HGRN_SUB):
                st_save[0, sub] = states_s[sub * HGRN_SUB // CHUNK]
            on, _ = _rms(o)
            tok = on * p["ng"][...]
        elif kind == 2:
            pc_save, pcar = rest

            @pl.when(i == 0)
            def _():
                pcar[...] = jnp.zeros_like(pcar)

            pc_save[0] = pcar[...]
            pp = proj_ref[:, 0:TOK]
            _, _, y = _pool_core(pp, pcar[...], i * TS, p["wbd"])
            pcar[...] = pp[TS - 2 * SUB:, :]
            tok = y * p["scale"][...]
        else:
            cc_save, hc_save, h_save, ccar, hcar = rest

            @pl.when(i == 0)
            def _():
                ccar[...] = jnp.zeros_like(ccar)
                hcar[...] = jnp.zeros_like(hcar)

            cc_save[0] = ccar[...]
            hc_save[0] = hcar[...]
            xb = proj_ref[:, 0:TOK]
            lc = _lru_core(xb, ccar[...], i * TS, p)
            P, B = _scan_fwd(lc["a"], lc["bt"])
            tok = P * hcar[SUB - 1:SUB, :] + B
            h_save[0] = tok
            ccar[...] = xb[TS - SUB:, :]
            hcar[...] = tok[TS - SUB:, :]

        xo, _ = _xattn_fwd(proj_ref[:, off["qx"]:off["qx"] + XW], khT_ref, vh_ref)
        gate = proj_ref[:, off["gate"]:off["gate"] + D_MODEL]
        mixed = jnp.concatenate([tok, xo], axis=-1) * (gate * _sig(gate))
        z = ALPHA * x + _mm(mixed, wout_ref[...])
        z_ref[...] = z
        xhat, _ = _ln(z)
        xout = xhat * lng_ref[...] + lnb_ref[...]
        if last:
            e = xout - tgt_ref[...]
            xout_ref[...] = e * (1.0 / D_MODEL)
            es = _rowsum(e * e)
            tot = es[:, 0:LANE]
            for j in range(1, D_MODEL // LANE):
                tot = tot + es[:, j * LANE:(j + 1) * LANE]

            @pl.when(i == 0)
            def _():
                loss_ref[...] = jnp.zeros_like(loss_ref)

            loss_ref[0:1, :] += tot
        else:
            xout_ref[...] = xout

    tile = lambda w: pl.BlockSpec((TS, w), lambda i: (i, 0))
    in_arrays = [xin, w_in, w_out, lng, lnb, khT, vh] + pvals + ([tgt] if last else [])
    in_specs = [tile(D_MODEL)] + [_const_spec(a) for a in in_arrays[1:7 + len(pvals)]] + ([tile(D_MODEL)] if last else [])
    out_shape = [jax.ShapeDtypeStruct((S, D_MODEL), f32), jax.ShapeDtypeStruct((S, W), f32), jax.ShapeDtypeStruct((S, D_MODEL), f32)]
    out_specs = [tile(D_MODEL), tile(W), tile(D_MODEL)]
    if last:
        out_shape.append(jax.ShapeDtypeStruct((SUB, LANE), f32))
        out_specs.append(_acc_spec((SUB, LANE)))
    scratch = []
    save = lambda *s: (jax.ShapeDtypeStruct((nt,) + s, f32), pl.BlockSpec((1,) + s, lambda i, _n=len(s): (i,) + (0,) * _n))
    if kind == 1:
        saved = [save(TS // HGRN_SUB, NH, HD, HD), save(TS, TOK)]
        scratch = ([pltpu.VMEM((NH, HD, HD), f32)] + [pltpu.VMEM((TS // CHUNK, NH, HD, HD), f32)] * 2
                   + [pltpu.VMEM((TS, TOK), f32)] * 5)
    elif kind == 2:
        saved = [save(2 * SUB, TOK)]
        scratch = [pltpu.VMEM((2 * SUB, TOK), f32)]
    elif kind == 3:
        saved = [save(SUB, TOK), save(SUB, TOK), save(TS, TOK)]
        scratch = [pltpu.VMEM((SUB, TOK), f32)] * 2
    else:
        saved = []
    for sh, sp in saved:
        out_shape.append(sh)
        out_specs.append(sp)
    return pl.pallas_call(body, name=f"fwd_layer{kind}", grid=(nt,), in_specs=in_specs, out_specs=out_specs,
                          out_shape=out_shape, scratch_shapes=scratch, compiler_params=_params())(*in_arrays)


def _small_grad_shapes(kind):
    if kind == 0:
        return dict(dwtri=(NH, HD, HD), dbacc=(NH, HD, HD))
    if kind == 1:
        return dict(dlb=(SUB, TOK), dng=(SUB, TOK))
    if kind == 2:
        return dict(dwbd=(TOK, TOK), dscale=(SUB, TOK))
    return dict(dcw=(SUB, TOK), dvec=(SUB, TOK), dwgx=(NH, HD, HD), dwga=(NH, HD, HD))


def _bwd_layer(kind, dxout, z, proj, w_inT, w_outT, lng, kh, khT, vh, vhT, prm, saves):
    S = dxout.shape[0]
    TS = _TS[kind]
    nt = S // TS
    off = _OFFS[kind]
    W = off["W"]
    pnames = _PRM[kind]
    pvals = [prm[n] for n in pnames]
    sg_shapes = _small_grad_shapes(kind)
    sg_names = list(sg_shapes)
    n_saves = len(saves)

    def body(*refs):
        it = iter(refs)
        dxo_ref, z_ref, proj_ref, winT_ref, woutT_ref, lng_ref, kh_ref, khT_ref, vh_ref, vhT_ref = (next(it) for _ in range(10))
        p = {n: next(it) for n in pnames}
        sv = [next(it) for _ in range(n_saves)]
        dxin_ref, dproj_ref, mixed_ref, dy_ref, dln_ref, dk_ref, dv_ref = (next(it) for _ in range(7))
        sg = {n: next(it) for n in sg_names}
        rest = list(it)
        step = pl.program_id(0)
        i = nt - 1 - step

        @pl.when(step == 0)
        def _():
            dln_ref[...] = jnp.zeros_like(dln_ref)
            dk_ref[...] = jnp.zeros_like(dk_ref)
            dv_ref[...] = jnp.zeros_like(dv_ref)
            for n in sg_names:
                sg[n][...] = jnp.zeros_like(sg[n])

        dxo = dxo_ref[...]
        xhat, rstd = _ln(z_ref[...])
        dln_ref[0:1, :] += _rowsum(dxo * xhat)
        dln_ref[1:2, :] += _rowsum(dxo)
        dz = _ln_bwd(dxo * lng_ref[...], xhat, rstd)
        dyb = dz.astype(bf16)
        dy_ref[...] = dyb
        dmixed = _mm(dyb, woutT_ref[...])

        aux = {}
        if kind == 0:
            u_raw, v_raw = proj_ref[:, 0:TOK], proj_ref[:, TOK:2 * TOK]
            gu, tu, tv, vns, rstds, mx = _gmlp_core(u_raw, v_raw, p["wtri"], p["bcolb"])
            tok = gu * mx
        elif kind == 1:
            st_save, o_save = sv
            (dst_ref, fst_ref, states_s, dsts_s, u_s, qdec_s, kend_s, v_s, a_s, do_s, dqdec_s, dkend_s, dv_s,
             dgl_s) = rest

            @pl.when(step == 0)
            def _():
                dst_ref[...] = jnp.zeros_like(dst_ref)

            o = o_save[0]
            on, rs = _rms(o)
            tok = on * p["ng"][...]
            aux = dict(o=o, on=on, rs=rs)
        elif kind == 2:
            pc_save, = sv
            dpcar, = rest
            pp = proj_ref[:, 0:TOK]
            diff, inv_cnt, y = _pool_core(pp, pc_save[0], i * TS, p["wbd"])
            tok = y * p["scale"][...]
        else:
            cc_save, hc_save, h_save = sv
            dccar, gcar = rest
            xb = proj_ref[:, 0:TOK]
            lc = _lru_core(xb, cc_save[0], i * TS, p)
            hin = hc_save[0, SUB - 1:SUB, :]
            tok = h_save[0]

        xo, ps = _xattn_fwd(proj_ref[:, off["qx"]:off["qx"] + XW], khT_ref, vh_ref)
        gate = proj_ref[:, off["gate"]:off["gate"] + D_MODEL]
        sgm = _sig(gate)
        sgate = gate * sgm
        cat = jnp.concatenate([tok, xo], axis=-1)
        mixed_ref[...] = (cat * sgate).astype(bf16)
        dcat = dmixed * sgate
        dproj_ref[:, off["gate"]:off["gate"] + D_MODEL] = (dmixed * cat * (sgm * (1.0 + gate * (1.0 - sgm)))).astype(bf16)
        dtok = dcat[:, 0:TOK]
        dxo_att = dcat[:, TOK:]

        qx = proj_ref[:, off["qx"]:off["qx"] + XW]
        dqx = jnp.zeros((TS, XW), f32)
        for h in range(XHEADS):
            dp = _mm(dxo_att, vhT_ref[h])
            ds = ps[h] * (dp - jnp.sum(dp * ps[h], axis=-1, keepdims=True)) * (XDIM ** -0.5)
            dqx = dqx + _mm(ds, kh_ref[h])
            dk_ref[h] += _mm_tn(ds, qx)
            dv_ref[h] += _mm_tn(ps[h], dxo_att)
        dproj_ref[:, off["qx"]:off["qx"] + XW] = dqx.astype(bf16)

        if kind == 0:
            tril = lax.broadcasted_iota(jnp.int32, (HD, HD), 1) <= lax.broadcasted_iota(jnp.int32, (HD, HD), 0)
            dgu = dtok * mx
            dmx = dtok * gu
            dgvs = []
            for h in range(NH):
                dmh = dmx[:, _hs(h)]
                blks = []
                for n in range(TS // HD):
                    rs_ = slice(n * HD, (n + 1) * HD)
                    blks.append(_mm(p["wtriT"][h], dmh[rs_]))
                    sg["dwtri"][h] += jnp.where(tril, _mm_nt(dmh[rs_], vns[h][rs_]), 0.0)
                    sg["dbacc"][h] += dmh[rs_]
                dgvs.append(_ln_bwd(jnp.concatenate(blks, axis=0), vns[h], rstds[h]))
            dgv = jnp.concatenate(dgvs, axis=-1)
            dproj_ref[:, 0:TOK] = (dgu * _gelu_grad(u_raw, tu)).astype(bf16)
            dproj_ref[:, TOK:2 * TOK] = (dgv * _gelu_grad(v_raw, tv)).astype(bf16)
        elif kind == 1:
            o, on, rs = aux["o"], aux["on"], aux["rs"]
            ng = p["ng"][...]
            lb = p["lb"][...]
            sg["dng"][0:1, :] += _rowsum(dtok * on)
            dn = dtok * ng
            dos = []
            for h in range(NH):
                oh, r = o[:, _hs(h)], rs[h]
                dos.append(r * (dn[:, _hs(h)] - oh * (r * r) * _lmean(dn[:, _hs(h)] * oh)))
            do_all = jnp.concatenate(dos, axis=-1)
            _, tri = _chunk_mats(HD)
            same, _ = _chunk_mats(HGRN_SUB)
            triT = jnp.logical_and(same, lax.broadcasted_iota(jnp.int32, (HGRN_SUB, HGRN_SUB), 1)
                                   >= lax.broadcasted_iota(jnp.int32, (HGRN_SUB, HGRN_SUB), 0))
            row16 = lax.broadcasted_iota(jnp.int32, (CHUNK, HD), 0)
            nch = HGRN_SUB // CHUNK
            for sub in reversed(range(TS // HGRN_SUB)):
                rr = slice(sub * HGRN_SUB, (sub + 1) * HGRN_SUB)
                q_raw, v = proj_ref[rr, 0:TOK], proj_ref[rr, 2 * TOK:3 * TOK]
                hp = _hgrn_parallel(q_raw, proj_ref[rr, TOK:2 * TOK], lb)
                qdec_s[...] = hp["q_dec"]
                kend_s[...] = hp["k_end"]
                v_s[...] = v
                a_s[...] = hp["a"]
                fst_ref[...] = st_save[0, sub]
                _hgrn_inter_fwd(qdec_s, kend_s, v_s, a_s, None, fst_ref, states_s, u_s)
                do = do_all[rr]
                do_s[...] = do
                dqd, dki, dvi = [], [], []
                for h in range(NH):
                    bq, bk, bv = [], [], []
                    for b in range(HGRN_SUB // HD):
                        rs_ = slice(b * HD, (b + 1) * HD)
                        qd, ki = hp["q_dec"][rs_, _hs(h)], hp["k_inv"][rs_, _hs(h)]
                        sc = jnp.where(tri, _mm_nt(qd, ki), 0.0)
                        dsc = jnp.where(tri, _mm_nt(do[rs_, _hs(h)], v[rs_, _hs(h)]), 0.0)
                        bv.append(_mm_tn(sc, do[rs_, _hs(h)]))
                        bq.append(_mm(dsc, ki))
                        bk.append(_mm_tn(dsc, qd))
                    dqd.append(jnp.concatenate(bq, axis=0))
                    dki.append(jnp.concatenate(bk, axis=0))
                    dvi.append(jnp.concatenate(bv, axis=0))
                dqdec_s[...] = jnp.concatenate(dqd, axis=-1)
                dk_inv = jnp.concatenate(dki, axis=-1)
                dv_s[...] = jnp.concatenate(dvi, axis=-1)
                for c in range(nch):
                    for h in range(NH):
                        u_s[c, h] = _mm_tn(do_s[_cs(c), _hs(h)], qdec_s[_cs(c), _hs(h)])
                for h in range(NH):
                    dst = dst_ref[h]
                    for c in reversed(range(nch)):
                        dsts_s[c, h] = dst
                        dst = dst * a_s[c * CHUNK:c * CHUNK + 1, _hs(h)] + u_s[c, h]
                    dst_ref[h] = dst
                for c in range(nch):
                    for h in range(NH):
                        stp = states_s[c, h]
                        dst = dsts_s[c, h]
                        dqdec_s[_cs(c), _hs(h)] += _mm(do_s[_cs(c), _hs(h)], stp)
                        dkend_s[_cs(c), _hs(h)] = _mm(v_s[_cs(c), _hs(h)], dst)
                        dv_s[_cs(c), _hs(h)] += _mm_nt(kend_s[_cs(c), _hs(h)], dst)
                        da = jnp.sum(dst * stp, axis=0, keepdims=True) * a_s[c * CHUNK:c * CHUNK + 1, _hs(h)]
                        dgl_s[_cs(c), _hs(h)] = jnp.where(row16 == 0, jnp.broadcast_to(da, (CHUNK, HD)), 0.0)
                dq_dec = dqdec_s[...]
                dk_end = dkend_s[...]
                dg = dq_dec * hp["q_dec"] - dk_inv * hp["k_inv"] - dk_end * hp["k_end"]
                dk = dk_inv * hp["eng"] + dk_end * hp["ee"]
                dglr = dk_end * hp["k_end"] + dgl_s[...]
                dlogf = _mm_sel(triT, dg) + _mm_sel(same, dglr)
                df = dlogf / hp["f"] - dk
                sg["dlb"][0:1, :] += _rowsum(df * (1.0 - hp["sgm"]))
                dproj_ref[rr, 0:TOK] = (dq_dec * hp["eg"] * (hp["sq"] * (1.0 + q_raw * (1.0 - hp["sq"])))).astype(bf16)
                dproj_ref[rr, TOK:2 * TOK] = (df * (1.0 - lb) * hp["sgm"] * (1.0 - hp["sgm"])).astype(bf16)
                dproj_ref[rr, 2 * TOK:3 * TOK] = dv_s[...].astype(bf16)
        elif kind == 2:
            @pl.when(step == 0)
            def _():
                dpcar[...] = jnp.zeros_like(dpcar)

            sg["dscale"][0:1, :] += _rowsum(dtok * y)
            dyp = dtok * p["scale"][...]
            sg["dwbd"][...] += _mm_tn(diff, dyp)
            ddiff = _mm(dyp, p["wbdT"][...])
            q = ddiff * inv_cnt
            ext = jnp.concatenate([q, dpcar[...]], axis=0)
            n = TS + 2 * SUB
            r1 = ext + pltpu.roll(ext, n - 1, 0)
            r2 = r1 + pltpu.roll(r1, n - 2, 0)
            r3 = r2 + pltpu.roll(r2, n - 4, 0)
            r4 = r3 + pltpu.roll(r3, n - 8, 0)
            dproj_ref[:, 0:TOK] = (_pool_pick(r1, r2, r3, r4)[:TS] - ddiff).astype(bf16)
            dpcar[...] = q[0:2 * SUB, :]
        else:
            @pl.when(step == 0)
            def _():
                dccar[...] = jnp.zeros_like(dccar)
                gcar[...] = jnp.zeros_like(gcar)

            a, mult, gx, ga, xc = lc["a"], lc["mult"], lc["gx"], lc["ga"], lc["xc"]
            row = lax.broadcasted_iota(jnp.int32, (TS, TOK), 0)
            an = jnp.where(row == TS - 1, 1.0, pltpu.roll(a, TS - 1, 0))
            Pb, Bb = _scan_bwd(an, dtok)
            lam = Pb * gcar[0:1, :] + Bb
            gcar[...] = (a * lam)[0:SUB, :]
            hprev = jnp.where(row == 0, jnp.broadcast_to(hin, (TS, TOK)), pltpu.roll(tok, 1, 0))
            dmult = lam * gx * xc
            dgx = lam * mult * xc
            dxc = lam * mult * gx
            dla = lam * hprev * a - jnp.where(lc["first"], 0.0, dmult * a * a / mult)
            sp = lc["sp"]
            dga = -LRU_C * sp * dla
            dsp = _rowsum(-LRU_C * ga * dla)
            sg["dvec"][0:1, :] += dsp * (-_sig(-p["ap"][...]))
            dpx = dgx * gx * (1.0 - gx)
            dpa = dga * ga * (1.0 - ga)
            sg["dvec"][1:2, :] += _rowsum(dpx)
            sg["dvec"][2:3, :] += _rowsum(dpa)
            dxcs = []
            for h in range(NH):
                dxcs.append(_mm(dpx[:, _hs(h)], p["wgxT"][h]) + _mm(dpa[:, _hs(h)], p["wgaT"][h]))
                sg["dwgx"][h] += _mm_tn(xc[:, _hs(h)], dpx[:, _hs(h)])
                sg["dwga"][h] += _mm_tn(xc[:, _hs(h)], dpa[:, _hs(h)])
            dxc = dxc + jnp.concatenate(dxcs, axis=-1)
            sg["dvec"][3:4, :] += _rowsum(dxc)
            sg["dcw"][3:4, :] += _rowsum(dxc * xb)
            sg["dcw"][2:3, :] += _rowsum(dxc * lc["x1"])
            sg["dcw"][1:2, :] += _rowsum(dxc * lc["x2"])
            sg["dcw"][0:1, :] += _rowsum(dxc * lc["x3"])
            ext = jnp.concatenate([dxc, dccar[...]], axis=0)
            n = TS + SUB
            cw = p["cw"]
            dproj_ref[:, 0:TOK] = (cw[3:4, :] * dxc + cw[2:3, :] * pltpu.roll(ext, n - 1, 0)[:TS]
                                   + cw[1:2, :] * pltpu.roll(ext, n - 2, 0)[:TS]
                                   + cw[0:1, :] * pltpu.roll(ext, n - 3, 0)[:TS]).astype(bf16)
            dccar[...] = dxc[0:SUB, :]

        dxin_ref[...] = ALPHA * dz + _mm(dproj_ref[...], winT_ref[...])

    rtile = lambda w: pl.BlockSpec((TS, w), lambda s: (nt - 1 - s, 0))
    consts = [w_inT, w_outT, lng, kh, khT, vh, vhT] + pvals
    in_arrays = [dxout, z, proj] + consts + list(saves)
    in_specs = [rtile(D_MODEL), rtile(D_MODEL), rtile(W)] + [_const_spec(a) for a in consts]
    for a in saves:
        in_specs.append(pl.BlockSpec((1,) + a.shape[1:], lambda s, _n=a.ndim - 1: (nt - 1 - s,) + (0,) * _n))
    out_shape = [jax.ShapeDtypeStruct((S, D_MODEL), f32), jax.ShapeDtypeStruct((S, W), bf16),
                 jax.ShapeDtypeStruct((S, D_MODEL), bf16), jax.ShapeDtypeStruct((S, D_MODEL), bf16),
                 jax.ShapeDtypeStruct((SUB, D_MODEL), f32), jax.ShapeDtypeStruct((XHEADS, XW, XW), f32),
                 jax.ShapeDtypeStruct((XHEADS, XW, XW), f32)]
    out_specs = [rtile(D_MODEL), rtile(W), rtile(D_MODEL), rtile(D_MODEL), _acc_spec((SUB, D_MODEL)),
                 _acc_spec((XHEADS, XW, XW)), _acc_spec((XHEADS, XW, XW))]
    for n in sg_names:
        out_shape.append(jax.ShapeDtypeStruct(sg_shapes[n], f32))
        out_specs.append(_acc_spec(sg_shapes[n]))
    if kind == 1:
        scratch = ([pltpu.VMEM((NH, HD, HD), f32)] * 2 + [pltpu.VMEM((HGRN_SUB // CHUNK, NH, HD, HD), f32)] * 3
                   + [pltpu.VMEM((HGRN_SUB, TOK), f32)] * 9)
    elif kind == 2:
        scratch = [pltpu.VMEM((2 * SUB, TOK), f32)]
    elif kind == 3:
        scratch = [pltpu.VMEM((SUB, TOK), f32)] * 2
    else:
        scratch = []
    outs = pl.pallas_call(body, name=f"bwd_layer{kind}", grid=(nt,), in_specs=in_specs, out_specs=out_specs,
                          out_shape=out_shape, scratch_shapes=scratch, compiler_params=_params())(*in_arrays)
    return outs[:7], dict(zip(sg_names, outs[7:]))


def _prep(mem, w_kv, logits):
    def body(mem_ref, w_ref, lg_ref, kh_ref, khT_ref, vh_ref, vhT_ref, p_ref):
        kv = _mm(mem_ref[...], w_ref[...])
        k, v = kv[:, 0:XW], kv[:, XW:]
        kT, vT = k.T, v.T
        col = lax.broadcasted_iota(jnp.int32, (XW, XW), 1) // XDIM
        row = lax.broadcasted_iota(jnp.int32, (XW, XW), 0) // XDIM
        for h in range(XHEADS):
            kh_ref[h] = jnp.where(col == h, k, 0.0).astype(bf16)
            vh_ref[h] = jnp.where(col == h, v, 0.0).astype(bf16)
            khT_ref[h] = jnp.where(row == h, kT, 0.0).astype(bf16)
            vhT_ref[h] = jnp.where(row == h, vT, 0.0).astype(bf16)
        lg = lg_ref[...]
        e = jnp.exp(lg - jnp.max(lg, axis=0, keepdims=True))
        p_ref[...] = e / jnp.sum(e, axis=0, keepdims=True)

    vm = pl.BlockSpec(memory_space=pltpu.VMEM)
    hs = jax.ShapeDtypeStruct((XHEADS, XW, XW), bf16)
    return pl.pallas_call(body, name="prep_memory", in_specs=[vm] * 3, out_specs=[vm] * 5,
                          out_shape=[hs, hs, hs, hs, jax.ShapeDtypeStruct(logits.shape, f32)])(mem, w_kv, logits)


def _kv_bwd(mem, dks, dvs):
    def body(mem_ref, *refs):
        out_ref = refs[-1]
        col = lax.broadcasted_iota(jnp.int32, (XW, XW), 1) // XDIM
        dk = jnp.zeros((XW, XW), f32)
        dv = jnp.zeros((XW, XW), f32)
        for l in range(DEPTH):
            for h in range(XHEADS):
                dk = dk + jnp.where(col == h, refs[l][h], 0.0)
                dv = dv + jnp.where(col == h, refs[DEPTH + l][h], 0.0)
        out_ref[:, 0:XW] = _mm_tn(mem_ref[...], dk)
        out_ref[:, XW:] = _mm_tn(mem_ref[...], dv)

    vm = pl.BlockSpec(memory_space=pltpu.VMEM)
    return pl.pallas_call(body, name="kv_bwd", in_specs=[vm] * (1 + 2 * DEPTH), out_specs=vm,
                          out_shape=jax.ShapeDtypeStruct((D_MODEL, 2 * XW), f32))(mem, *dks, *dvs)


def _tn_gemm(a, b, name, nb):
    S, M = a.shape
    N = b.shape[1]
    NB = N // nb
    nk = S // TK

    def body(a_ref, b_ref, o_ref):
        @pl.when(pl.program_id(1) == 0)
        def _():
            o_ref[...] = jnp.zeros_like(o_ref)

        o_ref[...] += _mm_tn(a_ref[...], b_ref[...])

    return pl.pallas_call(body, name=name, grid=(nb, nk),
                          in_specs=[pl.BlockSpec((TK, M), lambda j, k: (k, 0)), pl.BlockSpec((TK, NB), lambda j, k: (k, j))],
                          out_specs=pl.BlockSpec((M, NB), lambda j, k: (0, j)),
                          out_shape=jax.ShapeDtypeStruct((M, N), f32),
                          compiler_params=pltpu.CompilerParams(dimension_semantics=("parallel", "arbitrary"),
                                                               vmem_limit_bytes=VMEM_LIMIT))(a, b)


def _rows_block(R, mult=16, cap=1024):
    best = R
    for d in range(mult, min(R, cap) + 1, mult):
        if R % d == 0:
            best = d
    return best


def _tn_gemm_sharded(a, b, name):
    S, M = a.shape
    Wq = b.shape[1] // 4
    nk = S // TK

    def body(a_ref, b_ref, o_ref):
        @pl.when(pl.program_id(0) == 0)
        def _():
            o_ref[...] = jnp.zeros_like(o_ref)

        at = a_ref[...].astype(MM)
        for j in range(4):
            o_ref[j] += _mm_tn(at, b_ref[:, j * Wq:(j + 1) * Wq])

    return pl.pallas_call(body, name=name, grid=(nk,),
                          in_specs=[pl.BlockSpec((TK, M), lambda k: (k, 0)), pl.BlockSpec((TK, 4 * Wq), lambda k: (k, 0))],
                          out_specs=pl.BlockSpec((4, M, Wq), lambda k: (0, 0, 0)),
                          out_shape=jax.ShapeDtypeStruct((4, M, Wq), f32), compiler_params=_params())(a, b)


HALF_ROWS = D_MODEL // 2
SHARD_ROWS = D_MODEL // 4


def _half_of_full(ref, kind, h):
    if kind == "rows":
        cols = ref.shape[1] // 2
        return ref.at[:, pl.ds(h * cols, cols)]
    return ref.at[:, pl.ds(h * HALF_ROWS, HALF_ROWS)]


def _shard_of_half(ref, kind, j):
    if kind == "rows":
        return ref.at[pl.ds(j * SHARD_ROWS, SHARD_ROWS)]
    return ref.at[j]


def _half_of_shard(ref, kind, h):
    if kind == "rows":
        cols = ref.shape[1] // 2
        return ref.at[:, pl.ds(h * cols, cols)]
    rows = ref.shape[0] // 2
    return ref.at[pl.ds(h * rows, rows)]


def _half_shape(full_shape, kind):
    if kind == "rows":
        return (full_shape[0], full_shape[1] // 2)
    return (4, HALF_ROWS, full_shape[2])


def _shard_half_shape(full_shape, kind):
    if kind == "rows":
        return (SHARD_ROWS, full_shape[1] // 2)
    return (HALF_ROWS, full_shape[2])


def _shard_shape(full_shape, kind):
    if kind == "rows":
        return (SHARD_ROWS, full_shape[1])
    return (D_MODEL, full_shape[2])


def _ew_call(body, name, grid, jc, ins, in_specs, out_shape, out_specs):
    gs = pltpu.PrefetchScalarGridSpec(num_scalar_prefetch=1, grid=grid, in_specs=in_specs, out_specs=out_specs)
    return pl.pallas_call(body, name=name, grid_spec=gs, out_shape=out_shape,
                          compiler_params=pltpu.CompilerParams(dimension_semantics=("parallel",) * len(grid),
                                                               vmem_limit_bytes=VMEM_LIMIT))(jc, *ins)


def _add_sibling(part, got, kind, jc, name):
    def body(jc_ref, a_ref, b_ref, o_ref, ob_ref):
        s = a_ref[...] + b_ref[...]
        o_ref[...] = s
        ob_ref[...] = s.astype(bf16)

    if kind == "rows":
        R, C = part.shape[0], part.shape[1] // 2
        grid = (2,)
        mine = pl.BlockSpec((R // 2, C), lambda i, jc_ref: (i, jc_ref[1]))
        spec = pl.BlockSpec((R // 2, C), lambda i, jc_ref: (i, 0))
    else:
        C = part.shape[2]
        grid = (4, 2)
        mine = pl.BlockSpec((None, HALF_ROWS // 2, C), lambda s, i, jc_ref: (s, 2 * jc_ref[1] + i, 0))
        spec = pl.BlockSpec((None, HALF_ROWS // 2, C), lambda s, i, jc_ref: (s, i, 0))
    hs = _half_shape(part.shape, kind)
    return _ew_call(body, name, grid, jc, [part, got], [mine, spec],
                    [jax.ShapeDtypeStruct(hs, f32), jax.ShapeDtypeStruct(hs, bf16)], [spec, spec])


def _add_chips(q32, r, kind, jc, name):
    def body(jc_ref, q_ref, r_ref, out_ref):
        out_ref[...] = ((q_ref[...] + r_ref[0].astype(f32)) + r_ref[1].astype(f32)) + r_ref[2].astype(f32)

    if kind == "rows":
        C = q32.shape[1]
        grid = (1,)
        qs = pl.BlockSpec((SHARD_ROWS, C), lambda i, jc_ref: (jc_ref[0], 0))
        rs = pl.BlockSpec((3, SHARD_ROWS, C), lambda i, jc_ref: (0, 0, 0))
        os_ = pl.BlockSpec((SHARD_ROWS, C), lambda i, jc_ref: (0, jc_ref[1]))
        full_shape = (D_MODEL, 2 * C)
    else:
        C = q32.shape[2]
        grid = (2,)
        qs = pl.BlockSpec((None, HALF_ROWS // 2, C), lambda i, jc_ref: (jc_ref[0], i, 0))
        rs = pl.BlockSpec((3, HALF_ROWS // 2, C), lambda i, jc_ref: (0, i, 0))
        os_ = pl.BlockSpec((HALF_ROWS // 2, C), lambda i, jc_ref: (2 * jc_ref[1] + i, 0))
        full_shape = (4, D_MODEL, C)
    return _ew_call(body, name, grid, jc, [q32, r], [qs, rs], jax.ShapeDtypeStruct(_shard_shape(full_shape, kind), f32), os_)


def _adamw(w, g, m, v, name):
    R, C = w.shape
    br = _rows_block(R, mult=SUB, cap=512)
    c1 =1.0 / (1.0 - ADAM_B1 ** ADAM_STEP)
    c2 = 1.0 / (1.0 - ADAM_B2 ** ADAM_STEP)

    def body(w_ref, g_ref, m_ref, v_ref, d_ref, nm_ref, nv_ref):
        g_ = g_ref[...]
        nm = ADAM_B1 * m_ref[...] + (1.0 - ADAM_B1) * g_
        nv = ADAM_B2 * v_ref[...] + (1.0 - ADAM_B2) * (g_ * g_)
        nm_ref[...] = nm
        nv_ref[...] = nv
        d_ref[...] = -ADAM_LR * ((nm * c1) / (jnp.sqrt(nv * c2) + ADAM_EPS) + ADAM_WD * w_ref[...])

    spec = pl.BlockSpec((br, C), lambda i: (i, 0))
    sh = jax.ShapeDtypeStruct((R, C), f32)
    return pl.pallas_call(body, name=name, grid=(R // br,), in_specs=[spec] * 4, out_specs=[spec] * 3,
                          out_shape=[sh, sh, sh], compiler_params=_params("parallel"))(w, g, m, v)


def _small_finish(dbacc, p_soft, dlb):
    def body(db_ref, p_ref, dlb_ref, dbs_ref, dlg_ref):
        lane = lax.broadcasted_iota(jnp.int32, (HD, HD), 1)
        acc = jnp.zeros((HD, HD), f32)
        for h in range(NH):
            acc = acc + jnp.where(lane == h, jnp.sum(db_ref[h], axis=-1, keepdims=True), 0.0)
        dbs_ref[...] = acc
        p = p_ref[...]
        p1 = p[1:2, :]
        rowi = lax.broadcasted_iota(jnp.int32, p.shape, 0)
        dlg_ref[...] = dlb_ref[0:1, :] * p1 * (jnp.where(rowi == 1, 1.0, 0.0) - p)

    vm = pl.BlockSpec(memory_space=pltpu.VMEM)
    return pl.pallas_call(body, name="small_finish", in_specs=[vm] * 3, out_specs=[vm] * 2,
                          out_shape=[jax.ShapeDtypeStruct((HD, HD), f32), jax.ShapeDtypeStruct(p_soft.shape, f32)])(dbacc, p_soft, dlb)


def _where_am_i():
    return lax.axis_index("x"), lax.axis_index("y"), lax.axis_index("c")


MAX_PIECES = 8


def _nchunks(rows, mult):
    for n in range(MAX_PIECES, 0, -1):
        if rows % (n * mult) == 0:
            return n
    return 1


def _leading_pieces(src, dst):
    n = src.shape[0]
    if len(src.shape) >= 3 and n <= MAX_PIECES:
        return [(src.at[s], dst.at[s]) for s in range(n)]
    return [(src, dst)]


def _ag_weights(shards, kinds, jshard):
    n = len(shards)

    def body(*refs):
        sh_refs, out_refs, token = refs[:n], refs[2 * n:3 * n], refs[3 * n]
        send_sems, recv_sems = refs[3 * n + 1:]
        x, y, c = _where_am_i()
        j = 2 * x + y
        sib = (x, y, 1 - c)
        chips = [(1 - x, y), (x, 1 - y), (1 - x, 1 - y)]
        token[...] = jnp.zeros_like(token)

        def cp(k, src, dst, to):
            return pltpu.make_async_remote_copy(src_ref=src, dst_ref=dst, send_sem=send_sems.at[k], recv_sem=recv_sems.at[k],
                                                device_id=to, device_id_type=MESH)

        started = []
        for a in range(n):
            for k, (cx, cy) in enumerate(chips):
                d = cp(6 * a + k, _half_of_shard(sh_refs[a], kinds[a], c), _half_of_shard(out_refs[a].at[j], kinds[a], c), (cx, cy, c))
                d.start()
                started.append(d)
        for a in range(n):
            for k, (cx, cy) in enumerate(chips):
                blk = _half_of_shard(out_refs[a].at[2 * cx + cy], kinds[a], c)
                cp(6 * a + k, blk, blk, (cx, cy, c)).wait_recv()
                d = cp(6 * a + 3 + k, blk, blk, sib)
                d.start()
                started.append(d)
        for a in range(n):
            for k, (cx, cy) in enumerate(chips):
                blk = _half_of_shard(out_refs[a].at[2 * cx + cy], kinds[a], 1 - c)
                cp(6 * a + 3 + k, blk, blk, sib).wait_recv()
        for d in started:
            d.wait_send()

    placed = [lax.dynamic_update_slice(jnp.zeros((4,) + s.shape, s.dtype), s[None], (jshard,) + (0,) * s.ndim) for s in shards]
    anyspec = pl.BlockSpec(memory_space=pl.ANY)
    outs = pl.pallas_call(body, name="all_gather_weights", in_specs=[anyspec] * (2 * n),
                          out_specs=[anyspec] * n + [pl.BlockSpec(memory_space=pltpu.VMEM)],
                          out_shape=[jax.ShapeDtypeStruct(p.shape, p.dtype) for p in placed] + [jax.ShapeDtypeStruct((SUB, LANE), f32)],
                          input_output_aliases={n + a: a for a in range(n)},
                          scratch_shapes=[pltpu.SemaphoreType.DMA((6 * n,)), pltpu.SemaphoreType.DMA((6 * n,))],
                          compiler_params=pltpu.CompilerParams(has_side_effects=True))(*shards, *placed)
    return outs[:n], outs[n]


_HBM = pl.BlockSpec(memory_space=pltpu.HBM)
_SEM = pl.BlockSpec(memory_space=pltpu.SEMAPHORE)
_FLOWING = pltpu.SideEffectType.DATAFLOW_SIDE_EFFECTING


def _peers6(x, y, c):
    chips = [(1 - x, y), (x, 1 - y), (1 - x, 1 - y)]
    return [(2 * k + e, chip, c if e == 0 else 1 - c) for k, chip in enumerate(chips) for e in range(2)]


def _ag_start(shards, jshard, name, after=None):
    n = len(shards)

    def body(*refs):
        out_refs = refs[2 * n:4 * n]
        send_sems, recv_sems, token = refs[4 * n:]
        x, y, c = _where_am_i()
        j = 2 * x + y
        for a in range(n):
            for slot, (cx, cy), tc in _peers6(x, y, c):
                pltpu.make_async_remote_copy(src_ref=_half_of_shard(out_refs[a], "win", c),
                                             dst_ref=_half_of_shard(out_refs[n + a].at[j], "win", c),
                                             send_sem=send_sems.at[6 * a + slot], recv_sem=recv_sems.at[6 * a + slot],
                                             device_id=(cx, cy, tc), device_id_type=MESH).start()
        token[...] = jnp.zeros_like(token)

    fill = jnp.zeros((), f32) if after is None else after[0, 0]
    placed = [lax.dynamic_update_slice(jnp.broadcast_to(fill.astype(s.dtype), (4,) + s.shape), s[None], (jshard,) + (0,) * s.ndim)
              for s in shards]
    hbm = lambda t: pltpu.with_memory_space_constraint(t, pltpu.HBM)
    both = list(shards) + placed
    outs = pl.pallas_call(
        body, name=name, in_specs=[_HBM] * (2 * n), out_specs=[_HBM] * (2 * n) + [_SEM, _SEM, pl.BlockSpec(memory_space=pltpu.VMEM)],
        out_shape=[pltpu.HBM(p.shape, p.dtype) for p in both] + [pltpu.SemaphoreType.DMA((6 * n,)), pltpu.SemaphoreType.DMA((6 * n,)),
                                                                jax.ShapeDtypeStruct((SUB, LANE), f32)],
        input_output_aliases={a: a for a in range(2 * n)},
        compiler_params=pltpu.CompilerParams(has_side_effects=_FLOWING))(*[hbm(t) for t in both])
    return outs[:2 * n], outs[2 * n], outs[2 * n + 1], outs[2 * n + 2]


def _ag_wait(bufs, send_sems, recv_sems, after, name):
    n = len(bufs) // 2

    def body(*refs):
        sh_refs, g_refs = refs[:n], refs[n:2 * n]
        send_sems, recv_sems = refs[2 * n], refs[2 * n + 1]
        x, y, c = _where_am_i()
        for a in range(n):
            for slot, (cx, cy), tc in _peers6(x, y, c):
                cp = pltpu.make_async_remote_copy(src_ref=_half_of_shard(sh_refs[a], "win", c),
                                                  dst_ref=_half_of_shard(g_refs[a].at[2 * cx + cy], "win", tc),
                                                  send_sem=send_sems.at[6 * a + slot], recv_sem=recv_sems.at[6 * a + slot],
                                                  device_id=(cx, cy, tc), device_id_type=MESH)
                cp.wait_send()
                cp.wait_recv()

    outs = pl.pallas_call(body, name=name, in_specs=[_HBM] * (2 * n) + [_SEM, _SEM, pl.BlockSpec(memory_space=pl.ANY)],
                          out_specs=[_HBM] * (2 * n), out_shape=[pltpu.HBM(b.shape, b.dtype) for b in bufs],
                          input_output_aliases={a: a for a in range(2 * n)},
                          compiler_params=pltpu.CompilerParams(has_side_effects=_FLOWING))(*bufs, send_sems, recv_sems, after)
    return outs[n:]


def _rs_swap(parts, kinds, name):
    n = len(parts)

    def body(*refs):
        p_refs, got_refs = refs[:n], refs[n:2 * n]
        send_sems, recv_sems = refs[2 * n:]
        x, y, c = _where_am_i()

        def cp(a, src, dst):
            return pltpu.make_async_remote_copy(src_ref=src, dst_ref=dst, send_sem=send_sems.at[a], recv_sem=recv_sems.at[a],
                                                device_id=(x, y, 1 - c), device_id_type=MESH)

        for a in range(n):
            for src, dst in _leading_pieces(_half_of_full(p_refs[a], kinds[a], 1 - c), got_refs[a]):
                cp(a, src, dst).start()
        for a in range(n):
            cp(a, got_refs[a], got_refs[a]).wait()

    anyspec = pl.BlockSpec(memory_space=pl.ANY)
    return pl.pallas_call(body, name=name, in_specs=[anyspec] * n, out_specs=[anyspec] * n,
                          out_shape=[jax.ShapeDtypeStruct(_half_shape(p.shape, k), p.dtype) for p, k in zip(parts, kinds)],
                          scratch_shapes=[pltpu.SemaphoreType.DMA((n,)), pltpu.SemaphoreType.DMA((n,))],
                          compiler_params=pltpu.CompilerParams(has_side_effects=True))(*parts)


def _rs_owners(qbs, kinds, full_shapes):
    n = len(qbs)

    def body(*refs):
        q_refs, got_refs = refs[:n], refs[n:2 * n]
        send_sems, recv_sems = refs[2 * n:]
        x, y, c = _where_am_i()
        chips = [(1 - x, y), (x, 1 - y), (1 - x, 1 - y)]
        ds = []
        for a in range(n):
            for k, (cx, cy) in enumerate(chips):
                d = pltpu.make_async_remote_copy(src_ref=_shard_of_half(q_refs[a], kinds[a], 2 * cx + cy), dst_ref=got_refs[a].at[k],
                                                 send_sem=send_sems.at[3 * a + k], recv_sem=recv_sems.at[3 * a + k],
                                                 device_id=(cx, cy, c), device_id_type=MESH)
                d.start()
                ds.append(d)
        for d in ds:
            d.wait()

    anyspec = pl.BlockSpec(memory_space=pl.ANY)
    return pl.pallas_call(body, name="rs_to_owners", in_specs=[anyspec] * n, out_specs=[anyspec] * n,
                          out_shape=[jax.ShapeDtypeStruct((3,) + _shard_half_shape(fs, k), bf16) for fs, k in zip(full_shapes, kinds)],
                          scratch_shapes=[pltpu.SemaphoreType.DMA((3 * n,)), pltpu.SemaphoreType.DMA((3 * n,))],
                          compiler_params=pltpu.CompilerParams(has_side_effects=True))(*qbs)


def _rs_owners_start(qbs, kinds, full_shapes, name):
    n = len(qbs)

    def body(*refs):
        q_refs, got_refs = refs[2 * n:3 * n], refs[3 * n:4 * n]
        send_sems, recv_sems, token = refs[4 * n:]
        x, y, c = _where_am_i()
        for a in range(n):
            for k, (cx, cy) in enumerate([(1 - x, y), (x, 1 - y), (1 - x, 1 - y)]):
                pltpu.make_async_remote_copy(src_ref=_shard_of_half(q_refs[a], kinds[a], 2 * cx + cy), dst_ref=got_refs[a].at[k],
                                             send_sem=send_sems.at[3 * a + k], recv_sem=recv_sems.at[3 * a + k],
                                             device_id=(cx, cy, c), device_id_type=MESH).start()
        token[...] = jnp.zeros_like(token)

    hbm = lambda t: pltpu.with_memory_space_constraint(t, pltpu.HBM)
    lands = [lax.empty((3,) + _shard_half_shape(fs, k), bf16) for fs, k in zip(full_shapes, kinds)]
    both = list(qbs) + lands
    outs = pl.pallas_call(
        body, name=name, in_specs=[_HBM] * (2 * n), out_specs=[_HBM] * (2 * n) + [_SEM, _SEM, pl.BlockSpec(memory_space=pltpu.VMEM)],
        out_shape=[pltpu.HBM(t.shape, t.dtype) for t in both] + [pltpu.SemaphoreType.DMA((3 * n,)), pltpu.SemaphoreType.DMA((3 * n,)),
                                                                jax.ShapeDtypeStruct((SUB, LANE), f32)],
        input_output_aliases={a: a for a in range(2 * n)},
        compiler_params=pltpu.CompilerParams(has_side_effects=_FLOWING))(*[hbm(t) for t in both])
    return outs[:2 * n], outs[2 * n], outs[2 * n + 1], outs[2 * n + 2]


def _rs_owners_wait(bufs, send_sems, recv_sems, kinds, after, name):
    n = len(bufs) // 2

    def body(*refs):
        q_refs, got_refs = refs[:n], refs[n:2 * n]
        send_sems, recv_sems = refs[2 * n], refs[2 * n + 1]
        x, y, c = _where_am_i()
        for a in range(n):
            for k, (cx, cy) in enumerate([(1 - x, y), (x, 1 - y), (1 - x, 1 - y)]):
                cp = pltpu.make_async_remote_copy(src_ref=_shard_of_half(q_refs[a], kinds[a], 2 * cx + cy), dst_ref=got_refs[a].at[k],
                                                  send_sem=send_sems.at[3 * a + k], recv_sem=recv_sems.at[3 * a + k],
                                                  device_id=(cx, cy, c), device_id_type=MESH)
                cp.wait_send()
                cp.wait_recv()

    outs = pl.pallas_call(body, name=name, in_specs=[_HBM] * (2 * n) + [_SEM, _SEM, pl.BlockSpec(memory_space=pl.ANY)],
                          out_specs=[_HBM] * (2 * n), out_shape=[pltpu.HBM(b.shape, b.dtype) for b in bufs],
                          input_output_aliases={a: a for a in range(2 * n)},
                          compiler_params=pltpu.CompilerParams(has_side_effects=_FLOWING))(*bufs, send_sems, recv_sems, after)
    return outs[n:]


def _rs_join(bufs, kinds):
    n = len(bufs)

    def body(*refs):
        out_refs = refs[n:2 * n]
        send_sems, recv_sems = refs[2 * n:]
        x, y, c = _where_am_i()

        def cp(a, h):
            blk = _half_of_shard(out_refs[a], kinds[a], h)
            return pltpu.make_async_remote_copy(src_ref=blk, dst_ref=blk, send_sem=send_sems.at[a], recv_sem=recv_sems.at[a],
                                                device_id=(x, y, 1 - c), device_id_type=MESH)

        for a in range(n):
            cp(a, c).start()
        for a in range(n):
            cp(a, c).wait_send()
            cp(a, 1 - c).wait_recv()

    anyspec = pl.BlockSpec(memory_space=pl.ANY)
    return pl.pallas_call(body, name="rs_join_halves", in_specs=[anyspec] * n, out_specs=[anyspec] * n,
                          out_shape=[jax.ShapeDtypeStruct(b.shape, b.dtype) for b in bufs],
                          input_output_aliases={a: a for a in range(n)},
                          scratch_shapes=[pltpu.SemaphoreType.DMA((n,)), pltpu.SemaphoreType.DMA((n,))],
                          compiler_params=pltpu.CompilerParams(has_side_effects=True))(*bufs)


def _all_reduce_small(g):
    R, C = g.shape
    H = R // 2
    NP = _nchunks(H, SUB)
    PR = H // NP

    def body(g_ref, out_ref, sib_ref, chip_ref, send_sems, recv_sems):
        x, y, c = _where_am_i()
        j = 2 * x + y
        sib = (x, y, 1 - c)
        chips = [(1 - x, y), (x, 1 - y), (1 - x, 1 - y)]
        rows = pl.ds(pl.multiple_of(c * H, SUB), H)

        def cp(k, src, dst, to):
            return pltpu.make_async_remote_copy(src_ref=src, dst_ref=dst, send_sem=send_sems.at[k], recv_sem=recv_sems.at[k],
                                                device_id=to, device_id_type=MESH)

        def pieces(k, src, dst, to):
            for q in range(NP):
                cp(k, src.at[pl.ds(q * PR, PR)], dst.at[pl.ds(q * PR, PR)], to).start()

        for half in range(2):
            pieces(0, g_ref.at[pl.ds(half * H, H)], sib_ref.at[pl.ds(half * H, H)], sib)
        cp(0, g_ref, sib_ref, sib).wait()
        chip_ref[j] = g_ref[rows, :] + sib_ref[rows, :]
        for k, (cx, cy) in enumerate(chips):
            pieces(1 + k, chip_ref.at[j], chip_ref.at[j], (cx, cy, c))
        for k, (cx, cy) in enumerate(chips):
            blk = chip_ref.at[2 * cx + cy]
            cp(1 + k, blk, blk, (cx, cy, c)).wait()
        out_ref[rows, :] = ((chip_ref[0] + chip_ref[1]) + chip_ref[2]) + chip_ref[3]
        other = out_ref.at[pl.ds(pl.multiple_of((1 - c) * H, SUB), H)]
        pieces(4, out_ref.at[rows], out_ref.at[rows], sib)
        cp(4, other, other, sib).wait()

    vm = pl.BlockSpec(memory_space=pltpu.VMEM)
    return pl.pallas_call(body, name="all_reduce_small", in_specs=[vm], out_specs=vm,
                          out_shape=jax.ShapeDtypeStruct((R, C), f32),
                          scratch_shapes=[pltpu.VMEM((R, C), f32), pltpu.VMEM((4, H, C), f32),
                                          pltpu.SemaphoreType.DMA((5,)), pltpu.SemaphoreType.DMA((5,))],
                          compiler_params=pltpu.CompilerParams(has_side_effects=True, vmem_limit_bytes=VMEM_LIMIT))(g)


SPLIT_MIN_ELEMS = 1 << 16


def _all_reduce_many(gs):
    n = len(gs)
    split = [g.ndim == 3 and g.shape[0] % 2 == 0 and g.size >= SPLIT_MIN_ELEMS for g in gs]
    part_shape = [((g.shape[0] // 2,) + g.shape[1:]) if s else g.shape for g, s in zip(gs, split)]
    n_split = sum(split)

    def body(*refs):
        g, out, sibs, chipb = refs[:n], refs[n:2 * n], refs[2 * n:3 * n], refs[3 * n:4 * n]
        send_sems, recv_sems = refs[4 * n:]
        x, y, c = _where_am_i()
        j = 2 * x + y
        sib = (x, y, 1 - c)
        chips = [(1 - x, y), (x, 1 - y), (1 - x, 1 - y)]

        def cp(k, src, dst, to):
            return pltpu.make_async_remote_copy(src_ref=src, dst_ref=dst, send_sem=send_sems.at[k], recv_sem=recv_sems.at[k],
                                                device_id=to, device_id_type=MESH)

        def part(a, h):
            return pl.ds(h * part_shape[a][0], part_shape[a][0]) if split[a] else Ellipsis

        def mine(ref, a, h):
            return ref.at[part(a, h)] if split[a] else ref

        swaps = [cp(a, g[a], sibs[a], sib) for a in range(n)]
        for d in swaps:
            d.start()
        for a in range(n):
            swaps[a].wait()
            chipb[a][j] = g[a][part(a, c)] + sibs[a][part(a, c)]
        sends = [cp(n + 3 * a + k, chipb[a].at[j], chipb[a].at[j], (cx, cy, c)) for a in range(n) for k, (cx, cy) in enumerate(chips)]
        for d in sends:
            d.start()
        for a in range(n):
            for k, (cx, cy) in enumerate(chips):
                blk = chipb[a].at[2 * cx + cy]
                cp(n + 3 * a + k, blk, blk, (cx, cy, c)).wait_recv()
            out[a][part(a, c)] = ((chipb[a][0] + chipb[a][1]) + chipb[a][2]) + chipb[a][3]
        for d in sends:
            d.wait_send()
        backs = [(a, cp(4 * n + i, mine(out[a], a, c), mine(out[a], a, c), sib)) for i, a in enumerate([a for a in range(n) if split[a]])]
        for _, d in backs:
            d.start()
        for i, (a, d) in enumerate(backs):
            d.wait_send()
            cp(4 * n + i, mine(out[a], a, 1 - c), mine(out[a], a, 1 - c), sib).wait_recv()

    vm = pl.BlockSpec(memory_space=pltpu.VMEM)
    nsem = 4 * n + n_split
    return pl.pallas_call(body, name="all_reduce_small_grads", in_specs=[vm] * n, out_specs=[vm] * n,
                          out_shape=[jax.ShapeDtypeStruct(g.shape, f32) for g in gs],
                          scratch_shapes=([pltpu.VMEM(g.shape, f32) for g in gs] + [pltpu.VMEM((4,) + ps, f32) for ps in part_shape]
                                          + [pltpu.SemaphoreType.DMA((nsem,)), pltpu.SemaphoreType.DMA((nsem,))]),
                          compiler_params=pltpu.CompilerParams(has_side_effects=True, vmem_limit_bytes=VMEM_LIMIT))(*gs)


def _adamw_many(ws, gs, ms, vs, name):
    n = len(ws)
    c1 = 1.0 / (1.0 - ADAM_B1 ** ADAM_STEP)
    c2 = 1.0 / (1.0 - ADAM_B2 ** ADAM_STEP)

    def body(*refs):
        for a in range(n):
            w_ref, g_ref, m_ref, v_ref, d_ref, nm_ref, nv_ref = (refs[i * n + a] for i in range(7))
            g_ = g_ref[...]
            nm = ADAM_B1 * m_ref[...] + (1.0 - ADAM_B1) * g_
            nv = ADAM_B2 * v_ref[...] + (1.0 - ADAM_B2) * (g_ * g_)
            nm_ref[...] = nm
            nv_ref[...] = nv
            d_ref[...] = -ADAM_LR * ((nm * c1) / (jnp.sqrt(nv * c2) + ADAM_EPS) + ADAM_WD * w_ref[...])

    vm = pl.BlockSpec(memory_space=pltpu.VMEM)
    sh = [jax.ShapeDtypeStruct(w.shape, f32) for w in ws]
    outs = pl.pallas_call(body, name=name, in_specs=[vm] * (4 * n), out_specs=[vm] * (3 * n), out_shape=sh * 3,
                          compiler_params=pltpu.CompilerParams(vmem_limit_bytes=VMEM_LIMIT))(*ws, *gs, *ms, *vs)
    return outs[:n], outs[n:2 * n], outs[2 * n:]


def _pack_flat(arrs, rows_mult):
    flat = jnp.concatenate([a.reshape(-1) for a in arrs])
    n = flat.shape[0]
    tot = -(-n // (rows_mult * LANE)) * rows_mult * LANE
    return jnp.pad(flat, (0, tot - n)).reshape(-1, LANE)


def _unpack_flat(buf, shapes):
    flat = buf.reshape(-1)
    out, o = [], 0
    for s in shapes:
        n = math.prod(s)
        out.append(flat[o:o + n].reshape(s))
        o += n
    return out


_BIG = ("mem_kv_w", "w_out", "a_w_in", "b_w_in", "c_w_in", "d_w_in")
SMALL_ROWS_MULT = 256


def _row8(v):
    v = v.reshape(-1, v.shape[-1])
    return jnp.pad(v, ((0, SUB - v.shape[0]), (0, 0)))


def kernel(x, mem, mem_kv_w, ln_g, ln_b, w_out, hgrn_lb_logits, a_w_in, a_w_s, a_b_s, b_w_in, b_norm_g, c_w_in, c_w_pool, c_scale, d_w_in, d_conv_w, d_conv_b, d_w_gx, d_b_gx, d_w_ga, d_b_ga, d_a_param, loss_target, m_mem_kv_w, m_ln_g, m_ln_b, m_w_out, m_hgrn_lb_logits, m_a_w_in, m_a_w_s, m_a_b_s, m_b_w_in, m_b_norm_g, m_c_w_in, m_c_w_pool, m_c_scale, m_d_w_in, m_d_conv_w, m_d_conv_b, m_d_w_gx, m_d_b_gx, m_d_w_ga, m_d_b_ga, m_d_a_param, v_mem_kv_w, v_ln_g, v_ln_b, v_w_out, v_hgrn_lb_logits, v_a_w_in, v_a_w_s, v_a_b_s, v_b_w_in, v_b_norm_g, v_c_w_in, v_c_w_pool, v_c_scale, v_d_w_in, v_d_conv_w, v_d_conv_b, v_d_w_gx, v_d_b_gx, v_d_w_ga, v_d_b_ga, v_d_a_param):
    names = ["mem_kv_w", "ln_g", "ln_b", "w_out", "hgrn_lb_logits", "a_w_in", "a_w_s", "a_b_s", "b_w_in", "b_norm_g", "c_w_in",
             "c_w_pool", "c_scale", "d_w_in", "d_conv_w", "d_conv_b", "d_w_gx", "d_b_gx", "d_w_ga", "d_b_ga", "d_a_param"]
    w = dict(mem_kv_w=mem_kv_w, ln_g=ln_g, ln_b=ln_b, w_out=w_out, hgrn_lb_logits=hgrn_lb_logits, a_w_in=a_w_in, a_w_s=a_w_s,
             a_b_s=a_b_s, b_w_in=b_w_in, b_norm_g=b_norm_g, c_w_in=c_w_in, c_w_pool=c_w_pool, c_scale=c_scale, d_w_in=d_w_in,
             d_conv_w=d_conv_w, d_conv_b=d_conv_b, d_w_gx=d_w_gx, d_b_gx=d_b_gx, d_w_ga=d_w_ga, d_b_ga=d_b_ga, d_a_param=d_a_param)
    m = dict(zip(names, [m_mem_kv_w, m_ln_g, m_ln_b, m_w_out, m_hgrn_lb_logits, m_a_w_in, m_a_w_s, m_a_b_s, m_b_w_in, m_b_norm_g,
                         m_c_w_in, m_c_w_pool, m_c_scale, m_d_w_in, m_d_conv_w, m_d_conv_b, m_d_w_gx, m_d_b_gx, m_d_w_ga,
                         m_d_b_ga, m_d_a_param]))
    v = dict(zip(names, [v_mem_kv_w, v_ln_g, v_ln_b, v_w_out, v_hgrn_lb_logits, v_a_w_in, v_a_w_s, v_a_b_s, v_b_w_in, v_b_norm_g,
                         v_c_w_in, v_c_w_pool, v_c_scale, v_d_w_in, v_d_conv_w, v_d_conv_b, v_d_w_gx, v_d_b_gx, v_d_w_ga,
                         v_d_b_ga, v_d_a_param]))
    xi, yi = lax.axis_index("x"), lax.axis_index("y")
    jshard = 2 * xi + yi
    x2 = x[0]
    mem2 = mem[0]
    tgt2 = loss_target[0]

    w_in_sh = [w[n][0].astype(bf16) for n in _BIG[2:]]
    w_out_sh = w_out.astype(bf16)
    gath0, tie = _ag_weights([mem_kv_w.astype(bf16), w_out_sh[0], w_in_sh[0]], ("rows", "win", "win"), jshard)
    w_kv = gath0[0].reshape(D_MODEL, 2 * XW)

    def layer_weights(g_in, g_out):
        return (g_in.transpose(1, 0, 2).reshape(D_MODEL, -1), g_in.transpose(0, 2, 1).reshape(-1, D_MODEL),
                g_out.reshape(D_MODEL, D_MODEL), g_out.transpose(2, 0, 1).reshape(D_MODEL, D_MODEL))

    lw = [layer_weights(gath0[2], gath0[1])]

    def gather_small(shard):
        z = jnp.zeros((4, POOL_GROUP), f32)
        return lax.dynamic_update_slice(z, shard.reshape(1, POOL_GROUP), (jshard, 0))

    sm_sh = jnp.concatenate([gather_small(b_norm_g), gather_small(c_scale), gather_small(d_conv_b), gather_small(d_a_param)]
                            + [gather_small(d_conv_w[:, r]) for r in range(4)], axis=0)
    ci = lax.axis_index("c")
    sm_all = _all_reduce_small(_pack_flat([jnp.where(ci == 0, sm_sh, 0.0)], SUB * 2) + tie[0:1, 0:1])
    pending = [None]
    tie = sm_all
    for l in range(1, DEPTH):
        bufs, ssem, rsem, tie = _ag_start([w_in_sh[l], w_out_sh[l]], jshard, f"gather_start{l}", tie)
        pending.append((bufs, ssem, rsem))
    tied_gain = {0: ln_g[0:1] + tie[0:1, 0:1]}
    sm = _unpack_flat(sm_all, [(8, 4 * POOL_GROUP)])[0]
    ng_full, scale_full, convb_full, ap_full = sm[0:1], sm[1:2], sm[2:3], sm[3:4]
    convw_full = sm[4:8]

    tril = jnp.tril(jnp.ones((HD, HD), bool))
    wtri = jnp.where(tril, a_w_s[0], 0.0)
    wbd = jnp.zeros((TOK, TOK), f32)
    for g in range(4):
        wbd = lax.dynamic_update_slice(wbd, c_w_pool[0, g], (g * POOL_GROUP, g * POOL_GROUP))
    kh, khT, vh, vhT, p_soft = _prep(mem2, w_kv, hgrn_lb_logits)
    prm = [
        dict(wtri=wtri.astype(bf16), wtriT=wtri.transpose(0, 2, 1).astype(bf16),
             bcolb=jnp.broadcast_to(a_b_s[0][:, :, None], (NH, HD, HD))),
        dict(lb=p_soft[1:2], ng=ng_full),
        dict(wbd=wbd.astype(bf16), wbdT=wbd.T.astype(bf16), scale=scale_full),
        dict(cw=_row8(convw_full), cb=convb_full, wgx=d_w_gx[0].astype(bf16), wgxT=d_w_gx[0].transpose(0, 2, 1).astype(bf16),
             bgx=d_b_gx.reshape(1, TOK), wga=d_w_ga[0].astype(bf16), wgaT=d_w_ga[0].transpose(0, 2, 1).astype(bf16),
             bga=d_b_ga.reshape(1, TOK), ap=ap_full),
    ]

    acts = []
    h = x2
    for l in range(DEPTH):
        if l:
            bufs, ssem, rsem = pending[l]
            lw.append(layer_weights(*_ag_wait(bufs, ssem, rsem, h, f"gather_wait{l}")))
        outs = _fwd_layer(l, h, lw[l][0], lw[l][2], tied_gain.get(l, ln_g[l:l + 1]), ln_b[l:l + 1], khT, vh, prm[l],
                          tgt2 if l == DEPTH - 1 else None)
        nfix = 4 if l == DEPTH - 1 else 3
        acts.append(dict(xin=h, proj=outs[1], z=outs[2], saves=outs[nfix:]))
        if l == DEPTH - 1:
            loss_part = outs[3]
        h = outs[0]
    loss = lax.psum(0.5 / D_MODEL * jnp.sum(loss_part), ("x", "y", "c"))

    dh = h
    dln = [None] * DEPTH
    dks, dvs = [None] * DEPTH, [None] * DEPTH
    sgr = [None] * DEPTH
    jc = jnp.stack([jshard, ci]).astype(jnp.int32)
    lkinds = ("win", "rows")
    q32s, flying = [None] * DEPTH, [None] * DEPTH
    back_gain = {DEPTH - 1: ln_g[DEPTH - 1:] + (loss - loss)}
    for l in reversed(range(DEPTH)):
        a = acts[l]
        (dxin, dproj, mixedb, dyb, dln[l], dks[l], dvs[l]), sgr[l] = _bwd_layer(
            l, dh, a["z"], a["proj"], lw[l][1], lw[l][3], back_gain.get(l, ln_g[l:l + 1]), kh, khT, vh, vhT, prm[l], a["saves"])
        if _OFFS[l]["W"] // 4 % LANE:
            gw_in = _tn_gemm(a["xin"], dproj, f"grad_w_in{l}", 1).reshape(D_MODEL, 4, -1).transpose(1, 0, 2)
        else:
            gw_in = _tn_gemm_sharded(a["xin"], dproj, f"grad_w_in{l}")
        parts = [gw_in, _tn_gemm(mixedb, dyb, f"grad_w_out{l}", 1)]
        lk = lkinds
        if l == 0:
            parts.append(_kv_bwd(mem2, dks, dvs))
            lk = lkinds + ("rows",)
        gots = _rs_swap(parts, lk, f"rs_swap_halves{l}")
        sums = [_add_sibling(p, g, k, jc, f"rs_add_sibling{l}_{i}") for i, (p, g, k) in enumerate(zip(parts, gots, lk))]
        q32s[l] = [s[0] for s in sums]
        shapes = [p.shape for p in parts]
        if l:
            bufs, ssem, rsem, tok = _rs_owners_start([s[1] for s in sums], lk, shapes, f"rs_owners_start{l}")
            flying[l] = (bufs, ssem, rsem)
            back_gain[l - 1] = ln_g[l - 1:l] + tok[0:1, 0:1]
        else:
            last_got = _rs_owners([s[1] for s in sums], lk, shapes)
        dh = dxin
    grad_x = dh[None]
    fin, fin_kinds = {}, []
    for l in range(DEPTH):
        lk = lkinds + (("rows",) if l == 0 else ())
        got = last_got if l == 0 else _rs_owners_wait(*flying[l], lk, grad_x, f"rs_owners_wait{l}")
        fin[l] = [_add_chips(q, r, k, jc, f"rs_add_chips{l}_{i}") for i, (q, r, k) in enumerate(zip(q32s[l], got, lk))]
        fin_kinds += list(lk)
    joined = _rs_join([t for l in range(DEPTH) for t in fin[l]], tuple(fin_kinds))
    by_layer, o = [], 0
    for l in range(DEPTH):
        by_layer.append(joined[o:o + len(fin[l])])
        o += len(fin[l])
    gbig = {"mem_kv_w": by_layer[0][2], "w_out": jnp.stack([by_layer[l][1] for l in range(DEPTH)])}
    for l, n in enumerate(_BIG[2:]):
        gbig[n] = by_layer[l][0]
    g_sh, d_sh, m_sh, v_sh = {}, {}, {}, {}
    for n in _BIG:
        as2d = lambda t: t.reshape(-1, t.shape[-1])
        upd = _adamw(as2d(w[n]), as2d(gbig[n]), as2d(m[n]), as2d(v[n]), f"adamw_{n}")
        g_sh[n] = gbig[n].reshape(w[n].shape)
        d_sh[n], m_sh[n], v_sh[n] = (u.reshape(w[n].shape) for u in upd)

    dbs, dlogits = _small_finish(sgr[0]["dbacc"], p_soft, sgr[1]["dlb"])
    gs = {
        "ln_g": jnp.concatenate([dln[l][0:1] for l in range(DEPTH)], axis=0),
        "ln_b": jnp.concatenate([dln[l][1:2] for l in range(DEPTH)], axis=0),
        "hgrn_lb_logits": dlogits,
        "a_w_s": sgr[0]["dwtri"][None],
        "a_b_s": dbs[:, 0:NH].T[None],
        "b_norm_g": sgr[1]["dng"][0:1],
        "c_w_pool": jnp.stack([sgr[2]["dwbd"][g * POOL_GROUP:(g + 1) * POOL_GROUP, g * POOL_GROUP:(g + 1) * POOL_GROUP]
                               for g in range(4)])[None],
        "c_scale": sgr[2]["dscale"][0:1],
        "d_conv_w": sgr[3]["dcw"][0:4][None],
        "d_conv_b": sgr[3]["dvec"][3:4],
        "d_w_gx": sgr[3]["dwgx"][None],
        "d_b_gx": sgr[3]["dvec"][1:2].reshape(1, NH, HD),
        "d_w_ga": sgr[3]["dwga"][None],
        "d_b_ga": sgr[3]["dvec"][2:3].reshape(1, NH, HD),
        "d_a_param": sgr[3]["dvec"][0:1],
    }
    small = [n for n in names if n not in _BIG]
    drop1 = lambda t: t.reshape(t.shape[1:]) if t.ndim > 2 and t.shape[0] == 1 else t
    gsum = dict(zip(small, _all_reduce_many([drop1(gs[n]) for n in small])))
    for n in ("b_norm_g", "c_scale", "d_conv_b", "d_a_param"):
        gsum[n] = lax.dynamic_slice(gsum[n], (0, jshard * POOL_GROUP), (1, POOL_GROUP))
    gsum["d_conv_w"] = lax.dynamic_slice(gsum["d_conv_w"], (0, jshard * POOL_GROUP), (4, POOL_GROUP))
    upd = _adamw_many(*[[drop1(d[n]) for n in small] for d in (w, gsum, m, v)], "adamw_small")
    gsum = {n: gsum[n].reshape(w[n].shape) for n in small}
    d_sm, m_sm, v_sm = ({n: u.reshape(w[n].shape) for n, u in zip(small, us)} for us in upd)

    grads = {**gsum, **g_sh}
    deltas = {**d_sm, **d_sh}
    new_m = {**m_sm, **m_sh}
    new_v = {**v_sm, **v_sh}
    return (loss, grad_x, *[grads[n] for n in names], *[deltas[n] for n in names], *[new_m[n] for n in names],
            *[new_v[n] for n in names])
```

```python
import functools
import math

import jax
import jax.numpy as jnp
from jax import lax
from jax.experimental import pallas as pl
from jax.experimental.pallas import tpu as pltpu

f32 = jnp.float32
bf16 = jnp.bfloat16
MM = bf16

D_MODEL = 1024
TOK = 768
XW = 256
XHEADS = 4
XDIM = 64
HD = 128
NH = TOK // HD
CHUNK = 16
POOL_GROUP = 192
DEPTH = 4
ALPHA = (2 * DEPTH) ** 0.25
LN_EPS = 1e-5
RMS_EPS = 1e-6
LRU_C = 8.0
ADAM_LR, ADAM_B1, ADAM_B2, ADAM_EPS, ADAM_WD, ADAM_STEP = 0.001, 0.9, 0.999, 1e-08, 0.01, 10

_TS = (256, 256, 256, 256)
HGRN_SUB = 128
TK = 512
SUB = 8
LANE = 128
VMEM_LIMIT = 58 * 1024 * 1024

_OFFS = (
    dict(u=0, v=768, qx=1536, gate=1792, W=2816),
    dict(q=0, f=768, i=1536, qx=2304, gate=2560, W=3584),
    dict(p=0, qx=768, gate=1024, W=2048),
    dict(xb=0, qx=768, gate=1024, W=2048),
)
_PRM = (
    ("wtri", "wtriT", "bcolb"),
    ("lb", "ng"),
    ("wbd", "wbdT", "scale"),
    ("cw", "cb", "wgx", "wgxT", "bgx", "wga", "wgaT", "bga", "ap"),
)
MESH = pl.DeviceIdType.MESH


def _mm(a, b):
    return jnp.dot(a.astype(MM), b.astype(MM), preferred_element_type=f32)


def _mm_nt(a, b):
    return lax.dot_general(a.astype(MM), b.astype(MM), (((1,), (1,)), ((), ())), preferred_element_type=f32)


def _mm_tn(a, b):
    return lax.dot_general(a.astype(MM), b.astype(MM), (((0,), (0,)), ((), ())), preferred_element_type=f32)


def _mm_sel(sel, b):
    s = sel.astype(bf16)
    hi = b.astype(bf16)
    lo = (b - hi.astype(f32)).astype(bf16)
    return jnp.dot(s, hi, preferred_element_type=f32) + jnp.dot(s, lo, preferred_element_type=f32)


def _sig(x):
    return jax.nn.sigmoid(x)


_GC = math.sqrt(2.0 / math.pi)


def _gelu(x):
    t = jnp.tanh(_GC * (x + 0.044715 * x * x * x))
    return 0.5 * x * (1.0 + t), t


def _gelu_grad(x, t):
    return 0.5 * (1.0 + t) + 0.5 * x * (1.0 - t * t) * _GC * (1.0 + 3.0 * 0.044715 * x * x)


def _rowsum(x):
    return jnp.sum(x, axis=0, keepdims=True)


def _lmean(x):
    return jnp.mean(x, axis=-1, keepdims=True)


def _ln(z):
    mu = _lmean(z)
    zc = z - mu
    rstd = lax.rsqrt(_lmean(zc * zc) + LN_EPS)
    return zc * rstd, rstd


def _ln_bwd(dxh, xhat, rstd):
    return rstd * (dxh - _lmean(dxh) - xhat * _lmean(dxh * xhat))


def _hs(h):
    return slice(h * HD, (h + 1) * HD)


def _expm1(x):
    small = x * (1.0 + x * 0.5 * (1.0 + x * (1.0 / 3.0) * (1.0 + x * 0.25 * (1.0 + x * 0.2 * (1.0 + x * (1.0 / 6.0))))))
    return jnp.where(jnp.abs(x) < 0.25, small, jnp.exp(x) - 1.0)


def _softplus(x):
    e = jnp.exp(-jnp.abs(x))
    l1p = jnp.where(e < 1e-4, e - 0.5 * e * e, jnp.log(1.0 + e))
    return jnp.maximum(x, 0.0) + l1p


def _scan_fwd(a, b):
    n = a.shape[0]
    row = lax.broadcasted_iota(jnp.int32, a.shape, 0)
    d = 1
    while d < n:
        if d % SUB:
            m = row >= d
            b = jnp.where(m, a * pltpu.roll(b, d, 0) + b, b)
            a = jnp.where(m, a * pltpu.roll(a, d, 0), a)
        else:
            b = a * jnp.concatenate([jnp.zeros((d,) + b.shape[1:], f32), b[:n - d]], axis=0) + b
            a = a * jnp.concatenate([jnp.ones((d,) + a.shape[1:], f32), a[:n - d]], axis=0)
        d *= 2
    return a, b


def _scan_bwd(a, b):
    n = a.shape[0]
    row = lax.broadcasted_iota(jnp.int32, a.shape, 0)
    d = 1
    while d < n:
        if d % SUB:
            m = row < n - d
            b = jnp.where(m, a * pltpu.roll(b, n - d, 0) + b, b)
            a = jnp.where(m, a * pltpu.roll(a, n - d, 0), a)
        else:
            b = a * jnp.concatenate([b[d:], jnp.zeros((d,) + b.shape[1:], f32)], axis=0) + b
            a = a * jnp.concatenate([a[d:], jnp.ones((d,) + a.shape[1:], f32)], axis=0)
        d *= 2
    return a, b


def _chunk_mats(n):
    r = lax.broadcasted_iota(jnp.int32, (n, n), 0)
    c = lax.broadcasted_iota(jnp.int32, (n, n), 1)
    same = (r // CHUNK) == (c // CHUNK)
    return same, jnp.logical_and(same, c <= r)


def _pool_w(shape):
    lane = lax.broadcasted_iota(jnp.int32, shape, 1)
    return jnp.where(lane < POOL_GROUP, 2, jnp.where(lane < 2 * POOL_GROUP, 4, jnp.where(lane < 3 * POOL_GROUP, 8, 16)))


def _pool_pick(r1, r2, r3, r4):
    lane = lax.broadcasted_iota(jnp.int32, r1.shape, 1)
    return jnp.where(lane < POOL_GROUP, r1, jnp.where(lane < 2 * POOL_GROUP, r2, jnp.where(lane < 3 * POOL_GROUP, r3, r4)))


def _const_spec(a):
    nd = a.ndim
    return pl.BlockSpec(a.shape, lambda i, _nd=nd: (0,) * _nd, pipeline_mode=pl.Buffered(1))


def _acc_spec(shape):
    nd = len(shape)
    return pl.BlockSpec(shape, lambda i, _nd=nd: (0,) * _nd)


def _params(sem="arbitrary"):
    return pltpu.CompilerParams(dimension_semantics=(sem,), vmem_limit_bytes=VMEM_LIMIT)


def _xattn_fwd(qx, khT_ref, vh_ref):
    xo = jnp.zeros((qx.shape[0], XW), f32)
    ps = []
    for h in range(XHEADS):
        s = _mm(qx, khT_ref[h]) * (XDIM ** -0.5)
        e = jnp.exp(s - jnp.max(s, axis=-1, keepdims=True))
        p = e / jnp.sum(e, axis=-1, keepdims=True)
        xo = xo + _mm(p, vh_ref[h])
        ps.append(p)
    return xo, ps


def _hgrn_parallel(q_raw, fl, lb):
    n = q_raw.shape[0]
    same, tri = _chunk_mats(n)
    sq = _sig(q_raw)
    qf = q_raw * sq
    sgm = _sig(fl)
    f = lb + (1.0 - lb) * sgm
    logf = jnp.log(f)
    k = 1.0 - f
    g = _mm_sel(tri, logf)
    gl = _mm_sel(same, logf)
    eg = jnp.exp(g)
    eng = jnp.exp(-g)
    ee = jnp.exp(gl - g)
    return dict(sq=sq, qf=qf, sgm=sgm, f=f, k=k, eg=eg, eng=eng, ee=ee, q_dec=qf * eg, k_inv=k * eng, k_end=k * ee,
                a=jnp.exp(gl))


def _hgrn_intra(q_dec, k_inv, v):
    n = q_dec.shape[0]
    _, tri = _chunk_mats(HD)
    outs = []
    for h in range(NH):
        blks = []
        for b in range(n // HD):
            rs = slice(b * HD, (b + 1) * HD)
            sc = jnp.where(tri, _mm_nt(q_dec[rs, _hs(h)], k_inv[rs, _hs(h)]), 0.0)
            blks.append(_mm(sc, v[rs, _hs(h)]))
        outs.append(jnp.concatenate(blks, axis=0))
    return jnp.concatenate(outs, axis=-1)


def _cs(c):
    return slice(c * CHUNK, (c + 1) * CHUNK)


def _hgrn_inter_fwd(qdec_s, kend_s, v_s, a_s, oint_s, st_ref, states_s, u_s):
    n = qdec_s.shape[0] // CHUNK
    for c in range(n):
        for h in range(NH):
            u_s[c, h] = _mm_tn(v_s[_cs(c), _hs(h)], kend_s[_cs(c), _hs(h)])
    for h in range(NH):
        st = st_ref[h]
        for c in range(n):
            states_s[c, h] = st
            st = st * a_s[c * CHUNK:c * CHUNK + 1, _hs(h)] + u_s[c, h]
        st_ref[h] = st
    if oint_s is None:
        return
    for c in range(n):
        for h in range(NH):
            oint_s[_cs(c), _hs(h)] = _mm_nt(qdec_s[_cs(c), _hs(h)], states_s[c, h])


def _rms(o):
    outs, rs = [], []
    for h in range(NH):
        oh = o[:, _hs(h)]
        r = lax.rsqrt(_lmean(oh * oh) + RMS_EPS)
        outs.append(oh * r)
        rs.append(r)
    return jnp.concatenate(outs, axis=-1), rs


def _gmlp_core(u_raw, v_raw, wtri_ref, bcolb_ref):
    gu, tu = _gelu(u_raw)
    gv, tv = _gelu(v_raw)
    vns, rstds, mixeds = [], [], []
    for h in range(NH):
        vn, rstd = _ln(gv[:, _hs(h)])
        blks = []
        for n in range(u_raw.shape[0] // HD):
            blks.append(_mm(wtri_ref[h], vn[n * HD:(n + 1) * HD]) + bcolb_ref[h])
        vns.append(vn)
        rstds.append(rstd)
        mixeds.append(jnp.concatenate(blks, axis=0))
    mixed = jnp.concatenate(mixeds, axis=-1)
    return gu, tu, tv, vns, rstds, mixed


def _pool_core(p, carry, row0, wbd_ref):
    ext = jnp.concatenate([carry, p], axis=0)
    r1 = ext + pltpu.roll(ext, 1, 0)
    r2 = r1 + pltpu.roll(r1, 2, 0)
    r3 = r2 + pltpu.roll(r2, 4, 0)
    r4 = r3 + pltpu.roll(r3, 8, 0)
    sel = _pool_pick(r1, r2, r3, r4)[2 * SUB:]
    grow = row0 + lax.broadcasted_iota(jnp.int32, p.shape, 0)
    inv_cnt = 1.0 / jnp.minimum(grow + 1, _pool_w(p.shape)).astype(f32)
    diff = sel * inv_cnt - p
    return diff, inv_cnt, _mm(diff, wbd_ref[...])


def _lru_core(xb, ccar, row0, p):
    ext = jnp.concatenate([ccar, xb], axis=0)
    cw = p["cw"]
    x1, x2, x3 = pltpu.roll(ext, 1, 0)[SUB:], pltpu.roll(ext, 2, 0)[SUB:], pltpu.roll(ext, 3, 0)[SUB:]
    xc = cw[3:4, :] * xb + cw[2:3, :] * x1 + cw[1:2, :] * x2 + cw[0:1, :] * x3 + p["cb"][...]
    gxs, gas = [], []
    for h in range(NH):
        gxs.append(_mm(xc[:, _hs(h)], p["wgx"][h]))
        gas.append(_mm(xc[:, _hs(h)], p["wga"][h]))
    gx = _sig(jnp.concatenate(gxs, axis=-1) + p["bgx"][...])
    ga = _sig(jnp.concatenate(gas, axis=-1) + p["bga"][...])
    sp = _softplus(-p["ap"][...])
    la = -LRU_C * ga * sp
    a = jnp.exp(la)
    grow = row0 + lax.broadcasted_iota(jnp.int32, xb.shape, 0)
    first = grow == 0
    mult = jnp.where(first, 1.0, jnp.sqrt(-_expm1(2.0 * la)))
    bt = mult * gx * xc
    return dict(x1=x1, x2=x2, x3=x3, xc=xc, gx=gx, ga=ga, sp=sp, a=a, mult=mult, bt=bt, first=first)


def _fwd_layer(kind, xin, w_in, w_out, lng, lnb, khT, vh, prm, tgt):
    S = xin.shape[0]
    TS = _TS[kind]
    nt = S // TS
    off = _OFFS[kind]
    W = off["W"]
    last = tgt is not None
    pnames = _PRM[kind]
    pvals = [prm[n] for n in pnames]

    def body(*refs):
        it = iter(refs)
        xin_ref, win_ref, wout_ref, lng_ref, lnb_ref, khT_ref, vh_ref = (next(it) for _ in range(7))
        p = {n: next(it) for n in pnames}
        tgt_ref = next(it) if last else None
        xout_ref, proj_ref, z_ref = next(it), next(it), next(it)
        loss_ref = next(it) if last else None
        rest = list(it)
        i = pl.program_id(0)
        x = xin_ref[...]
        proj_ref[...] = _mm(x, win_ref[...])

        if kind == 0:
            gu, _, _, _, _, mixed = _gmlp_core(proj_ref[:, 0:TOK], proj_ref[:, TOK:2 * TOK], p["wtri"], p["bcolb"])
            tok = gu * mixed
        elif kind == 1:
            st_save, o_save, st_ref, states_s, u_s, qdec_s, kend_s, v_s, a_s, oint_s = rest

            @pl.when(i == 0)
            def _():
                st_ref[...] = jnp.zeros_like(st_ref)

            st_save[0, 0] = st_ref[...]
            v = proj_ref[:, 2 * TOK:3 * TOK]
            hp = _hgrn_parallel(proj_ref[:, 0:TOK], proj_ref[:, TOK:2 * TOK], p["lb"][...])
            qdec_s[...] = hp["q_dec"]
            kend_s[...] = hp["k_end"]
            v_s[...] = v
            a_s[...] = hp["a"]
            o_intra = _hgrn_intra(hp["q_dec"], hp["k_inv"], v)
            _hgrn_inter_fwd(qdec_s, kend_s, v_s, a_s, oint_s, st_ref, states_s, u_s)
            o = o_intra + oint_s[...]
            o_save[0] = o
            for sub in range(1, TS // HGRN_SUB):
                st_save[0, sub] = states_s[sub * HGRN_SUB // CHUNK]
            on, _ = _rms(o)
            tok = on * p["ng"][...]
        elif kind == 2:
            pc_save, pcar = rest

            @pl.when(i == 0)
            def _():
                pcar[...] = jnp.zeros_like(pcar)

            pc_save[0] = pcar[...]
            pp = proj_ref[:, 0:TOK]
            _, _, y = _pool_core(pp, pcar[...], i * TS, p["wbd"])
            pcar[...] = pp[TS - 2 * SUB:, :]
            tok = y * p["scale"][...]
        else:
            cc_save, hc_save, h_save, ccar, hcar = rest

            @pl.when(i == 0)
            def _():
                ccar[...] = jnp.zeros_like(ccar)
                hcar[...] = jnp.zeros_like(hcar)

            cc_save[0] = ccar[...]
            hc_save[0] = hcar[...]
            xb = proj_ref[:, 0:TOK]
            lc = _lru_core(xb, ccar[...], i * TS, p)
            P, B = _scan_fwd(lc["a"], lc["bt"])
            tok = P * hcar[SUB - 1:SUB, :] + B
            h_save[0] = tok
            ccar[...] = xb[TS - SUB:, :]
            hcar[...] = tok[TS - SUB:, :]

        xo, _ = _xattn_fwd(proj_ref[:, off["qx"]:off["qx"] + XW], khT_ref, vh_ref)
        gate = proj_ref[:, off["gate"]:off["gate"] + D_MODEL]
        mixed = jnp.concatenate([tok, xo], axis=-1) * (gate * _sig(gate))
        z = ALPHA * x + _mm(mixed, wout_ref[...])
        z_ref[...] = z
        xhat, _ = _ln(z)
        xout = xhat * lng_ref[...] + lnb_ref[...]
        if last:
            e = xout - tgt_ref[...]
            xout_ref[...] = e * (1.0 / D_MODEL)
            es = _rowsum(e * e)
            tot = es[:, 0:LANE]
            for j in range(1, D_MODEL // LANE):
                tot = tot + es[:, j * LANE:(j + 1) * LANE]

            @pl.when(i == 0)
            def _():
                loss_ref[...] = jnp.zeros_like(loss_ref)

            loss_ref[0:1, :] += tot
        else:
            xout_ref[...] = xout

    tile = lambda w: pl.BlockSpec((TS, w), lambda i: (i, 0))
    in_arrays = [xin, w_in, w_out, lng, lnb, khT, vh] + pvals + ([tgt] if last else [])
    in_specs = [tile(D_MODEL)] + [_const_spec(a) for a in in_arrays[1:7 + len(pvals)]] + ([tile(D_MODEL)] if last else [])
    out_shape = [jax.ShapeDtypeStruct((S, D_MODEL), f32), jax.ShapeDtypeStruct((S, W), f32), jax.ShapeDtypeStruct((S, D_MODEL), f32)]
    out_specs = [tile(D_MODEL), tile(W), tile(D_MODEL)]
    if last:
        out_shape.append(jax.ShapeDtypeStruct((SUB, LANE), f32))
        out_specs.append(_acc_spec((SUB, LANE)))
    scratch = []
    save = lambda *s: (jax.ShapeDtypeStruct((nt,) + s, f32), pl.BlockSpec((1,) + s, lambda i, _n=len(s): (i,) + (0,) * _n))
    if kind == 1:
        saved = [save(TS // HGRN_SUB, NH, HD, HD), save(TS, TOK)]
        scratch = ([pltpu.VMEM((NH, HD, HD), f32)] + [pltpu.VMEM((TS // CHUNK, NH, HD, HD), f32)] * 2
                   + [pltpu.VMEM((TS, TOK), f32)] * 5)
    elif kind == 2:
        saved = [save(2 * SUB, TOK)]
        scratch = [pltpu.VMEM((2 * SUB, TOK), f32)]
    elif kind == 3:
        saved = [save(SUB, TOK), save(SUB, TOK), save(TS, TOK)]
        scratch = [pltpu.VMEM((SUB, TOK), f32)] * 2
    else:
        saved = []
    for sh, sp in saved:
        out_shape.append(sh)
        out_specs.append(sp)
    return pl.pallas_call(body, name=f"fwd_layer{kind}", grid=(nt,), in_specs=in_specs, out_specs=out_specs,
                          out_shape=out_shape, scratch_shapes=scratch, compiler_params=_params())(*in_arrays)


def _small_grad_shapes(kind):
    if kind == 0:
        return dict(dwtri=(NH, HD, HD), dbacc=(NH, HD, HD))
    if kind == 1:
        return dict(dlb=(SUB, TOK), dng=(SUB, TOK))
    if kind == 2:
        return dict(dwbd=(TOK, TOK), dscale=(SUB, TOK))
    return dict(dcw=(SUB, TOK), dvec=(SUB, TOK), dwgx=(NH, HD, HD), dwga=(NH, HD, HD))


def _bwd_layer(kind, dxout, z, proj, w_inT, w_outT, lng, kh, khT, vh, vhT, prm, saves):
    S = dxout.shape[0]
    TS = _TS[kind]
    nt = S // TS
    off = _OFFS[kind]
    W = off["W"]
    pnames = _PRM[kind]
    pvals = [prm[n] for n in pnames]
    sg_shapes = _small_grad_shapes(kind)
    sg_names = list(sg_shapes)
    n_saves = len(saves)

    def body(*refs):
        it = iter(refs)
        dxo_ref, z_ref, proj_ref, winT_ref, woutT_ref, lng_ref, kh_ref, khT_ref, vh_ref, vhT_ref = (next(it) for _ in range(10))
        p = {n: next(it) for n in pnames}
        sv = [next(it) for _ in range(n_saves)]
        dxin_ref, dproj_ref, mixed_ref, dy_ref, dln_ref, dk_ref, dv_ref = (next(it) for _ in range(7))
        sg = {n: next(it) for n in sg_names}
        rest = list(it)
        step = pl.program_id(0)
        i = nt - 1 - step

        @pl.when(step == 0)
        def _():
            dln_ref[...] = jnp.zeros_like(dln_ref)
            dk_ref[...] = jnp.zeros_like(dk_ref)
            dv_ref[...] = jnp.zeros_like(dv_ref)
            for n in sg_names:
                sg[n][...] = jnp.zeros_like(sg[n])

        dxo = dxo_ref[...]
        xhat, rstd = _ln(z_ref[...])
        dln_ref[0:1, :] += _rowsum(dxo * xhat)
        dln_ref[1:2, :] += _rowsum(dxo)
        dz = _ln_bwd(dxo * lng_ref[...], xhat, rstd)
        dyb = dz.astype(bf16)
        dy_ref[...] = dyb
        dmixed = _mm(dyb, woutT_ref[...])

        aux = {}
        if kind == 0:
            u_raw, v_raw = proj_ref[:, 0:TOK], proj_ref[:, TOK:2 * TOK]
            gu, tu, tv, vns, rstds, mx = _gmlp_core(u_raw, v_raw, p["wtri"], p["bcolb"])
            tok = gu * mx
        elif kind == 1:
            st_save, o_save = sv
            (dst_ref, fst_ref, states_s, dsts_s, u_s, qdec_s, kend_s, v_s, a_s, do_s, dqdec_s, dkend_s, dv_s,
             dgl_s) = rest

            @pl.when(step == 0)
            def _():
                dst_ref[...] = jnp.zeros_like(dst_ref)

            o = o_save[0]
            on, rs = _rms(o)
            tok = on * p["ng"][...]
            aux = dict(o=o, on=on, rs=rs)
        elif kind == 2:
            pc_save, = sv
            dpcar, = rest
            pp = proj_ref[:, 0:TOK]
            diff, inv_cnt, y = _pool_core(pp, pc_save[0], i * TS, p["wbd"])
            tok = y * p["scale"][...]
        else:
            cc_save, hc_save, h_save = sv
            dccar, gcar = rest
            xb = proj_ref[:, 0:TOK]
            lc = _lru_core(xb, cc_save[0], i * TS, p)
            hin = hc_save[0, SUB - 1:SUB, :]
            tok = h_save[0]

        xo, ps = _xattn_fwd(proj_ref[:, off["qx"]:off["qx"] + XW], khT_ref, vh_ref)
        gate = proj_ref[:, off["gate"]:off["gate"] + D_MODEL]
        sgm = _sig(gate)
        sgate = gate * sgm
        cat = jnp.concatenate([tok, xo], axis=-1)
        mixed_ref[...] = (cat * sgate).astype(bf16)
        dcat = dmixed * sgate
        dproj_ref[:, off["gate"]:off["gate"] + D_MODEL] = (dmixed * cat * (sgm * (1.0 + gate * (1.0 - sgm)))).astype(bf16)
        dtok = dcat[:, 0:TOK]
        dxo_att = dcat[:, TOK:]

        qx = proj_ref[:, off["qx"]:off["qx"] + XW]
        dqx = jnp.zeros((TS, XW), f32)
        for h in range(XHEADS):
            dp = _mm(dxo_att, vhT_ref[h])
            ds = ps[h] * (dp - jnp.sum(dp * ps[h], axis=-1, keepdims=True)) * (XDIM ** -0.5)
            dqx = dqx + _mm(ds, kh_ref[h])
            dk_ref[h] += _mm_tn(ds, qx)
            dv_ref[h] += _mm_tn(ps[h], dxo_att)
        dproj_ref[:, off["qx"]:off["qx"] + XW] = dqx.astype(bf16)

        if kind == 0:
            tril = lax.broadcasted_iota(jnp.int32, (HD, HD), 1) <= lax.broadcasted_iota(jnp.int32, (HD, HD), 0)
            dgu = dtok * mx
            dmx = dtok * gu
            dgvs = []
            for h in range(NH):
                dmh = dmx[:, _hs(h)]
                blks = []
                for n in range(TS // HD):
                    rs_ = slice(n * HD, (n + 1) * HD)
                    blks.append(_mm(p["wtriT"][h], dmh[rs_]))
                    sg["dwtri"][h] += jnp.where(tril, _mm_nt(dmh[rs_], vns[h][rs_]), 0.0)
                    sg["dbacc"][h] += dmh[rs_]
                dgvs.append(_ln_bwd(jnp.concatenate(blks, axis=0), vns[h], rstds[h]))
            dgv = jnp.concatenate(dgvs, axis=-1)
            dproj_ref[:, 0:TOK] = (dgu * _gelu_grad(u_raw, tu)).astype(bf16)
            dproj_ref[:, TOK:2 * TOK] = (dgv * _gelu_grad(v_raw, tv)).astype(bf16)
        elif kind == 1:
            o, on, rs = aux["o"], aux["on"], aux["rs"]
            ng = p["ng"][...]
            lb = p["lb"][...]
            sg["dng"][0:1, :] += _rowsum(dtok * on)
            dn = dtok * ng
            dos = []
            for h in range(NH):
                oh, r = o[:, _hs(h)], rs[h]
                dos.append(r * (dn[:, _hs(h)] - oh * (r * r) * _lmean(dn[:, _hs(h)] * oh)))
            do_all = jnp.concatenate(dos, axis=-1)
            _, tri = _chunk_mats(HD)
            same, _ = _chunk_mats(HGRN_SUB)
            triT = jnp.logical_and(same, lax.broadcasted_iota(jnp.int32, (HGRN_SUB, HGRN_SUB), 1)
                                   >= lax.broadcasted_iota(jnp.int32, (HGRN_SUB, HGRN_SUB), 0))
            row16 = lax.broadcasted_iota(jnp.int32, (CHUNK, HD), 0)
            nch = HGRN_SUB // CHUNK
            for sub in reversed(range(TS // HGRN_SUB)):
                rr = slice(sub * HGRN_SUB, (sub + 1) * HGRN_SUB)
                q_raw, v = proj_ref[rr, 0:TOK], proj_ref[rr, 2 * TOK:3 * TOK]
                hp = _hgrn_parallel(q_raw, proj_ref[rr, TOK:2 * TOK], lb)
                qdec_s[...] = hp["q_dec"]
                kend_s[...] = hp["k_end"]
                v_s[...] = v
                a_s[...] = hp["a"]
                fst_ref[...] = st_save[0, sub]
                _hgrn_inter_fwd(qdec_s, kend_s, v_s, a_s, None, fst_ref, states_s, u_s)
                do = do_all[rr]
                do_s[...] = do
                dqd, dki, dvi = [], [], []
                for h in range(NH):
                    bq, bk, bv = [], [], []
                    for b in range(HGRN_SUB // HD):
                        rs_ = slice(b * HD, (b + 1) * HD)
                        qd, ki = hp["q_dec"][rs_, _hs(h)], hp["k_inv"][rs_, _hs(h)]
                        sc = jnp.where(tri, _mm_nt(qd, ki), 0.0)
                        dsc = jnp.where(tri, _mm_nt(do[rs_, _hs(h)], v[rs_, _hs(h)]), 0.0)
                        bv.append(_mm_tn(sc, do[rs_, _hs(h)]))
                        bq.append(_mm(dsc, ki))
                        bk.append(_mm_tn(dsc, qd))
                    dqd.append(jnp.concatenate(bq, axis=0))
                    dki.append(jnp.concatenate(bk, axis=0))
                    dvi.append(jnp.concatenate(bv, axis=0))
                dqdec_s[...] = jnp.concatenate(dqd, axis=-1)
                dk_inv = jnp.concatenate(dki, axis=-1)
                dv_s[...] = jnp.concatenate(dvi, axis=-1)
                for c in range(nch):
                    for h in range(NH):
                        u_s[c, h] = _mm_tn(do_s[_cs(c), _hs(h)], qdec_s[_cs(c), _hs(h)])
                for h in range(NH):
                    dst = dst_ref[h]
                    for c in reversed(range(nch)):
                        dsts_s[c, h] = dst
                        dst = dst * a_s[c * CHUNK:c * CHUNK + 1, _hs(h)] + u_s[c, h]
                    dst_ref[h] = dst
                for c in range(nch):
                    for h in range(NH):
                        stp = states_s[c, h]
                        dst = dsts_s[c, h]
                        dqdec_s[_cs(c), _hs(h)] += _mm(do_s[_cs(c), _hs(h)], stp)
                        dkend_s[_cs(c), _hs(h)] = _mm(v_s[_cs(c), _hs(h)], dst)
                        dv_s[_cs(c), _hs(h)] += _mm_nt(kend_s[_cs(c), _hs(h)], dst)
                        da = jnp.sum(dst * stp, axis=0, keepdims=True) * a_s[c * CHUNK:c * CHUNK + 1, _hs(h)]
                        dgl_s[_cs(c), _hs(h)] = jnp.where(row16 == 0, jnp.broadcast_to(da, (CHUNK, HD)), 0.0)
                dq_dec = dqdec_s[...]
                dk_end = dkend_s[...]
                dg = dq_dec * hp["q_dec"] - dk_inv * hp["k_inv"] - dk_end * hp["k_end"]
                dk = dk_inv * hp["eng"] + dk_end * hp["ee"]
                dglr = dk_end * hp["k_end"] + dgl_s[...]
                dlogf = _mm_sel(triT, dg) + _mm_sel(same, dglr)
                df = dlogf / hp["f"] - dk
                sg["dlb"][0:1, :] += _rowsum(df * (1.0 - hp["sgm"]))
                dproj_ref[rr, 0:TOK] = (dq_dec * hp["eg"] * (hp["sq"] * (1.0 + q_raw * (1.0 - hp["sq"])))).astype(bf16)
                dproj_ref[rr, TOK:2 * TOK] = (df * (1.0 - lb) * hp["sgm"] * (1.0 - hp["sgm"])).astype(bf16)
                dproj_ref[rr, 2 * TOK:3 * TOK] = dv_s[...].astype(bf16)
        elif kind == 2:
            @pl.when(step == 0)
            def _():
                dpcar[...] = jnp.zeros_like(dpcar)

            sg["dscale"][0:1, :] += _rowsum(dtok * y)
            dyp = dtok * p["scale"][...]
            sg["dwbd"][...] += _mm_tn(diff, dyp)
            ddiff = _mm(dyp, p["wbdT"][...])
            q = ddiff * inv_cnt
            ext = jnp.concatenate([q, dpcar[...]], axis=0)
            n = TS + 2 * SUB
            r1 = ext + pltpu.roll(ext, n - 1, 0)
            r2 = r1 + pltpu.roll(r1, n - 2, 0)
            r3 = r2 + pltpu.roll(r2, n - 4, 0)
            r4 = r3 + pltpu.roll(r3, n - 8, 0)
            dproj_ref[:, 0:TOK] = (_pool_pick(r1, r2, r3, r4)[:TS] - ddiff).astype(bf16)
            dpcar[...] = q[0:2 * SUB, :]
        else:
            @pl.when(step == 0)
            def _():
                dccar[...] = jnp.zeros_like(dccar)
                gcar[...] = jnp.zeros_like(gcar)

            a, mult, gx, ga, xc = lc["a"], lc["mult"], lc["gx"], lc["ga"], lc["xc"]
            row = lax.broadcasted_iota(jnp.int32, (TS, TOK), 0)
            an = jnp.where(row == TS - 1, 1.0, pltpu.roll(a, TS - 1, 0))
            Pb, Bb = _scan_bwd(an, dtok)
            lam = Pb * gcar[0:1, :] + Bb
            gcar[...] = (a * lam)[0:SUB, :]
            hprev = jnp.where(row == 0, jnp.broadcast_to(hin, (TS, TOK)), pltpu.roll(tok, 1, 0))
            dmult = lam * gx * xc
            dgx = lam * mult * xc
            dxc = lam * mult * gx
            dla = lam * hprev * a - jnp.where(lc["first"], 0.0, dmult * a * a / mult)
            sp = lc["sp"]
            dga = -LRU_C * sp * dla
            dsp = _rowsum(-LRU_C * ga * dla)
            sg["dvec"][0:1, :] += dsp * (-_sig(-p["ap"][...]))
            dpx = dgx * gx * (1.0 - gx)
            dpa = dga * ga * (1.0 - ga)
            sg["dvec"][1:2, :] += _rowsum(dpx)
            sg["dvec"][2:3, :] += _rowsum(dpa)
            dxcs = []
            for h in range(NH):
                dxcs.append(_mm(dpx[:, _hs(h)], p["wgxT"][h]) + _mm(dpa[:, _hs(h)], p["wgaT"][h]))
                sg["dwgx"][h] += _mm_tn(xc[:, _hs(h)], dpx[:, _hs(h)])
                sg["dwga"][h] += _mm_tn(xc[:, _hs(h)], dpa[:, _hs(h)])
            dxc = dxc + jnp.concatenate(dxcs, axis=-1)
            sg["dvec"][3:4, :] += _rowsum(dxc)
            sg["dcw"][3:4, :] += _rowsum(dxc * xb)
            sg["dcw"][2:3, :] += _rowsum(dxc * lc["x1"])
            sg["dcw"][1:2, :] += _rowsum(dxc * lc["x2"])
            sg["dcw"][0:1, :] += _rowsum(dxc * lc["x3"])
            ext = jnp.concatenate([dxc, dccar[...]], axis=0)
            n = TS + SUB
            cw = p["cw"]
            dproj_ref[:, 0:TOK] = (cw[3:4, :] * dxc + cw[2:3, :] * pltpu.roll(ext, n - 1, 0)[:TS]
                                   + cw[1:2, :] * pltpu.roll(ext, n - 2, 0)[:TS]
                                   + cw[0:1, :] * pltpu.roll(ext, n - 3, 0)[:TS]).astype(bf16)
            dccar[...] = dxc[0:SUB, :]

        dxin_ref[...] = ALPHA * dz + _mm(dproj_ref[...], winT_ref[...])

    rtile = lambda w: pl.BlockSpec((TS, w), lambda s: (nt - 1 - s, 0))
    consts = [w_inT, w_outT, lng, kh, khT, vh, vhT] + pvals
    in_arrays = [dxout, z, proj] + consts + list(saves)
    in_specs = [rtile(D_MODEL), rtile(D_MODEL), rtile(W)] + [_const_spec(a) for a in consts]
    for a in saves:
        in_specs.append(pl.BlockSpec((1,) + a.shape[1:], lambda s, _n=a.ndim - 1: (nt - 1 - s,) + (0,) * _n))
    out_shape = [jax.ShapeDtypeStruct((S, D_MODEL), f32), jax.ShapeDtypeStruct((S, W), bf16),
                 jax.ShapeDtypeStruct((S, D_MODEL), bf16), jax.ShapeDtypeStruct((S, D_MODEL), bf16),
                 jax.ShapeDtypeStruct((SUB, D_MODEL), f32), jax.ShapeDtypeStruct((XHEADS, XW, XW), f32),
                 jax.ShapeDtypeStruct((XHEADS, XW, XW), f32)]
    out_specs = [rtile(D_MODEL), rtile(W), rtile(D_MODEL), rtile(D_MODEL), _acc_spec((SUB, D_MODEL)),
                 _acc_spec((XHEADS, XW, XW)), _acc_spec((XHEADS, XW, XW))]
    for n in sg_names:
        out_shape.append(jax.ShapeDtypeStruct(sg_shapes[n], f32))
        out_specs.append(_acc_spec(sg_shapes[n]))
    if kind == 1:
        scratch = ([pltpu.VMEM((NH, HD, HD), f32)] * 2 + [pltpu.VMEM((HGRN_SUB // CHUNK, NH, HD, HD), f32)] * 3
                   + [pltpu.VMEM((HGRN_SUB, TOK), f32)] * 9)
    elif kind == 2:
        scratch = [pltpu.VMEM((2 * SUB, TOK), f32)]
    elif kind == 3:
        scratch = [pltpu.VMEM((SUB, TOK), f32)] * 2
    else:
        scratch = []
    outs = pl.pallas_call(body, name=f"bwd_layer{kind}", grid=(nt,), in_specs=in_specs, out_specs=out_specs,
                          out_shape=out_shape, scratch_shapes=scratch, compiler_params=_params())(*in_arrays)
    return outs[:7], dict(zip(sg_names, outs[7:]))


def _prep(mem, w_kv, logits):
    def body(mem_ref, w_ref, lg_ref, kh_ref, khT_ref, vh_ref, vhT_ref, p_ref):
        kv = _mm(mem_ref[...], w_ref[...])
        k, v = kv[:, 0:XW], kv[:, XW:]
        kT, vT = k.T, v.T
        col = lax.broadcasted_iota(jnp.int32, (XW, XW), 1) // XDIM
        row = lax.broadcasted_iota(jnp.int32, (XW, XW), 0) // XDIM
        for h in range(XHEADS):
            kh_ref[h] = jnp.where(col == h, k, 0.0).astype(bf16)
            vh_ref[h] = jnp.where(col == h, v, 0.0).astype(bf16)
            khT_ref[h] = jnp.where(row == h, kT, 0.0).astype(bf16)
            vhT_ref[h] = jnp.where(row == h, vT, 0.0).astype(bf16)
        lg = lg_ref[...]
        e = jnp.exp(lg - jnp.max(lg, axis=0, keepdims=True))
        p_ref[...] = e / jnp.sum(e, axis=0, keepdims=True)

    vm = pl.BlockSpec(memory_space=pltpu.VMEM)
    hs = jax.ShapeDtypeStruct((XHEADS, XW, XW), bf16)
    return pl.pallas_call(body, name="prep_memory", in_specs=[vm] * 3, out_specs=[vm] * 5,
                          out_shape=[hs, hs, hs, hs, jax.ShapeDtypeStruct(logits.shape, f32)])(mem, w_kv, logits)


def _kv_bwd(mem, dks, dvs):
    def body(mem_ref, *refs):
        out_ref = refs[-1]
        col = lax.broadcasted_iota(jnp.int32, (XW, XW), 1) // XDIM
        dk = jnp.zeros((XW, XW), f32)
        dv = jnp.zeros((XW, XW), f32)
        for l in range(DEPTH):
            for h in range(XHEADS):
                dk = dk + jnp.where(col == h, refs[l][h], 0.0)
                dv = dv + jnp.where(col == h, refs[DEPTH + l][h], 0.0)
        out_ref[:, 0:XW] = _mm_tn(mem_ref[...], dk)
        out_ref[:, XW:] = _mm_tn(mem_ref[...], dv)

    vm = pl.BlockSpec(memory_space=pltpu.VMEM)
    return pl.pallas_call(body, name="kv_bwd", in_specs=[vm] * (1 + 2 * DEPTH), out_specs=vm,
                          out_shape=jax.ShapeDtypeStruct((D_MODEL, 2 * XW), f32))(mem, *dks, *dvs)


def _tn_gemm(a, b, name, nb):
    S, M = a.shape
    N = b.shape[1]
    NB = N // nb
    nk = S // TK

    def body(a_ref, b_ref, o_ref):
        @pl.when(pl.program_id(1) == 0)
        def _():
            o_ref[...] = jnp.zeros_like(o_ref)

        o_ref[...] += _mm_tn(a_ref[...], b_ref[...])

    return pl.pallas_call(body, name=name, grid=(nb, nk),
                          in_specs=[pl.BlockSpec((TK, M), lambda j, k: (k, 0)), pl.BlockSpec((TK, NB), lambda j, k: (k, j))],
                          out_specs=pl.BlockSpec((M, NB), lambda j, k: (0, j)),
                          out_shape=jax.ShapeDtypeStruct((M, N), f32),
                          compiler_params=pltpu.CompilerParams(dimension_semantics=("parallel", "arbitrary"),
                                                               vmem_limit_bytes=VMEM_LIMIT))(a, b)


def _rows_block(R, mult=16, cap=1024):
    best = R
    for d in range(mult, min(R, cap) + 1, mult):
        if R % d == 0:
            best = d
    return best


def _tn_gemm_sharded(a, b, name):
    S, M = a.shape
    Wq = b.shape[1] // 4
    nk = S // TK

    def body(a_ref, b_ref, o_ref):
        @pl.when(pl.program_id(0) == 0)
        def _():
            o_ref[...] = jnp.zeros_like(o_ref)

        at = a_ref[...].astype(MM)
        for j in range(4):
            o_ref[j] += _mm_tn(at, b_ref[:, j * Wq:(j + 1) * Wq])

    return pl.pallas_call(body, name=name, grid=(nk,),
                          in_specs=[pl.BlockSpec((TK, M), lambda k: (k, 0)), pl.BlockSpec((TK, 4 * Wq), lambda k: (k, 0))],
                          out_specs=pl.BlockSpec((4, M, Wq), lambda k: (0, 0, 0)),
                          out_shape=jax.ShapeDtypeStruct((4, M, Wq), f32), compiler_params=_params())(a, b)


HALF_ROWS = D_MODEL // 2
SHARD_ROWS = D_MODEL // 4


def _half_of_full(ref, kind, h):
    if kind == "rows":
        cols = ref.shape[1] // 2
        return ref.at[:, pl.ds(h * cols, cols)]
    return ref.at[:, pl.ds(h * HALF_ROWS, HALF_ROWS)]


def _shard_of_half(ref, kind, j):
    if kind == "rows":
        return ref.at[pl.ds(j * SHARD_ROWS, SHARD_ROWS)]
    return ref.at[j]


def _half_of_shard(ref, kind, h):
    if kind == "rows":
        cols = ref.shape[1] // 2
        return ref.at[:, pl.ds(h * cols, cols)]
    rows = ref.shape[0] // 2
    return ref.at[pl.ds(h * rows, rows)]


def _half_shape(full_shape, kind):
    if kind == "rows":
        return (full_shape[0], full_shape[1] // 2)
    return (4, HALF_ROWS, full_shape[2])


def _shard_half_shape(full_shape, kind):
    if kind == "rows":
        return (SHARD_ROWS, full_shape[1] // 2)
    return (HALF_ROWS, full_shape[2])


def _shard_shape(full_shape, kind):
    if kind == "rows":
        return (SHARD_ROWS, full_shape[1])
    return (D_MODEL, full_shape[2])


def _ew_call(body, name, grid, jc, ins, in_specs, out_shape, out_specs):
    gs = pltpu.PrefetchScalarGridSpec(num_scalar_prefetch=1, grid=grid, in_specs=in_specs, out_specs=out_specs)
    return pl.pallas_call(body, name=name, grid_spec=gs, out_shape=out_shape,
                          compiler_params=pltpu.CompilerParams(dimension_semantics=("parallel",) * len(grid),
                                                               vmem_limit_bytes=VMEM_LIMIT))(jc, *ins)


def _add_sibling(part, got, kind, jc, name):
    def body(jc_ref, a_ref, b_ref, o_ref, ob_ref):
        s = a_ref[...] + b_ref[...]
        o_ref[...] = s
        ob_ref[...] = s.astype(bf16)

    if kind == "rows":
        R, C = part.shape[0], part.shape[1] // 2
        grid = (2,)
        mine = pl.BlockSpec((R // 2, C), lambda i, jc_ref: (i, jc_ref[1]))
        spec = pl.BlockSpec((R // 2, C), lambda i, jc_ref: (i, 0))
    else:
        C = part.shape[2]
        grid = (4, 2)
        mine = pl.BlockSpec((None, HALF_ROWS // 2, C), lambda s, i, jc_ref: (s, 2 * jc_ref[1] + i, 0))
        spec = pl.BlockSpec((None, HALF_ROWS // 2, C), lambda s, i, jc_ref: (s, i, 0))
    hs = _half_shape(part.shape, kind)
    return _ew_call(body, name, grid, jc, [part, got], [mine, spec],
                    [jax.ShapeDtypeStruct(hs, f32), jax.ShapeDtypeStruct(hs, bf16)], [spec, spec])


def _add_chips(q32, r, kind, jc, name):
    def body(jc_ref, q_ref, r_ref, out_ref):
        out_ref[...] = ((q_ref[...] + r_ref[0].astype(f32)) + r_ref[1].astype(f32)) + r_ref[2].astype(f32)

    if kind == "rows":
        C = q32.shape[1]
        grid = (1,)
        qs = pl.BlockSpec((SHARD_ROWS, C), lambda i, jc_ref: (jc_ref[0], 0))
        rs = pl.BlockSpec((3, SHARD_ROWS, C), lambda i, jc_ref: (0, 0, 0))
        os_ = pl.BlockSpec((SHARD_ROWS, C), lambda i, jc_ref: (0, jc_ref[1]))
        full_shape = (D_MODEL, 2 * C)
    else:
        C = q32.shape[2]
        grid = (2,)
        qs = pl.BlockSpec((None, HALF_ROWS // 2, C), lambda i, jc_ref: (jc_ref[0], i, 0))
        rs = pl.BlockSpec((3, HALF_ROWS // 2, C), lambda i, jc_ref: (0, i, 0))
        os_ = pl.BlockSpec((HALF_ROWS // 2, C), lambda i, jc_ref: (2 * jc_ref[1] + i, 0))
        full_shape = (4, D_MODEL, C)
    return _ew_call(body, name, grid, jc, [q32, r], [qs, rs], jax.ShapeDtypeStruct(_shard_shape(full_shape, kind), f32), os_)


def _adamw(w, g, m, v, name):
    R, C = w.shape
    br = _rows_block(R, mult=SUB, cap=512)
    c1 = 1.0 / (1.0 - ADAM_B1 ** ADAM_STEP)
    c2 = 1.0 / (1.0 - ADAM_B2 ** ADAM_STEP)

    def body(w_ref, g_ref, m_ref, v_ref, d_ref, nm_ref, nv_ref):
        g_ = g_ref[...]
        nm = ADAM_B1 * m_ref[...] + (1.0 - ADAM_B1) * g_
        nv = ADAM_B2 * v_ref[...] + (1.0 - ADAM_B2) * (g_ * g_)
        nm_ref[...] = nm
        nv_ref[...] = nv
        d_ref[...] = -ADAM_LR * ((nm * c1) / (jnp.sqrt(nv * c2) + ADAM_EPS) + ADAM_WD * w_ref[...])

    spec = pl.BlockSpec((br, C), lambda i: (i, 0))
    sh = jax.ShapeDtypeStruct((R, C), f32)
    return pl.pallas_call(body, name=name, grid=(R // br,), in_specs=[spec] * 4, out_specs=[spec] * 3,
                          out_shape=[sh, sh, sh], compiler_params=_params("parallel"))(w, g, m, v)


def _small_finish(dbacc, p_soft, dlb):
    def body(db_ref, p_ref, dlb_ref, dbs_ref, dlg_ref):
        lane = lax.broadcasted_iota(jnp.int32, (HD, HD), 1)
        acc = jnp.zeros((HD, HD), f32)
        for h in range(NH):
            acc = acc + jnp.where(lane == h, jnp.sum(db_ref[h], axis=-1, keepdims=True), 0.0)
        dbs_ref[...] = acc
        p = p_ref[...]
        p1 = p[1:2, :]
        rowi = lax.broadcasted_iota(jnp.int32, p.shape, 0)
        dlg_ref[...] = dlb_ref[0:1, :] * p1 * (jnp.where(rowi == 1, 1.0, 0.0) - p)

    vm = pl.BlockSpec(memory_space=pltpu.VMEM)
    return pl.pallas_call(body, name="small_finish", in_specs=[vm] * 3, out_specs=[vm] * 2,
                          out_shape=[jax.ShapeDtypeStruct((HD, HD), f32), jax.ShapeDtypeStruct(p_soft.shape, f32)])(dbacc, p_soft, dlb)


def _where_am_i():
    return lax.axis_index("x"), lax.axis_index("y"), lax.axis_index("c")


MAX_PIECES = 8


def _nchunks(rows, mult):
    for n in range(MAX_PIECES, 0, -1):
        if rows % (n * mult) == 0:
            return n
    return 1


def _leading_pieces(src, dst):
    n = src.shape[0]
    if len(src.shape) >= 3 and n <= MAX_PIECES:
        return [(src.at[s], dst.at[s]) for s in range(n)]
    return [(src, dst)]


def _ag_weights(shards, kinds, jshard):
    n = len(shards)

    def body(*refs):
        sh_refs, out_refs, token = refs[:n], refs[2 * n:3 * n], refs[3 * n]
        send_sems, recv_sems = refs[3 * n + 1:]
        x, y, c = _where_am_i()
        j = 2 * x + y
        sib = (x, y, 1 - c)
        chips = [(1 - x, y), (x, 1 - y), (1 - x, 1 - y)]
        token[...] = jnp.zeros_like(token)

        def cp(k, src, dst, to):
            return pltpu.make_async_remote_copy(src_ref=src, dst_ref=dst, send_sem=send_sems.at[k], recv_sem=recv_sems.at[k],
                                                device_id=to, device_id_type=MESH)

        started = []
        for a in range(n):
            for k, (cx, cy) in enumerate(chips):
                d = cp(6 * a + k, _half_of_shard(sh_refs[a], kinds[a], c), _half_of_shard(out_refs[a].at[j], kinds[a], c), (cx, cy, c))
                d.start()
                started.append(d)
        for a in range(n):
            for k, (cx, cy) in enumerate(chips):
                blk = _half_of_shard(out_refs[a].at[2 * cx + cy], kinds[a], c)
                cp(6 * a + k, blk, blk, (cx, cy, c)).wait_recv()
                d = cp(6 * a + 3 + k, blk, blk, sib)
                d.start()
                started.append(d)
        for a in range(n):
            for k, (cx, cy) in enumerate(chips):
                blk = _half_of_shard(out_refs[a].at[2 * cx + cy], kinds[a], 1 - c)
                cp(6 * a + 3 + k, blk, blk, sib).wait_recv()
        for d in started:
            d.wait_send()

    placed = [lax.dynamic_update_slice(jnp.zeros((4,) + s.shape, s.dtype), s[None], (jshard,) + (0,) * s.ndim) for s in shards]
    anyspec = pl.BlockSpec(memory_space=pl.ANY)
    outs = pl.pallas_call(body, name="all_gather_weights", in_specs=[anyspec] * (2 * n),
                          out_specs=[anyspec] * n + [pl.BlockSpec(memory_space=pltpu.VMEM)],
                          out_shape=[jax.ShapeDtypeStruct(p.shape, p.dtype) for p in placed] + [jax.ShapeDtypeStruct((SUB, LANE), f32)],
                          input_output_aliases={n + a: a for a in range(n)},
                          scratch_shapes=[pltpu.SemaphoreType.DMA((6 * n,)), pltpu.SemaphoreType.DMA((6 * n,))],
                          compiler_params=pltpu.CompilerParams(has_side_effects=True))(*shards, *placed)
    return outs[:n], outs[n]


_HBM = pl.BlockSpec(memory_space=pltpu.HBM)
_SEM = pl.BlockSpec(memory_space=pltpu.SEMAPHORE)
_FLOWING = pltpu.SideEffectType.DATAFLOW_SIDE_EFFECTING


def _peers6(x, y, c):
    chips = [(1 - x, y), (x, 1 - y), (1 - x, 1 - y)]
    return [(2 * k + e, chip, c if e == 0 else 1 - c) for k, chip in enumerate(chips) for e in range(2)]


def _ag_start(shards, jshard, name, after=None):
    n = len(shards)

    def body(*refs):
        out_refs = refs[2 * n:4 * n]
        send_sems, recv_sems, token = refs[4 * n:]
        x, y, c = _where_am_i()
        j = 2 * x + y
        for a in range(n):
            for slot, (cx, cy), tc in _peers6(x, y, c):
                pltpu.make_async_remote_copy(src_ref=_half_of_shard(out_refs[a], "win", c),
                                             dst_ref=_half_of_shard(out_refs[n + a].at[j], "win", c),
                                             send_sem=send_sems.at[6 * a + slot], recv_sem=recv_sems.at[6 * a + slot],
                                             device_id=(cx, cy, tc), device_id_type=MESH).start()
        token[...] = jnp.zeros_like(token)

    fill = jnp.zeros((), f32) if after is None else after[0, 0]
    placed = [lax.dynamic_update_slice(jnp.broadcast_to(fill.astype(s.dtype), (4,) + s.shape), s[None], (jshard,) + (0,) * s.ndim)
              for s in shards]
    hbm = lambda t: pltpu.with_memory_space_constraint(t, pltpu.HBM)
    both = list(shards) + placed
    outs = pl.pallas_call(
        body, name=name, in_specs=[_HBM] * (2 * n), out_specs=[_HBM] * (2 * n) + [_SEM, _SEM, pl.BlockSpec(memory_space=pltpu.VMEM)],
        out_shape=[pltpu.HBM(p.shape, p.dtype) for p in both] + [pltpu.SemaphoreType.DMA((6 * n,)), pltpu.SemaphoreType.DMA((6 * n,)),
                                                                jax.ShapeDtypeStruct((SUB, LANE), f32)],
        input_output_aliases={a: a for a in range(2 * n)},
        compiler_params=pltpu.CompilerParams(has_side_effects=_FLOWING))(*[hbm(t) for t in both])
    return outs[:2 * n], outs[2 * n], outs[2 * n + 1], outs[2 * n + 2]


def _ag_wait(bufs, send_sems, recv_sems, after, name):
    n = len(bufs) // 2

    def body(*refs):
        sh_refs, g_refs = refs[:n], refs[n:2 * n]
        send_sems, recv_sems = refs[2 * n], refs[2 * n + 1]
        x, y, c = _where_am_i()
        for a in range(n):
            for slot, (cx, cy), tc in _peers6(x, y, c):
                cp = pltpu.make_async_remote_copy(src_ref=_half_of_shard(sh_refs[a], "win", c),
                                                  dst_ref=_half_of_shard(g_refs[a].at[2 * cx + cy], "win", tc),
                                                  send_sem=send_sems.at[6 * a + slot], recv_sem=recv_sems.at[6 * a + slot],
                                                  device_id=(cx, cy, tc), device_id_type=MESH)
                cp.wait_send()
                cp.wait_recv()

    outs = pl.pallas_call(body, name=name, in_specs=[_HBM] * (2 * n) + [_SEM, _SEM, pl.BlockSpec(memory_space=pl.ANY)],
                          out_specs=[_HBM] * (2 * n), out_shape=[pltpu.HBM(b.shape, b.dtype) for b in bufs],
                          input_output_aliases={a: a for a in range(2 * n)},
                          compiler_params=pltpu.CompilerParams(has_side_effects=_FLOWING))(*bufs, send_sems, recv_sems, after)
    return outs[n:]


def _rs_swap(parts, kinds, name):
    n = len(parts)

    def body(*refs):
        p_refs, got_refs = refs[:n], refs[n:2 * n]
        send_sems, recv_sems = refs[2 * n:]
        x, y, c = _where_am_i()

        def cp(a, src, dst):
            return pltpu.make_async_remote_copy(src_ref=src, dst_ref=dst, send_sem=send_sems.at[a], recv_sem=recv_sems.at[a],
                                                device_id=(x, y, 1 - c), device_id_type=MESH)

        for a in range(n):
            for src, dst in _leading_pieces(_half_of_full(p_refs[a], kinds[a], 1 - c), got_refs[a]):
                cp(a, src, dst).start()
        for a in range(n):
            cp(a, got_refs[a], got_refs[a]).wait()

    anyspec = pl.BlockSpec(memory_space=pl.ANY)
    return pl.pallas_call(body, name=name, in_specs=[anyspec] * n, out_specs=[anyspec] * n,
                          out_shape=[jax.ShapeDtypeStruct(_half_shape(p.shape, k), p.dtype) for p, k in zip(parts, kinds)],
                          scratch_shapes=[pltpu.SemaphoreType.DMA((n,)), pltpu.SemaphoreType.DMA((n,))],
                          compiler_params=pltpu.CompilerParams(has_side_effects=True))(*parts)


def _rs_owners(qbs, kinds, full_shapes):
    n = len(qbs)

    def body(*refs):
        q_refs, got_refs = refs[:n], refs[n:2 * n]
        send_sems, recv_sems = refs[2 * n:]
        x, y, c = _where_am_i()
        chips = [(1 - x, y), (x, 1 - y), (1 - x, 1 - y)]
        ds = []
        for a in range(n):
            for k, (cx, cy) in enumerate(chips):
                d = pltpu.make_async_remote_copy(src_ref=_shard_of_half(q_refs[a], kinds[a], 2 * cx + cy), dst_ref=got_refs[a].at[k],
                                                 send_sem=send_sems.at[3 * a + k], recv_sem=recv_sems.at[3 * a + k],
                                                 device_id=(cx, cy, c), device_id_type=MESH)
                d.start()
                ds.append(d)
        for d in ds:
            d.wait()

    anyspec = pl.BlockSpec(memory_space=pl.ANY)
    return pl.pallas_call(body, name="rs_to_owners", in_specs=[anyspec] * n, out_specs=[anyspec] * n,
                          out_shape=[jax.ShapeDtypeStruct((3,) + _shard_half_shape(fs, k), bf16) for fs, k in zip(full_shapes, kinds)],
                          scratch_shapes=[pltpu.SemaphoreType.DMA((3 * n,)), pltpu.SemaphoreType.DMA((3 * n,))],
                          compiler_params=pltpu.CompilerParams(has_side_effects=True))(*qbs)


def _rs_owners_start(qbs, kinds, full_shapes, name):
    n = len(qbs)

    def body(*refs):
        q_refs, got_refs = refs[2 * n:3 * n], refs[3 * n:4 * n]
        send_sems, recv_sems, token = refs[4 * n:]
        x, y, c = _where_am_i()
        for a in range(n):
            for k, (cx, cy) in enumerate([(1 - x, y), (x, 1 - y), (1 - x, 1 - y)]):
                pltpu.make_async_remote_copy(src_ref=_shard_of_half(q_refs[a], kinds[a], 2 * cx + cy), dst_ref=got_refs[a].at[k],
                                             send_sem=send_sems.at[3 * a + k], recv_sem=recv_sems.at[3 * a + k],
                                             device_id=(cx, cy, c), device_id_type=MESH).start()
        token[...] = jnp.zeros_like(token)

    hbm = lambda t: pltpu.with_memory_space_constraint(t, pltpu.HBM)
    lands = [lax.empty((3,) + _shard_half_shape(fs, k), bf16) for fs, k in zip(full_shapes, kinds)]
    both = list(qbs) + lands
    outs = pl.pallas_call(
        body, name=name, in_specs=[_HBM] * (2 * n), out_specs=[_HBM] * (2 * n) + [_SEM, _SEM, pl.BlockSpec(memory_space=pltpu.VMEM)],
        out_shape=[pltpu.HBM(t.shape, t.dtype) for t in both] + [pltpu.SemaphoreType.DMA((3 * n,)), pltpu.SemaphoreType.DMA((3 * n,)),
                                                                jax.ShapeDtypeStruct((SUB, LANE), f32)],
        input_output_aliases={a: a for a in range(2 * n)},
        compiler_params=pltpu.CompilerParams(has_side_effects=_FLOWING))(*[hbm(t) for t in both])
    return outs[:2 * n], outs[2 * n], outs[2 * n + 1], outs[2 * n + 2]


def _rs_owners_wait(bufs, send_sems, recv_sems, kinds, after, name):
    n = len(bufs) // 2

    def body(*refs):
        q_refs, got_refs = refs[:n], refs[n:2 * n]
        send_sems, recv_sems = refs[2 * n], refs[2 * n + 1]
        x, y, c = _where_am_i()
        for a in range(n):
            for k, (cx, cy) in enumerate([(1 - x, y), (x, 1 - y), (1 - x, 1 - y)]):
                cp = pltpu.make_async_remote_copy(src_ref=_shard_of_half(q_refs[a], kinds[a], 2 * cx + cy), dst_ref=got_refs[a].at[k],
                                                  send_sem=send_sems.at[3 * a + k], recv_sem=recv_sems.at[3 * a + k],
                                                  device_id=(cx, cy, c), device_id_type=MESH)
                cp.wait_send()
                cp.wait_recv()

    outs = pl.pallas_call(body, name=name, in_specs=[_HBM] * (2 * n) + [_SEM, _SEM, pl.BlockSpec(memory_space=pl.ANY)],
                          out_specs=[_HBM] * (2 * n), out_shape=[pltpu.HBM(b.shape, b.dtype) for b in bufs],
                          input_output_aliases={a: a for a in range(2 * n)},
                          compiler_params=pltpu.CompilerParams(has_side_effects=_FLOWING))(*bufs, send_sems, recv_sems, after)
    return outs[n:]


def _rs_join(bufs, kinds):
    n = len(bufs)

    def body(*refs):
        out_refs = refs[n:2 * n]
        send_sems, recv_sems = refs[2 * n:]
        x, y, c = _where_am_i()

        def cp(a, h):
            blk = _half_of_shard(out_refs[a], kinds[a], h)
            return pltpu.make_async_remote_copy(src_ref=blk, dst_ref=blk, send_sem=send_sems.at[a], recv_sem=recv_sems.at[a],
                                                device_id=(x, y, 1 - c), device_id_type=MESH)

        for a in range(n):
            cp(a, c).start()
        for a in range(n):
            cp(a, c).wait_send()
            cp(a, 1 - c).wait_recv()

    anyspec = pl.BlockSpec(memory_space=pl.ANY)
    return pl.pallas_call(body, name="rs_join_halves", in_specs=[anyspec] * n, out_specs=[anyspec] * n,
                          out_shape=[jax.ShapeDtypeStruct(b.shape, b.dtype) for b in bufs],
                          input_output_aliases={a: a for a in range(n)},
                          scratch_shapes=[pltpu.SemaphoreType.DMA((n,)), pltpu.SemaphoreType.DMA((n,))],
                          compiler_params=pltpu.CompilerParams(has_side_effects=True))(*bufs)


def _all_reduce_small(g):
    R, C = g.shape
    H = R // 2
    NP = _nchunks(H, SUB)
    PR = H // NP

    def body(g_ref, out_ref, sib_ref, chip_ref, send_sems, recv_sems):
        x, y, c = _where_am_i()
        j = 2 * x + y
        sib = (x, y, 1 - c)
        chips = [(1 - x, y), (x, 1 - y), (1 - x, 1 - y)]
        rows = pl.ds(pl.multiple_of(c * H, SUB), H)

        def cp(k, src, dst, to):
            return pltpu.make_async_remote_copy(src_ref=src, dst_ref=dst, send_sem=send_sems.at[k], recv_sem=recv_sems.at[k],
                                                device_id=to, device_id_type=MESH)

        def pieces(k, src, dst, to):
            for q in range(NP):
                cp(k, src.at[pl.ds(q * PR, PR)], dst.at[pl.ds(q * PR, PR)], to).start()

        for half in range(2):
            pieces(0, g_ref.at[pl.ds(half * H, H)], sib_ref.at[pl.ds(half * H, H)], sib)
        cp(0, g_ref, sib_ref, sib).wait()
        chip_ref[j] = g_ref[rows, :] + sib_ref[rows, :]
        for k, (cx, cy) in enumerate(chips):
            pieces(1 + k, chip_ref.at[j], chip_ref.at[j], (cx, cy, c))
        for k, (cx, cy) in enumerate(chips):
            blk = chip_ref.at[2 * cx + cy]
            cp(1 + k, blk, blk, (cx, cy, c)).wait()
        out_ref[rows, :] = ((chip_ref[0] + chip_ref[1]) + chip_ref[2]) + chip_ref[3]
        other = out_ref.at[pl.ds(pl.multiple_of((1 - c) * H, SUB), H)]
        pieces(4, out_ref.at[rows], out_ref.at[rows], sib)
        cp(4, other, other, sib).wait()

    vm = pl.BlockSpec(memory_space=pltpu.VMEM)
    return pl.pallas_call(body, name="all_reduce_small", in_specs=[vm], out_specs=vm,
                          out_shape=jax.ShapeDtypeStruct((R, C), f32),
                          scratch_shapes=[pltpu.VMEM((R, C), f32), pltpu.VMEM((4, H, C), f32),
                                          pltpu.SemaphoreType.DMA((5,)), pltpu.SemaphoreType.DMA((5,))],
                          compiler_params=pltpu.CompilerParams(has_side_effects=True, vmem_limit_bytes=VMEM_LIMIT))(g)


SPLIT_MIN_ELEMS = 1 << 16


def _all_reduce_many(gs):
    n = len(gs)
    split = [g.ndim == 3 and g.shape[0] % 2 == 0 and g.size >= SPLIT_MIN_ELEMS for g in gs]
    part_shape = [((g.shape[0] // 2,) + g.shape[1:]) if s else g.shape for g, s in zip(gs, split)]
    n_split = sum(split)

    def body(*refs):
        g, out, sibs, chipb = refs[:n], refs[n:2 * n], refs[2 * n:3 * n], refs[3 * n:4 * n]
        send_sems, recv_sems = refs[4 * n:]
        x, y, c = _where_am_i()
        j = 2 * x + y
        sib = (x, y, 1 - c)
        chips = [(1 - x, y), (x, 1 - y), (1 - x, 1 - y)]

        def cp(k, src, dst, to):
            return pltpu.make_async_remote_copy(src_ref=src, dst_ref=dst, send_sem=send_sems.at[k], recv_sem=recv_sems.at[k],
                                                device_id=to, device_id_type=MESH)

        def part(a, h):
            return pl.ds(h * part_shape[a][0], part_shape[a][0]) if split[a] else Ellipsis

        def mine(ref, a, h):
            return ref.at[part(a, h)] if split[a] else ref

        swaps = [cp(a, g[a], sibs[a], sib) for a in range(n)]
        for d in swaps:
            d.start()
        for a in range(n):
            swaps[a].wait()
            chipb[a][j] = g[a][part(a, c)] + sibs[a][part(a, c)]
        sends = [cp(n + 3 * a + k, chipb[a].at[j], chipb[a].at[j], (cx, cy, c)) for a in range(n) for k, (cx, cy) in enumerate(chips)]
        for d in sends:
            d.start()
        for a in range(n):
            for k, (cx, cy) in enumerate(chips):
                blk = chipb[a].at[2 * cx + cy]
                cp(n + 3 * a + k, blk, blk, (cx, cy, c)).wait_recv()
            out[a][part(a, c)] = ((chipb[a][0] + chipb[a][1]) + chipb[a][2]) + chipb[a][3]
        for d in sends:
            d.wait_send()
        backs = [(a, cp(4 * n + i, mine(out[a], a, c), mine(out[a], a, c), sib)) for i, a in enumerate([a for a in range(n) if split[a]])]
        for _, d in backs:
            d.start()
        for i, (a, d) in enumerate(backs):
            d.wait_send()
            cp(4 * n + i, mine(out[a], a, 1 - c), mine(out[a], a, 1 - c), sib).wait_recv()

    vm = pl.BlockSpec(memory_space=pltpu.VMEM)
    nsem = 4 * n + n_split
    return pl.pallas_call(body, name="all_reduce_small_grads", in_specs=[vm] * n, out_specs=[vm] * n,
                          out_shape=[jax.ShapeDtypeStruct(g.shape, f32) for g in gs],
                          scratch_shapes=([pltpu.VMEM(g.shape, f32) for g in gs] + [pltpu.VMEM((4,) + ps, f32) for ps in part_shape]
                                          + [pltpu.SemaphoreType.DMA((nsem,)), pltpu.SemaphoreType.DMA((nsem,))]),
                          compiler_params=pltpu.CompilerParams(has_side_effects=True, vmem_limit_bytes=VMEM_LIMIT))(*gs)


def _adamw_many(ws, gs, ms, vs, name):
    n = len(ws)
    c1 = 1.0 / (1.0 - ADAM_B1 ** ADAM_STEP)
    c2 = 1.0 / (1.0 - ADAM_B2 ** ADAM_STEP)

    def body(*refs):
        for a in range(n):
            w_ref, g_ref, m_ref, v_ref, d_ref, nm_ref, nv_ref = (refs[i * n + a] for i in range(7))
            g_ = g_ref[...]
            nm = ADAM_B1 * m_ref[...] + (1.0 - ADAM_B1) * g_
            nv = ADAM_B2 * v_ref[...] + (1.0 - ADAM_B2) * (g_ * g_)
            nm_ref[...] = nm
            nv_ref[...] = nv
            d_ref[...] = -ADAM_LR * ((nm * c1) / (jnp.sqrt(nv * c2) + ADAM_EPS) + ADAM_WD * w_ref[...])

    vm = pl.BlockSpec(memory_space=pltpu.VMEM)
    sh = [jax.ShapeDtypeStruct(w.shape, f32) for w in ws]
    outs = pl.pallas_call(body, name=name, in_specs=[vm] * (4 * n), out_specs=[vm] * (3 * n), out_shape=sh * 3,
                          compiler_params=pltpu.CompilerParams(vmem_limit_bytes=VMEM_LIMIT))(*ws, *gs, *ms, *vs)
    return outs[:n], outs[n:2 * n], outs[2 * n:]


def _pack_flat(arrs, rows_mult):
    flat = jnp.concatenate([a.reshape(-1) for a in arrs])
    n = flat.shape[0]
    tot = -(-n // (rows_mult * LANE)) * rows_mult * LANE
    return jnp.pad(flat, (0, tot - n)).reshape(-1, LANE)


def _unpack_flat(buf, shapes):
    flat = buf.reshape(-1)
    out, o = [], 0
    for s in shapes:
        n = math.prod(s)
        out.append(flat[o:o + n].reshape(s))
        o += n
    return out


_BIG = ("mem_kv_w", "w_out", "a_w_in", "b_w_in", "c_w_in", "d_w_in")
SMALL_ROWS_MULT = 256


def _row8(v):
    v = v.reshape(-1, v.shape[-1])
    return jnp.pad(v, ((0, SUB - v.shape[0]), (0, 0)))


def kernel(x, mem, mem_kv_w, ln_g, ln_b, w_out, hgrn_lb_logits, a_w_in, a_w_s, a_b_s, b_w_in, b_norm_g, c_w_in, c_w_pool, c_scale, d_w_in, d_conv_w, d_conv_b, d_w_gx, d_b_gx, d_w_ga, d_b_ga, d_a_param, loss_target, m_mem_kv_w, m_ln_g, m_ln_b, m_w_out, m_hgrn_lb_logits, m_a_w_in, m_a_w_s, m_a_b_s, m_b_w_in, m_b_norm_g, m_c_w_in, m_c_w_pool, m_c_scale, m_d_w_in, m_d_conv_w, m_d_conv_b, m_d_w_gx, m_d_b_gx, m_d_w_ga, m_d_b_ga, m_d_a_param, v_mem_kv_w, v_ln_g, v_ln_b, v_w_out, v_hgrn_lb_logits, v_a_w_in, v_a_w_s, v_a_b_s, v_b_w_in, v_b_norm_g, v_c_w_in, v_c_w_pool, v_c_scale, v_d_w_in, v_d_conv_w, v_d_conv_b, v_d_w_gx, v_d_b_gx, v_d_w_ga, v_d_b_ga, v_d_a_param):
    names = ["mem_kv_w", "ln_g", "ln_b", "w_out", "hgrn_lb_logits", "a_w_in", "a_w_s", "a_b_s", "b_w_in", "b_norm_g", "c_w_in",
             "c_w_pool", "c_scale", "d_w_in", "d_conv_w", "d_conv_b", "d_w_gx", "d_b_gx", "d_w_ga", "d_b_ga", "d_a_param"]
    w = dict(mem_kv_w=mem_kv_w, ln_g=ln_g, ln_b=ln_b, w_out=w_out, hgrn_lb_logits=hgrn_lb_logits, a_w_in=a_w_in, a_w_s=a_w_s,
             a_b_s=a_b_s, b_w_in=b_w_in, b_norm_g=b_norm_g, c_w_in=c_w_in, c_w_pool=c_w_pool, c_scale=c_scale, d_w_in=d_w_in,
             d_conv_w=d_conv_w, d_conv_b=d_conv_b, d_w_gx=d_w_gx, d_b_gx=d_b_gx, d_w_ga=d_w_ga, d_b_ga=d_b_ga, d_a_param=d_a_param)
    m = dict(zip(names, [m_mem_kv_w, m_ln_g, m_ln_b, m_w_out, m_hgrn_lb_logits, m_a_w_in, m_a_w_s, m_a_b_s, m_b_w_in, m_b_norm_g,
                         m_c_w_in, m_c_w_pool, m_c_scale, m_d_w_in, m_d_conv_w, m_d_conv_b, m_d_w_gx, m_d_b_gx, m_d_w_ga,
                         m_d_b_ga, m_d_a_param]))
    v = dict(zip(names, [v_mem_kv_w, v_ln_g, v_ln_b, v_w_out, v_hgrn_lb_logits, v_a_w_in, v_a_w_s, v_a_b_s, v_b_w_in, v_b_norm_g,
                         v_c_w_in, v_c_w_pool, v_c_scale, v_d_w_in, v_d_conv_w, v_d_conv_b, v_d_w_gx, v_d_b_gx, v_d_w_ga,
                         v_d_b_ga, v_d_a_param]))
    xi, yi = lax.axis_index("x"), lax.axis_index("y")
    jshard = 2 * xi + yi
    x2 = x[0]
    mem2 = mem[0]
    tgt2 = loss_target[0]

    w_in_sh = [w[n][0].astype(bf16) for n in _BIG[2:]]
    w_out_sh = w_out.astype(bf16)
    gath0, tie = _ag_weights([mem_kv_w.astype(bf16), w_out_sh[0], w_in_sh[0]], ("rows", "win", "win"), jshard)
    w_kv = gath0[0].reshape(D_MODEL, 2 * XW)

    def layer_weights(g_in, g_out, g_inT=None):
        w_inT = g_in.transpose(0, 2, 1).reshape(-1, D_MODEL) if g_inT is None else g_inT.reshape(-1, D_MODEL)
        return (g_in.transpose(1, 0, 2).reshape(D_MODEL, -1), w_inT,
                g_out.reshape(D_MODEL, D_MODEL), g_out.transpose(2, 0, 1).reshape(D_MODEL, D_MODEL))

    lw = [layer_weights(gath0[2], gath0[1])]

    def gather_small(shard):
        z = jnp.zeros((4, POOL_GROUP), f32)
        return lax.dynamic_update_slice(z, shard.reshape(1, POOL_GROUP), (jshard, 0))

    sm_sh = jnp.concatenate([gather_small(b_norm_g), gather_small(c_scale), gather_small(d_conv_b), gather_small(d_a_param)]
                            + [gather_small(d_conv_w[:, r]) for r in range(4)], axis=0)
    ci = lax.axis_index("c")
    sm_all = _all_reduce_small(_pack_flat([jnp.where(ci == 0, sm_sh, 0.0)], SUB * 2) + tie[0:1, 0:1])
    pending = [None]
    tie = sm_all
    for l in range(1, DEPTH):
        bufs, ssem, rsem, tie = _ag_start([w_in_sh[l], w_out_sh[l], w_in_sh[l].T], jshard, f"gather_start{l}", tie)
        pending.append((bufs, ssem, rsem))
    tied_gain = {0: ln_g[0:1] + tie[0:1, 0:1]}
    sm = _unpack_flat(sm_all, [(8, 4 * POOL_GROUP)])[0]
    ng_full, scale_full, convb_full, ap_full = sm[0:1], sm[1:2], sm[2:3], sm[3:4]
    convw_full = sm[4:8]

    tril = jnp.tril(jnp.ones((HD, HD), bool))
    wtri = jnp.where(tril, a_w_s[0], 0.0)
    wbd = jnp.zeros((TOK, TOK), f32)
    for g in range(4):
        wbd = lax.dynamic_update_slice(wbd, c_w_pool[0, g], (g * POOL_GROUP, g * POOL_GROUP))
    kh, khT, vh, vhT, p_soft = _prep(mem2, w_kv, hgrn_lb_logits)
    prm = [
        dict(wtri=wtri.astype(bf16), wtriT=wtri.transpose(0, 2, 1).astype(bf16),
             bcolb=jnp.broadcast_to(a_b_s[0][:, :, None], (NH, HD, HD))),
        dict(lb=p_soft[1:2], ng=ng_full),
        dict(wbd=wbd.astype(bf16), wbdT=wbd.T.astype(bf16), scale=scale_full),
        dict(cw=_row8(convw_full), cb=convb_full, wgx=d_w_gx[0].astype(bf16), wgxT=d_w_gx[0].transpose(0, 2, 1).astype(bf16),
             bgx=d_b_gx.reshape(1, TOK), wga=d_w_ga[0].astype(bf16), wgaT=d_w_ga[0].transpose(0, 2, 1).astype(bf16),
             bga=d_b_ga.reshape(1, TOK), ap=ap_full),
    ]

    acts = []
    h = x2
    for l in range(DEPTH):
        if l:
            bufs, ssem, rsem = pending[l]
            lw.append(layer_weights(*_ag_wait(bufs, ssem, rsem, h, f"gather_wait{l}")))
        outs = _fwd_layer(l, h, lw[l][0], lw[l][2], tied_gain.get(l, ln_g[l:l + 1]), ln_b[l:l + 1], khT, vh, prm[l],
                          tgt2 if l == DEPTH - 1 else None)
        nfix = 4 if l == DEPTH - 1 else 3
        acts.append(dict(xin=h, proj=outs[1], z=outs[2], saves=outs[nfix:]))
        if l == DEPTH - 1:
            loss_part = outs[3]
        h = outs[0]
    loss = lax.psum(0.5 / D_MODEL * jnp.sum(loss_part), ("x", "y", "c"))

    dh = h
    dln = [None] * DEPTH
    dks, dvs = [None] * DEPTH, [None] * DEPTH
    sgr = [None] * DEPTH
    jc = jnp.stack([jshard, ci]).astype(jnp.int32)
    lkinds = ("win", "rows")
    q32s, flying = [None] * DEPTH, [None] * DEPTH
    back_gain = {DEPTH - 1: ln_g[DEPTH - 1:] + (loss - loss)}
    for l in reversed(range(DEPTH)):
        a = acts[l]
        (dxin, dproj, mixedb, dyb, dln[l], dks[l], dvs[l]), sgr[l] = _bwd_layer(
            l, dh, a["z"], a["proj"], lw[l][1], lw[l][3], back_gain.get(l, ln_g[l:l + 1]), kh, khT, vh, vhT, prm[l], a["saves"])
        if _OFFS[l]["W"] // 4 % LANE:
            gw_in = _tn_gemm(a["xin"], dproj, f"grad_w_in{l}", 1).reshape(D_MODEL, 4, -1).transpose(1, 0, 2)
        else:
            gw_in = _tn_gemm_sharded(a["xin"], dproj, f"grad_w_in{l}")
        parts = [gw_in, _tn_gemm(mixedb, dyb, f"grad_w_out{l}", 1)]
        lk = lkinds
        if l == 0:
            parts.append(_kv_bwd(mem2, dks, dvs))
            lk = lkinds + ("rows",)
        gots = _rs_swap(parts, lk, f"rs_swap_halves{l}")
        sums = [_add_sibling(p, g, k, jc, f"rs_add_sibling{l}_{i}") for i, (p, g, k) in enumerate(zip(parts, gots, lk))]
        q32s[l] = [s[0] for s in sums]
        shapes = [p.shape for p in parts]
        if l:
            bufs, ssem, rsem, tok = _rs_owners_start([s[1] for s in sums], lk, shapes, f"rs_owners_start{l}")
            flying[l] = (bufs, ssem, rsem)
            back_gain[l - 1] = ln_g[l - 1:l] + tok[0:1, 0:1]
        else:
            last_got = _rs_owners([s[1] for s in sums], lk, shapes)
        dh = dxin
    grad_x = dh[None]
    fin, fin_kinds = {}, []
    for l in range(DEPTH):
        lk = lkinds + (("rows",) if l == 0 else ())
        got = last_got if l == 0 else _rs_owners_wait(*flying[l], lk, grad_x, f"rs_owners_wait{l}")
        fin[l] = [_add_chips(q, r, k, jc, f"rs_add_chips{l}_{i}") for i, (q, r, k) in enumerate(zip(q32s[l], got, lk))]
        fin_kinds += list(lk)
    joined = _rs_join([t for l in range(DEPTH) for t in fin[l]], tuple(fin_kinds))
    by_layer, o = [], 0
    for l in range(DEPTH):
        by_layer.append(joined[o:o + len(fin[l])])
        o += len(fin[l])
    gbig = {"mem_kv_w": by_layer[0][2], "w_out": jnp.stack([by_layer[l][1] for l in range(DEPTH)])}
    for l, n in enumerate(_BIG[2:]):
        gbig[n] = by_layer[l][0]
    g_sh, d_sh, m_sh, v_sh = {}, {}, {}, {}
    for n in _BIG:
        as2d = lambda t: t.reshape(-1, t.shape[-1])
        upd = _adamw(as2d(w[n]), as2d(gbig[n]), as2d(m[n]), as2d(v[n]), f"adamw_{n}")
        g_sh[n] = gbig[n].reshape(w[n].shape)
        d_sh[n], m_sh[n], v_sh[n] = (u.reshape(w[n].shape) for u in upd)

    dbs, dlogits = _small_finish(sgr[0]["dbacc"], p_soft, sgr[1]["dlb"])
    gs = {
        "ln_g": jnp.concatenate([dln[l][0:1] for l in range(DEPTH)], axis=0),
        "ln_b": jnp.concatenate([dln[l][1:2] for l in range(DEPTH)], axis=0),
        "hgrn_lb_logits": dlogits,
        "a_w_s": sgr[0]["dwtri"][None],
        "a_b_s": dbs[:, 0:NH].T[None],
        "b_norm_g": sgr[1]["dng"][0:1],
        "c_w_pool": jnp.stack([sgr[2]["dwbd"][g * POOL_GROUP:(g + 1) * POOL_GROUP, g * POOL_GROUP:(g + 1) * POOL_GROUP]
                               for g in range(4)])[None],
        "c_scale": sgr[2]["dscale"][0:1],
        "d_conv_w": sgr[3]["dcw"][0:4][None],
        "d_conv_b": sgr[3]["dvec"][3:4],
        "d_w_gx": sgr[3]["dwgx"][None],
        "d_b_gx": sgr[3]["dvec"][1:2].reshape(1, NH, HD),
        "d_w_ga": sgr[3]["dwga"][None],
        "d_b_ga": sgr[3]["dvec"][2:3].reshape(1, NH, HD),
        "d_a_param": sgr[3]["dvec"][0:1],
    }
    small = [n for n in names if n not in _BIG]
    drop1 = lambda t: t.reshape(t.shape[1:]) if t.ndim > 2 and t.shape[0] == 1 else t
    gsum = dict(zip(small, _all_reduce_many([drop1(gs[n]) for n in small])))
    for n in ("b_norm_g", "c_scale", "d_conv_b", "d_a_param"):
        gsum[n] = lax.dynamic_slice(gsum[n], (0, jshard * POOL_GROUP), (1, POOL_GROUP))
    gsum["d_conv_w"] = lax.dynamic_slice(gsum["d_conv_w"], (0, jshard * POOL_GROUP), (4, POOL_GROUP))
    upd = _adamw_many(*[[drop1(d[n]) for n in small] for d in (w, gsum, m, v)], "adamw_small")
    gsum = {n: gsum[n].reshape(w[n].shape) for n in small}
    d_sm, m_sm, v_sm = ({n: u.reshape(w[n].shape) for n, u in zip(small, us)} for us in upd)

    grads = {**gsum, **g_sh}
    deltas = {**d_sm, **d_sh}
    new_m = {**m_sm, **m_sh}
    new_v = {**v_sm, **v_sh}
    return (loss, grad_x, *[grads[n] for n in names], *[deltas[n] for n in names], *[new_m[n] for n in names],
            *[new_v[n] for n in names])
```

```python
import functools
import math

import jax
import jax.numpy as jnp
from jax import lax
from jax.experimental import pallas as pl
from jax.experimental.pallas import tpu as pltpu

f32 = jnp.float32
bf16 = jnp.bfloat16
MM = bf16

D_MODEL = 1024
TOK = 768
XW = 256
XHEADS = 4
XDIM = 64
HD = 128
NH = TOK // HD
CHUNK = 16
POOL_GROUP = 192
DEPTH = 4
ALPHA = (2 * DEPTH) ** 0.25
LN_EPS = 1e-5
RMS_EPS = 1e-6
LRU_C = 8.0
ADAM_LR, ADAM_B1, ADAM_B2, ADAM_EPS, ADAM_WD, ADAM_STEP = 0.001, 0.9, 0.999, 1e-08, 0.01, 10

_TS = (256, 256, 256, 256)
HGRN_SUB = 128
TK = 512
SUB = 8
LANE = 128
VMEM_LIMIT = 58 * 1024 * 1024

_OFFS = (
    dict(u=0, v=768, qx=1536, gate=1792, W=2816),
    dict(q=0, f=768, i=1536, qx=2304, gate=2560, W=3584),
    dict(p=0, qx=768, gate=1024, W=2048),
    dict(xb=0, qx=768, gate=1024, W=2048),
)
_PRM = (
    ("wtri", "wtriT", "bcolb"),
    ("lb", "ng"),
    ("wbd", "wbdT", "scale"),
    ("cw", "cb", "wgx", "wgxT", "bgx", "wga", "wgaT", "bga", "ap"),
)
MESH = pl.DeviceIdType.MESH


def _mm(a, b):
    return jnp.dot(a.astype(MM), b.astype(MM), preferred_element_type=f32)


def _mm_nt(a, b):
    return lax.dot_general(a.astype(MM), b.astype(MM), (((1,), (1,)), ((), ())), preferred_element_type=f32)


def _mm_tn(a, b):
    return lax.dot_general(a.astype(MM), b.astype(MM), (((0,), (0,)), ((), ())), preferred_element_type=f32)


def _mm_sel(sel, b):
    s = sel.astype(bf16)
    hi = b.astype(bf16)
    lo = (b - hi.astype(f32)).astype(bf16)
    return jnp.dot(s, hi, preferred_element_type=f32) + jnp.dot(s, lo, preferred_element_type=f32)


def _sig(x):
    return jax.nn.sigmoid(x)


_GC = math.sqrt(2.0 / math.pi)


def _gelu(x):
    t = jnp.tanh(_GC * (x + 0.044715 * x * x * x))
    return 0.5 * x * (1.0 + t), t


def _gelu_grad(x, t):
    return 0.5 * (1.0 + t) + 0.5 * x * (1.0 - t * t) * _GC * (1.0 + 3.0 * 0.044715 * x * x)


def _rowsum(x):
    return jnp.sum(x, axis=0, keepdims=True)


def _lmean(x):
    return jnp.mean(x, axis=-1, keepdims=True)


def _ln(z):
    mu = _lmean(z)
    zc = z - mu
    rstd = lax.rsqrt(_lmean(zc * zc) + LN_EPS)
    return zc * rstd, rstd


def _ln_bwd(dxh, xhat, rstd):
    return rstd * (dxh - _lmean(dxh) - xhat * _lmean(dxh * xhat))


def _hs(h):
    return slice(h * HD, (h + 1) * HD)


def _expm1(x):
    small = x * (1.0 + x * 0.5 * (1.0 + x * (1.0 / 3.0) * (1.0 + x * 0.25 * (1.0 + x * 0.2 * (1.0 + x * (1.0 / 6.0))))))
    return jnp.where(jnp.abs(x) < 0.25, small, jnp.exp(x) - 1.0)


def _softplus(x):
    e = jnp.exp(-jnp.abs(x))
    l1p = jnp.where(e < 1e-4, e - 0.5 * e * e, jnp.log(1.0 + e))
    return jnp.maximum(x, 0.0) + l1p


def _scan_fwd(a, b):
    n = a.shape[0]
    row = lax.broadcasted_iota(jnp.int32, a.shape, 0)
    d = 1
    while d < n:
        if d % SUB:
            m = row >= d
            b = jnp.where(m, a * pltpu.roll(b, d, 0) + b, b)
            a = jnp.where(m, a * pltpu.roll(a, d, 0), a)
        else:
            b = a * jnp.concatenate([jnp.zeros((d,) + b.shape[1:], f32), b[:n - d]], axis=0) + b
            a = a * jnp.concatenate([jnp.ones((d,) + a.shape[1:], f32), a[:n - d]], axis=0)
        d *= 2
    return a, b


def _scan_bwd(a, b):
    n = a.shape[0]
    row = lax.broadcasted_iota(jnp.int32, a.shape, 0)
    d = 1
    while d < n:
        if d % SUB:
            m = row < n - d
            b = jnp.where(m, a * pltpu.roll(b, n - d, 0) + b, b)
            a = jnp.where(m, a * pltpu.roll(a, n - d, 0), a)
        else:
            b = a * jnp.concatenate([b[d:], jnp.zeros((d,) + b.shape[1:], f32)], axis=0) + b
            a = a * jnp.concatenate([a[d:], jnp.ones((d,) + a.shape[1:], f32)], axis=0)
        d *= 2
    return a, b


def _chunk_mats(n):
    r = lax.broadcasted_iota(jnp.int32, (n, n), 0)
    c = lax.broadcasted_iota(jnp.int32, (n, n), 1)
    same = (r // CHUNK) == (c // CHUNK)
    return same, jnp.logical_and(same, c <= r)


def _pool_w(shape):
    lane = lax.broadcasted_iota(jnp.int32, shape, 1)
    return jnp.where(lane < POOL_GROUP, 2, jnp.where(lane < 2 * POOL_GROUP, 4, jnp.where(lane < 3 * POOL_GROUP, 8, 16)))


def _pool_pick(r1, r2, r3, r4):
    lane = lax.broadcasted_iota(jnp.int32, r1.shape, 1)
    return jnp.where(lane < POOL_GROUP, r1, jnp.where(lane < 2 * POOL_GROUP, r2, jnp.where(lane < 3 * POOL_GROUP, r3, r4)))


def _const_spec(a):
    nd = a.ndim
    return pl.BlockSpec(a.shape, lambda i, _nd=nd: (0,) * _nd, pipeline_mode=pl.Buffered(1))


def _acc_spec(shape):
    nd = len(shape)
    return pl.BlockSpec(shape, lambda i, _nd=nd: (0,) * _nd)


def _params(sem="arbitrary"):
    return pltpu.CompilerParams(dimension_semantics=(sem,), vmem_limit_bytes=VMEM_LIMIT)


def _xattn_fwd(qx, khT_ref, vh_ref):
    xo = jnp.zeros((qx.shape[0], XW), f32)
    ps = []
    for h in range(XHEADS):
        s = _mm(qx, khT_ref[h]) * (XDIM ** -0.5)
        e = jnp.exp(s - jnp.max(s, axis=-1, keepdims=True))
        p = e / jnp.sum(e, axis=-1, keepdims=True)
        xo = xo + _mm(p, vh_ref[h])
        ps.append(p)
    return xo, ps


def _hgrn_parallel(q_raw, fl, lb):
    n = q_raw.shape[0]
    same, tri = _chunk_mats(n)
    sq = _sig(q_raw)
    qf = q_raw * sq
    sgm = _sig(fl)
    f = lb + (1.0 - lb) * sgm
    logf = jnp.log(f)
    k = 1.0 - f
    g = _mm_sel(tri, logf)
    gl = _mm_sel(same, logf)
    eg = jnp.exp(g)
    eng = jnp.exp(-g)
    ee = jnp.exp(gl - g)
    return dict(sq=sq, qf=qf, sgm=sgm, f=f, k=k, eg=eg, eng=eng, ee=ee, q_dec=qf * eg, k_inv=k * eng, k_end=k * ee,
                a=jnp.exp(gl))


def _hgrn_intra(q_dec, k_inv, v):
    n = q_dec.shape[0]
    _, tri = _chunk_mats(HD)
    outs = []
    for h in range(NH):
        blks = []
        for b in range(n // HD):
            rs = slice(b * HD, (b + 1) * HD)
            sc = jnp.where(tri, _mm_nt(q_dec[rs, _hs(h)], k_inv[rs, _hs(h)]), 0.0)
            blks.append(_mm(sc, v[rs, _hs(h)]))
        outs.append(jnp.concatenate(blks, axis=0))
    return jnp.concatenate(outs, axis=-1)


def _cs(c):
    return slice(c * CHUNK, (c + 1) * CHUNK)


def _hgrn_inter_fwd(qdec_s, kend_s, v_s, a_s, oint_s, st_ref, states_s, u_s):
    n = qdec_s.shape[0] // CHUNK
    for c in range(n):
        for h in range(NH):
            u_s[c, h] = _mm_tn(v_s[_cs(c), _hs(h)], kend_s[_cs(c), _hs(h)])
    for h in range(NH):
        st = st_ref[h]
        for c in range(n):
            states_s[c, h] = st
            st = st * a_s[c * CHUNK:c * CHUNK + 1, _hs(h)] + u_s[c, h]
        st_ref[h] = st
    if oint_s is None:
        return
    for c in range(n):
        for h in range(NH):
            oint_s[_cs(c), _hs(h)] = _mm_nt(qdec_s[_cs(c), _hs(h)], states_s[c, h])


def _rms(o):
    outs, rs = [], []
    for h in range(NH):
        oh = o[:, _hs(h)]
        r = lax.rsqrt(_lmean(oh * oh) + RMS_EPS)
        outs.append(oh * r)
        rs.append(r)
    return jnp.concatenate(outs, axis=-1), rs


def _gmlp_core(u_raw, v_raw, wtri_ref, bcolb_ref):
    gu, tu = _gelu(u_raw)
    gv, tv = _gelu(v_raw)
    vns, rstds, mixeds = [], [], []
    for h in range(NH):
        vn, rstd = _ln(gv[:, _hs(h)])
        blks = []
        for n in range(u_raw.shape[0] // HD):
            blks.append(_mm(wtri_ref[h], vn[n * HD:(n + 1) * HD]) + bcolb_ref[h])
        vns.append(vn)
        rstds.append(rstd)
        mixeds.append(jnp.concatenate(blks, axis=0))
    mixed = jnp.concatenate(mixeds, axis=-1)
    return gu, tu, tv, vns, rstds, mixed


def _pool_core(p, carry, row0, wbd_ref):
    ext = jnp.concatenate([carry, p], axis=0)
    r1 = ext + pltpu.roll(ext, 1, 0)
    r2 = r1 + pltpu.roll(r1, 2, 0)
    r3 = r2 + pltpu.roll(r2, 4, 0)
    r4 = r3 + pltpu.roll(r3, 8, 0)
    sel = _pool_pick(r1, r2, r3, r4)[2 * SUB:]
    grow = row0 + lax.broadcasted_iota(jnp.int32, p.shape, 0)
    inv_cnt = 1.0 / jnp.minimum(grow + 1, _pool_w(p.shape)).astype(f32)
    diff = sel * inv_cnt - p
    return diff, inv_cnt, _mm(diff, wbd_ref[...])


def _lru_core(xb, ccar, row0, p):
    ext = jnp.concatenate([ccar, xb], axis=0)
    cw = p["cw"]
    x1, x2, x3 = pltpu.roll(ext, 1, 0)[SUB:], pltpu.roll(ext, 2, 0)[SUB:], pltpu.roll(ext, 3, 0)[SUB:]
    xc = cw[3:4, :] * xb + cw[2:3, :] * x1 + cw[1:2, :] * x2 + cw[0:1, :] * x3 + p["cb"][...]
    gxs, gas = [], []
    for h in range(NH):
        gxs.append(_mm(xc[:, _hs(h)], p["wgx"][h]))
        gas.append(_mm(xc[:, _hs(h)], p["wga"][h]))
    gx = _sig(jnp.concatenate(gxs, axis=-1) + p["bgx"][...])
    ga = _sig(jnp.concatenate(gas, axis=-1) + p["bga"][...])
    sp = _softplus(-p["ap"][...])
    la = -LRU_C * ga * sp
    a = jnp.exp(la)
    grow = row0 + lax.broadcasted_iota(jnp.int32, xb.shape, 0)
    first = grow == 0
    mult = jnp.where(first, 1.0, jnp.sqrt(-_expm1(2.0 * la)))
    bt = mult * gx * xc
    return dict(x1=x1, x2=x2, x3=x3, xc=xc, gx=gx, ga=ga, sp=sp, a=a, mult=mult, bt=bt, first=first)


def _fwd_layer(kind, xin, w_in, w_out, lng, lnb, khT, vh, prm, tgt):
    S = xin.shape[0]
    TS = _TS[kind]
    nt = S // TS
    off = _OFFS[kind]
    W = off["W"]
    last = tgt is not None
    pnames = _PRM[kind]
    pvals = [prm[n] for n in pnames]

    def body(*refs):
        it = iter(refs)
        xin_ref, win_ref, wout_ref, lng_ref, lnb_ref, khT_ref, vh_ref = (next(it) for _ in range(7))
        p = {n: next(it) for n in pnames}
        tgt_ref = next(it) if last else None
        xout_ref, proj_ref, z_ref = next(it), next(it), next(it)
        loss_ref = next(it) if last else None
        rest = list(it)
        i = pl.program_id(0)
        x = xin_ref[...]
        proj_ref[...] = _mm(x, win_ref[...])

        if kind == 0:
            gu, _, _, _, _, mixed = _gmlp_core(proj_ref[:, 0:TOK], proj_ref[:, TOK:2 * TOK], p["wtri"], p["bcolb"])
            tok = gu * mixed
        elif kind == 1:
            st_save, o_save, st_ref, states_s, u_s, qdec_s, kend_s, v_s, a_s, oint_s = rest

            @pl.when(i == 0)
            def _():
                st_ref[...] = jnp.zeros_like(st_ref)

            st_save[0, 0] = st_ref[...]
            v = proj_ref[:, 2 * TOK:3 * TOK]
            hp = _hgrn_parallel(proj_ref[:, 0:TOK], proj_ref[:, TOK:2 * TOK], p["lb"][...])
            qdec_s[...] = hp["q_dec"]
            kend_s[...] = hp["k_end"]
            v_s[...] = v
            a_s[...] = hp["a"]
            o_intra = _hgrn_intra(hp["q_dec"], hp["k_inv"], v)
            _hgrn_inter_fwd(qdec_s, kend_s, v_s, a_s, oint_s, st_ref, states_s, u_s)
            o = o_intra + oint_s[...]
            o_save[0] = o
            for sub in range(1, TS // HGRN_SUB):
                st_save[0, sub] = states_s[sub * HGRN_SUB // CHUNK]
            on, _ = _rms(o)
            tok = on * p["ng"][...]
        elif kind == 2:
            pc_save, pcar = rest

            @pl.when(i == 0)
            def _():
                pcar[...] = jnp.zeros_like(pcar)

            pc_save[0] = pcar[...]
            pp = proj_ref[:, 0:TOK]
            _, _, y = _pool_core(pp, pcar[...], i * TS, p["wbd"])
            pcar[...] = pp[TS - 2 * SUB:, :]
            tok = y * p["scale"][...]
        else:
            cc_save, hc_save, h_save, ccar, hcar = rest

            @pl.when(i == 0)
            def _():
                ccar[...] = jnp.zeros_like(ccar)
                hcar[...] = jnp.zeros_like(hcar)

            cc_save[0] = ccar[...]
            hc_save[0] = hcar[...]
            xb = proj_ref[:, 0:TOK]
            lc = _lru_core(xb, ccar[...], i * TS, p)
            P, B = _scan_fwd(lc["a"], lc["bt"])
            tok = P * hcar[SUB - 1:SUB, :] + B
            h_save[0] = tok
            ccar[...] = xb[TS - SUB:, :]
            hcar[...] = tok[TS - SUB:, :]

        xo, _ = _xattn_fwd(proj_ref[:, off["qx"]:off["qx"] + XW], khT_ref, vh_ref)
        gate = proj_ref[:, off["gate"]:off["gate"] + D_MODEL]
        mixed = jnp.concatenate([tok, xo], axis=-1) * (gate * _sig(gate))
        z = ALPHA * x + _mm(mixed, wout_ref[...])
        z_ref[...] = z
        xhat, _ = _ln(z)
        xout = xhat * lng_ref[...] + lnb_ref[...]
        if last:
            e = xout - tgt_ref[...]
            xout_ref[...] = e * (1.0 / D_MODEL)
            es = _rowsum(e * e)
            tot = es[:, 0:LANE]
            for j in range(1, D_MODEL // LANE):
                tot = tot + es[:, j * LANE:(j + 1) * LANE]

            @pl.when(i == 0)
            def _():
                loss_ref[...] = jnp.zeros_like(loss_ref)

            loss_ref[0:1, :] += tot
        else:
            xout_ref[...] = xout

    tile = lambda w: pl.BlockSpec((TS, w), lambda i: (i, 0))
    in_arrays = [xin, w_in, w_out, lng, lnb, khT, vh] + pvals + ([tgt] if last else [])
    in_specs = [tile(D_MODEL)] + [_const_spec(a) for a in in_arrays[1:7 + len(pvals)]] + ([tile(D_MODEL)] if last else [])
    out_shape = [jax.ShapeDtypeStruct((S, D_MODEL), f32), jax.ShapeDtypeStruct((S, W), f32), jax.ShapeDtypeStruct((S, D_MODEL), f32)]
    out_specs = [tile(D_MODEL), tile(W), tile(D_MODEL)]
    if last:
        out_shape.append(jax.ShapeDtypeStruct((SUB, LANE), f32))
        out_specs.append(_acc_spec((SUB, LANE)))
    scratch = []
    save = lambda *s: (jax.ShapeDtypeStruct((nt,) + s, f32), pl.BlockSpec((1,) + s, lambda i, _n=len(s): (i,) + (0,) * _n))
    if kind == 1:
        saved = [save(TS // HGRN_SUB, NH, HD, HD), save(TS, TOK)]
        scratch = ([pltpu.VMEM((NH, HD, HD), f32)] + [pltpu.VMEM((TS // CHUNK, NH, HD, HD), f32)] * 2
                   + [pltpu.VMEM((TS, TOK), f32)] * 5)
    elif kind == 2:
        saved = [save(2 * SUB, TOK)]
        scratch = [pltpu.VMEM((2 * SUB, TOK), f32)]
    elif kind == 3:
        saved = [save(SUB, TOK), save(SUB, TOK), save(TS, TOK)]
        scratch = [pltpu.VMEM((SUB, TOK), f32)] * 2
    else:
        saved = []
    for sh, sp in saved:
        out_shape.append(sh)
        out_specs.append(sp)
    return pl.pallas_call(body, name=f"fwd_layer{kind}", grid=(nt,), in_specs=in_specs, out_specs=out_specs,
                          out_shape=out_shape, scratch_shapes=scratch, compiler_params=_params())(*in_arrays)


def _small_grad_shapes(kind):
    if kind == 0:
        return dict(dwtri=(NH, HD, HD), dbacc=(NH, HD, HD))
    if kind == 1:
        return dict(dlb=(SUB, TOK), dng=(SUB, TOK))
    if kind == 2:
        return dict(dwbd=(TOK, TOK), dscale=(SUB, TOK))
    return dict(dcw=(SUB, TOK), dvec=(SUB, TOK), dwgx=(NH, HD, HD), dwga=(NH, HD, HD))


def _bwd_layer(kind, dxout, z, proj, w_inT, w_outT, lng, kh, khT, vh, vhT, prm, saves):
    S = dxout.shape[0]
    TS = _TS[kind]
    nt = S // TS
    off = _OFFS[kind]
    W = off["W"]
    pnames = _PRM[kind]
    pvals = [prm[n] for n in pnames]
    sg_shapes = _small_grad_shapes(kind)
    sg_names = list(sg_shapes)
    n_saves = len(saves)

    def body(*refs):
        it = iter(refs)
        dxo_ref, z_ref, proj_ref, winT_ref, woutT_ref, lng_ref, kh_ref, khT_ref, vh_ref, vhT_ref = (next(it) for _ in range(10))
        p = {n: next(it) for n in pnames}
        sv = [next(it) for _ in range(n_saves)]
        dxin_ref, dproj_ref, mixed_ref, dy_ref, dln_ref, dk_ref, dv_ref = (next(it) for _ in range(7))
        sg = {n: next(it) for n in sg_names}
        rest = list(it)
        step = pl.program_id(0)
        i = nt - 1 - step

        @pl.when(step == 0)
        def _():
            dln_ref[...] = jnp.zeros_like(dln_ref)
            dk_ref[...] = jnp.zeros_like(dk_ref)
            dv_ref[...] = jnp.zeros_like(dv_ref)
            for n in sg_names:
                sg[n][...] = jnp.zeros_like(sg[n])

        dxo = dxo_ref[...]
        xhat, rstd = _ln(z_ref[...])
        dln_ref[0:1, :] += _rowsum(dxo * xhat)
        dln_ref[1:2, :] += _rowsum(dxo)
        dz = _ln_bwd(dxo * lng_ref[...], xhat, rstd)
        dyb = dz.astype(bf16)
        dy_ref[...] = dyb
        dmixed = _mm(dyb, woutT_ref[...])

        aux = {}
        if kind == 0:
            u_raw, v_raw = proj_ref[:, 0:TOK], proj_ref[:, TOK:2 * TOK]
            gu, tu, tv, vns, rstds, mx = _gmlp_core(u_raw, v_raw, p["wtri"], p["bcolb"])
            tok = gu * mx
        elif kind == 1:
            st_save, o_save = sv
            (dst_ref, fst_ref, states_s, dsts_s, u_s, qdec_s, kend_s, v_s, a_s, do_s, dqdec_s, dkend_s, dv_s,
             dgl_s) = rest

            @pl.when(step == 0)
            def _():
                dst_ref[...] = jnp.zeros_like(dst_ref)

            o = o_save[0]
            on, rs = _rms(o)
            tok = on * p["ng"][...]
            aux = dict(o=o, on=on, rs=rs)
        elif kind == 2:
            pc_save, = sv
            dpcar, = rest
            pp = proj_ref[:, 0:TOK]
            diff, inv_cnt, y = _pool_core(pp, pc_save[0], i * TS, p["wbd"])
            tok = y * p["scale"][...]
        else:
            cc_save, hc_save, h_save = sv
            dccar, gcar = rest
            xb = proj_ref[:, 0:TOK]
            lc = _lru_core(xb, cc_save[0], i * TS, p)
            hin = hc_save[0, SUB - 1:SUB, :]
            tok = h_save[0]

        xo, ps = _xattn_fwd(proj_ref[:, off["qx"]:off["qx"] + XW], khT_ref, vh_ref)
        gate = proj_ref[:, off["gate"]:off["gate"] + D_MODEL]
        sgm = _sig(gate)
        sgate = gate * sgm
        cat = jnp.concatenate([tok, xo], axis=-1)
        mixed_ref[...] = (cat * sgate).astype(bf16)
        dcat = dmixed * sgate
        dproj_ref[:, off["gate"]:off["gate"] + D_MODEL] = (dmixed * cat * (sgm * (1.0 + gate * (1.0 - sgm)))).astype(bf16)
        dtok = dcat[:, 0:TOK]
        dxo_att = dcat[:, TOK:]

        qx = proj_ref[:, off["qx"]:off["qx"] + XW]
        dqx = jnp.zeros((TS, XW), f32)
        for h in range(XHEADS):
            dp = _mm(dxo_att, vhT_ref[h])
            ds = ps[h] * (dp - jnp.sum(dp * ps[h], axis=-1, keepdims=True)) * (XDIM ** -0.5)
            dqx = dqx + _mm(ds, kh_ref[h])
            dk_ref[h] += _mm_tn(ds, qx)
            dv_ref[h] += _mm_tn(ps[h], dxo_att)
        dproj_ref[:, off["qx"]:off["qx"] + XW] = dqx.astype(bf16)

        if kind == 0:
            tril = lax.broadcasted_iota(jnp.int32, (HD, HD), 1) <= lax.broadcasted_iota(jnp.int32, (HD, HD), 0)
            dgu = dtok * mx
            dmx = dtok * gu
            dgvs = []
            for h in range(NH):
                dmh = dmx[:, _hs(h)]
                blks = []
                for n in range(TS // HD):
                    rs_ = slice(n * HD, (n + 1) * HD)
                    blks.append(_mm(p["wtriT"][h], dmh[rs_]))
                    sg["dwtri"][h] += jnp.where(tril, _mm_nt(dmh[rs_], vns[h][rs_]), 0.0)
                    sg["dbacc"][h] += dmh[rs_]
                dgvs.append(_ln_bwd(jnp.concatenate(blks, axis=0), vns[h], rstds[h]))
            dgv = jnp.concatenate(dgvs, axis=-1)
            dproj_ref[:, 0:TOK] = (dgu * _gelu_grad(u_raw, tu)).astype(bf16)
            dproj_ref[:, TOK:2 * TOK] = (dgv * _gelu_grad(v_raw, tv)).astype(bf16)
        elif kind == 1:
            o, on, rs = aux["o"], aux["on"], aux["rs"]
            ng = p["ng"][...]
            lb = p["lb"][...]
            sg["dng"][0:1, :] += _rowsum(dtok * on)
            dn = dtok * ng
            dos = []
            for h in range(NH):
                oh, r = o[:, _hs(h)], rs[h]
                dos.append(r * (dn[:, _hs(h)] - oh * (r * r) * _lmean(dn[:, _hs(h)] * oh)))
            do_all = jnp.concatenate(dos, axis=-1)
            _, tri = _chunk_mats(HD)
            same, _ = _chunk_mats(HGRN_SUB)
            triT = jnp.logical_and(same, lax.broadcasted_iota(jnp.int32, (HGRN_SUB, HGRN_SUB), 1)
                                   >= lax.broadcasted_iota(jnp.int32, (HGRN_SUB, HGRN_SUB), 0))
            row16 = lax.broadcasted_iota(jnp.int32, (CHUNK, HD), 0)
            nch = HGRN_SUB // CHUNK
            for sub in reversed(range(TS // HGRN_SUB)):
                rr = slice(sub * HGRN_SUB, (sub + 1) * HGRN_SUB)
                q_raw, v = proj_ref[rr, 0:TOK], proj_ref[rr, 2 * TOK:3 * TOK]
                hp = _hgrn_parallel(q_raw, proj_ref[rr, TOK:2 * TOK], lb)
                qdec_s[...] = hp["q_dec"]
                kend_s[...] = hp["k_end"]
                v_s[...] = v
                a_s[...] = hp["a"]
                fst_ref[...] = st_save[0, sub]
                _hgrn_inter_fwd(qdec_s, kend_s, v_s, a_s, None, fst_ref, states_s, u_s)
                do = do_all[rr]
                do_s[...] = do
                dqd, dki, dvi = [], [], []
                for h in range(NH):
                    bq, bk, bv = [], [], []
                    for b in range(HGRN_SUB // HD):
                        rs_ = slice(b * HD, (b + 1) * HD)
                        qd, ki = hp["q_dec"][rs_, _hs(h)], hp["k_inv"][rs_, _hs(h)]
                        sc = jnp.where(tri, _mm_nt(qd, ki), 0.0)
                        dsc = jnp.where(tri, _mm_nt(do[rs_, _hs(h)], v[rs_, _hs(h)]), 0.0)
                        bv.append(_mm_tn(sc, do[rs_, _hs(h)]))
                        bq.append(_mm(dsc, ki))
                        bk.append(_mm_tn(dsc, qd))
                    dqd.append(jnp.concatenate(bq, axis=0))
                    dki.append(jnp.concatenate(bk, axis=0))
                    dvi.append(jnp.concatenate(bv, axis=0))
                dqdec_s[...] = jnp.concatenate(dqd, axis=-1)
                dk_inv = jnp.concatenate(dki, axis=-1)
                dv_s[...] = jnp.concatenate(dvi, axis=-1)
                for c in range(nch):
                    for h in range(NH):
                        u_s[c, h] = _mm_tn(do_s[_cs(c), _hs(h)], qdec_s[_cs(c), _hs(h)])
                for h in range(NH):
                    dst = dst_ref[h]
                    for c in reversed(range(nch)):
                        dsts_s[c, h] = dst
                        dst = dst * a_s[c * CHUNK:c * CHUNK + 1, _hs(h)] + u_s[c, h]
                    dst_ref[h] = dst
                for c in range(nch):
                    for h in range(NH):
                        stp = states_s[c, h]
                        dst = dsts_s[c, h]
                        dqdec_s[_cs(c), _hs(h)] += _mm(do_s[_cs(c), _hs(h)], stp)
                        dkend_s[_cs(c), _hs(h)] = _mm(v_s[_cs(c), _hs(h)], dst)
                        dv_s[_cs(c), _hs(h)] += _mm_nt(kend_s[_cs(c), _hs(h)], dst)
                        da = jnp.sum(dst * stp, axis=0, keepdims=True) * a_s[c * CHUNK:c * CHUNK + 1, _hs(h)]
                        dgl_s[_cs(c), _hs(h)] = jnp.where(row16 == 0, jnp.broadcast_to(da, (CHUNK, HD)), 0.0)
                dq_dec = dqdec_s[...]
                dk_end = dkend_s[...]
                dg = dq_dec * hp["q_dec"] - dk_inv * hp["k_inv"] - dk_end * hp["k_end"]
                dk = dk_inv * hp["eng"] + dk_end * hp["ee"]
                dglr = dk_end * hp["k_end"] + dgl_s[...]
                dlogf = _mm_sel(triT, dg) + _mm_sel(same, dglr)
                df = dlogf / hp["f"] - dk
                sg["dlb"][0:1, :] += _rowsum(df * (1.0 - hp["sgm"]))
                dproj_ref[rr, 0:TOK] = (dq_dec * hp["eg"] * (hp["sq"] * (1.0 + q_raw * (1.0 - hp["sq"])))).astype(bf16)
                dproj_ref[rr, TOK:2 * TOK] = (df * (1.0 - lb) * hp["sgm"] * (1.0 - hp["sgm"])).astype(bf16)
                dproj_ref[rr, 2 * TOK:3 * TOK] = dv_s[...].astype(bf16)
        elif kind == 2:
            @pl.when(step == 0)
            def _():
                dpcar[...] = jnp.zeros_like(dpcar)

            sg["dscale"][0:1, :] += _rowsum(dtok * y)
            dyp = dtok * p["scale"][...]
            sg["dwbd"][...] += _mm_tn(diff, dyp)
            ddiff = _mm(dyp, p["wbdT"][...])
            q = ddiff * inv_cnt
            ext = jnp.concatenate([q, dpcar[...]], axis=0)
            n = TS + 2 * SUB
            r1 = ext + pltpu.roll(ext, n - 1, 0)
            r2 = r1 + pltpu.roll(r1, n - 2, 0)
            r3 = r2 + pltpu.roll(r2, n - 4, 0)
            r4 = r3 + pltpu.roll(r3, n - 8, 0)
            dproj_ref[:, 0:TOK] = (_pool_pick(r1, r2, r3, r4)[:TS] - ddiff).astype(bf16)
            dpcar[...] = q[0:2 * SUB, :]
        else:
            @pl.when(step == 0)
            def _():
                dccar[...] = jnp.zeros_like(dccar)
                gcar[...] = jnp.zeros_like(gcar)

            a, mult, gx, ga, xc = lc["a"], lc["mult"], lc["gx"], lc["ga"], lc["xc"]
            row = lax.broadcasted_iota(jnp.int32, (TS, TOK), 0)
            an = jnp.where(row == TS - 1, 1.0, pltpu.roll(a, TS - 1, 0))
            Pb, Bb = _scan_bwd(an, dtok)
            lam = Pb * gcar[0:1, :] + Bb
            gcar[...] = (a * lam)[0:SUB, :]
            hprev = jnp.where(row == 0, jnp.broadcast_to(hin, (TS, TOK)), pltpu.roll(tok, 1, 0))
            dmult = lam * gx * xc
            dgx = lam * mult * xc
            dxc = lam * mult * gx
            dla = lam * hprev * a - jnp.where(lc["first"], 0.0, dmult * a * a / mult)
            sp = lc["sp"]
            dga = -LRU_C * sp * dla
            dsp = _rowsum(-LRU_C * ga * dla)
            sg["dvec"][0:1, :] += dsp * (-_sig(-p["ap"][...]))
            dpx = dgx * gx * (1.0 - gx)
            dpa = dga * ga * (1.0 - ga)
            sg["dvec"][1:2, :] += _rowsum(dpx)
            sg["dvec"][2:3, :] += _rowsum(dpa)
            dxcs = []
            for h in range(NH):
                dxcs.append(_mm(dpx[:, _hs(h)], p["wgxT"][h]) + _mm(dpa[:, _hs(h)], p["wgaT"][h]))
                sg["dwgx"][h] += _mm_tn(xc[:, _hs(h)], dpx[:, _hs(h)])
                sg["dwga"][h] += _mm_tn(xc[:, _hs(h)], dpa[:, _hs(h)])
            dxc = dxc + jnp.concatenate(dxcs, axis=-1)
            sg["dvec"][3:4, :] += _rowsum(dxc)
            sg["dcw"][3:4, :] += _rowsum(dxc * xb)
            sg["dcw"][2:3, :] += _rowsum(dxc * lc["x1"])
            sg["dcw"][1:2, :] += _rowsum(dxc * lc["x2"])
            sg["dcw"][0:1, :] += _rowsum(dxc * lc["x3"])
            ext = jnp.concatenate([dxc, dccar[...]], axis=0)
            n = TS + SUB
            cw = p["cw"]
            dproj_ref[:, 0:TOK] = (cw[3:4, :] * dxc + cw[2:3, :] * pltpu.roll(ext, n - 1, 0)[:TS]
                                   + cw[1:2, :] * pltpu.roll(ext, n - 2, 0)[:TS]
                                   + cw[0:1, :] * pltpu.roll(ext, n - 3, 0)[:TS]).astype(bf16)
            dccar[...] = dxc[0:SUB, :]

        dxin_ref[...] = ALPHA * dz + _mm(dproj_ref[...], winT_ref[...])

    rtile = lambda w: pl.BlockSpec((TS, w), lambda s: (nt - 1 - s, 0))
    consts = [w_inT, w_outT, lng, kh, khT, vh, vhT] + pvals
    in_arrays = [dxout, z, proj] + consts + list(saves)
    in_specs = [rtile(D_MODEL), rtile(D_MODEL), rtile(W)] + [_const_spec(a) for a in consts]
    for a in saves:
        in_specs.append(pl.BlockSpec((1,) + a.shape[1:], lambda s, _n=a.ndim - 1: (nt - 1 - s,) + (0,) * _n))
    out_shape = [jax.ShapeDtypeStruct((S, D_MODEL), f32), jax.ShapeDtypeStruct((S, W), bf16),
                 jax.ShapeDtypeStruct((S, D_MODEL), bf16), jax.ShapeDtypeStruct((S, D_MODEL), bf16),
                 jax.ShapeDtypeStruct((SUB, D_MODEL), f32), jax.ShapeDtypeStruct((XHEADS, XW, XW), f32),
                 jax.ShapeDtypeStruct((XHEADS, XW, XW), f32)]
    out_specs = [rtile(D_MODEL), rtile(W), rtile(D_MODEL), rtile(D_MODEL), _acc_spec((SUB, D_MODEL)),
                 _acc_spec((XHEADS, XW, XW)), _acc_spec((XHEADS, XW, XW))]
    for n in sg_names:
        out_shape.append(jax.ShapeDtypeStruct(sg_shapes[n], f32))
        out_specs.append(_acc_spec(sg_shapes[n]))
    if kind == 1:
        scratch = ([pltpu.VMEM((NH, HD, HD), f32)] * 2 + [pltpu.VMEM((HGRN_SUB // CHUNK, NH, HD, HD), f32)] * 3
                   + [pltpu.VMEM((HGRN_SUB, TOK), f32)] * 9)
    elif kind == 2:
        scratch = [pltpu.VMEM((2 * SUB, TOK), f32)]
    elif kind == 3:
        scratch = [pltpu.VMEM((SUB, TOK), f32)] * 2
    else:
        scratch = []
    outs = pl.pallas_call(body, name=f"bwd_layer{kind}", grid=(nt,), in_specs=in_specs, out_specs=out_specs,
                          out_shape=out_shape, scratch_shapes=scratch, compiler_params=_params())(*in_arrays)
    return outs[:7], dict(zip(sg_names, outs[7:]))


def _prep(mem, w_kv, logits):
    def body(mem_ref, w_ref, lg_ref, kh_ref, khT_ref, vh_ref, vhT_ref, p_ref):
        kv = _mm(mem_ref[...], w_ref[...])
        k, v = kv[:, 0:XW], kv[:, XW:]
        kT, vT = k.T, v.T
        col = lax.broadcasted_iota(jnp.int32, (XW, XW), 1) // XDIM
        row = lax.broadcasted_iota(jnp.int32, (XW, XW), 0) // XDIM
        for h in range(XHEADS):
            kh_ref[h] = jnp.where(col == h, k, 0.0).astype(bf16)
            vh_ref[h] = jnp.where(col == h, v, 0.0).astype(bf16)
            khT_ref[h] = jnp.where(row == h, kT, 0.0).astype(bf16)
            vhT_ref[h] = jnp.where(row == h, vT, 0.0).astype(bf16)
        lg = lg_ref[...]
        e = jnp.exp(lg - jnp.max(lg, axis=0, keepdims=True))
        p_ref[...] = e / jnp.sum(e, axis=0, keepdims=True)

    vm = pl.BlockSpec(memory_space=pltpu.VMEM)
    hs = jax.ShapeDtypeStruct((XHEADS, XW, XW), bf16)
    return pl.pallas_call(body, name="prep_memory", in_specs=[vm] * 3, out_specs=[vm] * 5,
                          out_shape=[hs, hs, hs, hs, jax.ShapeDtypeStruct(logits.shape, f32)])(mem, w_kv, logits)


def _kv_bwd(mem, dks, dvs):
    def body(mem_ref, *refs):
        out_ref = refs[-1]
        col = lax.broadcasted_iota(jnp.int32, (XW, XW), 1) // XDIM
        dk = jnp.zeros((XW, XW), f32)
        dv = jnp.zeros((XW, XW), f32)
        for l in range(DEPTH):
            for h in range(XHEADS):
                dk = dk + jnp.where(col == h, refs[l][h], 0.0)
                dv = dv + jnp.where(col == h, refs[DEPTH + l][h], 0.0)
        out_ref[:, 0:XW] = _mm_tn(mem_ref[...], dk)
        out_ref[:, XW:] = _mm_tn(mem_ref[...], dv)

    vm = pl.BlockSpec(memory_space=pltpu.VMEM)
    return pl.pallas_call(body, name="kv_bwd", in_specs=[vm] * (1 + 2 * DEPTH), out_specs=vm,
                          out_shape=jax.ShapeDtypeStruct((D_MODEL, 2 * XW), f32))(mem, *dks, *dvs)


def _tn_gemm(a, b, name, nb):
    S, M = a.shape
    N = b.shape[1]
    NB = N // nb
    nk = S // TK

    def body(a_ref, b_ref, o_ref):
        @pl.when(pl.program_id(1) == 0)
        def _():
            o_ref[...] = jnp.zeros_like(o_ref)

        o_ref[...] += _mm_tn(a_ref[...], b_ref[...])

    return pl.pallas_call(body, name=name, grid=(nb, nk),
                          in_specs=[pl.BlockSpec((TK, M), lambda j, k: (k, 0)), pl.BlockSpec((TK, NB), lambda j, k: (k, j))],
                          out_specs=pl.BlockSpec((M, NB), lambda j, k: (0, j)),
                          out_shape=jax.ShapeDtypeStruct((M, N), f32),
                          compiler_params=pltpu.CompilerParams(dimension_semantics=("parallel", "arbitrary"),
                                                               vmem_limit_bytes=VMEM_LIMIT))(a, b)


def _rows_block(R, mult=16, cap=1024):
    best = R
    for d in range(mult, min(R, cap) + 1, mult):
        if R % d == 0:
            best = d
    return best


def _tn_gemm_sharded(a, b, name):
    S, M = a.shape
    Wq = b.shape[1] // 4
    nk = S // TK

    def body(a_ref, b_ref, o_ref):
        @pl.when(pl.program_id(0) == 0)
        def _():
            o_ref[...] = jnp.zeros_like(o_ref)

        at = a_ref[...].astype(MM)
        for j in range(4):
            o_ref[j] += _mm_tn(at, b_ref[:, j * Wq:(j + 1) * Wq])

    return pl.pallas_call(body, name=name, grid=(nk,),
                          in_specs=[pl.BlockSpec((TK, M), lambda k: (k, 0)), pl.BlockSpec((TK, 4 * Wq), lambda k: (k, 0))],
                          out_specs=pl.BlockSpec((4, M, Wq), lambda k: (0, 0, 0)),
                          out_shape=jax.ShapeDtypeStruct((4, M, Wq), f32), compiler_params=_params())(a, b)


HALF_ROWS = D_MODEL // 2
SHARD_ROWS = D_MODEL // 4


def _half_of_full(ref, kind, h):
    if kind == "rows":
        cols = ref.shape[1] // 2
        return ref.at[:, pl.ds(h * cols, cols)]
    return ref.at[:, pl.ds(h * HALF_ROWS, HALF_ROWS)]


def _shard_of_half(ref, kind, j):
    if kind == "rows":
        return ref.at[pl.ds(j * SHARD_ROWS, SHARD_ROWS)]
    return ref.at[j]


def _half_of_shard(ref, kind, h):
    if kind == "rows":
        cols = ref.shape[1] // 2
        return ref.at[:, pl.ds(h * cols, cols)]
    rows = ref.shape[0] // 2
    return ref.at[pl.ds(h * rows, rows)]


def _half_shape(full_shape, kind):
    if kind == "rows":
        return (full_shape[0], full_shape[1] // 2)
    return (4, HALF_ROWS, full_shape[2])


def _shard_half_shape(full_shape, kind):
    if kind == "rows":
        return (SHARD_ROWS, full_shape[1] // 2)
    return (HALF_ROWS, full_shape[2])


def _shard_shape(full_shape, kind):
    if kind == "rows":
        return (SHARD_ROWS, full_shape[1])
    return (D_MODEL, full_shape[2])


def _ew_call(body, name, grid, jc, ins, in_specs, out_shape, out_specs):
    gs = pltpu.PrefetchScalarGridSpec(num_scalar_prefetch=1, grid=grid, in_specs=in_specs, out_specs=out_specs)
    return pl.pallas_call(body, name=name, grid_spec=gs, out_shape=out_shape,
                          compiler_params=pltpu.CompilerParams(dimension_semantics=("parallel",) * len(grid),
                                                               vmem_limit_bytes=VMEM_LIMIT))(jc, *ins)


def _add_sibling(part, got, kind, jc, name):
    def body(jc_ref, a_ref, b_ref, o_ref, ob_ref):
        s = a_ref[...] + b_ref[...]
        o_ref[...] = s
        ob_ref[...] = s.astype(bf16)

    if kind == "rows":
        R, C = part.shape[0], part.shape[1] // 2
        grid = (2,)
        mine = pl.BlockSpec((R // 2, C), lambda i, jc_ref: (i, jc_ref[1]))
        spec = pl.BlockSpec((R // 2, C), lambda i, jc_ref: (i, 0))
    else:
        C = part.shape[2]
        grid = (4, 2)
        mine = pl.BlockSpec((None, HALF_ROWS // 2, C), lambda s, i, jc_ref: (s, 2 * jc_ref[1] + i, 0))
        spec = pl.BlockSpec((None, HALF_ROWS // 2, C), lambda s, i, jc_ref: (s, i, 0))
    hs = _half_shape(part.shape, kind)
    return _ew_call(body, name, grid, jc, [part, got], [mine, spec],
                    [jax.ShapeDtypeStruct(hs, f32), jax.ShapeDtypeStruct(hs, bf16)], [spec, spec])


def _add_chips(q32, r, kind, jc, name):
    def body(jc_ref, q_ref, r_ref, out_ref):
        out_ref[...] = ((q_ref[...] + r_ref[0].astype(f32)) + r_ref[1].astype(f32)) + r_ref[2].astype(f32)

    if kind == "rows":
        C = q32.shape[1]
        grid = (1,)
        qs = pl.BlockSpec((SHARD_ROWS, C), lambda i, jc_ref: (jc_ref[0], 0))
        rs = pl.BlockSpec((3, SHARD_ROWS, C), lambda i, jc_ref: (0, 0, 0))
        os_ = pl.BlockSpec((SHARD_ROWS, C), lambda i, jc_ref: (0, jc_ref[1]))
        full_shape = (D_MODEL, 2 * C)
    else:
        C = q32.shape[2]
        grid = (2,)
        qs = pl.BlockSpec((None, HALF_ROWS // 2, C), lambda i, jc_ref: (jc_ref[0], i, 0))
        rs = pl.BlockSpec((3, HALF_ROWS // 2, C), lambda i, jc_ref: (0, i, 0))
        os_ = pl.BlockSpec((HALF_ROWS // 2, C), lambda i, jc_ref: (2 * jc_ref[1] + i, 0))
        full_shape = (4, D_MODEL, C)
    return _ew_call(body, name, grid, jc, [q32, r], [qs, rs], jax.ShapeDtypeStruct(_shard_shape(full_shape, kind), f32), os_)


def _adamw(w, g, m, v, name):
    R, C = w.shape
    br = _rows_block(R, mult=SUB, cap=512)
    c1 =1.0 / (1.0 - ADAM_B1 ** ADAM_STEP)
    c2 = 1.0 / (1.0 - ADAM_B2 ** ADAM_STEP)

    def body(w_ref, g_ref, m_ref, v_ref, d_ref, nm_ref, nv_ref):
        g_ = g_ref[...]
        nm = ADAM_B1 * m_ref[...] + (1.0 - ADAM_B1) * g_
        nv = ADAM_B2 * v_ref[...] + (1.0 - ADAM_B2) * (g_ * g_)
        nm_ref[...] = nm
        nv_ref[...] = nv
        d_ref[...] = -ADAM_LR * ((nm * c1) / (jnp.sqrt(nv * c2) + ADAM_EPS) + ADAM_WD * w_ref[...])

    spec = pl.BlockSpec((br, C), lambda i: (i, 0))
    sh = jax.ShapeDtypeStruct((R, C), f32)
    return pl.pallas_call(body, name=name, grid=(R // br,), in_specs=[spec] * 4, out_specs=[spec] * 3,
                          out_shape=[sh, sh, sh], compiler_params=_params("parallel"))(w, g, m, v)


def _small_finish(dbacc, p_soft, dlb):
    def body(db_ref, p_ref, dlb_ref, dbs_ref, dlg_ref):
        lane = lax.broadcasted_iota(jnp.int32, (HD, HD), 1)
        acc = jnp.zeros((HD, HD), f32)
        for h in range(NH):
            acc = acc + jnp.where(lane == h, jnp.sum(db_ref[h], axis=-1, keepdims=True), 0.0)
        dbs_ref[...] = acc
        p = p_ref[...]
        p1 = p[1:2, :]
        rowi = lax.broadcasted_iota(jnp.int32, p.shape, 0)
        dlg_ref[...] = dlb_ref[0:1, :] * p1 * (jnp.where(rowi == 1, 1.0, 0.0) - p)

    vm = pl.BlockSpec(memory_space=pltpu.VMEM)
    return pl.pallas_call(body, name="small_finish", in_specs=[vm] * 3, out_specs=[vm] * 2,
                          out_shape=[jax.ShapeDtypeStruct((HD, HD), f32), jax.ShapeDtypeStruct(p_soft.shape, f32)])(dbacc, p_soft, dlb)


def _where_am_i():
    return lax.axis_index("x"), lax.axis_index("y"), lax.axis_index("c")


MAX_PIECES = 8


def _nchunks(rows, mult):
    for n in range(MAX_PIECES, 0, -1):
        if rows % (n * mult) == 0:
            return n
    return 1


def _leading_pieces(src, dst):
    n = src.shape[0]
    if len(src.shape) >= 3 and n <= MAX_PIECES:
        return [(src.at[s], dst.at[s]) for s in range(n)]
    return [(src, dst)]


def _ag_weights(shards, kinds, jshard):
    n = len(shards)

    def body(*refs):
        sh_refs, out_refs, token = refs[:n], refs[2 * n:3 * n], refs[3 * n]
        send_sems, recv_sems = refs[3 * n + 1:]
        x, y, c = _where_am_i()
        j = 2 * x + y
        sib = (x, y, 1 - c)
        chips = [(1 - x, y), (x, 1 - y), (1 - x, 1 - y)]
        token[...] = jnp.zeros_like(token)

        def cp(k, src, dst, to):
            return pltpu.make_async_remote_copy(src_ref=src, dst_ref=dst, send_sem=send_sems.at[k], recv_sem=recv_sems.at[k],
                                                device_id=to, device_id_type=MESH)

        started = []
        for a in range(n):
            for k, (cx, cy) in enumerate(chips):
                d = cp(6 * a + k, _half_of_shard(sh_refs[a], kinds[a], c), _half_of_shard(out_refs[a].at[j], kinds[a], c), (cx, cy, c))
                d.start()
                started.append(d)
        for a in range(n):
            for k, (cx, cy) in enumerate(chips):
                blk = _half_of_shard(out_refs[a].at[2 * cx + cy], kinds[a], c)
                cp(6 * a + k, blk, blk, (cx, cy, c)).wait_recv()
                d = cp(6 * a + 3 + k, blk, blk, sib)
                d.start()
                started.append(d)
        for a in range(n):
            for k, (cx, cy) in enumerate(chips):
                blk = _half_of_shard(out_refs[a].at[2 * cx + cy], kinds[a], 1 - c)
                cp(6 * a + 3 + k, blk, blk, sib).wait_recv()
        for d in started:
            d.wait_send()

    placed = [lax.dynamic_update_slice(jnp.zeros((4,) + s.shape, s.dtype), s[None], (jshard,) + (0,) * s.ndim) for s in shards]
    anyspec = pl.BlockSpec(memory_space=pl.ANY)
    outs = pl.pallas_call(body, name="all_gather_weights", in_specs=[anyspec] * (2 * n),
                          out_specs=[anyspec] * n + [pl.BlockSpec(memory_space=pltpu.VMEM)],
                          out_shape=[jax.ShapeDtypeStruct(p.shape, p.dtype) for p in placed] + [jax.ShapeDtypeStruct((SUB, LANE), f32)],
                          input_output_aliases={n + a: a for a in range(n)},
                          scratch_shapes=[pltpu.SemaphoreType.DMA((6 * n,)), pltpu.SemaphoreType.DMA((6 * n,))],
                          compiler_params=pltpu.CompilerParams(has_side_effects=True))(*shards, *placed)
    return outs[:n], outs[n]


_HBM = pl.BlockSpec(memory_space=pltpu.HBM)
_SEM = pl.BlockSpec(memory_space=pltpu.SEMAPHORE)
_FLOWING = pltpu.SideEffectType.DATAFLOW_SIDE_EFFECTING


def _peers6(x, y, c):
    chips = [(1 - x, y), (x, 1 - y), (1 - x, 1 - y)]
    return [(2 * k + e, chip, c if e == 0 else 1 - c) for k, chip in enumerate(chips) for e in range(2)]


def _ag_start(shards, jshard, name, after=None):
    n = len(shards)

    def body(*refs):
        out_refs = refs[2 * n:4 * n]
        send_sems, recv_sems, token = refs[4 * n:]
        x, y, c = _where_am_i()
        j = 2 * x + y
        for a in range(n):
            for slot, (cx, cy), tc in _peers6(x, y, c):
                pltpu.make_async_remote_copy(src_ref=_half_of_shard(out_refs[a], "win", c),
                                             dst_ref=_half_of_shard(out_refs[n + a].at[j], "win", c),
                                             send_sem=send_sems.at[6 * a + slot], recv_sem=recv_sems.at[6 * a + slot],
                                             device_id=(cx, cy, tc), device_id_type=MESH).start()
        token[...] = jnp.zeros_like(token)

    fill = jnp.zeros((), f32) if after is None else after[0, 0]
    placed = [lax.dynamic_update_slice(jnp.broadcast_to(fill.astype(s.dtype), (4,) + s.shape), s[None], (jshard,) + (0,) * s.ndim)
              for s in shards]
    hbm = lambda t: pltpu.with_memory_space_constraint(t, pltpu.HBM)
    both = list(shards) + placed
    outs = pl.pallas_call(
        body, name=name, in_specs=[_HBM] * (2 * n), out_specs=[_HBM] * (2 * n) + [_SEM, _SEM, pl.BlockSpec(memory_space=pltpu.VMEM)],
        out_shape=[pltpu.HBM(p.shape, p.dtype) for p in both] + [pltpu.SemaphoreType.DMA((6 * n,)), pltpu.SemaphoreType.DMA((6 * n,)),
                                                                jax.ShapeDtypeStruct((SUB, LANE), f32)],
        input_output_aliases={a: a for a in range(2 * n)},
        compiler_params=pltpu.CompilerParams(has_side_effects=_FLOWING))(*[hbm(t) for t in both])
    return outs[:2 * n], outs[2 * n], outs[2 * n + 1], outs[2 * n + 2]


def _ag_wait(bufs, send_sems, recv_sems, after, name):
    n = len(bufs) // 2

    def body(*refs):
        sh_refs, g_refs = refs[:n], refs[n:2 * n]
        send_sems, recv_sems = refs[2 * n], refs[2 * n + 1]
        x, y, c = _where_am_i()
        for a in range(n):
            for slot, (cx, cy), tc in _peers6(x, y, c):
                cp = pltpu.make_async_remote_copy(src_ref=_half_of_shard(sh_refs[a], "win", c),
                                                  dst_ref=_half_of_shard(g_refs[a].at[2 * cx + cy], "win", tc),
                                                  send_sem=send_sems.at[6 * a + slot], recv_sem=recv_sems.at[6 * a + slot],
                                                  device_id=(cx, cy, tc), device_id_type=MESH)
                cp.wait_send()
                cp.wait_recv()

    outs = pl.pallas_call(body, name=name, in_specs=[_HBM] * (2 * n) + [_SEM, _SEM, pl.BlockSpec(memory_space=pl.ANY)],
                          out_specs=[_HBM] * (2 * n), out_shape=[pltpu.HBM(b.shape, b.dtype) for b in bufs],
                          input_output_aliases={a: a for a in range(2 * n)},
                          compiler_params=pltpu.CompilerParams(has_side_effects=_FLOWING))(*bufs, send_sems, recv_sems, after)
    return outs[n:]


def _rs_swap(parts, kinds, name):
    n = len(parts)

    def body(*refs):
        p_refs, got_refs = refs[:n], refs[n:2 * n]
        send_sems, recv_sems = refs[2 * n:]
        x, y, c = _where_am_i()

        def cp(a, src, dst):
            return pltpu.make_async_remote_copy(src_ref=src, dst_ref=dst, send_sem=send_sems.at[a], recv_sem=recv_sems.at[a],
                                                device_id=(x, y, 1 - c), device_id_type=MESH)

        for a in range(n):
            for src, dst in _leading_pieces(_half_of_full(p_refs[a], kinds[a], 1 - c), got_refs[a]):
                cp(a, src, dst).start()
        for a in range(n):
            cp(a, got_refs[a], got_refs[a]).wait()

    anyspec = pl.BlockSpec(memory_space=pl.ANY)
    return pl.pallas_call(body, name=name, in_specs=[anyspec] * n, out_specs=[anyspec] * n,
                          out_shape=[jax.ShapeDtypeStruct(_half_shape(p.shape, k), p.dtype) for p, k in zip(parts, kinds)],
                          scratch_shapes=[pltpu.SemaphoreType.DMA((n,)), pltpu.SemaphoreType.DMA((n,))],
                          compiler_params=pltpu.CompilerParams(has_side_effects=True))(*parts)


def _rs_owners(qbs, kinds, full_shapes):
    n = len(qbs)

    def body(*refs):
        q_refs, got_refs = refs[:n], refs[n:2 * n]
        send_sems, recv_sems = refs[2 * n:]
        x, y, c = _where_am_i()
        chips = [(1 - x, y), (x, 1 - y), (1 - x, 1 - y)]
        ds = []
        for a in range(n):
            for k, (cx, cy) in enumerate(chips):
                d = pltpu.make_async_remote_copy(src_ref=_shard_of_half(q_refs[a], kinds[a], 2 * cx + cy), dst_ref=got_refs[a].at[k],
                                                 send_sem=send_sems.at[3 * a + k], recv_sem=recv_sems.at[3 * a + k],
                                                 device_id=(cx, cy, c), device_id_type=MESH)
                d.start()
                ds.append(d)
        for d in ds:
            d.wait()

    anyspec = pl.BlockSpec(memory_space=pl.ANY)
    return pl.pallas_call(body, name="rs_to_owners", in_specs=[anyspec] * n, out_specs=[anyspec] * n,
                          out_shape=[jax.ShapeDtypeStruct((3,) + _shard_half_shape(fs, k), bf16) for fs, k in zip(full_shapes, kinds)],
                          scratch_shapes=[pltpu.SemaphoreType.DMA((3 * n,)), pltpu.SemaphoreType.DMA((3 * n,))],
                          compiler_params=pltpu.CompilerParams(has_side_effects=True))(*qbs)


def _rs_owners_start(qbs, kinds, full_shapes, name):
    n = len(qbs)

    def body(*refs):
        q_refs, got_refs = refs[2 * n:3 * n], refs[3 * n:4 * n]
        send_sems, recv_sems, token = refs[4 * n:]
        x, y, c = _where_am_i()
        for a in range(n):
            for k, (cx, cy) in enumerate([(1 - x, y), (x, 1 - y), (1 - x, 1 - y)]):
                pltpu.make_async_remote_copy(src_ref=_shard_of_half(q_refs[a], kinds[a], 2 * cx + cy), dst_ref=got_refs[a].at[k],
                                             send_sem=send_sems.at[3 * a + k], recv_sem=recv_sems.at[3 * a + k],
                                             device_id=(cx, cy, c), device_id_type=MESH).start()
        token[...] = jnp.zeros_like(token)

    hbm = lambda t: pltpu.with_memory_space_constraint(t, pltpu.HBM)
    lands = [lax.empty((3,) + _shard_half_shape(fs, k), bf16) for fs, k in zip(full_shapes, kinds)]
    both = list(qbs) + lands
    outs = pl.pallas_call(
        body, name=name, in_specs=[_HBM] * (2 * n), out_specs=[_HBM] * (2 * n) + [_SEM, _SEM, pl.BlockSpec(memory_space=pltpu.VMEM)],
        out_shape=[pltpu.HBM(t.shape, t.dtype) for t in both] + [pltpu.SemaphoreType.DMA((3 * n,)), pltpu.SemaphoreType.DMA((3 * n,)),
                                                                jax.ShapeDtypeStruct((SUB, LANE), f32)],
        input_output_aliases={a: a for a in range(2 * n)},
        compiler_params=pltpu.CompilerParams(has_side_effects=_FLOWING))(*[hbm(t) for t in both])
    return outs[:2 * n], outs[2 * n], outs[2 * n + 1], outs[2 * n + 2]


def _rs_owners_wait(bufs, send_sems, recv_sems, kinds, after, name):
    n = len(bufs) // 2

    def body(*refs):
        q_refs, got_refs = refs[:n], refs[n:2 * n]
        send_sems, recv_sems = refs[2 * n], refs[2 * n + 1]
        x, y, c = _where_am_i()
        for a in range(n):
            for k, (cx, cy) in enumerate([(1 - x, y), (x, 1 - y), (1 - x, 1 - y)]):
                cp = pltpu.make_async_remote_copy(src_ref=_shard_of_half(q_refs[a], kinds[a], 2 * cx + cy), dst_ref=got_refs[a].at[k],
                                                  send_sem=send_sems.at[3 * a + k], recv_sem=recv_sems.at[3 * a + k],
                                                  device_id=(cx, cy, c), device_id_type=MESH)
                cp.wait_send()
                cp.wait_recv()

    outs = pl.pallas_call(body, name=name, in_specs=[_HBM] * (2 * n) + [_SEM, _SEM, pl.BlockSpec(memory_space=pl.ANY)],
                          out_specs=[_HBM] * (2 * n), out_shape=[pltpu.HBM(b.shape, b.dtype) for b in bufs],
                          input_output_aliases={a: a for a in range(2 * n)},
                          compiler_params=pltpu.CompilerParams(has_side_effects=_FLOWING))(*bufs, send_sems, recv_sems, after)
    return outs[n:]


def _rs_join(bufs, kinds):
    n = len(bufs)

    def body(*refs):
        out_refs = refs[n:2 * n]
        send_sems, recv_sems = refs[2 * n:]
        x, y, c = _where_am_i()

        def cp(a, h):
            blk = _half_of_shard(out_refs[a], kinds[a], h)
            return pltpu.make_async_remote_copy(src_ref=blk, dst_ref=blk, send_sem=send_sems.at[a], recv_sem=recv_sems.at[a],
                                                device_id=(x, y, 1 - c), device_id_type=MESH)

        for a in range(n):
            cp(a, c).start()
        for a in range(n):
            cp(a, c).wait_send()
            cp(a, 1 - c).wait_recv()

    anyspec = pl.BlockSpec(memory_space=pl.ANY)
    return pl.pallas_call(body, name="rs_join_halves", in_specs=[anyspec] * n, out_specs=[anyspec] * n,
                          out_shape=[jax.ShapeDtypeStruct(b.shape, b.dtype) for b in bufs],
                          input_output_aliases={a: a for a in range(n)},
                          scratch_shapes=[pltpu.SemaphoreType.DMA((n,)), pltpu.SemaphoreType.DMA((n,))],
                          compiler_params=pltpu.CompilerParams(has_side_effects=True))(*bufs)


def _all_reduce_small(g):
    R, C = g.shape
    H = R // 2
    NP = _nchunks(H, SUB)
    PR = H // NP

    def body(g_ref, out_ref, sib_ref, chip_ref, send_sems, recv_sems):
        x, y, c = _where_am_i()
        j = 2 * x + y
        sib = (x, y, 1 - c)
        chips = [(1 - x, y), (x, 1 - y), (1 - x, 1 - y)]
        rows = pl.ds(pl.multiple_of(c * H, SUB), H)

        def cp(k, src, dst, to):
            return pltpu.make_async_remote_copy(src_ref=src, dst_ref=dst, send_sem=send_sems.at[k], recv_sem=recv_sems.at[k],
                                                device_id=to, device_id_type=MESH)

        def pieces(k, src, dst, to):
            for q in range(NP):
                cp(k, src.at[pl.ds(q * PR, PR)], dst.at[pl.ds(q * PR, PR)], to).start()

        for half in range(2):
            pieces(0, g_ref.at[pl.ds(half * H, H)], sib_ref.at[pl.ds(half * H, H)], sib)
        cp(0, g_ref, sib_ref, sib).wait()
        chip_ref[j] = g_ref[rows, :] + sib_ref[rows, :]
        for k, (cx, cy) in enumerate(chips):
            pieces(1 + k, chip_ref.at[j], chip_ref.at[j], (cx, cy, c))
        for k, (cx, cy) in enumerate(chips):
            blk = chip_ref.at[2 * cx + cy]
            cp(1 + k, blk, blk, (cx, cy, c)).wait()
        out_ref[rows, :] = ((chip_ref[0] + chip_ref[1]) + chip_ref[2]) + chip_ref[3]
        other = out_ref.at[pl.ds(pl.multiple_of((1 - c) * H, SUB), H)]
        pieces(4, out_ref.at[rows], out_ref.at[rows], sib)
        cp(4, other, other, sib).wait()

    vm = pl.BlockSpec(memory_space=pltpu.VMEM)
    return pl.pallas_call(body, name="all_reduce_small", in_specs=[vm], out_specs=vm,
                          out_shape=jax.ShapeDtypeStruct((R, C), f32),
                          scratch_shapes=[pltpu.VMEM((R, C), f32), pltpu.VMEM((4, H, C), f32),
                                          pltpu.SemaphoreType.DMA((5,)), pltpu.SemaphoreType.DMA((5,))],
                          compiler_params=pltpu.CompilerParams(has_side_effects=True, vmem_limit_bytes=VMEM_LIMIT))(g)


SPLIT_MIN_ELEMS = 1 << 16


def _all_reduce_many(gs):
    n = len(gs)
    split = [g.ndim == 3 and g.shape[0] % 2 == 0 and g.size >= SPLIT_MIN_ELEMS for g in gs]
    part_shape = [((g.shape[0] // 2,) + g.shape[1:]) if s else g.shape for g, s in zip(gs, split)]
    n_split = sum(split)

    def body(*refs):
        g, out, sibs, chipb = refs[:n], refs[n:2 * n], refs[2 * n:3 * n], refs[3 * n:4 * n]
        send_sems, recv_sems = refs[4 * n:]
        x, y, c = _where_am_i()
        j = 2 * x + y
        sib = (x, y, 1 - c)
        chips = [(1 - x, y), (x, 1 - y), (1 - x, 1 - y)]

        def cp(k, src, dst, to):
            return pltpu.make_async_remote_copy(src_ref=src, dst_ref=dst, send_sem=send_sems.at[k], recv_sem=recv_sems.at[k],
                                                device_id=to, device_id_type=MESH)

        def part(a, h):
            return pl.ds(h * part_shape[a][0], part_shape[a][0]) if split[a] else Ellipsis

        def mine(ref, a, h):
            return ref.at[part(a, h)] if split[a] else ref

        swaps = [cp(a, g[a], sibs[a], sib) for a in range(n)]
        for d in swaps:
            d.start()
        for a in range(n):
            swaps[a].wait()
            chipb[a][j] = g[a][part(a, c)] + sibs[a][part(a, c)]
        sends = [cp(n + 3 * a + k, chipb[a].at[j], chipb[a].at[j], (cx, cy, c)) for a in range(n) for k, (cx, cy) in enumerate(chips)]
        for d in sends:
            d.start()
        for a in range(n):
            for k, (cx, cy) in enumerate(chips):
                blk = chipb[a].at[2 * cx + cy]
                cp(n + 3 * a + k, blk, blk, (cx, cy, c)).wait_recv()
            out[a][part(a, c)] = ((chipb[a][0] + chipb[a][1]) + chipb[a][2]) + chipb[a][3]
        for d in sends:
            d.wait_send()
        backs = [(a, cp(4 * n + i, mine(out[a], a, c), mine(out[a], a, c), sib)) for i, a in enumerate([a for a in range(n) if split[a]])]
        for _, d in backs:
            d.start()
        for i, (a, d) in enumerate(backs):
            d.wait_send()
            cp(4 * n + i, mine(out[a], a, 1 - c), mine(out[a], a, 1 - c), sib).wait_recv()

    vm = pl.BlockSpec(memory_space=pltpu.VMEM)
    nsem = 4 * n + n_split
    return pl.pallas_call(body, name="all_reduce_small_grads", in_specs=[vm] * n, out_specs=[vm] * n,
                          out_shape=[jax.ShapeDtypeStruct(g.shape, f32) for g in gs],
                          scratch_shapes=([pltpu.VMEM(g.shape, f32) for g in gs] + [pltpu.VMEM((4,) + ps, f32) for ps in part_shape]
                                          + [pltpu.SemaphoreType.DMA((nsem,)), pltpu.SemaphoreType.DMA((nsem,))]),
                          compiler_params=pltpu.CompilerParams(has_side_effects=True, vmem_limit_bytes=VMEM_LIMIT))(*gs)


def _adamw_many(ws, gs, ms, vs, name):
    n = len(ws)
    c1 = 1.0 / (1.0 - ADAM_B1 ** ADAM_STEP)
    c2 = 1.0 / (1.0 - ADAM_B2 ** ADAM_STEP)

    def body(*refs):
        for a in range(n):
            w_ref, g_ref, m_ref, v_ref, d_ref, nm_ref, nv_ref = (refs[i * n + a] for i in range(7))
            g_ = g_ref[...]
            nm = ADAM_B1 * m_ref[...] + (1.0 - ADAM_B1) * g_
            nv = ADAM_B2 * v_ref[...] + (1.0 - ADAM_B2) * (g_ * g_)
            nm_ref[...] = nm
            nv_ref[...] = nv
            d_ref[...] = -ADAM_LR * ((nm * c1) / (jnp.sqrt(nv * c2) + ADAM_EPS) + ADAM_WD * w_ref[...])

    vm = pl.BlockSpec(memory_space=pltpu.VMEM)
    sh = [jax.ShapeDtypeStruct(w.shape, f32) for w in ws]
    outs = pl.pallas_call(body, name=name, in_specs=[vm] * (4 * n), out_specs=[vm] * (3 * n), out_shape=sh * 3,
                          compiler_params=pltpu.CompilerParams(vmem_limit_bytes=VMEM_LIMIT))(*ws, *gs, *ms, *vs)
    return outs[:n], outs[n:2 * n], outs[2 * n:]


def _pack_flat(arrs, rows_mult):
    flat = jnp.concatenate([a.reshape(-1) for a in arrs])
    n = flat.shape[0]
    tot = -(-n // (rows_mult * LANE)) * rows_mult * LANE
    return jnp.pad(flat, (0, tot - n)).reshape(-1, LANE)


def _unpack_flat(buf, shapes):
    flat = buf.reshape(-1)
    out, o = [], 0
    for s in shapes:
        n = math.prod(s)
        out.append(flat[o:o + n].reshape(s))
        o += n
    return out


_BIG = ("mem_kv_w", "w_out", "a_w_in", "b_w_in", "c_w_in", "d_w_in")
SMALL_ROWS_MULT = 256


def _row8(v):
    v = v.reshape(-1, v.shape[-1])
    return jnp.pad(v, ((0, SUB - v.shape[0]), (0, 0)))


def kernel(x, mem, mem_kv_w, ln_g, ln_b, w_out, hgrn_lb_logits, a_w_in, a_w_s, a_b_s, b_w_in, b_norm_g, c_w_in, c_w_pool, c_scale, d_w_in, d_conv_w, d_conv_b, d_w_gx, d_b_gx, d_w_ga, d_b_ga, d_a_param, loss_target, m_mem_kv_w, m_ln_g, m_ln_b, m_w_out, m_hgrn_lb_logits, m_a_w_in, m_a_w_s, m_a_b_s, m_b_w_in, m_b_norm_g, m_c_w_in, m_c_w_pool, m_c_scale, m_d_w_in, m_d_conv_w, m_d_conv_b, m_d_w_gx, m_d_b_gx, m_d_w_ga, m_d_b_ga, m_d_a_param, v_mem_kv_w, v_ln_g, v_ln_b, v_w_out, v_hgrn_lb_logits, v_a_w_in, v_a_w_s, v_a_b_s, v_b_w_in, v_b_norm_g, v_c_w_in, v_c_w_pool, v_c_scale, v_d_w_in, v_d_conv_w, v_d_conv_b, v_d_w_gx, v_d_b_gx, v_d_w_ga, v_d_b_ga, v_d_a_param):
    names = ["mem_kv_w", "ln_g", "ln_b", "w_out", "hgrn_lb_logits", "a_w_in", "a_w_s", "a_b_s", "b_w_in", "b_norm_g", "c_w_in",
             "c_w_pool", "c_scale", "d_w_in", "d_conv_w", "d_conv_b", "d_w_gx", "d_b_gx", "d_w_ga", "d_b_ga", "d_a_param"]
    w = dict(mem_kv_w=mem_kv_w, ln_g=ln_g, ln_b=ln_b, w_out=w_out, hgrn_lb_logits=hgrn_lb_logits, a_w_in=a_w_in, a_w_s=a_w_s,
             a_b_s=a_b_s, b_w_in=b_w_in, b_norm_g=b_norm_g, c_w_in=c_w_in, c_w_pool=c_w_pool, c_scale=c_scale, d_w_in=d_w_in,
             d_conv_w=d_conv_w, d_conv_b=d_conv_b, d_w_gx=d_w_gx, d_b_gx=d_b_gx, d_w_ga=d_w_ga, d_b_ga=d_b_ga, d_a_param=d_a_param)
    m = dict(zip(names, [m_mem_kv_w, m_ln_g, m_ln_b, m_w_out, m_hgrn_lb_logits, m_a_w_in, m_a_w_s, m_a_b_s, m_b_w_in, m_b_norm_g,
                         m_c_w_in, m_c_w_pool, m_c_scale, m_d_w_in, m_d_conv_w, m_d_conv_b, m_d_w_gx, m_d_b_gx, m_d_w_ga,
                         m_d_b_ga, m_d_a_param]))
    v = dict(zip(names, [v_mem_kv_w, v_ln_g, v_ln_b, v_w_out, v_hgrn_lb_logits, v_a_w_in, v_a_w_s, v_a_b_s, v_b_w_in, v_b_norm_g,
                         v_c_w_in, v_c_w_pool, v_c_scale, v_d_w_in, v_d_conv_w, v_d_conv_b, v_d_w_gx, v_d_b_gx, v_d_w_ga,
                         v_d_b_ga, v_d_a_param]))
    xi, yi = lax.axis_index("x"), lax.axis_index("y")
    jshard = 2 * xi + yi
    x2 = x[0]
    mem2 = mem[0]
    tgt2 = loss_target[0]

    w_in_sh = [w[n][0].astype(bf16) for n in _BIG[2:]]
    w_out_sh = w_out.astype(bf16)
    gath0, tie = _ag_weights([mem_kv_w.astype(bf16), w_out_sh[0], w_in_sh[0]], ("rows", "win", "win"), jshard)
    w_kv = gath0[0].reshape(D_MODEL, 2 * XW)

    def layer_weights(g_in, g_out):
        return (g_in.transpose(1, 0, 2).reshape(D_MODEL, -1), g_in.transpose(0, 2, 1).reshape(-1, D_MODEL),
                g_out.reshape(D_MODEL, D_MODEL), g_out.transpose(2, 0, 1).reshape(D_MODEL, D_MODEL))

    lw = [layer_weights(gath0[2], gath0[1])]

    def gather_small(shard):
        z = jnp.zeros((4, POOL_GROUP), f32)
        return lax.dynamic_update_slice(z, shard.reshape(1, POOL_GROUP), (jshard, 0))

    sm_sh = jnp.concatenate([gather_small(b_norm_g), gather_small(c_scale), gather_small(d_conv_b), gather_small(d_a_param)]
                            + [gather_small(d_conv_w[:, r]) for r in range(4)], axis=0)
    ci = lax.axis_index("c")
    sm_all = _all_reduce_small(_pack_flat([jnp.where(ci == 0, sm_sh, 0.0)], SUB * 2) + tie[0:1, 0:1])
    pending = [None]
    tie = sm_all
    for l in range(1, DEPTH):
        bufs, ssem, rsem, tie = _ag_start([w_in_sh[l], w_out_sh[l]], jshard, f"gather_start{l}", tie)
        pending.append((bufs, ssem, rsem))
    tied_gain = {0: ln_g[0:1] + tie[0:1, 0:1]}
    sm = _unpack_flat(sm_all, [(8, 4 * POOL_GROUP)])[0]
    ng_full, scale_full, convb_full, ap_full = sm[0:1], sm[1:2], sm[2:3], sm[3:4]
    convw_full = sm[4:8]

    tril = jnp.tril(jnp.ones((HD, HD), bool))
    wtri = jnp.where(tril, a_w_s[0], 0.0)
    wbd = jnp.zeros((TOK, TOK), f32)
    for g in range(4):
        wbd = lax.dynamic_update_slice(wbd, c_w_pool[0, g], (g * POOL_GROUP, g * POOL_GROUP))
    kh, khT, vh, vhT, p_soft = _prep(mem2, w_kv, hgrn_lb_logits)
    prm = [
        dict(wtri=wtri.astype(bf16), wtriT=wtri.transpose(0, 2, 1).astype(bf16),
             bcolb=jnp.broadcast_to(a_b_s[0][:, :, None], (NH, HD, HD))),
        dict(lb=p_soft[1:2], ng=ng_full),
        dict(wbd=wbd.astype(bf16), wbdT=wbd.T.astype(bf16), scale=scale_full),
        dict(cw=_row8(convw_full), cb=convb_full, wgx=d_w_gx[0].astype(bf16), wgxT=d_w_gx[0].transpose(0, 2, 1).astype(bf16),
             bgx=d_b_gx.reshape(1, TOK), wga=d_w_ga[0].astype(bf16), wgaT=d_w_ga[0].transpose(0, 2, 1).astype(bf16),
             bga=d_b_ga.reshape(1, TOK), ap=ap_full),
    ]

    acts = []
    h = x2
    for l in range(DEPTH):
        if l:
            bufs, ssem, rsem = pending[l]
            lw.append(layer_weights(*_ag_wait(bufs, ssem, rsem, h, f"gather_wait{l}")))
        outs = _fwd_layer(l, h, lw[l][0], lw[l][2], tied_gain.get(l, ln_g[l:l + 1]), ln_b[l:l + 1], khT, vh, prm[l],
                          tgt2 if l == DEPTH - 1 else None)
        nfix = 4 if l == DEPTH - 1 else 3
        acts.append(dict(xin=h, proj=outs[1], z=outs[2], saves=outs[nfix:]))
        if l == DEPTH - 1:
            loss_part = outs[3]
        h = outs[0]
    loss = lax.psum(0.5 / D_MODEL * jnp.sum(loss_part), ("x", "y", "c"))

    dh = h
    dln = [None] * DEPTH
    dks, dvs = [None] * DEPTH, [None] * DEPTH
    sgr = [None] * DEPTH
    jc = jnp.stack([jshard, ci]).astype(jnp.int32)
    lkinds = ("win", "rows")
    q32s, flying = [None] * DEPTH, [None] * DEPTH
    back_gain = {DEPTH - 1: ln_g[DEPTH - 1:] + (loss - loss)}
    for l in reversed(range(DEPTH)):
        a = acts[l]
        (dxin, dproj, mixedb, dyb, dln[l], dks[l], dvs[l]), sgr[l] = _bwd_layer(
            l, dh, a["z"], a["proj"], lw[l][1], lw[l][3], back_gain.get(l, ln_g[l:l + 1]), kh, khT, vh, vhT, prm[l], a["saves"])
        if _OFFS[l]["W"] // 4 % LANE:
            gw_in = _tn_gemm(a["xin"], dproj, f"grad_w_in{l}", 1).reshape(D_MODEL, 4, -1).transpose(1, 0, 2)
        else:
            gw_in = _tn_gemm_sharded(a["xin"], dproj, f"grad_w_in{l}")
        parts = [gw_in, _tn_gemm(mixedb, dyb, f"grad_w_out{l}", 1)]
        lk = lkinds
        if l == 0:
            parts.append(_kv_bwd(mem2, dks, dvs))
            lk = lkinds + ("rows",)
        gots = _rs_swap(parts, lk, f"rs_swap_halves{l}")
        sums = [_add_sibling(p, g, k, jc, f"rs_add_sibling{l}_{i}") for i, (p, g, k) in enumerate(zip(parts, gots, lk))]
        q32s[l] = [s[0] for s in sums]
        shapes = [p.shape for p in parts]
        bufs, ssem, rsem, tok = _rs_owners_start([s[1] for s in sums], lk, shapes, f"rs_owners_start{l}")
        flying[l] = (bufs, ssem, rsem)
        if l:
            back_gain[l - 1] = ln_g[l - 1:l] + tok[0:1, 0:1]
        dh = dxin
    grad_x = dh[None]

    dbs, dlogits = _small_finish(sgr[0]["dbacc"], p_soft, sgr[1]["dlb"])
    gs = {
        "ln_g": jnp.concatenate([dln[l][0:1] for l in range(DEPTH)], axis=0) + tok[0:1, 0:1],
        "ln_b": jnp.concatenate([dln[l][1:2] for l in range(DEPTH)], axis=0),
        "hgrn_lb_logits": dlogits,
        "a_w_s": sgr[0]["dwtri"][None],
        "a_b_s": dbs[:, 0:NH].T[None],
        "b_norm_g": sgr[1]["dng"][0:1],
        "c_w_pool": jnp.stack([sgr[2]["dwbd"][g * POOL_GROUP:(g + 1) * POOL_GROUP, g * POOL_GROUP:(g + 1) * POOL_GROUP]
                               for g in range(4)])[None],
        "c_scale": sgr[2]["dscale"][0:1],
        "d_conv_w": sgr[3]["dcw"][0:4][None],
        "d_conv_b": sgr[3]["dvec"][3:4],
        "d_w_gx": sgr[3]["dwgx"][None],
        "d_b_gx": sgr[3]["dvec"][1:2].reshape(1, NH, HD),
        "d_w_ga": sgr[3]["dwga"][None],
        "d_b_ga": sgr[3]["dvec"][2:3].reshape(1, NH, HD),
        "d_a_param": sgr[3]["dvec"][0:1],
    }
    small = [n for n in names if n not in _BIG]
    drop1 = lambda t: t.reshape(t.shape[1:]) if t.ndim > 2 and t.shape[0] == 1 else t
    gsum = dict(zip(small, _all_reduce_many([drop1(gs[n]) for n in small])))

    fin, fin_kinds = {}, []
    for l in range(DEPTH):
        lk = lkinds + (("rows",) if l == 0 else ())
        got = _rs_owners_wait(*flying[l], lk, gsum["ln_g"] if l == 0 else grad_x, f"rs_owners_wait{l}")
        fin[l] = [_add_chips(q, r, k, jc, f"rs_add_chips{l}_{i}") for i, (q, r, k) in enumerate(zip(q32s[l], got, lk))]
        fin_kinds += list(lk)
    joined = _rs_join([t for l in range(DEPTH) for t in fin[l]], tuple(fin_kinds))
    by_layer, o = [], 0
    for l in range(DEPTH):
        by_layer.append(joined[o:o + len(fin[l])])
        o += len(fin[l])
    gbig = {"mem_kv_w": by_layer[0][2], "w_out": jnp.stack([by_layer[l][1] for l in range(DEPTH)])}
    for l, n in enumerate(_BIG[2:]):
        gbig[n] = by_layer[l][0]
    g_sh, d_sh, m_sh, v_sh = {}, {}, {}, {}
    for n in _BIG:
        as2d = lambda t: t.reshape(-1, t.shape[-1])
        upd = _adamw(as2d(w[n]), as2d(gbig[n]), as2d(m[n]), as2d(v[n]), f"adamw_{n}")
        g_sh[n] = gbig[n].reshape(w[n].shape)
        d_sh[n], m_sh[n], v_sh[n] = (u.reshape(w[n].shape) for u in upd)

    for n in ("b_norm_g", "c_scale", "d_conv_b", "d_a_param"):
        gsum[n] = lax.dynamic_slice(gsum[n], (0, jshard * POOL_GROUP), (1, POOL_GROUP))
    gsum["d_conv_w"] = lax.dynamic_slice(gsum["d_conv_w"], (0, jshard * POOL_GROUP), (4, POOL_GROUP))
    upd = _adamw_many(*[[drop1(d[n]) for n in small] for d in (w, gsum, m, v)], "adamw_small")
    gsum = {n: gsum[n].reshape(w[n].shape) for n in small}
    d_sm, m_sm, v_sm = ({n: u.reshape(w[n].shape) for n, u in zip(small, us)} for us in upd)

    grads = {**gsum, **g_sh}
    deltas = {**d_sm, **d_sh}
    new_m = {**m_sm, **m_sh}
    new_v = {**v_sm, **v_sh}
    return (loss, grad_x, *[grads[n] for n in names], *[deltas[n] for n in names], *[new_m[n] for n in names],
            *[new_v[n] for n in names])
```

```python
import functools
import math

import jax
import jax.numpy as jnp
from jax import lax
from jax.experimental import pallas as pl
from jax.experimental.pallas import tpu as pltpu

f32 = jnp.float32
bf16 = jnp.bfloat16
MM = bf16

D_MODEL = 1024
TOK = 768
XW = 256
XHEADS = 4
XDIM = 64
HD = 128
NH = TOK // HD
CHUNK = 16
POOL_GROUP = 192
DEPTH = 4
ALPHA = (2 * DEPTH) ** 0.25
LN_EPS = 1e-5
RMS_EPS = 1e-6
LRU_C = 8.0
ADAM_LR, ADAM_B1, ADAM_B2, ADAM_EPS, ADAM_WD, ADAM_STEP = 0.001, 0.9, 0.999, 1e-08, 0.01, 10

_TS = (256, 256, 256, 256)
HGRN_SUB = 128
TK = 512
SUB = 8
LANE = 128
VMEM_LIMIT = 58 * 1024 * 1024

_OFFS = (
    dict(u=0, v=768, qx=1536, gate=1792, W=2816),
    dict(q=0, f=768, i=1536, qx=2304, gate=2560, W=3584),
    dict(p=0, qx=768, gate=1024, W=2048),
    dict(xb=0, qx=768, gate=1024, W=2048),
)
_PRM = (
    ("wtri", "wtriT", "bcolb"),
    ("lb", "ng"),
    ("wbd", "wbdT", "scale"),
    ("cw", "cb", "wgx", "wgxT", "bgx", "wga", "wgaT", "bga", "ap"),
)
MESH = pl.DeviceIdType.MESH


def _mm(a, b):
    return jnp.dot(a.astype(MM), b.astype(MM), preferred_element_type=f32)


def _mm_nt(a, b):
    return lax.dot_general(a.astype(MM), b.astype(MM), (((1,), (1,)), ((), ())), preferred_element_type=f32)


def _mm_tn(a, b):
    return lax.dot_general(a.astype(MM), b.astype(MM), (((0,), (0,)), ((), ())), preferred_element_type=f32)


def _mm_sel(sel, b):
    s = sel.astype(bf16)
    hi = b.astype(bf16)
    lo = (b - hi.astype(f32)).astype(bf16)
    return jnp.dot(s, hi, preferred_element_type=f32) + jnp.dot(s, lo, preferred_element_type=f32)


def _sig(x):
    return jax.nn.sigmoid(x)


_GC = math.sqrt(2.0 / math.pi)


def _gelu(x):
    t = jnp.tanh(_GC * (x + 0.044715 * x * x * x))
    return 0.5 * x * (1.0 + t), t


def _gelu_grad(x, t):
    return 0.5 * (1.0 + t) + 0.5 * x * (1.0 - t * t) * _GC * (1.0 + 3.0 * 0.044715 * x * x)


def _rowsum(x):
    return jnp.sum(x, axis=0, keepdims=True)


def _lmean(x):
    return jnp.mean(x, axis=-1, keepdims=True)


def _ln(z):
    mu = _lmean(z)
    zc = z - mu
    rstd = lax.rsqrt(_lmean(zc * zc) + LN_EPS)
    return zc * rstd, rstd


def _ln_bwd(dxh, xhat, rstd):
    return rstd * (dxh - _lmean(dxh) - xhat * _lmean(dxh * xhat))


def _hs(h):
    return slice(h * HD, (h + 1) * HD)


def _expm1(x):
    small = x * (1.0 + x * 0.5 * (1.0 + x * (1.0 / 3.0) * (1.0 + x * 0.25 * (1.0 + x * 0.2 * (1.0 + x * (1.0 / 6.0))))))
    return jnp.where(jnp.abs(x) < 0.25, small, jnp.exp(x) - 1.0)


def _softplus(x):
    e = jnp.exp(-jnp.abs(x))
    l1p = jnp.where(e < 1e-4, e - 0.5 * e * e, jnp.log(1.0 + e))
    return jnp.maximum(x, 0.0) + l1p


def _scan_fwd(a, b):
    n = a.shape[0]
    row = lax.broadcasted_iota(jnp.int32, a.shape, 0)
    d = 1
    while d < n:
        if d % SUB:
            m = row >= d
            b = jnp.where(m, a * pltpu.roll(b, d, 0) + b, b)
            a = jnp.where(m, a * pltpu.roll(a, d, 0), a)
        else:
            b = a * jnp.concatenate([jnp.zeros((d,) + b.shape[1:], f32), b[:n - d]], axis=0) + b
            a = a * jnp.concatenate([jnp.ones((d,) + a.shape[1:], f32), a[:n - d]], axis=0)
        d *= 2
    return a, b


def _scan_bwd(a, b):
    n = a.shape[0]
    row = lax.broadcasted_iota(jnp.int32, a.shape, 0)
    d = 1
    while d < n:
        if d % SUB:
            m = row < n - d
            b = jnp.where(m, a * pltpu.roll(b, n - d, 0) + b, b)
            a = jnp.where(m, a * pltpu.roll(a, n - d, 0), a)
        else:
            b = a * jnp.concatenate([b[d:], jnp.zeros((d,) + b.shape[1:], f32)], axis=0) + b
            a = a * jnp.concatenate([a[d:], jnp.ones((d,) + a.shape[1:], f32)], axis=0)
        d *= 2
    return a, b


def _chunk_mats(n):
    r = lax.broadcasted_iota(jnp.int32, (n, n), 0)
    c = lax.broadcasted_iota(jnp.int32, (n, n), 1)
    same = (r // CHUNK) == (c // CHUNK)
    return same, jnp.logical_and(same, c <= r)


def _pool_w(shape):
    lane = lax.broadcasted_iota(jnp.int32, shape, 1)
    return jnp.where(lane < POOL_GROUP, 2, jnp.where(lane < 2 * POOL_GROUP, 4, jnp.where(lane < 3 * POOL_GROUP, 8, 16)))


def _pool_pick(r1, r2, r3, r4):
    lane = lax.broadcasted_iota(jnp.int32, r1.shape, 1)
    return jnp.where(lane < POOL_GROUP, r1, jnp.where(lane < 2 * POOL_GROUP, r2, jnp.where(lane < 3 * POOL_GROUP, r3, r4)))


def _const_spec(a):
    nd = a.ndim
    return pl.BlockSpec(a.shape, lambda i, _nd=nd: (0,) * _nd, pipeline_mode=pl.Buffered(1))


def _acc_spec(shape):
    nd = len(shape)
    return pl.BlockSpec(shape, lambda i, _nd=nd: (0,) * _nd)


def _params(sem="arbitrary"):
    return pltpu.CompilerParams(dimension_semantics=(sem,), vmem_limit_bytes=VMEM_LIMIT)


def _xattn_fwd(qx, khT_ref, vh_ref):
    xo = jnp.zeros((qx.shape[0], XW), f32)
    ps = []
    for h in range(XHEADS):
        s = _mm(qx, khT_ref[h]) * (XDIM ** -0.5)
        e = jnp.exp(s - jnp.max(s, axis=-1, keepdims=True))
        p = e / jnp.sum(e, axis=-1, keepdims=True)
        xo = xo + _mm(p, vh_ref[h])
        ps.append(p)
    return xo, ps


def _hgrn_parallel(q_raw, fl, lb):
    n = q_raw.shape[0]
    same, tri = _chunk_mats(n)
    sq = _sig(q_raw)
    qf = q_raw * sq
    sgm = _sig(fl)
    f = lb + (1.0 - lb) * sgm
    logf = jnp.log(f)
    k = 1.0 - f
    g = _mm_sel(tri, logf)
    gl = _mm_sel(same, logf)
    eg = jnp.exp(g)
    eng = jnp.exp(-g)
    ee = jnp.exp(gl - g)
    return dict(sq=sq, qf=qf, sgm=sgm, f=f, k=k, eg=eg, eng=eng, ee=ee, q_dec=qf * eg, k_inv=k * eng, k_end=k * ee,
                a=jnp.exp(gl))


def _hgrn_intra(q_dec, k_inv, v):
    n = q_dec.shape[0]
    _, tri = _chunk_mats(HD)
    outs = []
    for h in range(NH):
        blks = []
        for b in range(n // HD):
            rs = slice(b * HD, (b + 1) * HD)
            sc = jnp.where(tri, _mm_nt(q_dec[rs, _hs(h)], k_inv[rs, _hs(h)]), 0.0)
            blks.append(_mm(sc, v[rs, _hs(h)]))
        outs.append(jnp.concatenate(blks, axis=0))
    return jnp.concatenate(outs, axis=-1)


def _cs(c):
    return slice(c * CHUNK, (c + 1) * CHUNK)


def _hgrn_inter_fwd(qdec_s, kend_s, v_s, a_s, oint_s, st_ref, states_s, u_s):
    n = qdec_s.shape[0] // CHUNK
    for c in range(n):
        for h in range(NH):
            u_s[c, h] = _mm_tn(v_s[_cs(c), _hs(h)], kend_s[_cs(c), _hs(h)])
    for h in range(NH):
        st = st_ref[h]
        for c in range(n):
            states_s[c, h] = st
            st = st * a_s[c * CHUNK:c * CHUNK + 1, _hs(h)] + u_s[c, h]
        st_ref[h] = st
    if oint_s is None:
        return
    for c in range(n):
        for h in range(NH):
            oint_s[_cs(c), _hs(h)] = _mm_nt(qdec_s[_cs(c), _hs(h)], states_s[c, h])


def _rms(o):
    outs, rs = [], []
    for h in range(NH):
        oh = o[:, _hs(h)]
        r = lax.rsqrt(_lmean(oh * oh) + RMS_EPS)
        outs.append(oh * r)
        rs.append(r)
    return jnp.concatenate(outs, axis=-1), rs


def _gmlp_core(u_raw, v_raw, wtri_ref, bcolb_ref):
    gu, tu = _gelu(u_raw)
    gv, tv = _gelu(v_raw)
    vns, rstds, mixeds = [], [], []
    for h in range(NH):
        vn, rstd = _ln(gv[:, _hs(h)])
        blks = []
        for n in range(u_raw.shape[0] // HD):
            blks.append(_mm(wtri_ref[h], vn[n * HD:(n + 1) * HD]) + bcolb_ref[h])
        vns.append(vn)
        rstds.append(rstd)
        mixeds.append(jnp.concatenate(blks, axis=0))
    mixed = jnp.concatenate(mixeds, axis=-1)
    return gu, tu, tv, vns, rstds, mixed


def _pool_core(p, carry, row0, wbd_ref):
    ext = jnp.concatenate([carry, p], axis=0)
    r1 = ext + pltpu.roll(ext, 1, 0)
    r2 = r1 + pltpu.roll(r1, 2, 0)
    r3 = r2 + pltpu.roll(r2, 4, 0)
    r4 = r3 + pltpu.roll(r3, 8, 0)
    sel = _pool_pick(r1, r2, r3, r4)[2 * SUB:]
    grow = row0 + lax.broadcasted_iota(jnp.int32, p.shape, 0)
    inv_cnt = 1.0 / jnp.minimum(grow + 1, _pool_w(p.shape)).astype(f32)
    diff = sel * inv_cnt - p
    return diff, inv_cnt, _mm(diff, wbd_ref[...])


def _lru_core(xb, ccar, row0, p):
    ext = jnp.concatenate([ccar, xb], axis=0)
    cw = p["cw"]
    x1, x2, x3 = pltpu.roll(ext, 1, 0)[SUB:], pltpu.roll(ext, 2, 0)[SUB:], pltpu.roll(ext, 3, 0)[SUB:]
    xc = cw[3:4, :] * xb + cw[2:3, :] * x1 + cw[1:2, :] * x2 + cw[0:1, :] * x3 + p["cb"][...]
    gxs, gas = [], []
    for h in range(NH):
        gxs.append(_mm(xc[:, _hs(h)], p["wgx"][h]))
        gas.append(_mm(xc[:, _hs(h)], p["wga"][h]))
    gx = _sig(jnp.concatenate(gxs, axis=-1) + p["bgx"][...])
    ga = _sig(jnp.concatenate(gas, axis=-1) + p["bga"][...])
    sp = _softplus(-p["ap"][...])
    la = -LRU_C * ga * sp
    a = jnp.exp(la)
    grow = row0 + lax.broadcasted_iota(jnp.int32, xb.shape, 0)
    first = grow == 0
    mult = jnp.where(first, 1.0, jnp.sqrt(-_expm1(2.0 * la)))
    bt = mult * gx * xc
    return dict(x1=x1, x2=x2, x3=x3, xc=xc, gx=gx, ga=ga, sp=sp, a=a, mult=mult, bt=bt, first=first)


def _lru_core_head(xb, ccar, row0, p, h):
    hs = _hs(h)
    ext = jnp.concatenate([ccar, xb], axis=0)
    cw = p["cw"]
    x1, x2, x3 = pltpu.roll(ext, 1, 0)[SUB:], pltpu.roll(ext, 2, 0)[SUB:], pltpu.roll(ext, 3, 0)[SUB:]
    xc = cw[3:4, hs] * xb + cw[2:3, hs] * x1 + cw[1:2, hs] * x2 + cw[0:1, hs] * x3 + p["cb"][:, hs]
    gx = _sig(_mm(xc, p["wgx"][h]) + p["bgx"][:, hs])
    ga = _sig(_mm(xc, p["wga"][h]) + p["bga"][:, hs])
    sp = _softplus(-p["ap"][:, hs])
    la = -LRU_C * ga * sp
    a = jnp.exp(la)
    first = (row0 + lax.broadcasted_iota(jnp.int32, xb.shape, 0)) == 0
    mult = jnp.where(first, 1.0, jnp.sqrt(-_expm1(2.0 * la)))
    return dict(x1=x1, x2=x2, x3=x3, xc=xc, gx=gx, ga=ga, sp=sp, a=a, mult=mult, first=first)


def _fwd_layer(kind, xin, w_in, w_out, lng, lnb, khT, vh, prm, tgt):
    S = xin.shape[0]
    TS = _TS[kind]
    nt = S // TS
    off = _OFFS[kind]
    W = off["W"]
    last = tgt is not None
    pnames = _PRM[kind]
    pvals = [prm[n] for n in pnames]

    def body(*refs):
        it = iter(refs)
        xin_ref, win_ref, wout_ref, lng_ref, lnb_ref, khT_ref, vh_ref = (next(it) for _ in range(7))
        p = {n: next(it) for n in pnames}
        tgt_ref = next(it) if last else None
        xout_ref, proj_ref, z_ref = next(it), next(it), next(it)
        loss_ref = next(it) if last else None
        rest = list(it)
        i = pl.program_id(0)
        x = xin_ref[...]
        proj_ref[...] = _mm(x, win_ref[...])

        if kind == 0:
            gu, _, _, _, _, mixed = _gmlp_core(proj_ref[:, 0:TOK], proj_ref[:, TOK:2 * TOK], p["wtri"], p["bcolb"])
            tok = gu * mixed
        elif kind == 1:
            st_save, o_save, st_ref, states_s, u_s, qdec_s, kend_s, v_s, a_s, oint_s = rest

            @pl.when(i == 0)
            def _():
                st_ref[...] = jnp.zeros_like(st_ref)

            st_save[0, 0] = st_ref[...]
            v = proj_ref[:, 2 * TOK:3 * TOK]
            hp = _hgrn_parallel(proj_ref[:, 0:TOK], proj_ref[:, TOK:2 * TOK], p["lb"][...])
            qdec_s[...] = hp["q_dec"]
            kend_s[...] = hp["k_end"]
            v_s[...] = v
            a_s[...] = hp["a"]
            o_intra = _hgrn_intra(hp["q_dec"], hp["k_inv"], v)
            _hgrn_inter_fwd(qdec_s, kend_s, v_s, a_s, oint_s, st_ref, states_s, u_s)
            o = o_intra + oint_s[...]
            o_save[0] = o
            for sub in range(1, TS // HGRN_SUB):
                st_save[0, sub] = states_s[sub * HGRN_SUB // CHUNK]
            on, _ = _rms(o)
            tok = on * p["ng"][...]
        elif kind == 2:
            pc_save, pcar = rest

            @pl.when(i == 0)
            def _():
                pcar[...] = jnp.zeros_like(pcar)

            pc_save[0] = pcar[...]
            pp = proj_ref[:, 0:TOK]
            _, _, y = _pool_core(pp, pcar[...], i * TS, p["wbd"])
            pcar[...] = pp[TS - 2 * SUB:, :]
            tok = y * p["scale"][...]
        else:
            cc_save, hc_save, h_save, ccar, hcar = rest

            @pl.when(i == 0)
            def _():
                ccar[...] = jnp.zeros_like(ccar)
                hcar[...] = jnp.zeros_like(hcar)

            cc_save[0] = ccar[...]
            hc_save[0] = hcar[...]
            xb = proj_ref[:, 0:TOK]
            lc = _lru_core(xb, ccar[...], i * TS, p)
            P, B = _scan_fwd(lc["a"], lc["bt"])
            tok = P * hcar[SUB - 1:SUB, :] + B
            h_save[0] = tok
            ccar[...] = xb[TS - SUB:, :]
            hcar[...] = tok[TS - SUB:, :]

        xo, _ = _xattn_fwd(proj_ref[:, off["qx"]:off["qx"] + XW], khT_ref, vh_ref)
        gate = proj_ref[:, off["gate"]:off["gate"] + D_MODEL]
        mixed = jnp.concatenate([tok, xo], axis=-1) * (gate * _sig(gate))
        z = ALPHA * x + _mm(mixed, wout_ref[...])
        z_ref[...] = z
        xhat, _ = _ln(z)
        xout = xhat * lng_ref[...] + lnb_ref[...]
        if last:
            e = xout - tgt_ref[...]
            xout_ref[...] = e * (1.0 / D_MODEL)
            es = _rowsum(e * e)
            tot = es[:, 0:LANE]
            for j in range(1, D_MODEL // LANE):
                tot = tot + es[:, j * LANE:(j + 1) * LANE]

            @pl.when(i == 0)
            def _():
                loss_ref[...] = jnp.zeros_like(loss_ref)

            loss_ref[0:1, :] += tot
        else:
            xout_ref[...] = xout

    tile = lambda w: pl.BlockSpec((TS, w), lambda i: (i, 0))
    in_arrays = [xin, w_in, w_out, lng, lnb, khT, vh] + pvals + ([tgt] if last else [])
    in_specs = [tile(D_MODEL)] + [_const_spec(a) for a in in_arrays[1:7 + len(pvals)]] + ([tile(D_MODEL)] if last else [])
    out_shape = [jax.ShapeDtypeStruct((S, D_MODEL), f32), jax.ShapeDtypeStruct((S, W), f32), jax.ShapeDtypeStruct((S, D_MODEL), f32)]
    out_specs = [tile(D_MODEL), tile(W), tile(D_MODEL)]
    if last:
        out_shape.append(jax.ShapeDtypeStruct((SUB, LANE), f32))
        out_specs.append(_acc_spec((SUB, LANE)))
    scratch = []
    save = lambda *s: (jax.ShapeDtypeStruct((nt,) + s, f32), pl.BlockSpec((1,) + s, lambda i, _n=len(s): (i,) + (0,) * _n))
    if kind == 1:
        saved = [save(TS // HGRN_SUB, NH, HD, HD), save(TS, TOK)]
        scratch = ([pltpu.VMEM((NH, HD, HD), f32)] + [pltpu.VMEM((TS // CHUNK, NH, HD, HD), f32)] * 2
                   + [pltpu.VMEM((TS, TOK), f32)] * 5)
    elif kind == 2:
        saved = [save(2 * SUB, TOK)]
        scratch = [pltpu.VMEM((2 * SUB, TOK), f32)]
    elif kind == 3:
        saved = [save(SUB, TOK), save(SUB, TOK), save(TS, TOK)]
        scratch = [pltpu.VMEM((SUB, TOK), f32)] * 2
    else:
        saved = []
    for sh, sp in saved:
        out_shape.append(sh)
        out_specs.append(sp)
    return pl.pallas_call(body, name=f"fwd_layer{kind}", grid=(nt,), in_specs=in_specs, out_specs=out_specs,
                          out_shape=out_shape, scratch_shapes=scratch, compiler_params=_params())(*in_arrays)


def _small_grad_shapes(kind):
    if kind == 0:
        return dict(dwtri=(NH, HD, HD), dbacc=(NH, HD, HD))
    if kind == 1:
        return dict(dlb=(SUB, TOK), dng=(SUB, TOK))
    if kind == 2:
        return dict(dwbd=(TOK, TOK), dscale=(SUB, TOK))
    return dict(dcw=(SUB, TOK), dvec=(SUB, TOK), dwgx=(NH, HD, HD), dwga=(NH, HD, HD))


def _bwd_layer(kind, dxout, z, proj, w_inT, w_outT, lng, kh, khT, vh, vhT, prm, saves):
    S = dxout.shape[0]
    TS = _TS[kind]
    nt = S // TS
    off = _OFFS[kind]
    W = off["W"]
    pnames = _PRM[kind]
    pvals = [prm[n] for n in pnames]
    sg_shapes = _small_grad_shapes(kind)
    sg_names = list(sg_shapes)
    n_saves = len(saves)

    def body(*refs):
        it = iter(refs)
        dxo_ref, z_ref, proj_ref, winT_ref, woutT_ref, lng_ref, kh_ref, khT_ref, vh_ref, vhT_ref = (next(it) for _ in range(10))
        p = {n: next(it) for n in pnames}
        sv = [next(it) for _ in range(n_saves)]
        dxin_ref, dproj_ref, mixed_ref, dy_ref, dln_ref, dk_ref, dv_ref = (next(it) for _ in range(7))
        sg = {n: next(it) for n in sg_names}
        rest = list(it)
        step = pl.program_id(0)
        i = nt - 1 - step

        @pl.when(step == 0)
        def _():
            dln_ref[...] = jnp.zeros_like(dln_ref)
            dk_ref[...] = jnp.zeros_like(dk_ref)
            dv_ref[...] = jnp.zeros_like(dv_ref)
            for n in sg_names:
                sg[n][...] = jnp.zeros_like(sg[n])

        dxo = dxo_ref[...]
        xhat, rstd = _ln(z_ref[...])
        dln_ref[0:1, :] += _rowsum(dxo * xhat)
        dln_ref[1:2, :] += _rowsum(dxo)
        dz = _ln_bwd(dxo * lng_ref[...], xhat, rstd)
        dyb = dz.astype(bf16)
        dy_ref[...] = dyb
        dmixed = _mm(dyb, woutT_ref[...])

        aux = {}
        if kind == 0:
            u_raw, v_raw = proj_ref[:, 0:TOK], proj_ref[:, TOK:2 * TOK]
            gu, tu, tv, vns, rstds, mx = _gmlp_core(u_raw, v_raw, p["wtri"], p["bcolb"])
            tok = gu * mx
        elif kind == 1:
            st_save, o_save = sv
            (dst_ref, fst_ref, states_s, dsts_s, u_s, qdec_s, kend_s, v_s, a_s, do_s, dqdec_s, dkend_s, dv_s,
             dgl_s) = rest

            @pl.when(step == 0)
            def _():
                dst_ref[...] = jnp.zeros_like(dst_ref)

            o = o_save[0]
            on, rs = _rms(o)
            tok = on * p["ng"][...]
            aux = dict(o=o, on=on, rs=rs)
        elif kind == 2:
            pc_save, = sv
            dpcar, = rest
            pp = proj_ref[:, 0:TOK]
            diff, inv_cnt, y = _pool_core(pp, pc_save[0], i * TS, p["wbd"])
            tok = y * p["scale"][...]
        else:
            cc_save, hc_save, h_save = sv
            dccar, gcar = rest
            hin = hc_save[0, SUB - 1:SUB, :]
            tok = h_save[0]

        xo, ps = _xattn_fwd(proj_ref[:, off["qx"]:off["qx"] + XW], khT_ref, vh_ref)
        gate = proj_ref[:, off["gate"]:off["gate"] + D_MODEL]
        sgm = _sig(gate)
        sgate = gate * sgm
        cat = jnp.concatenate([tok, xo], axis=-1)
        mixed_ref[...] = (cat * sgate).astype(bf16)
        dcat = dmixed * sgate
        dproj_ref[:, off["gate"]:off["gate"] + D_MODEL] = (dmixed * cat * (sgm * (1.0 + gate * (1.0 - sgm)))).astype(bf16)
        dtok = dcat[:, 0:TOK]
        dxo_att = dcat[:, TOK:]

        qx = proj_ref[:, off["qx"]:off["qx"] + XW]
        dqx = jnp.zeros((TS, XW), f32)
        for h in range(XHEADS):
            dp = _mm(dxo_att, vhT_ref[h])
            ds = ps[h] * (dp - jnp.sum(dp * ps[h], axis=-1, keepdims=True)) * (XDIM ** -0.5)
            dqx = dqx + _mm(ds, kh_ref[h])
            dk_ref[h] += _mm_tn(ds, qx)
            dv_ref[h] += _mm_tn(ps[h], dxo_att)
        dproj_ref[:, off["qx"]:off["qx"] + XW] = dqx.astype(bf16)

        if kind == 0:
            tril = lax.broadcasted_iota(jnp.int32, (HD, HD), 1) <= lax.broadcasted_iota(jnp.int32, (HD, HD), 0)
            dgu = dtok * mx
            dmx = dtok * gu
            dgvs = []
            for h in range(NH):
                dmh = dmx[:, _hs(h)]
                blks = []
                for n in range(TS // HD):
                    rs_ = slice(n * HD, (n + 1) * HD)
                    blks.append(_mm(p["wtriT"][h], dmh[rs_]))
                    sg["dwtri"][h] += jnp.where(tril, _mm_nt(dmh[rs_], vns[h][rs_]), 0.0)
                    sg["dbacc"][h] += dmh[rs_]
                dgvs.append(_ln_bwd(jnp.concatenate(blks, axis=0), vns[h], rstds[h]))
            dgv = jnp.concatenate(dgvs, axis=-1)
            dproj_ref[:, 0:TOK] = (dgu * _gelu_grad(u_raw, tu)).astype(bf16)
            dproj_ref[:, TOK:2 * TOK] = (dgv * _gelu_grad(v_raw, tv)).astype(bf16)
        elif kind == 1:
            o, on, rs = aux["o"], aux["on"], aux["rs"]
            ng = p["ng"][...]
            lb = p["lb"][...]
            sg["dng"][0:1, :] += _rowsum(dtok * on)
            dn = dtok * ng
            dos = []
            for h in range(NH):
                oh, r = o[:, _hs(h)], rs[h]
                dos.append(r * (dn[:, _hs(h)] - oh * (r * r) * _lmean(dn[:, _hs(h)] * oh)))
            do_all = jnp.concatenate(dos, axis=-1)
            _, tri = _chunk_mats(HD)
            same, _ = _chunk_mats(HGRN_SUB)
            triT = jnp.logical_and(same, lax.broadcasted_iota(jnp.int32, (HGRN_SUB, HGRN_SUB), 1)
                                   >= lax.broadcasted_iota(jnp.int32, (HGRN_SUB, HGRN_SUB), 0))
            row16 = lax.broadcasted_iota(jnp.int32, (CHUNK, HD), 0)
            nch = HGRN_SUB // CHUNK
            for sub in reversed(range(TS // HGRN_SUB)):
                rr = slice(sub * HGRN_SUB, (sub + 1) * HGRN_SUB)
                q_raw, v = proj_ref[rr, 0:TOK], proj_ref[rr, 2 * TOK:3 * TOK]
                hp = _hgrn_parallel(q_raw, proj_ref[rr, TOK:2 * TOK], lb)
                qdec_s[...] = hp["q_dec"]
                kend_s[...] = hp["k_end"]
                v_s[...] = v
                a_s[...] = hp["a"]
                fst_ref[...] = st_save[0, sub]
                _hgrn_inter_fwd(qdec_s, kend_s, v_s, a_s, None, fst_ref, states_s, u_s)
                do = do_all[rr]
                do_s[...] = do
                dqd, dki, dvi = [], [], []
                for h in range(NH):
                    bq, bk, bv = [], [], []
                    for b in range(HGRN_SUB // HD):
                        rs_ = slice(b * HD, (b + 1) * HD)
                        qd, ki = hp["q_dec"][rs_, _hs(h)], hp["k_inv"][rs_, _hs(h)]
                        sc = jnp.where(tri, _mm_nt(qd, ki), 0.0)
                        dsc = jnp.where(tri, _mm_nt(do[rs_, _hs(h)], v[rs_, _hs(h)]), 0.0)
                        bv.append(_mm_tn(sc, do[rs_, _hs(h)]))
                        bq.append(_mm(dsc, ki))
                        bk.append(_mm_tn(dsc, qd))
                    dqd.append(jnp.concatenate(bq, axis=0))
                    dki.append(jnp.concatenate(bk, axis=0))
                    dvi.append(jnp.concatenate(bv, axis=0))
                dqdec_s[...] = jnp.concatenate(dqd, axis=-1)
                dk_inv = jnp.concatenate(dki, axis=-1)
                dv_s[...] = jnp.concatenate(dvi, axis=-1)
                for c in range(nch):
                    for h in range(NH):
                        u_s[c, h] = _mm_tn(do_s[_cs(c), _hs(h)], qdec_s[_cs(c), _hs(h)])
                for h in range(NH):
                    dst = dst_ref[h]
                    for c in reversed(range(nch)):
                        dsts_s[c, h] = dst
                        dst = dst * a_s[c * CHUNK:c * CHUNK + 1, _hs(h)] + u_s[c, h]
                    dst_ref[h] = dst
                for c in range(nch):
                    for h in range(NH):
                        stp = states_s[c, h]
                        dst = dsts_s[c, h]
                        dqdec_s[_cs(c), _hs(h)] += _mm(do_s[_cs(c), _hs(h)], stp)
                        dkend_s[_cs(c), _hs(h)] = _mm(v_s[_cs(c), _hs(h)], dst)
                        dv_s[_cs(c), _hs(h)] += _mm_nt(kend_s[_cs(c), _hs(h)], dst)
                        da = jnp.sum(dst * stp, axis=0, keepdims=True) * a_s[c * CHUNK:c * CHUNK + 1, _hs(h)]
                        dgl_s[_cs(c), _hs(h)] = jnp.where(row16 == 0, jnp.broadcast_to(da, (CHUNK, HD)), 0.0)
                dq_dec = dqdec_s[...]
                dk_end = dkend_s[...]
                dg = dq_dec * hp["q_dec"] - dk_inv * hp["k_inv"] - dk_end * hp["k_end"]
                dk = dk_inv * hp["eng"] + dk_end * hp["ee"]
                dglr = dk_end * hp["k_end"] + dgl_s[...]
                dlogf = _mm_sel(triT, dg) + _mm_sel(same, dglr)
                df = dlogf / hp["f"] - dk
                sg["dlb"][0:1, :] += _rowsum(df * (1.0 - hp["sgm"]))
                dproj_ref[rr, 0:TOK] = (dq_dec * hp["eg"] * (hp["sq"] * (1.0 + q_raw * (1.0 - hp["sq"])))).astype(bf16)
                dproj_ref[rr, TOK:2 * TOK] = (df * (1.0 - lb) * hp["sgm"] * (1.0 - hp["sgm"])).astype(bf16)
                dproj_ref[rr, 2 * TOK:3 * TOK] = dv_s[...].astype(bf16)
        elif kind == 2:
            @pl.when(step == 0)
            def _():
                dpcar[...] = jnp.zeros_like(dpcar)

            sg["dscale"][0:1, :] += _rowsum(dtok * y)
            dyp = dtok * p["scale"][...]
            sg["dwbd"][...] += _mm_tn(diff, dyp)
            ddiff = _mm(dyp, p["wbdT"][...])
            q = ddiff * inv_cnt
            ext = jnp.concatenate([q, dpcar[...]], axis=0)
            n = TS + 2 * SUB
            r1 = ext + pltpu.roll(ext, n - 1, 0)
            r2 = r1 + pltpu.roll(r1, n - 2, 0)
            r3 = r2 + pltpu.roll(r2, n - 4, 0)
            r4 = r3 + pltpu.roll(r3, n - 8, 0)
            dproj_ref[:, 0:TOK] = (_pool_pick(r1, r2, r3, r4)[:TS] - ddiff).astype(bf16)
            dpcar[...] = q[0:2 * SUB, :]
        else:
            @pl.when(step == 0)
            def _():
                dccar[...] = jnp.zeros_like(dccar)
                gcar[...] = jnp.zeros_like(gcar)

            row = lax.broadcasted_iota(jnp.int32, (TS, HD), 0)
            n = TS + SUB
            cw = p["cw"]
            for h in range(NH):
                hs = _hs(h)
                xb = proj_ref[:, hs]
                lc = _lru_core_head(xb, cc_save[0, :, hs], i * TS, p, h)
                a, mult, gx, ga, xc = lc["a"], lc["mult"], lc["gx"], lc["ga"], lc["xc"]
                an = jnp.where(row == TS - 1, 1.0, pltpu.roll(a, TS - 1, 0))
                Pb, Bb = _scan_bwd(an, dtok[:, hs])
                lam = Pb * gcar[0:1, hs] + Bb
                gcar[:, hs] = (a * lam)[0:SUB, :]
                hprev = jnp.where(row == 0, jnp.broadcast_to(hin[:, hs], (TS, HD)), pltpu.roll(tok[:, hs], 1, 0))
                dmult = lam * gx * xc
                dgx = lam * mult * xc
                dla = lam * hprev * a - jnp.where(lc["first"], 0.0, dmult * a * a / mult)
                dga = -LRU_C * lc["sp"] * dla
                dsp = _rowsum(-LRU_C * ga * dla)
                sg["dvec"][0:1, hs] += dsp * (-_sig(-p["ap"][:, hs]))
                dpx = dgx * gx * (1.0 - gx)
                dpa = dga * ga * (1.0 - ga)
                sg["dvec"][1:2, hs] += _rowsum(dpx)
                sg["dvec"][2:3, hs] += _rowsum(dpa)
                dxc = lam * mult * gx + _mm(dpx, p["wgxT"][h]) + _mm(dpa, p["wgaT"][h])
                sg["dwgx"][h] += _mm_tn(xc, dpx)
                sg["dwga"][h] += _mm_tn(xc, dpa)
                sg["dvec"][3:4, hs] += _rowsum(dxc)
                sg["dcw"][3:4, hs] += _rowsum(dxc * xb)
                sg["dcw"][2:3, hs] += _rowsum(dxc * lc["x1"])
                sg["dcw"][1:2, hs] += _rowsum(dxc * lc["x2"])
                sg["dcw"][0:1, hs] += _rowsum(dxc * lc["x3"])
                ext = jnp.concatenate([dxc, dccar[:, hs]], axis=0)
                dproj_ref[:, hs] = (cw[3:4, hs] * dxc + cw[2:3, hs] * pltpu.roll(ext, n - 1, 0)[:TS]
                                    + cw[1:2, hs] * pltpu.roll(ext, n - 2, 0)[:TS]
                                    + cw[0:1, hs] * pltpu.roll(ext, n - 3, 0)[:TS]).astype(bf16)
                dccar[:, hs] = dxc[0:SUB, :]

        dxin_ref[...] = ALPHA * dz + _mm(dproj_ref[...], winT_ref[...])

    rtile = lambda w: pl.BlockSpec((TS, w), lambda s: (nt - 1 - s, 0))
    consts = [w_inT, w_outT, lng, kh, khT, vh, vhT] + pvals
    in_arrays = [dxout, z, proj] + consts + list(saves)
    in_specs = [rtile(D_MODEL), rtile(D_MODEL), rtile(W)] + [_const_spec(a) for a in consts]
    for a in saves:
        in_specs.append(pl.BlockSpec((1,) + a.shape[1:], lambda s, _n=a.ndim - 1: (nt - 1 - s,) + (0,) * _n))
    out_shape = [jax.ShapeDtypeStruct((S, D_MODEL), f32), jax.ShapeDtypeStruct((S, W), bf16),
                 jax.ShapeDtypeStruct((S, D_MODEL), bf16), jax.ShapeDtypeStruct((S, D_MODEL), bf16),
                 jax.ShapeDtypeStruct((SUB, D_MODEL), f32), jax.ShapeDtypeStruct((XHEADS, XW, XW), f32),
                 jax.ShapeDtypeStruct((XHEADS, XW, XW), f32)]
    out_specs = [rtile(D_MODEL), rtile(W), rtile(D_MODEL), rtile(D_MODEL), _acc_spec((SUB, D_MODEL)),
                 _acc_spec((XHEADS, XW, XW)), _acc_spec((XHEADS, XW, XW))]
    for n in sg_names:
        out_shape.append(jax.ShapeDtypeStruct(sg_shapes[n], f32))
        out_specs.append(_acc_spec(sg_shapes[n]))
    if kind == 1:
        scratch = ([pltpu.VMEM((NH, HD, HD), f32)] * 2 + [pltpu.VMEM((HGRN_SUB // CHUNK, NH, HD, HD), f32)] * 3
                   + [pltpu.VMEM((HGRN_SUB, TOK), f32)] * 9)
    elif kind == 2:
        scratch = [pltpu.VMEM((2 * SUB, TOK), f32)]
    elif kind == 3:
        scratch = [pltpu.VMEM((SUB, TOK), f32)] * 2
    else:
        scratch = []
    outs = pl.pallas_call(body, name=f"bwd_layer{kind}", grid=(nt,), in_specs=in_specs, out_specs=out_specs,
                          out_shape=out_shape, scratch_shapes=scratch, compiler_params=_params())(*in_arrays)
    return outs[:7], dict(zip(sg_names, outs[7:]))


def _prep(mem, w_kv, logits):
    def body(mem_ref, w_ref, lg_ref, kh_ref, khT_ref, vh_ref, vhT_ref, p_ref):
        kv = _mm(mem_ref[...], w_ref[...])
        k, v = kv[:, 0:XW], kv[:, XW:]
        kT, vT = k.T, v.T
        col = lax.broadcasted_iota(jnp.int32, (XW, XW), 1) // XDIM
        row = lax.broadcasted_iota(jnp.int32, (XW, XW), 0) // XDIM
        for h in range(XHEADS):
            kh_ref[h] = jnp.where(col == h, k, 0.0).astype(bf16)
            vh_ref[h] = jnp.where(col == h, v, 0.0).astype(bf16)
            khT_ref[h] = jnp.where(row == h, kT, 0.0).astype(bf16)
            vhT_ref[h] = jnp.where(row == h, vT, 0.0).astype(bf16)
        lg = lg_ref[...]
        e = jnp.exp(lg - jnp.max(lg, axis=0, keepdims=True))
        p_ref[...] = e / jnp.sum(e, axis=0, keepdims=True)

    vm = pl.BlockSpec(memory_space=pltpu.VMEM)
    hs = jax.ShapeDtypeStruct((XHEADS, XW, XW), bf16)
    return pl.pallas_call(body, name="prep_memory", in_specs=[vm] * 3, out_specs=[vm] * 5,
                          out_shape=[hs, hs, hs, hs, jax.ShapeDtypeStruct(logits.shape, f32)])(mem, w_kv, logits)


def _kv_bwd(mem, dks, dvs):
    def body(mem_ref, *refs):
        out_ref = refs[-1]
        col = lax.broadcasted_iota(jnp.int32, (XW, XW), 1) // XDIM
        dk = jnp.zeros((XW, XW), f32)
        dv = jnp.zeros((XW, XW), f32)
        for l in range(DEPTH):
            for h in range(XHEADS):
                dk = dk + jnp.where(col == h, refs[l][h], 0.0)
                dv = dv + jnp.where(col == h, refs[DEPTH + l][h], 0.0)
        out_ref[:, 0:XW] = _mm_tn(mem_ref[...], dk)
        out_ref[:, XW:] = _mm_tn(mem_ref[...], dv)

    vm = pl.BlockSpec(memory_space=pltpu.VMEM)
    return pl.pallas_call(body, name="kv_bwd", in_specs=[vm] * (1 + 2 * DEPTH), out_specs=vm,
                          out_shape=jax.ShapeDtypeStruct((D_MODEL, 2 * XW), f32))(mem, *dks, *dvs)


def _tn_gemm(a, b, name, nb):
    S, M = a.shape
    N = b.shape[1]
    NB = N // nb
    nk = S // TK

    def body(a_ref, b_ref, o_ref):
        @pl.when(pl.program_id(1) == 0)
        def _():
            o_ref[...] = jnp.zeros_like(o_ref)

        o_ref[...] += _mm_tn(a_ref[...], b_ref[...])

    return pl.pallas_call(body, name=name, grid=(nb, nk),
                          in_specs=[pl.BlockSpec((TK, M), lambda j, k: (k, 0)), pl.BlockSpec((TK, NB), lambda j, k: (k, j))],
                          out_specs=pl.BlockSpec((M, NB), lambda j, k: (0, j)),
                          out_shape=jax.ShapeDtypeStruct((M, N), f32),
                          compiler_params=pltpu.CompilerParams(dimension_semantics=("parallel", "arbitrary"),
                                                               vmem_limit_bytes=VMEM_LIMIT))(a, b)


def _rows_block(R, mult=16, cap=1024):
    best = R
    for d in range(mult, min(R, cap) + 1, mult):
        if R % d == 0:
            best = d
    return best


def _tn_gemm_sharded(a, b, name):
    S, M = a.shape
    Wq = b.shape[1] // 4
    nk = S // TK

    def body(a_ref, b_ref, o_ref):
        @pl.when(pl.program_id(0) == 0)
        def _():
            o_ref[...] = jnp.zeros_like(o_ref)

        at = a_ref[...].astype(MM)
        for j in range(4):
            o_ref[j] += _mm_tn(at, b_ref[:, j * Wq:(j + 1) * Wq])

    return pl.pallas_call(body, name=name, grid=(nk,),
                          in_specs=[pl.BlockSpec((TK, M), lambda k: (k, 0)), pl.BlockSpec((TK, 4 * Wq), lambda k: (k, 0))],
                          out_specs=pl.BlockSpec((4, M, Wq), lambda k: (0, 0, 0)),
                          out_shape=jax.ShapeDtypeStruct((4, M, Wq), f32), compiler_params=_params())(a, b)


HALF_ROWS = D_MODEL // 2
SHARD_ROWS = D_MODEL // 4


def _half_of_full(ref, kind, h):
    if kind == "rows":
        cols = ref.shape[1] // 2
        return ref.at[:, pl.ds(h * cols, cols)]
    return ref.at[:, pl.ds(h * HALF_ROWS, HALF_ROWS)]


def _shard_of_half(ref, kind, j):
    if kind == "rows":
        return ref.at[pl.ds(j * SHARD_ROWS, SHARD_ROWS)]
    return ref.at[j]


def _half_of_shard(ref, kind, h):
    if kind == "rows":
        cols = ref.shape[1] // 2
        return ref.at[:, pl.ds(h * cols, cols)]
    rows = ref.shape[0] // 2
    return ref.at[pl.ds(h * rows, rows)]


def _half_shape(full_shape, kind):
    if kind == "rows":
        return (full_shape[0], full_shape[1] // 2)
    return (4, HALF_ROWS, full_shape[2])


def _shard_half_shape(full_shape, kind):
    if kind == "rows":
        return (SHARD_ROWS, full_shape[1] // 2)
    return (HALF_ROWS, full_shape[2])


def _shard_shape(full_shape, kind):
    if kind == "rows":
        return (SHARD_ROWS, full_shape[1])
    return (D_MODEL, full_shape[2])


def _ew_call(body, name, grid, jc, ins, in_specs, out_shape, out_specs):
    gs = pltpu.PrefetchScalarGridSpec(num_scalar_prefetch=1, grid=grid, in_specs=in_specs, out_specs=out_specs)
    return pl.pallas_call(body, name=name, grid_spec=gs, out_shape=out_shape,
                          compiler_params=pltpu.CompilerParams(dimension_semantics=("parallel",) * len(grid),
                                                               vmem_limit_bytes=VMEM_LIMIT))(jc, *ins)


def _add_sibling(part, got, kind, jc, name):
    def body(jc_ref, a_ref, b_ref, o_ref, ob_ref):
        s = a_ref[...] + b_ref[...]
        o_ref[...] = s
        ob_ref[...] = s.astype(bf16)

    if kind == "rows":
        R, C = part.shape[0], part.shape[1] // 2
        grid = (2,)
        mine = pl.BlockSpec((R // 2, C), lambda i, jc_ref: (i, jc_ref[1]))
        spec = pl.BlockSpec((R // 2, C), lambda i, jc_ref: (i, 0))
    else:
        C = part.shape[2]
        grid = (4, 2)
        mine = pl.BlockSpec((None, HALF_ROWS // 2, C), lambda s, i, jc_ref: (s, 2 * jc_ref[1] + i, 0))
        spec = pl.BlockSpec((None, HALF_ROWS // 2, C), lambda s, i, jc_ref: (s, i, 0))
    hs = _half_shape(part.shape, kind)
    return _ew_call(body, name, grid, jc, [part, got], [mine, spec],
                    [jax.ShapeDtypeStruct(hs, f32), jax.ShapeDtypeStruct(hs, bf16)], [spec, spec])


def _add_chips(q32, r, kind, jc, name):
    def body(jc_ref, q_ref, r_ref, out_ref):
        out_ref[...] = ((q_ref[...] + r_ref[0].astype(f32)) + r_ref[1].astype(f32)) + r_ref[2].astype(f32)

    if kind == "rows":
        C = q32.shape[1]
        grid = (1,)
        qs = pl.BlockSpec((SHARD_ROWS, C), lambda i, jc_ref: (jc_ref[0], 0))
        rs = pl.BlockSpec((3, SHARD_ROWS, C), lambda i, jc_ref: (0, 0, 0))
        os_ = pl.BlockSpec((SHARD_ROWS, C), lambda i, jc_ref: (0, jc_ref[1]))
        full_shape = (D_MODEL, 2 * C)
    else:
        C = q32.shape[2]
        grid = (2,)
        qs = pl.BlockSpec((None, HALF_ROWS // 2, C), lambda i, jc_ref: (jc_ref[0], i, 0))
        rs = pl.BlockSpec((3, HALF_ROWS // 2, C), lambda i, jc_ref: (0, i, 0))
        os_ = pl.BlockSpec((HALF_ROWS // 2, C), lambda i, jc_ref: (2 * jc_ref[1] + i, 0))
        full_shape = (4, D_MODEL, C)
    return _ew_call(body, name, grid, jc, [q32, r], [qs, rs], jax.ShapeDtypeStruct(_shard_shape(full_shape, kind), f32), os_)


def _adamw(w, g, m, v, name):
    R, C = w.shape
    br = _rows_block(R, mult=SUB, cap=512)
    c1 =1.0 / (1.0 - ADAM_B1 ** ADAM_STEP)
    c2 = 1.0 / (1.0 - ADAM_B2 ** ADAM_STEP)

    def body(w_ref, g_ref, m_ref, v_ref, d_ref, nm_ref, nv_ref):
        g_ = g_ref[...]
        nm = ADAM_B1 * m_ref[...] + (1.0 - ADAM_B1) * g_
        nv = ADAM_B2 * v_ref[...] + (1.0 - ADAM_B2) * (g_ * g_)
        nm_ref[...] = nm
        nv_ref[...] = nv
        d_ref[...] = -ADAM_LR * ((nm * c1) / (jnp.sqrt(nv * c2) + ADAM_EPS) + ADAM_WD * w_ref[...])

    spec = pl.BlockSpec((br, C), lambda i: (i, 0))
    sh = jax.ShapeDtypeStruct((R, C), f32)
    return pl.pallas_call(body, name=name, grid=(R // br,), in_specs=[spec] * 4, out_specs=[spec] * 3,
                          out_shape=[sh, sh, sh], compiler_params=_params("parallel"))(w, g, m, v)


def _small_finish(dbacc, p_soft, dlb):
    def body(db_ref, p_ref, dlb_ref, dbs_ref, dlg_ref):
        lane = lax.broadcasted_iota(jnp.int32, (HD, HD), 1)
        acc = jnp.zeros((HD, HD), f32)
        for h in range(NH):
            acc = acc + jnp.where(lane == h, jnp.sum(db_ref[h], axis=-1, keepdims=True), 0.0)
        dbs_ref[...] = acc
        p = p_ref[...]
        p1 = p[1:2, :]
        rowi = lax.broadcasted_iota(jnp.int32, p.shape, 0)
        dlg_ref[...] = dlb_ref[0:1, :] * p1 * (jnp.where(rowi == 1, 1.0, 0.0) - p)

    vm = pl.BlockSpec(memory_space=pltpu.VMEM)
    return pl.pallas_call(body, name="small_finish", in_specs=[vm] * 3, out_specs=[vm] * 2,
                          out_shape=[jax.ShapeDtypeStruct((HD, HD), f32), jax.ShapeDtypeStruct(p_soft.shape, f32)])(dbacc, p_soft, dlb)


def _where_am_i():
    return lax.axis_index("x"), lax.axis_index("y"), lax.axis_index("c")


MAX_PIECES = 8


def _nchunks(rows, mult):
    for n in range(MAX_PIECES, 0, -1):
        if rows % (n * mult) == 0:
            return n
    return 1


def _leading_pieces(src, dst):
    n = src.shape[0]
    if len(src.shape) >= 3 and n <= MAX_PIECES:
        return [(src.at[s], dst.at[s]) for s in range(n)]
    return [(src, dst)]


def _ag_weights(shards, kinds, jshard):
    n = len(shards)

    def body(*refs):
        sh_refs, out_refs, token = refs[:n], refs[2 * n:3 * n], refs[3 * n]
        send_sems, recv_sems = refs[3 * n + 1:]
        x, y, c = _where_am_i()
        j = 2 * x + y
        sib = (x, y, 1 - c)
        chips = [(1 - x, y), (x, 1 - y), (1 - x, 1 - y)]
        token[...] = jnp.zeros_like(token)

        def cp(k, src, dst, to):
            return pltpu.make_async_remote_copy(src_ref=src, dst_ref=dst, send_sem=send_sems.at[k], recv_sem=recv_sems.at[k],
                                                device_id=to, device_id_type=MESH)

        started = []
        for a in range(n):
            for k, (cx, cy) in enumerate(chips):
                d = cp(6 * a + k, _half_of_shard(sh_refs[a], kinds[a], c), _half_of_shard(out_refs[a].at[j], kinds[a], c), (cx, cy, c))
                d.start()
                started.append(d)
        for a in range(n):
            for k, (cx, cy) in enumerate(chips):
                blk = _half_of_shard(out_refs[a].at[2 * cx + cy], kinds[a], c)
                cp(6 * a + k, blk, blk, (cx, cy, c)).wait_recv()
                d = cp(6 * a + 3 + k, blk, blk, sib)
                d.start()
                started.append(d)
        for a in range(n):
            for k, (cx, cy) in enumerate(chips):
                blk = _half_of_shard(out_refs[a].at[2 * cx + cy], kinds[a], 1 - c)
                cp(6 * a + 3 + k, blk, blk, sib).wait_recv()
        for d in started:
            d.wait_send()

    placed = [lax.dynamic_update_slice(jnp.zeros((4,) + s.shape, s.dtype), s[None], (jshard,) + (0,) * s.ndim) for s in shards]
    anyspec = pl.BlockSpec(memory_space=pl.ANY)
    outs = pl.pallas_call(body, name="all_gather_weights", in_specs=[anyspec] * (2 * n),
                          out_specs=[anyspec] * n + [pl.BlockSpec(memory_space=pltpu.VMEM)],
                          out_shape=[jax.ShapeDtypeStruct(p.shape, p.dtype) for p in placed] + [jax.ShapeDtypeStruct((SUB, LANE), f32)],
                          input_output_aliases={n + a: a for a in range(n)},
                          scratch_shapes=[pltpu.SemaphoreType.DMA((6 * n,)), pltpu.SemaphoreType.DMA((6 * n,))],
                          compiler_params=pltpu.CompilerParams(has_side_effects=True))(*shards, *placed)
    return outs[:n], outs[n]


_HBM = pl.BlockSpec(memory_space=pltpu.HBM)
_SEM = pl.BlockSpec(memory_space=pltpu.SEMAPHORE)
_FLOWING = pltpu.SideEffectType.DATAFLOW_SIDE_EFFECTING


def _peers6(x, y, c):
    chips = [(1 - x, y), (x, 1 - y), (1 - x, 1 - y)]
    return [(2 * k + e, chip, c if e == 0 else 1 - c) for k, chip in enumerate(chips) for e in range(2)]


def _ag_start(shards, jshard, name, after=None):
    n = len(shards)

    def body(*refs):
        out_refs = refs[2 * n:4 * n]
        send_sems, recv_sems, token = refs[4 * n:]
        x, y, c = _where_am_i()
        j = 2 * x + y
        for a in range(n):
            for slot, (cx, cy), tc in _peers6(x, y, c):
                pltpu.make_async_remote_copy(src_ref=_half_of_shard(out_refs[a], "win", c),
                                             dst_ref=_half_of_shard(out_refs[n + a].at[j], "win", c),
                                             send_sem=send_sems.at[6 * a + slot], recv_sem=recv_sems.at[6 * a + slot],
                                             device_id=(cx, cy, tc), device_id_type=MESH).start()
        token[...] = jnp.zeros_like(token)

    fill = jnp.zeros((), f32) if after is None else after[0, 0]
    placed = [lax.dynamic_update_slice(jnp.broadcast_to(fill.astype(s.dtype), (4,) + s.shape), s[None], (jshard,) + (0,) * s.ndim)
              for s in shards]
    hbm = lambda t: pltpu.with_memory_space_constraint(t, pltpu.HBM)
    both = list(shards) + placed
    outs = pl.pallas_call(
        body, name=name, in_specs=[_HBM] * (2 * n), out_specs=[_HBM] * (2 * n) + [_SEM, _SEM, pl.BlockSpec(memory_space=pltpu.VMEM)],
        out_shape=[pltpu.HBM(p.shape, p.dtype) for p in both] + [pltpu.SemaphoreType.DMA((6 * n,)), pltpu.SemaphoreType.DMA((6 * n,)),
                                                                jax.ShapeDtypeStruct((SUB, LANE), f32)],
        input_output_aliases={a: a for a in range(2 * n)},
        compiler_params=pltpu.CompilerParams(has_side_effects=_FLOWING))(*[hbm(t) for t in both])
    return outs[:2 * n], outs[2 * n], outs[2 * n + 1], outs[2 * n + 2]


def _ag_wait(bufs, send_sems, recv_sems, after, name):
    n = len(bufs) // 2

    def body(*refs):
        sh_refs, g_refs = refs[:n], refs[n:2 * n]
        send_sems, recv_sems = refs[2 * n], refs[2 * n + 1]
        x, y, c = _where_am_i()
        for a in range(n):
            for slot, (cx, cy), tc in _peers6(x, y, c):
                cp = pltpu.make_async_remote_copy(src_ref=_half_of_shard(sh_refs[a], "win", c),
                                                  dst_ref=_half_of_shard(g_refs[a].at[2 * cx + cy], "win", tc),
                                                  send_sem=send_sems.at[6 * a + slot], recv_sem=recv_sems.at[6 * a + slot],
                                                  device_id=(cx, cy, tc), device_id_type=MESH)
                cp.wait_send()
                cp.wait_recv()

    outs = pl.pallas_call(body, name=name, in_specs=[_HBM] * (2 * n) + [_SEM, _SEM, pl.BlockSpec(memory_space=pl.ANY)],
                          out_specs=[_HBM] * (2 * n), out_shape=[pltpu.HBM(b.shape, b.dtype) for b in bufs],
                          input_output_aliases={a: a for a in range(2 * n)},
                          compiler_params=pltpu.CompilerParams(has_side_effects=_FLOWING))(*bufs, send_sems, recv_sems, after)
    return outs[n:]


def _rs_swap(parts, kinds, name):
    n = len(parts)

    def body(*refs):
        p_refs, got_refs = refs[:n], refs[n:2 * n]
        send_sems, recv_sems = refs[2 * n:]
        x, y, c = _where_am_i()

        def cp(a, src, dst):
            return pltpu.make_async_remote_copy(src_ref=src, dst_ref=dst, send_sem=send_sems.at[a], recv_sem=recv_sems.at[a],
                                                device_id=(x, y, 1 - c), device_id_type=MESH)

        for a in range(n):
            for src, dst in _leading_pieces(_half_of_full(p_refs[a], kinds[a], 1 - c), got_refs[a]):
                cp(a, src, dst).start()
        for a in range(n):
            cp(a, got_refs[a], got_refs[a]).wait()

    anyspec = pl.BlockSpec(memory_space=pl.ANY)
    return pl.pallas_call(body, name=name, in_specs=[anyspec] * n, out_specs=[anyspec] * n,
                          out_shape=[jax.ShapeDtypeStruct(_half_shape(p.shape, k), p.dtype) for p, k in zip(parts, kinds)],
                          scratch_shapes=[pltpu.SemaphoreType.DMA((n,)), pltpu.SemaphoreType.DMA((n,))],
                          compiler_params=pltpu.CompilerParams(has_side_effects=True))(*parts)


def _rs_owners(qbs, kinds, full_shapes):
    n = len(qbs)

    def body(*refs):
        q_refs, got_refs = refs[:n], refs[n:2 * n]
        send_sems, recv_sems = refs[2 * n:]
        x, y, c = _where_am_i()
        chips = [(1 - x, y), (x, 1 - y), (1 - x, 1 - y)]
        ds = []
        for a in range(n):
            for k, (cx, cy) in enumerate(chips):
                d = pltpu.make_async_remote_copy(src_ref=_shard_of_half(q_refs[a], kinds[a], 2 * cx + cy), dst_ref=got_refs[a].at[k],
                                                 send_sem=send_sems.at[3 * a + k], recv_sem=recv_sems.at[3 * a + k],
                                                 device_id=(cx, cy, c), device_id_type=MESH)
                d.start()
                ds.append(d)
        for d in ds:
            d.wait()

    anyspec = pl.BlockSpec(memory_space=pl.ANY)
    return pl.pallas_call(body, name="rs_to_owners", in_specs=[anyspec] * n, out_specs=[anyspec] * n,
                          out_shape=[jax.ShapeDtypeStruct((3,) + _shard_half_shape(fs, k), bf16) for fs, k in zip(full_shapes, kinds)],
                          scratch_shapes=[pltpu.SemaphoreType.DMA((3 * n,)), pltpu.SemaphoreType.DMA((3 * n,))],
                          compiler_params=pltpu.CompilerParams(has_side_effects=True))(*qbs)


def _rs_owners_start(qbs, kinds, full_shapes, name):
    n = len(qbs)

    def body(*refs):
        q_refs, got_refs = refs[2 * n:3 * n], refs[3 * n:4 * n]
        send_sems, recv_sems, token = refs[4 * n:]
        x, y, c = _where_am_i()
        for a in range(n):
            for k, (cx, cy) in enumerate([(1 - x, y), (x, 1 - y), (1 - x, 1 - y)]):
                pltpu.make_async_remote_copy(src_ref=_shard_of_half(q_refs[a], kinds[a], 2 * cx + cy), dst_ref=got_refs[a].at[k],
                                             send_sem=send_sems.at[3 * a + k], recv_sem=recv_sems.at[3 * a + k],
                                             device_id=(cx, cy, c), device_id_type=MESH).start()
        token[...] = jnp.zeros_like(token)

    hbm = lambda t: pltpu.with_memory_space_constraint(t, pltpu.HBM)
    lands = [lax.empty((3,) + _shard_half_shape(fs, k), bf16) for fs, k in zip(full_shapes, kinds)]
    both = list(qbs) + lands
    outs = pl.pallas_call(
        body, name=name, in_specs=[_HBM] * (2 * n), out_specs=[_HBM] * (2 * n) + [_SEM, _SEM, pl.BlockSpec(memory_space=pltpu.VMEM)],
        out_shape=[pltpu.HBM(t.shape, t.dtype) for t in both] + [pltpu.SemaphoreType.DMA((3 * n,)), pltpu.SemaphoreType.DMA((3 * n,)),
                                                                jax.ShapeDtypeStruct((SUB, LANE), f32)],
        input_output_aliases={a: a for a in range(2 * n)},
        compiler_params=pltpu.CompilerParams(has_side_effects=_FLOWING))(*[hbm(t) for t in both])
    return outs[:2 * n], outs[2 * n], outs[2 * n + 1], outs[2 * n + 2]


def _rs_owners_wait(bufs, send_sems, recv_sems, kinds, after, name):
    n = len(bufs) // 2

    def body(*refs):
        q_refs, got_refs = refs[:n], refs[n:2 * n]
        send_sems, recv_sems = refs[2 * n], refs[2 * n + 1]
        x, y, c = _where_am_i()
        for a in range(n):
            for k, (cx, cy) in enumerate([(1 - x, y), (x, 1 - y), (1 - x, 1 - y)]):
                cp = pltpu.make_async_remote_copy(src_ref=_shard_of_half(q_refs[a], kinds[a], 2 * cx + cy), dst_ref=got_refs[a].at[k],
                                                  send_sem=send_sems.at[3 * a + k], recv_sem=recv_sems.at[3 * a + k],
                                                  device_id=(cx, cy, c), device_id_type=MESH)
                cp.wait_send()
                cp.wait_recv()

    outs = pl.pallas_call(body, name=name, in_specs=[_HBM] * (2 * n) + [_SEM, _SEM, pl.BlockSpec(memory_space=pl.ANY)],
                          out_specs=[_HBM] * (2 * n), out_shape=[pltpu.HBM(b.shape, b.dtype) for b in bufs],
                          input_output_aliases={a: a for a in range(2 * n)},
                          compiler_params=pltpu.CompilerParams(has_side_effects=_FLOWING))(*bufs, send_sems, recv_sems, after)
    return outs[n:]


def _rs_join(bufs, kinds):
    n = len(bufs)

    def body(*refs):
        out_refs = refs[n:2 * n]
        send_sems, recv_sems = refs[2 * n:]
        x, y, c = _where_am_i()

        def cp(a, h):
            blk = _half_of_shard(out_refs[a], kinds[a], h)
            return pltpu.make_async_remote_copy(src_ref=blk, dst_ref=blk, send_sem=send_sems.at[a], recv_sem=recv_sems.at[a],
                                                device_id=(x, y, 1 - c), device_id_type=MESH)

        for a in range(n):
            cp(a, c).start()
        for a in range(n):
            cp(a, c).wait_send()
            cp(a, 1 - c).wait_recv()

    anyspec = pl.BlockSpec(memory_space=pl.ANY)
    return pl.pallas_call(body, name="rs_join_halves", in_specs=[anyspec] * n, out_specs=[anyspec] * n,
                          out_shape=[jax.ShapeDtypeStruct(b.shape, b.dtype) for b in bufs],
                          input_output_aliases={a: a for a in range(n)},
                          scratch_shapes=[pltpu.SemaphoreType.DMA((n,)), pltpu.SemaphoreType.DMA((n,))],
                          compiler_params=pltpu.CompilerParams(has_side_effects=True))(*bufs)


def _all_reduce_small(g):
    R, C = g.shape
    H = R // 2
    NP = _nchunks(H, SUB)
    PR = H // NP

    def body(g_ref, out_ref, sib_ref, chip_ref, send_sems, recv_sems):
        x, y, c = _where_am_i()
        j = 2 * x + y
        sib = (x, y, 1 - c)
        chips = [(1 - x, y), (x, 1 - y), (1 - x, 1 - y)]
        rows = pl.ds(pl.multiple_of(c * H, SUB), H)

        def cp(k, src, dst, to):
            return pltpu.make_async_remote_copy(src_ref=src, dst_ref=dst, send_sem=send_sems.at[k], recv_sem=recv_sems.at[k],
                                                device_id=to, device_id_type=MESH)

        def pieces(k, src, dst, to):
            for q in range(NP):
                cp(k, src.at[pl.ds(q * PR, PR)], dst.at[pl.ds(q * PR, PR)], to).start()

        for half in range(2):
            pieces(0, g_ref.at[pl.ds(half * H, H)], sib_ref.at[pl.ds(half * H, H)], sib)
        cp(0, g_ref, sib_ref, sib).wait()
        chip_ref[j] = g_ref[rows, :] + sib_ref[rows, :]
        for k, (cx, cy) in enumerate(chips):
            pieces(1 + k, chip_ref.at[j], chip_ref.at[j], (cx, cy, c))
        for k, (cx, cy) in enumerate(chips):
            blk = chip_ref.at[2 * cx + cy]
            cp(1 + k, blk, blk, (cx, cy, c)).wait()
        out_ref[rows, :] = ((chip_ref[0] + chip_ref[1]) + chip_ref[2]) + chip_ref[3]
        other = out_ref.at[pl.ds(pl.multiple_of((1 - c) * H, SUB), H)]
        pieces(4, out_ref.at[rows], out_ref.at[rows], sib)
        cp(4, other, other, sib).wait()

    vm = pl.BlockSpec(memory_space=pltpu.VMEM)
    return pl.pallas_call(body, name="all_reduce_small", in_specs=[vm], out_specs=vm,
                          out_shape=jax.ShapeDtypeStruct((R, C), f32),
                          scratch_shapes=[pltpu.VMEM((R, C), f32), pltpu.VMEM((4, H, C), f32),
                                          pltpu.SemaphoreType.DMA((5,)), pltpu.SemaphoreType.DMA((5,))],
                          compiler_params=pltpu.CompilerParams(has_side_effects=True, vmem_limit_bytes=VMEM_LIMIT))(g)


SPLIT_MIN_ELEMS = 1 << 16


def _all_reduce_many(gs):
    n = len(gs)
    split = [g.ndim == 3 and g.shape[0] % 2 == 0 and g.size >= SPLIT_MIN_ELEMS for g in gs]
    part_shape = [((g.shape[0] // 2,) + g.shape[1:]) if s else g.shape for g, s in zip(gs, split)]
    n_split = sum(split)

    def body(*refs):
        g, out, sibs, chipb = refs[:n], refs[n:2 * n], refs[2 * n:3 * n], refs[3 * n:4 * n]
        send_sems, recv_sems = refs[4 * n:]
        x, y, c = _where_am_i()
        j = 2 * x + y
        sib = (x, y, 1 - c)
        chips = [(1 - x, y), (x, 1 - y), (1 - x, 1 - y)]

        def cp(k, src, dst, to):
            return pltpu.make_async_remote_copy(src_ref=src, dst_ref=dst, send_sem=send_sems.at[k], recv_sem=recv_sems.at[k],
                                                device_id=to, device_id_type=MESH)

        def part(a, h):
            return pl.ds(h * part_shape[a][0], part_shape[a][0]) if split[a] else Ellipsis

        def mine(ref, a, h):
            return ref.at[part(a, h)] if split[a] else ref

        swaps = [cp(a, g[a], sibs[a], sib) for a in range(n)]
        for d in swaps:
            d.start()
        for a in range(n):
            swaps[a].wait()
            chipb[a][j] = g[a][part(a, c)] + sibs[a][part(a, c)]
        sends = [cp(n + 3 * a + k, chipb[a].at[j], chipb[a].at[j], (cx, cy, c)) for a in range(n) for k, (cx, cy) in enumerate(chips)]
        for d in sends:
            d.start()
        for a in range(n):
            for k, (cx, cy) in enumerate(chips):
                blk = chipb[a].at[2 * cx + cy]
                cp(n + 3 * a + k, blk, blk, (cx, cy, c)).wait_recv()
            out[a][part(a, c)] = ((chipb[a][0] + chipb[a][1]) + chipb[a][2]) + chipb[a][3]
        for d in sends:
            d.wait_send()
        backs = [(a, cp(4 * n + i, mine(out[a], a, c), mine(out[a], a, c), sib)) for i, a in enumerate([a for a in range(n) if split[a]])]
        for _, d in backs:
            d.start()
        for i, (a, d) in enumerate(backs):
            d.wait_send()
            cp(4 * n + i, mine(out[a], a, 1 - c), mine(out[a], a, 1 - c), sib).wait_recv()

    vm = pl.BlockSpec(memory_space=pltpu.VMEM)
    nsem = 4 * n + n_split
    return pl.pallas_call(body, name="all_reduce_small_grads", in_specs=[vm] * n, out_specs=[vm] * n,
                          out_shape=[jax.ShapeDtypeStruct(g.shape, f32) for g in gs],
                          scratch_shapes=([pltpu.VMEM(g.shape, f32) for g in gs] + [pltpu.VMEM((4,) + ps, f32) for ps in part_shape]
                                          + [pltpu.SemaphoreType.DMA((nsem,)), pltpu.SemaphoreType.DMA((nsem,))]),
                          compiler_params=pltpu.CompilerParams(has_side_effects=True, vmem_limit_bytes=VMEM_LIMIT))(*gs)


def _adamw_many(ws, gs, ms, vs, name):
    n = len(ws)
    c1 = 1.0 / (1.0 - ADAM_B1 ** ADAM_STEP)
    c2 = 1.0 / (1.0 - ADAM_B2 ** ADAM_STEP)

    def body(*refs):
        for a in range(n):
            w_ref, g_ref, m_ref, v_ref, d_ref, nm_ref, nv_ref = (refs[i * n + a] for i in range(7))
            g_ = g_ref[...]
            nm = ADAM_B1 * m_ref[...] + (1.0 - ADAM_B1) * g_
            nv = ADAM_B2 * v_ref[...] + (1.0 - ADAM_B2) * (g_ * g_)
            nm_ref[...] = nm
            nv_ref[...] = nv
            d_ref[...] = -ADAM_LR * ((nm * c1) / (jnp.sqrt(nv * c2) + ADAM_EPS) + ADAM_WD * w_ref[...])

    vm = pl.BlockSpec(memory_space=pltpu.VMEM)
    sh = [jax.ShapeDtypeStruct(w.shape, f32) for w in ws]
    outs = pl.pallas_call(body, name=name, in_specs=[vm] * (4 * n), out_specs=[vm] * (3 * n), out_shape=sh * 3,
                          compiler_params=pltpu.CompilerParams(vmem_limit_bytes=VMEM_LIMIT))(*ws, *gs, *ms, *vs)
    return outs[:n], outs[n:2 * n], outs[2 * n:]


def _pack_flat(arrs, rows_mult):
    flat = jnp.concatenate([a.reshape(-1) for a in arrs])
    n = flat.shape[0]
    tot = -(-n // (rows_mult * LANE)) * rows_mult * LANE
    return jnp.pad(flat, (0, tot - n)).reshape(-1, LANE)


def _unpack_flat(buf, shapes):
    flat = buf.reshape(-1)
    out, o = [], 0
    for s in shapes:
        n = math.prod(s)
        out.append(flat[o:o + n].reshape(s))
        o += n
    return out


_BIG = ("mem_kv_w", "w_out", "a_w_in", "b_w_in", "c_w_in", "d_w_in")
SMALL_ROWS_MULT = 256


def _row8(v):
    v = v.reshape(-1, v.shape[-1])
    return jnp.pad(v, ((0, SUB - v.shape[0]), (0, 0)))


def kernel(x, mem, mem_kv_w, ln_g, ln_b, w_out, hgrn_lb_logits, a_w_in, a_w_s, a_b_s, b_w_in, b_norm_g, c_w_in, c_w_pool, c_scale, d_w_in, d_conv_w, d_conv_b, d_w_gx, d_b_gx, d_w_ga, d_b_ga, d_a_param, loss_target, m_mem_kv_w, m_ln_g, m_ln_b, m_w_out, m_hgrn_lb_logits, m_a_w_in, m_a_w_s, m_a_b_s, m_b_w_in, m_b_norm_g, m_c_w_in, m_c_w_pool, m_c_scale, m_d_w_in, m_d_conv_w, m_d_conv_b, m_d_w_gx, m_d_b_gx, m_d_w_ga, m_d_b_ga, m_d_a_param, v_mem_kv_w, v_ln_g, v_ln_b, v_w_out, v_hgrn_lb_logits, v_a_w_in, v_a_w_s, v_a_b_s, v_b_w_in, v_b_norm_g, v_c_w_in, v_c_w_pool, v_c_scale, v_d_w_in, v_d_conv_w, v_d_conv_b, v_d_w_gx, v_d_b_gx, v_d_w_ga, v_d_b_ga, v_d_a_param):
    names = ["mem_kv_w", "ln_g", "ln_b", "w_out", "hgrn_lb_logits", "a_w_in", "a_w_s", "a_b_s", "b_w_in", "b_norm_g", "c_w_in",
             "c_w_pool", "c_scale", "d_w_in", "d_conv_w", "d_conv_b", "d_w_gx", "d_b_gx", "d_w_ga", "d_b_ga", "d_a_param"]
    w = dict(mem_kv_w=mem_kv_w, ln_g=ln_g, ln_b=ln_b, w_out=w_out, hgrn_lb_logits=hgrn_lb_logits, a_w_in=a_w_in, a_w_s=a_w_s,
             a_b_s=a_b_s, b_w_in=b_w_in, b_norm_g=b_norm_g, c_w_in=c_w_in, c_w_pool=c_w_pool, c_scale=c_scale, d_w_in=d_w_in,
             d_conv_w=d_conv_w, d_conv_b=d_conv_b, d_w_gx=d_w_gx, d_b_gx=d_b_gx, d_w_ga=d_w_ga, d_b_ga=d_b_ga, d_a_param=d_a_param)
    m = dict(zip(names, [m_mem_kv_w, m_ln_g, m_ln_b, m_w_out, m_hgrn_lb_logits, m_a_w_in, m_a_w_s, m_a_b_s, m_b_w_in, m_b_norm_g,
                         m_c_w_in, m_c_w_pool, m_c_scale, m_d_w_in, m_d_conv_w, m_d_conv_b, m_d_w_gx, m_d_b_gx, m_d_w_ga,
                         m_d_b_ga, m_d_a_param]))
    v = dict(zip(names, [v_mem_kv_w, v_ln_g, v_ln_b, v_w_out, v_hgrn_lb_logits, v_a_w_in, v_a_w_s, v_a_b_s, v_b_w_in, v_b_norm_g,
                         v_c_w_in, v_c_w_pool, v_c_scale, v_d_w_in, v_d_conv_w, v_d_conv_b, v_d_w_gx, v_d_b_gx, v_d_w_ga,
                         v_d_b_ga, v_d_a_param]))
    xi, yi = lax.axis_index("x"), lax.axis_index("y")
    jshard = 2 * xi + yi
    x2 = x[0]
    mem2 = mem[0]
    tgt2 = loss_target[0]

    w_in_sh = [w[n][0].astype(bf16) for n in _BIG[2:]]
    w_out_sh = w_out.astype(bf16)
    gath0, tie = _ag_weights([mem_kv_w.astype(bf16), w_out_sh[0], w_in_sh[0]], ("rows", "win", "win"), jshard)
    w_kv = gath0[0].reshape(D_MODEL, 2 * XW)

    def layer_weights(g_in, g_out):
        return (g_in.transpose(1, 0, 2).reshape(D_MODEL, -1), g_in.transpose(0, 2, 1).reshape(-1, D_MODEL),
                g_out.reshape(D_MODEL, D_MODEL), g_out.transpose(2, 0, 1).reshape(D_MODEL, D_MODEL))

    lw = [layer_weights(gath0[2], gath0[1])]

    def gather_small(shard):
        z = jnp.zeros((4, POOL_GROUP), f32)
        return lax.dynamic_update_slice(z, shard.reshape(1, POOL_GROUP), (jshard, 0))

    sm_sh = jnp.concatenate([gather_small(b_norm_g), gather_small(c_scale), gather_small(d_conv_b), gather_small(d_a_param)]
                            + [gather_small(d_conv_w[:, r]) for r in range(4)], axis=0)
    ci = lax.axis_index("c")
    sm_all = _all_reduce_small(_pack_flat([jnp.where(ci == 0, sm_sh, 0.0)], SUB * 2) + tie[0:1, 0:1])
    pending = [None]
    tie = sm_all
    for l in range(1, DEPTH):
        bufs, ssem, rsem, tie = _ag_start([w_in_sh[l], w_out_sh[l]], jshard, f"gather_start{l}", tie)
        pending.append((bufs, ssem, rsem))
    tied_gain = {0: ln_g[0:1] + tie[0:1, 0:1]}
    sm = _unpack_flat(sm_all, [(8, 4 * POOL_GROUP)])[0]
    ng_full, scale_full, convb_full, ap_full = sm[0:1], sm[1:2], sm[2:3], sm[3:4]
    convw_full = sm[4:8]

    tril = jnp.tril(jnp.ones((HD, HD), bool))
    wtri = jnp.where(tril, a_w_s[0], 0.0)
    wbd = jnp.zeros((TOK, TOK), f32)
    for g in range(4):
        wbd = lax.dynamic_update_slice(wbd, c_w_pool[0, g], (g * POOL_GROUP, g * POOL_GROUP))
    kh, khT, vh, vhT, p_soft = _prep(mem2, w_kv, hgrn_lb_logits)
    prm = [
        dict(wtri=wtri.astype(bf16), wtriT=wtri.transpose(0, 2, 1).astype(bf16),
             bcolb=jnp.broadcast_to(a_b_s[0][:, :, None], (NH, HD, HD))),
        dict(lb=p_soft[1:2], ng=ng_full),
        dict(wbd=wbd.astype(bf16), wbdT=wbd.T.astype(bf16), scale=scale_full),
        dict(cw=_row8(convw_full), cb=convb_full, wgx=d_w_gx[0].astype(bf16), wgxT=d_w_gx[0].transpose(0, 2, 1).astype(bf16),
             bgx=d_b_gx.reshape(1, TOK), wga=d_w_ga[0].astype(bf16), wgaT=d_w_ga[0].transpose(0, 2, 1).astype(bf16),
             bga=d_b_ga.reshape(1, TOK), ap=ap_full),
    ]

    acts = []
    h = x2
    for l in range(DEPTH):
        if l:
            bufs, ssem, rsem = pending[l]
            lw.append(layer_weights(*_ag_wait(bufs, ssem, rsem, h, f"gather_wait{l}")))
        outs = _fwd_layer(l, h, lw[l][0], lw[l][2], tied_gain.get(l, ln_g[l:l + 1]), ln_b[l:l + 1], khT, vh, prm[l],
                          tgt2 if l == DEPTH - 1 else None)
        nfix = 4 if l == DEPTH - 1 else 3
        acts.append(dict(xin=h, proj=outs[1], z=outs[2], saves=outs[nfix:]))
        if l == DEPTH - 1:
            loss_part = outs[3]
        h = outs[0]
    loss = lax.psum(0.5 / D_MODEL * jnp.sum(loss_part), ("x", "y", "c"))

    dh = h
    dln = [None] * DEPTH
    dks, dvs = [None] * DEPTH, [None] * DEPTH
    sgr = [None] * DEPTH
    jc = jnp.stack([jshard, ci]).astype(jnp.int32)
    lkinds = ("win", "rows")
    q32s, flying = [None] * DEPTH, [None] * DEPTH
    back_gain = {DEPTH - 1: ln_g[DEPTH - 1:] + (loss - loss)}
    for l in reversed(range(DEPTH)):
        a = acts[l]
        (dxin, dproj, mixedb, dyb, dln[l], dks[l], dvs[l]), sgr[l] = _bwd_layer(
            l, dh, a["z"], a["proj"], lw[l][1], lw[l][3], back_gain.get(l, ln_g[l:l + 1]), kh, khT, vh, vhT, prm[l], a["saves"])
        if _OFFS[l]["W"] // 4 % LANE:
            gw_in = _tn_gemm(a["xin"], dproj, f"grad_w_in{l}", 1).reshape(D_MODEL, 4, -1).transpose(1, 0, 2)
        else:
            gw_in = _tn_gemm_sharded(a["xin"], dproj, f"grad_w_in{l}")
        parts = [gw_in, _tn_gemm(mixedb, dyb, f"grad_w_out{l}", 1)]
        lk = lkinds
        if l == 0:
            parts.append(_kv_bwd(mem2, dks, dvs))
            lk = lkinds + ("rows",)
        gots = _rs_swap(parts, lk, f"rs_swap_halves{l}")
        sums = [_add_sibling(p, g, k, jc, f"rs_add_sibling{l}_{i}") for i, (p, g, k) in enumerate(zip(parts, gots, lk))]
        q32s[l] = [s[0] for s in sums]
        shapes = [p.shape for p in parts]
        bufs, ssem, rsem, tok = _rs_owners_start([s[1] for s in sums], lk, shapes, f"rs_owners_start{l}")
        flying[l] = (bufs, ssem, rsem)
        if l:
            back_gain[l - 1] = ln_g[l - 1:l] + tok[0:1, 0:1]
        dh = dxin
    grad_x = dh[None]

    dbs, dlogits = _small_finish(sgr[0]["dbacc"], p_soft, sgr[1]["dlb"])
    gs = {
        "ln_g": jnp.concatenate([dln[l][0:1] for l in range(DEPTH)], axis=0) + tok[0:1, 0:1],
        "ln_b": jnp.concatenate([dln[l][1:2] for l in range(DEPTH)], axis=0),
        "hgrn_lb_logits": dlogits,
        "a_w_s": sgr[0]["dwtri"][None],
        "a_b_s": dbs[:, 0:NH].T[None],
        "b_norm_g": sgr[1]["dng"][0:1],
        "c_w_pool": jnp.stack([sgr[2]["dwbd"][g * POOL_GROUP:(g + 1) * POOL_GROUP, g * POOL_GROUP:(g + 1) * POOL_GROUP]
                               for g in range(4)])[None],
        "c_scale": sgr[2]["dscale"][0:1],
        "d_conv_w": sgr[3]["dcw"][0:4][None],
        "d_conv_b": sgr[3]["dvec"][3:4],
        "d_w_gx": sgr[3]["dwgx"][None],
        "d_b_gx": sgr[3]["dvec"][1:2].reshape(1, NH, HD),
        "d_w_ga": sgr[3]["dwga"][None],
        "d_b_ga": sgr[3]["dvec"][2:3].reshape(1, NH, HD),
        "d_a_param": sgr[3]["dvec"][0:1],
    }
    small = [n for n in names if n not in _BIG]
    drop1 = lambda t: t.reshape(t.shape[1:]) if t.ndim > 2 and t.shape[0] == 1 else t
    gsum = dict(zip(small, _all_reduce_many([drop1(gs[n]) for n in small])))

    fin, fin_kinds = {}, []
    for l in range(DEPTH):
        lk = lkinds + (("rows",) if l == 0 else ())
        got = _rs_owners_wait(*flying[l], lk, gsum["ln_g"] if l == 0 else grad_x, f"rs_owners_wait{l}")
        fin[l] = [_add_chips(q, r, k, jc, f"rs_add_chips{l}_{i}") for i, (q, r, k) in enumerate(zip(q32s[l], got, lk))]
        fin_kinds += list(lk)
    joined = _rs_join([t for l in range(DEPTH) for t in fin[l]], tuple(fin_kinds))
    by_layer, o = [], 0
    for l in range(DEPTH):
        by_layer.append(joined[o:o + len(fin[l])])
        o += len(fin[l])
    gbig = {"mem_kv_w": by_layer[0][2], "w_out": jnp.stack([by_layer[l][1] for l in range(DEPTH)])}
    for l, n in enumerate(_BIG[2:]):
        gbig[n] = by_layer[l][0]
    g_sh, d_sh, m_sh, v_sh = {}, {}, {}, {}
    for n in _BIG:
        as2d = lambda t: t.reshape(-1, t.shape[-1])
        upd = _adamw(as2d(w[n]), as2d(gbig[n]), as2d(m[n]), as2d(v[n]), f"adamw_{n}")
        g_sh[n] = gbig[n].reshape(w[n].shape)
        d_sh[n], m_sh[n], v_sh[n] = (u.reshape(w[n].shape) for u in upd)

    for n in ("b_norm_g", "c_scale", "d_conv_b", "d_a_param"):
        gsum[n] = lax.dynamic_slice(gsum[n], (0, jshard * POOL_GROUP), (1, POOL_GROUP))
    gsum["d_conv_w"] = lax.dynamic_slice(gsum["d_conv_w"], (0, jshard * POOL_GROUP), (4, POOL_GROUP))
    upd = _adamw_many(*[[drop1(d[n]) for n in small] for d in (w, gsum, m, v)], "adamw_small")
    gsum = {n: gsum[n].reshape(w[n].shape) for n in small}
    d_sm, m_sm, v_sm = ({n: u.reshape(w[n].shape) for n, u in zip(small, us)} for us in upd)

    grads = {**gsum, **g_sh}
    deltas = {**d_sm, **d_sh}
    new_m = {**m_sm, **m_sh}
    new_v = {**v_sm, **v_sh}
    return (loss, grad_x, *[grads[n] for n in names], *[deltas[n] for n in names], *[new_m[n] for n in names],
            *[new_v[n] for n in names])
```
